```python
import math
import jax, jax.numpy as jnp
from jax import lax
import numpy as np

D_MODEL = 2048
BATCH = 8
SEQ = 8192
DEPTH = 1

HEAD_DIM = 64
N_Q_HEADS = 16
N_KV_HEADS = 2
Q_PER_KV = N_Q_HEADS // N_KV_HEADS
ATTN_WIDTH = N_Q_HEADS * HEAD_DIM
KV_WIDTH = N_KV_HEADS * HEAD_DIM
WINDOW = 128
BLOCK = 128
ROPE_THETA = 10000.0
CONV_WIDTH = D_MODEL - ATTN_WIDTH
CONV_GROUPS = CONV_WIDTH // HEAD_DIM
SHORT_CONV_K = 3
MIX_WIDTH = ATTN_WIDTH + CONV_WIDTH
IN_WIDTH = ATTN_WIDTH + 2 * KV_WIDTH + 3 * CONV_WIDTH
D_FF = 5632
FFN_CONV_K = 3
LN_EPS = 1e-5
DEEPNORM_ALPHA = (2 * DEPTH) ** 0.25
DEEPNORM_BETA = (8 * DEPTH) ** -0.25
NEG_INF = -1e30

kernel_name = "hymba_swa_sink_shortconv_convffn_deepnorm"


def layer_norm(x, g, b):
    xf = x.astype(jnp.float32)
    mu = jnp.mean(xf, axis=-1, keepdims=True)
    var = jnp.mean(jnp.square(xf - mu), axis=-1, keepdims=True)
    y = (xf - mu) * lax.rsqrt(var + LN_EPS) * g.astype(jnp.float32) + b.astype(jnp.float32)
    return y.astype(x.dtype)


def causal_depthwise_conv(x, w):
    k = w.shape[0]
    return lax.conv_general_dilated(
        x, w[:, None, :].astype(x.dtype), window_strides=(1,), padding=[(k - 1, 0)],
        dimension_numbers=("NWC", "WIO", "NWC"), feature_group_count=x.shape[-1])


def rope(x, positions):
    half = HEAD_DIM // 2
    inv_freq = ROPE_THETA ** (-jnp.arange(half, dtype=jnp.float32) / half)
    ang = positions.astype(jnp.float32)[:, None] * inv_freq[None, :]
    cos = jnp.cos(ang)[None, :, None, :]
    sin = jnp.sin(ang)[None, :, None, :]
    xf = x.astype(jnp.float32)
    x1, x2 = xf[..., :half], xf[..., half:]
    out = jnp.concatenate([x1 * cos - x2 * sin, x2 * cos + x1 * sin], axis=-1)
    return out.astype(x.dtype)


def sliding_window_gqa(q, k, v, sinks):
    b, s = q.shape[0], q.shape[1]
    n = s // BLOCK
    qb = q.reshape(b, n, BLOCK, N_KV_HEADS, Q_PER_KV, HEAD_DIM)

    def band(t):
        tb = t.reshape(b, n, BLOCK, N_KV_HEADS, HEAD_DIM)
        prev = jnp.pad(tb, ((0, 0), (1, 0), (0, 0), (0, 0), (0, 0)))[:, :-1]
        return jnp.concatenate([prev, tb], axis=2)

    kb, vb = band(k), band(v)
    scores = jnp.einsum("bnqkgd,bnskd->bnkgqs", qb.astype(jnp.float32),
                        kb.astype(jnp.float32)) * (HEAD_DIM ** -0.5)
    qi = jnp.arange(BLOCK)[:, None]
    kj = jnp.arange(2 * BLOCK)[None, :]
    diff = BLOCK + qi - kj
    kpos = (jnp.arange(n)[:, None, None] - 1) * BLOCK + kj[None]
    valid = (diff >= 0) & (diff < WINDOW) & (kpos >= 0)
    scores = jnp.where(valid[None, :, None, None], scores, NEG_INF)
    sink = jnp.broadcast_to(
        sinks.astype(jnp.float32).reshape(1, 1, N_KV_HEADS, Q_PER_KV, 1, 1),
        scores.shape[:-1] + (1,))
    probs = jax.nn.softmax(jnp.concatenate([scores, sink], axis=-1), axis=-1)[..., :-1]
    out = jnp.einsum("bnkgqs,bnskd->bnqkgd", probs.astype(v.dtype), vb)
    return out.reshape(b, s, ATTN_WIDTH)


def _fwd_setup_inputs(seed: int = 0) -> dict:
    key = jax.random.key(seed)
    ks = jax.random.split(key, 12)
    f32 = jnp.float32
    x = jax.random.normal(ks[0], (BATCH, SEQ, D_MODEL), f32)
    w_in = jax.random.normal(ks[1], (DEPTH, D_MODEL, IN_WIDTH), f32) * D_MODEL ** -0.5
    attn_sinks = jax.random.normal(ks[2], (DEPTH, N_Q_HEADS), f32) * 0.5
    short_conv_w = jax.random.normal(ks[3], (DEPTH, SHORT_CONV_K, CONV_WIDTH), f32) * SHORT_CONV_K ** -0.5
    w_out = jax.random.normal(ks[4], (DEPTH, MIX_WIDTH, D_MODEL), f32) * (MIX_WIDTH ** -0.5 * DEEPNORM_BETA)
    ln1_g = 1.0 + 0.02 * jax.random.normal(ks[5], (DEPTH, D_MODEL), f32)
    ln1_b = 0.02 * jax.random.normal(ks[6], (DEPTH, D_MODEL), f32)
    ffn_w_up = jax.random.normal(ks[7], (DEPTH, D_MODEL, 2 * D_FF), f32) * D_MODEL ** -0.5
    ffn_conv_w = jax.random.normal(ks[8], (DEPTH, FFN_CONV_K, 2 * D_FF), f32) * FFN_CONV_K ** -0.5
    ffn_w_down = jax.random.normal(ks[9], (DEPTH, D_FF, D_MODEL), f32) * (D_FF ** -0.5 * DEEPNORM_BETA)
    ln2_g = 1.0 + 0.02 * jax.random.normal(ks[10], (DEPTH, D_MODEL), f32)
    ln2_b = 0.02 * jax.random.normal(ks[11], (DEPTH, D_MODEL), f32)
    return {"x": x, "w_in": w_in, "attn_sinks": attn_sinks, "short_conv_w": short_conv_w,
            "w_out": w_out, "ln1_g": ln1_g, "ln1_b": ln1_b, "ffn_w_up": ffn_w_up,
            "ffn_conv_w": ffn_conv_w, "ffn_w_down": ffn_w_down, "ln2_g": ln2_g, "ln2_b": ln2_b}


def _fwd_reference(x, w_in, attn_sinks, short_conv_w, w_out, ln1_g, ln1_b,
              ffn_w_up, ffn_conv_w, ffn_w_down, ln2_g, ln2_b):
    b, s = x.shape[0], x.shape[1]
    positions = jnp.arange(s)
    split_pts = [ATTN_WIDTH,
                 ATTN_WIDTH + KV_WIDTH,
                 ATTN_WIDTH + 2 * KV_WIDTH,
                 ATTN_WIDTH + 2 * KV_WIDTH + CONV_WIDTH,
                 ATTN_WIDTH + 2 * KV_WIDTH + 2 * CONV_WIDTH]
    for l in range(DEPTH):
        proj = jnp.einsum("bsd,de->bse", x, w_in[l])
        q, k, v, gate_b, gate_c, h = jnp.split(proj, split_pts, axis=-1)
        q = rope(q.reshape(b, s, N_Q_HEADS, HEAD_DIM), positions)
        k = rope(k.reshape(b, s, N_KV_HEADS, HEAD_DIM), positions)
        v = v.reshape(b, s, N_KV_HEADS, HEAD_DIM)
        attn = sliding_window_gqa(q, k, v, attn_sinks[l])
        conv = gate_b * causal_depthwise_conv(gate_c * h, short_conv_w[l])
        mix = jnp.concatenate([attn, conv], axis=-1)
        y = jnp.einsum("bsm,md->bsd", mix, w_out[l])
        x = layer_norm(DEEPNORM_ALPHA * x + y, ln1_g[l], ln1_b[l])
        u = jnp.einsum("bsd,df->bsf", x, ffn_w_up[l])
        u = causal_depthwise_conv(u, ffn_conv_w[l])
        a, g = jnp.split(u, 2, axis=-1)
        y = jnp.einsum("bsf,fd->bsd", jax.nn.silu(a) * g, ffn_w_down[l])
        x = layer_norm(DEEPNORM_ALPHA * x + y, ln2_g[l], ln2_b[l])
    return x


import jax as _jax
import jax.numpy as _jnp

TWIN_FORMAT = 'train_step'
FWD_PARAMS = ['x', 'w_in', 'attn_sinks', 'short_conv_w', 'w_out', 'ln1_g', 'ln1_b', 'ffn_w_up', 'ffn_conv_w', 'ffn_w_down', 'ln2_g', 'ln2_b']
TWIN_WEIGHTS = ['w_in', 'attn_sinks', 'short_conv_w', 'w_out', 'ln1_g', 'ln1_b', 'ffn_w_up', 'ffn_conv_w', 'ffn_w_down', 'ln2_g', 'ln2_b']
TWIN_DIFF_INPUT = 'x'
TWIN_INPUTS = ['x', 'w_in', 'attn_sinks', 'short_conv_w', 'w_out', 'ln1_g', 'ln1_b', 'ffn_w_up', 'ffn_conv_w', 'ffn_w_down', 'ln2_g', 'ln2_b', 'loss_target', 'm_w_in', 'm_attn_sinks', 'm_short_conv_w', 'm_w_out', 'm_ln1_g', 'm_ln1_b', 'm_ffn_w_up', 'm_ffn_conv_w', 'm_ffn_w_down', 'm_ln2_g', 'm_ln2_b', 'v_w_in', 'v_attn_sinks', 'v_short_conv_w', 'v_w_out', 'v_ln1_g', 'v_ln1_b', 'v_ffn_w_up', 'v_ffn_conv_w', 'v_ffn_w_down', 'v_ln2_g', 'v_ln2_b']
TWIN_OUTPUTS = ['loss', 'grad_x', 'grad_w_in', 'grad_attn_sinks', 'grad_short_conv_w', 'grad_w_out', 'grad_ln1_g', 'grad_ln1_b', 'grad_ffn_w_up', 'grad_ffn_conv_w', 'grad_ffn_w_down', 'grad_ln2_g', 'grad_ln2_b', 'delta_w_in', 'delta_attn_sinks', 'delta_short_conv_w', 'delta_w_out', 'delta_ln1_g', 'delta_ln1_b', 'delta_ffn_w_up', 'delta_ffn_conv_w', 'delta_ffn_w_down', 'delta_ln2_g', 'delta_ln2_b', 'new_m_w_in', 'new_m_attn_sinks', 'new_m_short_conv_w', 'new_m_w_out', 'new_m_ln1_g', 'new_m_ln1_b', 'new_m_ffn_w_up', 'new_m_ffn_conv_w', 'new_m_ffn_w_down', 'new_m_ln2_g', 'new_m_ln2_b', 'new_v_w_in', 'new_v_attn_sinks', 'new_v_short_conv_w', 'new_v_w_out', 'new_v_ln1_g', 'new_v_ln1_b', 'new_v_ffn_w_up', 'new_v_ffn_conv_w', 'new_v_ffn_w_down', 'new_v_ln2_g', 'new_v_ln2_b']
TWIN_LEAF_KINDS = {'loss': 'loss', 'grad_x': 'grad_x', 'grad_w_in': 'grad_w', 'grad_attn_sinks': 'grad_w', 'grad_short_conv_w': 'grad_w', 'grad_w_out': 'grad_w', 'grad_ln1_g': 'grad_w', 'grad_ln1_b': 'grad_w', 'grad_ffn_w_up': 'grad_w', 'grad_ffn_conv_w': 'grad_w', 'grad_ffn_w_down': 'grad_w', 'grad_ln2_g': 'grad_w', 'grad_ln2_b': 'grad_w', 'delta_w_in': 'delta_w', 'delta_attn_sinks': 'delta_w', 'delta_short_conv_w': 'delta_w', 'delta_w_out': 'delta_w', 'delta_ln1_g': 'delta_w', 'delta_ln1_b': 'delta_w', 'delta_ffn_w_up': 'delta_w', 'delta_ffn_conv_w': 'delta_w', 'delta_ffn_w_down': 'delta_w', 'delta_ln2_g': 'delta_w', 'delta_ln2_b': 'delta_w', 'new_m_w_in': 'new_m', 'new_m_attn_sinks': 'new_m', 'new_m_short_conv_w': 'new_m', 'new_m_w_out': 'new_m', 'new_m_ln1_g': 'new_m', 'new_m_ln1_b': 'new_m', 'new_m_ffn_w_up': 'new_m', 'new_m_ffn_conv_w': 'new_m', 'new_m_ffn_w_down': 'new_m', 'new_m_ln2_g': 'new_m', 'new_m_ln2_b': 'new_m', 'new_v_w_in': 'new_v', 'new_v_attn_sinks': 'new_v', 'new_v_short_conv_w': 'new_v', 'new_v_w_out': 'new_v', 'new_v_ln1_g': 'new_v', 'new_v_ln1_b': 'new_v', 'new_v_ffn_w_up': 'new_v', 'new_v_ffn_conv_w': 'new_v', 'new_v_ffn_w_down': 'new_v', 'new_v_ln2_g': 'new_v', 'new_v_ln2_b': 'new_v'}


def _forward(args):
    return _fwd_reference(*[args[k] for k in FWD_PARAMS])


def _output_shape():
    def fwd():
        inp = _fwd_setup_inputs(0)
        return _fwd_reference(*[inp[k] for k in FWD_PARAMS])
    out = _jax.eval_shape(fwd)
    return out.shape, out.dtype

N_MICROBATCH = 1
ADAM_LR = 0.001
ADAM_B1 = 0.9
ADAM_B2 = 0.999
ADAM_EPS = 1e-08
ADAM_WD = 0.01
ADAM_STEP = 10
PER_EXAMPLE_BATCH_AXIS = {'x': 0, 'loss_target': 0}
SHARED_INPUTS = []
_WEIGHT_DTYPES = {'w_in': _jnp.float32, 'attn_sinks': _jnp.float32, 'short_conv_w': _jnp.float32, 'w_out': _jnp.float32, 'ln1_g': _jnp.float32, 'ln1_b': _jnp.float32, 'ffn_w_up': _jnp.float32, 'ffn_conv_w': _jnp.float32, 'ffn_w_down': _jnp.float32, 'ln2_g': _jnp.float32, 'ln2_b': _jnp.float32}
MOMENT_SCALE = {'w_in': 5.178825e-02, 'attn_sinks': 1.169259e-02, 'short_conv_w': 6.278883e-02, 'w_out': 7.347694e-02, 'ln1_g': 9.981132e-01, 'ln1_b': 4.667233e-01, 'ffn_w_up': 2.237633e-02, 'ffn_conv_w': 2.233864e-02, 'ffn_w_down': 6.133637e-02, 'ln2_g': 3.199565e+01, 'ln2_b': 6.879022e-01}


def _to_microbatches(a, axis):
    t = _jnp.moveaxis(a, axis, 0)
    t = t.reshape((N_MICROBATCH, t.shape[0] // N_MICROBATCH) + t.shape[1:])
    return _jnp.moveaxis(t, 1, axis + 1)


def setup_inputs(seed: int = 0) -> dict:
    inp = _fwd_setup_inputs(seed)
    key = _jax.random.fold_in(_jax.random.key(seed), 7919)
    shape, _ = _output_shape()
    out = dict(inp)
    out["loss_target"] = _jax.random.normal(_jax.random.fold_in(key, 0), shape, _jnp.float32)
    for i, name in enumerate(TWIN_WEIGHTS):
        w = inp[name].astype(_jnp.float32)
        if MOMENT_SCALE is None:
            s = _jnp.sqrt(_jnp.mean(_jnp.square(w)) + 1e-30)
        else:
            s = MOMENT_SCALE[name]
        km, kv = _jax.random.split(_jax.random.fold_in(key, i + 1))
        out[name] = w
        out["m_" + name] = s * _jax.random.normal(km, w.shape, _jnp.float32)
        out["v_" + name] = (s * s) * _jax.random.uniform(kv, w.shape, _jnp.float32, 0.5, 1.5)
    if N_MICROBATCH > 1:
        for name, axis in PER_EXAMPLE_BATCH_AXIS.items():
            out[name] = _to_microbatches(out[name], axis)
    return {'x': out['x'], 'w_in': out['w_in'], 'attn_sinks': out['attn_sinks'], 'short_conv_w': out['short_conv_w'], 'w_out': out['w_out'], 'ln1_g': out['ln1_g'], 'ln1_b': out['ln1_b'], 'ffn_w_up': out['ffn_w_up'], 'ffn_conv_w': out['ffn_conv_w'], 'ffn_w_down': out['ffn_w_down'], 'ln2_g': out['ln2_g'], 'ln2_b': out['ln2_b'], 'loss_target': out['loss_target'], 'm_w_in': out['m_w_in'], 'm_attn_sinks': out['m_attn_sinks'], 'm_short_conv_w': out['m_short_conv_w'], 'm_w_out': out['m_w_out'], 'm_ln1_g': out['m_ln1_g'], 'm_ln1_b': out['m_ln1_b'], 'm_ffn_w_up': out['m_ffn_w_up'], 'm_ffn_conv_w': out['m_ffn_conv_w'], 'm_ffn_w_down': out['m_ffn_w_down'], 'm_ln2_g': out['m_ln2_g'], 'm_ln2_b': out['m_ln2_b'], 'v_w_in': out['v_w_in'], 'v_attn_sinks': out['v_attn_sinks'], 'v_short_conv_w': out['v_short_conv_w'], 'v_w_out': out['v_w_out'], 'v_ln1_g': out['v_ln1_g'], 'v_ln1_b': out['v_ln1_b'], 'v_ffn_w_up': out['v_ffn_w_up'], 'v_ffn_conv_w': out['v_ffn_conv_w'], 'v_ffn_w_down': out['v_ffn_w_down'], 'v_ln2_g': out['v_ln2_g'], 'v_ln2_b': out['v_ln2_b']}


def _loss(weights, diff, rest, loss_target):
    with _jax.named_scope("forward"):
        args = {**rest, TWIN_DIFF_INPUT: diff, **{k: w.astype(_WEIGHT_DTYPES[k]) for k, w in weights.items()}}
        y = _forward(args)
    with _jax.named_scope("loss_head"):
        err = _jnp.square(y.astype(_jnp.float32) - loss_target)
        return 0.5 * _jnp.sum(_jnp.mean(err, axis=-1)) if err.ndim else 0.5 * err


def _adamw(w, g, m, v):
    m = ADAM_B1 * m + (1.0 - ADAM_B1) * g
    v = ADAM_B2 * v + (1.0 - ADAM_B2) * _jnp.square(g)
    m_hat = m / (1.0 - ADAM_B1 ** ADAM_STEP)
    v_hat = v / (1.0 - ADAM_B2 ** ADAM_STEP)
    delta = -ADAM_LR * (m_hat / (_jnp.sqrt(v_hat) + ADAM_EPS) + ADAM_WD * w)
    return delta, m, v


def reference(x, w_in, attn_sinks, short_conv_w, w_out, ln1_g, ln1_b, ffn_w_up, ffn_conv_w, ffn_w_down, ln2_g, ln2_b, loss_target, m_w_in, m_attn_sinks, m_short_conv_w, m_w_out, m_ln1_g, m_ln1_b, m_ffn_w_up, m_ffn_conv_w, m_ffn_w_down, m_ln2_g, m_ln2_b, v_w_in, v_attn_sinks, v_short_conv_w, v_w_out, v_ln1_g, v_ln1_b, v_ffn_w_up, v_ffn_conv_w, v_ffn_w_down, v_ln2_g, v_ln2_b):
    given = dict(x=x, w_in=w_in, attn_sinks=attn_sinks, short_conv_w=short_conv_w, w_out=w_out, ln1_g=ln1_g, ln1_b=ln1_b, ffn_w_up=ffn_w_up, ffn_conv_w=ffn_conv_w, ffn_w_down=ffn_w_down, ln2_g=ln2_g, ln2_b=ln2_b, loss_target=loss_target, m_w_in=m_w_in, m_attn_sinks=m_attn_sinks, m_short_conv_w=m_short_conv_w, m_w_out=m_w_out, m_ln1_g=m_ln1_g, m_ln1_b=m_ln1_b, m_ffn_w_up=m_ffn_w_up, m_ffn_conv_w=m_ffn_conv_w, m_ffn_w_down=m_ffn_w_down, m_ln2_g=m_ln2_g, m_ln2_b=m_ln2_b, v_w_in=v_w_in, v_attn_sinks=v_attn_sinks, v_short_conv_w=v_short_conv_w, v_w_out=v_w_out, v_ln1_g=v_ln1_g, v_ln1_b=v_ln1_b, v_ffn_w_up=v_ffn_w_up, v_ffn_conv_w=v_ffn_conv_w, v_ffn_w_down=v_ffn_w_down, v_ln2_g=v_ln2_g, v_ln2_b=v_ln2_b)
    weights = {n: given[n] for n in TWIN_WEIGHTS}
    shared = {n: given[n] for n in SHARED_INPUTS}
    per_example = {n: given[n] for n in ['x']}
    grad_fn = _jax.value_and_grad(_loss, argnums=(0, 1))

    def one_microbatch(ex, loss_target):
        ex = dict(ex)
        diff = ex.pop(TWIN_DIFF_INPUT)
        return grad_fn(weights, diff, {**shared, **ex}, loss_target)

    if N_MICROBATCH == 1:
        loss, (grad_w, grad_x) = one_microbatch(per_example, given["loss_target"])
    else:
        def body(carry, xs):
            loss_sum, grad_sum = carry
            l_k, (gw_k, gx_k) = one_microbatch(xs[0], xs[1])
            with _jax.named_scope("update"):
                return (loss_sum + l_k, _jax.tree.map(_jnp.add, grad_sum, gw_k)), gx_k

        init = (_jnp.zeros((), _jnp.float32), _jax.tree.map(_jnp.zeros_like, weights))
        (loss, grad_w), grad_x = _jax.lax.scan(body, init, (per_example, given["loss_target"]))
    with _jax.named_scope("update"):
        delta_w, new_m, new_v = {}, {}, {}
        for n in TWIN_WEIGHTS:
            delta_w[n], new_m[n], new_v[n] = _adamw(weights[n], grad_w[n], given["m_" + n], given["v_" + n])
    return (loss, grad_x, *[grad_w[n] for n in TWIN_WEIGHTS], *[delta_w[n] for n in TWIN_WEIGHTS],
            *[new_m[n] for n in TWIN_WEIGHTS], *[new_v[n] for n in TWIN_WEIGHTS])
```

```python
import functools

import jax
import jax.numpy as jnp
from jax import lax
from jax.experimental import pallas as pl
from jax.experimental.pallas import tpu as pltpu

F32 = jnp.float32
BF16 = jnp.bfloat16
MESH = pl.DeviceIdType.MESH
ANY = pl.BlockSpec(memory_space=pl.ANY)

HEAD_DIM = 64
N_Q_HEADS = 16
N_KV_HEADS = 2
ATTN_WIDTH = N_Q_HEADS * HEAD_DIM
KV_WIDTH = N_KV_HEADS * HEAD_DIM
BLOCK = 128
ROPE_THETA = 10000.0
LN_EPS = 1e-5
ALPHA = 2.0 ** 0.25
NEG_INF = -1e30
ADAM_LR, ADAM_B1, ADAM_B2, ADAM_EPS, ADAM_WD, ADAM_STEP = 0.001, 0.9, 0.999, 1e-08, 0.01, 10
N_CHIPS = 4
LANES = 128
SLAB = 128


def _cp(sem, vmem_mb):
    return pltpu.CompilerParams(dimension_semantics=sem, vmem_limit_bytes=vmem_mb << 20)


def _matmul(a, b, *, mode, m, n, k, tm, tn, tk, out_dtype, name, vmem_mb, a_spec=None, b_spec=None,
            res=None, alpha=1.0):
    nj, ni, nk = n // tn, m // tm, k // tk
    assert nj * tn == n and ni * tm == m and nk * tk == k, (name, m, n, k, tm, tn, tk)
    if mode == "nn":
        dims = ((1,), (0,))
        a_spec = a_spec or pl.BlockSpec((tm, tk), lambda j, i, kk: (i, kk))
        b_spec = b_spec or pl.BlockSpec((tk, tn), lambda j, i, kk: (kk, j))
    elif mode == "nt":
        dims = ((1,), (1,))
        a_spec = a_spec or pl.BlockSpec((tm, tk), lambda j, i, kk: (i, kk))
        b_spec = b_spec or pl.BlockSpec((tn, tk), lambda j, i, kk: (j, kk))
    else:
        dims = ((0,), (0,))
        a_spec = a_spec or pl.BlockSpec((tk, tm), lambda j, i, kk: (kk, i))
        b_spec = b_spec or pl.BlockSpec((tk, tn), lambda j, i, kk: (kk, j))
    has_res = res is not None

    def body(*refs):
        a_ref, b_ref = refs[0], refs[1]
        res_ref = refs[2] if has_res else None
        o_ref = refs[2 + has_res]
        part = lax.dot_general(a_ref[...], b_ref[...], (dims, ((), ())), preferred_element_type=F32)

        def finish(acc):
            if has_res:
                acc = acc + alpha * res_ref[...]
            o_ref[...] = acc.astype(o_ref.dtype)

        if nk == 1:
            finish(part)
        else:
            acc_ref = refs[3 + has_res]
            kk = pl.program_id(2)

            @pl.when(kk == 0)
            def _():
                acc_ref[...] = part

            @pl.when(kk > 0)
            def _():
                acc_ref[...] += part

            @pl.when(kk == nk - 1)
            def _():
                finish(acc_ref[...])

    in_specs = [a_spec, b_spec]
    args = [a, b]
    if has_res:
        in_specs.append(pl.BlockSpec((tm, tn), lambda j, i, kk: (i, j)))
        args.append(res)
    return pl.pallas_call(
        body, name=name, grid=(nj, ni, nk), in_specs=in_specs,
        out_specs=pl.BlockSpec((tm, tn), lambda j, i, kk: (i, j)),
        out_shape=jax.ShapeDtypeStruct((m, n), out_dtype),
        scratch_shapes=[pltpu.VMEM((tm, tn), F32)] if nk > 1 else [],
        compiler_params=_cp(("arbitrary", "arbitrary", "arbitrary"), vmem_mb),
    )(*args)


def _pick(total, want, mult):
    if total <= want:
        return total
    for t in range(want, 0, -1):
        if total % t == 0 and t % mult == 0:
            return t
    return total


def _rope_tables(s):
    half = HEAD_DIM // 2
    inv_freq = ROPE_THETA ** (-jnp.arange(half, dtype=F32) / half)
    ang = jnp.arange(s, dtype=F32)[:, None] * inv_freq[None, :]
    cos = jnp.tile(jnp.cos(ang), (1, LANES // half))
    sin = jnp.tile(jnp.concatenate([-jnp.sin(ang), jnp.sin(ang)], axis=1), (1, LANES // HEAD_DIM))
    return cos, sin


def _rope(x, cos, sin, lo):
    partner = jnp.where(lo, pltpu.roll(x, LANES - HEAD_DIM // 2, 1), pltpu.roll(x, HEAD_DIM // 2, 1))
    return x * cos + partner * sin


def _dot(a, b, dims):
    return lax.dot_general(a, b, (dims, ((), ())), preferred_element_type=F32)


NN, NT, TN = ((1,), (0,)), ((1,), (1,)), ((0,), (0,))


def _kv_variants(t, head_lo):
    r = pltpu.roll(t, HEAD_DIM, 1)
    zero = jnp.zeros_like(t)
    a = (jnp.where(head_lo, t, zero).astype(BF16), jnp.where(head_lo, r, zero).astype(BF16))
    b = (jnp.where(head_lo, zero, r).astype(BF16), jnp.where(head_lo, zero, t).astype(BF16))
    return a, b


def _attn_probs(qp, ka, kb, valid, sink_a, sink_b):
    out = []
    for kk, sink in ((ka, sink_a), (kb, sink_b)):
        s = jnp.where(valid, _dot(qp, kk, NT), NEG_INF)
        mx = jnp.maximum(jnp.max(s, axis=1, keepdims=True), sink)
        e = jnp.exp(s - mx)
        es = jnp.exp(sink - mx)
        inv = 1.0 / (jnp.sum(e, axis=1, keepdims=True) + es)
        out.append((e * inv, es * inv))
    return out


def _attn_common(i, q_ref, k_ref, v_ref, kp_ref, vp_ref, cos_ref, sin_ref, cosp_ref, sinp_ref):
    lane = lax.broadcasted_iota(jnp.int32, (1, LANES), 1)
    lo = (lane % HEAD_DIM) < (HEAD_DIM // 2)
    head_lo = lane < HEAD_DIM
    cos, sin = cos_ref[...], sin_ref[...]
    kc = _rope(k_ref[...].astype(F32), cos, sin, lo)
    kp = _rope(kp_ref[...].astype(F32), cosp_ref[...], sinp_ref[...], lo)
    kext = jnp.concatenate([kp, kc], axis=0)
    vext = jnp.concatenate([vp_ref[...].astype(F32), v_ref[...].astype(F32)], axis=0)
    ka, kb = _kv_variants(kext, head_lo)
    va, vb = _kv_variants(vext, head_lo)
    qi = lax.broadcasted_iota(jnp.int32, (BLOCK, 1), 0)
    kj = lax.broadcasted_iota(jnp.int32, (1, 2 * BLOCK), 1)
    valid = (kj > qi) & (kj <= qi + BLOCK) & ((kj >= BLOCK) | (i > 0))
    return lo, head_lo, cos, sin, ka, kb, va, vb, valid


def _attn_fwd(proj, sinks, cos, sin, s):
    nb = s // BLOCK
    kcol, vcol = ATTN_WIDTH // LANES, ATTN_WIDTH // LANES + 1

    def body(q_ref, k_ref, v_ref, kp_ref, vp_ref, cos_ref, sin_ref, cosp_ref, sinp_ref, sink_ref, o_ref):
        i = pl.program_id(0)
        lo, head_lo, cs, sn, ka, kb, va, vb, valid = _attn_common(
            i, q_ref, k_ref, v_ref, kp_ref, vp_ref, cos_ref, sin_ref, cosp_ref, sinp_ref)
        for p in range(N_Q_HEADS // 2):
            j = p // (N_Q_HEADS // 2 // N_KV_HEADS)
            qp = (_rope(q_ref[:, p * LANES:(p + 1) * LANES].astype(F32), cs, sn, lo) * HEAD_DIM ** -0.5).astype(BF16)
            (pa, _), (pb, _) = _attn_probs(qp, ka[j], kb[j], valid, sink_ref[0, 2 * p], sink_ref[0, 2 * p + 1])
            o = _dot(pa.astype(BF16), va[j], NN) + _dot(pb.astype(BF16), vb[j], NN)
            o_ref[:, p * LANES:(p + 1) * LANES] = o.astype(BF16)

    prev = lambda i: (jnp.maximum(i - 1, 0), 0)
    return pl.pallas_call(
        body, name="attn_fwd", grid=(nb,),
        in_specs=[pl.BlockSpec((BLOCK, ATTN_WIDTH), lambda i: (i, 0)),
                  pl.BlockSpec((BLOCK, LANES), lambda i: (i, kcol)),
                  pl.BlockSpec((BLOCK, LANES), lambda i: (i, vcol)),
                  pl.BlockSpec((BLOCK, LANES), lambda i: (jnp.maximum(i - 1, 0), kcol)),
                  pl.BlockSpec((BLOCK, LANES), lambda i: (jnp.maximum(i - 1, 0), vcol)),
                  pl.BlockSpec((BLOCK, LANES), lambda i: (i, 0)),
                  pl.BlockSpec((BLOCK, LANES), lambda i: (i, 0)),
                  pl.BlockSpec((BLOCK, LANES), prev),
                  pl.BlockSpec((BLOCK, LANES), prev),
                  pl.BlockSpec(memory_space=pltpu.SMEM)],
        out_specs=pl.BlockSpec((BLOCK, ATTN_WIDTH), lambda i: (i, 0)),
        out_shape=jax.ShapeDtypeStruct((s, ATTN_WIDTH), BF16),
        compiler_params=_cp(("arbitrary",), 32),
    )(proj, proj, proj, proj, proj, cos, sin, cos, sin, sinks)


def _attn_bwd(proj, dmix, sinks, cos, sin, s):
    nb = s // BLOCK
    kcol, vcol = ATTN_WIDTH // LANES, ATTN_WIDTH // LANES + 1
    pairs_per_kv = N_Q_HEADS // 2 // N_KV_HEADS

    def body(q_ref, k_ref, v_ref, kp_ref, vp_ref, cos_ref, sin_ref, cosp_ref, sinp_ref, sink_ref, do_ref,
             dq_ref, dk_ref, dv_ref, dsink_ref, ck_ref, cv_ref):
        g = pl.program_id(0)
        i = nb - 1 - g

        @pl.when(g == 0)
        def _():
            ck_ref[...] = jnp.zeros_like(ck_ref)
            cv_ref[...] = jnp.zeros_like(cv_ref)
            dsink_ref[...] = jnp.zeros_like(dsink_ref)

        lo, head_lo, cs, sn, ka, kb, va, vb, valid = _attn_common(
            i, q_ref, k_ref, v_ref, kp_ref, vp_ref, cos_ref, sin_ref, cosp_ref, sinp_ref)
        lane = lax.broadcasted_iota(jnp.int32, (1, LANES), 1)
        dk_j = [jnp.zeros((2 * BLOCK, LANES), F32) for _ in range(N_KV_HEADS)]
        dv_j = [jnp.zeros((2 * BLOCK, LANES), F32) for _ in range(N_KV_HEADS)]
        dsink = jnp.zeros((1, LANES), F32)
        for p in range(N_Q_HEADS // 2):
            j = p // pairs_per_kv
            qp = (_rope(q_ref[:, p * LANES:(p + 1) * LANES].astype(F32), cs, sn, lo) * HEAD_DIM ** -0.5).astype(BF16)
            probs = _attn_probs(qp, ka[j], kb[j], valid, sink_ref[0, 2 * p], sink_ref[0, 2 * p + 1])
            do = do_ref[:, p * LANES:(p + 1) * LANES]
            dq_r = jnp.zeros((BLOCK, LANES), F32)
            dkc, dvc = [], []
            for hh, ((pr, ps), kk, vv) in enumerate(zip(probs, (ka[j], kb[j]), (va[j], vb[j]))):
                dp = _dot(do, vv, NT)
                delta = jnp.sum(pr * dp, axis=1, keepdims=True)
                ds = (pr * (dp - delta)).astype(BF16)
                dsink = dsink + jnp.where(lane == 2 * p + hh, -jnp.sum(ps * delta, axis=0, keepdims=True), 0.0)
                dq_r = dq_r + _dot(ds, kk, NN)
                dkc.append(_dot(ds, qp, TN))
                dvc.append(_dot(pr.astype(BF16), do, TN))
            dk_j[j] = dk_j[j] + jnp.where(head_lo, dkc[0], dkc[1])
            dv_j[j] = dv_j[j] + jnp.where(head_lo, dvc[0], dvc[1])
            dq = _rope(dq_r * HEAD_DIM ** -0.5, cs, -sn, lo)
            dq_ref[:, p * LANES:(p + 1) * LANES] = dq.astype(BF16)
        tot_k = [t + pltpu.roll(t, HEAD_DIM, 1) for t in dk_j]
        tot_v = [t + pltpu.roll(t, HEAD_DIM, 1) for t in dv_j]
        dkext = jnp.where(head_lo, tot_k[0], tot_k[1])
        dvext = jnp.where(head_lo, tot_v[0], tot_v[1])
        dk_r = dkext[BLOCK:] + ck_ref[...]
        dk_ref[...] = _rope(dk_r, cs, -sn, lo).astype(BF16)
        dv_ref[...] = (dvext[BLOCK:] + cv_ref[...]).astype(BF16)
        ck_ref[...] = dkext[:BLOCK]
        cv_ref[...] = dvext[:BLOCK]
        dsink_ref[0:1, :] += dsink

    cur = lambda col: (lambda g: (nb - 1 - g, col))
    prv = lambda col: (lambda g: (jnp.maximum(nb - 2 - g, 0), col))
    blk = lambda w, f: pl.BlockSpec((BLOCK, w), f)
    return pl.pallas_call(
        body, name="attn_bwd", grid=(nb,),
        in_specs=[blk(ATTN_WIDTH, cur(0)), blk(LANES, cur(kcol)), blk(LANES, cur(vcol)),
                  blk(LANES, prv(kcol)), blk(LANES, prv(vcol)),
                  blk(LANES, cur(0)), blk(LANES, cur(0)), blk(LANES, prv(0)), blk(LANES, prv(0)),
                  pl.BlockSpec(memory_space=pltpu.SMEM),
                  blk(ATTN_WIDTH, cur(0))],
        out_specs=[blk(ATTN_WIDTH, cur(0)), blk(LANES, cur(0)), blk(LANES, cur(0)),
                   pl.BlockSpec((8, LANES), lambda g: (0, 0))],
        out_shape=[jax.ShapeDtypeStruct((s, ATTN_WIDTH), BF16), jax.ShapeDtypeStruct((s, LANES), BF16),
                   jax.ShapeDtypeStruct((s, LANES), BF16), jax.ShapeDtypeStruct((8, LANES), F32)],
        scratch_shapes=[pltpu.VMEM((BLOCK, LANES), F32), pltpu.VMEM((BLOCK, LANES), F32)],
        compiler_params=_cp(("arbitrary",), 32),
    )(proj, proj, proj, proj, proj, cos, sin, cos, sin, sinks, dmix)


def _causal_conv(x, prev8, w):
    r = x.shape[0]
    row = lax.broadcasted_iota(jnp.int32, (r, 1), 0)
    s1 = jnp.where(row == 0, prev8[7:8], pltpu.roll(x, 1, 0))
    s2 = jnp.where(row == 0, prev8[6:7], jnp.where(row == 1, prev8[7:8], pltpu.roll(x, 2, 0)))
    return w[0:1] * s2 + w[1:2] * s1 + w[2:3] * x, s1, s2


def _conv_bwd(dy, x, s1, s2, w, next8):
    r = x.shape[0]
    row = lax.broadcasted_iota(jnp.int32, (r, 1), 0)
    n1 = jnp.where(row == r - 1, next8[0:1], pltpu.roll(dy, r - 1, 0))
    n2 = jnp.where(row == r - 2, next8[0:1], jnp.where(row == r - 1, next8[1:2], pltpu.roll(dy, r - 2, 0)))
    dx = w[2:3] * dy + w[1:2] * n1 + w[0:1] * n2
    dws = [jnp.sum(dy * t, axis=0, keepdims=True) for t in (s2, s1, x)]
    return dx, dws


CONV_COLS = 256


def _convmix_cols(d):
    conv_w = d - ATTN_WIDTH
    base = (ATTN_WIDTH + 2 * KV_WIDTH) // CONV_COLS
    step = conv_w // CONV_COLS
    return base, base + step, base + 2 * step, step


def _convmix_fwd(proj, scw8, s, d):
    gb0, gc0, h0, ncb = _convmix_cols(d)
    tr = _pick(s, 1024, 16)
    ni = s // tr

    def body(gb_ref, gc_ref, h_ref, w_ref, o_ref, carry_ref):
        @pl.when(pl.program_id(1) == 0)
        def _():
            carry_ref[...] = jnp.zeros_like(carry_ref)

        gch = gc_ref[...].astype(F32) * h_ref[...].astype(F32)
        cc, _, _ = _causal_conv(gch, carry_ref[...], w_ref[...])
        o_ref[...] = (gb_ref[...].astype(F32) * cc).astype(BF16)
        carry_ref[...] = gch[tr - 8:]

    spec = lambda c0: pl.BlockSpec((tr, CONV_COLS), lambda j, i: (i, c0 + j))
    return pl.pallas_call(
        body, name="convmix_fwd", grid=(ncb, ni),
        in_specs=[spec(gb0), spec(gc0), spec(h0), pl.BlockSpec((8, CONV_COLS), lambda j, i: (0, j))],
        out_specs=pl.BlockSpec((tr, CONV_COLS), lambda j, i: (i, j)),
        out_shape=jax.ShapeDtypeStruct((s, d - ATTN_WIDTH), BF16),
        scratch_shapes=[pltpu.VMEM((8, CONV_COLS), F32)],
        compiler_params=_cp(("arbitrary", "arbitrary"), 32),
    )(proj, proj, proj, scw8)


def _convmix_bwd(proj, dmix, scw8, s, d):
    gb0, gc0, h0, ncb = _convmix_cols(d)
    tr = _pick(s, 1024, 16)
    ni = s // tr
    dc0 = ATTN_WIDTH // CONV_COLS

    def body(dc_ref, gb_ref, gc_ref, h_ref, gcp_ref, hp_ref, w_ref, d3_ref, dw_ref, nxt_ref):
        g = pl.program_id(1)
        i = ni - 1 - g

        @pl.when(g == 0)
        def _():
            nxt_ref[...] = jnp.zeros_like(nxt_ref)
            dw_ref[...] = jnp.zeros_like(dw_ref)

        w = w_ref[...]
        gb, gc, h = gb_ref[...].astype(F32), gc_ref[...].astype(F32), h_ref[...].astype(F32)
        gch = gc * h
        prev8 = (gcp_ref[...].astype(F32) * hp_ref[...].astype(F32))[8:16] * (i > 0).astype(F32)
        cc, s1, s2 = _causal_conv(gch, prev8, w)
        dc = dc_ref[...].astype(F32)
        dcc = dc * gb
        dgch, dws = _conv_bwd(dcc, gch, s1, s2, w, nxt_ref[...])
        d3_ref[0] = (dc * cc).astype(BF16)
        d3_ref[1] = (dgch * h).astype(BF16)
        d3_ref[2] = (dgch * gc).astype(BF16)
        for t in range(3):
            dw_ref[t:t + 1, :] += dws[t]
        nxt_ref[...] = dcc[0:8]

    cur = lambda c0: pl.BlockSpec((tr, CONV_COLS), lambda j, g: (ni - 1 - g, c0 + j))
    prv = lambda c0: pl.BlockSpec((16, CONV_COLS), lambda j, g: (jnp.maximum((ni - 1 - g) * (tr // 16) - 1, 0), c0 + j))
    return pl.pallas_call(
        body, name="convmix_bwd", grid=(ncb, ni),
        in_specs=[cur(dc0), cur(gb0), cur(gc0), cur(h0), prv(gc0), prv(h0),
                  pl.BlockSpec((8, CONV_COLS), lambda j, g: (0, j))],
        out_specs=[pl.BlockSpec((3, tr, CONV_COLS), lambda j, g: (0, ni - 1 - g, j)),
                   pl.BlockSpec((8, CONV_COLS), lambda j, g: (0, j))],
        out_shape=[jax.ShapeDtypeStruct((3, s, d - ATTN_WIDTH), BF16), jax.ShapeDtypeStruct((8, d - ATTN_WIDTH), F32)],
        scratch_shapes=[pltpu.VMEM((8, CONV_COLS), F32)],
        compiler_params=_cp(("arbitrary", "arbitrary"), 32),
    )(dmix, proj, proj, proj, proj, proj, scw8)


def _ln_fwd(z):
    mu = jnp.mean(z, axis=-1, keepdims=True)
    zc = z - mu
    var = jnp.mean(zc * zc, axis=-1, keepdims=True)
    rstd = lax.rsqrt(var + LN_EPS)
    return zc * rstd, rstd


def _ln_bwd(dout, xh, rstd, g):
    dxh = dout * g
    c1 = jnp.mean(dxh, axis=-1, keepdims=True)
    c2 = jnp.mean(dxh * xh, axis=-1, keepdims=True)
    dz = rstd * (dxh - c1 - xh * c2)
    return dz, jnp.sum(dout * xh, axis=0, keepdims=True), jnp.sum(dout, axis=0, keepdims=True)


def _outproj_ln1(attn, conv, wout, x, g1, b1, s, d):
    tm = _pick(s, 256, 16)
    ka = attn.shape[1]

    def body(a_ref, c_ref, wt_ref, wb_ref, x_ref, g_ref, b_ref, x1_ref, x1b_ref, xh_ref, rs_ref):
        y = _dot(a_ref[...], wt_ref[...], NN) + _dot(c_ref[...], wb_ref[...], NN)
        xh, rstd = _ln_fwd(ALPHA * x_ref[...] + y)
        x1 = xh * g_ref[...] + b_ref[...]
        x1_ref[...] = x1
        x1b_ref[...] = x1.astype(BF16)
        xh_ref[...] = xh.astype(BF16)
        rs_ref[...] = rstd

    row = lambda w: pl.BlockSpec((tm, w), lambda i: (i, 0))
    vec = pl.BlockSpec((1, d), lambda i: (0, 0))
    return pl.pallas_call(
        body, name="outproj_ln1", grid=(s // tm,),
        in_specs=[row(ka), row(d - ka), pl.BlockSpec((ka, d), lambda i: (0, 0)),
                  pl.BlockSpec((d - ka, d), lambda i: (ka // (d - ka), 0)), row(d), vec, vec],
        out_specs=[row(d), row(d), row(d), row(1)],
        out_shape=[jax.ShapeDtypeStruct((s, d), F32), jax.ShapeDtypeStruct((s, d), BF16),
                   jax.ShapeDtypeStruct((s, d), BF16), jax.ShapeDtypeStruct((s, 1), F32)],
        compiler_params=_cp(("arbitrary",), 48),
    )(attn, conv, wout, wout, x, g1, b1)


def _ffn_up(x1b, wup, fcw8, s, d, dff):
    tm = _pick(s, 1024, 16)
    tn = _pick(dff, 512, LANES)
    nj, ni = dff // tn, s // tm

    def body(x_ref, wa_ref, wg_ref, ca_ref, cg_ref, u_ref, h_ref, carry_ref):
        @pl.when(pl.program_id(1) == 0)
        def _():
            carry_ref[...] = jnp.zeros_like(carry_ref)

        xa = x_ref[...]
        ys = []
        for part, (w_ref, c_ref) in enumerate(((wa_ref, ca_ref), (wg_ref, cg_ref))):
            ub = _dot(xa, w_ref[...], NN).astype(BF16)
            u_ref[part] = ub
            u = ub.astype(F32)
            y, _, _ = _causal_conv(u, carry_ref[part], c_ref[...])
            carry_ref[part] = u[tm - 8:]
            ys.append(y)
        a2, g2 = ys
        sig = 1.0 / (1.0 + jnp.exp(-a2))
        h_ref[...] = (a2 * sig * g2).astype(BF16)

    return pl.pallas_call(
        body, name="ffn_up", grid=(nj, ni),
        in_specs=[pl.BlockSpec((tm, d), lambda j, i: (i, 0)),
                  pl.BlockSpec((d, tn), lambda j, i: (0, j)),
                  pl.BlockSpec((d, tn), lambda j, i: (0, j + nj)),
                  pl.BlockSpec((8, tn), lambda j, i: (0, j)),
                  pl.BlockSpec((8, tn), lambda j, i: (0, j + nj))],
        out_specs=[pl.BlockSpec((2, tm, tn), lambda j, i: (0, i, j)),
                   pl.BlockSpec((tm, tn), lambda j, i: (i, j))],
        out_shape=[jax.ShapeDtypeStruct((2, s, dff), BF16), jax.ShapeDtypeStruct((s, dff), BF16)],
        scratch_shapes=[pltpu.VMEM((2, 8, tn), F32)],
        compiler_params=_cp(("arbitrary", "arbitrary"), 48),
    )(x1b, wup, wup, fcw8, fcw8)


def _ffn_mid_bwd(dh, u3, fcw8, s, dff):
    tm = _pick(s, 1024, 16)
    tn = _pick(dff, 512, LANES)
    nj, ni = dff // tn, s // tm

    def body(dh_ref, u_ref, up_ref, ca_ref, cg_ref, du_ref, dw_ref, nxt_ref):
        g = pl.program_id(1)
        i = ni - 1 - g

        @pl.when(g == 0)
        def _():
            nxt_ref[...] = jnp.zeros_like(nxt_ref)
            dw_ref[...] = jnp.zeros_like(dw_ref)

        has_prev = (i > 0).astype(F32)
        saved = []
        for part, c_ref in enumerate((ca_ref, cg_ref)):
            u = u_ref[part].astype(F32)
            prev8 = up_ref[part].astype(F32)[8:16] * has_prev
            y, s1, s2 = _causal_conv(u, prev8, c_ref[...])
            saved.append((u, s1, s2, y))
        a2, g2 = saved[0][3], saved[1][3]
        sig = 1.0 / (1.0 + jnp.exp(-a2))
        silu = a2 * sig
        dhv = dh_ref[...].astype(F32)
        dys = (dhv * g2 * (sig * (1.0 + a2 * (1.0 - sig))), dhv * silu)
        for part, (c_ref, dy) in enumerate(zip((ca_ref, cg_ref), dys)):
            u, s1, s2, _ = saved[part]
            dx, dws = _conv_bwd(dy, u, s1, s2, c_ref[...], nxt_ref[part])
            du_ref[part] = dx.astype(BF16)
            for t in range(3):
                dw_ref[part, t:t + 1, :] += dws[t]
            nxt_ref[part] = dy[0:8]

    return pl.pallas_call(
        body, name="ffn_mid_bwd", grid=(nj, ni),
        in_specs=[pl.BlockSpec((tm, tn), lambda j, g: (ni - 1 - g, j)),
                  pl.BlockSpec((2, tm, tn), lambda j, g: (0, ni - 1 - g, j)),
                  pl.BlockSpec((2, 16, tn), lambda j, g: (0, jnp.maximum((ni - 1 - g) * (tm // 16) - 1, 0), j)),
                  pl.BlockSpec((8, tn), lambda j, g: (0, j)),
                  pl.BlockSpec((8, tn), lambda j, g: (0, j + nj))],
        out_specs=[pl.BlockSpec((2, tm, tn), lambda j, g: (0, ni - 1 - g, j)),
                   pl.BlockSpec((2, 8, tn), lambda j, g: (0, 0, j))],
        out_shape=[jax.ShapeDtypeStruct((2, s, dff), BF16), jax.ShapeDtypeStruct((2, 8, dff), F32)],
        scratch_shapes=[pltpu.VMEM((2, 8, tn), F32)],
        compiler_params=_cp(("arbitrary", "arbitrary"), 56),
    )(dh, u3, u3, fcw8, fcw8)


def _ffn_down_loss(hmid, wdown, x1, target, g2, b2, s, d, dff):
    tm = _pick(s, 512, SLAB)
    tk = _pick(dff, 1408, LANES)
    ni, nk = s // tm, dff // tk
    slab = min(SLAB, tm)

    def body(h_ref, w_ref, x1_ref, t_ref, g_ref, b_ref, dz_ref, dzb_ref, st_ref, acc_ref):
        i, kk = pl.program_id(0), pl.program_id(1)

        @pl.when((i == 0) & (kk == 0))
        def _():
            st_ref[...] = jnp.zeros_like(st_ref)

        part = _dot(h_ref[...], w_ref[...], NN)

        @pl.when(kk == 0)
        def _():
            acc_ref[...] = part

        @pl.when(kk > 0)
        def _():
            acc_ref[...] += part

        @pl.when(kk == nk - 1)
        def _():
            g, b = g_ref[...], b_ref[...]

            def one(sl, carry):
                rows = pl.ds(pl.multiple_of(sl * slab, slab), slab)
                xh, rstd = _ln_fwd(ALPHA * x1_ref[rows, :] + acc_ref[rows, :])
                diff = xh * g + b - t_ref[rows, :]
                sq = jnp.sum(jnp.sum(diff * diff, axis=1, keepdims=True), axis=0, keepdims=True)
                dz, dg, db = _ln_bwd(diff * (1.0 / d), xh, rstd, g)
                dz_ref[rows, :] = dz
                dzb_ref[rows, :] = dz.astype(BF16)
                st_ref[0:1, :] += dg
                st_ref[1:2, :] += db
                st_ref[2:3, :] += sq
                return carry

            lax.fori_loop(0, tm // slab, one, 0)

    row = pl.BlockSpec((tm, d), lambda i, kk: (i, 0))
    vec = pl.BlockSpec((1, d), lambda i, kk: (0, 0))
    return pl.pallas_call(
        body, name="ffn_down_loss", grid=(ni, nk),
        in_specs=[pl.BlockSpec((tm, tk), lambda i, kk: (i, kk)), pl.BlockSpec((tk, d), lambda i, kk: (kk, 0)),
                  row, row, vec, vec],
        out_specs=[row, row, pl.BlockSpec((8, d), lambda i, kk: (0, 0))],
        out_shape=[jax.ShapeDtypeStruct((s, d), F32), jax.ShapeDtypeStruct((s, d), BF16),
                   jax.ShapeDtypeStruct((8, d), F32)],
        scratch_shapes=[pltpu.VMEM((tm, d), F32)],
        compiler_params=_cp(("arbitrary", "arbitrary"), 56),
    )(hmid, wdown, x1, target, g2, b2)


def _ffn_dx_ln1_bwd(du3, wup, dz2, xh1, rstd1, g1, s, d, dff):
    tm = _pick(s, 512, SLAB)
    tk = _pick(dff, 1408, LANES)
    nkh = dff // tk
    ni, nk = s // tm, 2 * nkh
    slab = min(SLAB, tm)

    def body(a_ref, w_ref, dz2_ref, xh_ref, rs_ref, g_ref, dz_ref, dzb_ref, st_ref, acc_ref):
        i, kk = pl.program_id(0), pl.program_id(1)

        @pl.when((i == 0) & (kk == 0))
        def _():
            st_ref[...] = jnp.zeros_like(st_ref)

        part = _dot(a_ref[...], w_ref[...], NT)

        @pl.when(kk == 0)
        def _():
            acc_ref[...] = part

        @pl.when(kk > 0)
        def _():
            acc_ref[...] += part

        @pl.when(kk == nk - 1)
        def _():
            g = g_ref[...]

            def one(sl, carry):
                rows = pl.ds(pl.multiple_of(sl * slab, slab), slab)
                dx1 = ALPHA * dz2_ref[rows, :] + acc_ref[rows, :]
                dz, dg, db = _ln_bwd(dx1, xh_ref[rows, :].astype(F32), rs_ref[rows, :], g)
                dz_ref[rows, :] = dz
                dzb_ref[rows, :] = dz.astype(BF16)
                st_ref[0:1, :] += dg
                st_ref[1:2, :] += db
                return carry

            lax.fori_loop(0, tm // slab, one, 0)

    row = pl.BlockSpec((tm, d), lambda i, kk: (i, 0))
    return pl.pallas_call(
        body, name="ffn_dx_ln1_bwd", grid=(ni, nk),
        in_specs=[pl.BlockSpec((None, tm, tk), lambda i, kk: (kk // nkh, i, kk % nkh)),
                  pl.BlockSpec((d, tk), lambda i, kk: (0, kk)),
                  row, row, pl.BlockSpec((tm, 1), lambda i, kk: (i, 0)), pl.BlockSpec((1, d), lambda i, kk: (0, 0))],
        out_specs=[row, row, pl.BlockSpec((8, d), lambda i, kk: (0, 0))],
        out_shape=[jax.ShapeDtypeStruct((s, d), F32), jax.ShapeDtypeStruct((s, d), BF16),
                   jax.ShapeDtypeStruct((8, d), F32)],
        scratch_shapes=[pltpu.VMEM((tm, d), F32)],
        compiler_params=_cp(("arbitrary", "arbitrary"), 56),
    )(du3, wup, dz2, xh1, rstd1, g1)


def _local_step(x, target, win_t, wout, wup, wdown, scw8, fcw8, sinks, ln1_g, ln1_b, ln2_g, ln2_b):
    s, d = x.shape
    dff = wdown.shape[0]
    n_in = win_t.shape[0]
    xb = x.astype(BF16)
    cos, sin = _rope_tables(s)

    proj = _matmul(xb, win_t, mode="nt", m=s, n=n_in, k=d, tm=_pick(s, 512, 16), tn=_pick(n_in, 2176, LANES), tk=d,
                   out_dtype=BF16, name="in_proj", vmem_mb=48)
    attn = _attn_fwd(proj, sinks, cos, sin, s)
    conv = _convmix_fwd(proj, scw8, s, d)
    x1, x1b, xh1, rstd1 = _outproj_ln1(attn, conv, wout, x, ln1_g, ln1_b, s, d)
    u3, hmid = _ffn_up(x1b, wup, fcw8, s, d, dff)
    dz2, dz2b, st2 = _ffn_down_loss(hmid, wdown, x1, target, ln2_g, ln2_b, s, d, dff)

    ts = _pick(s, 512, 16)
    g_wdown = _matmul(hmid, dz2b, mode="tn", m=dff, n=d, k=s, tm=_pick(dff, 1408, LANES), tn=d, tk=ts,
                      out_dtype=BF16, name="grad_w_down", vmem_mb=48)
    dh = _matmul(dz2b, wdown, mode="nt", m=s, n=dff, k=d, tm=_pick(s, 1024, 16), tn=_pick(dff, 1408, LANES), tk=d,
                 out_dtype=BF16, name="ffn_dh", vmem_mb=48)
    du3, dfcw = _ffn_mid_bwd(dh, u3, fcw8, s, dff)
    tnu = _pick(dff, 1408, LANES)
    njh = dff // tnu
    g_wup = _matmul(x1b, du3, mode="tn", m=d, n=2 * dff, k=s, tm=d, tn=tnu, tk=ts, out_dtype=BF16, name="grad_w_up",
                    vmem_mb=48, b_spec=pl.BlockSpec((None, ts, tnu), lambda j, i, kk: (j // njh, kk, j % njh)))
    dz1, dz1b, st1 = _ffn_dx_ln1_bwd(du3, wup, dz2, xh1, rstd1, ln1_g, s, d, dff)

    mix = jnp.concatenate([attn, conv], axis=1)
    g_wout = _matmul(mix, dz1b, mode="tn", m=d, n=d, k=s, tm=_pick(d, 1024, LANES), tn=d, tk=ts, out_dtype=BF16,
                     name="grad_w_out", vmem_mb=48)
    dmix = _matmul(dz1b, wout, mode="nt", m=s, n=d, k=d, tm=_pick(s, 1024, 16), tn=_pick(d, 1024, LANES), tk=d,
                   out_dtype=BF16, name="out_dmix", vmem_mb=48)
    d3, dscw = _convmix_bwd(proj, dmix, scw8, s, d)
    dq, dk, dv, dsink = _attn_bwd(proj, dmix, sinks, cos, sin, s)
    dproj = jnp.concatenate([dq, dk, dv, d3[0], d3[1], d3[2]], axis=1)
    g_win_t = _matmul(dproj, xb, mode="tn", m=n_in, n=d, k=s, tm=_pick(n_in, 2176, LANES), tn=_pick(d, 1024, LANES),
                      tk=ts, out_dtype=BF16, name="grad_w_in", vmem_mb=48)
    grad_x = _matmul(dproj, win_t, mode="nn", m=s, n=d, k=n_in, tm=_pick(s, 512, 16), tn=_pick(d, 1024, LANES),
                     tk=n_in, out_dtype=F32, name="grad_x", vmem_mb=56, res=dz1, alpha=ALPHA)
    small = dict(loss_sq=st2[2, 0], ln2_g=st2[0], ln2_b=st2[1], ln1_g=st1[0], ln1_b=st1[1], sinks=dsink[0, :N_Q_HEADS],
                 fcw=jnp.concatenate([dfcw[0, :3], dfcw[1, :3]], axis=1), scw=dscw[:3])
    return grad_x, dict(win_t=g_win_t, wout=g_wout, wup=g_wup, wdown=g_wdown), small


BIG = ("win_t", "wout", "wup", "wdown")


def _geom(shard_shapes):
    out = {}
    for name in BIG:
        r, c = shard_shapes[name]
        out[name] = ("col" if name == "wup" else "row", (r, c), (r // 2, c))
    return out


def _full_shape(kind, shard):
    r, c = shard
    return (N_CHIPS * r, c) if kind == "row" else (r, N_CHIPS * c)


def _block_of(ref, kind, shard, chip):
    r, c = shard
    return ref.at[pl.ds(chip * r, r), :] if kind == "row" else ref.at[:, pl.ds(chip * c, c)]


def _piece_of(ref, kind, shard, chip, half):
    r, c = shard
    if kind == "row":
        return ref.at[pl.ds(chip * r + half * (r // 2), r // 2), :]
    return ref.at[pl.ds(half * (r // 2), r // 2), pl.ds(chip * c, c)]


def _shard_piece(ref, shard, half):
    r, _ = shard
    return ref.at[pl.ds(half * (r // 2), r // 2), :]


def _me():
    return lax.axis_index("x"), lax.axis_index("y"), lax.axis_index("c")


def _other_chips(x, y):
    return [(1 - x, y), (x, 1 - y), (1 - x, 1 - y)]


def _remote(src, dst, send_sem, recv_sem, dev):
    return pltpu.make_async_remote_copy(src_ref=src, dst_ref=dst, send_sem=send_sem, recv_sem=recv_sem,
                                        device_id=dev, device_id_type=MESH)


def _allgather_weights(shards, small_shards):
    geom = _geom({n: shards[n].shape for n in BIG})
    nb, ns = len(BIG), len(small_shards)
    small_w = [a.shape[1] for a in small_shards]

    def body(*refs):
        sh = refs[:nb]
        sm = refs[nb:nb + ns]
        full = refs[nb + ns:2 * nb + ns]
        smf = refs[2 * nb + ns:2 * nb + 2 * ns]
        send, recv, loc = refs[2 * nb + 2 * ns:]
        x, y, c = _me()
        chip = 2 * x + y
        sib = (x, y, 1 - c)
        others = _other_chips(x, y)
        locals_, sends = [], []
        for m, name in enumerate(BIG):
            kind, shard, _ = geom[name]
            cp = pltpu.make_async_copy(sh[m], _block_of(full[m], kind, shard, chip), loc.at[m])
            cp.start()
            locals_.append(cp)
            for k, (qx, qy) in enumerate(others):
                cp = _remote(_shard_piece(sh[m], shard, c), _piece_of(full[m], kind, shard, chip, c),
                             send.at[6 * m + k], recv.at[6 * m + k], (qx, qy, c))
                cp.start()
                sends.append(cp)
        for t in range(ns):
            cp = pltpu.make_async_copy(sm[t], smf[t].at[:, pl.ds(chip * small_w[t], small_w[t])], loc.at[nb + t])
            cp.start()
            locals_.append(cp)
            for k, (qx, qy) in enumerate(others):
                cp = _remote(sm[t], smf[t].at[:, pl.ds(chip * small_w[t], small_w[t])],
                             send.at[6 * nb + 3 * t + k], recv.at[6 * nb + 3 * t + k], (qx, qy, c))
                cp.start()
                sends.append(cp)
        for m, name in enumerate(BIG):
            kind, shard, _ = geom[name]
            for k, (qx, qy) in enumerate(others):
                got = _piece_of(full[m], kind, shard, 2 * qx + qy, c)
                _remote(got, got, send.at[6 * m + k], recv.at[6 * m + k], (qx, qy, c)).wait_recv()
                cp = _remote(got, got, send.at[6 * m + 3 + k], recv.at[6 * m + 3 + k], sib)
                cp.start()
                sends.append(cp)
        for t in range(ns):
            for k, (qx, qy) in enumerate(others):
                got = smf[t].at[:, pl.ds((2 * qx + qy) * small_w[t], small_w[t])]
                _remote(got, got, send.at[6 * nb + 3 * t + k], recv.at[6 * nb + 3 * t + k], (qx, qy, c)).wait_recv()
        for m, name in enumerate(BIG):
            kind, shard, _ = geom[name]
            for k, (qx, qy) in enumerate(others):
                got = _piece_of(full[m], kind, shard, 2 * qx + qy, 1 - c)
                _remote(got, got, send.at[6 * m + 3 + k], recv.at[6 * m + 3 + k], sib).wait_recv()
        for cp in sends:
            cp.wait_send()
        for cp in locals_:
            cp.wait()

    nsem = 6 * nb + 3 * ns
    out_shape = [jax.ShapeDtypeStruct(_full_shape(geom[n][0], geom[n][1]), BF16) for n in BIG]
    out_shape += [jax.ShapeDtypeStruct((8, N_CHIPS * w), F32) for w in small_w]
    outs = pl.pallas_call(
        body, name="allgather_weights", in_specs=[ANY] * (nb + ns), out_specs=[ANY] * (nb + ns), out_shape=out_shape,
        scratch_shapes=[pltpu.SemaphoreType.DMA((nsem,)), pltpu.SemaphoreType.DMA((nsem,)),
                        pltpu.SemaphoreType.DMA((nb + ns,))],
    )(*[shards[n] for n in BIG], *small_shards)
    return dict(zip(BIG, outs[:nb])), list(outs[nb:])


def _sibling_exchange(grads, geom):
    nb = len(BIG)

    def body(*refs):
        g = refs[:nb]
        own = refs[nb:2 * nb]
        got = refs[2 * nb:3 * nb]
        send, recv, loc = refs[3 * nb:]
        x, y, c = _me()
        sib = (x, y, 1 - c)
        cps, lcs = [], []
        for m, name in enumerate(BIG):
            kind, shard, _ = geom[name]
            for r in range(N_CHIPS):
                cp = _remote(_piece_of(g[m], kind, shard, r, 1 - c), got[m].at[r],
                             send.at[N_CHIPS * m + r], recv.at[N_CHIPS * m + r], sib)
                cp.start()
                cps.append(cp)
                lc = pltpu.make_async_copy(_piece_of(g[m], kind, shard, r, c), own[m].at[r], loc.at[N_CHIPS * m + r])
                lc.start()
                lcs.append(lc)
        for cp in cps:
            cp.wait_recv()
        for cp in cps:
            cp.wait_send()
        for lc in lcs:
            lc.wait()

    shapes = [jax.ShapeDtypeStruct((N_CHIPS,) + geom[n][2], BF16) for n in BIG]
    outs = pl.pallas_call(
        body, name="grad_sibling_exchange", in_specs=[ANY] * nb, out_specs=[ANY] * (2 * nb), out_shape=shapes + shapes,
        scratch_shapes=[pltpu.SemaphoreType.DMA((N_CHIPS * nb,)), pltpu.SemaphoreType.DMA((N_CHIPS * nb,)),
                        pltpu.SemaphoreType.DMA((N_CHIPS * nb,))],
    )(*[grads[n] for n in BIG])
    return outs[:nb], outs[nb:]


def _chip_exchange(chip_sums, geom):
    nb = len(BIG)

    def body(*refs):
        t = refs[:nb]
        got = refs[nb:2 * nb]
        send, recv, loc = refs[2 * nb:]
        x, y, c = _me()
        chip = 2 * x + y
        cps, lcs = [], []
        for m in range(nb):
            for k, (qx, qy) in enumerate(_other_chips(x, y)):
                cp = _remote(t[m].at[2 * qx + qy], got[m].at[k], send.at[3 * m + k], recv.at[3 * m + k], (qx, qy, c))
                cp.start()
                cps.append(cp)
            lc = pltpu.make_async_copy(t[m].at[chip], got[m].at[3], loc.at[m])
            lc.start()
            lcs.append(lc)
        for cp in cps:
            cp.wait_recv()
        for cp in cps:
            cp.wait_send()
        for lc in lcs:
            lc.wait()

    outs = pl.pallas_call(
        body, name="grad_chip_exchange", in_specs=[ANY] * nb, out_specs=[ANY] * nb,
        out_shape=[jax.ShapeDtypeStruct((N_CHIPS,) + geom[n][2], BF16) for n in BIG],
        scratch_shapes=[pltpu.SemaphoreType.DMA((3 * nb,)), pltpu.SemaphoreType.DMA((3 * nb,)),
                        pltpu.SemaphoreType.DMA((nb,))],
    )(*chip_sums)
    return outs


def _sibling_assemble(halves, geom):
    nb = len(BIG)

    def body(*refs):
        h = refs[:nb]
        full = refs[nb:2 * nb]
        send, recv, loc = refs[2 * nb:]
        x, y, c = _me()
        sib = (x, y, 1 - c)
        cps, lcs = [], []
        for m, name in enumerate(BIG):
            shard = geom[name][1]
            cp = _remote(h[m], _shard_piece(full[m], shard, c), send.at[m], recv.at[m], sib)
            cp.start()
            cps.append(cp)
            lc = pltpu.make_async_copy(h[m], _shard_piece(full[m], shard, c), loc.at[m])
            lc.start()
            lcs.append(lc)
        for m, name in enumerate(BIG):
            dst = _shard_piece(full[m], geom[name][1], 1 - c)
            _remote(h[m], dst, send.at[m], recv.at[m], sib).wait_recv()
        for cp in cps:
            cp.wait_send()
        for lc in lcs:
            lc.wait()

    outs = pl.pallas_call(
        body, name="grad_sibling_assemble", in_specs=[ANY] * nb, out_specs=[ANY] * nb,
        out_shape=[jax.ShapeDtypeStruct(geom[n][1], F32) for n in BIG],
        scratch_shapes=[pltpu.SemaphoreType.DMA((nb,)), pltpu.SemaphoreType.DMA((nb,)), pltpu.SemaphoreType.DMA((nb,))],
    )(*halves)
    return outs


def _allreduce_small(part):
    rows = part.shape[0]
    flips = [(a, b, e) for a in (0, 1) for b in (0, 1) for e in (0, 1) if (a, b, e) != (0, 0, 0)]

    def body(p_ref, o_ref, all_ref, send, recv):
        x, y, c = _me()
        me = 4 * x + 2 * y + c
        all_ref[me] = p_ref[...]
        cps = []
        for k, (a, b, e) in enumerate(flips):
            cp = _remote(p_ref, all_ref.at[me], send.at[k], recv.at[k], (x ^ a, y ^ b, c ^ e))
            cp.start()
            cps.append(cp)
        for k, (a, b, e) in enumerate(flips):
            peer = 4 * (x ^ a) + 2 * (y ^ b) + (c ^ e)
            _remote(p_ref, all_ref.at[peer], send.at[k], recv.at[k], (x ^ a, y ^ b, c ^ e)).wait_recv()
        for cp in cps:
            cp.wait_send()
        tot = all_ref[0]
        for dev in range(1, 8):
            tot = tot + all_ref[dev]
        o_ref[...] = tot

    vm = pl.BlockSpec(memory_space=pltpu.VMEM)
    return pl.pallas_call(
        body, name="allreduce_small", in_specs=[vm], out_specs=vm, out_shape=jax.ShapeDtypeStruct((rows, LANES), F32),
        scratch_shapes=[pltpu.VMEM((8, rows, LANES), F32), pltpu.SemaphoreType.DMA((7,)), pltpu.SemaphoreType.DMA((7,))],
    )(part)


def _rows_tile(rows, cols, mult):
    return _pick(rows, max(mult, (1 << 19) // cols // mult * mult), mult)


def _add_pairs(a, b):
    p, r, c = a.shape
    tr = _rows_tile(r, c, 16)

    def body(a_ref, b_ref, o_ref):
        o_ref[...] = (a_ref[...].astype(F32) + b_ref[...].astype(F32)).astype(BF16)

    spec = pl.BlockSpec((None, tr, c), lambda q, i: (q, i, 0))
    return pl.pallas_call(
        body, name="grad_add_sibling", grid=(p, r // tr), in_specs=[spec, spec], out_specs=spec,
        out_shape=jax.ShapeDtypeStruct((p, r, c), BF16), compiler_params=_cp(("arbitrary", "arbitrary"), 32),
    )(a, b)


def _add_four(t):
    _, r, c = t.shape
    tr = _rows_tile(r, c, 16)

    def body(t3, t0, t1, t2, o_ref):
        o_ref[...] = ((t3[...].astype(F32) + t0[...].astype(F32)) + t1[...].astype(F32)) + t2[...].astype(F32)

    spec = lambda q: pl.BlockSpec((None, tr, c), lambda i: (q, i, 0))
    return pl.pallas_call(
        body, name="grad_add_chips", grid=(r // tr,), in_specs=[spec(3), spec(0), spec(1), spec(2)],
        out_specs=pl.BlockSpec((tr, c), lambda i: (i, 0)), out_shape=jax.ShapeDtypeStruct((r, c), F32),
        compiler_params=_cp(("arbitrary",), 32),
    )(t, t, t, t)


def _adamw(w, g, m, v, name):
    r, c = w.shape
    tr = _rows_tile(r, c, 8)

    def body(w_ref, g_ref, m_ref, v_ref, go_ref, d_ref, mo_ref, vo_ref):
        gv = g_ref[...]
        mn = ADAM_B1 * m_ref[...] + (1.0 - ADAM_B1) * gv
        vn = ADAM_B2 * v_ref[...] + (1.0 - ADAM_B2) * (gv * gv)
        m_hat = mn / (1.0 - ADAM_B1 ** ADAM_STEP)
        v_hat = vn / (1.0 - ADAM_B2 ** ADAM_STEP)
        go_ref[...] = gv
        d_ref[...] = -ADAM_LR * (m_hat / (jnp.sqrt(v_hat) + ADAM_EPS) + ADAM_WD * w_ref[...])
        mo_ref[...] = mn
        vo_ref[...] = vn

    spec = pl.BlockSpec((tr, c), lambda i: (i, 0))
    return pl.pallas_call(
        body, name=name, grid=(r // tr,), in_specs=[spec] * 4, out_specs=[spec] * 4,
        out_shape=[jax.ShapeDtypeStruct((r, c), F32)] * 4, compiler_params=_cp(("arbitrary",), 32),
    )(w, g, m, v)


def _pack(vectors, rows):
    flat = jnp.concatenate([v.reshape(-1).astype(F32) for v in vectors])
    return jnp.pad(flat, (0, rows * LANES - flat.shape[0])).reshape(rows, LANES)


def _unpack(packed, shapes):
    flat = packed.reshape(-1)
    out, off = [], 0
    for shp in shapes:
        n = 1
        for t in shp:
            n *= t
        out.append(flat[off:off + n].reshape(shp))
        off += n
    return out


def _rows_for(shapes):
    n = sum(functools.reduce(lambda a, b: a * b, shp, 1) for shp in shapes)
    return -(-n // (8 * LANES)) * 8


def kernel(x, w_in, attn_sinks, short_conv_w, w_out, ln1_g, ln1_b, ffn_w_up, ffn_conv_w, ffn_w_down, ln2_g, ln2_b, loss_target, m_w_in, m_attn_sinks, m_short_conv_w, m_w_out, m_ln1_g, m_ln1_b, m_ffn_w_up, m_ffn_conv_w, m_ffn_w_down, m_ln2_g, m_ln2_b, v_w_in, v_attn_sinks, v_short_conv_w, v_w_out, v_ln1_g, v_ln1_b, v_ffn_w_up, v_ffn_conv_w, v_ffn_w_down, v_ln2_g, v_ln2_b):
    xs, tgt = x[0], loss_target[0]
    s, d = xs.shape
    chip = 2 * lax.axis_index("x") + lax.axis_index("y")

    t_in = lambda a: a[0].T
    w_big = dict(win_t=t_in(w_in), wout=w_out[0], wup=ffn_w_up[0], wdown=ffn_w_down[0])
    m_big = dict(win_t=t_in(m_w_in), wout=m_w_out[0], wup=m_ffn_w_up[0], wdown=m_ffn_w_down[0])
    v_big = dict(win_t=t_in(v_w_in), wout=v_w_out[0], wup=v_ffn_w_up[0], wdown=v_ffn_w_down[0])
    geom = _geom({n: w_big[n].shape for n in BIG})
    pad8 = lambda a: jnp.pad(a[0], ((0, 5), (0, 0)))
    full, (scw8, fcw8) = _allgather_weights({n: w_big[n].astype(BF16) for n in BIG},
                                            [pad8(short_conv_w), pad8(ffn_conv_w)])

    grad_x, g_big, g_small = _local_step(xs, tgt, full["win_t"], full["wout"], full["wup"], full["wdown"], scw8, fcw8,
                                         attn_sinks, ln1_g, ln1_b, ln2_g, ln2_b)

    own, got = _sibling_exchange(g_big, geom)
    chip_sums = [_add_pairs(a, b) for a, b in zip(own, got)]
    gathered = _chip_exchange(chip_sums, geom)
    halves = [_add_four(t) for t in gathered]
    g_shards = dict(zip(BIG, _sibling_assemble(halves, geom)))

    small_names = ("ln1_g", "ln1_b", "ln2_g", "ln2_b", "sinks", "fcw", "scw")
    small_shapes = [g_small[n].shape for n in small_names]
    red = _allreduce_small(_pack([g_small["loss_sq"].reshape(1)] + [g_small[n] for n in small_names],
                                 _rows_for([(1,)] + small_shapes)))
    loss_sq, *gs = _unpack(red, [(1,)] + small_shapes)
    gs = dict(zip(small_names, gs))
    loss = (0.5 / d) * loss_sq[0]
    fw, sw = ffn_conv_w.shape[2], short_conv_w.shape[2]
    gs["fcw"] = lax.dynamic_slice_in_dim(gs["fcw"], chip * fw, fw, axis=1)
    gs["scw"] = lax.dynamic_slice_in_dim(gs["scw"], chip * sw, sw, axis=1)

    upd = {n: _adamw(w_big[n], g_shards[n], m_big[n], v_big[n], "adamw_" + n) for n in BIG}
    upd["win_t"] = tuple(a.T for a in upd["win_t"])
    sm_w = dict(ln1_g=ln1_g[0], ln1_b=ln1_b[0], ln2_g=ln2_g[0], ln2_b=ln2_b[0], sinks=attn_sinks[0],
                fcw=ffn_conv_w[0], scw=short_conv_w[0])
    sm_m = dict(ln1_g=m_ln1_g[0], ln1_b=m_ln1_b[0], ln2_g=m_ln2_g[0], ln2_b=m_ln2_b[0], sinks=m_attn_sinks[0],
                fcw=m_ffn_conv_w[0], scw=m_short_conv_w[0])
    sm_v = dict(ln1_g=v_ln1_g[0], ln1_b=v_ln1_b[0], ln2_g=v_ln2_g[0], ln2_b=v_ln2_b[0], sinks=v_attn_sinks[0],
                fcw=v_ffn_conv_w[0], scw=v_short_conv_w[0])
    shapes = [sm_w[n].shape for n in small_names]
    rows = _rows_for(shapes)
    packed = [_pack([t[n] for n in small_names], rows) for t in (sm_w, gs, sm_m, sm_v)]
    sm_out = [dict(zip(small_names, _unpack(a, shapes))) for a in _adamw(*packed, "adamw_small")]

    def leaf(kind, name):
        if name in ("w_in", "w_out", "ffn_w_up", "ffn_w_down"):
            key = dict(w_in="win_t", w_out="wout", ffn_w_up="wup", ffn_w_down="wdown")[name]
            return upd[key][kind][None]
        key = dict(attn_sinks="sinks", short_conv_w="scw", ffn_conv_w="fcw").get(name, name)
        return sm_out[kind][key][None]

    order = ("w_in", "attn_sinks", "short_conv_w", "w_out", "ln1_g", "ln1_b", "ffn_w_up", "ffn_conv_w", "ffn_w_down",
             "ln2_g", "ln2_b")
    outs = [loss, grad_x[None]]
    for kind in range(4):
        outs += [leaf(kind, n) for n in order]
    return tuple(outs)
```

```python
import functools

import jax
import jax.numpy as jnp
from jax import lax
from jax.experimental import pallas as pl
from jax.experimental.pallas import tpu as pltpu

F32 = jnp.float32
BF16 = jnp.bfloat16
MESH = pl.DeviceIdType.MESH
ANY = pl.BlockSpec(memory_space=pl.ANY)

HEAD_DIM = 64
N_Q_HEADS = 16
N_KV_HEADS = 2
ATTN_WIDTH = N_Q_HEADS * HEAD_DIM
KV_WIDTH = N_KV_HEADS * HEAD_DIM
BLOCK = 128
ROPE_THETA = 10000.0
LN_EPS = 1e-5
ALPHA = 2.0 ** 0.25
NEG_INF = -1e30
ADAM_LR, ADAM_B1, ADAM_B2, ADAM_EPS, ADAM_WD, ADAM_STEP = 0.001, 0.9, 0.999, 1e-08, 0.01, 10
N_CHIPS = 4
LANES = 128
SLAB = 128


def _cp(sem, vmem_mb):
    return pltpu.CompilerParams(dimension_semantics=sem, vmem_limit_bytes=vmem_mb << 20)


def _matmul(a, b, *, mode, m, n, k, tm, tn, tk, out_dtype, name, vmem_mb, a_spec=None, b_spec=None,
            res=None, alpha=1.0):
    nj, ni, nk = n // tn, m // tm, k // tk
    assert nj * tn == n and ni * tm == m and nk * tk == k, (name, m, n, k, tm, tn, tk)
    if mode == "nn":
        dims = ((1,), (0,))
        a_spec = a_spec or pl.BlockSpec((tm, tk), lambda j, i, kk: (i, kk))
        b_spec = b_spec or pl.BlockSpec((tk, tn), lambda j, i, kk: (kk, j))
    elif mode == "nt":
        dims = ((1,), (1,))
        a_spec = a_spec or pl.BlockSpec((tm, tk), lambda j, i, kk: (i, kk))
        b_spec = b_spec or pl.BlockSpec((tn, tk), lambda j, i, kk: (j, kk))
    else:
        dims = ((0,), (0,))
        a_spec = a_spec or pl.BlockSpec((tk, tm), lambda j, i, kk: (kk, i))
        b_spec = b_spec or pl.BlockSpec((tk, tn), lambda j, i, kk: (kk, j))
    has_res = res is not None

    def body(*refs):
        a_ref, b_ref = refs[0], refs[1]
        res_ref = refs[2] if has_res else None
        o_ref = refs[2 + has_res]
        part = lax.dot_general(a_ref[...], b_ref[...], (dims, ((), ())), preferred_element_type=F32)

        def finish(acc):
            if has_res:
                acc = acc + alpha * res_ref[...]
            o_ref[...] = acc.astype(o_ref.dtype)

        if nk == 1:
            finish(part)
        else:
            acc_ref = refs[3 + has_res]
            kk = pl.program_id(2)

            @pl.when(kk == 0)
            def _():
                acc_ref[...] = part

            @pl.when(kk > 0)
            def _():
                acc_ref[...] += part

            @pl.when(kk == nk - 1)
            def _():
                finish(acc_ref[...])

    in_specs = [a_spec, b_spec]
    args = [a, b]
    if has_res:
        in_specs.append(pl.BlockSpec((tm, tn), lambda j, i, kk: (i, j)))
        args.append(res)
    return pl.pallas_call(
        body, name=name, grid=(nj, ni, nk), in_specs=in_specs,
        out_specs=pl.BlockSpec((tm, tn), lambda j, i, kk: (i, j)),
        out_shape=jax.ShapeDtypeStruct((m, n), out_dtype),
        scratch_shapes=[pltpu.VMEM((tm, tn), F32)] if nk > 1 else [],
        compiler_params=_cp(("arbitrary", "arbitrary", "arbitrary"), vmem_mb),
    )(*args)


def _pick(total, want, mult):
    if total <= want:
        return total
    for t in range(want, 0, -1):
        if total % t == 0 and t % mult == 0:
            return t
    return total


def _rope_tables(s):
    half = HEAD_DIM // 2
    inv_freq = ROPE_THETA ** (-jnp.arange(half, dtype=F32) / half)
    ang = jnp.arange(s, dtype=F32)[:, None] * inv_freq[None, :]
    cos = jnp.tile(jnp.cos(ang), (1, LANES // half))
    sin = jnp.tile(jnp.concatenate([-jnp.sin(ang), jnp.sin(ang)], axis=1), (1, LANES // HEAD_DIM))
    return cos, sin


def _rope(x, cos, sin, lo):
    partner = jnp.where(lo, pltpu.roll(x, LANES - HEAD_DIM // 2, 1), pltpu.roll(x, HEAD_DIM // 2, 1))
    return x * cos + partner * sin


def _dot(a, b, dims):
    return lax.dot_general(a, b, (dims, ((), ())), preferred_element_type=F32)


NN, NT, TN = ((1,), (0,)), ((1,), (1,)), ((0,), (0,))


def _kv_variants(t, head_lo):
    r = pltpu.roll(t, HEAD_DIM, 1)
    zero = jnp.zeros_like(t)
    a = (jnp.where(head_lo, t, zero).astype(BF16), jnp.where(head_lo, r, zero).astype(BF16))
    b = (jnp.where(head_lo, zero, r).astype(BF16), jnp.where(head_lo, zero, t).astype(BF16))
    return a, b


def _attn_probs(qp, ka, kb, valid, sink_a, sink_b):
    out = []
    for kk, sink in ((ka, sink_a), (kb, sink_b)):
        s = jnp.where(valid, _dot(qp, kk, NT), NEG_INF)
        mx = jnp.maximum(jnp.max(s, axis=1, keepdims=True), sink)
        e = jnp.exp(s - mx)
        es = jnp.exp(sink - mx)
        inv = 1.0 / (jnp.sum(e, axis=1, keepdims=True) + es)
        out.append((e * inv, es * inv))
    return out


def _attn_common(i, q_ref, k_ref, v_ref, kp_ref, vp_ref, cos_ref, sin_ref, cosp_ref, sinp_ref):
    lane = lax.broadcasted_iota(jnp.int32, (1, LANES), 1)
    lo = (lane % HEAD_DIM) < (HEAD_DIM // 2)
    head_lo = lane < HEAD_DIM
    cos, sin = cos_ref[...], sin_ref[...]
    kc = _rope(k_ref[...].astype(F32), cos, sin, lo)
    kp = _rope(kp_ref[...].astype(F32), cosp_ref[...], sinp_ref[...], lo)
    kext = jnp.concatenate([kp, kc], axis=0)
    vext = jnp.concatenate([vp_ref[...].astype(F32), v_ref[...].astype(F32)], axis=0)
    ka, kb = _kv_variants(kext, head_lo)
    va, vb = _kv_variants(vext, head_lo)
    qi = lax.broadcasted_iota(jnp.int32, (BLOCK, 1), 0)
    kj = lax.broadcasted_iota(jnp.int32, (1, 2 * BLOCK), 1)
    valid = (kj > qi) & (kj <= qi + BLOCK) & ((kj >= BLOCK) | (i > 0))
    return lo, head_lo, cos, sin, ka, kb, va, vb, valid


def _attn_fwd(proj, sinks, cos, sin, s):
    nb = s // BLOCK
    kcol, vcol = ATTN_WIDTH // LANES, ATTN_WIDTH // LANES + 1

    def body(q_ref, k_ref, v_ref, kp_ref, vp_ref, cos_ref, sin_ref, cosp_ref, sinp_ref, sink_ref, o_ref):
        i = pl.program_id(0)
        lo, head_lo, cs, sn, ka, kb, va, vb, valid = _attn_common(
            i, q_ref, k_ref, v_ref, kp_ref, vp_ref, cos_ref, sin_ref, cosp_ref, sinp_ref)
        for p in range(N_Q_HEADS // 2):
            j = p // (N_Q_HEADS // 2 // N_KV_HEADS)
            qp = (_rope(q_ref[:, p * LANES:(p + 1) * LANES].astype(F32), cs, sn, lo) * HEAD_DIM ** -0.5).astype(BF16)
            (pa, _), (pb, _) = _attn_probs(qp, ka[j], kb[j], valid, sink_ref[0, 2 * p], sink_ref[0, 2 * p + 1])
            o = _dot(pa.astype(BF16), va[j], NN) + _dot(pb.astype(BF16), vb[j], NN)
            o_ref[:, p * LANES:(p + 1) * LANES] = o.astype(BF16)

    prev = lambda i: (jnp.maximum(i - 1, 0), 0)
    return pl.pallas_call(
        body, name="attn_fwd", grid=(nb,),
        in_specs=[pl.BlockSpec((BLOCK, ATTN_WIDTH), lambda i: (i, 0)),
                  pl.BlockSpec((BLOCK, LANES), lambda i: (i, kcol)),
                  pl.BlockSpec((BLOCK, LANES), lambda i: (i, vcol)),
                  pl.BlockSpec((BLOCK, LANES), lambda i: (jnp.maximum(i - 1, 0), kcol)),
                  pl.BlockSpec((BLOCK, LANES), lambda i: (jnp.maximum(i - 1, 0), vcol)),
                  pl.BlockSpec((BLOCK, LANES), lambda i: (i, 0)),
                  pl.BlockSpec((BLOCK, LANES), lambda i: (i, 0)),
                  pl.BlockSpec((BLOCK, LANES), prev),
                  pl.BlockSpec((BLOCK, LANES), prev),
                  pl.BlockSpec(memory_space=pltpu.SMEM)],
        out_specs=pl.BlockSpec((BLOCK, ATTN_WIDTH), lambda i: (i, 0)),
        out_shape=jax.ShapeDtypeStruct((s, ATTN_WIDTH), BF16),
        compiler_params=_cp(("arbitrary",), 32),
    )(proj, proj, proj, proj, proj, cos, sin, cos, sin, sinks)


def _attn_bwd(proj, dmix, sinks, cos, sin, s):
    nb = s // BLOCK
    kcol, vcol = ATTN_WIDTH // LANES, ATTN_WIDTH // LANES + 1
    pairs_per_kv = N_Q_HEADS // 2 // N_KV_HEADS

    def body(q_ref, k_ref, v_ref, kp_ref, vp_ref, cos_ref, sin_ref, cosp_ref, sinp_ref, sink_ref, do_ref,
             dq_ref, dk_ref, dv_ref, dsink_ref, ck_ref, cv_ref):
        g = pl.program_id(0)
        i = nb - 1 - g

        @pl.when(g == 0)
        def _():
            ck_ref[...] = jnp.zeros_like(ck_ref)
            cv_ref[...] = jnp.zeros_like(cv_ref)
            dsink_ref[...] = jnp.zeros_like(dsink_ref)

        lo, head_lo, cs, sn, ka, kb, va, vb, valid = _attn_common(
            i, q_ref, k_ref, v_ref, kp_ref, vp_ref, cos_ref, sin_ref, cosp_ref, sinp_ref)
        lane = lax.broadcasted_iota(jnp.int32, (1, LANES), 1)
        dk_j = [jnp.zeros((2 * BLOCK, LANES), F32) for _ in range(N_KV_HEADS)]
        dv_j = [jnp.zeros((2 * BLOCK, LANES), F32) for _ in range(N_KV_HEADS)]
        dsink = jnp.zeros((1, LANES), F32)
        for p in range(N_Q_HEADS // 2):
            j = p // pairs_per_kv
            qp = (_rope(q_ref[:, p * LANES:(p + 1) * LANES].astype(F32), cs, sn, lo) * HEAD_DIM ** -0.5).astype(BF16)
            probs = _attn_probs(qp, ka[j], kb[j], valid, sink_ref[0, 2 * p], sink_ref[0, 2 * p + 1])
            do = do_ref[:, p * LANES:(p + 1) * LANES]
            dq_r = jnp.zeros((BLOCK, LANES), F32)
            dkc, dvc = [], []
            for hh, ((pr, ps), kk, vv) in enumerate(zip(probs, (ka[j], kb[j]), (va[j], vb[j]))):
                dp = _dot(do, vv, NT)
                delta = jnp.sum(pr * dp, axis=1, keepdims=True)
                ds = (pr * (dp - delta)).astype(BF16)
                dsink = dsink + jnp.where(lane == 2 * p + hh, -jnp.sum(ps * delta, axis=0, keepdims=True), 0.0)
                dq_r = dq_r + _dot(ds, kk, NN)
                dkc.append(_dot(ds, qp, TN))
                dvc.append(_dot(pr.astype(BF16), do, TN))
            dk_j[j] = dk_j[j] + jnp.where(head_lo, dkc[0], dkc[1])
            dv_j[j] = dv_j[j] + jnp.where(head_lo, dvc[0], dvc[1])
            dq = _rope(dq_r * HEAD_DIM ** -0.5, cs, -sn, lo)
            dq_ref[:, p * LANES:(p + 1) * LANES] = dq.astype(BF16)
        tot_k = [t + pltpu.roll(t, HEAD_DIM, 1) for t in dk_j]
        tot_v = [t + pltpu.roll(t, HEAD_DIM, 1) for t in dv_j]
        dkext = jnp.where(head_lo, tot_k[0], tot_k[1])
        dvext = jnp.where(head_lo, tot_v[0], tot_v[1])
        dk_r = dkext[BLOCK:] + ck_ref[...]
        dk_ref[...] = _rope(dk_r, cs, -sn, lo).astype(BF16)
        dv_ref[...] = (dvext[BLOCK:] + cv_ref[...]).astype(BF16)
        ck_ref[...] = dkext[:BLOCK]
        cv_ref[...] = dvext[:BLOCK]
        dsink_ref[0:1, :] += dsink

    cur = lambda col: (lambda g: (nb - 1 - g, col))
    prv = lambda col: (lambda g: (jnp.maximum(nb - 2 - g, 0), col))
    blk = lambda w, f: pl.BlockSpec((BLOCK, w), f)
    return pl.pallas_call(
        body, name="attn_bwd", grid=(nb,),
        in_specs=[blk(ATTN_WIDTH, cur(0)), blk(LANES, cur(kcol)), blk(LANES, cur(vcol)),
                  blk(LANES, prv(kcol)), blk(LANES, prv(vcol)),
                  blk(LANES, cur(0)), blk(LANES, cur(0)), blk(LANES, prv(0)), blk(LANES, prv(0)),
                  pl.BlockSpec(memory_space=pltpu.SMEM),
                  blk(ATTN_WIDTH, cur(0))],
        out_specs=[blk(ATTN_WIDTH, cur(0)), blk(LANES, cur(0)), blk(LANES, cur(0)),
                   pl.BlockSpec((8, LANES), lambda g: (0, 0))],
        out_shape=[jax.ShapeDtypeStruct((s, ATTN_WIDTH), BF16), jax.ShapeDtypeStruct((s, LANES), BF16),
                   jax.ShapeDtypeStruct((s, LANES), BF16), jax.ShapeDtypeStruct((8, LANES), F32)],
        scratch_shapes=[pltpu.VMEM((BLOCK, LANES), F32), pltpu.VMEM((BLOCK, LANES), F32)],
        compiler_params=_cp(("arbitrary",), 32),
    )(proj, proj, proj, proj, proj, cos, sin, cos, sin, sinks, dmix)


def _causal_conv(x, prev8, w):
    r = x.shape[0]
    row = lax.broadcasted_iota(jnp.int32, (r, 1), 0)
    s1 = jnp.where(row == 0, prev8[7:8], pltpu.roll(x, 1, 0))
    s2 = jnp.where(row == 0, prev8[6:7], jnp.where(row == 1, prev8[7:8], pltpu.roll(x, 2, 0)))
    return w[0:1] * s2 + w[1:2] * s1 + w[2:3] * x, s1, s2


def _conv_bwd(dy, x, s1, s2, w, next8):
    r = x.shape[0]
    row = lax.broadcasted_iota(jnp.int32, (r, 1), 0)
    n1 = jnp.where(row == r - 1, next8[0:1], pltpu.roll(dy, r - 1, 0))
    n2 = jnp.where(row == r - 2, next8[0:1], jnp.where(row == r - 1, next8[1:2], pltpu.roll(dy, r - 2, 0)))
    dx = w[2:3] * dy + w[1:2] * n1 + w[0:1] * n2
    dws = [jnp.sum(dy * t, axis=0, keepdims=True) for t in (s2, s1, x)]
    return dx, dws


CONV_COLS = 256


def _convmix_cols(d):
    conv_w = d - ATTN_WIDTH
    base = (ATTN_WIDTH + 2 * KV_WIDTH) // CONV_COLS
    step = conv_w // CONV_COLS
    return base, base + step, base + 2 * step, step


def _convmix_fwd(proj, scw8, s, d):
    gb0, gc0, h0, ncb = _convmix_cols(d)
    tr = _pick(s, 1024, 16)
    ni = s // tr

    def body(gb_ref, gc_ref, h_ref, w_ref, o_ref, carry_ref):
        @pl.when(pl.program_id(1) == 0)
        def _():
            carry_ref[...] = jnp.zeros_like(carry_ref)

        gch = gc_ref[...].astype(F32) * h_ref[...].astype(F32)
        cc, _, _ = _causal_conv(gch, carry_ref[...], w_ref[...])
        o_ref[...] = (gb_ref[...].astype(F32) * cc).astype(BF16)
        carry_ref[...] = gch[tr - 8:]

    spec = lambda c0: pl.BlockSpec((tr, CONV_COLS), lambda j, i: (i, c0 + j))
    return pl.pallas_call(
        body, name="convmix_fwd", grid=(ncb, ni),
        in_specs=[spec(gb0), spec(gc0), spec(h0), pl.BlockSpec((8, CONV_COLS), lambda j, i: (0, j))],
        out_specs=pl.BlockSpec((tr, CONV_COLS), lambda j, i: (i, j)),
        out_shape=jax.ShapeDtypeStruct((s, d - ATTN_WIDTH), BF16),
        scratch_shapes=[pltpu.VMEM((8, CONV_COLS), F32)],
        compiler_params=_cp(("arbitrary", "arbitrary"), 32),
    )(proj, proj, proj, scw8)


def _convmix_bwd(proj, dmix, scw8, s, d):
    gb0, gc0, h0, ncb = _convmix_cols(d)
    tr = _pick(s, 1024, 16)
    ni = s // tr
    dc0 = ATTN_WIDTH // CONV_COLS

    def body(dc_ref, gb_ref, gc_ref, h_ref, gcp_ref, hp_ref, w_ref, d3_ref, dw_ref, nxt_ref):
        g = pl.program_id(1)
        i = ni - 1 - g

        @pl.when(g == 0)
        def _():
            nxt_ref[...] = jnp.zeros_like(nxt_ref)
            dw_ref[...] = jnp.zeros_like(dw_ref)

        w = w_ref[...]
        gb, gc, h = gb_ref[...].astype(F32), gc_ref[...].astype(F32), h_ref[...].astype(F32)
        gch = gc * h
        prev8 = (gcp_ref[...].astype(F32) * hp_ref[...].astype(F32))[8:16] * (i > 0).astype(F32)
        cc, s1, s2 = _causal_conv(gch, prev8, w)
        dc = dc_ref[...].astype(F32)
        dcc = dc * gb
        dgch, dws = _conv_bwd(dcc, gch, s1, s2, w, nxt_ref[...])
        d3_ref[0] = (dc * cc).astype(BF16)
        d3_ref[1] = (dgch * h).astype(BF16)
        d3_ref[2] = (dgch * gc).astype(BF16)
        for t in range(3):
            dw_ref[t:t + 1, :] += dws[t]
        nxt_ref[...] = dcc[0:8]

    cur = lambda c0: pl.BlockSpec((tr, CONV_COLS), lambda j, g: (ni - 1 - g, c0 + j))
    prv = lambda c0: pl.BlockSpec((16, CONV_COLS), lambda j, g: (jnp.maximum((ni - 1 - g) * (tr // 16) - 1, 0), c0 + j))
    return pl.pallas_call(
        body, name="convmix_bwd", grid=(ncb, ni),
        in_specs=[cur(dc0), cur(gb0), cur(gc0), cur(h0), prv(gc0), prv(h0),
                  pl.BlockSpec((8, CONV_COLS), lambda j, g: (0, j))],
        out_specs=[pl.BlockSpec((3, tr, CONV_COLS), lambda j, g: (0, ni - 1 - g, j)),
                   pl.BlockSpec((8, CONV_COLS), lambda j, g: (0, j))],
        out_shape=[jax.ShapeDtypeStruct((3, s, d - ATTN_WIDTH), BF16), jax.ShapeDtypeStruct((8, d - ATTN_WIDTH), F32)],
        scratch_shapes=[pltpu.VMEM((8, CONV_COLS), F32)],
        compiler_params=_cp(("arbitrary", "arbitrary"), 32),
    )(dmix, proj, proj, proj, proj, proj, scw8)


def _ln_fwd(z):
    mu = jnp.mean(z, axis=-1, keepdims=True)
    zc = z - mu
    var = jnp.mean(zc * zc, axis=-1, keepdims=True)
    rstd = lax.rsqrt(var + LN_EPS)
    return zc * rstd, rstd


def _ln_bwd(dout, xh, rstd, g):
    dxh = dout * g
    c1 = jnp.mean(dxh, axis=-1, keepdims=True)
    c2 = jnp.mean(dxh * xh, axis=-1, keepdims=True)
    dz = rstd * (dxh - c1 - xh * c2)
    return dz, jnp.sum(dout * xh, axis=0, keepdims=True), jnp.sum(dout, axis=0, keepdims=True)


def _outproj_ln1(attn, conv, wout, x, g1, b1, s, d):
    tm = _pick(s, 256, 16)
    ka = attn.shape[1]

    def body(a_ref, c_ref, wt_ref, wb_ref, x_ref, g_ref, b_ref, x1_ref, x1b_ref, xh_ref, rs_ref):
        y = _dot(a_ref[...], wt_ref[...], NN) + _dot(c_ref[...], wb_ref[...], NN)
        xh, rstd = _ln_fwd(ALPHA * x_ref[...] + y)
        x1 = xh * g_ref[...] + b_ref[...]
        x1_ref[...] = x1
        x1b_ref[...] = x1.astype(BF16)
        xh_ref[...] = xh.astype(BF16)
        rs_ref[...] = rstd

    row = lambda w: pl.BlockSpec((tm, w), lambda i: (i, 0))
    vec = pl.BlockSpec((1, d), lambda i: (0, 0))
    return pl.pallas_call(
        body, name="outproj_ln1", grid=(s // tm,),
        in_specs=[row(ka), row(d - ka), pl.BlockSpec((ka, d), lambda i: (0, 0)),
                  pl.BlockSpec((d - ka, d), lambda i: (ka // (d - ka), 0)), row(d), vec, vec],
        out_specs=[row(d), row(d), row(d), row(1)],
        out_shape=[jax.ShapeDtypeStruct((s, d), F32), jax.ShapeDtypeStruct((s, d), BF16),
                   jax.ShapeDtypeStruct((s, d), BF16), jax.ShapeDtypeStruct((s, 1), F32)],
        compiler_params=_cp(("arbitrary",), 48),
    )(attn, conv, wout, wout, x, g1, b1)


def _ffn_up(x1b, wup, fcw8, s, d, dff):
    tm = _pick(s, 1024, 16)
    tn = _pick(dff, 512, LANES)
    nj, ni = dff // tn, s // tm

    def body(x_ref, wa_ref, wg_ref, ca_ref, cg_ref, u_ref, h_ref, carry_ref):
        @pl.when(pl.program_id(1) == 0)
        def _():
            carry_ref[...] = jnp.zeros_like(carry_ref)

        xa = x_ref[...]
        ys = []
        for part, (w_ref, c_ref) in enumerate(((wa_ref, ca_ref), (wg_ref, cg_ref))):
            ub = _dot(xa, w_ref[...], NN).astype(BF16)
            u_ref[part] = ub
            u = ub.astype(F32)
            y, _, _ = _causal_conv(u, carry_ref[part], c_ref[...])
            carry_ref[part] = u[tm - 8:]
            ys.append(y)
        a2, g2 = ys
        sig = 1.0 / (1.0 + jnp.exp(-a2))
        h_ref[...] = (a2 * sig * g2).astype(BF16)

    return pl.pallas_call(
        body, name="ffn_up", grid=(nj, ni),
        in_specs=[pl.BlockSpec((tm, d), lambda j, i: (i, 0)),
                  pl.BlockSpec((d, tn), lambda j, i: (0, j)),
                  pl.BlockSpec((d, tn), lambda j, i: (0, j + nj)),
                  pl.BlockSpec((8, tn), lambda j, i: (0, j)),
                  pl.BlockSpec((8, tn), lambda j, i: (0, j + nj))],
        out_specs=[pl.BlockSpec((2, tm, tn), lambda j, i: (0, i, j)),
                   pl.BlockSpec((tm, tn), lambda j, i: (i, j))],
        out_shape=[jax.ShapeDtypeStruct((2, s, dff), BF16), jax.ShapeDtypeStruct((s, dff), BF16)],
        scratch_shapes=[pltpu.VMEM((2, 8, tn), F32)],
        compiler_params=_cp(("arbitrary", "arbitrary"), 48),
    )(x1b, wup, wup, fcw8, fcw8)


def _ffn_mid_bwd(dh, u3, fcw8, s, dff):
    tm = _pick(s, 1024, 16)
    tn = _pick(dff, 512, LANES)
    nj, ni = dff // tn, s // tm

    def body(dh_ref, u_ref, up_ref, ca_ref, cg_ref, du_ref, dw_ref, nxt_ref):
        g = pl.program_id(1)
        i = ni - 1 - g

        @pl.when(g == 0)
        def _():
            nxt_ref[...] = jnp.zeros_like(nxt_ref)
            dw_ref[...] = jnp.zeros_like(dw_ref)

        has_prev = (i > 0).astype(F32)
        saved = []
        for part, c_ref in enumerate((ca_ref, cg_ref)):
            u = u_ref[part].astype(F32)
            prev8 = up_ref[part].astype(F32)[8:16] * has_prev
            y, s1, s2 = _causal_conv(u, prev8, c_ref[...])
            saved.append((u, s1, s2, y))
        a2, g2 = saved[0][3], saved[1][3]
        sig = 1.0 / (1.0 + jnp.exp(-a2))
        silu = a2 * sig
        dhv = dh_ref[...].astype(F32)
        dys = (dhv * g2 * (sig * (1.0 + a2 * (1.0 - sig))), dhv * silu)
        for part, (c_ref, dy) in enumerate(zip((ca_ref, cg_ref), dys)):
            u, s1, s2, _ = saved[part]
            dx, dws = _conv_bwd(dy, u, s1, s2, c_ref[...], nxt_ref[part])
            du_ref[part] = dx.astype(BF16)
            for t in range(3):
                dw_ref[part, t:t + 1, :] += dws[t]
            nxt_ref[part] = dy[0:8]

    return pl.pallas_call(
        body, name="ffn_mid_bwd", grid=(nj, ni),
        in_specs=[pl.BlockSpec((tm, tn), lambda j, g: (ni - 1 - g, j)),
                  pl.BlockSpec((2, tm, tn), lambda j, g: (0, ni - 1 - g, j)),
                  pl.BlockSpec((2, 16, tn), lambda j, g: (0, jnp.maximum((ni - 1 - g) * (tm // 16) - 1, 0), j)),
                  pl.BlockSpec((8, tn), lambda j, g: (0, j)),
                  pl.BlockSpec((8, tn), lambda j, g: (0, j + nj))],
        out_specs=[pl.BlockSpec((2, tm, tn), lambda j, g: (0, ni - 1 - g, j)),
                   pl.BlockSpec((2, 8, tn), lambda j, g: (0, 0, j))],
        out_shape=[jax.ShapeDtypeStruct((2, s, dff), BF16), jax.ShapeDtypeStruct((2, 8, dff), F32)],
        scratch_shapes=[pltpu.VMEM((2, 8, tn), F32)],
        compiler_params=_cp(("arbitrary", "arbitrary"), 56),
    )(dh, u3, u3, fcw8, fcw8)


def _ffn_down_loss(hmid, wdown, x1, target, g2, b2, s, d, dff):
    tm = _pick(s, 512, SLAB)
    tk = _pick(dff, 1408, LANES)
    ni, nk = s // tm, dff // tk
    slab = min(SLAB, tm)

    def body(h_ref, w_ref, x1_ref, t_ref, g_ref, b_ref, dz_ref, dzb_ref, st_ref, acc_ref):
        i, kk = pl.program_id(0), pl.program_id(1)

        @pl.when((i == 0) & (kk == 0))
        def _():
            st_ref[...] = jnp.zeros_like(st_ref)

        part = _dot(h_ref[...], w_ref[...], NN)

        @pl.when(kk == 0)
        def _():
            acc_ref[...] = part

        @pl.when(kk > 0)
        def _():
            acc_ref[...] += part

        @pl.when(kk == nk - 1)
        def _():
            g, b = g_ref[...], b_ref[...]

            def one(sl, carry):
                rows = pl.ds(pl.multiple_of(sl * slab, slab), slab)
                xh, rstd = _ln_fwd(ALPHA * x1_ref[rows, :] + acc_ref[rows, :])
                diff = xh * g + b - t_ref[rows, :]
                sq = jnp.sum(jnp.sum(diff * diff, axis=1, keepdims=True), axis=0, keepdims=True)
                dz, dg, db = _ln_bwd(diff * (1.0 / d), xh, rstd, g)
                dz_ref[rows, :] = dz
                dzb_ref[rows, :] = dz.astype(BF16)
                st_ref[0:1, :] += dg
                st_ref[1:2, :] += db
                st_ref[2:3, :] += sq
                return carry

            lax.fori_loop(0, tm // slab, one, 0)

    row = pl.BlockSpec((tm, d), lambda i, kk: (i, 0))
    vec = pl.BlockSpec((1, d), lambda i, kk: (0, 0))
    return pl.pallas_call(
        body, name="ffn_down_loss", grid=(ni, nk),
        in_specs=[pl.BlockSpec((tm, tk), lambda i, kk: (i, kk)), pl.BlockSpec((tk, d), lambda i, kk: (kk, 0)),
                  row, row, vec, vec],
        out_specs=[row, row, pl.BlockSpec((8, d), lambda i, kk: (0, 0))],
        out_shape=[jax.ShapeDtypeStruct((s, d), F32), jax.ShapeDtypeStruct((s, d), BF16),
                   jax.ShapeDtypeStruct((8, d), F32)],
        scratch_shapes=[pltpu.VMEM((tm, d), F32)],
        compiler_params=_cp(("arbitrary", "arbitrary"), 56),
    )(hmid, wdown, x1, target, g2, b2)


def _ffn_dx_ln1_bwd(du3, wup, dz2, xh1, rstd1, g1, s, d, dff):
    tm = _pick(s, 512, SLAB)
    tk = _pick(dff, 1408, LANES)
    nkh = dff // tk
    ni, nk = s // tm, 2 * nkh
    slab = min(SLAB, tm)

    def body(a_ref, w_ref, dz2_ref, xh_ref, rs_ref, g_ref, dz_ref, dzb_ref, st_ref, acc_ref):
        i, kk = pl.program_id(0), pl.program_id(1)

        @pl.when((i == 0) & (kk == 0))
        def _():
            st_ref[...] = jnp.zeros_like(st_ref)

        part = _dot(a_ref[...], w_ref[...], NT)

        @pl.when(kk == 0)
        def _():
            acc_ref[...] = part

        @pl.when(kk > 0)
        def _():
            acc_ref[...] += part

        @pl.when(kk == nk - 1)
        def _():
            g = g_ref[...]

            def one(sl, carry):
                rows = pl.ds(pl.multiple_of(sl * slab, slab), slab)
                dx1 = ALPHA * dz2_ref[rows, :] + acc_ref[rows, :]
                dz, dg, db = _ln_bwd(dx1, xh_ref[rows, :].astype(F32), rs_ref[rows, :], g)
                dz_ref[rows, :] = dz
                dzb_ref[rows, :] = dz.astype(BF16)
                st_ref[0:1, :] += dg
                st_ref[1:2, :] += db
                return carry

            lax.fori_loop(0, tm // slab, one, 0)

    row = pl.BlockSpec((tm, d), lambda i, kk: (i, 0))
    return pl.pallas_call(
        body, name="ffn_dx_ln1_bwd", grid=(ni, nk),
        in_specs=[pl.BlockSpec((None, tm, tk), lambda i, kk: (kk // nkh, i, kk % nkh)),
                  pl.BlockSpec((d, tk), lambda i, kk: (0, kk)),
                  row, row, pl.BlockSpec((tm, 1), lambda i, kk: (i, 0)), pl.BlockSpec((1, d), lambda i, kk: (0, 0))],
        out_specs=[row, row, pl.BlockSpec((8, d), lambda i, kk: (0, 0))],
        out_shape=[jax.ShapeDtypeStruct((s, d), F32), jax.ShapeDtypeStruct((s, d), BF16),
                   jax.ShapeDtypeStruct((8, d), F32)],
        scratch_shapes=[pltpu.VMEM((tm, d), F32)],
        compiler_params=_cp(("arbitrary", "arbitrary"), 56),
    )(du3, wup, dz2, xh1, rstd1, g1)


def _local_step(x, target, win_t, wout, wup, wdown, scw8, fcw8, sinks, ln1_g, ln1_b, ln2_g, ln2_b):
    s, d = x.shape
    dff = wdown.shape[0]
    n_in = win_t.shape[0]
    xb = x.astype(BF16)
    cos, sin = _rope_tables(s)

    proj = _matmul(xb, win_t, mode="nt", m=s, n=n_in, k=d, tm=_pick(s, 512, 16), tn=_pick(n_in, 2176, LANES), tk=d,
                   out_dtype=BF16, name="in_proj", vmem_mb=48)
    attn = _attn_fwd(proj, sinks, cos, sin, s)
    conv = _convmix_fwd(proj, scw8, s, d)
    x1, x1b, xh1, rstd1 = _outproj_ln1(attn, conv, wout, x, ln1_g, ln1_b, s, d)
    u3, hmid = _ffn_up(x1b, wup, fcw8, s, d, dff)
    dz2, dz2b, st2 = _ffn_down_loss(hmid, wdown, x1, target, ln2_g, ln2_b, s, d, dff)

    ts = _pick(s, 512, 16)
    g_wdown = _matmul(hmid, dz2b, mode="tn", m=dff, n=d, k=s, tm=_pick(dff, 1408, LANES), tn=d, tk=ts,
                      out_dtype=BF16, name="grad_w_down", vmem_mb=48)
    dh = _matmul(dz2b, wdown, mode="nt", m=s, n=dff, k=d, tm=_pick(s, 1024, 16), tn=_pick(dff, 1408, LANES), tk=d,
                 out_dtype=BF16, name="ffn_dh", vmem_mb=48)
    du3, dfcw = _ffn_mid_bwd(dh, u3, fcw8, s, dff)
    tnu = _pick(dff, 1408, LANES)
    njh = dff // tnu
    g_wup = _matmul(x1b, du3, mode="tn", m=d, n=2 * dff, k=s, tm=d, tn=tnu, tk=ts, out_dtype=BF16, name="grad_w_up",
                    vmem_mb=48, b_spec=pl.BlockSpec((None, ts, tnu), lambda j, i, kk: (j // njh, kk, j % njh)))
    dz1, dz1b, st1 = _ffn_dx_ln1_bwd(du3, wup, dz2, xh1, rstd1, ln1_g, s, d, dff)

    mix = jnp.concatenate([attn, conv], axis=1)
    g_wout = _matmul(mix, dz1b, mode="tn", m=d, n=d, k=s, tm=_pick(d, 1024, LANES), tn=d, tk=ts, out_dtype=BF16,
                     name="grad_w_out", vmem_mb=48)
    dmix = _matmul(dz1b, wout, mode="nt", m=s, n=d, k=d, tm=_pick(s, 1024, 16), tn=_pick(d, 1024, LANES), tk=d,
                   out_dtype=BF16, name="out_dmix", vmem_mb=48)
    d3, dscw = _convmix_bwd(proj, dmix, scw8, s, d)
    dq, dk, dv, dsink = _attn_bwd(proj, dmix, sinks, cos, sin, s)
    dproj = jnp.concatenate([dq, dk, dv, d3[0], d3[1], d3[2]], axis=1)
    g_win_t = _matmul(dproj, xb, mode="tn", m=n_in, n=d, k=s, tm=_pick(n_in, 2176, LANES), tn=_pick(d, 1024, LANES),
                      tk=ts, out_dtype=BF16, name="grad_w_in", vmem_mb=48)
    grad_x = _matmul(dproj, win_t, mode="nn", m=s, n=d, k=n_in, tm=_pick(s, 512, 16), tn=_pick(d, 1024, LANES),
                     tk=n_in, out_dtype=F32, name="grad_x", vmem_mb=56, res=dz1, alpha=ALPHA)
    small = dict(loss_sq=st2[2, 0], ln2_g=st2[0], ln2_b=st2[1], ln1_g=st1[0], ln1_b=st1[1], sinks=dsink[0, :N_Q_HEADS],
                 fcw=jnp.concatenate([dfcw[0, :3], dfcw[1, :3]], axis=1), scw=dscw[:3])
    return grad_x, dict(win_t=g_win_t, wout=g_wout, wup=g_wup, wdown=g_wdown), small


BIG = ("win_t", "wout", "wup", "wdown")


def _geom(shard_shapes):
    out = {}
    for name in BIG:
        r, c = shard_shapes[name]
        out[name] = ("col" if name == "wup" else "row", (r, c), (r // 2, c))
    return out


def _full_shape(kind, shard):
    r, c = shard
    return (N_CHIPS * r, c) if kind == "row" else (r, N_CHIPS * c)


def _piece_of(ref, kind, shard, chip, half):
    r, c = shard
    if kind == "row":
        return ref.at[pl.ds(chip * r + half * (r // 2), r // 2), :]
    return ref.at[pl.ds(half * (r // 2), r // 2), pl.ds(chip * c, c)]


def _shard_piece(ref, shard, half):
    r, _ = shard
    return ref.at[pl.ds(half * (r // 2), r // 2), :]


def _me():
    return lax.axis_index("x"), lax.axis_index("y"), lax.axis_index("c")


def _other_chips(x, y):
    return [(1 - x, y), (x, 1 - y), (1 - x, 1 - y)]


def _remote(src, dst, send_sem, recv_sem, dev):
    return pltpu.make_async_remote_copy(src_ref=src, dst_ref=dst, send_sem=send_sem, recv_sem=recv_sem,
                                        device_id=dev, device_id_type=MESH)


def _place_shard(w, chip1, kind, name):
    r, c = w.shape
    tr = _rows_tile(r, c, 16)
    nt = r // tr

    def body(chip_ref, w_ref, o_ref):
        o_ref[...] = w_ref[...].astype(BF16)

    out_map = (lambda i, chip_ref: (chip_ref[0] * nt + i, 0)) if kind == "row" else (lambda i, chip_ref: (i, chip_ref[0]))
    return pl.pallas_call(
        body, name="place_" + name,
        grid_spec=pltpu.PrefetchScalarGridSpec(
            num_scalar_prefetch=1, grid=(nt,),
            in_specs=[pl.BlockSpec((tr, c), lambda i, chip_ref: (i, 0))],
            out_specs=pl.BlockSpec((tr, c), out_map)),
        out_shape=jax.ShapeDtypeStruct(_full_shape(kind, (r, c)), BF16),
        compiler_params=_cp(("arbitrary",), 32),
    )(chip1, w)


def _allgather_weights(placed, geom, small_shards):
    nb, ns = len(BIG), len(small_shards)
    small_w = [a.shape[1] for a in small_shards]

    def body(*refs):
        sm = refs[nb:nb + ns]
        full = refs[nb + ns:2 * nb + ns]
        smf = refs[2 * nb + ns:2 * nb + 2 * ns]
        send, recv, loc = refs[2 * nb + 2 * ns:]
        x, y, c = _me()
        chip = 2 * x + y
        sib = (x, y, 1 - c)
        others = _other_chips(x, y)
        locals_, sends = [], []
        for m, name in enumerate(BIG):
            kind, shard, _ = geom[name]
            mine = _piece_of(full[m], kind, shard, chip, c)
            for k, (qx, qy) in enumerate(others):
                cp = _remote(mine, mine, send.at[6 * m + k], recv.at[6 * m + k], (qx, qy, c))
                cp.start()
                sends.append(cp)
        for t in range(ns):
            cp = pltpu.make_async_copy(sm[t], smf[t].at[:, pl.ds(chip * small_w[t], small_w[t])], loc.at[t])
            cp.start()
            locals_.append(cp)
            for k, (qx, qy) in enumerate(others):
                cp = _remote(sm[t], smf[t].at[:, pl.ds(chip * small_w[t], small_w[t])],
                             send.at[6 * nb + 3 * t + k], recv.at[6 * nb + 3 * t + k], (qx, qy, c))
                cp.start()
                sends.append(cp)
        for m, name in enumerate(BIG):
            kind, shard, _ = geom[name]
            for k, (qx, qy) in enumerate(others):
                got = _piece_of(full[m], kind, shard, 2 * qx + qy, c)
                _remote(got, got, send.at[6 * m + k], recv.at[6 * m + k], (qx, qy, c)).wait_recv()
                cp = _remote(got, got, send.at[6 * m + 3 + k], recv.at[6 * m + 3 + k], sib)
                cp.start()
                sends.append(cp)
        for t in range(ns):
            for k, (qx, qy) in enumerate(others):
                got = smf[t].at[:, pl.ds((2 * qx + qy) * small_w[t], small_w[t])]
                _remote(got, got, send.at[6 * nb + 3 * t + k], recv.at[6 * nb + 3 * t + k], (qx, qy, c)).wait_recv()
        for m, name in enumerate(BIG):
            kind, shard, _ = geom[name]
            for k, (qx, qy) in enumerate(others):
                got = _piece_of(full[m], kind, shard, 2 * qx + qy, 1 - c)
                _remote(got, got, send.at[6 * m + 3 + k], recv.at[6 * m + 3 + k], sib).wait_recv()
        for cp in sends:
            cp.wait_send()
        for cp in locals_:
            cp.wait()

    nsem = 6 * nb + 3 * ns
    out_shape = [jax.ShapeDtypeStruct(placed[n].shape, BF16) for n in BIG]
    out_shape += [jax.ShapeDtypeStruct((8, N_CHIPS * w), F32) for w in small_w]
    outs = pl.pallas_call(
        body, name="allgather_weights", in_specs=[ANY] * (nb + ns), out_specs=[ANY] * (nb + ns), out_shape=out_shape,
        input_output_aliases={m: m for m in range(nb)},
        scratch_shapes=[pltpu.SemaphoreType.DMA((nsem,)), pltpu.SemaphoreType.DMA((nsem,)),
                        pltpu.SemaphoreType.DMA((ns,))],
    )(*[placed[n] for n in BIG], *small_shards)
    return dict(zip(BIG, outs[:nb])), list(outs[nb:])


def _sibling_exchange(grads, geom):
    nb = len(BIG)

    def body(*refs):
        g = refs[:nb]
        got = refs[nb:2 * nb]
        send, recv = refs[2 * nb:]
        x, y, c = _me()
        sib = (x, y, 1 - c)
        cps = []
        for m, name in enumerate(BIG):
            kind, shard, _ = geom[name]
            for r in range(N_CHIPS):
                cp = _remote(_piece_of(g[m], kind, shard, r, 1 - c), got[m].at[r],
                             send.at[N_CHIPS * m + r], recv.at[N_CHIPS * m + r], sib)
                cp.start()
                cps.append(cp)
        for cp in cps:
            cp.wait_recv()
        for cp in cps:
            cp.wait_send()

    return pl.pallas_call(
        body, name="grad_sibling_exchange", in_specs=[ANY] * nb, out_specs=[ANY] * nb,
        out_shape=[jax.ShapeDtypeStruct((N_CHIPS,) + geom[n][2], BF16) for n in BIG],
        scratch_shapes=[pltpu.SemaphoreType.DMA((N_CHIPS * nb,)), pltpu.SemaphoreType.DMA((N_CHIPS * nb,))],
    )(*[grads[n] for n in BIG])


def _chip_exchange(chip_sums, geom):
    nb = len(BIG)

    def body(*refs):
        t = refs[:nb]
        got = refs[nb:2 * nb]
        send, recv = refs[2 * nb:]
        x, y, c = _me()
        cps = []
        for m in range(nb):
            for k, (qx, qy) in enumerate(_other_chips(x, y)):
                cp = _remote(t[m].at[2 * qx + qy], got[m].at[k], send.at[3 * m + k], recv.at[3 * m + k], (qx, qy, c))
                cp.start()
                cps.append(cp)
        for cp in cps:
            cp.wait_recv()
        for cp in cps:
            cp.wait_send()

    return pl.pallas_call(
        body, name="grad_chip_exchange", in_specs=[ANY] * nb, out_specs=[ANY] * nb,
        out_shape=[jax.ShapeDtypeStruct((N_CHIPS - 1,) + geom[n][2], BF16) for n in BIG],
        scratch_shapes=[pltpu.SemaphoreType.DMA((3 * nb,)), pltpu.SemaphoreType.DMA((3 * nb,))],
    )(*chip_sums)


def _sibling_assemble(shards, geom):
    nb = len(BIG)

    def body(*refs):
        full = refs[nb:2 * nb]
        send, recv = refs[2 * nb:]
        x, y, c = _me()
        sib = (x, y, 1 - c)
        cps = []
        for m, name in enumerate(BIG):
            mine = _shard_piece(full[m], geom[name][1], c)
            cp = _remote(mine, mine, send.at[m], recv.at[m], sib)
            cp.start()
            cps.append(cp)
        for m, name in enumerate(BIG):
            theirs = _shard_piece(full[m], geom[name][1], 1 - c)
            _remote(theirs, theirs, send.at[m], recv.at[m], sib).wait_recv()
        for cp in cps:
            cp.wait_send()

    return pl.pallas_call(
        body, name="grad_sibling_assemble", in_specs=[ANY] * nb, out_specs=[ANY] * nb,
        out_shape=[jax.ShapeDtypeStruct(geom[n][1], F32) for n in BIG],
        input_output_aliases={m: m for m in range(nb)},
        scratch_shapes=[pltpu.SemaphoreType.DMA((nb,)), pltpu.SemaphoreType.DMA((nb,))],
    )(*shards)


def _allreduce_small(part):
    rows = part.shape[0]
    flips = [(a, b, e) for a in (0, 1) for b in (0, 1) for e in (0, 1) if (a, b, e) != (0, 0, 0)]

    def body(p_ref, o_ref, all_ref, send, recv):
        x, y, c = _me()
        me = 4 * x + 2 * y + c
        all_ref[me] = p_ref[...]
        cps = []
        for k, (a, b, e) in enumerate(flips):
            cp = _remote(p_ref, all_ref.at[me], send.at[k], recv.at[k], (x ^ a, y ^ b, c ^ e))
            cp.start()
            cps.append(cp)
        for k, (a, b, e) in enumerate(flips):
            peer = 4 * (x ^ a) + 2 * (y ^ b) + (c ^ e)
            _remote(p_ref, all_ref.at[peer], send.at[k], recv.at[k], (x ^ a, y ^ b, c ^ e)).wait_recv()
        for cp in cps:
            cp.wait_send()
        tot = all_ref[0]
        for dev in range(1, 8):
            tot = tot + all_ref[dev]
        o_ref[...] = tot

    vm = pl.BlockSpec(memory_space=pltpu.VMEM)
    return pl.pallas_call(
        body, name="allreduce_small", in_specs=[vm], out_specs=vm, out_shape=jax.ShapeDtypeStruct((rows, LANES), F32),
        scratch_shapes=[pltpu.VMEM((8, rows, LANES), F32), pltpu.SemaphoreType.DMA((7,)), pltpu.SemaphoreType.DMA((7,))],
    )(part)


def _rows_tile(rows, cols, mult):
    return _pick(rows, max(mult, (1 << 19) // cols // mult * mult), mult)


def _add_pairs(g, got, kind, shard, where, name):
    p, r, c = got.shape
    tr = _rows_tile(r, c, 16)
    nt = r // tr

    def body(w_ref, a_ref, b_ref, o_ref):
        o_ref[...] = (a_ref[...].astype(F32) + b_ref[...].astype(F32)).astype(BF16)

    if kind == "row":
        g_map = lambda q, i, w_ref: ((2 * q + w_ref[1]) * nt + i, 0)
    else:
        g_map = lambda q, i, w_ref: (w_ref[1] * nt + i, q)
    spec = pl.BlockSpec((None, tr, c), lambda q, i, w_ref: (q, i, 0))
    return pl.pallas_call(
        body, name="grad_add_sibling_" + name,
        grid_spec=pltpu.PrefetchScalarGridSpec(
            num_scalar_prefetch=1, grid=(p, nt), in_specs=[pl.BlockSpec((tr, c), g_map), spec], out_specs=spec),
        out_shape=jax.ShapeDtypeStruct((p, r, c), BF16), compiler_params=_cp(("arbitrary", "arbitrary"), 32),
    )(where, g, got)


def _add_four(t, got, shard, where, name):
    _, r, c = t.shape
    tr = _rows_tile(r, c, 16)
    nt = r // tr

    def body(w_ref, own, t0, t1, t2, o_ref):
        o_ref[...] = ((own[...].astype(F32) + t0[...].astype(F32)) + t1[...].astype(F32)) + t2[...].astype(F32)

    spec = lambda q: pl.BlockSpec((None, tr, c), lambda i, w_ref: (q, i, 0))
    return pl.pallas_call(
        body, name="grad_add_chips_" + name,
        grid_spec=pltpu.PrefetchScalarGridSpec(
            num_scalar_prefetch=1, grid=(nt,),
            in_specs=[pl.BlockSpec((None, tr, c), lambda i, w_ref: (w_ref[0], i, 0)), spec(0), spec(1), spec(2)],
            out_specs=pl.BlockSpec((tr, c), lambda i, w_ref: (w_ref[1] * nt + i, 0))),
        out_shape=jax.ShapeDtypeStruct(shard, F32), compiler_params=_cp(("arbitrary",), 32),
    )(where, t, got, got, got)


def _adamw(w, g, m, v, name):
    r, c = w.shape
    tr = _rows_tile(r, c, 8)

    def body(w_ref, g_ref, m_ref, v_ref, go_ref, d_ref, mo_ref, vo_ref):
        gv = g_ref[...]
        mn = ADAM_B1 * m_ref[...] + (1.0 - ADAM_B1) * gv
        vn = ADAM_B2 * v_ref[...] + (1.0 - ADAM_B2) * (gv * gv)
        m_hat = mn / (1.0 - ADAM_B1 ** ADAM_STEP)
        v_hat = vn / (1.0 - ADAM_B2 ** ADAM_STEP)
        go_ref[...] = gv
        d_ref[...] = -ADAM_LR * (m_hat / (jnp.sqrt(v_hat) + ADAM_EPS) + ADAM_WD * w_ref[...])
        mo_ref[...] = mn
        vo_ref[...] = vn

    spec = pl.BlockSpec((tr, c), lambda i: (i, 0))
    return pl.pallas_call(
        body, name=name, grid=(r // tr,), in_specs=[spec] * 4, out_specs=[spec] * 4,
        out_shape=[jax.ShapeDtypeStruct((r, c), F32)] * 4, compiler_params=_cp(("arbitrary",), 32),
    )(w, g, m, v)


def _pack(vectors, rows):
    flat = jnp.concatenate([v.reshape(-1).astype(F32) for v in vectors])
    return jnp.pad(flat, (0, rows * LANES - flat.shape[0])).reshape(rows, LANES)


def _unpack(packed, shapes):
    flat = packed.reshape(-1)
    out, off = [], 0
    for shp in shapes:
        n = 1
        for t in shp:
            n *= t
        out.append(flat[off:off + n].reshape(shp))
        off += n
    return out


def _rows_for(shapes):
    n = sum(functools.reduce(lambda a, b: a * b, shp, 1) for shp in shapes)
    return -(-n // (8 * LANES)) * 8


def kernel(x, w_in, attn_sinks, short_conv_w, w_out, ln1_g, ln1_b, ffn_w_up, ffn_conv_w, ffn_w_down, ln2_g, ln2_b, loss_target, m_w_in, m_attn_sinks, m_short_conv_w, m_w_out, m_ln1_g, m_ln1_b, m_ffn_w_up, m_ffn_conv_w, m_ffn_w_down, m_ln2_g, m_ln2_b, v_w_in, v_attn_sinks, v_short_conv_w, v_w_out, v_ln1_g, v_ln1_b, v_ffn_w_up, v_ffn_conv_w, v_ffn_w_down, v_ln2_g, v_ln2_b):
    xs, tgt = x[0], loss_target[0]
    s, d = xs.shape
    chip = 2 * lax.axis_index("x") + lax.axis_index("y")

    t_in = lambda a: a[0].T
    w_big = dict(win_t=t_in(w_in), wout=w_out[0], wup=ffn_w_up[0], wdown=ffn_w_down[0])
    m_big = dict(win_t=t_in(m_w_in), wout=m_w_out[0], wup=m_ffn_w_up[0], wdown=m_ffn_w_down[0])
    v_big = dict(win_t=t_in(v_w_in), wout=v_w_out[0], wup=v_ffn_w_up[0], wdown=v_ffn_w_down[0])
    geom = _geom({n: w_big[n].shape for n in BIG})
    pad8 = lambda a: jnp.pad(a[0], ((0, 5), (0, 0)))
    where = jnp.stack([chip, lax.axis_index("c")]).astype(jnp.int32)
    placed = {n: _place_shard(w_big[n], where[:1], geom[n][0], n) for n in BIG}
    full, (scw8, fcw8) = _allgather_weights(placed, geom, [pad8(short_conv_w), pad8(ffn_conv_w)])

    grad_x, g_big, g_small = _local_step(xs, tgt, full["win_t"], full["wout"], full["wup"], full["wdown"], scw8, fcw8,
                                         attn_sinks, ln1_g, ln1_b, ln2_g, ln2_b)

    from_sibling = _sibling_exchange(g_big, geom)
    chip_sums = [_add_pairs(g_big[n], from_sibling[m], geom[n][0], geom[n][1], where, n) for m, n in enumerate(BIG)]
    from_chips = _chip_exchange(chip_sums, geom)
    halves = [_add_four(chip_sums[m], from_chips[m], geom[n][1], where, n) for m, n in enumerate(BIG)]
    g_shards = dict(zip(BIG, _sibling_assemble(halves, geom)))

    small_names = ("ln1_g", "ln1_b", "ln2_g", "ln2_b", "sinks", "fcw", "scw")
    small_shapes = [g_small[n].shape for n in small_names]
    red = _allreduce_small(_pack([g_small["loss_sq"].reshape(1)] + [g_small[n] for n in small_names],
                                 _rows_for([(1,)] + small_shapes)))
    loss_sq, *gs = _unpack(red, [(1,)] + small_shapes)
    gs = dict(zip(small_names, gs))
    loss = (0.5 / d) * loss_sq[0]
    fw, sw = ffn_conv_w.shape[2], short_conv_w.shape[2]
    gs["fcw"] = lax.dynamic_slice_in_dim(gs["fcw"], chip * fw, fw, axis=1)
    gs["scw"] = lax.dynamic_slice_in_dim(gs["scw"], chip * sw, sw, axis=1)

    upd = {n: _adamw(w_big[n], g_shards[n], m_big[n], v_big[n], "adamw_" + n) for n in BIG}
    upd["win_t"] = tuple(a.T for a in upd["win_t"])
    sm_w = dict(ln1_g=ln1_g[0], ln1_b=ln1_b[0], ln2_g=ln2_g[0], ln2_b=ln2_b[0], sinks=attn_sinks[0],
                fcw=ffn_conv_w[0], scw=short_conv_w[0])
    sm_m = dict(ln1_g=m_ln1_g[0], ln1_b=m_ln1_b[0], ln2_g=m_ln2_g[0], ln2_b=m_ln2_b[0], sinks=m_attn_sinks[0],
                fcw=m_ffn_conv_w[0], scw=m_short_conv_w[0])
    sm_v = dict(ln1_g=v_ln1_g[0], ln1_b=v_ln1_b[0], ln2_g=v_ln2_g[0], ln2_b=v_ln2_b[0], sinks=v_attn_sinks[0],
                fcw=v_ffn_conv_w[0], scw=v_short_conv_w[0])
    shapes = [sm_w[n].shape for n in small_names]
    rows = _rows_for(shapes)
    packed = [_pack([t[n] for n in small_names], rows) for t in (sm_w, gs, sm_m, sm_v)]
    sm_out = [dict(zip(small_names, _unpack(a, shapes))) for a in _adamw(*packed, "adamw_small")]

    def leaf(kind, name):
        if name in ("w_in", "w_out", "ffn_w_up", "ffn_w_down"):
            key = dict(w_in="win_t", w_out="wout", ffn_w_up="wup", ffn_w_down="wdown")[name]
            return upd[key][kind][None]
        key = dict(attn_sinks="sinks", short_conv_w="scw", ffn_conv_w="fcw").get(name, name)
        return sm_out[kind][key][None]

    order = ("w_in", "attn_sinks", "short_conv_w", "w_out", "ln1_g", "ln1_b", "ffn_w_up", "ffn_conv_w", "ffn_w_down",
             "ln2_g", "ln2_b")
    outs = [loss, grad_x[None]]
    for kind in range(4):
        outs += [leaf(kind, n) for n in order]
    return tuple(outs)
```

```python
import functools

import jax
import jax.numpy as jnp
from jax import lax
from jax.experimental import pallas as pl
from jax.experimental.pallas import tpu as pltpu

F32 = jnp.float32
BF16 = jnp.bfloat16
MESH = pl.DeviceIdType.MESH
ANY = pl.BlockSpec(memory_space=pl.ANY)

HEAD_DIM = 64
N_Q_HEADS = 16
N_KV_HEADS = 2
ATTN_WIDTH = N_Q_HEADS * HEAD_DIM
KV_WIDTH = N_KV_HEADS * HEAD_DIM
BLOCK = 128
ROPE_THETA = 10000.0
LN_EPS = 1e-5
ALPHA = 2.0 ** 0.25
NEG_INF = -1e30
ADAM_LR, ADAM_B1, ADAM_B2, ADAM_EPS, ADAM_WD, ADAM_STEP = 0.001, 0.9, 0.999, 1e-08, 0.01, 10
N_CHIPS = 4
LANES = 128
SLAB = 128


def _cp(sem, vmem_mb):
    return pltpu.CompilerParams(dimension_semantics=sem, vmem_limit_bytes=vmem_mb << 20)


def _matmul(a, b, *, mode, m, n, k, tm, tn, tk, out_dtype, name, vmem_mb, a_spec=None, b_spec=None,
            res=None, alpha=1.0):
    nj, ni, nk = n // tn, m // tm, k // tk
    assert nj * tn == n and ni * tm == m and nk * tk == k, (name, m, n, k, tm, tn, tk)
    if mode == "nn":
        dims = ((1,), (0,))
        a_spec = a_spec or pl.BlockSpec((tm, tk), lambda j, i, kk: (i, kk))
        b_spec = b_spec or pl.BlockSpec((tk, tn), lambda j, i, kk: (kk, j))
    elif mode == "nt":
        dims = ((1,), (1,))
        a_spec = a_spec or pl.BlockSpec((tm, tk), lambda j, i, kk: (i, kk))
        b_spec = b_spec or pl.BlockSpec((tn, tk), lambda j, i, kk: (j, kk))
    else:
        dims = ((0,), (0,))
        a_spec = a_spec or pl.BlockSpec((tk, tm), lambda j, i, kk: (kk, i))
        b_spec = b_spec or pl.BlockSpec((tk, tn), lambda j, i, kk: (kk, j))
    has_res = res is not None

    def body(*refs):
        a_ref, b_ref = refs[0], refs[1]
        res_ref = refs[2] if has_res else None
        o_ref = refs[2 + has_res]
        part = lax.dot_general(a_ref[...], b_ref[...], (dims, ((), ())), preferred_element_type=F32)

        def finish(acc):
            if has_res:
                acc = acc + alpha * res_ref[...].astype(F32)
            o_ref[...] = acc.astype(o_ref.dtype)

        if nk == 1:
            finish(part)
        else:
            acc_ref = refs[3 + has_res]
            kk = pl.program_id(2)

            @pl.when(kk == 0)
            def _():
                acc_ref[...] = part

            @pl.when(kk > 0)
            def _():
                acc_ref[...] += part

            @pl.when(kk == nk - 1)
            def _():
                finish(acc_ref[...])

    in_specs = [a_spec, b_spec]
    args = [a, b]
    if has_res:
        in_specs.append(pl.BlockSpec((tm, tn), lambda j, i, kk: (i, j)))
        args.append(res)
    return pl.pallas_call(
        body, name=name, grid=(nj, ni, nk), in_specs=in_specs,
        out_specs=pl.BlockSpec((tm, tn), lambda j, i, kk: (i, j)),
        out_shape=jax.ShapeDtypeStruct((m, n), out_dtype),
        scratch_shapes=[pltpu.VMEM((tm, tn), F32)] if nk > 1 else [],
        compiler_params=_cp(("arbitrary", "arbitrary", "arbitrary"), vmem_mb),
    )(*args)


def _pick(total, want, mult):
    if total <= want:
        return total
    for t in range(want, 0, -1):
        if total % t == 0 and t % mult == 0:
            return t
    return total


def _rope_tables(s):
    half = HEAD_DIM // 2
    inv_freq = ROPE_THETA ** (-jnp.arange(half, dtype=F32) / half)
    ang = jnp.arange(s, dtype=F32)[:, None] * inv_freq[None, :]
    cos = jnp.tile(jnp.cos(ang), (1, LANES // half))
    sin = jnp.tile(jnp.concatenate([-jnp.sin(ang), jnp.sin(ang)], axis=1), (1, LANES // HEAD_DIM))
    return cos, sin


def _rope(x, cos, sin, lo):
    partner = jnp.where(lo, pltpu.roll(x, LANES - HEAD_DIM // 2, 1), pltpu.roll(x, HEAD_DIM // 2, 1))
    return x * cos + partner * sin


def _dot(a, b, dims):
    return lax.dot_general(a, b, (dims, ((), ())), preferred_element_type=F32)


NN, NT, TN = ((1,), (0,)), ((1,), (1,)), ((0,), (0,))


def _kv_variants(t, head_lo):
    r = pltpu.roll(t, HEAD_DIM, 1)
    zero = jnp.zeros_like(t)
    a = (jnp.where(head_lo, t, zero).astype(BF16), jnp.where(head_lo, r, zero).astype(BF16))
    b = (jnp.where(head_lo, zero, r).astype(BF16), jnp.where(head_lo, zero, t).astype(BF16))
    return a, b


PAIRS_PER_KV = N_Q_HEADS // 2 // N_KV_HEADS
STACK = PAIRS_PER_KV * BLOCK


def _stack_pairs(ref, j, fn):
    return jnp.concatenate([fn(ref[:, p * LANES:(p + 1) * LANES])
                            for p in range(j * PAIRS_PER_KV, (j + 1) * PAIRS_PER_KV)], axis=0)


def _stack_sinks(sink_ref, j, hh):
    return jnp.concatenate([jnp.full((BLOCK, 1), sink_ref[0, 2 * p + hh], F32)
                            for p in range(j * PAIRS_PER_KV, (j + 1) * PAIRS_PER_KV)], axis=0)


def _attn_probs(qp, ka, kb, valid, sink_a, sink_b):
    out = []
    for kk, sink in ((ka, sink_a), (kb, sink_b)):
        s = jnp.where(valid, _dot(qp, kk, NT), NEG_INF)
        mx = jnp.maximum(jnp.max(s, axis=1, keepdims=True), sink)
        e = jnp.exp(s - mx)
        es = jnp.exp(sink - mx)
        inv = 1.0 / (jnp.sum(e, axis=1, keepdims=True) + es)
        out.append((e * inv, es * inv))
    return out


def _attn_common(i, q_ref, k_ref, v_ref, kp_ref, vp_ref, cos_ref, sin_ref, cosp_ref, sinp_ref):
    lane = lax.broadcasted_iota(jnp.int32, (1, LANES), 1)
    lo = (lane % HEAD_DIM) < (HEAD_DIM // 2)
    head_lo = lane < HEAD_DIM
    cos, sin = cos_ref[...], sin_ref[...]
    kc = _rope(k_ref[...].astype(F32), cos, sin, lo)
    kp = _rope(kp_ref[...].astype(F32), cosp_ref[...], sinp_ref[...], lo)
    kext = jnp.concatenate([kp, kc], axis=0)
    vext = jnp.concatenate([vp_ref[...].astype(F32), v_ref[...].astype(F32)], axis=0)
    ka, kb = _kv_variants(kext, head_lo)
    va, vb = _kv_variants(vext, head_lo)
    qi = lax.broadcasted_iota(jnp.int32, (STACK, 1), 0) % BLOCK
    kj = lax.broadcasted_iota(jnp.int32, (1, 2 * BLOCK), 1)
    valid = (kj > qi) & (kj <= qi + BLOCK) & ((kj >= BLOCK) | (i > 0))
    cos4 = jnp.concatenate([cos] * PAIRS_PER_KV, axis=0)
    sin4 = jnp.concatenate([sin] * PAIRS_PER_KV, axis=0)
    return lo, head_lo, cos, sin, cos4, sin4, ka, kb, va, vb, valid


def _attn_fwd(proj, sinks, cos, sin, s):
    nb = s // BLOCK
    kcol, vcol = ATTN_WIDTH // LANES, ATTN_WIDTH // LANES + 1

    def body(q_ref, k_ref, v_ref, kp_ref, vp_ref, cos_ref, sin_ref, cosp_ref, sinp_ref, sink_ref, o_ref):
        i = pl.program_id(0)
        lo, head_lo, cs, sn, cs4, sn4, ka, kb, va, vb, valid = _attn_common(
            i, q_ref, k_ref, v_ref, kp_ref, vp_ref, cos_ref, sin_ref, cosp_ref, sinp_ref)
        for j in range(N_KV_HEADS):
            q4 = _stack_pairs(q_ref, j, lambda t: t.astype(F32))
            qp = (_rope(q4, cs4, sn4, lo) * HEAD_DIM ** -0.5).astype(BF16)
            (pa, _), (pb, _) = _attn_probs(qp, ka[j], kb[j], valid, _stack_sinks(sink_ref, j, 0),
                                           _stack_sinks(sink_ref, j, 1))
            o = (_dot(pa.astype(BF16), va[j], NN) + _dot(pb.astype(BF16), vb[j], NN)).astype(BF16)
            for t in range(PAIRS_PER_KV):
                p = j * PAIRS_PER_KV + t
                o_ref[:, p * LANES:(p + 1) * LANES] = o[t * BLOCK:(t + 1) * BLOCK]

    prev = lambda i: (jnp.maximum(i - 1, 0), 0)
    return pl.pallas_call(
        body, name="attn_fwd", grid=(nb,),
        in_specs=[pl.BlockSpec((BLOCK, ATTN_WIDTH), lambda i: (i, 0)),
                  pl.BlockSpec((BLOCK, LANES), lambda i: (i, kcol)),
                  pl.BlockSpec((BLOCK, LANES), lambda i: (i, vcol)),
                  pl.BlockSpec((BLOCK, LANES), lambda i: (jnp.maximum(i - 1, 0), kcol)),
                  pl.BlockSpec((BLOCK, LANES), lambda i: (jnp.maximum(i - 1, 0), vcol)),
                  pl.BlockSpec((BLOCK, LANES), lambda i: (i, 0)),
                  pl.BlockSpec((BLOCK, LANES), lambda i: (i, 0)),
                  pl.BlockSpec((BLOCK, LANES), prev),
                  pl.BlockSpec((BLOCK, LANES), prev),
                  pl.BlockSpec(memory_space=pltpu.SMEM)],
        out_specs=pl.BlockSpec((BLOCK, ATTN_WIDTH), lambda i: (i, 0)),
        out_shape=jax.ShapeDtypeStruct((s, ATTN_WIDTH), BF16),
        compiler_params=_cp(("arbitrary",), 32),
    )(proj, proj, proj, proj, proj, cos, sin, cos, sin, sinks)


def _attn_bwd(proj, dmix, sinks, cos, sin, s):
    nb = s // BLOCK
    kcol, vcol = ATTN_WIDTH // LANES, ATTN_WIDTH // LANES + 1
    pairs_per_kv = N_Q_HEADS // 2 // N_KV_HEADS

    def body(q_ref, k_ref, v_ref, kp_ref, vp_ref, cos_ref, sin_ref, cosp_ref, sinp_ref, sink_ref, do_ref,
             dq_ref, dk_ref, dv_ref, dsink_ref, ck_ref, cv_ref):
        g = pl.program_id(0)
        i = nb - 1 - g

        @pl.when(g == 0)
        def _():
            ck_ref[...] = jnp.zeros_like(ck_ref)
            cv_ref[...] = jnp.zeros_like(cv_ref)
            dsink_ref[...] = jnp.zeros_like(dsink_ref)

        lo, head_lo, cs, sn, cs4, sn4, ka, kb, va, vb, valid = _attn_common(
            i, q_ref, k_ref, v_ref, kp_ref, vp_ref, cos_ref, sin_ref, cosp_ref, sinp_ref)
        lane = lax.broadcasted_iota(jnp.int32, (1, LANES), 1)
        dk_j, dv_j = [], []
        dsink = jnp.zeros((1, LANES), F32)
        for j in range(N_KV_HEADS):
            q4 = _stack_pairs(q_ref, j, lambda t: t.astype(F32))
            qp = (_rope(q4, cs4, sn4, lo) * HEAD_DIM ** -0.5).astype(BF16)
            probs = _attn_probs(qp, ka[j], kb[j], valid, _stack_sinks(sink_ref, j, 0), _stack_sinks(sink_ref, j, 1))
            do = _stack_pairs(do_ref, j, lambda t: t)
            dq_r = jnp.zeros((STACK, LANES), F32)
            dkc, dvc = [], []
            for hh, ((pr, ps), kk, vv) in enumerate(zip(probs, (ka[j], kb[j]), (va[j], vb[j]))):
                dp = _dot(do, vv, NT)
                delta = jnp.sum(pr * dp, axis=1, keepdims=True)
                ds = (pr * (dp - delta)).astype(BF16)
                psd = ps * delta
                for t in range(PAIRS_PER_KV):
                    head = 2 * (j * PAIRS_PER_KV + t) + hh
                    dsink = dsink + jnp.where(
                        lane == head, -jnp.sum(psd[t * BLOCK:(t + 1) * BLOCK], axis=0, keepdims=True), 0.0)
                dq_r = dq_r + _dot(ds, kk, NN)
                dkc.append(_dot(ds, qp, TN))
                dvc.append(_dot(pr.astype(BF16), do, TN))
            dk_j.append(jnp.where(head_lo, dkc[0], dkc[1]))
            dv_j.append(jnp.where(head_lo, dvc[0], dvc[1]))
            dq = _rope(dq_r * HEAD_DIM ** -0.5, cs4, -sn4, lo).astype(BF16)
            for t in range(PAIRS_PER_KV):
                p = j * PAIRS_PER_KV + t
                dq_ref[:, p * LANES:(p + 1) * LANES] = dq[t * BLOCK:(t + 1) * BLOCK]
        tot_k = [t + pltpu.roll(t, HEAD_DIM, 1) for t in dk_j]
        tot_v = [t + pltpu.roll(t, HEAD_DIM, 1) for t in dv_j]
        dkext = jnp.where(head_lo, tot_k[0], tot_k[1])
        dvext = jnp.where(head_lo, tot_v[0], tot_v[1])
        dk_r = dkext[BLOCK:] + ck_ref[...]
        dk_ref[...] = _rope(dk_r, cs, -sn, lo).astype(BF16)
        dv_ref[...] = (dvext[BLOCK:] + cv_ref[...]).astype(BF16)
        ck_ref[...] = dkext[:BLOCK]
        cv_ref[...] = dvext[:BLOCK]
        dsink_ref[0:1, :] += dsink

    cur = lambda col: (lambda g: (nb - 1 - g, col))
    prv = lambda col: (lambda g: (jnp.maximum(nb - 2 - g, 0), col))
    blk = lambda w, f: pl.BlockSpec((BLOCK, w), f)
    return pl.pallas_call(
        body, name="attn_bwd", grid=(nb,),
        in_specs=[blk(ATTN_WIDTH, cur(0)), blk(LANES, cur(kcol)), blk(LANES, cur(vcol)),
                  blk(LANES, prv(kcol)), blk(LANES, prv(vcol)),
                  blk(LANES, cur(0)), blk(LANES, cur(0)), blk(LANES, prv(0)), blk(LANES, prv(0)),
                  pl.BlockSpec(memory_space=pltpu.SMEM),
                  blk(ATTN_WIDTH, cur(0))],
        out_specs=[blk(ATTN_WIDTH, cur(0)), blk(LANES, cur(0)), blk(LANES, cur(0)),
                   pl.BlockSpec((8, LANES), lambda g: (0, 0))],
        out_shape=[jax.ShapeDtypeStruct((s, ATTN_WIDTH), BF16), jax.ShapeDtypeStruct((s, LANES), BF16),
                   jax.ShapeDtypeStruct((s, LANES), BF16), jax.ShapeDtypeStruct((8, LANES), F32)],
        scratch_shapes=[pltpu.VMEM((BLOCK, LANES), F32), pltpu.VMEM((BLOCK, LANES), F32)],
        compiler_params=_cp(("arbitrary",), 32),
    )(proj, proj, proj, proj, proj, cos, sin, cos, sin, sinks, dmix)


def _causal_conv(x, prev8, w):
    row = lax.broadcasted_iota(jnp.int32, (8, 1), 0)
    r1, r2 = pltpu.roll(x, 1, 0), pltpu.roll(x, 2, 0)
    s1 = jnp.concatenate([jnp.where(row == 0, prev8[7:8], r1[:8]), r1[8:]], axis=0)
    s2 = jnp.concatenate([jnp.where(row == 0, prev8[6:7], jnp.where(row == 1, prev8[7:8], r2[:8])), r2[8:]], axis=0)
    return w[0:1] * s2 + w[1:2] * s1 + w[2:3] * x, s1, s2


def _conv_bwd(dy, x, s1, s2, w, next8):
    r = x.shape[0]
    row = lax.broadcasted_iota(jnp.int32, (8, 1), 0)
    r1, r2 = pltpu.roll(dy, r - 1, 0), pltpu.roll(dy, r - 2, 0)
    n1 = jnp.concatenate([r1[:r - 8], jnp.where(row == 7, next8[0:1], r1[r - 8:])], axis=0)
    n2 = jnp.concatenate([r2[:r - 8], jnp.where(row == 6, next8[0:1], jnp.where(row == 7, next8[1:2], r2[r - 8:]))],
                         axis=0)
    dx = w[2:3] * dy + w[1:2] * n1 + w[0:1] * n2
    dws = [jnp.sum(dy * t, axis=0, keepdims=True) for t in (s2, s1, x)]
    return dx, dws


CONV_COLS = 256


def _convmix_cols(d):
    conv_w = d - ATTN_WIDTH
    base = (ATTN_WIDTH + 2 * KV_WIDTH) // CONV_COLS
    step = conv_w // CONV_COLS
    return base, base + step, base + 2 * step, step


def _convmix_fwd(proj, scw8, s, d):
    gb0, gc0, h0, ncb = _convmix_cols(d)
    tr = _pick(s, 1024, 16)
    ni = s // tr

    def body(gb_ref, gc_ref, h_ref, w_ref, o_ref, carry_ref):
        @pl.when(pl.program_id(1) == 0)
        def _():
            carry_ref[...] = jnp.zeros_like(carry_ref)

        gch = gc_ref[...].astype(F32) * h_ref[...].astype(F32)
        cc, _, _ = _causal_conv(gch, carry_ref[...], w_ref[...])
        o_ref[...] = (gb_ref[...].astype(F32) * cc).astype(BF16)
        carry_ref[...] = gch[tr - 8:]

    spec = lambda c0: pl.BlockSpec((tr, CONV_COLS), lambda j, i: (i, c0 + j))
    return pl.pallas_call(
        body, name="convmix_fwd", grid=(ncb, ni),
        in_specs=[spec(gb0), spec(gc0), spec(h0), pl.BlockSpec((8, CONV_COLS), lambda j, i: (0, j))],
        out_specs=pl.BlockSpec((tr, CONV_COLS), lambda j, i: (i, j)),
        out_shape=jax.ShapeDtypeStruct((s, d - ATTN_WIDTH), BF16),
        scratch_shapes=[pltpu.VMEM((8, CONV_COLS), F32)],
        compiler_params=_cp(("arbitrary", "arbitrary"), 32),
    )(proj, proj, proj, scw8)


def _convmix_bwd(proj, dmix, scw8, s, d):
    gb0, gc0, h0, ncb = _convmix_cols(d)
    tr = _pick(s, 1024, 16)
    ni = s // tr
    dc0 = ATTN_WIDTH // CONV_COLS

    def body(dc_ref, gb_ref, gc_ref, h_ref, gcp_ref, hp_ref, w_ref, d3_ref, dw_ref, nxt_ref):
        g = pl.program_id(1)
        i = ni - 1 - g

        @pl.when(g == 0)
        def _():
            nxt_ref[...] = jnp.zeros_like(nxt_ref)
            dw_ref[...] = jnp.zeros_like(dw_ref)

        w = w_ref[...]
        gb, gc, h = gb_ref[...].astype(F32), gc_ref[...].astype(F32), h_ref[...].astype(F32)
        gch = gc * h
        prev8 = (gcp_ref[...].astype(F32) * hp_ref[...].astype(F32))[8:16] * (i > 0).astype(F32)
        cc, s1, s2 = _causal_conv(gch, prev8, w)
        dc = dc_ref[...].astype(F32)
        dcc = dc * gb
        dgch, dws = _conv_bwd(dcc, gch, s1, s2, w, nxt_ref[...])
        d3_ref[0] = (dc * cc).astype(BF16)
        d3_ref[1] = (dgch * h).astype(BF16)
        d3_ref[2] = (dgch * gc).astype(BF16)
        for t in range(3):
            dw_ref[t:t + 1, :] += dws[t]
        nxt_ref[...] = dcc[0:8]

    cur = lambda c0: pl.BlockSpec((tr, CONV_COLS), lambda j, g: (ni - 1 - g, c0 + j))
    prv = lambda c0: pl.BlockSpec((16, CONV_COLS), lambda j, g: (jnp.maximum((ni - 1 - g) * (tr // 16) - 1, 0), c0 + j))
    return pl.pallas_call(
        body, name="convmix_bwd", grid=(ncb, ni),
        in_specs=[cur(dc0), cur(gb0), cur(gc0), cur(h0), prv(gc0), prv(h0),
                  pl.BlockSpec((8, CONV_COLS), lambda j, g: (0, j))],
        out_specs=[pl.BlockSpec((3, tr, CONV_COLS), lambda j, g: (0, ni - 1 - g, j)),
                   pl.BlockSpec((8, CONV_COLS), lambda j, g: (0, j))],
        out_shape=[jax.ShapeDtypeStruct((3, s, d - ATTN_WIDTH), BF16), jax.ShapeDtypeStruct((8, d - ATTN_WIDTH), F32)],
        scratch_shapes=[pltpu.VMEM((8, CONV_COLS), F32)],
        compiler_params=_cp(("arbitrary", "arbitrary"), 32),
    )(dmix, proj, proj, proj, proj, proj, scw8)


def _ln_fwd(z):
    mu = jnp.mean(z, axis=-1, keepdims=True)
    zc = z - mu
    var = jnp.mean(zc * zc, axis=-1, keepdims=True)
    rstd = lax.rsqrt(var + LN_EPS)
    return zc * rstd, rstd


def _ln_bwd(dout, xh, rstd, g):
    dxh = dout * g
    c1 = jnp.mean(dxh, axis=-1, keepdims=True)
    c2 = jnp.mean(dxh * xh, axis=-1, keepdims=True)
    dz = rstd * (dxh - c1 - xh * c2)
    return dz, jnp.sum(dout * xh, axis=0, keepdims=True), jnp.sum(dout, axis=0, keepdims=True)


def _outproj_ln1(attn, conv, wout, x, g1, b1, s, d):
    tm = _pick(s, 256, 16)
    ka = attn.shape[1]

    def body(a_ref, c_ref, wt_ref, wb_ref, x_ref, g_ref, b_ref, x1_ref, x1b_ref, xh_ref, rs_ref):
        y = _dot(a_ref[...], wt_ref[...], NN) + _dot(c_ref[...], wb_ref[...], NN)
        xh, rstd = _ln_fwd(ALPHA * x_ref[...] + y)
        x1 = xh * g_ref[...] + b_ref[...]
        x1_ref[...] = x1
        x1b_ref[...] = x1.astype(BF16)
        xh_ref[...] = xh.astype(BF16)
        rs_ref[...] = rstd

    row = lambda w: pl.BlockSpec((tm, w), lambda i: (i, 0))
    vec = pl.BlockSpec((1, d), lambda i: (0, 0))
    return pl.pallas_call(
        body, name="outproj_ln1", grid=(s // tm,),
        in_specs=[row(ka), row(d - ka), pl.BlockSpec((ka, d), lambda i: (0, 0)),
                  pl.BlockSpec((d - ka, d), lambda i: (ka // (d - ka), 0)), row(d), vec, vec],
        out_specs=[row(d), row(d), row(d), row(1)],
        out_shape=[jax.ShapeDtypeStruct((s, d), F32), jax.ShapeDtypeStruct((s, d), BF16),
                   jax.ShapeDtypeStruct((s, d), BF16), jax.ShapeDtypeStruct((s, 1), F32)],
        compiler_params=_cp(("arbitrary",), 48),
    )(attn, conv, wout, wout, x, g1, b1)


def _ffn_up(x1b, wup, fcw8, s, d, dff):
    tm = _pick(s, 1024, 16)
    tn = _pick(dff, 512, LANES)
    nj, ni = dff // tn, s // tm

    def body(x_ref, wa_ref, wg_ref, ca_ref, cg_ref, u_ref, h_ref, carry_ref):
        @pl.when(pl.program_id(1) == 0)
        def _():
            carry_ref[...] = jnp.zeros_like(carry_ref)

        xa = x_ref[...]
        ys = []
        for part, (w_ref, c_ref) in enumerate(((wa_ref, ca_ref), (wg_ref, cg_ref))):
            ub = _dot(xa, w_ref[...], NN).astype(BF16)
            u_ref[part] = ub
            u = ub.astype(F32)
            y, _, _ = _causal_conv(u, carry_ref[part], c_ref[...])
            carry_ref[part] = u[tm - 8:]
            ys.append(y)
        a2, g2 = ys
        sig = 1.0 / (1.0 + jnp.exp(-a2))
        h_ref[...] = (a2 * sig * g2).astype(BF16)

    return pl.pallas_call(
        body, name="ffn_up", grid=(nj, ni),
        in_specs=[pl.BlockSpec((tm, d), lambda j, i: (i, 0)),
                  pl.BlockSpec((d, tn), lambda j, i: (0, j)),
                  pl.BlockSpec((d, tn), lambda j, i: (0, j + nj)),
                  pl.BlockSpec((8, tn), lambda j, i: (0, j)),
                  pl.BlockSpec((8, tn), lambda j, i: (0, j + nj))],
        out_specs=[pl.BlockSpec((2, tm, tn), lambda j, i: (0, i, j)),
                   pl.BlockSpec((tm, tn), lambda j, i: (i, j))],
        out_shape=[jax.ShapeDtypeStruct((2, s, dff), BF16), jax.ShapeDtypeStruct((s, dff), BF16)],
        scratch_shapes=[pltpu.VMEM((2, 8, tn), F32)],
        compiler_params=_cp(("arbitrary", "arbitrary"), 48),
    )(x1b, wup, wup, fcw8, fcw8)


def _ffn_mid_bwd(dh, u3, fcw8, s, dff):
    tm = _pick(s, 1024, 16)
    tn = _pick(dff, 512, LANES)
    nj, ni = dff // tn, s // tm

    def body(dh_ref, u_ref, up_ref, ca_ref, cg_ref, du_ref, dw_ref, nxt_ref):
        g = pl.program_id(1)
        i = ni - 1 - g

        @pl.when(g == 0)
        def _():
            nxt_ref[...] = jnp.zeros_like(nxt_ref)
            dw_ref[...] = jnp.zeros_like(dw_ref)

        has_prev = (i > 0).astype(F32)
        saved = []
        for part, c_ref in enumerate((ca_ref, cg_ref)):
            u = u_ref[part].astype(F32)
            prev8 = up_ref[part].astype(F32)[8:16] * has_prev
            y, s1, s2 = _causal_conv(u, prev8, c_ref[...])
            saved.append((u, s1, s2, y))
        a2, g2 = saved[0][3], saved[1][3]
        sig = 1.0 / (1.0 + jnp.exp(-a2))
        silu = a2 * sig
        dhv = dh_ref[...].astype(F32)
        dys = (dhv * g2 * (sig * (1.0 + a2 * (1.0 - sig))), dhv * silu)
        for part, (c_ref, dy) in enumerate(zip((ca_ref, cg_ref), dys)):
            u, s1, s2, _ = saved[part]
            dx, dws = _conv_bwd(dy, u, s1, s2, c_ref[...], nxt_ref[part])
            du_ref[part] = dx.astype(BF16)
            for t in range(3):
                dw_ref[part, t:t + 1, :] += dws[t]
            nxt_ref[part] = dy[0:8]

    return pl.pallas_call(
        body, name="ffn_mid_bwd", grid=(nj, ni),
        in_specs=[pl.BlockSpec((tm, tn), lambda j, g: (ni - 1 - g, j)),
                  pl.BlockSpec((2, tm, tn), lambda j, g: (0, ni - 1 - g, j)),
                  pl.BlockSpec((2, 16, tn), lambda j, g: (0, jnp.maximum((ni - 1 - g) * (tm // 16) - 1, 0), j)),
                  pl.BlockSpec((8, tn), lambda j, g: (0, j)),
                  pl.BlockSpec((8, tn), lambda j, g: (0, j + nj))],
        out_specs=[pl.BlockSpec((2, tm, tn), lambda j, g: (0, ni - 1 - g, j)),
                   pl.BlockSpec((2, 8, tn), lambda j, g: (0, 0, j))],
        out_shape=[jax.ShapeDtypeStruct((2, s, dff), BF16), jax.ShapeDtypeStruct((2, 8, dff), F32)],
        scratch_shapes=[pltpu.VMEM((2, 8, tn), F32)],
        compiler_params=_cp(("arbitrary", "arbitrary"), 56),
    )(dh, u3, u3, fcw8, fcw8)


def _ffn_down_loss(hmid, wdown, x1, target, g2, b2, s, d, dff):
    tm = _pick(s, 512, SLAB)
    tk = _pick(dff, 1408, LANES)
    ni, nk = s // tm, dff // tk
    slab = min(SLAB, tm)

    def body(h_ref, w_ref, x1_ref, t_ref, g_ref, b_ref, dzb_ref, st_ref, acc_ref):
        i, kk = pl.program_id(0), pl.program_id(1)

        @pl.when((i == 0) & (kk == 0))
        def _():
            st_ref[...] = jnp.zeros_like(st_ref)

        part = _dot(h_ref[...], w_ref[...], NN)

        @pl.when(kk == 0)
        def _():
            acc_ref[...] = part

        @pl.when(kk > 0)
        def _():
            acc_ref[...] += part

        @pl.when(kk == nk - 1)
        def _():
            g, b = g_ref[...], b_ref[...]

            def one(sl, carry):
                rows = pl.ds(pl.multiple_of(sl * slab, slab), slab)
                xh, rstd = _ln_fwd(ALPHA * x1_ref[rows, :] + acc_ref[rows, :])
                diff = xh * g + b - t_ref[rows, :]
                sq = jnp.sum(jnp.sum(diff * diff, axis=1, keepdims=True), axis=0, keepdims=True)
                dz, dg, db = _ln_bwd(diff * (1.0 / d), xh, rstd, g)
                dzb_ref[rows, :] = dz.astype(BF16)
                st_ref[0:1, :] += dg
                st_ref[1:2, :] += db
                st_ref[2:3, :] += sq
                return carry

            lax.fori_loop(0, tm // slab, one, 0)

    row = pl.BlockSpec((tm, d), lambda i, kk: (i, 0))
    vec = pl.BlockSpec((1, d), lambda i, kk: (0, 0))
    return pl.pallas_call(
        body, name="ffn_down_loss", grid=(ni, nk),
        in_specs=[pl.BlockSpec((tm, tk), lambda i, kk: (i, kk)), pl.BlockSpec((tk, d), lambda i, kk: (kk, 0)),
                  row, row, vec, vec],
        out_specs=[row, pl.BlockSpec((8, d), lambda i, kk: (0, 0))],
        out_shape=[jax.ShapeDtypeStruct((s, d), BF16), jax.ShapeDtypeStruct((8, d), F32)],
        scratch_shapes=[pltpu.VMEM((tm, d), F32)],
        compiler_params=_cp(("arbitrary", "arbitrary"), 48),
    )(hmid, wdown, x1, target, g2, b2)


def _ffn_dx_ln1_bwd(du3, wup, dz2b, xh1, rstd1, g1, s, d, dff):
    tm = _pick(s, 1024, SLAB)
    tk = _pick(dff, 1408, LANES)
    nkh = dff // tk
    ni, nk = s // tm, 2 * nkh
    slab = min(SLAB, tm)

    def body(a_ref, w_ref, dz2_ref, xh_ref, rs_ref, g_ref, dzb_ref, st_ref, acc_ref):
        i, kk = pl.program_id(0), pl.program_id(1)

        @pl.when((i == 0) & (kk == 0))
        def _():
            st_ref[...] = jnp.zeros_like(st_ref)

        part = _dot(a_ref[...], w_ref[...], NT)

        @pl.when(kk == 0)
        def _():
            acc_ref[...] = part

        @pl.when(kk > 0)
        def _():
            acc_ref[...] += part

        @pl.when(kk == nk - 1)
        def _():
            g = g_ref[...]

            def one(sl, carry):
                rows = pl.ds(pl.multiple_of(sl * slab, slab), slab)
                dx1 = ALPHA * dz2_ref[rows, :].astype(F32) + acc_ref[rows, :]
                dz, dg, db = _ln_bwd(dx1, xh_ref[rows, :].astype(F32), rs_ref[rows, :], g)
                dzb_ref[rows, :] = dz.astype(BF16)
                st_ref[0:1, :] += dg
                st_ref[1:2, :] += db
                return carry

            lax.fori_loop(0, tm // slab, one, 0)

    row = pl.BlockSpec((tm, d), lambda i, kk: (i, 0))
    row1 = pl.BlockSpec((tm, d), lambda i, kk: (i, 0), pipeline_mode=pl.Buffered(1))
    return pl.pallas_call(
        body, name="ffn_dx_ln1_bwd", grid=(ni, nk),
        in_specs=[pl.BlockSpec((None, tm, tk), lambda i, kk: (kk // nkh, i, kk % nkh)),
                  pl.BlockSpec((d, tk), lambda i, kk: (0, kk)),
                  row1, row1, pl.BlockSpec((tm, 1), lambda i, kk: (i, 0)), pl.BlockSpec((1, d), lambda i, kk: (0, 0))],
        out_specs=[row, pl.BlockSpec((8, d), lambda i, kk: (0, 0))],
        out_shape=[jax.ShapeDtypeStruct((s, d), BF16), jax.ShapeDtypeStruct((8, d), F32)],
        scratch_shapes=[pltpu.VMEM((tm, d), F32)],
        compiler_params=_cp(("arbitrary", "arbitrary"), 56),
    )(du3, wup, dz2b, xh1, rstd1, g1)


def _local_step(x, target, win_t, wout, wup, wdown, scw8, fcw8, sinks, ln1_g, ln1_b, ln2_g, ln2_b):
    s, d = x.shape
    dff = wdown.shape[0]
    n_in = win_t.shape[0]
    xb = x.astype(BF16)
    cos, sin = _rope_tables(s)

    proj = _matmul(xb, win_t, mode="nt", m=s, n=n_in, k=d, tm=_pick(s, 512, 16), tn=_pick(n_in, 2176, LANES), tk=d,
                   out_dtype=BF16, name="in_proj", vmem_mb=48)
    attn = _attn_fwd(proj, sinks, cos, sin, s)
    conv = _convmix_fwd(proj, scw8, s, d)
    x1, x1b, xh1, rstd1 = _outproj_ln1(attn, conv, wout, x, ln1_g, ln1_b, s, d)
    u3, hmid = _ffn_up(x1b, wup, fcw8, s, d, dff)
    dz2b, st2 = _ffn_down_loss(hmid, wdown, x1, target, ln2_g, ln2_b, s, d, dff)

    ts = _pick(s, 2048, 16)
    g_wdown = _matmul(hmid, dz2b, mode="tn", m=dff, n=d, k=s, tm=_pick(dff, 1408, LANES), tn=_pick(d, 1024, LANES),
                      tk=ts, out_dtype=BF16, name="grad_w_down", vmem_mb=48)
    dh = _matmul(dz2b, wdown, mode="nt", m=s, n=dff, k=d, tm=_pick(s, 1024, 16), tn=_pick(dff, 1408, LANES), tk=d,
                 out_dtype=BF16, name="ffn_dh", vmem_mb=48)
    du3, dfcw = _ffn_mid_bwd(dh, u3, fcw8, s, dff)
    tnu = _pick(dff, 1408, LANES)
    njh = dff // tnu
    g_wup = _matmul(x1b, du3, mode="tn", m=d, n=2 * dff, k=s, tm=_pick(d, 1024, LANES), tn=tnu, tk=ts, out_dtype=BF16,
                    name="grad_w_up", vmem_mb=48,
                    b_spec=pl.BlockSpec((None, ts, tnu), lambda j, i, kk: (j // njh, kk, j % njh)))
    dz1b, st1 = _ffn_dx_ln1_bwd(du3, wup, dz2b, xh1, rstd1, ln1_g, s, d, dff)

    mix = jnp.concatenate([attn, conv], axis=1)
    g_wout = _matmul(mix, dz1b, mode="tn", m=d, n=d, k=s, tm=_pick(d, 1024, LANES), tn=_pick(d, 1024, LANES), tk=ts,
                     out_dtype=BF16, name="grad_w_out", vmem_mb=48)
    dmix = _matmul(dz1b, wout, mode="nt", m=s, n=d, k=d, tm=_pick(s, 1024, 16), tn=_pick(d, 1024, LANES), tk=d,
                   out_dtype=BF16, name="out_dmix", vmem_mb=48)
    d3, dscw = _convmix_bwd(proj, dmix, scw8, s, d)
    dq, dk, dv, dsink = _attn_bwd(proj, dmix, sinks, cos, sin, s)
    dproj = jnp.concatenate([dq, dk, dv, d3[0], d3[1], d3[2]], axis=1)
    g_win_t = _matmul(dproj, xb, mode="tn", m=n_in, n=d, k=s, tm=_pick(n_in, 2176, LANES), tn=_pick(d, 512, LANES),
                      tk=ts, out_dtype=BF16, name="grad_w_in", vmem_mb=48)
    grad_x = _matmul(dproj, win_t, mode="nn", m=s, n=d, k=n_in, tm=_pick(s, 512, 16), tn=_pick(d, 1024, LANES),
                     tk=n_in, out_dtype=F32, name="grad_x", vmem_mb=56, res=dz1b, alpha=ALPHA)
    small = dict(loss_sq=st2[2, 0], ln2_g=st2[0], ln2_b=st2[1], ln1_g=st1[0], ln1_b=st1[1], sinks=dsink[0, :N_Q_HEADS],
                 fcw=jnp.concatenate([dfcw[0, :3], dfcw[1, :3]], axis=1), scw=dscw[:3])
    return grad_x, dict(win_t=g_win_t, wout=g_wout, wup=g_wup, wdown=g_wdown), small


BIG = ("win_t", "wout", "wup", "wdown")


def _geom(shard_shapes):
    out = {}
    for name in BIG:
        r, c = shard_shapes[name]
        out[name] = ("col" if name == "wup" else "row", (r, c), (r // 2, c))
    return out


def _full_shape(kind, shard):
    r, c = shard
    return (N_CHIPS * r, c) if kind == "row" else (r, N_CHIPS * c)


def _piece_of(ref, kind, shard, chip, half):
    r, c = shard
    if kind == "row":
        return ref.at[pl.ds(chip * r + half * (r // 2), r // 2), :]
    return ref.at[pl.ds(half * (r // 2), r // 2), pl.ds(chip * c, c)]


def _shard_piece(ref, shard, half):
    r, _ = shard
    return ref.at[pl.ds(half * (r // 2), r // 2), :]


def _me():
    return lax.axis_index("x"), lax.axis_index("y"), lax.axis_index("c")


def _other_chips(x, y):
    return [(1 - x, y), (x, 1 - y), (1 - x, 1 - y)]


def _remote(src, dst, send_sem, recv_sem, dev):
    return pltpu.make_async_remote_copy(src_ref=src, dst_ref=dst, send_sem=send_sem, recv_sem=recv_sem,
                                        device_id=dev, device_id_type=MESH)


def _place_shard(w, chip1, kind, name):
    r, c = w.shape
    tr = _rows_tile(r, c, 16)
    nt = r // tr

    def body(chip_ref, w_ref, o_ref):
        o_ref[...] = w_ref[...].astype(BF16)

    out_map = (lambda i, chip_ref: (chip_ref[0] * nt + i, 0)) if kind == "row" else (lambda i, chip_ref: (i, chip_ref[0]))
    return pl.pallas_call(
        body, name="place_" + name,
        grid_spec=pltpu.PrefetchScalarGridSpec(
            num_scalar_prefetch=1, grid=(nt,),
            in_specs=[pl.BlockSpec((tr, c), lambda i, chip_ref: (i, 0))],
            out_specs=pl.BlockSpec((tr, c), out_map)),
        out_shape=jax.ShapeDtypeStruct(_full_shape(kind, (r, c)), BF16),
        compiler_params=_cp(("arbitrary",), 32),
    )(chip1, w)


def _allgather_weights(placed, geom, small_shards):
    nb, ns = len(BIG), len(small_shards)
    small_w = [a.shape[1] for a in small_shards]

    def body(*refs):
        sm = refs[nb:nb + ns]
        full = refs[nb + ns:2 * nb + ns]
        smf = refs[2 * nb + ns:2 * nb + 2 * ns]
        send, recv, loc = refs[2 * nb + 2 * ns:]
        x, y, c = _me()
        chip = 2 * x + y
        sib = (x, y, 1 - c)
        others = _other_chips(x, y)
        locals_, sends = [], []
        for m, name in enumerate(BIG):
            kind, shard, _ = geom[name]
            mine = _piece_of(full[m], kind, shard, chip, c)
            for k, (qx, qy) in enumerate(others):
                cp = _remote(mine, mine, send.at[6 * m + k], recv.at[6 * m + k], (qx, qy, c))
                cp.start()
                sends.append(cp)
        for t in range(ns):
            cp = pltpu.make_async_copy(sm[t], smf[t].at[:, pl.ds(chip * small_w[t], small_w[t])], loc.at[t])
            cp.start()
            locals_.append(cp)
            for k, (qx, qy) in enumerate(others):
                cp = _remote(sm[t], smf[t].at[:, pl.ds(chip * small_w[t], small_w[t])],
                             send.at[6 * nb + 3 * t + k], recv.at[6 * nb + 3 * t + k], (qx, qy, c))
                cp.start()
                sends.append(cp)
        for m, name in enumerate(BIG):
            kind, shard, _ = geom[name]
            for k, (qx, qy) in enumerate(others):
                got = _piece_of(full[m], kind, shard, 2 * qx + qy, c)
                _remote(got, got, send.at[6 * m + k], recv.at[6 * m + k], (qx, qy, c)).wait_recv()
                cp = _remote(got, got, send.at[6 * m + 3 + k], recv.at[6 * m + 3 + k], sib)
                cp.start()
                sends.append(cp)
        for t in range(ns):
            for k, (qx, qy) in enumerate(others):
                got = smf[t].at[:, pl.ds((2 * qx + qy) * small_w[t], small_w[t])]
                _remote(got, got, send.at[6 * nb + 3 * t + k], recv.at[6 * nb + 3 * t + k], (qx, qy, c)).wait_recv()
        for m, name in enumerate(BIG):
            kind, shard, _ = geom[name]
            for k, (qx, qy) in enumerate(others):
                got = _piece_of(full[m], kind, shard, 2 * qx + qy, 1 - c)
                _remote(got, got, send.at[6 * m + 3 + k], recv.at[6 * m + 3 + k], sib).wait_recv()
        for cp in sends:
            cp.wait_send()
        for cp in locals_:
            cp.wait()

    nsem = 6 * nb + 3 * ns
    out_shape = [jax.ShapeDtypeStruct(placed[n].shape, BF16) for n in BIG]
    out_shape += [jax.ShapeDtypeStruct((8, N_CHIPS * w), F32) for w in small_w]
    outs = pl.pallas_call(
        body, name="allgather_weights", in_specs=[ANY] * (nb + ns), out_specs=[ANY] * (nb + ns), out_shape=out_shape,
        input_output_aliases={m: m for m in range(nb)},
        scratch_shapes=[pltpu.SemaphoreType.DMA((nsem,)), pltpu.SemaphoreType.DMA((nsem,)),
                        pltpu.SemaphoreType.DMA((ns,))],
    )(*[placed[n] for n in BIG], *small_shards)
    return dict(zip(BIG, outs[:nb])), list(outs[nb:])


def _sibling_exchange(grads, geom):
    nb = len(BIG)

    def body(*refs):
        g = refs[:nb]
        got = refs[nb:2 * nb]
        send, recv = refs[2 * nb:]
        x, y, c = _me()
        sib = (x, y, 1 - c)
        cps = []
        for m, name in enumerate(BIG):
            kind, shard, _ = geom[name]
            for r in range(N_CHIPS):
                cp = _remote(_piece_of(g[m], kind, shard, r, 1 - c), got[m].at[r],
                             send.at[N_CHIPS * m + r], recv.at[N_CHIPS * m + r], sib)
                cp.start()
                cps.append(cp)
        for cp in cps:
            cp.wait_recv()
        for cp in cps:
            cp.wait_send()

    return pl.pallas_call(
        body, name="grad_sibling_exchange", in_specs=[ANY] * nb, out_specs=[ANY] * nb,
        out_shape=[jax.ShapeDtypeStruct((N_CHIPS,) + geom[n][2], BF16) for n in BIG],
        scratch_shapes=[pltpu.SemaphoreType.DMA((N_CHIPS * nb,)), pltpu.SemaphoreType.DMA((N_CHIPS * nb,))],
    )(*[grads[n] for n in BIG])


def _chip_exchange(chip_sums, geom):
    nb = len(BIG)

    def body(*refs):
        t = refs[:nb]
        got = refs[nb:2 * nb]
        send, recv = refs[2 * nb:]
        x, y, c = _me()
        cps = []
        for m in range(nb):
            for k, (qx, qy) in enumerate(_other_chips(x, y)):
                cp = _remote(t[m].at[2 * qx + qy], got[m].at[k], send.at[3 * m + k], recv.at[3 * m + k], (qx, qy, c))
                cp.start()
                cps.append(cp)
        for cp in cps:
            cp.wait_recv()
        for cp in cps:
            cp.wait_send()

    return pl.pallas_call(
        body, name="grad_chip_exchange", in_specs=[ANY] * nb, out_specs=[ANY] * nb,
        out_shape=[jax.ShapeDtypeStruct((N_CHIPS - 1,) + geom[n][2], BF16) for n in BIG],
        scratch_shapes=[pltpu.SemaphoreType.DMA((3 * nb,)), pltpu.SemaphoreType.DMA((3 * nb,))],
    )(*chip_sums)


def _sibling_assemble(shards, geom):
    nb = len(BIG)

    def body(*refs):
        full = refs[nb:2 * nb]
        send, recv = refs[2 * nb:]
        x, y, c = _me()
        sib = (x, y, 1 - c)
        cps = []
        for m, name in enumerate(BIG):
            mine = _shard_piece(full[m], geom[name][1], c)
            cp = _remote(mine, mine, send.at[m], recv.at[m], sib)
            cp.start()
            cps.append(cp)
        for m, name in enumerate(BIG):
            theirs = _shard_piece(full[m], geom[name][1], 1 - c)
            _remote(theirs, theirs, send.at[m], recv.at[m], sib).wait_recv()
        for cp in cps:
            cp.wait_send()

    return pl.pallas_call(
        body, name="grad_sibling_assemble", in_specs=[ANY] * nb, out_specs=[ANY] * nb,
        out_shape=[jax.ShapeDtypeStruct(geom[n][1], F32) for n in BIG],
        input_output_aliases={m: m for m in range(nb)},
        scratch_shapes=[pltpu.SemaphoreType.DMA((nb,)), pltpu.SemaphoreType.DMA((nb,))],
    )(*shards)


def _allreduce_small(part):
    rows = part.shape[0]
    flips = [(a, b, e) for a in (0, 1) for b in (0, 1) for e in (0, 1) if (a, b, e) != (0, 0, 0)]

    def body(p_ref, o_ref, all_ref, send, recv):
        x, y, c = _me()
        me = 4 * x + 2 * y + c
        all_ref[me] = p_ref[...]
        cps = []
        for k, (a, b, e) in enumerate(flips):
            cp = _remote(p_ref, all_ref.at[me], send.at[k], recv.at[k], (x ^ a, y ^ b, c ^ e))
            cp.start()
            cps.append(cp)
        for k, (a, b, e) in enumerate(flips):
            peer = 4 * (x ^ a) + 2 * (y ^ b) + (c ^ e)
            _remote(p_ref, all_ref.at[peer], send.at[k], recv.at[k], (x ^ a, y ^ b, c ^ e)).wait_recv()
        for cp in cps:
            cp.wait_send()
        tot = all_ref[0]
        for dev in range(1, 8):
            tot = tot + all_ref[dev]
        o_ref[...] = tot

    vm = pl.BlockSpec(memory_space=pltpu.VMEM)
    return pl.pallas_call(
        body, name="allreduce_small", in_specs=[vm], out_specs=vm, out_shape=jax.ShapeDtypeStruct((rows, LANES), F32),
        scratch_shapes=[pltpu.VMEM((8, rows, LANES), F32), pltpu.SemaphoreType.DMA((7,)), pltpu.SemaphoreType.DMA((7,))],
    )(part)


def _rows_tile(rows, cols, mult):
    return _pick(rows, max(mult, (1 << 19) // cols // mult * mult), mult)


def _add_pairs(g, got, kind, shard, where, name):
    p, r, c = got.shape
    tr = _rows_tile(r, c, 16)
    nt = r // tr

    def body(w_ref, a_ref, b_ref, o_ref):
        o_ref[...] = (a_ref[...].astype(F32) + b_ref[...].astype(F32)).astype(BF16)

    if kind == "row":
        g_map = lambda q, i, w_ref: ((2 * q + w_ref[1]) * nt + i, 0)
    else:
        g_map = lambda q, i, w_ref: (w_ref[1] * nt + i, q)
    spec = pl.BlockSpec((None, tr, c), lambda q, i, w_ref: (q, i, 0))
    return pl.pallas_call(
        body, name="grad_add_sibling_" + name,
        grid_spec=pltpu.PrefetchScalarGridSpec(
            num_scalar_prefetch=1, grid=(p, nt), in_specs=[pl.BlockSpec((tr, c), g_map), spec], out_specs=spec),
        out_shape=jax.ShapeDtypeStruct((p, r, c), BF16), compiler_params=_cp(("arbitrary", "arbitrary"), 32),
    )(where, g, got)


def _add_four(t, got, shard, where, name):
    _, r, c = t.shape
    tr = _rows_tile(r, c, 16)
    nt = r // tr

    def body(w_ref, own, t0, t1, t2, o_ref):
        o_ref[...] = ((own[...].astype(F32) + t0[...].astype(F32)) + t1[...].astype(F32)) + t2[...].astype(F32)

    spec = lambda q: pl.BlockSpec((None, tr, c), lambda i, w_ref: (q, i, 0))
    return pl.pallas_call(
        body, name="grad_add_chips_" + name,
        grid_spec=pltpu.PrefetchScalarGridSpec(
            num_scalar_prefetch=1, grid=(nt,),
            in_specs=[pl.BlockSpec((None, tr, c), lambda i, w_ref: (w_ref[0], i, 0)), spec(0), spec(1), spec(2)],
            out_specs=pl.BlockSpec((tr, c), lambda i, w_ref: (w_ref[1] * nt + i, 0))),
        out_shape=jax.ShapeDtypeStruct(shard, F32), compiler_params=_cp(("arbitrary",), 32),
    )(where, t, got, got, got)


def _adamw(w, g, m, v, name):
    r, c = w.shape
    tr = _rows_tile(r, c, 8)

    def body(w_ref, g_ref, m_ref, v_ref, go_ref, d_ref, mo_ref, vo_ref):
        gv = g_ref[...]
        mn = ADAM_B1 * m_ref[...] + (1.0 - ADAM_B1) * gv
        vn = ADAM_B2 * v_ref[...] + (1.0 - ADAM_B2) * (gv * gv)
        m_hat = mn / (1.0 - ADAM_B1 ** ADAM_STEP)
        v_hat = vn / (1.0 - ADAM_B2 ** ADAM_STEP)
        go_ref[...] = gv
        d_ref[...] = -ADAM_LR * (m_hat / (jnp.sqrt(v_hat) + ADAM_EPS) + ADAM_WD * w_ref[...])
        mo_ref[...] = mn
        vo_ref[...] = vn

    spec = pl.BlockSpec((tr, c), lambda i: (i, 0))
    return pl.pallas_call(
        body, name=name, grid=(r // tr,), in_specs=[spec] * 4, out_specs=[spec] * 4,
        out_shape=[jax.ShapeDtypeStruct((r, c), F32)] * 4, compiler_params=_cp(("arbitrary",), 32),
    )(w, g, m, v)


def _pack(vectors, rows):
    flat = jnp.concatenate([v.reshape(-1).astype(F32) for v in vectors])
    return jnp.pad(flat, (0, rows * LANES - flat.shape[0])).reshape(rows, LANES)


def _unpack(packed, shapes):
    flat = packed.reshape(-1)
    out, off = [], 0
    for shp in shapes:
        n = 1
        for t in shp:
            n *= t
        out.append(flat[off:off + n].reshape(shp))
        off += n
    return out


def _rows_for(shapes):
    n = sum(functools.reduce(lambda a, b: a * b, shp, 1) for shp in shapes)
    return -(-n // (8 * LANES)) * 8


def kernel(x, w_in, attn_sinks, short_conv_w, w_out, ln1_g, ln1_b, ffn_w_up, ffn_conv_w, ffn_w_down, ln2_g, ln2_b, loss_target, m_w_in, m_attn_sinks, m_short_conv_w, m_w_out, m_ln1_g, m_ln1_b, m_ffn_w_up, m_ffn_conv_w, m_ffn_w_down, m_ln2_g, m_ln2_b, v_w_in, v_attn_sinks, v_short_conv_w, v_w_out, v_ln1_g, v_ln1_b, v_ffn_w_up, v_ffn_conv_w, v_ffn_w_down, v_ln2_g, v_ln2_b):
    xs, tgt = x[0], loss_target[0]
    s, d = xs.shape
    chip = 2 * lax.axis_index("x") + lax.axis_index("y")

    t_in = lambda a: a[0].T
    w_big = dict(win_t=t_in(w_in), wout=w_out[0], wup=ffn_w_up[0], wdown=ffn_w_down[0])
    m_big = dict(win_t=t_in(m_w_in), wout=m_w_out[0], wup=m_ffn_w_up[0], wdown=m_ffn_w_down[0])
    v_big = dict(win_t=t_in(v_w_in), wout=v_w_out[0], wup=v_ffn_w_up[0], wdown=v_ffn_w_down[0])
    geom = _geom({n: w_big[n].shape for n in BIG})
    pad8 = lambda a: jnp.pad(a[0], ((0, 5), (0, 0)))
    where = jnp.stack([chip, lax.axis_index("c")]).astype(jnp.int32)
    placed = {n: _place_shard(w_big[n], where[:1], geom[n][0], n) for n in BIG}
    full, (scw8, fcw8) = _allgather_weights(placed, geom, [pad8(short_conv_w), pad8(ffn_conv_w)])

    grad_x, g_big, g_small = _local_step(xs, tgt, full["win_t"], full["wout"], full["wup"], full["wdown"], scw8, fcw8,
                                         attn_sinks, ln1_g, ln1_b, ln2_g, ln2_b)

    from_sibling = _sibling_exchange(g_big, geom)
    chip_sums = [_add_pairs(g_big[n], from_sibling[m], geom[n][0], geom[n][1], where, n) for m, n in enumerate(BIG)]
    from_chips = _chip_exchange(chip_sums, geom)
    halves = [_add_four(chip_sums[m], from_chips[m], geom[n][1], where, n) for m, n in enumerate(BIG)]
    g_shards = dict(zip(BIG, _sibling_assemble(halves, geom)))

    small_names = ("ln1_g", "ln1_b", "ln2_g", "ln2_b", "sinks", "fcw", "scw")
    small_shapes = [g_small[n].shape for n in small_names]
    red = _allreduce_small(_pack([g_small["loss_sq"].reshape(1)] + [g_small[n] for n in small_names],
                                 _rows_for([(1,)] + small_shapes)))
    loss_sq, *gs = _unpack(red, [(1,)] + small_shapes)
    gs = dict(zip(small_names, gs))
    loss = (0.5 / d) * loss_sq[0]
    fw, sw = ffn_conv_w.shape[2], short_conv_w.shape[2]
    gs["fcw"] = lax.dynamic_slice_in_dim(gs["fcw"], chip * fw, fw, axis=1)
    gs["scw"] = lax.dynamic_slice_in_dim(gs["scw"], chip * sw, sw, axis=1)

    upd = {n: _adamw(w_big[n], g_shards[n], m_big[n], v_big[n], "adamw_" + n) for n in BIG}
    upd["win_t"] = tuple(a.T for a in upd["win_t"])
    sm_w = dict(ln1_g=ln1_g[0], ln1_b=ln1_b[0], ln2_g=ln2_g[0], ln2_b=ln2_b[0], sinks=attn_sinks[0],
                fcw=ffn_conv_w[0], scw=short_conv_w[0])
    sm_m = dict(ln1_g=m_ln1_g[0], ln1_b=m_ln1_b[0], ln2_g=m_ln2_g[0], ln2_b=m_ln2_b[0], sinks=m_attn_sinks[0],
                fcw=m_ffn_conv_w[0], scw=m_short_conv_w[0])
    sm_v = dict(ln1_g=v_ln1_g[0], ln1_b=v_ln1_b[0], ln2_g=v_ln2_g[0], ln2_b=v_ln2_b[0], sinks=v_attn_sinks[0],
                fcw=v_ffn_conv_w[0], scw=v_short_conv_w[0])
    shapes = [sm_w[n].shape for n in small_names]
    rows = _rows_for(shapes)
    packed = [_pack([t[n] for n in small_names], rows) for t in (sm_w, gs, sm_m, sm_v)]
    sm_out = [dict(zip(small_names, _unpack(a, shapes))) for a in _adamw(*packed, "adamw_small")]

    def leaf(kind, name):
        if name in ("w_in", "w_out", "ffn_w_up", "ffn_w_down"):
            key = dict(w_in="win_t", w_out="wout", ffn_w_up="wup", ffn_w_down="wdown")[name]
            return upd[key][kind][None]
        key = dict(attn_sinks="sinks", short_conv_w="scw", ffn_conv_w="fcw").get(name, name)
        return sm_out[kind][key][None]

    order = ("w_in", "attn_sinks", "short_conv_w", "w_out", "ln1_g", "ln1_b", "ffn_w_up", "ffn_conv_w", "ffn_w_down",
             "ln2_g", "ln2_b")
    outs = [loss, grad_x[None]]
    for kind in range(4):
        outs += [leaf(kind, n) for n in order]
    return tuple(outs)
```

```python
import functools

import jax
import jax.numpy as jnp
from jax import lax
from jax.experimental import pallas as pl
from jax.experimental.pallas import tpu as pltpu

F32 = jnp.float32
BF16 = jnp.bfloat16
MESH = pl.DeviceIdType.MESH
ANY = pl.BlockSpec(memory_space=pl.ANY)

HEAD_DIM = 64
N_Q_HEADS = 16
N_KV_HEADS = 2
ATTN_WIDTH = N_Q_HEADS * HEAD_DIM
KV_WIDTH = N_KV_HEADS * HEAD_DIM
BLOCK = 128
ROPE_THETA = 10000.0
LN_EPS = 1e-5
ALPHA = 2.0 ** 0.25
NEG_INF = -1e30
ADAM_LR, ADAM_B1, ADAM_B2, ADAM_EPS, ADAM_WD, ADAM_STEP = 0.001, 0.9, 0.999, 1e-08, 0.01, 10
N_CHIPS = 4
LANES = 128
SLAB = 128


def _cp(sem, vmem_mb):
    return pltpu.CompilerParams(dimension_semantics=sem, vmem_limit_bytes=vmem_mb << 20)


def _matmul(a, b, *, mode, m, n, k, tm, tn, tk, out_dtype, name, vmem_mb, a_spec=None, b_spec=None,
            res=None, alpha=1.0):
    nj, ni, nk = n // tn, m // tm, k // tk
    assert nj * tn == n and ni * tm == m and nk * tk == k, (name, m, n, k, tm, tn, tk)
    if mode == "nn":
        dims = ((1,), (0,))
        a_spec = a_spec or pl.BlockSpec((tm, tk), lambda j, i, kk: (i, kk))
        b_spec = b_spec or pl.BlockSpec((tk, tn), lambda j, i, kk: (kk, j))
    elif mode == "nt":
        dims = ((1,), (1,))
        a_spec = a_spec or pl.BlockSpec((tm, tk), lambda j, i, kk: (i, kk))
        b_spec = b_spec or pl.BlockSpec((tn, tk), lambda j, i, kk: (j, kk))
    else:
        dims = ((0,), (0,))
        a_spec = a_spec or pl.BlockSpec((tk, tm), lambda j, i, kk: (kk, i))
        b_spec = b_spec or pl.BlockSpec((tk, tn), lambda j, i, kk: (kk, j))
    has_res = res is not None

    def body(*refs):
        a_ref, b_ref = refs[0], refs[1]
        res_ref = refs[2] if has_res else None
        o_ref = refs[2 + has_res]
        part = lax.dot_general(a_ref[...], b_ref[...], (dims, ((), ())), preferred_element_type=F32)

        def finish(acc):
            if has_res:
                acc = acc + alpha * res_ref[...].astype(F32)
            o_ref[...] = acc.astype(o_ref.dtype)

        if nk == 1:
            finish(part)
        else:
            acc_ref = refs[3 + has_res]
            kk = pl.program_id(2)

            @pl.when(kk == 0)
            def _():
                acc_ref[...] = part

            @pl.when(kk > 0)
            def _():
                acc_ref[...] += part

            @pl.when(kk == nk - 1)
            def _():
                finish(acc_ref[...])

    in_specs = [a_spec, b_spec]
    args = [a, b]
    if has_res:
        in_specs.append(pl.BlockSpec((tm, tn), lambda j, i, kk: (i, j)))
        args.append(res)
    return pl.pallas_call(
        body, name=name, grid=(nj, ni, nk), in_specs=in_specs,
        out_specs=pl.BlockSpec((tm, tn), lambda j, i, kk: (i, j)),
        out_shape=jax.ShapeDtypeStruct((m, n), out_dtype),
        scratch_shapes=[pltpu.VMEM((tm, tn), F32)] if nk > 1 else [],
        compiler_params=_cp(("arbitrary", "arbitrary", "arbitrary"), vmem_mb),
    )(*args)


def _pick(total, want, mult):
    if total <= want:
        return total
    for t in range(want, 0, -1):
        if total % t == 0 and t % mult == 0:
            return t
    return total


def _rope_tables(s):
    half = HEAD_DIM // 2
    inv_freq = ROPE_THETA ** (-jnp.arange(half, dtype=F32) / half)
    ang = jnp.arange(s, dtype=F32)[:, None] * inv_freq[None, :]
    cos = jnp.tile(jnp.cos(ang), (1, LANES // half))
    sin = jnp.tile(jnp.concatenate([-jnp.sin(ang), jnp.sin(ang)], axis=1), (1, LANES // HEAD_DIM))
    return cos, sin


def _rope(x, cos, sin, lo):
    partner = jnp.where(lo, pltpu.roll(x, LANES - HEAD_DIM // 2, 1), pltpu.roll(x, HEAD_DIM // 2, 1))
    return x * cos + partner * sin


def _dot(a, b, dims):
    return lax.dot_general(a, b, (dims, ((), ())), preferred_element_type=F32)


NN, NT, TN = ((1,), (0,)), ((1,), (1,)), ((0,), (0,))


def _kv_variants(t, head_lo):
    r = pltpu.roll(t, HEAD_DIM, 1)
    zero = jnp.zeros_like(t)
    a = (jnp.where(head_lo, t, zero).astype(BF16), jnp.where(head_lo, r, zero).astype(BF16))
    b = (jnp.where(head_lo, zero, r).astype(BF16), jnp.where(head_lo, zero, t).astype(BF16))
    return a, b


PAIRS_PER_KV = N_Q_HEADS // 2 // N_KV_HEADS
STACK = PAIRS_PER_KV * BLOCK


def _stack_pairs(ref, j, fn):
    return jnp.concatenate([fn(ref[:, p * LANES:(p + 1) * LANES])
                            for p in range(j * PAIRS_PER_KV, (j + 1) * PAIRS_PER_KV)], axis=0)


def _stack_sinks(sink_ref, j, hh):
    return jnp.concatenate([jnp.full((BLOCK, 1), sink_ref[0, 2 * p + hh], F32)
                            for p in range(j * PAIRS_PER_KV, (j + 1) * PAIRS_PER_KV)], axis=0)


def _attn_probs(qp, ka, kb, valid, sink_a, sink_b):
    out = []
    for kk, sink in ((ka, sink_a), (kb, sink_b)):
        s = jnp.where(valid, _dot(qp, kk, NT), NEG_INF)
        mx = jnp.maximum(jnp.max(s, axis=1, keepdims=True), sink)
        e = jnp.exp(s - mx)
        es = jnp.exp(sink - mx)
        inv = 1.0 / (jnp.sum(e, axis=1, keepdims=True) + es)
        out.append((e * inv, es * inv))
    return out


def _attn_common(i, q_ref, k_ref, v_ref, kp_ref, vp_ref, cos_ref, sin_ref, cosp_ref, sinp_ref):
    lane = lax.broadcasted_iota(jnp.int32, (1, LANES), 1)
    lo = (lane % HEAD_DIM) < (HEAD_DIM // 2)
    head_lo = lane < HEAD_DIM
    cos, sin = cos_ref[...], sin_ref[...]
    kc = _rope(k_ref[...].astype(F32), cos, sin, lo)
    kp = _rope(kp_ref[...].astype(F32), cosp_ref[...], sinp_ref[...], lo)
    kext = jnp.concatenate([kp, kc], axis=0)
    vext = jnp.concatenate([vp_ref[...].astype(F32), v_ref[...].astype(F32)], axis=0)
    ka, kb = _kv_variants(kext, head_lo)
    va, vb = _kv_variants(vext, head_lo)
    qi = lax.broadcasted_iota(jnp.int32, (STACK, 1), 0) % BLOCK
    kj = lax.broadcasted_iota(jnp.int32, (1, 2 * BLOCK), 1)
    valid = (kj > qi) & (kj <= qi + BLOCK) & ((kj >= BLOCK) | (i > 0))
    cos4 = jnp.concatenate([cos] * PAIRS_PER_KV, axis=0)
    sin4 = jnp.concatenate([sin] * PAIRS_PER_KV, axis=0)
    return lo, head_lo, cos, sin, cos4, sin4, ka, kb, va, vb, valid


def _attn_fwd(proj, sinks, cos, sin, s):
    nb = s // BLOCK
    kcol, vcol = ATTN_WIDTH // LANES, ATTN_WIDTH // LANES + 1

    def body(q_ref, k_ref, v_ref, kp_ref, vp_ref, cos_ref, sin_ref, cosp_ref, sinp_ref, sink_ref, o_ref):
        i = pl.program_id(0)
        lo, head_lo, cs, sn, cs4, sn4, ka, kb, va, vb, valid = _attn_common(
            i, q_ref, k_ref, v_ref, kp_ref, vp_ref, cos_ref, sin_ref, cosp_ref, sinp_ref)
        for j in range(N_KV_HEADS):
            q4 = _stack_pairs(q_ref, j, lambda t: t.astype(F32))
            qp = (_rope(q4, cs4, sn4, lo) * HEAD_DIM ** -0.5).astype(BF16)
            (pa, _), (pb, _) = _attn_probs(qp, ka[j], kb[j], valid, _stack_sinks(sink_ref, j, 0),
                                           _stack_sinks(sink_ref, j, 1))
            o = (_dot(pa.astype(BF16), va[j], NN) + _dot(pb.astype(BF16), vb[j], NN)).astype(BF16)
            for t in range(PAIRS_PER_KV):
                p = j * PAIRS_PER_KV + t
                o_ref[:, p * LANES:(p + 1) * LANES] = o[t * BLOCK:(t + 1) * BLOCK]

    prev = lambda i: (jnp.maximum(i - 1, 0), 0)
    return pl.pallas_call(
        body, name="attn_fwd", grid=(nb,),
        in_specs=[pl.BlockSpec((BLOCK, ATTN_WIDTH), lambda i: (i, 0)),
                  pl.BlockSpec((BLOCK, LANES), lambda i: (i, kcol)),
                  pl.BlockSpec((BLOCK, LANES), lambda i: (i, vcol)),
                  pl.BlockSpec((BLOCK, LANES), lambda i: (jnp.maximum(i - 1, 0), kcol)),
                  pl.BlockSpec((BLOCK, LANES), lambda i: (jnp.maximum(i - 1, 0), vcol)),
                  pl.BlockSpec((BLOCK, LANES), lambda i: (i, 0)),
                  pl.BlockSpec((BLOCK, LANES), lambda i: (i, 0)),
                  pl.BlockSpec((BLOCK, LANES), prev),
                  pl.BlockSpec((BLOCK, LANES), prev),
                  pl.BlockSpec(memory_space=pltpu.SMEM)],
        out_specs=pl.BlockSpec((BLOCK, ATTN_WIDTH), lambda i: (i, 0)),
        out_shape=jax.ShapeDtypeStruct((s, ATTN_WIDTH), BF16),
        compiler_params=_cp(("arbitrary",), 32),
    )(proj, proj, proj, proj, proj, cos, sin, cos, sin, sinks)


def _attn_bwd(proj, dmix, sinks, cos, sin, s):
    nb = s // BLOCK
    kcol, vcol = ATTN_WIDTH // LANES, ATTN_WIDTH // LANES + 1
    pairs_per_kv = N_Q_HEADS // 2 // N_KV_HEADS

    def body(q_ref, k_ref, v_ref, kp_ref, vp_ref, cos_ref, sin_ref, cosp_ref, sinp_ref, sink_ref, do_ref,
             dq_ref, dk_ref, dv_ref, dsink_ref, ck_ref, cv_ref):
        g = pl.program_id(0)
        i = nb - 1 - g

        @pl.when(g == 0)
        def _():
            ck_ref[...] = jnp.zeros_like(ck_ref)
            cv_ref[...] = jnp.zeros_like(cv_ref)
            dsink_ref[...] = jnp.zeros_like(dsink_ref)

        lo, head_lo, cs, sn, cs4, sn4, ka, kb, va, vb, valid = _attn_common(
            i, q_ref, k_ref, v_ref, kp_ref, vp_ref, cos_ref, sin_ref, cosp_ref, sinp_ref)
        lane = lax.broadcasted_iota(jnp.int32, (1, LANES), 1)
        dk_j, dv_j = [], []
        dsink = jnp.zeros((1, LANES), F32)
        for j in range(N_KV_HEADS):
            q4 = _stack_pairs(q_ref, j, lambda t: t.astype(F32))
            qp = (_rope(q4, cs4, sn4, lo) * HEAD_DIM ** -0.5).astype(BF16)
            probs = _attn_probs(qp, ka[j], kb[j], valid, _stack_sinks(sink_ref, j, 0), _stack_sinks(sink_ref, j, 1))
            do = _stack_pairs(do_ref, j, lambda t: t)
            dq_r = jnp.zeros((STACK, LANES), F32)
            dkc, dvc = [], []
            for hh, ((pr, ps), kk, vv) in enumerate(zip(probs, (ka[j], kb[j]), (va[j], vb[j]))):
                dp = _dot(do, vv, NT)
                delta = jnp.sum(pr * dp, axis=1, keepdims=True)
                ds = (pr * (dp - delta)).astype(BF16)
                psd = ps * delta
                for t in range(PAIRS_PER_KV):
                    head = 2 * (j * PAIRS_PER_KV + t) + hh
                    dsink = dsink + jnp.where(
                        lane == head, -jnp.sum(psd[t * BLOCK:(t + 1) * BLOCK], axis=0, keepdims=True), 0.0)
                dq_r = dq_r + _dot(ds, kk, NN)
                dkc.append(_dot(ds, qp, TN))
                dvc.append(_dot(pr.astype(BF16), do, TN))
            dk_j.append(jnp.where(head_lo, dkc[0], dkc[1]))
            dv_j.append(jnp.where(head_lo, dvc[0], dvc[1]))
            dq = _rope(dq_r * HEAD_DIM ** -0.5, cs4, -sn4, lo).astype(BF16)
            for t in range(PAIRS_PER_KV):
                p = j * PAIRS_PER_KV + t
                dq_ref[:, p * LANES:(p + 1) * LANES] = dq[t * BLOCK:(t + 1) * BLOCK]
        tot_k = [t + pltpu.roll(t, HEAD_DIM, 1) for t in dk_j]
        tot_v = [t + pltpu.roll(t, HEAD_DIM, 1) for t in dv_j]
        dkext = jnp.where(head_lo, tot_k[0], tot_k[1])
        dvext = jnp.where(head_lo, tot_v[0], tot_v[1])
        dk_r = dkext[BLOCK:] + ck_ref[...]
        dk_ref[...] = _rope(dk_r, cs, -sn, lo).astype(BF16)
        dv_ref[...] = (dvext[BLOCK:] + cv_ref[...]).astype(BF16)
        ck_ref[...] = dkext[:BLOCK]
        cv_ref[...] = dvext[:BLOCK]
        dsink_ref[0:1, :] += dsink

    cur = lambda col: (lambda g: (nb - 1 - g, col))
    prv = lambda col: (lambda g: (jnp.maximum(nb - 2 - g, 0), col))
    blk = lambda w, f: pl.BlockSpec((BLOCK, w), f)
    return pl.pallas_call(
        body, name="attn_bwd", grid=(nb,),
        in_specs=[blk(ATTN_WIDTH, cur(0)), blk(LANES, cur(kcol)), blk(LANES, cur(vcol)),
                  blk(LANES, prv(kcol)), blk(LANES, prv(vcol)),
                  blk(LANES, cur(0)), blk(LANES, cur(0)), blk(LANES, prv(0)), blk(LANES, prv(0)),
                  pl.BlockSpec(memory_space=pltpu.SMEM),
                  blk(ATTN_WIDTH, cur(0))],
        out_specs=[blk(ATTN_WIDTH, cur(0)), blk(LANES, cur(0)), blk(LANES, cur(0)),
                   pl.BlockSpec((8, LANES), lambda g: (0, 0))],
        out_shape=[jax.ShapeDtypeStruct((s, ATTN_WIDTH), BF16), jax.ShapeDtypeStruct((s, LANES), BF16),
                   jax.ShapeDtypeStruct((s, LANES), BF16), jax.ShapeDtypeStruct((8, LANES), F32)],
        scratch_shapes=[pltpu.VMEM((BLOCK, LANES), F32), pltpu.VMEM((BLOCK, LANES), F32)],
        compiler_params=_cp(("arbitrary",), 32),
    )(proj, proj, proj, proj, proj, cos, sin, cos, sin, sinks, dmix)


def _causal_conv(x, prev8, w):
    row = lax.broadcasted_iota(jnp.int32, (8, 1), 0)
    r1, r2 = pltpu.roll(x, 1, 0), pltpu.roll(x, 2, 0)
    s1 = jnp.concatenate([jnp.where(row == 0, prev8[7:8], r1[:8]), r1[8:]], axis=0)
    s2 = jnp.concatenate([jnp.where(row == 0, prev8[6:7], jnp.where(row == 1, prev8[7:8], r2[:8])), r2[8:]], axis=0)
    return w[0:1] * s2 + w[1:2] * s1 + w[2:3] * x, s1, s2


def _conv_bwd(dy, x, s1, s2, w, next8):
    r = x.shape[0]
    row = lax.broadcasted_iota(jnp.int32, (8, 1), 0)
    r1, r2 = pltpu.roll(dy, r - 1, 0), pltpu.roll(dy, r - 2, 0)
    n1 = jnp.concatenate([r1[:r - 8], jnp.where(row == 7, next8[0:1], r1[r - 8:])], axis=0)
    n2 = jnp.concatenate([r2[:r - 8], jnp.where(row == 6, next8[0:1], jnp.where(row == 7, next8[1:2], r2[r - 8:]))],
                         axis=0)
    dx = w[2:3] * dy + w[1:2] * n1 + w[0:1] * n2
    dws = [jnp.sum(dy * t, axis=0, keepdims=True) for t in (s2, s1, x)]
    return dx, dws


CONV_COLS = 256


def _convmix_cols(d):
    conv_w = d - ATTN_WIDTH
    base = (ATTN_WIDTH + 2 * KV_WIDTH) // CONV_COLS
    step = conv_w // CONV_COLS
    return base, base + step, base + 2 * step, step


def _convmix_fwd(proj, scw8, s, d):
    gb0, gc0, h0, ncb = _convmix_cols(d)
    tr = _pick(s, 1024, 16)
    ni = s // tr

    def body(gb_ref, gc_ref, h_ref, w_ref, o_ref, carry_ref):
        @pl.when(pl.program_id(1) == 0)
        def _():
            carry_ref[...] = jnp.zeros_like(carry_ref)

        gch = gc_ref[...].astype(F32) * h_ref[...].astype(F32)
        cc, _, _ = _causal_conv(gch, carry_ref[...], w_ref[...])
        o_ref[...] = (gb_ref[...].astype(F32) * cc).astype(BF16)
        carry_ref[...] = gch[tr - 8:]

    spec = lambda c0: pl.BlockSpec((tr, CONV_COLS), lambda j, i: (i, c0 + j))
    return pl.pallas_call(
        body, name="convmix_fwd", grid=(ncb, ni),
        in_specs=[spec(gb0), spec(gc0), spec(h0), pl.BlockSpec((8, CONV_COLS), lambda j, i: (0, j))],
        out_specs=pl.BlockSpec((tr, CONV_COLS), lambda j, i: (i, j)),
        out_shape=jax.ShapeDtypeStruct((s, d - ATTN_WIDTH), BF16),
        scratch_shapes=[pltpu.VMEM((8, CONV_COLS), F32)],
        compiler_params=_cp(("arbitrary", "arbitrary"), 32),
    )(proj, proj, proj, scw8)


def _convmix_bwd(proj, dmix, scw8, s, d):
    gb0, gc0, h0, ncb = _convmix_cols(d)
    tr = _pick(s, 1024, 16)
    ni = s // tr
    dc0 = ATTN_WIDTH // CONV_COLS

    def body(dc_ref, gb_ref, gc_ref, h_ref, gcp_ref, hp_ref, w_ref, d3_ref, dw_ref, nxt_ref):
        g = pl.program_id(1)
        i = ni - 1 - g

        @pl.when(g == 0)
        def _():
            nxt_ref[...] = jnp.zeros_like(nxt_ref)
            dw_ref[...] = jnp.zeros_like(dw_ref)

        w = w_ref[...]
        gb, gc, h = gb_ref[...].astype(F32), gc_ref[...].astype(F32), h_ref[...].astype(F32)
        gch = gc * h
        prev8 = (gcp_ref[...].astype(F32) * hp_ref[...].astype(F32))[8:16] * (i > 0).astype(F32)
        cc, s1, s2 = _causal_conv(gch, prev8, w)
        dc = dc_ref[...].astype(F32)
        dcc = dc * gb
        dgch, dws = _conv_bwd(dcc, gch, s1, s2, w, nxt_ref[...])
        d3_ref[0] = (dc * cc).astype(BF16)
        d3_ref[1] = (dgch * h).astype(BF16)
        d3_ref[2] = (dgch * gc).astype(BF16)
        for t in range(3):
            dw_ref[t:t + 1, :] += dws[t]
        nxt_ref[...] = dcc[0:8]

    cur = lambda c0: pl.BlockSpec((tr, CONV_COLS), lambda j, g: (ni - 1 - g, c0 + j))
    prv = lambda c0: pl.BlockSpec((16, CONV_COLS), lambda j, g: (jnp.maximum((ni - 1 - g) * (tr // 16) - 1, 0), c0 + j))
    return pl.pallas_call(
        body, name="convmix_bwd", grid=(ncb, ni),
        in_specs=[cur(dc0), cur(gb0), cur(gc0), cur(h0), prv(gc0), prv(h0),
                  pl.BlockSpec((8, CONV_COLS), lambda j, g: (0, j))],
        out_specs=[pl.BlockSpec((3, tr, CONV_COLS), lambda j, g: (0, ni - 1 - g, j)),
                   pl.BlockSpec((8, CONV_COLS), lambda j, g: (0, j))],
        out_shape=[jax.ShapeDtypeStruct((3, s, d - ATTN_WIDTH), BF16), jax.ShapeDtypeStruct((8, d - ATTN_WIDTH), F32)],
        scratch_shapes=[pltpu.VMEM((8, CONV_COLS), F32)],
        compiler_params=_cp(("arbitrary", "arbitrary"), 32),
    )(dmix, proj, proj, proj, proj, proj, scw8)


def _ln_fwd(z):
    mu = jnp.mean(z, axis=-1, keepdims=True)
    zc = z - mu
    var = jnp.mean(zc * zc, axis=-1, keepdims=True)
    rstd = lax.rsqrt(var + LN_EPS)
    return zc * rstd, rstd


def _ln_bwd(dout, xh, rstd, g):
    dxh = dout * g
    c1 = jnp.mean(dxh, axis=-1, keepdims=True)
    c2 = jnp.mean(dxh * xh, axis=-1, keepdims=True)
    dz = rstd * (dxh - c1 - xh * c2)
    return dz, jnp.sum(dout * xh, axis=0, keepdims=True), jnp.sum(dout, axis=0, keepdims=True)


def _outproj_ln1(attn, conv, wout, x, g1, b1, s, d):
    tm = _pick(s, 256, 16)
    ka = attn.shape[1]

    def body(a_ref, c_ref, wt_ref, wb_ref, x_ref, g_ref, b_ref, x1_ref, x1b_ref, xh_ref, rs_ref):
        y = _dot(a_ref[...], wt_ref[...], NN) + _dot(c_ref[...], wb_ref[...], NN)
        xh, rstd = _ln_fwd(ALPHA * x_ref[...] + y)
        x1 = xh * g_ref[...] + b_ref[...]
        x1_ref[...] = x1
        x1b_ref[...] = x1.astype(BF16)
        xh_ref[...] = xh.astype(BF16)
        rs_ref[...] = rstd

    row = lambda w: pl.BlockSpec((tm, w), lambda i: (i, 0))
    vec = pl.BlockSpec((1, d), lambda i: (0, 0))
    return pl.pallas_call(
        body, name="outproj_ln1", grid=(s // tm,),
        in_specs=[row(ka), row(d - ka), pl.BlockSpec((ka, d), lambda i: (0, 0)),
                  pl.BlockSpec((d - ka, d), lambda i: (ka // (d - ka), 0)), row(d), vec, vec],
        out_specs=[row(d), row(d), row(d), row(1)],
        out_shape=[jax.ShapeDtypeStruct((s, d), F32), jax.ShapeDtypeStruct((s, d), BF16),
                   jax.ShapeDtypeStruct((s, d), BF16), jax.ShapeDtypeStruct((s, 1), F32)],
        compiler_params=_cp(("arbitrary",), 48),
    )(attn, conv, wout, wout, x, g1, b1)


def _ffn_up(x1b, wup, fcw8, s, d, dff):
    tm = _pick(s, 1024, 16)
    tn = _pick(dff, 512, LANES)
    nj, ni = dff // tn, s // tm

    def body(x_ref, wa_ref, wg_ref, ca_ref, cg_ref, u_ref, h_ref, carry_ref):
        @pl.when(pl.program_id(1) == 0)
        def _():
            carry_ref[...] = jnp.zeros_like(carry_ref)

        xa = x_ref[...]
        ys = []
        for part, (w_ref, c_ref) in enumerate(((wa_ref, ca_ref), (wg_ref, cg_ref))):
            ub = _dot(xa, w_ref[...], NN).astype(BF16)
            u_ref[part] = ub
            u = ub.astype(F32)
            y, _, _ = _causal_conv(u, carry_ref[part], c_ref[...])
            carry_ref[part] = u[tm - 8:]
            ys.append(y)
        a2, g2 = ys
        sig = 1.0 / (1.0 + jnp.exp(-a2))
        h_ref[...] = (a2 * sig * g2).astype(BF16)

    return pl.pallas_call(
        body, name="ffn_up", grid=(nj, ni),
        in_specs=[pl.BlockSpec((tm, d), lambda j, i: (i, 0)),
                  pl.BlockSpec((d, tn), lambda j, i: (0, j)),
                  pl.BlockSpec((d, tn), lambda j, i: (0, j + nj)),
                  pl.BlockSpec((8, tn), lambda j, i: (0, j)),
                  pl.BlockSpec((8, tn), lambda j, i: (0, j + nj))],
        out_specs=[pl.BlockSpec((2, tm, tn), lambda j, i: (0, i, j)),
                   pl.BlockSpec((tm, tn), lambda j, i: (i, j))],
        out_shape=[jax.ShapeDtypeStruct((2, s, dff), BF16), jax.ShapeDtypeStruct((s, dff), BF16)],
        scratch_shapes=[pltpu.VMEM((2, 8, tn), F32)],
        compiler_params=_cp(("arbitrary", "arbitrary"), 48),
    )(x1b, wup, wup, fcw8, fcw8)


def _ffn_mid_bwd(dh, u3, fcw8, s, dff):
    tm = _pick(s, 1024, 16)
    tn = _pick(dff, 512, LANES)
    nj, ni = dff // tn, s // tm

    def body(dh_ref, u_ref, up_ref, ca_ref, cg_ref, du_ref, dw_ref, nxt_ref):
        g = pl.program_id(1)
        i = ni - 1 - g

        @pl.when(g == 0)
        def _():
            nxt_ref[...] = jnp.zeros_like(nxt_ref)
            dw_ref[...] = jnp.zeros_like(dw_ref)

        has_prev = (i > 0).astype(F32)
        saved = []
        for part, c_ref in enumerate((ca_ref, cg_ref)):
            u = u_ref[part].astype(F32)
            prev8 = up_ref[part].astype(F32)[8:16] * has_prev
            y, s1, s2 = _causal_conv(u, prev8, c_ref[...])
            saved.append((u, s1, s2, y))
        a2, g2 = saved[0][3], saved[1][3]
        sig = 1.0 / (1.0 + jnp.exp(-a2))
        silu = a2 * sig
        dhv = dh_ref[...].astype(F32)
        dys = (dhv * g2 * (sig * (1.0 + a2 * (1.0 - sig))), dhv * silu)
        for part, (c_ref, dy) in enumerate(zip((ca_ref, cg_ref), dys)):
            u, s1, s2, _ = saved[part]
            dx, dws = _conv_bwd(dy, u, s1, s2, c_ref[...], nxt_ref[part])
            du_ref[part] = dx.astype(BF16)
            for t in range(3):
                dw_ref[part, t:t + 1, :] += dws[t]
            nxt_ref[part] = dy[0:8]

    return pl.pallas_call(
        body, name="ffn_mid_bwd", grid=(nj, ni),
        in_specs=[pl.BlockSpec((tm, tn), lambda j, g: (ni - 1 - g, j)),
                  pl.BlockSpec((2, tm, tn), lambda j, g: (0, ni - 1 - g, j)),
                  pl.BlockSpec((2, 16, tn), lambda j, g: (0, jnp.maximum((ni - 1 - g) * (tm // 16) - 1, 0), j)),
                  pl.BlockSpec((8, tn), lambda j, g: (0, j)),
                  pl.BlockSpec((8, tn), lambda j, g: (0, j + nj))],
        out_specs=[pl.BlockSpec((2, tm, tn), lambda j, g: (0, ni - 1 - g, j)),
                   pl.BlockSpec((2, 8, tn), lambda j, g: (0, 0, j))],
        out_shape=[jax.ShapeDtypeStruct((2, s, dff), BF16), jax.ShapeDtypeStruct((2, 8, dff), F32)],
        scratch_shapes=[pltpu.VMEM((2, 8, tn), F32)],
        compiler_params=_cp(("arbitrary", "arbitrary"), 56),
    )(dh, u3, u3, fcw8, fcw8)


def _ffn_down_loss(hmid, wdown, x1, target, g2, b2, s, d, dff):
    tm = _pick(s, 512, SLAB)
    tk = _pick(dff, 1408, LANES)
    ni, nk = s // tm, dff // tk
    slab = min(SLAB, tm)

    def body(h_ref, w_ref, x1_ref, t_ref, g_ref, b_ref, dzb_ref, st_ref, acc_ref):
        i, kk = pl.program_id(0), pl.program_id(1)

        @pl.when((i == 0) & (kk == 0))
        def _():
            st_ref[...] = jnp.zeros_like(st_ref)

        part = _dot(h_ref[...], w_ref[...], NN)

        @pl.when(kk == 0)
        def _():
            acc_ref[...] = part

        @pl.when(kk > 0)
        def _():
            acc_ref[...] += part

        @pl.when(kk == nk - 1)
        def _():
            g, b = g_ref[...], b_ref[...]

            def one(sl, carry):
                rows = pl.ds(pl.multiple_of(sl * slab, slab), slab)
                xh, rstd = _ln_fwd(ALPHA * x1_ref[rows, :] + acc_ref[rows, :])
                diff = xh * g + b - t_ref[rows, :]
                sq = jnp.sum(jnp.sum(diff * diff, axis=1, keepdims=True), axis=0, keepdims=True)
                dz, dg, db = _ln_bwd(diff * (1.0 / d), xh, rstd, g)
                dzb_ref[rows, :] = dz.astype(BF16)
                st_ref[0:1, :] += dg
                st_ref[1:2, :] += db
                st_ref[2:3, :] += sq
                return carry

            lax.fori_loop(0, tm // slab, one, 0)

    row = pl.BlockSpec((tm, d), lambda i, kk: (i, 0))
    vec = pl.BlockSpec((1, d), lambda i, kk: (0, 0))
    return pl.pallas_call(
        body, name="ffn_down_loss", grid=(ni, nk),
        in_specs=[pl.BlockSpec((tm, tk), lambda i, kk: (i, kk)), pl.BlockSpec((tk, d), lambda i, kk: (kk, 0)),
                  row, row, vec, vec],
        out_specs=[row, pl.BlockSpec((8, d), lambda i, kk: (0, 0))],
        out_shape=[jax.ShapeDtypeStruct((s, d), BF16), jax.ShapeDtypeStruct((8, d), F32)],
        scratch_shapes=[pltpu.VMEM((tm, d), F32)],
        compiler_params=_cp(("arbitrary", "arbitrary"), 48),
    )(hmid, wdown, x1, target, g2, b2)


def _ffn_dx_ln1_bwd(du3, wup, dz2b, xh1, rstd1, g1, s, d, dff):
    tm = _pick(s, 1024, SLAB)
    tk = _pick(dff, 1408, LANES)
    nkh = dff // tk
    ni, nk = s // tm, 2 * nkh
    slab = min(SLAB, tm)

    def body(a_ref, w_ref, dz2_ref, xh_ref, rs_ref, g_ref, dzb_ref, st_ref, acc_ref):
        i, kk = pl.program_id(0), pl.program_id(1)

        @pl.when((i == 0) & (kk == 0))
        def _():
            st_ref[...] = jnp.zeros_like(st_ref)

        part = _dot(a_ref[...], w_ref[...], NT)

        @pl.when(kk == 0)
        def _():
            acc_ref[...] = part

        @pl.when(kk > 0)
        def _():
            acc_ref[...] += part

        @pl.when(kk == nk - 1)
        def _():
            g = g_ref[...]

            def one(sl, carry):
                rows = pl.ds(pl.multiple_of(sl * slab, slab), slab)
                dx1 = ALPHA * dz2_ref[rows, :].astype(F32) + acc_ref[rows, :]
                dz, dg, db = _ln_bwd(dx1, xh_ref[rows, :].astype(F32), rs_ref[rows, :], g)
                dzb_ref[rows, :] = dz.astype(BF16)
                st_ref[0:1, :] += dg
                st_ref[1:2, :] += db
                return carry

            lax.fori_loop(0, tm // slab, one, 0)

    row = pl.BlockSpec((tm, d), lambda i, kk: (i, 0))
    row1 = pl.BlockSpec((tm, d), lambda i, kk: (i, 0), pipeline_mode=pl.Buffered(1))
    return pl.pallas_call(
        body, name="ffn_dx_ln1_bwd", grid=(ni, nk),
        in_specs=[pl.BlockSpec((None, tm, tk), lambda i, kk: (kk // nkh, i, kk % nkh)),
                  pl.BlockSpec((d, tk), lambda i, kk: (0, kk)),
                  row1, row1, pl.BlockSpec((tm, 1), lambda i, kk: (i, 0)), pl.BlockSpec((1, d), lambda i, kk: (0, 0))],
        out_specs=[row, pl.BlockSpec((8, d), lambda i, kk: (0, 0))],
        out_shape=[jax.ShapeDtypeStruct((s, d), BF16), jax.ShapeDtypeStruct((8, d), F32)],
        scratch_shapes=[pltpu.VMEM((tm, d), F32)],
        compiler_params=_cp(("arbitrary", "arbitrary"), 56),
    )(du3, wup, dz2b, xh1, rstd1, g1)


def _phase_mixer(x, x_in, win_t, wout, scw8, sinks, ln1_g, ln1_b):
    s, d = x.shape
    n_in = win_t.shape[0]
    xb = x_in.astype(BF16)
    cos, sin = _rope_tables(s)
    proj = _matmul(xb, win_t, mode="nt", m=s, n=n_in, k=d, tm=_pick(s, 512, 16), tn=_pick(n_in, 2176, LANES), tk=d,
                   out_dtype=BF16, name="in_proj", vmem_mb=48)
    attn = _attn_fwd(proj, sinks, cos, sin, s)
    conv = _convmix_fwd(proj, scw8, s, d)
    x1, x1b, xh1, rstd1 = _outproj_ln1(attn, conv, wout, x, ln1_g, ln1_b, s, d)
    return dict(xb=xb, cos=cos, sin=sin, proj=proj, attn=attn, conv=conv, x1=x1, x1b=x1b, xh1=xh1, rstd1=rstd1)


def _phase_ffn(a, target, wup, wdown, fcw8, ln2_g, ln2_b):
    x1, x1b = a["x1"], a["x1b"]
    s, d = x1.shape
    dff = wdown.shape[0]
    u3, hmid = _ffn_up(x1b, wup, fcw8, s, d, dff)
    dz2b, st2 = _ffn_down_loss(hmid, wdown, x1, target, ln2_g, ln2_b, s, d, dff)

    ts = _pick(s, 2048, 16)
    g_wdown = _matmul(hmid, dz2b, mode="tn", m=dff, n=d, k=s, tm=_pick(dff, 1408, LANES), tn=_pick(d, 1024, LANES),
                      tk=ts, out_dtype=BF16, name="grad_w_down", vmem_mb=48)
    dh = _matmul(dz2b, wdown, mode="nt", m=s, n=dff, k=d, tm=_pick(s, 1024, 16), tn=_pick(dff, 1408, LANES), tk=d,
                 out_dtype=BF16, name="ffn_dh", vmem_mb=48)
    du3, dfcw = _ffn_mid_bwd(dh, u3, fcw8, s, dff)
    tnu = _pick(dff, 1408, LANES)
    njh = dff // tnu
    g_wup = _matmul(x1b, du3, mode="tn", m=d, n=2 * dff, k=s, tm=_pick(d, 1024, LANES), tn=tnu, tk=ts, out_dtype=BF16,
                    name="grad_w_up", vmem_mb=48,
                    b_spec=pl.BlockSpec((None, ts, tnu), lambda j, i, kk: (j // njh, kk, j % njh)))
    return dict(du3=du3, dz2b=dz2b, st2=st2, dfcw=dfcw, wdown=g_wdown, wup=g_wup)


def _phase_rest(a, f, wup, wout, win_t, scw8, sinks, ln1_g):
    xb, cos, sin, proj, attn, conv = a["xb"], a["cos"], a["sin"], a["proj"], a["attn"], a["conv"]
    du3, dz2b, st2, dfcw = f["du3"], f["dz2b"], f["st2"], f["dfcw"]
    s, d = a["x1"].shape
    dff = wup.shape[1] // 2
    n_in = win_t.shape[0]
    ts = _pick(s, 2048, 16)
    dz1b, st1 = _ffn_dx_ln1_bwd(du3, wup, dz2b, a["xh1"], a["rstd1"], ln1_g, s, d, dff)

    mix = jnp.concatenate([attn, conv], axis=1)
    g_wout = _matmul(mix, dz1b, mode="tn", m=d, n=d, k=s, tm=_pick(d, 1024, LANES), tn=_pick(d, 1024, LANES), tk=ts,
                     out_dtype=BF16, name="grad_w_out", vmem_mb=48)
    dmix = _matmul(dz1b, wout, mode="nt", m=s, n=d, k=d, tm=_pick(s, 1024, 16), tn=_pick(d, 1024, LANES), tk=d,
                   out_dtype=BF16, name="out_dmix", vmem_mb=48)
    d3, dscw = _convmix_bwd(proj, dmix, scw8, s, d)
    dq, dk, dv, dsink = _attn_bwd(proj, dmix, sinks, cos, sin, s)
    dproj = jnp.concatenate([dq, dk, dv, d3[0], d3[1], d3[2]], axis=1)
    g_win_t = _matmul(dproj, xb, mode="tn", m=n_in, n=d, k=s, tm=_pick(n_in, 2176, LANES), tn=_pick(d, 512, LANES),
                      tk=ts, out_dtype=BF16, name="grad_w_in", vmem_mb=48)
    grad_x = _matmul(dproj, win_t, mode="nn", m=s, n=d, k=n_in, tm=_pick(s, 512, 16), tn=_pick(d, 1024, LANES),
                     tk=n_in, out_dtype=F32, name="grad_x", vmem_mb=56, res=dz1b, alpha=ALPHA)
    small = dict(loss_sq=st2[2, 0], ln2_g=st2[0], ln2_b=st2[1], ln1_g=st1[0], ln1_b=st1[1], sinks=dsink[0, :N_Q_HEADS],
                 fcw=jnp.concatenate([dfcw[0, :3], dfcw[1, :3]], axis=1), scw=dscw[:3])
    return grad_x, dict(win_t=g_win_t, wout=g_wout), small


def _local_step(x, target, win_t, wout, wup, wdown, scw8, fcw8, sinks, ln1_g, ln1_b, ln2_g, ln2_b):
    a = _phase_mixer(x, x, win_t, wout, scw8, sinks, ln1_g, ln1_b)
    f = _phase_ffn(a, target, wup, wdown, fcw8, ln2_g, ln2_b)
    grad_x, g, small = _phase_rest(a, f, wup, wout, win_t, scw8, sinks, ln1_g)
    return grad_x, dict(g, wup=f["wup"], wdown=f["wdown"]), small


MIXER = ("win_t", "wout")
FFN = ("wup", "wdown")
BIG = MIXER + FFN


def _geom(shard_shapes):
    out = {}
    for name in BIG:
        r, c = shard_shapes[name]
        out[name] = ("col" if name == "wup" else "row", (r, c), (r // 2, c))
    return out


def _full_shape(kind, shard):
    r, c = shard
    return (N_CHIPS * r, c) if kind == "row" else (r, N_CHIPS * c)


def _piece_of(ref, kind, shard, chip, half):
    r, c = shard
    if kind == "row":
        return ref.at[pl.ds(chip * r + half * (r // 2), r // 2), :]
    return ref.at[pl.ds(half * (r // 2), r // 2), pl.ds(chip * c, c)]


def _shard_piece(ref, shard, half):
    r, _ = shard
    return ref.at[pl.ds(half * (r // 2), r // 2), :]


def _me():
    return lax.axis_index("x"), lax.axis_index("y"), lax.axis_index("c")


def _other_chips(x, y):
    return [(1 - x, y), (x, 1 - y), (1 - x, 1 - y)]


def _remote(src, dst, send_sem, recv_sem, dev):
    return pltpu.make_async_remote_copy(src_ref=src, dst_ref=dst, send_sem=send_sem, recv_sem=recv_sem,
                                        device_id=dev, device_id_type=MESH)


def _place_shard(w, chip1, kind, name):
    r, c = w.shape
    tr = _rows_tile(r, c, 16)
    nt = r // tr

    def body(chip_ref, w_ref, o_ref):
        o_ref[...] = w_ref[...].astype(BF16)

    out_map = (lambda i, chip_ref: (chip_ref[0] * nt + i, 0)) if kind == "row" else (lambda i, chip_ref: (i, chip_ref[0]))
    return pl.pallas_call(
        body, name="place_" + name,
        grid_spec=pltpu.PrefetchScalarGridSpec(
            num_scalar_prefetch=1, grid=(nt,),
            in_specs=[pl.BlockSpec((tr, c), lambda i, chip_ref: (i, 0))],
            out_specs=pl.BlockSpec((tr, c), out_map)),
        out_shape=jax.ShapeDtypeStruct(_full_shape(kind, (r, c)), BF16),
        compiler_params=_cp(("arbitrary",), 32),
    )(chip1, w)


def _allgather_weights(names, placed, geom, small_shards):
    nb, ns = len(names), len(small_shards)
    small_w = [a.shape[1] for a in small_shards]

    def body(*refs):
        sm = refs[nb:nb + ns]
        full = refs[nb + ns:2 * nb + ns]
        smf = refs[2 * nb + ns:2 * nb + 2 * ns]
        send, recv, loc = refs[2 * nb + 2 * ns:]
        x, y, c = _me()
        chip = 2 * x + y
        sib = (x, y, 1 - c)
        others = _other_chips(x, y)
        locals_, sends = [], []
        for m, name in enumerate(names):
            kind, shard, _ = geom[name]
            mine = _piece_of(full[m], kind, shard, chip, c)
            for k, (qx, qy) in enumerate(others):
                cp = _remote(mine, mine, send.at[6 * m + k], recv.at[6 * m + k], (qx, qy, c))
                cp.start()
                sends.append(cp)
        for t in range(ns):
            cp = pltpu.make_async_copy(sm[t], smf[t].at[:, pl.ds(chip * small_w[t], small_w[t])], loc.at[t])
            cp.start()
            locals_.append(cp)
            for k, (qx, qy) in enumerate(others):
                cp = _remote(sm[t], smf[t].at[:, pl.ds(chip * small_w[t], small_w[t])],
                             send.at[6 * nb + 3 * t + k], recv.at[6 * nb + 3 * t + k], (qx, qy, c))
                cp.start()
                sends.append(cp)
        for m, name in enumerate(names):
            kind, shard, _ = geom[name]
            for k, (qx, qy) in enumerate(others):
                got = _piece_of(full[m], kind, shard, 2 * qx + qy, c)
                _remote(got, got, send.at[6 * m + k], recv.at[6 * m + k], (qx, qy, c)).wait_recv()
                cp = _remote(got, got, send.at[6 * m + 3 + k], recv.at[6 * m + 3 + k], sib)
                cp.start()
                sends.append(cp)
        for t in range(ns):
            for k, (qx, qy) in enumerate(others):
                got = smf[t].at[:, pl.ds((2 * qx + qy) * small_w[t], small_w[t])]
                _remote(got, got, send.at[6 * nb + 3 * t + k], recv.at[6 * nb + 3 * t + k], (qx, qy, c)).wait_recv()
        for m, name in enumerate(names):
            kind, shard, _ = geom[name]
            for k, (qx, qy) in enumerate(others):
                got = _piece_of(full[m], kind, shard, 2 * qx + qy, 1 - c)
                _remote(got, got, send.at[6 * m + 3 + k], recv.at[6 * m + 3 + k], sib).wait_recv()
        for cp in sends:
            cp.wait_send()
        for cp in locals_:
            cp.wait()

    nsem = 6 * nb + 3 * ns
    out_shape = [jax.ShapeDtypeStruct(placed[n].shape, BF16) for n in names]
    out_shape += [jax.ShapeDtypeStruct((8, N_CHIPS * w), F32) for w in small_w]
    outs = pl.pallas_call(
        body, name="allgather_weights", in_specs=[ANY] * (nb + ns), out_specs=[ANY] * (nb + ns), out_shape=out_shape,
        input_output_aliases={m: m for m in range(nb)},
        scratch_shapes=[pltpu.SemaphoreType.DMA((nsem,)), pltpu.SemaphoreType.DMA((nsem,)),
                        pltpu.SemaphoreType.DMA((ns,))],
    )(*[placed[n] for n in names], *small_shards)
    return dict(zip(names, outs[:nb])), list(outs[nb:])


def _sibling_exchange(names, grads, geom):
    nb = len(names)

    def body(*refs):
        g = refs[:nb]
        got = refs[nb:2 * nb]
        send, recv = refs[2 * nb:]
        x, y, c = _me()
        sib = (x, y, 1 - c)
        cps = []
        for m, name in enumerate(names):
            kind, shard, _ = geom[name]
            for r in range(N_CHIPS):
                cp = _remote(_piece_of(g[m], kind, shard, r, 1 - c), got[m].at[r],
                             send.at[N_CHIPS * m + r], recv.at[N_CHIPS * m + r], sib)
                cp.start()
                cps.append(cp)
        for cp in cps:
            cp.wait_recv()
        for cp in cps:
            cp.wait_send()

    return pl.pallas_call(
        body, name="grad_sibling_exchange_" + names[0], in_specs=[ANY] * nb, out_specs=[ANY] * nb,
        out_shape=[jax.ShapeDtypeStruct((N_CHIPS,) + geom[n][2], BF16) for n in names],
        scratch_shapes=[pltpu.SemaphoreType.DMA((N_CHIPS * nb,)), pltpu.SemaphoreType.DMA((N_CHIPS * nb,))],
    )(*[grads[n] for n in names])


def _chip_exchange(names, chip_sums, geom):
    nb = len(names)

    def body(*refs):
        t = refs[:nb]
        got = refs[nb:2 * nb]
        send, recv = refs[2 * nb:]
        x, y, c = _me()
        cps = []
        for m in range(nb):
            for k, (qx, qy) in enumerate(_other_chips(x, y)):
                cp = _remote(t[m].at[2 * qx + qy], got[m].at[k], send.at[3 * m + k], recv.at[3 * m + k], (qx, qy, c))
                cp.start()
                cps.append(cp)
        for cp in cps:
            cp.wait_recv()
        for cp in cps:
            cp.wait_send()

    return pl.pallas_call(
        body, name="grad_chip_exchange_" + names[0], in_specs=[ANY] * nb, out_specs=[ANY] * nb,
        out_shape=[jax.ShapeDtypeStruct((N_CHIPS - 1,) + geom[n][2], BF16) for n in names],
        scratch_shapes=[pltpu.SemaphoreType.DMA((3 * nb,)), pltpu.SemaphoreType.DMA((3 * nb,))],
    )(*chip_sums)


def _sibling_assemble(names, shards, geom):
    nb = len(names)

    def body(*refs):
        full = refs[nb:2 * nb]
        send, recv = refs[2 * nb:]
        x, y, c = _me()
        sib = (x, y, 1 - c)
        cps = []
        for m, name in enumerate(names):
            mine = _shard_piece(full[m], geom[name][1], c)
            cp = _remote(mine, mine, send.at[m], recv.at[m], sib)
            cp.start()
            cps.append(cp)
        for m, name in enumerate(names):
            theirs = _shard_piece(full[m], geom[name][1], 1 - c)
            _remote(theirs, theirs, send.at[m], recv.at[m], sib).wait_recv()
        for cp in cps:
            cp.wait_send()

    return pl.pallas_call(
        body, name="grad_sibling_assemble", in_specs=[ANY] * nb, out_specs=[ANY] * nb,
        out_shape=[jax.ShapeDtypeStruct(geom[n][1], F32) for n in names],
        input_output_aliases={m: m for m in range(nb)},
        scratch_shapes=[pltpu.SemaphoreType.DMA((nb,)), pltpu.SemaphoreType.DMA((nb,))],
    )(*shards)


HBM = pl.BlockSpec(memory_space=pltpu.HBM)
SEM = pl.BlockSpec(memory_space=pltpu.SEMAPHORE)
EFFECT = pltpu.SideEffectType.DATAFLOW_SIDE_EFFECTING
TOKEN = jax.ShapeDtypeStruct((8, LANES), F32)


def _hbm(a):
    return pltpu.with_memory_space_constraint(a, pltpu.HBM)


def _gather_copies(names, full, geom, send, recv):
    x, y, c = _me()
    out = []
    for m, name in enumerate(names):
        kind, shard, _ = geom[name]
        mine = _piece_of(full[m], kind, shard, 2 * x + y, c)
        for k, (qx, qy) in enumerate(_other_chips(x, y)):
            theirs = _piece_of(full[m], kind, shard, 2 * qx + qy, c)
            out.append((_remote(mine, mine, send.at[3 * m + k], recv.at[3 * m + k], (qx, qy, c)),
                        _remote(theirs, theirs, send.at[3 * m + k], recv.at[3 * m + k], (qx, qy, c))))
    return out


def _gather_start(names, placed, geom, after):
    nb = len(names)

    def body(*refs):
        full = refs[:nb]
        send, recv = refs[nb + 1], refs[nb + 2]
        token = refs[2 * nb + 3]
        for cp, _ in _gather_copies(names, full, geom, send, recv):
            cp.start()
        token[...] = jnp.zeros_like(token)

    outs = pl.pallas_call(
        body, name="gather_start_" + names[0],
        out_shape=(pltpu.SemaphoreType.DMA((3 * nb,)), pltpu.SemaphoreType.DMA((3 * nb,)),
                   *[pltpu.HBM(placed[n].shape, BF16) for n in names], TOKEN),
        in_specs=[HBM] * nb + [ANY], out_specs=(SEM, SEM, *[HBM] * nb, pl.BlockSpec(memory_space=pltpu.VMEM)),
        input_output_aliases={m: 2 + m for m in range(nb)},
        compiler_params=pltpu.CompilerParams(has_side_effects=EFFECT),
    )(*[_hbm(placed[n]) for n in names], after)
    return outs[0], outs[1], list(outs[2:2 + nb]), outs[2 + nb]


def _gather_wait(names, send, recv, thru, geom, after):
    nb = len(names)

    def body(*refs):
        full = refs[:nb]
        for mine, theirs in _gather_copies(names, full, geom, refs[nb], refs[nb + 1]):
            mine.wait_send()
            theirs.wait_recv()

    return pl.pallas_call(
        body, name="gather_wait_" + names[0], out_shape=tuple(pltpu.HBM(t.shape, t.dtype) for t in thru),
        in_specs=[HBM] * nb + [SEM, SEM, ANY], out_specs=tuple([HBM] * nb),
        input_output_aliases={m: m for m in range(nb)},
        compiler_params=pltpu.CompilerParams(has_side_effects=EFFECT),
    )(*thru, send, recv, after)


def _gather_forward(names, full, geom):
    nb = len(names)

    def body(*refs):
        arr = refs[nb:2 * nb]
        send, recv = refs[2 * nb:]
        x, y, c = _me()
        sib = (x, y, 1 - c)
        cps = []
        for m, name in enumerate(names):
            kind, shard, _ = geom[name]
            for k, (qx, qy) in enumerate(_other_chips(x, y)):
                got = _piece_of(arr[m], kind, shard, 2 * qx + qy, c)
                cp = _remote(got, got, send.at[3 * m + k], recv.at[3 * m + k], sib)
                cp.start()
                cps.append(cp)
        for m, name in enumerate(names):
            kind, shard, _ = geom[name]
            for k, (qx, qy) in enumerate(_other_chips(x, y)):
                theirs = _piece_of(arr[m], kind, shard, 2 * qx + qy, 1 - c)
                _remote(theirs, theirs, send.at[3 * m + k], recv.at[3 * m + k], sib).wait_recv()
        for cp in cps:
            cp.wait_send()

    return pl.pallas_call(
        body, name="gather_forward_" + names[0], in_specs=[ANY] * nb, out_specs=[ANY] * nb,
        out_shape=[jax.ShapeDtypeStruct(a.shape, a.dtype) for a in full],
        input_output_aliases={m: m for m in range(nb)},
        scratch_shapes=[pltpu.SemaphoreType.DMA((3 * nb,)), pltpu.SemaphoreType.DMA((3 * nb,))],
    )(*full)


def _scatter_copies(nb, t, got, send, recv):
    x, y, c = _me()
    return [_remote(t[m].at[2 * qx + qy], got[m].at[k], send.at[3 * m + k], recv.at[3 * m + k], (qx, qy, c))
            for m in range(nb) for k, (qx, qy) in enumerate(_other_chips(x, y))]


def _chip_exchange_start(names, chip_sums, geom, after):
    nb = len(names)
    lands = [lax.empty((N_CHIPS - 1,) + geom[n][2], BF16) for n in names]

    def body(*refs):
        t, got = refs[:nb], refs[nb:2 * nb]
        send, recv = refs[2 * nb + 1], refs[2 * nb + 2]
        token = refs[4 * nb + 3]
        for cp in _scatter_copies(nb, t, got, send, recv):
            cp.start()
        token[...] = jnp.zeros_like(token)

    both = list(chip_sums) + lands
    outs = pl.pallas_call(
        body, name="grad_chip_start_" + names[0],
        out_shape=(pltpu.SemaphoreType.DMA((3 * nb,)), pltpu.SemaphoreType.DMA((3 * nb,)),
                   *[pltpu.HBM(a.shape, a.dtype) for a in both], TOKEN),
        in_specs=[HBM] * (2 * nb) + [ANY],
        out_specs=(SEM, SEM, *[HBM] * (2 * nb), pl.BlockSpec(memory_space=pltpu.VMEM)),
        input_output_aliases={m: 2 + m for m in range(2 * nb)},
        compiler_params=pltpu.CompilerParams(has_side_effects=EFFECT),
    )(*[_hbm(a) for a in both], after)
    return outs[0], outs[1], list(outs[2:2 + 2 * nb]), outs[2 + 2 * nb]


def _chip_exchange_wait(names, send, recv, thru, after):
    nb = len(names)

    def body(*refs):
        for cp in _scatter_copies(nb, refs[:nb], refs[nb:2 * nb], refs[2 * nb], refs[2 * nb + 1]):
            cp.wait_send()
            cp.wait_recv()

    outs = pl.pallas_call(
        body, name="grad_chip_wait_" + names[0], out_shape=tuple(pltpu.HBM(t.shape, t.dtype) for t in thru),
        in_specs=[HBM] * (2 * nb) + [SEM, SEM, ANY], out_specs=tuple([HBM] * (2 * nb)),
        input_output_aliases={m: m for m in range(2 * nb)},
        compiler_params=pltpu.CompilerParams(has_side_effects=EFFECT),
    )(*thru, send, recv, after)
    return list(outs[nb:])


def _allreduce_small(part):
    rows = part.shape[0]
    flips = [(a, b, e) for a in (0, 1) for b in (0, 1) for e in (0, 1) if (a, b, e) != (0, 0, 0)]

    def body(p_ref, o_ref, all_ref, send, recv):
        x, y, c = _me()
        me = 4 * x + 2 * y + c
        all_ref[me] = p_ref[...]
        cps = []
        for k, (a, b, e) in enumerate(flips):
            cp = _remote(p_ref, all_ref.at[me], send.at[k], recv.at[k], (x ^ a, y ^ b, c ^ e))
            cp.start()
            cps.append(cp)
        for k, (a, b, e) in enumerate(flips):
            peer = 4 * (x ^ a) + 2 * (y ^ b) + (c ^ e)
            _remote(p_ref, all_ref.at[peer], send.at[k], recv.at[k], (x ^ a, y ^ b, c ^ e)).wait_recv()
        for cp in cps:
            cp.wait_send()
        tot = all_ref[0]
        for dev in range(1, 8):
            tot = tot + all_ref[dev]
        o_ref[...] = tot

    vm = pl.BlockSpec(memory_space=pltpu.VMEM)
    return pl.pallas_call(
        body, name="allreduce_small", in_specs=[vm], out_specs=vm, out_shape=jax.ShapeDtypeStruct((rows, LANES), F32),
        scratch_shapes=[pltpu.VMEM((8, rows, LANES), F32), pltpu.SemaphoreType.DMA((7,)), pltpu.SemaphoreType.DMA((7,))],
    )(part)


def _rows_tile(rows, cols, mult):
    return _pick(rows, max(mult, (1 << 19) // cols // mult * mult), mult)


def _add_pairs(g, got, kind, shard, where, name):
    p, r, c = got.shape
    tr = _rows_tile(r, c, 16)
    nt = r // tr

    def body(w_ref, a_ref, b_ref, o_ref):
        o_ref[...] = (a_ref[...].astype(F32) + b_ref[...].astype(F32)).astype(BF16)

    if kind == "row":
        g_map = lambda q, i, w_ref: ((2 * q + w_ref[1]) * nt + i, 0)
    else:
        g_map = lambda q, i, w_ref: (w_ref[1] * nt + i, q)
    spec = pl.BlockSpec((None, tr, c), lambda q, i, w_ref: (q, i, 0))
    return pl.pallas_call(
        body, name="grad_add_sibling_" + name,
        grid_spec=pltpu.PrefetchScalarGridSpec(
            num_scalar_prefetch=1, grid=(p, nt), in_specs=[pl.BlockSpec((tr, c), g_map), spec], out_specs=spec),
        out_shape=jax.ShapeDtypeStruct((p, r, c), BF16), compiler_params=_cp(("arbitrary", "arbitrary"), 32),
    )(where, g, got)


def _add_four(t, got, shard, where, name):
    _, r, c = t.shape
    tr = _rows_tile(r, c, 16)
    nt = r // tr

    def body(w_ref, own, t0, t1, t2, o_ref):
        o_ref[...] = ((own[...].astype(F32) + t0[...].astype(F32)) + t1[...].astype(F32)) + t2[...].astype(F32)

    spec = lambda q: pl.BlockSpec((None, tr, c), lambda i, w_ref: (q, i, 0))
    return pl.pallas_call(
        body, name="grad_add_chips_" + name,
        grid_spec=pltpu.PrefetchScalarGridSpec(
            num_scalar_prefetch=1, grid=(nt,),
            in_specs=[pl.BlockSpec((None, tr, c), lambda i, w_ref: (w_ref[0], i, 0)), spec(0), spec(1), spec(2)],
            out_specs=pl.BlockSpec((tr, c), lambda i, w_ref: (w_ref[1] * nt + i, 0))),
        out_shape=jax.ShapeDtypeStruct(shard, F32), compiler_params=_cp(("arbitrary",), 32),
    )(where, t, got, got, got)


def _adamw(w, g, m, v, name):
    r, c = w.shape
    tr = _rows_tile(r, c, 8)

    def body(w_ref, g_ref, m_ref, v_ref, go_ref, d_ref, mo_ref, vo_ref):
        gv = g_ref[...]
        mn = ADAM_B1 * m_ref[...] + (1.0 - ADAM_B1) * gv
        vn = ADAM_B2 * v_ref[...] + (1.0 - ADAM_B2) * (gv * gv)
        m_hat = mn / (1.0 - ADAM_B1 ** ADAM_STEP)
        v_hat = vn / (1.0 - ADAM_B2 ** ADAM_STEP)
        go_ref[...] = gv
        d_ref[...] = -ADAM_LR * (m_hat / (jnp.sqrt(v_hat) + ADAM_EPS) + ADAM_WD * w_ref[...])
        mo_ref[...] = mn
        vo_ref[...] = vn

    spec = pl.BlockSpec((tr, c), lambda i: (i, 0))
    return pl.pallas_call(
        body, name=name, grid=(r // tr,), in_specs=[spec] * 4, out_specs=[spec] * 4,
        out_shape=[jax.ShapeDtypeStruct((r, c), F32)] * 4, compiler_params=_cp(("arbitrary",), 32),
    )(w, g, m, v)


def _pack(vectors, rows):
    flat = jnp.concatenate([v.reshape(-1).astype(F32) for v in vectors])
    return jnp.pad(flat, (0, rows * LANES - flat.shape[0])).reshape(rows, LANES)


def _unpack(packed, shapes):
    flat = packed.reshape(-1)
    out, off = [], 0
    for shp in shapes:
        n = 1
        for t in shp:
            n *= t
        out.append(flat[off:off + n].reshape(shp))
        off += n
    return out


def _rows_for(shapes):
    n = sum(functools.reduce(lambda a, b: a * b, shp, 1) for shp in shapes)
    return -(-n // (8 * LANES)) * 8


def kernel(x, w_in, attn_sinks, short_conv_w, w_out, ln1_g, ln1_b, ffn_w_up, ffn_conv_w, ffn_w_down, ln2_g, ln2_b, loss_target, m_w_in, m_attn_sinks, m_short_conv_w, m_w_out, m_ln1_g, m_ln1_b, m_ffn_w_up, m_ffn_conv_w, m_ffn_w_down, m_ln2_g, m_ln2_b, v_w_in, v_attn_sinks, v_short_conv_w, v_w_out, v_ln1_g, v_ln1_b, v_ffn_w_up, v_ffn_conv_w, v_ffn_w_down, v_ln2_g, v_ln2_b):
    xs, tgt = x[0], loss_target[0]
    s, d = xs.shape
    chip = 2 * lax.axis_index("x") + lax.axis_index("y")

    t_in = lambda a: a[0].T
    w_big = dict(win_t=t_in(w_in), wout=w_out[0], wup=ffn_w_up[0], wdown=ffn_w_down[0])
    m_big = dict(win_t=t_in(m_w_in), wout=m_w_out[0], wup=m_ffn_w_up[0], wdown=m_ffn_w_down[0])
    v_big = dict(win_t=t_in(v_w_in), wout=v_w_out[0], wup=v_ffn_w_up[0], wdown=v_ffn_w_down[0])
    geom = _geom({n: w_big[n].shape for n in BIG})
    pad8 = lambda a: jnp.pad(a[0], ((0, 5), (0, 0)))
    where = jnp.stack([chip, lax.axis_index("c")]).astype(jnp.int32)
    placed = {n: _place_shard(w_big[n], where[:1], geom[n][0], n) for n in BIG}
    full, (scw8, fcw8) = _allgather_weights(MIXER, placed, geom, [pad8(short_conv_w), pad8(ffn_conv_w)])
    send, recv, thru, token = _gather_start(FFN, placed, geom, scw8)
    a = _phase_mixer(xs, xs + token[0, 0], full["win_t"], full["wout"], scw8, attn_sinks, ln1_g, ln1_b)
    landed = _gather_forward(FFN, _gather_wait(FFN, send, recv, thru, geom, a["x1b"]), geom)
    full.update(zip(FFN, landed))
    f = _phase_ffn(a, tgt, full["wup"], full["wdown"], fcw8, ln2_g, ln2_b)

    def chip_sums_of(names, grads):
        from_sibling = _sibling_exchange(names, grads, geom)
        return [_add_pairs(grads[n], from_sibling[m], geom[n][0], geom[n][1], where, n) for m, n in enumerate(names)]

    ffn_sums = chip_sums_of(FFN, f)
    send, recv, thru, token = _chip_exchange_start(FFN, ffn_sums, geom, f["st2"])
    grad_x, g_mixer, g_small = _phase_rest(a, f, full["wup"], full["wout"], full["win_t"], scw8, attn_sinks,
                                           ln1_g + token[0:1, 0:1])
    mixer_sums = chip_sums_of(MIXER, g_mixer)
    from_chips = dict(zip(MIXER, _chip_exchange(MIXER, mixer_sums, geom)))
    from_chips.update(zip(FFN, _chip_exchange_wait(FFN, send, recv, thru, from_chips[MIXER[0]])))
    chip_sums = dict(zip(MIXER + FFN, mixer_sums + ffn_sums))
    halves = [_add_four(chip_sums[n], from_chips[n], geom[n][1], where, n) for n in BIG]
    g_shards = dict(zip(BIG, _sibling_assemble(BIG, halves, geom)))

    small_names = ("ln1_g", "ln1_b", "ln2_g", "ln2_b", "sinks", "fcw", "scw")
    small_shapes = [g_small[n].shape for n in small_names]
    red = _allreduce_small(_pack([g_small["loss_sq"].reshape(1)] + [g_small[n] for n in small_names],
                                 _rows_for([(1,)] + small_shapes)))
    loss_sq, *gs = _unpack(red, [(1,)] + small_shapes)
    gs = dict(zip(small_names, gs))
    loss = (0.5 / d) * loss_sq[0]
    fw, sw = ffn_conv_w.shape[2], short_conv_w.shape[2]
    gs["fcw"] = lax.dynamic_slice_in_dim(gs["fcw"], chip * fw, fw, axis=1)
    gs["scw"] = lax.dynamic_slice_in_dim(gs["scw"], chip * sw, sw, axis=1)

    upd = {n: _adamw(w_big[n], g_shards[n], m_big[n], v_big[n], "adamw_" + n) for n in BIG}
    upd["win_t"] = tuple(a.T for a in upd["win_t"])
    sm_w = dict(ln1_g=ln1_g[0], ln1_b=ln1_b[0], ln2_g=ln2_g[0], ln2_b=ln2_b[0], sinks=attn_sinks[0],
                fcw=ffn_conv_w[0], scw=short_conv_w[0])
    sm_m = dict(ln1_g=m_ln1_g[0], ln1_b=m_ln1_b[0], ln2_g=m_ln2_g[0], ln2_b=m_ln2_b[0], sinks=m_attn_sinks[0],
                fcw=m_ffn_conv_w[0], scw=m_short_conv_w[0])
    sm_v = dict(ln1_g=v_ln1_g[0], ln1_b=v_ln1_b[0], ln2_g=v_ln2_g[0], ln2_b=v_ln2_b[0], sinks=v_attn_sinks[0],
                fcw=v_ffn_conv_w[0], scw=v_short_conv_w[0])
    shapes = [sm_w[n].shape for n in small_names]
    rows = _rows_for(shapes)
    packed = [_pack([t[n] for n in small_names], rows) for t in (sm_w, gs, sm_m, sm_v)]
    sm_out = [dict(zip(small_names, _unpack(a, shapes))) for a in _adamw(*packed, "adamw_small")]

    def leaf(kind, name):
        if name in ("w_in", "w_out", "ffn_w_up", "ffn_w_down"):
            key = dict(w_in="win_t", w_out="wout", ffn_w_up="wup", ffn_w_down="wdown")[name]
            return upd[key][kind][None]
        key = dict(attn_sinks="sinks", short_conv_w="scw", ffn_conv_w="fcw").get(name, name)
        return sm_out[kind][key][None]

    order = ("w_in", "attn_sinks", "short_conv_w", "w_out", "ln1_g", "ln1_b", "ffn_w_up", "ffn_conv_w", "ffn_w_down",
             "ln2_g", "ln2_b")
    outs = [loss, grad_x[None]]
    for kind in range(4):
        outs += [leaf(kind, n) for n in order]
    return tuple(outs)
```

```python
import functools

import jax
import jax.numpy as jnp
from jax import lax
from jax.experimental import pallas as pl
from jax.experimental.pallas import tpu as pltpu

F32 = jnp.float32
BF16 = jnp.bfloat16
MESH = pl.DeviceIdType.MESH
ANY = pl.BlockSpec(memory_space=pl.ANY)

HEAD_DIM = 64
N_Q_HEADS = 16
N_KV_HEADS = 2
ATTN_WIDTH = N_Q_HEADS * HEAD_DIM
KV_WIDTH = N_KV_HEADS * HEAD_DIM
BLOCK = 128
ROPE_THETA = 10000.0
LN_EPS = 1e-5
ALPHA = 2.0 ** 0.25
NEG_INF = -1e30
ADAM_LR, ADAM_B1, ADAM_B2, ADAM_EPS, ADAM_WD, ADAM_STEP = 0.001, 0.9, 0.999, 1e-08, 0.01, 10
N_CHIPS = 4
LANES = 128
SLAB = 128


def _cp(sem, vmem_mb):
    return pltpu.CompilerParams(dimension_semantics=sem, vmem_limit_bytes=vmem_mb << 20)


def _matmul(a, b, *, mode, m, n, k, tm, tn, tk, out_dtype, name, vmem_mb, a_spec=None, b_spec=None,
            res=None, alpha=1.0, after=None):
    nj, ni, nk = n // tn, m // tm, k // tk
    assert nj * tn == n and ni * tm == m and nk * tk == k, (name, m, n, k, tm, tn, tk)
    if mode == "nn":
        dims = ((1,), (0,))
        a_spec = a_spec or pl.BlockSpec((tm, tk), lambda j, i, kk: (i, kk))
        b_spec = b_spec or pl.BlockSpec((tk, tn), lambda j, i, kk: (kk, j))
    elif mode == "nt":
        dims = ((1,), (1,))
        a_spec = a_spec or pl.BlockSpec((tm, tk), lambda j, i, kk: (i, kk))
        b_spec = b_spec or pl.BlockSpec((tn, tk), lambda j, i, kk: (j, kk))
    else:
        dims = ((0,), (0,))
        a_spec = a_spec or pl.BlockSpec((tk, tm), lambda j, i, kk: (kk, i))
        b_spec = b_spec or pl.BlockSpec((tk, tn), lambda j, i, kk: (kk, j))
    has_res = res is not None
    has_after = after is not None

    def body(*refs):
        refs = refs[1:] if has_after else refs
        a_ref, b_ref = refs[0], refs[1]
        res_ref = refs[2] if has_res else None
        o_ref = refs[2 + has_res]
        part = lax.dot_general(a_ref[...], b_ref[...], (dims, ((), ())), preferred_element_type=F32)

        def finish(acc):
            if has_res:
                acc = acc + alpha * res_ref[...].astype(F32)
            o_ref[...] = acc.astype(o_ref.dtype)

        if nk == 1:
            finish(part)
        else:
            acc_ref = refs[3 + has_res]
            kk = pl.program_id(2)

            @pl.when(kk == 0)
            def _():
                acc_ref[...] = part

            @pl.when(kk > 0)
            def _():
                acc_ref[...] += part

            @pl.when(kk == nk - 1)
            def _():
                finish(acc_ref[...])

    in_specs = [a_spec, b_spec]
    args = [a, b]
    if has_res:
        in_specs.append(pl.BlockSpec((tm, tn), lambda j, i, kk: (i, j)))
        args.append(res)
    if has_after:
        in_specs.insert(0, pl.BlockSpec(after.shape, lambda j, i, kk: (0, 0)))
        args.insert(0, after)
    return pl.pallas_call(
        body, name=name, grid=(nj, ni, nk), in_specs=in_specs,
        out_specs=pl.BlockSpec((tm, tn), lambda j, i, kk: (i, j)),
        out_shape=jax.ShapeDtypeStruct((m, n), out_dtype),
        scratch_shapes=[pltpu.VMEM((tm, tn), F32)] if nk > 1 else [],
        compiler_params=_cp(("arbitrary", "arbitrary", "arbitrary"), vmem_mb),
    )(*args)


def _pick(total, want, mult):
    if total <= want:
        return total
    for t in range(want, 0, -1):
        if total % t == 0 and t % mult == 0:
            return t
    return total


def _rope_tables(s):
    half = HEAD_DIM // 2
    inv_freq = ROPE_THETA ** (-jnp.arange(half, dtype=F32) / half)
    ang = jnp.arange(s, dtype=F32)[:, None] * inv_freq[None, :]
    cos = jnp.tile(jnp.cos(ang), (1, LANES // half))
    sin = jnp.tile(jnp.concatenate([-jnp.sin(ang), jnp.sin(ang)], axis=1), (1, LANES // HEAD_DIM))
    return cos, sin


def _rope(x, cos, sin, lo):
    partner = jnp.where(lo, pltpu.roll(x, LANES - HEAD_DIM // 2, 1), pltpu.roll(x, HEAD_DIM // 2, 1))
    return x * cos + partner * sin


def _dot(a, b, dims):
    return lax.dot_general(a, b, (dims, ((), ())), preferred_element_type=F32)


NN, NT, TN = ((1,), (0,)), ((1,), (1,)), ((0,), (0,))


def _kv_variants(t, head_lo):
    r = pltpu.roll(t, HEAD_DIM, 1)
    zero = jnp.zeros_like(t)
    a = (jnp.where(head_lo, t, zero).astype(BF16), jnp.where(head_lo, r, zero).astype(BF16))
    b = (jnp.where(head_lo, zero, r).astype(BF16), jnp.where(head_lo, zero, t).astype(BF16))
    return a, b


PAIRS_PER_KV = N_Q_HEADS // 2 // N_KV_HEADS
STACK = PAIRS_PER_KV * BLOCK


def _stack_pairs(ref, j, fn):
    return jnp.concatenate([fn(ref[:, p * LANES:(p + 1) * LANES])
                            for p in range(j * PAIRS_PER_KV, (j + 1) * PAIRS_PER_KV)], axis=0)


def _stack_sinks(sink_ref, j, hh):
    return jnp.concatenate([jnp.full((BLOCK, 1), sink_ref[0, 2 * p + hh], F32)
                            for p in range(j * PAIRS_PER_KV, (j + 1) * PAIRS_PER_KV)], axis=0)


def _attn_probs(qp, ka, kb, valid, sink_a, sink_b):
    out = []
    for kk, sink in ((ka, sink_a), (kb, sink_b)):
        s = jnp.where(valid, _dot(qp, kk, NT), NEG_INF)
        mx = jnp.maximum(jnp.max(s, axis=1, keepdims=True), sink)
        e = jnp.exp(s - mx)
        es = jnp.exp(sink - mx)
        inv = 1.0 / (jnp.sum(e, axis=1, keepdims=True) + es)
        out.append((e * inv, es * inv))
    return out


def _attn_common(i, q_ref, k_ref, v_ref, kp_ref, vp_ref, cos_ref, sin_ref, cosp_ref, sinp_ref):
    lane = lax.broadcasted_iota(jnp.int32, (1, LANES), 1)
    lo = (lane % HEAD_DIM) < (HEAD_DIM // 2)
    head_lo = lane < HEAD_DIM
    cos, sin = cos_ref[...], sin_ref[...]
    kc = _rope(k_ref[...].astype(F32), cos, sin, lo)
    kp = _rope(kp_ref[...].astype(F32), cosp_ref[...], sinp_ref[...], lo)
    kext = jnp.concatenate([kp, kc], axis=0)
    vext = jnp.concatenate([vp_ref[...].astype(F32), v_ref[...].astype(F32)], axis=0)
    ka, kb = _kv_variants(kext, head_lo)
    va, vb = _kv_variants(vext, head_lo)
    qi = lax.broadcasted_iota(jnp.int32, (STACK, 1), 0) % BLOCK
    kj = lax.broadcasted_iota(jnp.int32, (1, 2 * BLOCK), 1)
    valid = (kj > qi) & (kj <= qi + BLOCK) & ((kj >= BLOCK) | (i > 0))
    cos4 = jnp.concatenate([cos] * PAIRS_PER_KV, axis=0)
    sin4 = jnp.concatenate([sin] * PAIRS_PER_KV, axis=0)
    return lo, head_lo, cos, sin, cos4, sin4, ka, kb, va, vb, valid


def _attn_fwd(proj, sinks, cos, sin, s):
    nb = s // BLOCK
    kcol, vcol = ATTN_WIDTH // LANES, ATTN_WIDTH // LANES + 1

    def body(q_ref, k_ref, v_ref, kp_ref, vp_ref, cos_ref, sin_ref, cosp_ref, sinp_ref, sink_ref, o_ref):
        i = pl.program_id(0)
        lo, head_lo, cs, sn, cs4, sn4, ka, kb, va, vb, valid = _attn_common(
            i, q_ref, k_ref, v_ref, kp_ref, vp_ref, cos_ref, sin_ref, cosp_ref, sinp_ref)
        for j in range(N_KV_HEADS):
            q4 = _stack_pairs(q_ref, j, lambda t: t.astype(F32))
            qp = (_rope(q4, cs4, sn4, lo) * HEAD_DIM ** -0.5).astype(BF16)
            (pa, _), (pb, _) = _attn_probs(qp, ka[j], kb[j], valid, _stack_sinks(sink_ref, j, 0),
                                           _stack_sinks(sink_ref, j, 1))
            o = (_dot(pa.astype(BF16), va[j], NN) + _dot(pb.astype(BF16), vb[j], NN)).astype(BF16)
            for t in range(PAIRS_PER_KV):
                p = j * PAIRS_PER_KV + t
                o_ref[:, p * LANES:(p + 1) * LANES] = o[t * BLOCK:(t + 1) * BLOCK]

    prev = lambda i: (jnp.maximum(i - 1, 0), 0)
    return pl.pallas_call(
        body, name="attn_fwd", grid=(nb,),
        in_specs=[pl.BlockSpec((BLOCK, ATTN_WIDTH), lambda i: (i, 0)),
                  pl.BlockSpec((BLOCK, LANES), lambda i: (i, kcol)),
                  pl.BlockSpec((BLOCK, LANES), lambda i: (i, vcol)),
                  pl.BlockSpec((BLOCK, LANES), lambda i: (jnp.maximum(i - 1, 0), kcol)),
                  pl.BlockSpec((BLOCK, LANES), lambda i: (jnp.maximum(i - 1, 0), vcol)),
                  pl.BlockSpec((BLOCK, LANES), lambda i: (i, 0)),
                  pl.BlockSpec((BLOCK, LANES), lambda i: (i, 0)),
                  pl.BlockSpec((BLOCK, LANES), prev),
                  pl.BlockSpec((BLOCK, LANES), prev),
                  pl.BlockSpec(memory_space=pltpu.SMEM)],
        out_specs=pl.BlockSpec((BLOCK, ATTN_WIDTH), lambda i: (i, 0)),
        out_shape=jax.ShapeDtypeStruct((s, ATTN_WIDTH), BF16),
        compiler_params=_cp(("arbitrary",), 32),
    )(proj, proj, proj, proj, proj, cos, sin, cos, sin, sinks)


def _attn_bwd(proj, dmix, sinks, cos, sin, s):
    nb = s // BLOCK
    kcol, vcol = ATTN_WIDTH // LANES, ATTN_WIDTH // LANES + 1
    pairs_per_kv = N_Q_HEADS // 2 // N_KV_HEADS

    def body(q_ref, k_ref, v_ref, kp_ref, vp_ref, cos_ref, sin_ref, cosp_ref, sinp_ref, sink_ref, do_ref,
             dq_ref, dk_ref, dv_ref, dsink_ref, ck_ref, cv_ref):
        g = pl.program_id(0)
        i = nb - 1 - g

        @pl.when(g == 0)
        def _():
            ck_ref[...] = jnp.zeros_like(ck_ref)
            cv_ref[...] = jnp.zeros_like(cv_ref)
            dsink_ref[...] = jnp.zeros_like(dsink_ref)

        lo, head_lo, cs, sn, cs4, sn4, ka, kb, va, vb, valid = _attn_common(
            i, q_ref, k_ref, v_ref, kp_ref, vp_ref, cos_ref, sin_ref, cosp_ref, sinp_ref)
        lane = lax.broadcasted_iota(jnp.int32, (1, LANES), 1)
        dk_j, dv_j = [], []
        dsink = jnp.zeros((1, LANES), F32)
        for j in range(N_KV_HEADS):
            q4 = _stack_pairs(q_ref, j, lambda t: t.astype(F32))
            qp = (_rope(q4, cs4, sn4, lo) * HEAD_DIM ** -0.5).astype(BF16)
            probs = _attn_probs(qp, ka[j], kb[j], valid, _stack_sinks(sink_ref, j, 0), _stack_sinks(sink_ref, j, 1))
            do = _stack_pairs(do_ref, j, lambda t: t)
            dq_r = jnp.zeros((STACK, LANES), F32)
            dkc, dvc = [], []
            for hh, ((pr, ps), kk, vv) in enumerate(zip(probs, (ka[j], kb[j]), (va[j], vb[j]))):
                dp = _dot(do, vv, NT)
                delta = jnp.sum(pr * dp, axis=1, keepdims=True)
                ds = (pr * (dp - delta)).astype(BF16)
                psd = ps * delta
                for t in range(PAIRS_PER_KV):
                    head = 2 * (j * PAIRS_PER_KV + t) + hh
                    dsink = dsink + jnp.where(
                        lane == head, -jnp.sum(psd[t * BLOCK:(t + 1) * BLOCK], axis=0, keepdims=True), 0.0)
                dq_r = dq_r + _dot(ds, kk, NN)
                dkc.append(_dot(ds, qp, TN))
                dvc.append(_dot(pr.astype(BF16), do, TN))
            dk_j.append(jnp.where(head_lo, dkc[0], dkc[1]))
            dv_j.append(jnp.where(head_lo, dvc[0], dvc[1]))
            dq = _rope(dq_r * HEAD_DIM ** -0.5, cs4, -sn4, lo).astype(BF16)
            for t in range(PAIRS_PER_KV):
                p = j * PAIRS_PER_KV + t
                dq_ref[:, p * LANES:(p + 1) * LANES] = dq[t * BLOCK:(t + 1) * BLOCK]
        tot_k = [t + pltpu.roll(t, HEAD_DIM, 1) for t in dk_j]
        tot_v = [t + pltpu.roll(t, HEAD_DIM, 1) for t in dv_j]
        dkext = jnp.where(head_lo, tot_k[0], tot_k[1])
        dvext = jnp.where(head_lo, tot_v[0], tot_v[1])
        dk_r = dkext[BLOCK:] + ck_ref[...]
        dk_ref[...] = _rope(dk_r, cs, -sn, lo).astype(BF16)
        dv_ref[...] = (dvext[BLOCK:] + cv_ref[...]).astype(BF16)
        ck_ref[...] = dkext[:BLOCK]
        cv_ref[...] = dvext[:BLOCK]
        dsink_ref[0:1, :] += dsink

    cur = lambda col: (lambda g: (nb - 1 - g, col))
    prv = lambda col: (lambda g: (jnp.maximum(nb - 2 - g, 0), col))
    blk = lambda w, f: pl.BlockSpec((BLOCK, w), f)
    return pl.pallas_call(
        body, name="attn_bwd", grid=(nb,),
        in_specs=[blk(ATTN_WIDTH, cur(0)), blk(LANES, cur(kcol)), blk(LANES, cur(vcol)),
                  blk(LANES, prv(kcol)), blk(LANES, prv(vcol)),
                  blk(LANES, cur(0)), blk(LANES, cur(0)), blk(LANES, prv(0)), blk(LANES, prv(0)),
                  pl.BlockSpec(memory_space=pltpu.SMEM),
                  blk(ATTN_WIDTH, cur(0))],
        out_specs=[blk(ATTN_WIDTH, cur(0)), blk(LANES, cur(0)), blk(LANES, cur(0)),
                   pl.BlockSpec((8, LANES), lambda g: (0, 0))],
        out_shape=[jax.ShapeDtypeStruct((s, ATTN_WIDTH), BF16), jax.ShapeDtypeStruct((s, LANES), BF16),
                   jax.ShapeDtypeStruct((s, LANES), BF16), jax.ShapeDtypeStruct((8, LANES), F32)],
        scratch_shapes=[pltpu.VMEM((BLOCK, LANES), F32), pltpu.VMEM((BLOCK, LANES), F32)],
        compiler_params=_cp(("arbitrary",), 32),
    )(proj, proj, proj, proj, proj, cos, sin, cos, sin, sinks, dmix)


def _causal_conv(x, prev8, w):
    row = lax.broadcasted_iota(jnp.int32, (8, 1), 0)
    r1, r2 = pltpu.roll(x, 1, 0), pltpu.roll(x, 2, 0)
    s1 = jnp.concatenate([jnp.where(row == 0, prev8[7:8], r1[:8]), r1[8:]], axis=0)
    s2 = jnp.concatenate([jnp.where(row == 0, prev8[6:7], jnp.where(row == 1, prev8[7:8], r2[:8])), r2[8:]], axis=0)
    return w[0:1] * s2 + w[1:2] * s1 + w[2:3] * x, s1, s2


def _conv_bwd(dy, x, s1, s2, w, next8):
    r = x.shape[0]
    row = lax.broadcasted_iota(jnp.int32, (8, 1), 0)
    r1, r2 = pltpu.roll(dy, r - 1, 0), pltpu.roll(dy, r - 2, 0)
    n1 = jnp.concatenate([r1[:r - 8], jnp.where(row == 7, next8[0:1], r1[r - 8:])], axis=0)
    n2 = jnp.concatenate([r2[:r - 8], jnp.where(row == 6, next8[0:1], jnp.where(row == 7, next8[1:2], r2[r - 8:]))],
                         axis=0)
    dx = w[2:3] * dy + w[1:2] * n1 + w[0:1] * n2
    dws = [jnp.sum(dy * t, axis=0, keepdims=True) for t in (s2, s1, x)]
    return dx, dws


CONV_COLS = 256


def _convmix_cols(d):
    conv_w = d - ATTN_WIDTH
    base = (ATTN_WIDTH + 2 * KV_WIDTH) // CONV_COLS
    step = conv_w // CONV_COLS
    return base, base + step, base + 2 * step, step


def _convmix_fwd(proj, scw8, s, d):
    gb0, gc0, h0, ncb = _convmix_cols(d)
    tr = _pick(s, 1024, 16)
    ni = s // tr

    def body(gb_ref, gc_ref, h_ref, w_ref, o_ref, carry_ref):
        @pl.when(pl.program_id(1) == 0)
        def _():
            carry_ref[...] = jnp.zeros_like(carry_ref)

        gch = gc_ref[...].astype(F32) * h_ref[...].astype(F32)
        cc, _, _ = _causal_conv(gch, carry_ref[...], w_ref[...])
        o_ref[...] = (gb_ref[...].astype(F32) * cc).astype(BF16)
        carry_ref[...] = gch[tr - 8:]

    spec = lambda c0: pl.BlockSpec((tr, CONV_COLS), lambda j, i: (i, c0 + j))
    return pl.pallas_call(
        body, name="convmix_fwd", grid=(ncb, ni),
        in_specs=[spec(gb0), spec(gc0), spec(h0), pl.BlockSpec((8, CONV_COLS), lambda j, i: (0, j))],
        out_specs=pl.BlockSpec((tr, CONV_COLS), lambda j, i: (i, j)),
        out_shape=jax.ShapeDtypeStruct((s, d - ATTN_WIDTH), BF16),
        scratch_shapes=[pltpu.VMEM((8, CONV_COLS), F32)],
        compiler_params=_cp(("arbitrary", "arbitrary"), 32),
    )(proj, proj, proj, scw8)


def _convmix_bwd(proj, dmix, scw8, s, d):
    gb0, gc0, h0, ncb = _convmix_cols(d)
    tr = _pick(s, 1024, 16)
    ni = s // tr
    dc0 = ATTN_WIDTH // CONV_COLS

    def body(dc_ref, gb_ref, gc_ref, h_ref, gcp_ref, hp_ref, w_ref, d3_ref, dw_ref, nxt_ref):
        g = pl.program_id(1)
        i = ni - 1 - g

        @pl.when(g == 0)
        def _():
            nxt_ref[...] = jnp.zeros_like(nxt_ref)
            dw_ref[...] = jnp.zeros_like(dw_ref)

        w = w_ref[...]
        gb, gc, h = gb_ref[...].astype(F32), gc_ref[...].astype(F32), h_ref[...].astype(F32)
        gch = gc * h
        prev8 = (gcp_ref[...].astype(F32) * hp_ref[...].astype(F32))[8:16] * (i > 0).astype(F32)
        cc, s1, s2 = _causal_conv(gch, prev8, w)
        dc = dc_ref[...].astype(F32)
        dcc = dc * gb
        dgch, dws = _conv_bwd(dcc, gch, s1, s2, w, nxt_ref[...])
        d3_ref[0] = (dc * cc).astype(BF16)
        d3_ref[1] = (dgch * h).astype(BF16)
        d3_ref[2] = (dgch * gc).astype(BF16)
        for t in range(3):
            dw_ref[t:t + 1, :] += dws[t]
        nxt_ref[...] = dcc[0:8]

    cur = lambda c0: pl.BlockSpec((tr, CONV_COLS), lambda j, g: (ni - 1 - g, c0 + j))
    prv = lambda c0: pl.BlockSpec((16, CONV_COLS), lambda j, g: (jnp.maximum((ni - 1 - g) * (tr // 16) - 1, 0), c0 + j))
    return pl.pallas_call(
        body, name="convmix_bwd", grid=(ncb, ni),
        in_specs=[cur(dc0), cur(gb0), cur(gc0), cur(h0), prv(gc0), prv(h0),
                  pl.BlockSpec((8, CONV_COLS), lambda j, g: (0, j))],
        out_specs=[pl.BlockSpec((3, tr, CONV_COLS), lambda j, g: (0, ni - 1 - g, j)),
                   pl.BlockSpec((8, CONV_COLS), lambda j, g: (0, j))],
        out_shape=[jax.ShapeDtypeStruct((3, s, d - ATTN_WIDTH), BF16), jax.ShapeDtypeStruct((8, d - ATTN_WIDTH), F32)],
        scratch_shapes=[pltpu.VMEM((8, CONV_COLS), F32)],
        compiler_params=_cp(("arbitrary", "arbitrary"), 32),
    )(dmix, proj, proj, proj, proj, proj, scw8)


def _ln_fwd(z):
    mu = jnp.mean(z, axis=-1, keepdims=True)
    zc = z - mu
    var = jnp.mean(zc * zc, axis=-1, keepdims=True)
    rstd = lax.rsqrt(var + LN_EPS)
    return zc * rstd, rstd


def _ln_bwd(dout, xh, rstd, g):
    dxh = dout * g
    c1 = jnp.mean(dxh, axis=-1, keepdims=True)
    c2 = jnp.mean(dxh * xh, axis=-1, keepdims=True)
    dz = rstd * (dxh - c1 - xh * c2)
    return dz, jnp.sum(dout * xh, axis=0, keepdims=True), jnp.sum(dout, axis=0, keepdims=True)


def _outproj_ln1(attn, conv, wout, x, g1, b1, s, d):
    tm = _pick(s, 256, 16)
    ka = attn.shape[1]

    def body(a_ref, c_ref, wt_ref, wb_ref, x_ref, g_ref, b_ref, x1_ref, x1b_ref, xh_ref, rs_ref):
        y = _dot(a_ref[...], wt_ref[...], NN) + _dot(c_ref[...], wb_ref[...], NN)
        xh, rstd = _ln_fwd(ALPHA * x_ref[...] + y)
        x1 = xh * g_ref[...] + b_ref[...]
        x1_ref[...] = x1
        x1b_ref[...] = x1.astype(BF16)
        xh_ref[...] = xh.astype(BF16)
        rs_ref[...] = rstd

    row = lambda w: pl.BlockSpec((tm, w), lambda i: (i, 0))
    vec = pl.BlockSpec((1, d), lambda i: (0, 0))
    return pl.pallas_call(
        body, name="outproj_ln1", grid=(s // tm,),
        in_specs=[row(ka), row(d - ka), pl.BlockSpec((ka, d), lambda i: (0, 0)),
                  pl.BlockSpec((d - ka, d), lambda i: (ka // (d - ka), 0)), row(d), vec, vec],
        out_specs=[row(d), row(d), row(d), row(1)],
        out_shape=[jax.ShapeDtypeStruct((s, d), F32), jax.ShapeDtypeStruct((s, d), BF16),
                   jax.ShapeDtypeStruct((s, d), BF16), jax.ShapeDtypeStruct((s, 1), F32)],
        compiler_params=_cp(("arbitrary",), 48),
    )(attn, conv, wout, wout, x, g1, b1)


def _ffn_up(x1b, wup, fcw8, s, d, dff):
    tm = _pick(s, 1024, 16)
    tn = _pick(dff, 512, LANES)
    nj, ni = dff // tn, s // tm

    def body(x_ref, wa_ref, wg_ref, ca_ref, cg_ref, u_ref, h_ref, carry_ref):
        @pl.when(pl.program_id(1) == 0)
        def _():
            carry_ref[...] = jnp.zeros_like(carry_ref)

        xa = x_ref[...]
        ys = []
        for part, (w_ref, c_ref) in enumerate(((wa_ref, ca_ref), (wg_ref, cg_ref))):
            ub = _dot(xa, w_ref[...], NN).astype(BF16)
            u_ref[part] = ub
            u = ub.astype(F32)
            y, _, _ = _causal_conv(u, carry_ref[part], c_ref[...])
            carry_ref[part] = u[tm - 8:]
            ys.append(y)
        a2, g2 = ys
        sig = 1.0 / (1.0 + jnp.exp(-a2))
        h_ref[...] = (a2 * sig * g2).astype(BF16)

    return pl.pallas_call(
        body, name="ffn_up", grid=(nj, ni),
        in_specs=[pl.BlockSpec((tm, d), lambda j, i: (i, 0)),
                  pl.BlockSpec((d, tn), lambda j, i: (0, j)),
                  pl.BlockSpec((d, tn), lambda j, i: (0, j + nj)),
                  pl.BlockSpec((8, tn), lambda j, i: (0, j)),
                  pl.BlockSpec((8, tn), lambda j, i: (0, j + nj))],
        out_specs=[pl.BlockSpec((2, tm, tn), lambda j, i: (0, i, j)),
                   pl.BlockSpec((tm, tn), lambda j, i: (i, j))],
        out_shape=[jax.ShapeDtypeStruct((2, s, dff), BF16), jax.ShapeDtypeStruct((s, dff), BF16)],
        scratch_shapes=[pltpu.VMEM((2, 8, tn), F32)],
        compiler_params=_cp(("arbitrary", "arbitrary"), 48),
    )(x1b, wup, wup, fcw8, fcw8)


def _ffn_mid_bwd(dz2b, wdown, u3, fcw8, s, d, dff):
    tm = _pick(s, 1024, 16)
    tn = _pick(dff, 512, LANES)
    nj, ni = dff // tn, s // tm

    def body(dz_ref, wd_ref, u_ref, up_ref, ca_ref, cg_ref, du_ref, dw_ref, nxt_ref):
        g = pl.program_id(1)
        i = ni - 1 - g

        @pl.when(g == 0)
        def _():
            nxt_ref[...] = jnp.zeros_like(nxt_ref)
            dw_ref[...] = jnp.zeros_like(dw_ref)

        has_prev = (i > 0).astype(F32)
        saved = []
        for part, c_ref in enumerate((ca_ref, cg_ref)):
            u = u_ref[part].astype(F32)
            prev8 = up_ref[part].astype(F32)[8:16] * has_prev
            y, s1, s2 = _causal_conv(u, prev8, c_ref[...])
            saved.append((u, s1, s2, y))
        a2, g2 = saved[0][3], saved[1][3]
        sig = 1.0 / (1.0 + jnp.exp(-a2))
        silu = a2 * sig
        dhv = _dot(dz_ref[...], wd_ref[...], NT)
        dys = (dhv * g2 * (sig * (1.0 + a2 * (1.0 - sig))), dhv * silu)
        for part, (c_ref, dy) in enumerate(zip((ca_ref, cg_ref), dys)):
            u, s1, s2, _ = saved[part]
            dx, dws = _conv_bwd(dy, u, s1, s2, c_ref[...], nxt_ref[part])
            du_ref[part] = dx.astype(BF16)
            for t in range(3):
                dw_ref[part, t:t + 1, :] += dws[t]
            nxt_ref[part] = dy[0:8]

    return pl.pallas_call(
        body, name="ffn_mid_bwd", grid=(nj, ni),
        in_specs=[pl.BlockSpec((tm, d), lambda j, g: (ni - 1 - g, 0)),
                  pl.BlockSpec((tn, d), lambda j, g: (j, 0)),
                  pl.BlockSpec((2, tm, tn), lambda j, g: (0, ni - 1 - g, j)),
                  pl.BlockSpec((2, 16, tn), lambda j, g: (0, jnp.maximum((ni - 1 - g) * (tm // 16) - 1, 0), j)),
                  pl.BlockSpec((8, tn), lambda j, g: (0, j)),
                  pl.BlockSpec((8, tn), lambda j, g: (0, j + nj))],
        out_specs=[pl.BlockSpec((2, tm, tn), lambda j, g: (0, ni - 1 - g, j)),
                   pl.BlockSpec((2, 8, tn), lambda j, g: (0, 0, j))],
        out_shape=[jax.ShapeDtypeStruct((2, s, dff), BF16), jax.ShapeDtypeStruct((2, 8, dff), F32)],
        scratch_shapes=[pltpu.VMEM((2, 8, tn), F32)],
        compiler_params=_cp(("arbitrary", "arbitrary"), 56),
    )(dz2b, wdown, u3, u3, fcw8, fcw8)


def _ffn_down_loss(hmid, wdown, x1, target, g2, b2, s, d, dff):
    tm = _pick(s, 512, SLAB)
    tk = _pick(dff, 1408, LANES)
    ni, nk = s // tm, dff // tk
    slab = min(SLAB, tm)

    def body(h_ref, w_ref, x1_ref, t_ref, g_ref, b_ref, dzb_ref, st_ref, acc_ref):
        i, kk = pl.program_id(0), pl.program_id(1)

        @pl.when((i == 0) & (kk == 0))
        def _():
            st_ref[...] = jnp.zeros_like(st_ref)

        part = _dot(h_ref[...], w_ref[...], NN)

        @pl.when(kk == 0)
        def _():
            acc_ref[...] = part

        @pl.when(kk > 0)
        def _():
            acc_ref[...] += part

        @pl.when(kk == nk - 1)
        def _():
            g, b = g_ref[...], b_ref[...]

            def one(sl, carry):
                rows = pl.ds(pl.multiple_of(sl * slab, slab), slab)
                xh, rstd = _ln_fwd(ALPHA * x1_ref[rows, :] + acc_ref[rows, :])
                diff = xh * g + b - t_ref[rows, :]
                sq = jnp.sum(jnp.sum(diff * diff, axis=1, keepdims=True), axis=0, keepdims=True)
                dz, dg, db = _ln_bwd(diff * (1.0 / d), xh, rstd, g)
                dzb_ref[rows, :] = dz.astype(BF16)
                st_ref[0:1, :] += dg
                st_ref[1:2, :] += db
                st_ref[2:3, :] += sq
                return carry

            lax.fori_loop(0, tm // slab, one, 0)

    row = pl.BlockSpec((tm, d), lambda i, kk: (i, 0))
    vec = pl.BlockSpec((1, d), lambda i, kk: (0, 0))
    return pl.pallas_call(
        body, name="ffn_down_loss", grid=(ni, nk),
        in_specs=[pl.BlockSpec((tm, tk), lambda i, kk: (i, kk)), pl.BlockSpec((tk, d), lambda i, kk: (kk, 0)),
                  row, row, vec, vec],
        out_specs=[row, pl.BlockSpec((8, d), lambda i, kk: (0, 0))],
        out_shape=[jax.ShapeDtypeStruct((s, d), BF16), jax.ShapeDtypeStruct((8, d), F32)],
        scratch_shapes=[pltpu.VMEM((tm, d), F32)],
        compiler_params=_cp(("arbitrary", "arbitrary"), 48),
    )(hmid, wdown, x1, target, g2, b2)


def _ffn_dx_ln1_bwd(du3, wup, dz2b, xh1, rstd1, g1, s, d, dff):
    tm = _pick(s, 1024, SLAB)
    tk = _pick(dff, 1408, LANES)
    nkh = dff // tk
    ni, nk = s // tm, 2 * nkh
    slab = min(SLAB, tm)

    def body(a_ref, w_ref, dz2_ref, xh_ref, rs_ref, g_ref, dzb_ref, st_ref, acc_ref):
        i, kk = pl.program_id(0), pl.program_id(1)

        @pl.when((i == 0) & (kk == 0))
        def _():
            st_ref[...] = jnp.zeros_like(st_ref)

        part = _dot(a_ref[...], w_ref[...], NT)

        @pl.when(kk == 0)
        def _():
            acc_ref[...] = part

        @pl.when(kk > 0)
        def _():
            acc_ref[...] += part

        @pl.when(kk == nk - 1)
        def _():
            g = g_ref[...]

            def one(sl, carry):
                rows = pl.ds(pl.multiple_of(sl * slab, slab), slab)
                dx1 = ALPHA * dz2_ref[rows, :].astype(F32) + acc_ref[rows, :]
                dz, dg, db = _ln_bwd(dx1, xh_ref[rows, :].astype(F32), rs_ref[rows, :], g)
                dzb_ref[rows, :] = dz.astype(BF16)
                st_ref[0:1, :] += dg
                st_ref[1:2, :] += db
                return carry

            lax.fori_loop(0, tm // slab, one, 0)

    row = pl.BlockSpec((tm, d), lambda i, kk: (i, 0))
    row1 = pl.BlockSpec((tm, d), lambda i, kk: (i, 0), pipeline_mode=pl.Buffered(1))
    return pl.pallas_call(
        body, name="ffn_dx_ln1_bwd", grid=(ni, nk),
        in_specs=[pl.BlockSpec((None, tm, tk), lambda i, kk: (kk // nkh, i, kk % nkh)),
                  pl.BlockSpec((d, tk), lambda i, kk: (0, kk)),
                  row1, row1, pl.BlockSpec((tm, 1), lambda i, kk: (i, 0)), pl.BlockSpec((1, d), lambda i, kk: (0, 0))],
        out_specs=[row, pl.BlockSpec((8, d), lambda i, kk: (0, 0))],
        out_shape=[jax.ShapeDtypeStruct((s, d), BF16), jax.ShapeDtypeStruct((8, d), F32)],
        scratch_shapes=[pltpu.VMEM((tm, d), F32)],
        compiler_params=_cp(("arbitrary", "arbitrary"), 56),
    )(du3, wup, dz2b, xh1, rstd1, g1)


def _phase_mixer(x, x_in, win_t, wout, scw8, sinks, ln1_g, ln1_b):
    s, d = x.shape
    n_in = win_t.shape[0]
    xb = x_in.astype(BF16)
    cos, sin = _rope_tables(s)
    proj = _matmul(xb, win_t, mode="nt", m=s, n=n_in, k=d, tm=_pick(s, 512, 16), tn=_pick(n_in, 2176, LANES), tk=d,
                   out_dtype=BF16, name="in_proj", vmem_mb=48)
    attn = _attn_fwd(proj, sinks, cos, sin, s)
    conv = _convmix_fwd(proj, scw8, s, d)
    x1, x1b, xh1, rstd1 = _outproj_ln1(attn, conv, wout, x, ln1_g, ln1_b, s, d)
    return dict(xb=xb, cos=cos, sin=sin, proj=proj, attn=attn, conv=conv, x1=x1, x1b=x1b, xh1=xh1, rstd1=rstd1)


def _phase_ffn(a, target, wup, wdown, fcw8, ln2_g, ln2_b):
    x1, x1b = a["x1"], a["x1b"]
    s, d = x1.shape
    dff = wdown.shape[0]
    u3, hmid = _ffn_up(x1b, wup, fcw8, s, d, dff)
    dz2b, st2 = _ffn_down_loss(hmid, wdown, x1, target, ln2_g, ln2_b, s, d, dff)

    ts = _pick(s, 2048, 16)
    g_wdown = _matmul(hmid, dz2b, mode="tn", m=dff, n=d, k=s, tm=_pick(dff, 1408, LANES), tn=_pick(d, 1024, LANES),
                      tk=ts, out_dtype=BF16, name="grad_w_down", vmem_mb=48)
    du3, dfcw = _ffn_mid_bwd(dz2b, wdown, u3, fcw8, s, d, dff)
    tnu = _pick(dff, 1408, LANES)
    njh = dff // tnu
    g_wup = _matmul(x1b, du3, mode="tn", m=d, n=2 * dff, k=s, tm=_pick(d, 1024, LANES), tn=tnu, tk=ts, out_dtype=BF16,
                    name="grad_w_up", vmem_mb=48,
                    b_spec=pl.BlockSpec((None, ts, tnu), lambda j, i, kk: (j // njh, kk, j % njh)))
    return dict(du3=du3, dz2b=dz2b, st2=st2, dfcw=dfcw, wdown=g_wdown, wup=g_wup)


def _phase_rest(a, f, wup, wout, win_t, scw8, sinks, ln1_g):
    xb, cos, sin, proj, attn, conv = a["xb"], a["cos"], a["sin"], a["proj"], a["attn"], a["conv"]
    du3, dz2b, st2, dfcw = f["du3"], f["dz2b"], f["st2"], f["dfcw"]
    s, d = a["x1"].shape
    dff = wup.shape[1] // 2
    n_in = win_t.shape[0]
    ts = _pick(s, 2048, 16)
    dz1b, st1 = _ffn_dx_ln1_bwd(du3, wup, dz2b, a["xh1"], a["rstd1"], ln1_g, s, d, dff)

    mix = jnp.concatenate([attn, conv], axis=1)
    g_wout = _matmul(mix, dz1b, mode="tn", m=d, n=d, k=s, tm=_pick(d, 1024, LANES), tn=_pick(d, 1024, LANES), tk=ts,
                     out_dtype=BF16, name="grad_w_out", vmem_mb=48)
    dmix = _matmul(dz1b, wout, mode="nt", m=s, n=d, k=d, tm=_pick(s, 1024, 16), tn=_pick(d, 1024, LANES), tk=d,
                   out_dtype=BF16, name="out_dmix", vmem_mb=48)
    d3, dscw = _convmix_bwd(proj, dmix, scw8, s, d)
    dq, dk, dv, dsink = _attn_bwd(proj, dmix, sinks, cos, sin, s)
    dproj = jnp.concatenate([dq, dk, dv, d3[0], d3[1], d3[2]], axis=1)
    g_win_t = _matmul(dproj, xb, mode="tn", m=n_in, n=d, k=s, tm=_pick(n_in, 2176, LANES), tn=_pick(d, 512, LANES),
                      tk=ts, out_dtype=BF16, name="grad_w_in", vmem_mb=48)
    small = dict(loss_sq=st2[2, 0], ln2_g=st2[0], ln2_b=st2[1], ln1_g=st1[0], ln1_b=st1[1], sinks=dsink[0, :N_Q_HEADS],
                 fcw=jnp.concatenate([dfcw[0, :3], dfcw[1, :3]], axis=1), scw=dscw[:3])
    return (dproj, dz1b), dict(win_t=g_win_t, wout=g_wout), small


def _grad_x(dproj, dz1b, win_t, after=None):
    s, n_in = dproj.shape
    d = win_t.shape[1]
    return _matmul(dproj, win_t, mode="nn", m=s, n=d, k=n_in, tm=_pick(s, 512, 16), tn=_pick(d, 1024, LANES),
                   tk=n_in, out_dtype=F32, name="grad_x", vmem_mb=56, res=dz1b, alpha=ALPHA, after=after)


def _local_step(x, target, win_t, wout, wup, wdown, scw8, fcw8, sinks, ln1_g, ln1_b, ln2_g, ln2_b):
    a = _phase_mixer(x, x, win_t, wout, scw8, sinks, ln1_g, ln1_b)
    f = _phase_ffn(a, target, wup, wdown, fcw8, ln2_g, ln2_b)
    (dproj, dz1b), g, small = _phase_rest(a, f, wup, wout, win_t, scw8, sinks, ln1_g)
    return _grad_x(dproj, dz1b, win_t), dict(g, wup=f["wup"], wdown=f["wdown"]), small


MIXER = ("win_t", "wout")
FFN = ("wup", "wdown")
BIG = MIXER + FFN


def _geom(shard_shapes):
    out = {}
    for name in BIG:
        r, c = shard_shapes[name]
        out[name] = ("col" if name == "wup" else "row", (r, c), (r // 2, c))
    return out


def _full_shape(kind, shard):
    r, c = shard
    return (N_CHIPS * r, c) if kind == "row" else (r, N_CHIPS * c)


def _piece_of(ref, kind, shard, chip, half):
    r, c = shard
    if kind == "row":
        return ref.at[pl.ds(chip * r + half * (r // 2), r // 2), :]
    return ref.at[pl.ds(half * (r // 2), r // 2), pl.ds(chip * c, c)]


def _shard_piece(ref, shard, half):
    r, _ = shard
    return ref.at[pl.ds(half * (r // 2), r // 2), :]


def _me():
    return lax.axis_index("x"), lax.axis_index("y"), lax.axis_index("c")


def _other_chips(x, y):
    return [(1 - x, y), (x, 1 - y), (1 - x, 1 - y)]


def _remote(src, dst, send_sem, recv_sem, dev):
    return pltpu.make_async_remote_copy(src_ref=src, dst_ref=dst, send_sem=send_sem, recv_sem=recv_sem,
                                        device_id=dev, device_id_type=MESH)


def _place_shard(w, chip1, kind, name):
    r, c = w.shape
    tr = _rows_tile(r, c, 16)
    nt = r // tr

    def body(chip_ref, w_ref, o_ref):
        o_ref[...] = w_ref[...].astype(BF16)

    out_map = (lambda i, chip_ref: (chip_ref[0] * nt + i, 0)) if kind == "row" else (lambda i, chip_ref: (i, chip_ref[0]))
    return pl.pallas_call(
        body, name="place_" + name,
        grid_spec=pltpu.PrefetchScalarGridSpec(
            num_scalar_prefetch=1, grid=(nt,),
            in_specs=[pl.BlockSpec((tr, c), lambda i, chip_ref: (i, 0))],
            out_specs=pl.BlockSpec((tr, c), out_map)),
        out_shape=jax.ShapeDtypeStruct(_full_shape(kind, (r, c)), BF16),
        compiler_params=_cp(("arbitrary",), 32),
    )(chip1, w)


def _allgather_weights(names, placed, geom, small_shards):
    nb, ns = len(names), len(small_shards)
    small_w = [a.shape[1] for a in small_shards]

    def body(*refs):
        sm = refs[nb:nb + ns]
        full = refs[nb + ns:2 * nb + ns]
        smf = refs[2 * nb + ns:2 * nb + 2 * ns]
        send, recv, loc = refs[2 * nb + 2 * ns:]
        x, y, c = _me()
        chip = 2 * x + y
        sib = (x, y, 1 - c)
        others = _other_chips(x, y)
        locals_, sends = [], []
        for m, name in enumerate(names):
            kind, shard, _ = geom[name]
            mine = _piece_of(full[m], kind, shard, chip, c)
            for k, (qx, qy) in enumerate(others):
                cp = _remote(mine, mine, send.at[6 * m + k], recv.at[6 * m + k], (qx, qy, c))
                cp.start()
                sends.append(cp)
        for t in range(ns):
            cp = pltpu.make_async_copy(sm[t], smf[t].at[:, pl.ds(chip * small_w[t], small_w[t])], loc.at[t])
            cp.start()
            locals_.append(cp)
            for k, (qx, qy) in enumerate(others):
                cp = _remote(sm[t], smf[t].at[:, pl.ds(chip * small_w[t], small_w[t])],
                             send.at[6 * nb + 3 * t + k], recv.at[6 * nb + 3 * t + k], (qx, qy, c))
                cp.start()
                sends.append(cp)
        for m, name in enumerate(names):
            kind, shard, _ = geom[name]
            for k, (qx, qy) in enumerate(others):
                got = _piece_of(full[m], kind, shard, 2 * qx + qy, c)
                _remote(got, got, send.at[6 * m + k], recv.at[6 * m + k], (qx, qy, c)).wait_recv()
                cp = _remote(got, got, send.at[6 * m + 3 + k], recv.at[6 * m + 3 + k], sib)
                cp.start()
                sends.append(cp)
        for t in range(ns):
            for k, (qx, qy) in enumerate(others):
                got = smf[t].at[:, pl.ds((2 * qx + qy) * small_w[t], small_w[t])]
                _remote(got, got, send.at[6 * nb + 3 * t + k], recv.at[6 * nb + 3 * t + k], (qx, qy, c)).wait_recv()
        for m, name in enumerate(names):
            kind, shard, _ = geom[name]
            for k, (qx, qy) in enumerate(others):
                got = _piece_of(full[m], kind, shard, 2 * qx + qy, 1 - c)
                _remote(got, got, send.at[6 * m + 3 + k], recv.at[6 * m + 3 + k], sib).wait_recv()
        for cp in sends:
            cp.wait_send()
        for cp in locals_:
            cp.wait()

    nsem = 6 * nb + 3 * ns
    out_shape = [jax.ShapeDtypeStruct(placed[n].shape, BF16) for n in names]
    out_shape += [jax.ShapeDtypeStruct((8, N_CHIPS * w), F32) for w in small_w]
    outs = pl.pallas_call(
        body, name="allgather_weights", in_specs=[ANY] * (nb + ns), out_specs=[ANY] * (nb + ns), out_shape=out_shape,
        input_output_aliases={m: m for m in range(nb)},
        scratch_shapes=[pltpu.SemaphoreType.DMA((nsem,)), pltpu.SemaphoreType.DMA((nsem,)),
                        pltpu.SemaphoreType.DMA((ns,))],
    )(*[placed[n] for n in names], *small_shards)
    return dict(zip(names, outs[:nb])), list(outs[nb:])


def _sibling_exchange(names, grads, geom):
    nb = len(names)

    def body(*refs):
        g = refs[:nb]
        got = refs[nb:2 * nb]
        send, recv = refs[2 * nb:]
        x, y, c = _me()
        sib = (x, y, 1 - c)
        cps = []
        for m, name in enumerate(names):
            kind, shard, _ = geom[name]
            for r in range(N_CHIPS):
                cp = _remote(_piece_of(g[m], kind, shard, r, 1 - c), got[m].at[r],
                             send.at[N_CHIPS * m + r], recv.at[N_CHIPS * m + r], sib)
                cp.start()
                cps.append(cp)
        for cp in cps:
            cp.wait_recv()
        for cp in cps:
            cp.wait_send()

    return pl.pallas_call(
        body, name="grad_sibling_exchange_" + names[0], in_specs=[ANY] * nb, out_specs=[ANY] * nb,
        out_shape=[jax.ShapeDtypeStruct((N_CHIPS,) + geom[n][2], BF16) for n in names],
        scratch_shapes=[pltpu.SemaphoreType.DMA((N_CHIPS * nb,)), pltpu.SemaphoreType.DMA((N_CHIPS * nb,))],
    )(*[grads[n] for n in names])


def _sibling_assemble(names, shards, geom):
    nb = len(names)

    def body(*refs):
        full = refs[nb:2 * nb]
        send, recv = refs[2 * nb:]
        x, y, c = _me()
        sib = (x, y, 1 - c)
        cps = []
        for m, name in enumerate(names):
            mine = _shard_piece(full[m], geom[name][1], c)
            cp = _remote(mine, mine, send.at[m], recv.at[m], sib)
            cp.start()
            cps.append(cp)
        for m, name in enumerate(names):
            theirs = _shard_piece(full[m], geom[name][1], 1 - c)
            _remote(theirs, theirs, send.at[m], recv.at[m], sib).wait_recv()
        for cp in cps:
            cp.wait_send()

    return pl.pallas_call(
        body, name="grad_sibling_assemble_" + names[0], in_specs=[ANY] * nb, out_specs=[ANY] * nb,
        out_shape=[jax.ShapeDtypeStruct(geom[n][1], F32) for n in names],
        input_output_aliases={m: m for m in range(nb)},
        scratch_shapes=[pltpu.SemaphoreType.DMA((nb,)), pltpu.SemaphoreType.DMA((nb,))],
    )(*shards)


HBM = pl.BlockSpec(memory_space=pltpu.HBM)
SEM = pl.BlockSpec(memory_space=pltpu.SEMAPHORE)
EFFECT = pltpu.SideEffectType.DATAFLOW_SIDE_EFFECTING
TOKEN = jax.ShapeDtypeStruct((8, LANES), F32)


def _hbm(a):
    return pltpu.with_memory_space_constraint(a, pltpu.HBM)


def _gather_copies(names, full, geom, send, recv):
    x, y, c = _me()
    out = []
    for m, name in enumerate(names):
        kind, shard, _ = geom[name]
        mine = _piece_of(full[m], kind, shard, 2 * x + y, c)
        for k, (qx, qy) in enumerate(_other_chips(x, y)):
            theirs = _piece_of(full[m], kind, shard, 2 * qx + qy, c)
            out.append((_remote(mine, mine, send.at[3 * m + k], recv.at[3 * m + k], (qx, qy, c)),
                        _remote(theirs, theirs, send.at[3 * m + k], recv.at[3 * m + k], (qx, qy, c))))
    return out


def _gather_start(names, placed, geom, after):
    nb = len(names)

    def body(*refs):
        full = refs[:nb]
        send, recv = refs[nb + 1], refs[nb + 2]
        token = refs[2 * nb + 3]
        for cp, _ in _gather_copies(names, full, geom, send, recv):
            cp.start()
        token[...] = jnp.zeros_like(token)

    outs = pl.pallas_call(
        body, name="gather_start_" + names[0],
        out_shape=(pltpu.SemaphoreType.DMA((3 * nb,)), pltpu.SemaphoreType.DMA((3 * nb,)),
                   *[pltpu.HBM(placed[n].shape, BF16) for n in names], TOKEN),
        in_specs=[HBM] * nb + [ANY], out_specs=(SEM, SEM, *[HBM] * nb, pl.BlockSpec(memory_space=pltpu.VMEM)),
        input_output_aliases={m: 2 + m for m in range(nb)},
        compiler_params=pltpu.CompilerParams(has_side_effects=EFFECT),
    )(*[_hbm(placed[n]) for n in names], after)
    return outs[0], outs[1], list(outs[2:2 + nb]), outs[2 + nb]


def _gather_wait(names, send, recv, thru, geom, after):
    nb = len(names)

    def body(*refs):
        full = refs[:nb]
        for mine, theirs in _gather_copies(names, full, geom, refs[nb], refs[nb + 1]):
            mine.wait_send()
            theirs.wait_recv()

    return pl.pallas_call(
        body, name="gather_wait_" + names[0], out_shape=tuple(pltpu.HBM(t.shape, t.dtype) for t in thru),
        in_specs=[HBM] * nb + [SEM, SEM, ANY], out_specs=tuple([HBM] * nb),
        input_output_aliases={m: m for m in range(nb)},
        compiler_params=pltpu.CompilerParams(has_side_effects=EFFECT),
    )(*thru, send, recv, after)


def _gather_forward(names, full, geom):
    nb = len(names)

    def body(*refs):
        arr = refs[nb:2 * nb]
        send, recv = refs[2 * nb:]
        x, y, c = _me()
        sib = (x, y, 1 - c)
        cps = []
        for m, name in enumerate(names):
            kind, shard, _ = geom[name]
            for k, (qx, qy) in enumerate(_other_chips(x, y)):
                got = _piece_of(arr[m], kind, shard, 2 * qx + qy, c)
                cp = _remote(got, got, send.at[3 * m + k], recv.at[3 * m + k], sib)
                cp.start()
                cps.append(cp)
        for m, name in enumerate(names):
            kind, shard, _ = geom[name]
            for k, (qx, qy) in enumerate(_other_chips(x, y)):
                theirs = _piece_of(arr[m], kind, shard, 2 * qx + qy, 1 - c)
                _remote(theirs, theirs, send.at[3 * m + k], recv.at[3 * m + k], sib).wait_recv()
        for cp in cps:
            cp.wait_send()

    return pl.pallas_call(
        body, name="gather_forward_" + names[0], in_specs=[ANY] * nb, out_specs=[ANY] * nb,
        out_shape=[jax.ShapeDtypeStruct(a.shape, a.dtype) for a in full],
        input_output_aliases={m: m for m in range(nb)},
        scratch_shapes=[pltpu.SemaphoreType.DMA((3 * nb,)), pltpu.SemaphoreType.DMA((3 * nb,))],
    )(*full)


def _scatter_copies(nb, t, got, send, recv):
    x, y, c = _me()
    return [_remote(t[m].at[2 * qx + qy], got[m].at[k], send.at[3 * m + k], recv.at[3 * m + k], (qx, qy, c))
            for m in range(nb) for k, (qx, qy) in enumerate(_other_chips(x, y))]


def _chip_exchange_start(names, chip_sums, geom, after):
    nb = len(names)
    lands = [lax.empty((N_CHIPS - 1,) + geom[n][2], BF16) for n in names]

    def body(*refs):
        t, got = refs[:nb], refs[nb:2 * nb]
        send, recv = refs[2 * nb + 1], refs[2 * nb + 2]
        token = refs[4 * nb + 3]
        for cp in _scatter_copies(nb, t, got, send, recv):
            cp.start()
        token[...] = jnp.zeros_like(token)

    both = list(chip_sums) + lands
    outs = pl.pallas_call(
        body, name="grad_chip_start_" + names[0],
        out_shape=(pltpu.SemaphoreType.DMA((3 * nb,)), pltpu.SemaphoreType.DMA((3 * nb,)),
                   *[pltpu.HBM(a.shape, a.dtype) for a in both], TOKEN),
        in_specs=[HBM] * (2 * nb) + [ANY],
        out_specs=(SEM, SEM, *[HBM] * (2 * nb), pl.BlockSpec(memory_space=pltpu.VMEM)),
        input_output_aliases={m: 2 + m for m in range(2 * nb)},
        compiler_params=pltpu.CompilerParams(has_side_effects=EFFECT),
    )(*[_hbm(a) for a in both], after)
    return outs[0], outs[1], list(outs[2:2 + 2 * nb]), outs[2 + 2 * nb]


def _chip_exchange_wait(names, send, recv, thru, after):
    nb = len(names)

    def body(*refs):
        for cp in _scatter_copies(nb, refs[:nb], refs[nb:2 * nb], refs[2 * nb], refs[2 * nb + 1]):
            cp.wait_send()
            cp.wait_recv()

    outs = pl.pallas_call(
        body, name="grad_chip_wait_" + names[0], out_shape=tuple(pltpu.HBM(t.shape, t.dtype) for t in thru),
        in_specs=[HBM] * (2 * nb) + [SEM, SEM, ANY], out_specs=tuple([HBM] * (2 * nb)),
        input_output_aliases={m: m for m in range(2 * nb)},
        compiler_params=pltpu.CompilerParams(has_side_effects=EFFECT),
    )(*thru, send, recv, after)
    return list(outs[nb:])


def _allreduce_small(part):
    rows = part.shape[0]
    flips = [(a, b, e) for a in (0, 1) for b in (0, 1) for e in (0, 1) if (a, b, e) != (0, 0, 0)]

    def body(p_ref, o_ref, all_ref, send, recv):
        x, y, c = _me()
        me = 4 * x + 2 * y + c
        all_ref[me] = p_ref[...]
        cps = []
        for k, (a, b, e) in enumerate(flips):
            cp = _remote(p_ref, all_ref.at[me], send.at[k], recv.at[k], (x ^ a, y ^ b, c ^ e))
            cp.start()
            cps.append(cp)
        for k, (a, b, e) in enumerate(flips):
            peer = 4 * (x ^ a) + 2 * (y ^ b) + (c ^ e)
            _remote(p_ref, all_ref.at[peer], send.at[k], recv.at[k], (x ^ a, y ^ b, c ^ e)).wait_recv()
        for cp in cps:
            cp.wait_send()
        tot = all_ref[0]
        for dev in range(1, 8):
            tot = tot + all_ref[dev]
        o_ref[...] = tot

    vm = pl.BlockSpec(memory_space=pltpu.VMEM)
    return pl.pallas_call(
        body, name="allreduce_small", in_specs=[vm], out_specs=vm, out_shape=jax.ShapeDtypeStruct((rows, LANES), F32),
        scratch_shapes=[pltpu.VMEM((8, rows, LANES), F32), pltpu.SemaphoreType.DMA((7,)), pltpu.SemaphoreType.DMA((7,))],
    )(part)


def _rows_tile(rows, cols, mult):
    return _pick(rows, max(mult, (1 << 19) // cols // mult * mult), mult)


def _add_pairs(g, got, kind, shard, where, name):
    p, r, c = got.shape
    tr = _rows_tile(r, c, 16)
    nt = r // tr

    def body(w_ref, a_ref, b_ref, o_ref):
        o_ref[...] = (a_ref[...].astype(F32) + b_ref[...].astype(F32)).astype(BF16)

    if kind == "row":
        g_map = lambda q, i, w_ref: ((2 * q + w_ref[1]) * nt + i, 0)
    else:
        g_map = lambda q, i, w_ref: (w_ref[1] * nt + i, q)
    spec = pl.BlockSpec((None, tr, c), lambda q, i, w_ref: (q, i, 0))
    return pl.pallas_call(
        body, name="grad_add_sibling_" + name,
        grid_spec=pltpu.PrefetchScalarGridSpec(
            num_scalar_prefetch=1, grid=(p, nt), in_specs=[pl.BlockSpec((tr, c), g_map), spec], out_specs=spec),
        out_shape=jax.ShapeDtypeStruct((p, r, c), BF16), compiler_params=_cp(("arbitrary", "arbitrary"), 32),
    )(where, g, got)


def _add_four(t, got, shard, where, name):
    _, r, c = t.shape
    tr = _rows_tile(r, c, 16)
    nt = r // tr

    def body(w_ref, own, t0, t1, t2, o_ref):
        o_ref[...] = ((own[...].astype(F32) + t0[...].astype(F32)) + t1[...].astype(F32)) + t2[...].astype(F32)

    spec = lambda q: pl.BlockSpec((None, tr, c), lambda i, w_ref: (q, i, 0))
    return pl.pallas_call(
        body, name="grad_add_chips_" + name,
        grid_spec=pltpu.PrefetchScalarGridSpec(
            num_scalar_prefetch=1, grid=(nt,),
            in_specs=[pl.BlockSpec((None, tr, c), lambda i, w_ref: (w_ref[0], i, 0)), spec(0), spec(1), spec(2)],
            out_specs=pl.BlockSpec((tr, c), lambda i, w_ref: (w_ref[1] * nt + i, 0))),
        out_shape=jax.ShapeDtypeStruct(shard, F32), compiler_params=_cp(("arbitrary",), 32),
    )(where, t, got, got, got)


def _adamw(w, g, m, v, name):
    r, c = w.shape
    tr = _rows_tile(r, c, 8)

    def body(w_ref, g_ref, m_ref, v_ref, go_ref, d_ref, mo_ref, vo_ref):
        gv = g_ref[...]
        mn = ADAM_B1 * m_ref[...] + (1.0 - ADAM_B1) * gv
        vn = ADAM_B2 * v_ref[...] + (1.0 - ADAM_B2) * (gv * gv)
        m_hat = mn / (1.0 - ADAM_B1 ** ADAM_STEP)
        v_hat = vn / (1.0 - ADAM_B2 ** ADAM_STEP)
        go_ref[...] = gv
        d_ref[...] = -ADAM_LR * (m_hat / (jnp.sqrt(v_hat) + ADAM_EPS) + ADAM_WD * w_ref[...])
        mo_ref[...] = mn
        vo_ref[...] = vn

    spec = pl.BlockSpec((tr, c), lambda i: (i, 0))
    return pl.pallas_call(
        body, name=name, grid=(r // tr,), in_specs=[spec] * 4, out_specs=[spec] * 4,
        out_shape=[jax.ShapeDtypeStruct((r, c), F32)] * 4, compiler_params=_cp(("arbitrary",), 32),
    )(w, g, m, v)


def _pack(vectors, rows):
    flat = jnp.concatenate([v.reshape(-1).astype(F32) for v in vectors])
    return jnp.pad(flat, (0, rows * LANES - flat.shape[0])).reshape(rows, LANES)


def _unpack(packed, shapes):
    flat = packed.reshape(-1)
    out, off = [], 0
    for shp in shapes:
        n = 1
        for t in shp:
            n *= t
        out.append(flat[off:off + n].reshape(shp))
        off += n
    return out


def _rows_for(shapes):
    n = sum(functools.reduce(lambda a, b: a * b, shp, 1) for shp in shapes)
    return -(-n // (8 * LANES)) * 8


def kernel(x, w_in, attn_sinks, short_conv_w, w_out, ln1_g, ln1_b, ffn_w_up, ffn_conv_w, ffn_w_down, ln2_g, ln2_b, loss_target, m_w_in, m_attn_sinks, m_short_conv_w, m_w_out, m_ln1_g, m_ln1_b, m_ffn_w_up, m_ffn_conv_w, m_ffn_w_down, m_ln2_g, m_ln2_b, v_w_in, v_attn_sinks, v_short_conv_w, v_w_out, v_ln1_g, v_ln1_b, v_ffn_w_up, v_ffn_conv_w, v_ffn_w_down, v_ln2_g, v_ln2_b):
    xs, tgt = x[0], loss_target[0]
    s, d = xs.shape
    chip = 2 * lax.axis_index("x") + lax.axis_index("y")

    t_in = lambda a: a[0].T
    w_big = dict(win_t=t_in(w_in), wout=w_out[0], wup=ffn_w_up[0], wdown=ffn_w_down[0])
    m_big = dict(win_t=t_in(m_w_in), wout=m_w_out[0], wup=m_ffn_w_up[0], wdown=m_ffn_w_down[0])
    v_big = dict(win_t=t_in(v_w_in), wout=v_w_out[0], wup=v_ffn_w_up[0], wdown=v_ffn_w_down[0])
    geom = _geom({n: w_big[n].shape for n in BIG})
    pad8 = lambda a: jnp.pad(a[0], ((0, 5), (0, 0)))
    where = jnp.stack([chip, lax.axis_index("c")]).astype(jnp.int32)
    placed = {n: _place_shard(w_big[n], where[:1], geom[n][0], n) for n in BIG}
    full, (scw8, fcw8) = _allgather_weights(MIXER, placed, geom, [pad8(short_conv_w), pad8(ffn_conv_w)])
    send, recv, thru, token = _gather_start(FFN, placed, geom, scw8)
    a = _phase_mixer(xs, xs + token[0, 0], full["win_t"], full["wout"], scw8, attn_sinks, ln1_g, ln1_b)
    landed = _gather_forward(FFN, _gather_wait(FFN, send, recv, thru, geom, a["x1b"]), geom)
    full.update(zip(FFN, landed))
    f = _phase_ffn(a, tgt, full["wup"], full["wdown"], fcw8, ln2_g, ln2_b)

    def chip_sums_of(names, grads):
        from_sibling = _sibling_exchange(names, grads, geom)
        return [_add_pairs(grads[n], from_sibling[m], geom[n][0], geom[n][1], where, n) for m, n in enumerate(names)]

    ffn_sums = chip_sums_of(FFN, f)
    send, recv, thru, token = _chip_exchange_start(FFN, ffn_sums, geom, f["st2"])
    (dproj, dz1b), g_mixer, g_small = _phase_rest(a, f, full["wup"], full["wout"], full["win_t"], scw8, attn_sinks,
                                                  ln1_g + token[0:1, 0:1])

    def finish(names, sums, from_chips):
        halves = [_add_four(sums[m], from_chips[m], geom[n][1], where, n) for m, n in enumerate(names)]
        shards = _sibling_assemble(names, halves, geom)
        return {n: _adamw(w_big[n], shards[m], m_big[n], v_big[n], "adamw_" + n) for m, n in enumerate(names)}

    mixer_sums = chip_sums_of(MIXER, g_mixer)
    send2, recv2, thru2, token2 = _chip_exchange_start(MIXER, mixer_sums, geom, f["st2"])
    grad_x = _grad_x(dproj, dz1b, full["win_t"], after=token2)
    upd = finish(FFN, ffn_sums, _chip_exchange_wait(FFN, send, recv, thru, grad_x))
    upd.update(finish(MIXER, mixer_sums, _chip_exchange_wait(MIXER, send2, recv2, thru2, upd[FFN[0]][1])))

    small_names = ("ln1_g", "ln1_b", "ln2_g", "ln2_b", "sinks", "fcw", "scw")
    small_shapes = [g_small[n].shape for n in small_names]
    red = _allreduce_small(_pack([g_small["loss_sq"].reshape(1)] + [g_small[n] for n in small_names],
                                 _rows_for([(1,)] + small_shapes)))
    loss_sq, *gs = _unpack(red, [(1,)] + small_shapes)
    gs = dict(zip(small_names, gs))
    loss = (0.5 / d) * loss_sq[0]
    fw, sw = ffn_conv_w.shape[2], short_conv_w.shape[2]
    gs["fcw"] = lax.dynamic_slice_in_dim(gs["fcw"], chip * fw, fw, axis=1)
    gs["scw"] = lax.dynamic_slice_in_dim(gs["scw"], chip * sw, sw, axis=1)

    upd["win_t"] = tuple(a.T for a in upd["win_t"])
    sm_w = dict(ln1_g=ln1_g[0], ln1_b=ln1_b[0], ln2_g=ln2_g[0], ln2_b=ln2_b[0], sinks=attn_sinks[0],
                fcw=ffn_conv_w[0], scw=short_conv_w[0])
    sm_m = dict(ln1_g=m_ln1_g[0], ln1_b=m_ln1_b[0], ln2_g=m_ln2_g[0], ln2_b=m_ln2_b[0], sinks=m_attn_sinks[0],
                fcw=m_ffn_conv_w[0], scw=m_short_conv_w[0])
    sm_v = dict(ln1_g=v_ln1_g[0], ln1_b=v_ln1_b[0], ln2_g=v_ln2_g[0], ln2_b=v_ln2_b[0], sinks=v_attn_sinks[0],
                fcw=v_ffn_conv_w[0], scw=v_short_conv_w[0])
    shapes = [sm_w[n].shape for n in small_names]
    rows = _rows_for(shapes)
    packed = [_pack([t[n] for n in small_names], rows) for t in (sm_w, gs, sm_m, sm_v)]
    sm_out = [dict(zip(small_names, _unpack(a, shapes))) for a in _adamw(*packed, "adamw_small")]

    def leaf(kind, name):
        if name in ("w_in", "w_out", "ffn_w_up", "ffn_w_down"):
            key = dict(w_in="win_t", w_out="wout", ffn_w_up="wup", ffn_w_down="wdown")[name]
            return upd[key][kind][None]
        key = dict(attn_sinks="sinks", short_conv_w="scw", ffn_conv_w="fcw").get(name, name)
        return sm_out[kind][key][None]

    order = ("w_in", "attn_sinks", "short_conv_w", "w_out", "ln1_g", "ln1_b", "ffn_w_up", "ffn_conv_w", "ffn_w_down",
             "ln2_g", "ln2_b")
    outs = [loss, grad_x[None]]
    for kind in range(4):
        outs += [leaf(kind, n) for n in order]
    return tuple(outs)
```

```python
import functools

import jax
import jax.numpy as jnp
from jax import lax
from jax.experimental import pallas as pl
from jax.experimental.pallas import tpu as pltpu

F32 = jnp.float32
BF16 = jnp.bfloat16
MESH = pl.DeviceIdType.MESH
ANY = pl.BlockSpec(memory_space=pl.ANY)

HEAD_DIM = 64
N_Q_HEADS = 16
N_KV_HEADS = 2
ATTN_WIDTH = N_Q_HEADS * HEAD_DIM
KV_WIDTH = N_KV_HEADS * HEAD_DIM
BLOCK = 128
ROPE_THETA = 10000.0
LN_EPS = 1e-5
ALPHA = 2.0 ** 0.25
NEG_INF = -1e30
ADAM_LR, ADAM_B1, ADAM_B2, ADAM_EPS, ADAM_WD, ADAM_STEP = 0.001, 0.9, 0.999, 1e-08, 0.01, 10
N_CHIPS = 4
LANES = 128
SLAB = 128


def _cp(sem, vmem_mb):
    return pltpu.CompilerParams(dimension_semantics=sem, vmem_limit_bytes=vmem_mb << 20)


def _matmul(a, b, *, mode, m, n, k, tm, tn, tk, out_dtype, name, vmem_mb, a_spec=None, b_spec=None,
            res=None, alpha=1.0, after=None):
    nj, ni, nk = n // tn, m // tm, k // tk
    assert nj * tn == n and ni * tm == m and nk * tk == k, (name, m, n, k, tm, tn, tk)
    if mode == "nn":
        dims = ((1,), (0,))
        a_spec = a_spec or pl.BlockSpec((tm, tk), lambda j, i, kk: (i, kk))
        b_spec = b_spec or pl.BlockSpec((tk, tn), lambda j, i, kk: (kk, j))
    elif mode == "nt":
        dims = ((1,), (1,))
        a_spec = a_spec or pl.BlockSpec((tm, tk), lambda j, i, kk: (i, kk))
        b_spec = b_spec or pl.BlockSpec((tn, tk), lambda j, i, kk: (j, kk))
    else:
        dims = ((0,), (0,))
        a_spec = a_spec or pl.BlockSpec((tk, tm), lambda j, i, kk: (kk, i))
        b_spec = b_spec or pl.BlockSpec((tk, tn), lambda j, i, kk: (kk, j))
    has_res = res is not None
    has_after = after is not None

    def body(*refs):
        refs = refs[1:] if has_after else refs
        a_ref, b_ref = refs[0], refs[1]
        res_ref = refs[2] if has_res else None
        o_ref = refs[2 + has_res]
        part = lax.dot_general(a_ref[...], b_ref[...], (dims, ((), ())), preferred_element_type=F32)

        def finish(acc):
            if has_res:
                acc = acc + alpha * res_ref[...].astype(F32)
            o_ref[...] = acc.astype(o_ref.dtype)

        if nk == 1:
            finish(part)
        else:
            acc_ref = refs[3 + has_res]
            kk = pl.program_id(2)

            @pl.when(kk == 0)
            def _():
                acc_ref[...] = part

            @pl.when(kk > 0)
            def _():
                acc_ref[...] += part

            @pl.when(kk == nk - 1)
            def _():
                finish(acc_ref[...])

    in_specs = [a_spec, b_spec]
    args = [a, b]
    if has_res:
        in_specs.append(pl.BlockSpec((tm, tn), lambda j, i, kk: (i, j)))
        args.append(res)
    if has_after:
        in_specs.insert(0, pl.BlockSpec(after.shape, lambda j, i, kk: (0, 0)))
        args.insert(0, after)
    return pl.pallas_call(
        body, name=name, grid=(nj, ni, nk), in_specs=in_specs,
        out_specs=pl.BlockSpec((tm, tn), lambda j, i, kk: (i, j)),
        out_shape=jax.ShapeDtypeStruct((m, n), out_dtype),
        scratch_shapes=[pltpu.VMEM((tm, tn), F32)] if nk > 1 else [],
        compiler_params=_cp(("arbitrary", "arbitrary", "arbitrary"), vmem_mb),
    )(*args)


def _pick(total, want, mult):
    if total <= want:
        return total
    for t in range(want, 0, -1):
        if total % t == 0 and t % mult == 0:
            return t
    return total


def _rope_tables(s):
    half = HEAD_DIM // 2
    inv_freq = ROPE_THETA ** (-jnp.arange(half, dtype=F32) / half)
    ang = jnp.arange(s, dtype=F32)[:, None] * inv_freq[None, :]
    cos = jnp.tile(jnp.cos(ang), (1, LANES // half))
    sin = jnp.tile(jnp.concatenate([-jnp.sin(ang), jnp.sin(ang)], axis=1), (1, LANES // HEAD_DIM))
    return cos, sin


def _rope(x, cos, sin, lo):
    partner = jnp.where(lo, pltpu.roll(x, LANES - HEAD_DIM // 2, 1), pltpu.roll(x, HEAD_DIM // 2, 1))
    return x * cos + partner * sin


def _dot(a, b, dims):
    return lax.dot_general(a, b, (dims, ((), ())), preferred_element_type=F32)


NN, NT, TN = ((1,), (0,)), ((1,), (1,)), ((0,), (0,))


def _kv_variants(t, head_lo):
    r = pltpu.roll(t, HEAD_DIM, 1)
    zero = jnp.zeros_like(t)
    a = (jnp.where(head_lo, t, zero).astype(BF16), jnp.where(head_lo, r, zero).astype(BF16))
    b = (jnp.where(head_lo, zero, r).astype(BF16), jnp.where(head_lo, zero, t).astype(BF16))
    return a, b


PAIRS_PER_KV = N_Q_HEADS // 2 // N_KV_HEADS
STACK = PAIRS_PER_KV * BLOCK


def _stack_pairs(ref, j, fn):
    return jnp.concatenate([fn(ref[:, p * LANES:(p + 1) * LANES])
                            for p in range(j * PAIRS_PER_KV, (j + 1) * PAIRS_PER_KV)], axis=0)


def _stack_sinks(sink_ref, j, hh):
    return jnp.concatenate([jnp.full((BLOCK, 1), sink_ref[0, 2 * p + hh], F32)
                            for p in range(j * PAIRS_PER_KV, (j + 1) * PAIRS_PER_KV)], axis=0)


def _attn_probs(qp, ka, kb, valid, sink_a, sink_b):
    out = []
    for kk, sink in ((ka, sink_a), (kb, sink_b)):
        s = jnp.where(valid, _dot(qp, kk, NT), NEG_INF)
        mx = jnp.maximum(jnp.max(s, axis=1, keepdims=True), sink)
        e = jnp.exp(s - mx)
        es = jnp.exp(sink - mx)
        inv = 1.0 / (jnp.sum(e, axis=1, keepdims=True) + es)
        out.append((e * inv, es * inv))
    return out


def _attn_common(i, q_ref, k_ref, v_ref, kp_ref, vp_ref, cos_ref, sin_ref, cosp_ref, sinp_ref):
    lane = lax.broadcasted_iota(jnp.int32, (1, LANES), 1)
    lo = (lane % HEAD_DIM) < (HEAD_DIM // 2)
    head_lo = lane < HEAD_DIM
    cos, sin = cos_ref[...], sin_ref[...]
    kc = _rope(k_ref[...].astype(F32), cos, sin, lo)
    kp = _rope(kp_ref[...].astype(F32), cosp_ref[...], sinp_ref[...], lo)
    kext = jnp.concatenate([kp, kc], axis=0)
    vext = jnp.concatenate([vp_ref[...].astype(F32), v_ref[...].astype(F32)], axis=0)
    ka, kb = _kv_variants(kext, head_lo)
    va, vb = _kv_variants(vext, head_lo)
    qi = lax.broadcasted_iota(jnp.int32, (STACK, 1), 0) % BLOCK
    kj = lax.broadcasted_iota(jnp.int32, (1, 2 * BLOCK), 1)
    valid = (kj > qi) & (kj <= qi + BLOCK) & ((kj >= BLOCK) | (i > 0))
    cos4 = jnp.concatenate([cos] * PAIRS_PER_KV, axis=0)
    sin4 = jnp.concatenate([sin] * PAIRS_PER_KV, axis=0)
    return lo, head_lo, cos, sin, cos4, sin4, ka, kb, va, vb, valid


def _attn_fwd(proj, sinks, cos, sin, s):
    nb = s // BLOCK
    kcol, vcol = ATTN_WIDTH // LANES, ATTN_WIDTH // LANES + 1

    def body(q_ref, k_ref, v_ref, kp_ref, vp_ref, cos_ref, sin_ref, cosp_ref, sinp_ref, sink_ref, o_ref):
        i = pl.program_id(0)
        lo, head_lo, cs, sn, cs4, sn4, ka, kb, va, vb, valid = _attn_common(
            i, q_ref, k_ref, v_ref, kp_ref, vp_ref, cos_ref, sin_ref, cosp_ref, sinp_ref)
        for j in range(N_KV_HEADS):
            q4 = _stack_pairs(q_ref, j, lambda t: t.astype(F32))
            qp = (_rope(q4, cs4, sn4, lo) * HEAD_DIM ** -0.5).astype(BF16)
            (pa, _), (pb, _) = _attn_probs(qp, ka[j], kb[j], valid, _stack_sinks(sink_ref, j, 0),
                                           _stack_sinks(sink_ref, j, 1))
            o = (_dot(pa.astype(BF16), va[j], NN) + _dot(pb.astype(BF16), vb[j], NN)).astype(BF16)
            for t in range(PAIRS_PER_KV):
                p = j * PAIRS_PER_KV + t
                o_ref[:, p * LANES:(p + 1) * LANES] = o[t * BLOCK:(t + 1) * BLOCK]

    prev = lambda i: (jnp.maximum(i - 1, 0), 0)
    return pl.pallas_call(
        body, name="attn_fwd", grid=(nb,),
        in_specs=[pl.BlockSpec((BLOCK, ATTN_WIDTH), lambda i: (i, 0)),
                  pl.BlockSpec((BLOCK, LANES), lambda i: (i, kcol)),
                  pl.BlockSpec((BLOCK, LANES), lambda i: (i, vcol)),
                  pl.BlockSpec((BLOCK, LANES), lambda i: (jnp.maximum(i - 1, 0), kcol)),
                  pl.BlockSpec((BLOCK, LANES), lambda i: (jnp.maximum(i - 1, 0), vcol)),
                  pl.BlockSpec((BLOCK, LANES), lambda i: (i, 0)),
                  pl.BlockSpec((BLOCK, LANES), lambda i: (i, 0)),
                  pl.BlockSpec((BLOCK, LANES), prev),
                  pl.BlockSpec((BLOCK, LANES), prev),
                  pl.BlockSpec(memory_space=pltpu.SMEM)],
        out_specs=pl.BlockSpec((BLOCK, ATTN_WIDTH), lambda i: (i, 0)),
        out_shape=jax.ShapeDtypeStruct((s, ATTN_WIDTH), BF16),
        compiler_params=_cp(("arbitrary",), 32),
    )(proj, proj, proj, proj, proj, cos, sin, cos, sin, sinks)


def _attn_bwd(proj, dmix, sinks, cos, sin, s):
    nb = s // BLOCK
    kcol, vcol = ATTN_WIDTH // LANES, ATTN_WIDTH // LANES + 1
    pairs_per_kv = N_Q_HEADS // 2 // N_KV_HEADS

    def body(q_ref, k_ref, v_ref, kp_ref, vp_ref, cos_ref, sin_ref, cosp_ref, sinp_ref, sink_ref, do_ref,
             dq_ref, dk_ref, dv_ref, dsink_ref, ck_ref, cv_ref):
        g = pl.program_id(0)
        i = nb - 1 - g

        @pl.when(g == 0)
        def _():
            ck_ref[...] = jnp.zeros_like(ck_ref)
            cv_ref[...] = jnp.zeros_like(cv_ref)
            dsink_ref[...] = jnp.zeros_like(dsink_ref)

        lo, head_lo, cs, sn, cs4, sn4, ka, kb, va, vb, valid = _attn_common(
            i, q_ref, k_ref, v_ref, kp_ref, vp_ref, cos_ref, sin_ref, cosp_ref, sinp_ref)
        lane = lax.broadcasted_iota(jnp.int32, (1, LANES), 1)
        dk_j, dv_j = [], []
        dsink = jnp.zeros((1, LANES), F32)
        for j in range(N_KV_HEADS):
            q4 = _stack_pairs(q_ref, j, lambda t: t.astype(F32))
            qp = (_rope(q4, cs4, sn4, lo) * HEAD_DIM ** -0.5).astype(BF16)
            probs = _attn_probs(qp, ka[j], kb[j], valid, _stack_sinks(sink_ref, j, 0), _stack_sinks(sink_ref, j, 1))
            do = _stack_pairs(do_ref, j, lambda t: t)
            dq_r = jnp.zeros((STACK, LANES), F32)
            dkc, dvc = [], []
            for hh, ((pr, ps), kk, vv) in enumerate(zip(probs, (ka[j], kb[j]), (va[j], vb[j]))):
                dp = _dot(do, vv, NT)
                delta = jnp.sum(pr * dp, axis=1, keepdims=True)
                ds = (pr * (dp - delta)).astype(BF16)
                psd = ps * delta
                for t in range(PAIRS_PER_KV):
                    head = 2 * (j * PAIRS_PER_KV + t) + hh
                    dsink = dsink + jnp.where(
                        lane == head, -jnp.sum(psd[t * BLOCK:(t + 1) * BLOCK], axis=0, keepdims=True), 0.0)
                dq_r = dq_r + _dot(ds, kk, NN)
                dkc.append(_dot(ds, qp, TN))
                dvc.append(_dot(pr.astype(BF16), do, TN))
            dk_j.append(jnp.where(head_lo, dkc[0], dkc[1]))
            dv_j.append(jnp.where(head_lo, dvc[0], dvc[1]))
            dq = _rope(dq_r * HEAD_DIM ** -0.5, cs4, -sn4, lo).astype(BF16)
            for t in range(PAIRS_PER_KV):
                p = j * PAIRS_PER_KV + t
                dq_ref[:, p * LANES:(p + 1) * LANES] = dq[t * BLOCK:(t + 1) * BLOCK]
        tot_k = [t + pltpu.roll(t, HEAD_DIM, 1) for t in dk_j]
        tot_v = [t + pltpu.roll(t, HEAD_DIM, 1) for t in dv_j]
        dkext = jnp.where(head_lo, tot_k[0], tot_k[1])
        dvext = jnp.where(head_lo, tot_v[0], tot_v[1])
        dk_r = dkext[BLOCK:] + ck_ref[...]
        dk_ref[...] = _rope(dk_r, cs, -sn, lo).astype(BF16)
        dv_ref[...] = (dvext[BLOCK:] + cv_ref[...]).astype(BF16)
        ck_ref[...] = dkext[:BLOCK]
        cv_ref[...] = dvext[:BLOCK]
        dsink_ref[0:1, :] += dsink

    cur = lambda col: (lambda g: (nb - 1 - g, col))
    prv = lambda col: (lambda g: (jnp.maximum(nb - 2 - g, 0), col))
    blk = lambda w, f: pl.BlockSpec((BLOCK, w), f)
    return pl.pallas_call(
        body, name="attn_bwd", grid=(nb,),
        in_specs=[blk(ATTN_WIDTH, cur(0)), blk(LANES, cur(kcol)), blk(LANES, cur(vcol)),
                  blk(LANES, prv(kcol)), blk(LANES, prv(vcol)),
                  blk(LANES, cur(0)), blk(LANES, cur(0)), blk(LANES, prv(0)), blk(LANES, prv(0)),
                  pl.BlockSpec(memory_space=pltpu.SMEM),
                  blk(ATTN_WIDTH, cur(0))],
        out_specs=[blk(ATTN_WIDTH, cur(0)), blk(LANES, cur(0)), blk(LANES, cur(0)),
                   pl.BlockSpec((8, LANES), lambda g: (0, 0))],
        out_shape=[jax.ShapeDtypeStruct((s, ATTN_WIDTH), BF16), jax.ShapeDtypeStruct((s, LANES), BF16),
                   jax.ShapeDtypeStruct((s, LANES), BF16), jax.ShapeDtypeStruct((8, LANES), F32)],
        scratch_shapes=[pltpu.VMEM((BLOCK, LANES), F32), pltpu.VMEM((BLOCK, LANES), F32)],
        compiler_params=_cp(("arbitrary",), 32),
    )(proj, proj, proj, proj, proj, cos, sin, cos, sin, sinks, dmix)


def _causal_conv(x, prev8, w):
    row = lax.broadcasted_iota(jnp.int32, (8, 1), 0)
    r1, r2 = pltpu.roll(x, 1, 0), pltpu.roll(x, 2, 0)
    s1 = jnp.concatenate([jnp.where(row == 0, prev8[7:8], r1[:8]), r1[8:]], axis=0)
    s2 = jnp.concatenate([jnp.where(row == 0, prev8[6:7], jnp.where(row == 1, prev8[7:8], r2[:8])), r2[8:]], axis=0)
    return w[0:1] * s2 + w[1:2] * s1 + w[2:3] * x, s1, s2


def _conv_bwd(dy, x, w, next8):
    r = x.shape[0]
    row = lax.broadcasted_iota(jnp.int32, (8, 1), 0)
    r1, r2 = pltpu.roll(dy, r - 1, 0), pltpu.roll(dy, r - 2, 0)
    n1 = jnp.concatenate([r1[:r - 8], jnp.where(row == 7, next8[0:1], r1[r - 8:])], axis=0)
    n2 = jnp.concatenate([r2[:r - 8], jnp.where(row == 6, next8[0:1], jnp.where(row == 7, next8[1:2], r2[r - 8:]))],
                         axis=0)
    dx = w[2:3] * dy + w[1:2] * n1 + w[0:1] * n2
    dws = [jnp.sum(t * x, axis=0, keepdims=True) for t in (n2, n1, dy)]
    return dx, dws


CONV_COLS = 256


def _convmix_cols(d):
    conv_w = d - ATTN_WIDTH
    base = (ATTN_WIDTH + 2 * KV_WIDTH) // CONV_COLS
    step = conv_w // CONV_COLS
    return base, base + step, base + 2 * step, step


def _convmix_fwd(proj, scw8, s, d):
    gb0, gc0, h0, ncb = _convmix_cols(d)
    tr = _pick(s, 1024, 16)
    ni = s // tr

    def body(gb_ref, gc_ref, h_ref, w_ref, o_ref, carry_ref):
        @pl.when(pl.program_id(1) == 0)
        def _():
            carry_ref[...] = jnp.zeros_like(carry_ref)

        gch = gc_ref[...].astype(F32) * h_ref[...].astype(F32)
        cc, _, _ = _causal_conv(gch, carry_ref[...], w_ref[...])
        o_ref[...] = (gb_ref[...].astype(F32) * cc).astype(BF16)
        carry_ref[...] = gch[tr - 8:]

    spec = lambda c0: pl.BlockSpec((tr, CONV_COLS), lambda j, i: (i, c0 + j))
    return pl.pallas_call(
        body, name="convmix_fwd", grid=(ncb, ni),
        in_specs=[spec(gb0), spec(gc0), spec(h0), pl.BlockSpec((8, CONV_COLS), lambda j, i: (0, j))],
        out_specs=pl.BlockSpec((tr, CONV_COLS), lambda j, i: (i, j)),
        out_shape=jax.ShapeDtypeStruct((s, d - ATTN_WIDTH), BF16),
        scratch_shapes=[pltpu.VMEM((8, CONV_COLS), F32)],
        compiler_params=_cp(("arbitrary", "arbitrary"), 32),
    )(proj, proj, proj, scw8)


def _convmix_bwd(proj, dmix, scw8, s, d):
    gb0, gc0, h0, ncb = _convmix_cols(d)
    tr = _pick(s, 1024, 16)
    ni = s // tr
    dc0 = ATTN_WIDTH // CONV_COLS

    def body(dc_ref, gb_ref, gc_ref, h_ref, gcp_ref, hp_ref, w_ref, d3_ref, dw_ref, nxt_ref):
        g = pl.program_id(1)
        i = ni - 1 - g

        @pl.when(g == 0)
        def _():
            nxt_ref[...] = jnp.zeros_like(nxt_ref)
            dw_ref[...] = jnp.zeros_like(dw_ref)

        w = w_ref[...]
        gb, gc, h = gb_ref[...].astype(F32), gc_ref[...].astype(F32), h_ref[...].astype(F32)
        gch = gc * h
        prev8 = (gcp_ref[...].astype(F32) * hp_ref[...].astype(F32))[8:16] * (i > 0).astype(F32)
        cc, s1, s2 = _causal_conv(gch, prev8, w)
        dc = dc_ref[...].astype(F32)
        dcc = dc * gb
        dgch, dws = _conv_bwd(dcc, gch, w, nxt_ref[...])
        d3_ref[0] = (dc * cc).astype(BF16)
        d3_ref[1] = (dgch * h).astype(BF16)
        d3_ref[2] = (dgch * gc).astype(BF16)
        for t in range(3):
            dw_ref[t:t + 1, :] += dws[t]
        nxt_ref[...] = dcc[0:8]

    cur = lambda c0: pl.BlockSpec((tr, CONV_COLS), lambda j, g: (ni - 1 - g, c0 + j))
    prv = lambda c0: pl.BlockSpec((16, CONV_COLS), lambda j, g: (jnp.maximum((ni - 1 - g) * (tr // 16) - 1, 0), c0 + j))
    return pl.pallas_call(
        body, name="convmix_bwd", grid=(ncb, ni),
        in_specs=[cur(dc0), cur(gb0), cur(gc0), cur(h0), prv(gc0), prv(h0),
                  pl.BlockSpec((8, CONV_COLS), lambda j, g: (0, j))],
        out_specs=[pl.BlockSpec((3, tr, CONV_COLS), lambda j, g: (0, ni - 1 - g, j)),
                   pl.BlockSpec((8, CONV_COLS), lambda j, g: (0, j))],
        out_shape=[jax.ShapeDtypeStruct((3, s, d - ATTN_WIDTH), BF16), jax.ShapeDtypeStruct((8, d - ATTN_WIDTH), F32)],
        scratch_shapes=[pltpu.VMEM((8, CONV_COLS), F32)],
        compiler_params=_cp(("arbitrary", "arbitrary"), 32),
    )(dmix, proj, proj, proj, proj, proj, scw8)


def _ln_fwd(z):
    mu = jnp.mean(z, axis=-1, keepdims=True)
    zc = z - mu
    var = jnp.mean(zc * zc, axis=-1, keepdims=True)
    rstd = lax.rsqrt(var + LN_EPS)
    return zc * rstd, rstd


def _ln_bwd(dout, xh, rstd, g):
    dxh = dout * g
    c1 = jnp.mean(dxh, axis=-1, keepdims=True)
    c2 = jnp.mean(dxh * xh, axis=-1, keepdims=True)
    dz = rstd * (dxh - c1 - xh * c2)
    return dz, jnp.sum(dout * xh, axis=0, keepdims=True), jnp.sum(dout, axis=0, keepdims=True)


def _outproj_ln1(attn, conv, wout, x, g1, b1, s, d):
    tm = _pick(s, 256, 16)
    ka = attn.shape[1]

    def body(a_ref, c_ref, wt_ref, wb_ref, x_ref, g_ref, b_ref, x1_ref, x1b_ref, xh_ref, rs_ref):
        y = _dot(a_ref[...], wt_ref[...], NN) + _dot(c_ref[...], wb_ref[...], NN)
        xh, rstd = _ln_fwd(ALPHA * x_ref[...] + y)
        x1 = xh * g_ref[...] + b_ref[...]
        x1_ref[...] = x1
        x1b_ref[...] = x1.astype(BF16)
        xh_ref[...] = xh.astype(BF16)
        rs_ref[...] = rstd

    row = lambda w: pl.BlockSpec((tm, w), lambda i: (i, 0))
    vec = pl.BlockSpec((1, d), lambda i: (0, 0))
    return pl.pallas_call(
        body, name="outproj_ln1", grid=(s // tm,),
        in_specs=[row(ka), row(d - ka), pl.BlockSpec((ka, d), lambda i: (0, 0)),
                  pl.BlockSpec((d - ka, d), lambda i: (ka // (d - ka), 0)), row(d), vec, vec],
        out_specs=[row(d), row(d), row(d), row(1)],
        out_shape=[jax.ShapeDtypeStruct((s, d), F32), jax.ShapeDtypeStruct((s, d), BF16),
                   jax.ShapeDtypeStruct((s, d), BF16), jax.ShapeDtypeStruct((s, 1), F32)],
        compiler_params=_cp(("arbitrary",), 48),
    )(attn, conv, wout, wout, x, g1, b1)


def _ffn_up(x1b, wup, fcw8, s, d, dff):
    tm = _pick(s, 1024, 16)
    tn = _pick(dff, 512, LANES)
    nj, ni = dff // tn, s // tm

    def body(x_ref, wa_ref, wg_ref, ca_ref, cg_ref, u_ref, y_ref, h_ref, carry_ref):
        @pl.when(pl.program_id(1) == 0)
        def _():
            carry_ref[...] = jnp.zeros_like(carry_ref)

        xa = x_ref[...]
        ys = []
        for part, (w_ref, c_ref) in enumerate(((wa_ref, ca_ref), (wg_ref, cg_ref))):
            ub = _dot(xa, w_ref[...], NN).astype(BF16)
            u_ref[part] = ub
            u = ub.astype(F32)
            y, _, _ = _causal_conv(u, carry_ref[part], c_ref[...])
            carry_ref[part] = u[tm - 8:]
            yb = y.astype(BF16)
            y_ref[part] = yb
            ys.append(yb.astype(F32))
        a2, g2 = ys
        sig = 1.0 / (1.0 + jnp.exp(-a2))
        h_ref[...] = (a2 * sig * g2).astype(BF16)

    return pl.pallas_call(
        body, name="ffn_up", grid=(nj, ni),
        in_specs=[pl.BlockSpec((tm, d), lambda j, i: (i, 0)),
                  pl.BlockSpec((d, tn), lambda j, i: (0, j)),
                  pl.BlockSpec((d, tn), lambda j, i: (0, j + nj)),
                  pl.BlockSpec((8, tn), lambda j, i: (0, j)),
                  pl.BlockSpec((8, tn), lambda j, i: (0, j + nj))],
        out_specs=[pl.BlockSpec((2, tm, tn), lambda j, i: (0, i, j)),
                   pl.BlockSpec((2, tm, tn), lambda j, i: (0, i, j)),
                   pl.BlockSpec((tm, tn), lambda j, i: (i, j))],
        out_shape=[jax.ShapeDtypeStruct((2, s, dff), BF16), jax.ShapeDtypeStruct((2, s, dff), BF16),
                   jax.ShapeDtypeStruct((s, dff), BF16)],
        scratch_shapes=[pltpu.VMEM((2, 8, tn), F32)],
        compiler_params=_cp(("arbitrary", "arbitrary"), 56),
    )(x1b, wup, wup, fcw8, fcw8)


def _ffn_mid_bwd(dz2b, wdown, u3, y3, fcw8, s, d, dff):
    tm = _pick(s, 1024, 16)
    tn = _pick(dff, 512, LANES)
    nj, ni = dff // tn, s // tm

    def body(dz_ref, wd_ref, u_ref, y_ref, ca_ref, cg_ref, du_ref, dw_ref, nxt_ref):
        @pl.when(pl.program_id(1) == 0)
        def _():
            nxt_ref[...] = jnp.zeros_like(nxt_ref)
            dw_ref[...] = jnp.zeros_like(dw_ref)

        a2, g2 = y_ref[0].astype(F32), y_ref[1].astype(F32)
        sig = 1.0 / (1.0 + jnp.exp(-a2))
        silu = a2 * sig
        dhv = _dot(dz_ref[...], wd_ref[...], NT)
        dys = (dhv * g2 * (sig * (1.0 + a2 * (1.0 - sig))), dhv * silu)
        for part, (c_ref, dy) in enumerate(zip((ca_ref, cg_ref), dys)):
            dx, dws = _conv_bwd(dy, u_ref[part].astype(F32), c_ref[...], nxt_ref[part])
            du_ref[part] = dx.astype(BF16)
            for t in range(3):
                dw_ref[part, t:t + 1, :] += dws[t]
            nxt_ref[part] = dy[0:8]

    return pl.pallas_call(
        body, name="ffn_mid_bwd", grid=(nj, ni),
        in_specs=[pl.BlockSpec((tm, d), lambda j, g: (ni - 1 - g, 0)),
                  pl.BlockSpec((tn, d), lambda j, g: (j, 0)),
                  pl.BlockSpec((2, tm, tn), lambda j, g: (0, ni - 1 - g, j)),
                  pl.BlockSpec((2, tm, tn), lambda j, g: (0, ni - 1 - g, j)),
                  pl.BlockSpec((8, tn), lambda j, g: (0, j)),
                  pl.BlockSpec((8, tn), lambda j, g: (0, j + nj))],
        out_specs=[pl.BlockSpec((2, tm, tn), lambda j, g: (0, ni - 1 - g, j)),
                   pl.BlockSpec((2, 8, tn), lambda j, g: (0, 0, j))],
        out_shape=[jax.ShapeDtypeStruct((2, s, dff), BF16), jax.ShapeDtypeStruct((2, 8, dff), F32)],
        scratch_shapes=[pltpu.VMEM((2, 8, tn), F32)],
        compiler_params=_cp(("arbitrary", "arbitrary"), 56),
    )(dz2b, wdown, u3, y3, fcw8, fcw8)


def _ffn_down_loss(hmid, wdown, x1, target, g2, b2, s, d, dff):
    tm = _pick(s, 512, SLAB)
    tk = _pick(dff, 1408, LANES)
    ni, nk = s // tm, dff // tk
    slab = min(SLAB, tm)

    def body(h_ref, w_ref, x1_ref, t_ref, g_ref, b_ref, dzb_ref, st_ref, acc_ref):
        i, kk = pl.program_id(0), pl.program_id(1)

        @pl.when((i == 0) & (kk == 0))
        def _():
            st_ref[...] = jnp.zeros_like(st_ref)

        part = _dot(h_ref[...], w_ref[...], NN)

        @pl.when(kk == 0)
        def _():
            acc_ref[...] = part

        @pl.when(kk > 0)
        def _():
            acc_ref[...] += part

        @pl.when(kk == nk - 1)
        def _():
            g, b = g_ref[...], b_ref[...]

            def one(sl, carry):
                rows = pl.ds(pl.multiple_of(sl * slab, slab), slab)
                xh, rstd = _ln_fwd(ALPHA * x1_ref[rows, :] + acc_ref[rows, :])
                diff = xh * g + b - t_ref[rows, :]
                sq = jnp.sum(jnp.sum(diff * diff, axis=1, keepdims=True), axis=0, keepdims=True)
                dz, dg, db = _ln_bwd(diff * (1.0 / d), xh, rstd, g)
                dzb_ref[rows, :] = dz.astype(BF16)
                st_ref[0:1, :] += dg
                st_ref[1:2, :] += db
                st_ref[2:3, :] += sq
                return carry

            lax.fori_loop(0, tm // slab, one, 0)

    row = pl.BlockSpec((tm, d), lambda i, kk: (i, 0))
    vec = pl.BlockSpec((1, d), lambda i, kk: (0, 0))
    return pl.pallas_call(
        body, name="ffn_down_loss", grid=(ni, nk),
        in_specs=[pl.BlockSpec((tm, tk), lambda i, kk: (i, kk)), pl.BlockSpec((tk, d), lambda i, kk: (kk, 0)),
                  row, row, vec, vec],
        out_specs=[row, pl.BlockSpec((8, d), lambda i, kk: (0, 0))],
        out_shape=[jax.ShapeDtypeStruct((s, d), BF16), jax.ShapeDtypeStruct((8, d), F32)],
        scratch_shapes=[pltpu.VMEM((tm, d), F32)],
        compiler_params=_cp(("arbitrary", "arbitrary"), 48),
    )(hmid, wdown, x1, target, g2, b2)


def _ffn_dx_ln1_bwd(du3, wup, dz2b, xh1, rstd1, g1, s, d, dff):
    tm = _pick(s, 1024, SLAB)
    tk = _pick(dff, 1408, LANES)
    nkh = dff // tk
    ni, nk = s // tm, 2 * nkh
    slab = min(SLAB, tm)

    def body(a_ref, w_ref, dz2_ref, xh_ref, rs_ref, g_ref, dzb_ref, st_ref, acc_ref):
        i, kk = pl.program_id(0), pl.program_id(1)

        @pl.when((i == 0) & (kk == 0))
        def _():
            st_ref[...] = jnp.zeros_like(st_ref)

        part = _dot(a_ref[...], w_ref[...], NT)

        @pl.when(kk == 0)
        def _():
            acc_ref[...] = part

        @pl.when(kk > 0)
        def _():
            acc_ref[...] += part

        @pl.when(kk == nk - 1)
        def _():
            g = g_ref[...]

            def one(sl, carry):
                rows = pl.ds(pl.multiple_of(sl * slab, slab), slab)
                dx1 = ALPHA * dz2_ref[rows, :].astype(F32) + acc_ref[rows, :]
                dz, dg, db = _ln_bwd(dx1, xh_ref[rows, :].astype(F32), rs_ref[rows, :], g)
                dzb_ref[rows, :] = dz.astype(BF16)
                st_ref[0:1, :] += dg
                st_ref[1:2, :] += db
                return carry

            lax.fori_loop(0, tm // slab, one, 0)

    row = pl.BlockSpec((tm, d), lambda i, kk: (i, 0))
    row1 = pl.BlockSpec((tm, d), lambda i, kk: (i, 0), pipeline_mode=pl.Buffered(1))
    return pl.pallas_call(
        body, name="ffn_dx_ln1_bwd", grid=(ni, nk),
        in_specs=[pl.BlockSpec((None, tm, tk), lambda i, kk: (kk // nkh, i, kk % nkh)),
                  pl.BlockSpec((d, tk), lambda i, kk: (0, kk)),
                  row1, row1, pl.BlockSpec((tm, 1), lambda i, kk: (i, 0)), pl.BlockSpec((1, d), lambda i, kk: (0, 0))],
        out_specs=[row, pl.BlockSpec((8, d), lambda i, kk: (0, 0))],
        out_shape=[jax.ShapeDtypeStruct((s, d), BF16), jax.ShapeDtypeStruct((8, d), F32)],
        scratch_shapes=[pltpu.VMEM((tm, d), F32)],
        compiler_params=_cp(("arbitrary", "arbitrary"), 56),
    )(du3, wup, dz2b, xh1, rstd1, g1)


def _phase_mixer(x, x_in, win_t, wout, scw8, sinks, ln1_g, ln1_b):
    s, d = x.shape
    n_in = win_t.shape[0]
    xb = x_in.astype(BF16)
    cos, sin = _rope_tables(s)
    proj = _matmul(xb, win_t, mode="nt", m=s, n=n_in, k=d, tm=_pick(s, 512, 16), tn=_pick(n_in, 2176, LANES), tk=d,
                   out_dtype=BF16, name="in_proj", vmem_mb=48)
    attn = _attn_fwd(proj, sinks, cos, sin, s)
    conv = _convmix_fwd(proj, scw8, s, d)
    x1, x1b, xh1, rstd1 = _outproj_ln1(attn, conv, wout, x, ln1_g, ln1_b, s, d)
    return dict(xb=xb, cos=cos, sin=sin, proj=proj, attn=attn, conv=conv, x1=x1, x1b=x1b, xh1=xh1, rstd1=rstd1)


def _phase_ffn(a, target, wup, wdown, fcw8, ln2_g, ln2_b):
    x1, x1b = a["x1"], a["x1b"]
    s, d = x1.shape
    dff = wdown.shape[0]
    u3, y3, hmid = _ffn_up(x1b, wup, fcw8, s, d, dff)
    dz2b, st2 = _ffn_down_loss(hmid, wdown, x1, target, ln2_g, ln2_b, s, d, dff)

    ts = _pick(s, 2048, 16)
    g_wdown = _matmul(hmid, dz2b, mode="tn", m=dff, n=d, k=s, tm=_pick(dff, 1408, LANES), tn=_pick(d, 1024, LANES),
                      tk=ts, out_dtype=BF16, name="grad_w_down", vmem_mb=48)
    du3, dfcw = _ffn_mid_bwd(dz2b, wdown, u3, y3, fcw8, s, d, dff)
    tnu = _pick(dff, 1408, LANES)
    njh = dff // tnu
    g_wup = _matmul(x1b, du3, mode="tn", m=d, n=2 * dff, k=s, tm=_pick(d, 1024, LANES), tn=tnu, tk=ts, out_dtype=BF16,
                    name="grad_w_up", vmem_mb=48,
                    b_spec=pl.BlockSpec((None, ts, tnu), lambda j, i, kk: (j // njh, kk, j % njh)))
    return dict(du3=du3, dz2b=dz2b, st2=st2, dfcw=dfcw, wdown=g_wdown, wup=g_wup)


def _phase_rest(a, f, wup, wout, win_t, scw8, sinks, ln1_g):
    xb, cos, sin, proj, attn, conv = a["xb"], a["cos"], a["sin"], a["proj"], a["attn"], a["conv"]
    du3, dz2b, st2, dfcw = f["du3"], f["dz2b"], f["st2"], f["dfcw"]
    s, d = a["x1"].shape
    dff = wup.shape[1] // 2
    n_in = win_t.shape[0]
    ts = _pick(s, 2048, 16)
    dz1b, st1 = _ffn_dx_ln1_bwd(du3, wup, dz2b, a["xh1"], a["rstd1"], ln1_g, s, d, dff)

    mix = jnp.concatenate([attn, conv], axis=1)
    g_wout = _matmul(mix, dz1b, mode="tn", m=d, n=d, k=s, tm=_pick(d, 1024, LANES), tn=_pick(d, 1024, LANES), tk=ts,
                     out_dtype=BF16, name="grad_w_out", vmem_mb=48)
    dmix = _matmul(dz1b, wout, mode="nt", m=s, n=d, k=d, tm=_pick(s, 1024, 16), tn=_pick(d, 1024, LANES), tk=d,
                   out_dtype=BF16, name="out_dmix", vmem_mb=48)
    d3, dscw = _convmix_bwd(proj, dmix, scw8, s, d)
    dq, dk, dv, dsink = _attn_bwd(proj, dmix, sinks, cos, sin, s)
    dproj = jnp.concatenate([dq, dk, dv, d3[0], d3[1], d3[2]], axis=1)
    g_win_t = _matmul(dproj, xb, mode="tn", m=n_in, n=d, k=s, tm=_pick(n_in, 2176, LANES), tn=_pick(d, 512, LANES),
                      tk=ts, out_dtype=BF16, name="grad_w_in", vmem_mb=48)
    small = dict(loss_sq=st2[2, 0], ln2_g=st2[0], ln2_b=st2[1], ln1_g=st1[0], ln1_b=st1[1], sinks=dsink[0, :N_Q_HEADS],
                 fcw=jnp.concatenate([dfcw[0, :3], dfcw[1, :3]], axis=1), scw=dscw[:3])
    return (dproj, dz1b), dict(win_t=g_win_t, wout=g_wout), small


def _grad_x(dproj, dz1b, win_t, after=None):
    s, n_in = dproj.shape
    d = win_t.shape[1]
    return _matmul(dproj, win_t, mode="nn", m=s, n=d, k=n_in, tm=_pick(s, 512, 16), tn=_pick(d, 1024, LANES),
                   tk=n_in, out_dtype=F32, name="grad_x", vmem_mb=56, res=dz1b, alpha=ALPHA, after=after)


def _local_step(x, target, win_t, wout, wup, wdown, scw8, fcw8, sinks, ln1_g, ln1_b, ln2_g, ln2_b):
    a = _phase_mixer(x, x, win_t, wout, scw8, sinks, ln1_g, ln1_b)
    f = _phase_ffn(a, target, wup, wdown, fcw8, ln2_g, ln2_b)
    (dproj, dz1b), g, small = _phase_rest(a, f, wup, wout, win_t, scw8, sinks, ln1_g)
    return _grad_x(dproj, dz1b, win_t), dict(g, wup=f["wup"], wdown=f["wdown"]), small


MIXER = ("win_t", "wout")
FFN = ("wup", "wdown")
BIG = MIXER + FFN


def _geom(shard_shapes):
    out = {}
    for name in BIG:
        r, c = shard_shapes[name]
        out[name] = ("col" if name == "wup" else "row", (r, c), (r // 2, c))
    return out


def _full_shape(kind, shard):
    r, c = shard
    return (N_CHIPS * r, c) if kind == "row" else (r, N_CHIPS * c)


def _piece_of(ref, kind, shard, chip, half):
    r, c = shard
    if kind == "row":
        return ref.at[pl.ds(chip * r + half * (r // 2), r // 2), :]
    return ref.at[pl.ds(half * (r // 2), r // 2), pl.ds(chip * c, c)]


def _shard_piece(ref, shard, half):
    r, _ = shard
    return ref.at[pl.ds(half * (r // 2), r // 2), :]


def _me():
    return lax.axis_index("x"), lax.axis_index("y"), lax.axis_index("c")


def _other_chips(x, y):
    return [(1 - x, y), (x, 1 - y), (1 - x, 1 - y)]


def _remote(src, dst, send_sem, recv_sem, dev):
    return pltpu.make_async_remote_copy(src_ref=src, dst_ref=dst, send_sem=send_sem, recv_sem=recv_sem,
                                        device_id=dev, device_id_type=MESH)


def _place_shard(w, chip1, kind, name):
    r, c = w.shape
    tr = _rows_tile(r, c, 16)
    nt = r // tr

    def body(chip_ref, w_ref, o_ref):
        o_ref[...] = w_ref[...].astype(BF16)

    out_map = (lambda i, chip_ref: (chip_ref[0] * nt + i, 0)) if kind == "row" else (lambda i, chip_ref: (i, chip_ref[0]))
    return pl.pallas_call(
        body, name="place_" + name,
        grid_spec=pltpu.PrefetchScalarGridSpec(
            num_scalar_prefetch=1, grid=(nt,),
            in_specs=[pl.BlockSpec((tr, c), lambda i, chip_ref: (i, 0))],
            out_specs=pl.BlockSpec((tr, c), out_map)),
        out_shape=jax.ShapeDtypeStruct(_full_shape(kind, (r, c)), BF16),
        compiler_params=_cp(("arbitrary",), 32),
    )(chip1, w)


def _allgather_weights(names, placed, geom, small_shards):
    nb, ns = len(names), len(small_shards)
    small_w = [a.shape[1] for a in small_shards]

    def body(*refs):
        sm = refs[nb:nb + ns]
        full = refs[nb + ns:2 * nb + ns]
        smf = refs[2 * nb + ns:2 * nb + 2 * ns]
        send, recv, loc = refs[2 * nb + 2 * ns:]
        x, y, c = _me()
        chip = 2 * x + y
        sib = (x, y, 1 - c)
        others = _other_chips(x, y)
        locals_, sends = [], []
        for m, name in enumerate(names):
            kind, shard, _ = geom[name]
            mine = _piece_of(full[m], kind, shard, chip, c)
            for k, (qx, qy) in enumerate(others):
                cp = _remote(mine, mine, send.at[6 * m + k], recv.at[6 * m + k], (qx, qy, c))
                cp.start()
                sends.append(cp)
        for t in range(ns):
            cp = pltpu.make_async_copy(sm[t], smf[t].at[:, pl.ds(chip * small_w[t], small_w[t])], loc.at[t])
            cp.start()
            locals_.append(cp)
            for k, (qx, qy) in enumerate(others):
                cp = _remote(sm[t], smf[t].at[:, pl.ds(chip * small_w[t], small_w[t])],
                             send.at[6 * nb + 3 * t + k], recv.at[6 * nb + 3 * t + k], (qx, qy, c))
                cp.start()
                sends.append(cp)
        for m, name in enumerate(names):
            kind, shard, _ = geom[name]
            for k, (qx, qy) in enumerate(others):
                got = _piece_of(full[m], kind, shard, 2 * qx + qy, c)
                _remote(got, got, send.at[6 * m + k], recv.at[6 * m + k], (qx, qy, c)).wait_recv()
                cp = _remote(got, got, send.at[6 * m + 3 + k], recv.at[6 * m + 3 + k], sib)
                cp.start()
                sends.append(cp)
        for t in range(ns):
            for k, (qx, qy) in enumerate(others):
                got = smf[t].at[:, pl.ds((2 * qx + qy) * small_w[t], small_w[t])]
                _remote(got, got, send.at[6 * nb + 3 * t + k], recv.at[6 * nb + 3 * t + k], (qx, qy, c)).wait_recv()
        for m, name in enumerate(names):
            kind, shard, _ = geom[name]
            for k, (qx, qy) in enumerate(others):
                got = _piece_of(full[m], kind, shard, 2 * qx + qy, 1 - c)
                _remote(got, got, send.at[6 * m + 3 + k], recv.at[6 * m + 3 + k], sib).wait_recv()
        for cp in sends:
            cp.wait_send()
        for cp in locals_:
            cp.wait()

    nsem = 6 * nb + 3 * ns
    out_shape = [jax.ShapeDtypeStruct(placed[n].shape, BF16) for n in names]
    out_shape += [jax.ShapeDtypeStruct((8, N_CHIPS * w), F32) for w in small_w]
    outs = pl.pallas_call(
        body, name="allgather_weights", in_specs=[ANY] * (nb + ns), out_specs=[ANY] * (nb + ns), out_shape=out_shape,
        input_output_aliases={m: m for m in range(nb)},
        scratch_shapes=[pltpu.SemaphoreType.DMA((nsem,)), pltpu.SemaphoreType.DMA((nsem,)),
                        pltpu.SemaphoreType.DMA((ns,))],
    )(*[placed[n] for n in names], *small_shards)
    return dict(zip(names, outs[:nb])), list(outs[nb:])


def _sibling_exchange(names, grads, geom):
    nb = len(names)

    def body(*refs):
        g = refs[:nb]
        got = refs[nb:2 * nb]
        send, recv = refs[2 * nb:]
        x, y, c = _me()
        sib = (x, y, 1 - c)
        cps = []
        for m, name in enumerate(names):
            kind, shard, _ = geom[name]
            for r in range(N_CHIPS):
                cp = _remote(_piece_of(g[m], kind, shard, r, 1 - c), got[m].at[r],
                             send.at[N_CHIPS * m + r], recv.at[N_CHIPS * m + r], sib)
                cp.start()
                cps.append(cp)
        for cp in cps:
            cp.wait_recv()
        for cp in cps:
            cp.wait_send()

    return pl.pallas_call(
        body, name="grad_sibling_exchange_" + names[0], in_specs=[ANY] * nb, out_specs=[ANY] * nb,
        out_shape=[jax.ShapeDtypeStruct((N_CHIPS,) + geom[n][2], BF16) for n in names],
        scratch_shapes=[pltpu.SemaphoreType.DMA((N_CHIPS * nb,)), pltpu.SemaphoreType.DMA((N_CHIPS * nb,))],
    )(*[grads[n] for n in names])


def _sibling_assemble(names, shards, geom):
    nb = len(names)

    def body(*refs):
        full = refs[nb:2 * nb]
        send, recv = refs[2 * nb:]
        x, y, c = _me()
        sib = (x, y, 1 - c)
        cps = []
        for m, name in enumerate(names):
            mine = _shard_piece(full[m], geom[name][1], c)
            cp = _remote(mine, mine, send.at[m], recv.at[m], sib)
            cp.start()
            cps.append(cp)
        for m, name in enumerate(names):
            theirs = _shard_piece(full[m], geom[name][1], 1 - c)
            _remote(theirs, theirs, send.at[m], recv.at[m], sib).wait_recv()
        for cp in cps:
            cp.wait_send()

    return pl.pallas_call(
        body, name="grad_sibling_assemble_" + names[0], in_specs=[ANY] * nb, out_specs=[ANY] * nb,
        out_shape=[jax.ShapeDtypeStruct(geom[n][1], F32) for n in names],
        input_output_aliases={m: m for m in range(nb)},
        scratch_shapes=[pltpu.SemaphoreType.DMA((nb,)), pltpu.SemaphoreType.DMA((nb,))],
    )(*shards)


HBM = pl.BlockSpec(memory_space=pltpu.HBM)
SEM = pl.BlockSpec(memory_space=pltpu.SEMAPHORE)
EFFECT = pltpu.SideEffectType.DATAFLOW_SIDE_EFFECTING
TOKEN = jax.ShapeDtypeStruct((8, LANES), F32)


def _hbm(a):
    return pltpu.with_memory_space_constraint(a, pltpu.HBM)


def _gather_copies(names, full, geom, send, recv):
    x, y, c = _me()
    out = []
    for m, name in enumerate(names):
        kind, shard, _ = geom[name]
        mine = _piece_of(full[m], kind, shard, 2 * x + y, c)
        for k, (qx, qy) in enumerate(_other_chips(x, y)):
            theirs = _piece_of(full[m], kind, shard, 2 * qx + qy, c)
            out.append((_remote(mine, mine, send.at[3 * m + k], recv.at[3 * m + k], (qx, qy, c)),
                        _remote(theirs, theirs, send.at[3 * m + k], recv.at[3 * m + k], (qx, qy, c))))
    return out


def _gather_start(names, placed, geom, after):
    nb = len(names)

    def body(*refs):
        full = refs[:nb]
        send, recv = refs[nb + 1], refs[nb + 2]
        token = refs[2 * nb + 3]
        for cp, _ in _gather_copies(names, full, geom, send, recv):
            cp.start()
        token[...] = jnp.zeros_like(token)

    outs = pl.pallas_call(
        body, name="gather_start_" + names[0],
        out_shape=(pltpu.SemaphoreType.DMA((3 * nb,)), pltpu.SemaphoreType.DMA((3 * nb,)),
                   *[pltpu.HBM(placed[n].shape, BF16) for n in names], TOKEN),
        in_specs=[HBM] * nb + [ANY], out_specs=(SEM, SEM, *[HBM] * nb, pl.BlockSpec(memory_space=pltpu.VMEM)),
        input_output_aliases={m: 2 + m for m in range(nb)},
        compiler_params=pltpu.CompilerParams(has_side_effects=EFFECT),
    )(*[_hbm(placed[n]) for n in names], after)
    return outs[0], outs[1], list(outs[2:2 + nb]), outs[2 + nb]


def _gather_wait(names, send, recv, thru, geom, after):
    nb = len(names)

    def body(*refs):
        full = refs[:nb]
        for mine, theirs in _gather_copies(names, full, geom, refs[nb], refs[nb + 1]):
            mine.wait_send()
            theirs.wait_recv()

    return pl.pallas_call(
        body, name="gather_wait_" + names[0], out_shape=tuple(pltpu.HBM(t.shape, t.dtype) for t in thru),
        in_specs=[HBM] * nb + [SEM, SEM, ANY], out_specs=tuple([HBM] * nb),
        input_output_aliases={m: m for m in range(nb)},
        compiler_params=pltpu.CompilerParams(has_side_effects=EFFECT),
    )(*thru, send, recv, after)


def _gather_forward(names, full, geom):
    nb = len(names)

    def body(*refs):
        arr = refs[nb:2 * nb]
        send, recv = refs[2 * nb:]
        x, y, c = _me()
        sib = (x, y, 1 - c)
        cps = []
        for m, name in enumerate(names):
            kind, shard, _ = geom[name]
            for k, (qx, qy) in enumerate(_other_chips(x, y)):
                got = _piece_of(arr[m], kind, shard, 2 * qx + qy, c)
                cp = _remote(got, got, send.at[3 * m + k], recv.at[3 * m + k], sib)
                cp.start()
                cps.append(cp)
        for m, name in enumerate(names):
            kind, shard, _ = geom[name]
            for k, (qx, qy) in enumerate(_other_chips(x, y)):
                theirs = _piece_of(arr[m], kind, shard, 2 * qx + qy, 1 - c)
                _remote(theirs, theirs, send.at[3 * m + k], recv.at[3 * m + k], sib).wait_recv()
        for cp in cps:
            cp.wait_send()

    return pl.pallas_call(
        body, name="gather_forward_" + names[0], in_specs=[ANY] * nb, out_specs=[ANY] * nb,
        out_shape=[jax.ShapeDtypeStruct(a.shape, a.dtype) for a in full],
        input_output_aliases={m: m for m in range(nb)},
        scratch_shapes=[pltpu.SemaphoreType.DMA((3 * nb,)), pltpu.SemaphoreType.DMA((3 * nb,))],
    )(*full)


def _scatter_copies(nb, t, got, send, recv):
    x, y, c = _me()
    return [_remote(t[m].at[2 * qx + qy], got[m].at[k], send.at[3 * m + k], recv.at[3 * m + k], (qx, qy, c))
            for m in range(nb) for k, (qx, qy) in enumerate(_other_chips(x, y))]


def _chip_exchange_start(names, chip_sums, geom, after):
    nb = len(names)
    lands = [lax.empty((N_CHIPS - 1,) + geom[n][2], BF16) for n in names]

    def body(*refs):
        t, got = refs[:nb], refs[nb:2 * nb]
        send, recv = refs[2 * nb + 1], refs[2 * nb + 2]
        token = refs[4 * nb + 3]
        for cp in _scatter_copies(nb, t, got, send, recv):
            cp.start()
        token[...] = jnp.zeros_like(token)

    both = list(chip_sums) + lands
    outs = pl.pallas_call(
        body, name="grad_chip_start_" + names[0],
        out_shape=(pltpu.SemaphoreType.DMA((3 * nb,)), pltpu.SemaphoreType.DMA((3 * nb,)),
                   *[pltpu.HBM(a.shape, a.dtype) for a in both], TOKEN),
        in_specs=[HBM] * (2 * nb) + [ANY],
        out_specs=(SEM, SEM, *[HBM] * (2 * nb), pl.BlockSpec(memory_space=pltpu.VMEM)),
        input_output_aliases={m: 2 + m for m in range(2 * nb)},
        compiler_params=pltpu.CompilerParams(has_side_effects=EFFECT),
    )(*[_hbm(a) for a in both], after)
    return outs[0], outs[1], list(outs[2:2 + 2 * nb]), outs[2 + 2 * nb]


def _chip_exchange_wait(names, send, recv, thru, after):
    nb = len(names)

    def body(*refs):
        for cp in _scatter_copies(nb, refs[:nb], refs[nb:2 * nb], refs[2 * nb], refs[2 * nb + 1]):
            cp.wait_send()
            cp.wait_recv()

    outs = pl.pallas_call(
        body, name="grad_chip_wait_" + names[0], out_shape=tuple(pltpu.HBM(t.shape, t.dtype) for t in thru),
        in_specs=[HBM] * (2 * nb) + [SEM, SEM, ANY], out_specs=tuple([HBM] * (2 * nb)),
        input_output_aliases={m: m for m in range(2 * nb)},
        compiler_params=pltpu.CompilerParams(has_side_effects=EFFECT),
    )(*thru, send, recv, after)
    return list(outs[nb:])


def _allreduce_small(part):
    rows = part.shape[0]
    flips = [(a, b, e) for a in (0, 1) for b in (0, 1) for e in (0, 1) if (a, b, e) != (0, 0, 0)]

    def body(p_ref, o_ref, all_ref, send, recv):
        x, y, c = _me()
        me = 4 * x + 2 * y + c
        all_ref[me] = p_ref[...]
        cps = []
        for k, (a, b, e) in enumerate(flips):
            cp = _remote(p_ref, all_ref.at[me], send.at[k], recv.at[k], (x ^ a, y ^ b, c ^ e))
            cp.start()
            cps.append(cp)
        for k, (a, b, e) in enumerate(flips):
            peer = 4 * (x ^ a) + 2 * (y ^ b) + (c ^ e)
            _remote(p_ref, all_ref.at[peer], send.at[k], recv.at[k], (x ^ a, y ^ b, c ^ e)).wait_recv()
        for cp in cps:
            cp.wait_send()
        tot = all_ref[0]
        for dev in range(1, 8):
            tot = tot + all_ref[dev]
        o_ref[...] = tot

    vm = pl.BlockSpec(memory_space=pltpu.VMEM)
    return pl.pallas_call(
        body, name="allreduce_small", in_specs=[vm], out_specs=vm, out_shape=jax.ShapeDtypeStruct((rows, LANES), F32),
        scratch_shapes=[pltpu.VMEM((8, rows, LANES), F32), pltpu.SemaphoreType.DMA((7,)), pltpu.SemaphoreType.DMA((7,))],
    )(part)


def _rows_tile(rows, cols, mult):
    return _pick(rows, max(mult, (1 << 19) // cols // mult * mult), mult)


def _add_pairs(g, got, kind, shard, where, name):
    p, r, c = got.shape
    tr = _rows_tile(r, c, 16)
    nt = r // tr

    def body(w_ref, a_ref, b_ref, o_ref):
        o_ref[...] = (a_ref[...].astype(F32) + b_ref[...].astype(F32)).astype(BF16)

    if kind == "row":
        g_map = lambda q, i, w_ref: ((2 * q + w_ref[1]) * nt + i, 0)
    else:
        g_map = lambda q, i, w_ref: (w_ref[1] * nt + i, q)
    spec = pl.BlockSpec((None, tr, c), lambda q, i, w_ref: (q, i, 0))
    return pl.pallas_call(
        body, name="grad_add_sibling_" + name,
        grid_spec=pltpu.PrefetchScalarGridSpec(
            num_scalar_prefetch=1, grid=(p, nt), in_specs=[pl.BlockSpec((tr, c), g_map), spec], out_specs=spec),
        out_shape=jax.ShapeDtypeStruct((p, r, c), BF16), compiler_params=_cp(("arbitrary", "arbitrary"), 32),
    )(where, g, got)


def _add_four(t, got, shard, where, name):
    _, r, c = t.shape
    tr = _rows_tile(r, c, 16)
    nt = r // tr

    def body(w_ref, own, t0, t1, t2, o_ref):
        o_ref[...] = ((own[...].astype(F32) + t0[...].astype(F32)) + t1[...].astype(F32)) + t2[...].astype(F32)

    spec = lambda q: pl.BlockSpec((None, tr, c), lambda i, w_ref: (q, i, 0))
    return pl.pallas_call(
        body, name="grad_add_chips_" + name,
        grid_spec=pltpu.PrefetchScalarGridSpec(
            num_scalar_prefetch=1, grid=(nt,),
            in_specs=[pl.BlockSpec((None, tr, c), lambda i, w_ref: (w_ref[0], i, 0)), spec(0), spec(1), spec(2)],
            out_specs=pl.BlockSpec((tr, c), lambda i, w_ref: (w_ref[1] * nt + i, 0))),
        out_shape=jax.ShapeDtypeStruct(shard, F32), compiler_params=_cp(("arbitrary",), 32),
    )(where, t, got, got, got)


def _adamw(w, g, m, v, name):
    r, c = w.shape
    tr = _rows_tile(r, c, 8)

    def body(w_ref, g_ref, m_ref, v_ref, go_ref, d_ref, mo_ref, vo_ref):
        gv = g_ref[...]
        mn = ADAM_B1 * m_ref[...] + (1.0 - ADAM_B1) * gv
        vn = ADAM_B2 * v_ref[...] + (1.0 - ADAM_B2) * (gv * gv)
        m_hat = mn / (1.0 - ADAM_B1 ** ADAM_STEP)
        v_hat = vn / (1.0 - ADAM_B2 ** ADAM_STEP)
        go_ref[...] = gv
        d_ref[...] = -ADAM_LR * (m_hat / (jnp.sqrt(v_hat) + ADAM_EPS) + ADAM_WD * w_ref[...])
        mo_ref[...] = mn
        vo_ref[...] = vn

    spec = pl.BlockSpec((tr, c), lambda i: (i, 0))
    return pl.pallas_call(
        body, name=name, grid=(r // tr,), in_specs=[spec] * 4, out_specs=[spec] * 4,
        out_shape=[jax.ShapeDtypeStruct((r, c), F32)] * 4, compiler_params=_cp(("arbitrary",), 32),
    )(w, g, m, v)


def _pack(vectors, rows):
    flat = jnp.concatenate([v.reshape(-1).astype(F32) for v in vectors])
    return jnp.pad(flat, (0, rows * LANES - flat.shape[0])).reshape(rows, LANES)


def _unpack(packed, shapes):
    flat = packed.reshape(-1)
    out, off = [], 0
    for shp in shapes:
        n = 1
        for t in shp:
            n *= t
        out.append(flat[off:off + n].reshape(shp))
        off += n
    return out


def _rows_for(shapes):
    n = sum(functools.reduce(lambda a, b: a * b, shp, 1) for shp in shapes)
    return -(-n // (8 * LANES)) * 8


def kernel(x, w_in, attn_sinks, short_conv_w, w_out, ln1_g, ln1_b, ffn_w_up, ffn_conv_w, ffn_w_down, ln2_g, ln2_b, loss_target, m_w_in, m_attn_sinks, m_short_conv_w, m_w_out, m_ln1_g, m_ln1_b, m_ffn_w_up, m_ffn_conv_w, m_ffn_w_down, m_ln2_g, m_ln2_b, v_w_in, v_attn_sinks, v_short_conv_w, v_w_out, v_ln1_g, v_ln1_b, v_ffn_w_up, v_ffn_conv_w, v_ffn_w_down, v_ln2_g, v_ln2_b):
    xs, tgt = x[0], loss_target[0]
    s, d = xs.shape
    chip = 2 * lax.axis_index("x") + lax.axis_index("y")

    t_in = lambda a: a[0].T
    w_big = dict(win_t=t_in(w_in), wout=w_out[0], wup=ffn_w_up[0], wdown=ffn_w_down[0])
    m_big = dict(win_t=t_in(m_w_in), wout=m_w_out[0], wup=m_ffn_w_up[0], wdown=m_ffn_w_down[0])
    v_big = dict(win_t=t_in(v_w_in), wout=v_w_out[0], wup=v_ffn_w_up[0], wdown=v_ffn_w_down[0])
    geom = _geom({n: w_big[n].shape for n in BIG})
    pad8 = lambda a: jnp.pad(a[0], ((0, 5), (0, 0)))
    where = jnp.stack([chip, lax.axis_index("c")]).astype(jnp.int32)
    placed = {n: _place_shard(w_big[n], where[:1], geom[n][0], n) for n in BIG}
    full, (scw8, fcw8) = _allgather_weights(MIXER, placed, geom, [pad8(short_conv_w), pad8(ffn_conv_w)])
    send, recv, thru, token = _gather_start(FFN, placed, geom, scw8)
    a = _phase_mixer(xs, xs + token[0, 0], full["win_t"], full["wout"], scw8, attn_sinks, ln1_g, ln1_b)
    landed = _gather_forward(FFN, _gather_wait(FFN, send, recv, thru, geom, a["x1b"]), geom)
    full.update(zip(FFN, landed))
    f = _phase_ffn(a, tgt, full["wup"], full["wdown"], fcw8, ln2_g, ln2_b)

    def chip_sums_of(names, grads):
        from_sibling = _sibling_exchange(names, grads, geom)
        return [_add_pairs(grads[n], from_sibling[m], geom[n][0], geom[n][1], where, n) for m, n in enumerate(names)]

    ffn_sums = chip_sums_of(FFN, f)
    send, recv, thru, token = _chip_exchange_start(FFN, ffn_sums, geom, f["st2"])
    (dproj, dz1b), g_mixer, g_small = _phase_rest(a, f, full["wup"], full["wout"], full["win_t"], scw8, attn_sinks,
                                                  ln1_g + token[0:1, 0:1])

    def finish(names, sums, from_chips):
        halves = [_add_four(sums[m], from_chips[m], geom[n][1], where, n) for m, n in enumerate(names)]
        shards = _sibling_assemble(names, halves, geom)
        return {n: _adamw(w_big[n], shards[m], m_big[n], v_big[n], "adamw_" + n) for m, n in enumerate(names)}

    mixer_sums = chip_sums_of(MIXER, g_mixer)
    send2, recv2, thru2, token2 = _chip_exchange_start(MIXER, mixer_sums, geom, f["st2"])
    grad_x = _grad_x(dproj, dz1b, full["win_t"], after=token2)
    upd = finish(FFN, ffn_sums, _chip_exchange_wait(FFN, send, recv, thru, grad_x))
    upd.update(finish(MIXER, mixer_sums, _chip_exchange_wait(MIXER, send2, recv2, thru2, upd[FFN[0]][1])))

    small_names = ("ln1_g", "ln1_b", "ln2_g", "ln2_b", "sinks", "fcw", "scw")
    small_shapes = [g_small[n].shape for n in small_names]
    red = _allreduce_small(_pack([g_small["loss_sq"].reshape(1)] + [g_small[n] for n in small_names],
                                 _rows_for([(1,)] + small_shapes)))
    loss_sq, *gs = _unpack(red, [(1,)] + small_shapes)
    gs = dict(zip(small_names, gs))
    loss = (0.5 / d) * loss_sq[0]
    fw, sw = ffn_conv_w.shape[2], short_conv_w.shape[2]
    gs["fcw"] = lax.dynamic_slice_in_dim(gs["fcw"], chip * fw, fw, axis=1)
    gs["scw"] = lax.dynamic_slice_in_dim(gs["scw"], chip * sw, sw, axis=1)

    upd["win_t"] = tuple(a.T for a in upd["win_t"])
    sm_w = dict(ln1_g=ln1_g[0], ln1_b=ln1_b[0], ln2_g=ln2_g[0], ln2_b=ln2_b[0], sinks=attn_sinks[0],
                fcw=ffn_conv_w[0], scw=short_conv_w[0])
    sm_m = dict(ln1_g=m_ln1_g[0], ln1_b=m_ln1_b[0], ln2_g=m_ln2_g[0], ln2_b=m_ln2_b[0], sinks=m_attn_sinks[0],
                fcw=m_ffn_conv_w[0], scw=m_short_conv_w[0])
    sm_v = dict(ln1_g=v_ln1_g[0], ln1_b=v_ln1_b[0], ln2_g=v_ln2_g[0], ln2_b=v_ln2_b[0], sinks=v_attn_sinks[0],
                fcw=v_ffn_conv_w[0], scw=v_short_conv_w[0])
    shapes = [sm_w[n].shape for n in small_names]
    rows = _rows_for(shapes)
    packed = [_pack([t[n] for n in small_names], rows) for t in (sm_w, gs, sm_m, sm_v)]
    sm_out = [dict(zip(small_names, _unpack(a, shapes))) for a in _adamw(*packed, "adamw_small")]

    def leaf(kind, name):
        if name in ("w_in", "w_out", "ffn_w_up", "ffn_w_down"):
            key = dict(w_in="win_t", w_out="wout", ffn_w_up="wup", ffn_w_down="wdown")[name]
            return upd[key][kind][None]
        key = dict(attn_sinks="sinks", short_conv_w="scw", ffn_conv_w="fcw").get(name, name)
        return sm_out[kind][key][None]

    order = ("w_in", "attn_sinks", "short_conv_w", "w_out", "ln1_g", "ln1_b", "ffn_w_up", "ffn_conv_w", "ffn_w_down",
             "ln2_g", "ln2_b")
    outs = [loss, grad_x[None]]
    for kind in range(4):
        outs += [leaf(kind, n) for n in order]
    return tuple(outs)
```

```python
import functools

import jax
import jax.numpy as jnp
from jax import lax
from jax.experimental import pallas as pl
from jax.experimental.pallas import tpu as pltpu

F32 = jnp.float32
BF16 = jnp.bfloat16
MESH = pl.DeviceIdType.MESH
ANY = pl.BlockSpec(memory_space=pl.ANY)

HEAD_DIM = 64
N_Q_HEADS = 16
N_KV_HEADS = 2
ATTN_WIDTH = N_Q_HEADS * HEAD_DIM
KV_WIDTH = N_KV_HEADS * HEAD_DIM
BLOCK = 128
ROPE_THETA = 10000.0
LN_EPS = 1e-5
ALPHA = 2.0 ** 0.25
NEG_INF = -1e30
ADAM_LR, ADAM_B1, ADAM_B2, ADAM_EPS, ADAM_WD, ADAM_STEP = 0.001, 0.9, 0.999, 1e-08, 0.01, 10
N_CHIPS = 4
LANES = 128
MXU_DIM = 256
SLAB = 128


def _cp(sem, vmem_mb):
    return pltpu.CompilerParams(dimension_semantics=sem, vmem_limit_bytes=vmem_mb << 20)


def _matmul(a, b, *, mode, m, n, k, tm, tn, tk, out_dtype, name, vmem_mb, a_spec=None, b_spec=None,
            res=None, alpha=1.0, after=None):
    nj, ni, nk = n // tn, m // tm, k // tk
    assert nj * tn == n and ni * tm == m and nk * tk == k, (name, m, n, k, tm, tn, tk)
    if mode == "nn":
        dims = ((1,), (0,))
        a_spec = a_spec or pl.BlockSpec((tm, tk), lambda j, i, kk: (i, kk))
        b_spec = b_spec or pl.BlockSpec((tk, tn), lambda j, i, kk: (kk, j))
    elif mode == "nt":
        dims = ((1,), (1,))
        a_spec = a_spec or pl.BlockSpec((tm, tk), lambda j, i, kk: (i, kk))
        b_spec = b_spec or pl.BlockSpec((tn, tk), lambda j, i, kk: (j, kk))
    else:
        dims = ((0,), (0,))
        a_spec = a_spec or pl.BlockSpec((tk, tm), lambda j, i, kk: (kk, i))
        b_spec = b_spec or pl.BlockSpec((tk, tn), lambda j, i, kk: (kk, j))
    has_res = res is not None
    has_after = after is not None

    def body(*refs):
        refs = refs[1:] if has_after else refs
        a_ref, b_ref = refs[0], refs[1]
        res_ref = refs[2] if has_res else None
        o_ref = refs[2 + has_res]
        part = lax.dot_general(a_ref[...], b_ref[...], (dims, ((), ())), preferred_element_type=F32)

        def finish(acc):
            if has_res:
                acc = acc + alpha * res_ref[...].astype(F32)
            o_ref[...] = acc.astype(o_ref.dtype)

        if nk == 1:
            finish(part)
        else:
            acc_ref = refs[3 + has_res]
            kk = pl.program_id(2)

            @pl.when(kk == 0)
            def _():
                acc_ref[...] = part

            @pl.when(kk > 0)
            def _():
                acc_ref[...] += part

            @pl.when(kk == nk - 1)
            def _():
                finish(acc_ref[...])

    in_specs = [a_spec, b_spec]
    args = [a, b]
    if has_res:
        in_specs.append(pl.BlockSpec((tm, tn), lambda j, i, kk: (i, j)))
        args.append(res)
    if has_after:
        in_specs.insert(0, pl.BlockSpec(after.shape, lambda j, i, kk: (0, 0)))
        args.insert(0, after)
    return pl.pallas_call(
        body, name=name, grid=(nj, ni, nk), in_specs=in_specs,
        out_specs=pl.BlockSpec((tm, tn), lambda j, i, kk: (i, j)),
        out_shape=jax.ShapeDtypeStruct((m, n), out_dtype),
        scratch_shapes=[pltpu.VMEM((tm, tn), F32)] if nk > 1 else [],
        compiler_params=_cp(("arbitrary", "arbitrary", "arbitrary"), vmem_mb),
    )(*args)


def _pick(total, want, mult):
    if total <= want:
        return total
    for t in range(want, 0, -1):
        if total % t == 0 and t % mult == 0:
            return t
    return total


def _rope_tables(s):
    half = HEAD_DIM // 2
    inv_freq = ROPE_THETA ** (-jnp.arange(half, dtype=F32) / half)
    ang = jnp.arange(s, dtype=F32)[:, None] * inv_freq[None, :]
    cos = jnp.tile(jnp.cos(ang), (1, LANES // half))
    sin = jnp.tile(jnp.concatenate([-jnp.sin(ang), jnp.sin(ang)], axis=1), (1, LANES // HEAD_DIM))
    return cos, sin


def _rope(x, cos, sin, lo):
    partner = jnp.where(lo, pltpu.roll(x, LANES - HEAD_DIM // 2, 1), pltpu.roll(x, HEAD_DIM // 2, 1))
    return x * cos + partner * sin


def _dot(a, b, dims):
    return lax.dot_general(a, b, (dims, ((), ())), preferred_element_type=F32)


NN, NT, TN = ((1,), (0,)), ((1,), (1,)), ((0,), (0,))


def _kv_variants(t, head_lo):
    r = pltpu.roll(t, HEAD_DIM, 1)
    zero = jnp.zeros_like(t)
    a = (jnp.where(head_lo, t, zero).astype(BF16), jnp.where(head_lo, r, zero).astype(BF16))
    b = (jnp.where(head_lo, zero, r).astype(BF16), jnp.where(head_lo, zero, t).astype(BF16))
    return a, b


PAIRS_PER_KV = N_Q_HEADS // 2 // N_KV_HEADS
STACK = PAIRS_PER_KV * BLOCK


def _stack_pairs(ref, j, fn):
    return jnp.concatenate([fn(ref[:, p * LANES:(p + 1) * LANES])
                            for p in range(j * PAIRS_PER_KV, (j + 1) * PAIRS_PER_KV)], axis=0)


def _stack_sinks(sink_ref, j, hh):
    return jnp.concatenate([jnp.full((BLOCK, 1), sink_ref[0, 2 * p + hh], F32)
                            for p in range(j * PAIRS_PER_KV, (j + 1) * PAIRS_PER_KV)], axis=0)


def _attn_probs(qp, ka, kb, valid, sink_a, sink_b):
    out = []
    for kk, sink in ((ka, sink_a), (kb, sink_b)):
        s = jnp.where(valid, _dot(qp, kk, NT), NEG_INF)
        mx = jnp.maximum(jnp.max(s, axis=1, keepdims=True), sink)
        e = jnp.exp(s - mx)
        es = jnp.exp(sink - mx)
        inv = 1.0 / (jnp.sum(e, axis=1, keepdims=True) + es)
        out.append((e * inv, es * inv))
    return out


def _attn_common(i, q_ref, k_ref, v_ref, kp_ref, vp_ref, cos_ref, sin_ref, cosp_ref, sinp_ref):
    lane = lax.broadcasted_iota(jnp.int32, (1, LANES), 1)
    lo = (lane % HEAD_DIM) < (HEAD_DIM // 2)
    head_lo = lane < HEAD_DIM
    cos, sin = cos_ref[...], sin_ref[...]
    kc = _rope(k_ref[...].astype(F32), cos, sin, lo)
    kp = _rope(kp_ref[...].astype(F32), cosp_ref[...], sinp_ref[...], lo)
    kext = jnp.concatenate([kp, kc], axis=0)
    vext = jnp.concatenate([vp_ref[...].astype(F32), v_ref[...].astype(F32)], axis=0)
    ka, kb = _kv_variants(kext, head_lo)
    va, vb = _kv_variants(vext, head_lo)
    qi = lax.broadcasted_iota(jnp.int32, (STACK, 1), 0) % BLOCK
    kj = lax.broadcasted_iota(jnp.int32, (1, 2 * BLOCK), 1)
    valid = (kj > qi) & (kj <= qi + BLOCK) & ((kj >= BLOCK) | (i > 0))
    cos4 = jnp.concatenate([cos] * PAIRS_PER_KV, axis=0)
    sin4 = jnp.concatenate([sin] * PAIRS_PER_KV, axis=0)
    return lo, head_lo, cos, sin, cos4, sin4, ka, kb, va, vb, valid


def _attn_fwd(proj, sinks, cos, sin, s):
    nb = s // BLOCK
    kcol, vcol = ATTN_WIDTH // LANES, ATTN_WIDTH // LANES + 1

    def body(q_ref, k_ref, v_ref, kp_ref, vp_ref, cos_ref, sin_ref, cosp_ref, sinp_ref, sink_ref, o_ref):
        i = pl.program_id(0)
        lo, head_lo, cs, sn, cs4, sn4, ka, kb, va, vb, valid = _attn_common(
            i, q_ref, k_ref, v_ref, kp_ref, vp_ref, cos_ref, sin_ref, cosp_ref, sinp_ref)
        for j in range(N_KV_HEADS):
            q4 = _stack_pairs(q_ref, j, lambda t: t.astype(F32))
            qp = (_rope(q4, cs4, sn4, lo) * HEAD_DIM ** -0.5).astype(BF16)
            (pa, _), (pb, _) = _attn_probs(qp, ka[j], kb[j], valid, _stack_sinks(sink_ref, j, 0),
                                           _stack_sinks(sink_ref, j, 1))
            o = (_dot(pa.astype(BF16), va[j], NN) + _dot(pb.astype(BF16), vb[j], NN)).astype(BF16)
            for t in range(PAIRS_PER_KV):
                p = j * PAIRS_PER_KV + t
                o_ref[:, p * LANES:(p + 1) * LANES] = o[t * BLOCK:(t + 1) * BLOCK]

    prev = lambda i: (jnp.maximum(i - 1, 0), 0)
    return pl.pallas_call(
        body, name="attn_fwd", grid=(nb,),
        in_specs=[pl.BlockSpec((BLOCK, ATTN_WIDTH), lambda i: (i, 0)),
                  pl.BlockSpec((BLOCK, LANES), lambda i: (i, kcol)),
                  pl.BlockSpec((BLOCK, LANES), lambda i: (i, vcol)),
                  pl.BlockSpec((BLOCK, LANES), lambda i: (jnp.maximum(i - 1, 0), kcol)),
                  pl.BlockSpec((BLOCK, LANES), lambda i: (jnp.maximum(i - 1, 0), vcol)),
                  pl.BlockSpec((BLOCK, LANES), lambda i: (i, 0)),
                  pl.BlockSpec((BLOCK, LANES), lambda i: (i, 0)),
                  pl.BlockSpec((BLOCK, LANES), prev),
                  pl.BlockSpec((BLOCK, LANES), prev),
                  pl.BlockSpec(memory_space=pltpu.SMEM)],
        out_specs=pl.BlockSpec((BLOCK, ATTN_WIDTH), lambda i: (i, 0)),
        out_shape=jax.ShapeDtypeStruct((s, ATTN_WIDTH), BF16),
        compiler_params=_cp(("arbitrary",), 32),
    )(proj, proj, proj, proj, proj, cos, sin, cos, sin, sinks)


def _attn_bwd(proj, dmix, sinks, cos, sin, s):
    nb = s // BLOCK
    kcol, vcol = ATTN_WIDTH // LANES, ATTN_WIDTH // LANES + 1
    pairs_per_kv = N_Q_HEADS // 2 // N_KV_HEADS

    def body(q_ref, k_ref, v_ref, kp_ref, vp_ref, cos_ref, sin_ref, cosp_ref, sinp_ref, sink_ref, do_ref,
             dq_ref, dk_ref, dv_ref, dsink_ref, ck_ref, cv_ref):
        g = pl.program_id(0)
        i = nb - 1 - g

        @pl.when(g == 0)
        def _():
            ck_ref[...] = jnp.zeros_like(ck_ref)
            cv_ref[...] = jnp.zeros_like(cv_ref)
            dsink_ref[...] = jnp.zeros_like(dsink_ref)

        lo, head_lo, cs, sn, cs4, sn4, ka, kb, va, vb, valid = _attn_common(
            i, q_ref, k_ref, v_ref, kp_ref, vp_ref, cos_ref, sin_ref, cosp_ref, sinp_ref)
        lane = lax.broadcasted_iota(jnp.int32, (1, LANES), 1)
        dk_j, dv_j = [], []
        dsink = jnp.zeros((1, LANES), F32)
        for j in range(N_KV_HEADS):
            q4 = _stack_pairs(q_ref, j, lambda t: t.astype(F32))
            qp = (_rope(q4, cs4, sn4, lo) * HEAD_DIM ** -0.5).astype(BF16)
            probs = _attn_probs(qp, ka[j], kb[j], valid, _stack_sinks(sink_ref, j, 0), _stack_sinks(sink_ref, j, 1))
            do = _stack_pairs(do_ref, j, lambda t: t)
            dq_r = jnp.zeros((STACK, LANES), F32)
            dkc, dvc = [], []
            for hh, ((pr, ps), kk, vv) in enumerate(zip(probs, (ka[j], kb[j]), (va[j], vb[j]))):
                dp = _dot(do, vv, NT)
                delta = jnp.sum(pr * dp, axis=1, keepdims=True)
                ds = (pr * (dp - delta)).astype(BF16)
                psd = ps * delta
                for t in range(PAIRS_PER_KV):
                    head = 2 * (j * PAIRS_PER_KV + t) + hh
                    dsink = dsink + jnp.where(
                        lane == head, -jnp.sum(psd[t * BLOCK:(t + 1) * BLOCK], axis=0, keepdims=True), 0.0)
                dq_r = dq_r + _dot(ds, kk, NN)
                dkc.append(_dot(ds, qp, TN))
                dvc.append(_dot(pr.astype(BF16), do, TN))
            dk_j.append(jnp.where(head_lo, dkc[0], dkc[1]))
            dv_j.append(jnp.where(head_lo, dvc[0], dvc[1]))
            dq = _rope(dq_r * HEAD_DIM ** -0.5, cs4, -sn4, lo).astype(BF16)
            for t in range(PAIRS_PER_KV):
                p = j * PAIRS_PER_KV + t
                dq_ref[:, p * LANES:(p + 1) * LANES] = dq[t * BLOCK:(t + 1) * BLOCK]
        tot_k = [t + pltpu.roll(t, HEAD_DIM, 1) for t in dk_j]
        tot_v = [t + pltpu.roll(t, HEAD_DIM, 1) for t in dv_j]
        dkext = jnp.where(head_lo, tot_k[0], tot_k[1])
        dvext = jnp.where(head_lo, tot_v[0], tot_v[1])
        dk_r = dkext[BLOCK:] + ck_ref[...]
        dk_ref[...] = _rope(dk_r, cs, -sn, lo).astype(BF16)
        dv_ref[...] = (dvext[BLOCK:] + cv_ref[...]).astype(BF16)
        ck_ref[...] = dkext[:BLOCK]
        cv_ref[...] = dvext[:BLOCK]
        dsink_ref[0:1, :] += dsink

    cur = lambda col: (lambda g: (nb - 1 - g, col))
    prv = lambda col: (lambda g: (jnp.maximum(nb - 2 - g, 0), col))
    blk = lambda w, f: pl.BlockSpec((BLOCK, w), f)
    return pl.pallas_call(
        body, name="attn_bwd", grid=(nb,),
        in_specs=[blk(ATTN_WIDTH, cur(0)), blk(LANES, cur(kcol)), blk(LANES, cur(vcol)),
                  blk(LANES, prv(kcol)), blk(LANES, prv(vcol)),
                  blk(LANES, cur(0)), blk(LANES, cur(0)), blk(LANES, prv(0)), blk(LANES, prv(0)),
                  pl.BlockSpec(memory_space=pltpu.SMEM),
                  blk(ATTN_WIDTH, cur(0))],
        out_specs=[blk(ATTN_WIDTH, cur(0)), blk(LANES, cur(0)), blk(LANES, cur(0)),
                   pl.BlockSpec((8, LANES), lambda g: (0, 0))],
        out_shape=[jax.ShapeDtypeStruct((s, ATTN_WIDTH), BF16), jax.ShapeDtypeStruct((s, LANES), BF16),
                   jax.ShapeDtypeStruct((s, LANES), BF16), jax.ShapeDtypeStruct((8, LANES), F32)],
        scratch_shapes=[pltpu.VMEM((BLOCK, LANES), F32), pltpu.VMEM((BLOCK, LANES), F32)],
        compiler_params=_cp(("arbitrary",), 32),
    )(proj, proj, proj, proj, proj, cos, sin, cos, sin, sinks, dmix)


def _causal_conv(x, prev8, w):
    row = lax.broadcasted_iota(jnp.int32, (8, 1), 0)
    r1, r2 = pltpu.roll(x, 1, 0), pltpu.roll(x, 2, 0)
    s1 = jnp.concatenate([jnp.where(row == 0, prev8[7:8], r1[:8]), r1[8:]], axis=0)
    s2 = jnp.concatenate([jnp.where(row == 0, prev8[6:7], jnp.where(row == 1, prev8[7:8], r2[:8])), r2[8:]], axis=0)
    return w[0:1] * s2 + w[1:2] * s1 + w[2:3] * x, s1, s2


def _conv_bwd(dy, x, w, next8):
    r = x.shape[0]
    row = lax.broadcasted_iota(jnp.int32, (8, 1), 0)
    r1, r2 = pltpu.roll(dy, r - 1, 0), pltpu.roll(dy, r - 2, 0)
    n1 = jnp.concatenate([r1[:r - 8], jnp.where(row == 7, next8[0:1], r1[r - 8:])], axis=0)
    n2 = jnp.concatenate([r2[:r - 8], jnp.where(row == 6, next8[0:1], jnp.where(row == 7, next8[1:2], r2[r - 8:]))],
                         axis=0)
    dx = w[2:3] * dy + w[1:2] * n1 + w[0:1] * n2
    dws = [jnp.sum(t * x, axis=0, keepdims=True) for t in (n2, n1, dy)]
    return dx, dws


CONV_COLS = 256


def _convmix_cols(d):
    conv_w = d - ATTN_WIDTH
    base = (ATTN_WIDTH + 2 * KV_WIDTH) // CONV_COLS
    step = conv_w // CONV_COLS
    return base, base + step, base + 2 * step, step


def _convmix_fwd(proj, scw8, s, d):
    gb0, gc0, h0, ncb = _convmix_cols(d)
    tr = _pick(s, 1024, 16)
    ni = s // tr

    def body(gb_ref, gc_ref, h_ref, w_ref, o_ref, carry_ref):
        @pl.when(pl.program_id(1) == 0)
        def _():
            carry_ref[...] = jnp.zeros_like(carry_ref)

        gch = gc_ref[...].astype(F32) * h_ref[...].astype(F32)
        cc, _, _ = _causal_conv(gch, carry_ref[...], w_ref[...])
        o_ref[...] = (gb_ref[...].astype(F32) * cc).astype(BF16)
        carry_ref[...] = gch[tr - 8:]

    spec = lambda c0: pl.BlockSpec((tr, CONV_COLS), lambda j, i: (i, c0 + j))
    return pl.pallas_call(
        body, name="convmix_fwd", grid=(ncb, ni),
        in_specs=[spec(gb0), spec(gc0), spec(h0), pl.BlockSpec((8, CONV_COLS), lambda j, i: (0, j))],
        out_specs=pl.BlockSpec((tr, CONV_COLS), lambda j, i: (i, j)),
        out_shape=jax.ShapeDtypeStruct((s, d - ATTN_WIDTH), BF16),
        scratch_shapes=[pltpu.VMEM((8, CONV_COLS), F32)],
        compiler_params=_cp(("arbitrary", "arbitrary"), 32),
    )(proj, proj, proj, scw8)


def _convmix_bwd(proj, dmix, scw8, s, d):
    gb0, gc0, h0, ncb = _convmix_cols(d)
    tr = _pick(s, 1024, 16)
    ni = s // tr
    dc0 = ATTN_WIDTH // CONV_COLS

    def body(dc_ref, gb_ref, gc_ref, h_ref, gcp_ref, hp_ref, w_ref, d3_ref, dw_ref, nxt_ref):
        g = pl.program_id(1)
        i = ni - 1 - g

        @pl.when(g == 0)
        def _():
            nxt_ref[...] = jnp.zeros_like(nxt_ref)
            dw_ref[...] = jnp.zeros_like(dw_ref)

        w = w_ref[...]
        gb, gc, h = gb_ref[...].astype(F32), gc_ref[...].astype(F32), h_ref[...].astype(F32)
        gch = gc * h
        prev8 = (gcp_ref[...].astype(F32) * hp_ref[...].astype(F32))[8:16] * (i > 0).astype(F32)
        cc, s1, s2 = _causal_conv(gch, prev8, w)
        dc = dc_ref[...].astype(F32)
        dcc = dc * gb
        dgch, dws = _conv_bwd(dcc, gch, w, nxt_ref[...])
        d3_ref[0] = (dc * cc).astype(BF16)
        d3_ref[1] = (dgch * h).astype(BF16)
        d3_ref[2] = (dgch * gc).astype(BF16)
        for t in range(3):
            dw_ref[t:t + 1, :] += dws[t]
        nxt_ref[...] = dcc[0:8]

    cur = lambda c0: pl.BlockSpec((tr, CONV_COLS), lambda j, g: (ni - 1 - g, c0 + j))
    prv = lambda c0: pl.BlockSpec((16, CONV_COLS), lambda j, g: (jnp.maximum((ni - 1 - g) * (tr // 16) - 1, 0), c0 + j))
    return pl.pallas_call(
        body, name="convmix_bwd", grid=(ncb, ni),
        in_specs=[cur(dc0), cur(gb0), cur(gc0), cur(h0), prv(gc0), prv(h0),
                  pl.BlockSpec((8, CONV_COLS), lambda j, g: (0, j))],
        out_specs=[pl.BlockSpec((3, tr, CONV_COLS), lambda j, g: (0, ni - 1 - g, j)),
                   pl.BlockSpec((8, CONV_COLS), lambda j, g: (0, j))],
        out_shape=[jax.ShapeDtypeStruct((3, s, d - ATTN_WIDTH), BF16), jax.ShapeDtypeStruct((8, d - ATTN_WIDTH), F32)],
        scratch_shapes=[pltpu.VMEM((8, CONV_COLS), F32)],
        compiler_params=_cp(("arbitrary", "arbitrary"), 32),
    )(dmix, proj, proj, proj, proj, proj, scw8)


def _ln_fwd(z):
    mu = jnp.mean(z, axis=-1, keepdims=True)
    zc = z - mu
    var = jnp.mean(zc * zc, axis=-1, keepdims=True)
    rstd = lax.rsqrt(var + LN_EPS)
    return zc * rstd, rstd


def _ln_bwd(dout, xh, rstd, g):
    dxh = dout * g
    c1 = jnp.mean(dxh, axis=-1, keepdims=True)
    c2 = jnp.mean(dxh * xh, axis=-1, keepdims=True)
    dz = rstd * (dxh - c1 - xh * c2)
    return dz, jnp.sum(dout * xh, axis=0, keepdims=True), jnp.sum(dout, axis=0, keepdims=True)


def _outproj_ln1(attn, conv, wout, x, g1, b1, s, d):
    tm = _pick(s, 256, 16)
    ka = attn.shape[1]

    def body(a_ref, c_ref, wt_ref, wb_ref, x_ref, g_ref, b_ref, x1_ref, x1b_ref, xh_ref, rs_ref):
        y = _dot(a_ref[...], wt_ref[...], NN) + _dot(c_ref[...], wb_ref[...], NN)
        xh, rstd = _ln_fwd(ALPHA * x_ref[...] + y)
        x1 = xh * g_ref[...] + b_ref[...]
        x1_ref[...] = x1
        x1b_ref[...] = x1.astype(BF16)
        xh_ref[...] = xh.astype(BF16)
        rs_ref[...] = rstd

    row = lambda w: pl.BlockSpec((tm, w), lambda i: (i, 0))
    vec = pl.BlockSpec((1, d), lambda i: (0, 0))
    return pl.pallas_call(
        body, name="outproj_ln1", grid=(s // tm,),
        in_specs=[row(ka), row(d - ka), pl.BlockSpec((ka, d), lambda i: (0, 0)),
                  pl.BlockSpec((d - ka, d), lambda i: (ka // (d - ka), 0)), row(d), vec, vec],
        out_specs=[row(d), row(d), row(d), row(1)],
        out_shape=[jax.ShapeDtypeStruct((s, d), F32), jax.ShapeDtypeStruct((s, d), BF16),
                   jax.ShapeDtypeStruct((s, d), BF16), jax.ShapeDtypeStruct((s, 1), F32)],
        compiler_params=_cp(("arbitrary",), 48),
    )(attn, conv, wout, wout, x, g1, b1)


def _ffn_up(x1b, wup, fcw8, s, d, dff):
    tm = _pick(s, 1024, 16)
    tn = _pick(dff, 512, LANES)
    nj, ni = dff // tn, s // tm

    def body(x_ref, wa_ref, wg_ref, ca_ref, cg_ref, u_ref, y_ref, h_ref, carry_ref):
        @pl.when(pl.program_id(1) == 0)
        def _():
            carry_ref[...] = jnp.zeros_like(carry_ref)

        xa = x_ref[...]
        ys = []
        for part, (w_ref, c_ref) in enumerate(((wa_ref, ca_ref), (wg_ref, cg_ref))):
            ub = _dot(xa, w_ref[...], NN).astype(BF16)
            u_ref[part] = ub
            u = ub.astype(F32)
            y, _, _ = _causal_conv(u, carry_ref[part], c_ref[...])
            carry_ref[part] = u[tm - 8:]
            yb = y.astype(BF16)
            y_ref[part] = yb
            ys.append(yb.astype(F32))
        a2, g2 = ys
        sig = 1.0 / (1.0 + jnp.exp(-a2))
        h_ref[...] = (a2 * sig * g2).astype(BF16)

    return pl.pallas_call(
        body, name="ffn_up", grid=(nj, ni),
        in_specs=[pl.BlockSpec((tm, d), lambda j, i: (i, 0)),
                  pl.BlockSpec((d, tn), lambda j, i: (0, j)),
                  pl.BlockSpec((d, tn), lambda j, i: (0, j + nj)),
                  pl.BlockSpec((8, tn), lambda j, i: (0, j)),
                  pl.BlockSpec((8, tn), lambda j, i: (0, j + nj))],
        out_specs=[pl.BlockSpec((2, tm, tn), lambda j, i: (0, i, j)),
                   pl.BlockSpec((2, tm, tn), lambda j, i: (0, i, j)),
                   pl.BlockSpec((tm, tn), lambda j, i: (i, j))],
        out_shape=[jax.ShapeDtypeStruct((2, s, dff), BF16), jax.ShapeDtypeStruct((2, s, dff), BF16),
                   jax.ShapeDtypeStruct((s, dff), BF16)],
        scratch_shapes=[pltpu.VMEM((2, 8, tn), F32)],
        compiler_params=_cp(("arbitrary", "arbitrary"), 56),
    )(x1b, wup, wup, fcw8, fcw8)


def _ffn_mid_bwd(dz2b, wdown, u3, y3, fcw8, s, d, dff):
    tm = _pick(s, 1024, 16)
    tn = _pick(dff, 512, LANES)
    nj, ni = dff // tn, s // tm

    def body(dz_ref, wd_ref, u_ref, y_ref, ca_ref, cg_ref, du_ref, dw_ref, nxt_ref):
        @pl.when(pl.program_id(1) == 0)
        def _():
            nxt_ref[...] = jnp.zeros_like(nxt_ref)
            dw_ref[...] = jnp.zeros_like(dw_ref)

        a2, g2 = y_ref[0].astype(F32), y_ref[1].astype(F32)
        sig = 1.0 / (1.0 + jnp.exp(-a2))
        silu = a2 * sig
        dhv = _dot(dz_ref[...], wd_ref[...], NT)
        dys = (dhv * g2 * (sig * (1.0 + a2 * (1.0 - sig))), dhv * silu)
        for part, (c_ref, dy) in enumerate(zip((ca_ref, cg_ref), dys)):
            dx, dws = _conv_bwd(dy, u_ref[part].astype(F32), c_ref[...], nxt_ref[part])
            du_ref[part] = dx.astype(BF16)
            for t in range(3):
                dw_ref[part, t:t + 1, :] += dws[t]
            nxt_ref[part] = dy[0:8]

    return pl.pallas_call(
        body, name="ffn_mid_bwd", grid=(nj, ni),
        in_specs=[pl.BlockSpec((tm, d), lambda j, g: (ni - 1 - g, 0)),
                  pl.BlockSpec((tn, d), lambda j, g: (j, 0)),
                  pl.BlockSpec((2, tm, tn), lambda j, g: (0, ni - 1 - g, j)),
                  pl.BlockSpec((2, tm, tn), lambda j, g: (0, ni - 1 - g, j)),
                  pl.BlockSpec((8, tn), lambda j, g: (0, j)),
                  pl.BlockSpec((8, tn), lambda j, g: (0, j + nj))],
        out_specs=[pl.BlockSpec((2, tm, tn), lambda j, g: (0, ni - 1 - g, j)),
                   pl.BlockSpec((2, 8, tn), lambda j, g: (0, 0, j))],
        out_shape=[jax.ShapeDtypeStruct((2, s, dff), BF16), jax.ShapeDtypeStruct((2, 8, dff), F32)],
        scratch_shapes=[pltpu.VMEM((2, 8, tn), F32)],
        compiler_params=_cp(("arbitrary", "arbitrary"), 56),
    )(dz2b, wdown, u3, y3, fcw8, fcw8)


def _ffn_down_loss(hmid, wdown, x1, target, g2, b2, s, d, dff):
    tm = _pick(s, 512, SLAB)
    tk = _pick(dff, 1408, LANES)
    ni, nk = s // tm, dff // tk
    slab = min(SLAB, tm)

    def body(h_ref, w_ref, x1_ref, t_ref, g_ref, b_ref, dzb_ref, st_ref, acc_ref):
        i, kk = pl.program_id(0), pl.program_id(1)

        @pl.when((i == 0) & (kk == 0))
        def _():
            st_ref[...] = jnp.zeros_like(st_ref)

        part = _dot(h_ref[...], w_ref[...], NN)

        @pl.when(kk == 0)
        def _():
            acc_ref[...] = part

        @pl.when(kk > 0)
        def _():
            acc_ref[...] += part

        @pl.when(kk == nk - 1)
        def _():
            g, b = g_ref[...], b_ref[...]

            def one(sl, carry):
                rows = pl.ds(pl.multiple_of(sl * slab, slab), slab)
                xh, rstd = _ln_fwd(ALPHA * x1_ref[rows, :] + acc_ref[rows, :])
                diff = xh * g + b - t_ref[rows, :]
                sq = jnp.sum(jnp.sum(diff * diff, axis=1, keepdims=True), axis=0, keepdims=True)
                dz, dg, db = _ln_bwd(diff * (1.0 / d), xh, rstd, g)
                dzb_ref[rows, :] = dz.astype(BF16)
                st_ref[0:1, :] += dg
                st_ref[1:2, :] += db
                st_ref[2:3, :] += sq
                return carry

            lax.fori_loop(0, tm // slab, one, 0)

    row = pl.BlockSpec((tm, d), lambda i, kk: (i, 0))
    vec = pl.BlockSpec((1, d), lambda i, kk: (0, 0))
    return pl.pallas_call(
        body, name="ffn_down_loss", grid=(ni, nk),
        in_specs=[pl.BlockSpec((tm, tk), lambda i, kk: (i, kk)), pl.BlockSpec((tk, d), lambda i, kk: (kk, 0)),
                  row, row, vec, vec],
        out_specs=[row, pl.BlockSpec((8, d), lambda i, kk: (0, 0))],
        out_shape=[jax.ShapeDtypeStruct((s, d), BF16), jax.ShapeDtypeStruct((8, d), F32)],
        scratch_shapes=[pltpu.VMEM((tm, d), F32)],
        compiler_params=_cp(("arbitrary", "arbitrary"), 48),
    )(hmid, wdown, x1, target, g2, b2)


def _ffn_dx_ln1_bwd(du3, wup, dz2b, xh1, rstd1, g1, s, d, dff):
    tm = _pick(s, 512, SLAB)
    tk = _pick(dff, 2816, MXU_DIM)
    nkh = dff // tk
    ni, nk = s // tm, 2 * nkh
    slab = min(SLAB, tm)

    def body(a_ref, w_ref, dz2_ref, xh_ref, rs_ref, g_ref, dzb_ref, st_ref, acc_ref):
        i, kk = pl.program_id(0), pl.program_id(1)

        @pl.when((i == 0) & (kk == 0))
        def _():
            st_ref[...] = jnp.zeros_like(st_ref)

        part = _dot(a_ref[...], w_ref[...], NT)

        @pl.when(kk == 0)
        def _():
            acc_ref[...] = part

        @pl.when(kk > 0)
        def _():
            acc_ref[...] += part

        @pl.when(kk == nk - 1)
        def _():
            g = g_ref[...]

            def one(sl, carry):
                rows = pl.ds(pl.multiple_of(sl * slab, slab), slab)
                dx1 = ALPHA * dz2_ref[rows, :].astype(F32) + acc_ref[rows, :]
                dz, dg, db = _ln_bwd(dx1, xh_ref[rows, :].astype(F32), rs_ref[rows, :], g)
                dzb_ref[rows, :] = dz.astype(BF16)
                st_ref[0:1, :] += dg
                st_ref[1:2, :] += db
                return carry

            lax.fori_loop(0, tm // slab, one, 0)

    row = pl.BlockSpec((tm, d), lambda i, kk: (i, 0))
    row1 = pl.BlockSpec((tm, d), lambda i, kk: (i, 0), pipeline_mode=pl.Buffered(1))
    return pl.pallas_call(
        body, name="ffn_dx_ln1_bwd", grid=(ni, nk),
        in_specs=[pl.BlockSpec((None, tm, tk), lambda i, kk: (kk // nkh, i, kk % nkh)),
                  pl.BlockSpec((d, tk), lambda i, kk: (0, kk)),
                  row1, row1, pl.BlockSpec((tm, 1), lambda i, kk: (i, 0)), pl.BlockSpec((1, d), lambda i, kk: (0, 0))],
        out_specs=[row, pl.BlockSpec((8, d), lambda i, kk: (0, 0))],
        out_shape=[jax.ShapeDtypeStruct((s, d), BF16), jax.ShapeDtypeStruct((8, d), F32)],
        scratch_shapes=[pltpu.VMEM((tm, d), F32)],
        compiler_params=_cp(("arbitrary", "arbitrary"), 56),
    )(du3, wup, dz2b, xh1, rstd1, g1)


def _phase_mixer(x, x_in, win_t, wout, scw8, sinks, ln1_g, ln1_b):
    s, d = x.shape
    n_in = win_t.shape[0]
    xb = x_in.astype(BF16)
    cos, sin = _rope_tables(s)
    proj = _matmul(xb, win_t, mode="nt", m=s, n=n_in, k=d, tm=_pick(s, 512, 16), tn=n_in, tk=d, out_dtype=BF16,
                   name="in_proj", vmem_mb=48,
                   b_spec=pl.BlockSpec((n_in, d), lambda j, i, kk: (0, 0), pipeline_mode=pl.Buffered(1)))
    attn = _attn_fwd(proj, sinks, cos, sin, s)
    conv = _convmix_fwd(proj, scw8, s, d)
    x1, x1b, xh1, rstd1 = _outproj_ln1(attn, conv, wout, x, ln1_g, ln1_b, s, d)
    return dict(xb=xb, cos=cos, sin=sin, proj=proj, attn=attn, conv=conv, x1=x1, x1b=x1b, xh1=xh1, rstd1=rstd1)


def _phase_ffn(a, target, wup, wdown, fcw8, ln2_g, ln2_b):
    x1, x1b = a["x1"], a["x1b"]
    s, d = x1.shape
    dff = wdown.shape[0]
    u3, y3, hmid = _ffn_up(x1b, wup, fcw8, s, d, dff)
    dz2b, st2 = _ffn_down_loss(hmid, wdown, x1, target, ln2_g, ln2_b, s, d, dff)

    ts = _pick(s, 2048, 16)
    g_wdown = _matmul(hmid, dz2b, mode="tn", m=dff, n=d, k=s, tm=_pick(dff, 2816, MXU_DIM), tn=_pick(d, 512, MXU_DIM),
                      tk=ts, out_dtype=BF16, name="grad_w_down", vmem_mb=56)
    du3, dfcw = _ffn_mid_bwd(dz2b, wdown, u3, y3, fcw8, s, d, dff)
    tnu = _pick(dff, 2816, MXU_DIM)
    njh = dff // tnu
    g_wup = _matmul(x1b, du3, mode="tn", m=d, n=2 * dff, k=s, tm=_pick(d, 512, LANES), tn=tnu, tk=ts, out_dtype=BF16,
                    name="grad_w_up", vmem_mb=56,
                    b_spec=pl.BlockSpec((None, ts, tnu), lambda j, i, kk: (j // njh, kk, j % njh)))
    return dict(du3=du3, dz2b=dz2b, st2=st2, dfcw=dfcw, wdown=g_wdown, wup=g_wup)


def _phase_rest(a, f, wup, wout, win_t, scw8, sinks, ln1_g):
    xb, cos, sin, proj, attn, conv = a["xb"], a["cos"], a["sin"], a["proj"], a["attn"], a["conv"]
    du3, dz2b, st2, dfcw = f["du3"], f["dz2b"], f["st2"], f["dfcw"]
    s, d = a["x1"].shape
    dff = wup.shape[1] // 2
    n_in = win_t.shape[0]
    ts = _pick(s, 2048, 16)
    dz1b, st1 = _ffn_dx_ln1_bwd(du3, wup, dz2b, a["xh1"], a["rstd1"], ln1_g, s, d, dff)

    mix = jnp.concatenate([attn, conv], axis=1)
    g_wout = _matmul(mix, dz1b, mode="tn", m=d, n=d, k=s, tm=_pick(d, 1024, LANES), tn=_pick(d, 1024, LANES), tk=ts,
                     out_dtype=BF16, name="grad_w_out", vmem_mb=48)
    dmix = _matmul(dz1b, wout, mode="nt", m=s, n=d, k=d, tm=_pick(s, 1024, 16), tn=_pick(d, 1024, LANES), tk=d,
                   out_dtype=BF16, name="out_dmix", vmem_mb=48)
    d3, dscw = _convmix_bwd(proj, dmix, scw8, s, d)
    dq, dk, dv, dsink = _attn_bwd(proj, dmix, sinks, cos, sin, s)
    dproj = jnp.concatenate([dq, dk, dv, d3[0], d3[1], d3[2]], axis=1)
    g_win_t = _matmul(dproj, xb, mode="tn", m=n_in, n=d, k=s, tm=_pick(n_in, 2176, LANES), tn=_pick(d, 512, LANES),
                      tk=ts, out_dtype=BF16, name="grad_w_in", vmem_mb=48)
    small = dict(loss_sq=st2[2, 0], ln2_g=st2[0], ln2_b=st2[1], ln1_g=st1[0], ln1_b=st1[1], sinks=dsink[0, :N_Q_HEADS],
                 fcw=jnp.concatenate([dfcw[0, :3], dfcw[1, :3]], axis=1), scw=dscw[:3])
    return (dproj, dz1b), dict(win_t=g_win_t, wout=g_wout), small


def _grad_x(dproj, dz1b, win_t, after=None):
    s, n_in = dproj.shape
    d = win_t.shape[1]
    return _matmul(dproj, win_t, mode="nn", m=s, n=d, k=n_in, tm=_pick(s, 512, 16), tn=_pick(d, 1024, LANES),
                   tk=n_in, out_dtype=F32, name="grad_x", vmem_mb=56, res=dz1b, alpha=ALPHA, after=after)


def _local_step(x, target, win_t, wout, wup, wdown, scw8, fcw8, sinks, ln1_g, ln1_b, ln2_g, ln2_b):
    a = _phase_mixer(x, x, win_t, wout, scw8, sinks, ln1_g, ln1_b)
    f = _phase_ffn(a, target, wup, wdown, fcw8, ln2_g, ln2_b)
    (dproj, dz1b), g, small = _phase_rest(a, f, wup, wout, win_t, scw8, sinks, ln1_g)
    return _grad_x(dproj, dz1b, win_t), dict(g, wup=f["wup"], wdown=f["wdown"]), small


MIXER = ("win_t", "wout")
FFN = ("wup", "wdown")
BIG = MIXER + FFN


def _geom(shard_shapes):
    out = {}
    for name in BIG:
        r, c = shard_shapes[name]
        out[name] = ("col" if name == "wup" else "row", (r, c), (r // 2, c))
    return out


def _full_shape(kind, shard):
    r, c = shard
    return (N_CHIPS * r, c) if kind == "row" else (r, N_CHIPS * c)


def _piece_of(ref, kind, shard, chip, half):
    r, c = shard
    if kind == "row":
        return ref.at[pl.ds(chip * r + half * (r // 2), r // 2), :]
    return ref.at[pl.ds(half * (r // 2), r // 2), pl.ds(chip * c, c)]


def _shard_piece(ref, shard, half):
    r, _ = shard
    return ref.at[pl.ds(half * (r // 2), r // 2), :]


def _me():
    return lax.axis_index("x"), lax.axis_index("y"), lax.axis_index("c")


def _other_chips(x, y):
    return [(1 - x, y), (x, 1 - y), (1 - x, 1 - y)]


def _remote(src, dst, send_sem, recv_sem, dev):
    return pltpu.make_async_remote_copy(src_ref=src, dst_ref=dst, send_sem=send_sem, recv_sem=recv_sem,
                                        device_id=dev, device_id_type=MESH)


def _place_shard(w, chip1, kind, name):
    r, c = w.shape
    tr = _rows_tile(r, c, 16)
    nt = r // tr

    def body(chip_ref, w_ref, o_ref):
        o_ref[...] = w_ref[...].astype(BF16)

    out_map = (lambda i, chip_ref: (chip_ref[0] * nt + i, 0)) if kind == "row" else (lambda i, chip_ref: (i, chip_ref[0]))
    return pl.pallas_call(
        body, name="place_" + name,
        grid_spec=pltpu.PrefetchScalarGridSpec(
            num_scalar_prefetch=1, grid=(nt,),
            in_specs=[pl.BlockSpec((tr, c), lambda i, chip_ref: (i, 0))],
            out_specs=pl.BlockSpec((tr, c), out_map)),
        out_shape=jax.ShapeDtypeStruct(_full_shape(kind, (r, c)), BF16),
        compiler_params=_cp(("arbitrary",), 32),
    )(chip1, w)


def _allgather_weights(names, placed, geom, small_shards):
    nb, ns = len(names), len(small_shards)
    small_w = [a.shape[1] for a in small_shards]

    def body(*refs):
        sm = refs[nb:nb + ns]
        full = refs[nb + ns:2 * nb + ns]
        smf = refs[2 * nb + ns:2 * nb + 2 * ns]
        send, recv, loc = refs[2 * nb + 2 * ns:]
        x, y, c = _me()
        chip = 2 * x + y
        sib = (x, y, 1 - c)
        others = _other_chips(x, y)
        locals_, sends = [], []
        for m, name in enumerate(names):
            kind, shard, _ = geom[name]
            mine = _piece_of(full[m], kind, shard, chip, c)
            for k, (qx, qy) in enumerate(others):
                cp = _remote(mine, mine, send.at[6 * m + k], recv.at[6 * m + k], (qx, qy, c))
                cp.start()
                sends.append(cp)
        for t in range(ns):
            cp = pltpu.make_async_copy(sm[t], smf[t].at[:, pl.ds(chip * small_w[t], small_w[t])], loc.at[t])
            cp.start()
            locals_.append(cp)
            for k, (qx, qy) in enumerate(others):
                cp = _remote(sm[t], smf[t].at[:, pl.ds(chip * small_w[t], small_w[t])],
                             send.at[6 * nb + 3 * t + k], recv.at[6 * nb + 3 * t + k], (qx, qy, c))
                cp.start()
                sends.append(cp)
        for m, name in enumerate(names):
            kind, shard, _ = geom[name]
            for k, (qx, qy) in enumerate(others):
                got = _piece_of(full[m], kind, shard, 2 * qx + qy, c)
                _remote(got, got, send.at[6 * m + k], recv.at[6 * m + k], (qx, qy, c)).wait_recv()
                cp = _remote(got, got, send.at[6 * m + 3 + k], recv.at[6 * m + 3 + k], sib)
                cp.start()
                sends.append(cp)
        for t in range(ns):
            for k, (qx, qy) in enumerate(others):
                got = smf[t].at[:, pl.ds((2 * qx + qy) * small_w[t], small_w[t])]
                _remote(got, got, send.at[6 * nb + 3 * t + k], recv.at[6 * nb + 3 * t + k], (qx, qy, c)).wait_recv()
        for m, name in enumerate(names):
            kind, shard, _ = geom[name]
            for k, (qx, qy) in enumerate(others):
                got = _piece_of(full[m], kind, shard, 2 * qx + qy, 1 - c)
                _remote(got, got, send.at[6 * m + 3 + k], recv.at[6 * m + 3 + k], sib).wait_recv()
        for cp in sends:
            cp.wait_send()
        for cp in locals_:
            cp.wait()

    nsem = 6 * nb + 3 * ns
    out_shape = [jax.ShapeDtypeStruct(placed[n].shape, BF16) for n in names]
    out_shape += [jax.ShapeDtypeStruct((8, N_CHIPS * w), F32) for w in small_w]
    outs = pl.pallas_call(
        body, name="allgather_weights", in_specs=[ANY] * (nb + ns), out_specs=[ANY] * (nb + ns), out_shape=out_shape,
        input_output_aliases={m: m for m in range(nb)},
        scratch_shapes=[pltpu.SemaphoreType.DMA((nsem,)), pltpu.SemaphoreType.DMA((nsem,)),
                        pltpu.SemaphoreType.DMA((ns,))],
    )(*[placed[n] for n in names], *small_shards)
    return dict(zip(names, outs[:nb])), list(outs[nb:])


def _sibling_exchange(names, grads, geom):
    nb = len(names)

    def body(*refs):
        g = refs[:nb]
        got = refs[nb:2 * nb]
        send, recv = refs[2 * nb:]
        x, y, c = _me()
        sib = (x, y, 1 - c)
        cps = []
        for m, name in enumerate(names):
            kind, shard, _ = geom[name]
            for r in range(N_CHIPS):
                cp = _remote(_piece_of(g[m], kind, shard, r, 1 - c), got[m].at[r],
                             send.at[N_CHIPS * m + r], recv.at[N_CHIPS * m + r], sib)
                cp.start()
                cps.append(cp)
        for cp in cps:
            cp.wait_recv()
        for cp in cps:
            cp.wait_send()

    return pl.pallas_call(
        body, name="grad_sibling_exchange_" + names[0], in_specs=[ANY] * nb, out_specs=[ANY] * nb,
        out_shape=[jax.ShapeDtypeStruct((N_CHIPS,) + geom[n][2], BF16) for n in names],
        scratch_shapes=[pltpu.SemaphoreType.DMA((N_CHIPS * nb,)), pltpu.SemaphoreType.DMA((N_CHIPS * nb,))],
    )(*[grads[n] for n in names])


def _sibling_assemble(names, shards, geom):
    nb = len(names)

    def body(*refs):
        full = refs[nb:2 * nb]
        send, recv = refs[2 * nb:]
        x, y, c = _me()
        sib = (x, y, 1 - c)
        cps = []
        for m, name in enumerate(names):
            mine = _shard_piece(full[m], geom[name][1], c)
            cp = _remote(mine, mine, send.at[m], recv.at[m], sib)
            cp.start()
            cps.append(cp)
        for m, name in enumerate(names):
            theirs = _shard_piece(full[m], geom[name][1], 1 - c)
            _remote(theirs, theirs, send.at[m], recv.at[m], sib).wait_recv()
        for cp in cps:
            cp.wait_send()

    return pl.pallas_call(
        body, name="grad_sibling_assemble_" + names[0], in_specs=[ANY] * nb, out_specs=[ANY] * nb,
        out_shape=[jax.ShapeDtypeStruct(geom[n][1], F32) for n in names],
        input_output_aliases={m: m for m in range(nb)},
        scratch_shapes=[pltpu.SemaphoreType.DMA((nb,)), pltpu.SemaphoreType.DMA((nb,))],
    )(*shards)


HBM = pl.BlockSpec(memory_space=pltpu.HBM)
SEM = pl.BlockSpec(memory_space=pltpu.SEMAPHORE)
EFFECT = pltpu.SideEffectType.DATAFLOW_SIDE_EFFECTING
TOKEN = jax.ShapeDtypeStruct((8, LANES), F32)


def _hbm(a):
    return pltpu.with_memory_space_constraint(a, pltpu.HBM)


def _gather_copies(names, full, geom, send, recv):
    x, y, c = _me()
    out = []
    for m, name in enumerate(names):
        kind, shard, _ = geom[name]
        mine = _piece_of(full[m], kind, shard, 2 * x + y, c)
        for k, (qx, qy) in enumerate(_other_chips(x, y)):
            theirs = _piece_of(full[m], kind, shard, 2 * qx + qy, c)
            out.append((_remote(mine, mine, send.at[3 * m + k], recv.at[3 * m + k], (qx, qy, c)),
                        _remote(theirs, theirs, send.at[3 * m + k], recv.at[3 * m + k], (qx, qy, c))))
    return out


def _gather_start(names, placed, geom, after):
    nb = len(names)

    def body(*refs):
        full = refs[:nb]
        send, recv = refs[nb + 1], refs[nb + 2]
        token = refs[2 * nb + 3]
        for cp, _ in _gather_copies(names, full, geom, send, recv):
            cp.start()
        token[...] = jnp.zeros_like(token)

    outs = pl.pallas_call(
        body, name="gather_start_" + names[0],
        out_shape=(pltpu.SemaphoreType.DMA((3 * nb,)), pltpu.SemaphoreType.DMA((3 * nb,)),
                   *[pltpu.HBM(placed[n].shape, BF16) for n in names], TOKEN),
        in_specs=[HBM] * nb + [ANY], out_specs=(SEM, SEM, *[HBM] * nb, pl.BlockSpec(memory_space=pltpu.VMEM)),
        input_output_aliases={m: 2 + m for m in range(nb)},
        compiler_params=pltpu.CompilerParams(has_side_effects=EFFECT),
    )(*[_hbm(placed[n]) for n in names], after)
    return outs[0], outs[1], list(outs[2:2 + nb]), outs[2 + nb]


def _gather_wait(names, send, recv, thru, geom, after):
    nb = len(names)

    def body(*refs):
        full = refs[:nb]
        for mine, theirs in _gather_copies(names, full, geom, refs[nb], refs[nb + 1]):
            mine.wait_send()
            theirs.wait_recv()

    return pl.pallas_call(
        body, name="gather_wait_" + names[0], out_shape=tuple(pltpu.HBM(t.shape, t.dtype) for t in thru),
        in_specs=[HBM] * nb + [SEM, SEM, ANY], out_specs=tuple([HBM] * nb),
        input_output_aliases={m: m for m in range(nb)},
        compiler_params=pltpu.CompilerParams(has_side_effects=EFFECT),
    )(*thru, send, recv, after)


def _gather_forward(names, full, geom):
    nb = len(names)

    def body(*refs):
        arr = refs[nb:2 * nb]
        send, recv = refs[2 * nb:]
        x, y, c = _me()
        sib = (x, y, 1 - c)
        cps = []
        for m, name in enumerate(names):
            kind, shard, _ = geom[name]
            for k, (qx, qy) in enumerate(_other_chips(x, y)):
                got = _piece_of(arr[m], kind, shard, 2 * qx + qy, c)
                cp = _remote(got, got, send.at[3 * m + k], recv.at[3 * m + k], sib)
                cp.start()
                cps.append(cp)
        for m, name in enumerate(names):
            kind, shard, _ = geom[name]
            for k, (qx, qy) in enumerate(_other_chips(x, y)):
                theirs = _piece_of(arr[m], kind, shard, 2 * qx + qy, 1 - c)
                _remote(theirs, theirs, send.at[3 * m + k], recv.at[3 * m + k], sib).wait_recv()
        for cp in cps:
            cp.wait_send()

    return pl.pallas_call(
        body, name="gather_forward_" + names[0], in_specs=[ANY] * nb, out_specs=[ANY] * nb,
        out_shape=[jax.ShapeDtypeStruct(a.shape, a.dtype) for a in full],
        input_output_aliases={m: m for m in range(nb)},
        scratch_shapes=[pltpu.SemaphoreType.DMA((3 * nb,)), pltpu.SemaphoreType.DMA((3 * nb,))],
    )(*full)


def _scatter_copies(nb, t, got, send, recv):
    x, y, c = _me()
    return [_remote(t[m].at[2 * qx + qy], got[m].at[k], send.at[3 * m + k], recv.at[3 * m + k], (qx, qy, c))
            for m in range(nb) for k, (qx, qy) in enumerate(_other_chips(x, y))]


def _chip_exchange_start(names, chip_sums, geom, after):
    nb = len(names)
    lands = [lax.empty((N_CHIPS - 1,) + geom[n][2], BF16) for n in names]

    def body(*refs):
        t, got = refs[:nb], refs[nb:2 * nb]
        send, recv = refs[2 * nb + 1], refs[2 * nb + 2]
        token = refs[4 * nb + 3]
        for cp in _scatter_copies(nb, t, got, send, recv):
            cp.start()
        token[...] = jnp.zeros_like(token)

    both = list(chip_sums) + lands
    outs = pl.pallas_call(
        body, name="grad_chip_start_" + names[0],
        out_shape=(pltpu.SemaphoreType.DMA((3 * nb,)), pltpu.SemaphoreType.DMA((3 * nb,)),
                   *[pltpu.HBM(a.shape, a.dtype) for a in both], TOKEN),
        in_specs=[HBM] * (2 * nb) + [ANY],
        out_specs=(SEM, SEM, *[HBM] * (2 * nb), pl.BlockSpec(memory_space=pltpu.VMEM)),
        input_output_aliases={m: 2 + m for m in range(2 * nb)},
        compiler_params=pltpu.CompilerParams(has_side_effects=EFFECT),
    )(*[_hbm(a) for a in both], after)
    return outs[0], outs[1], list(outs[2:2 + 2 * nb]), outs[2 + 2 * nb]


def _chip_exchange_wait(names, send, recv, thru, after):
    nb = len(names)

    def body(*refs):
        for cp in _scatter_copies(nb, refs[:nb], refs[nb:2 * nb], refs[2 * nb], refs[2 * nb + 1]):
            cp.wait_send()
            cp.wait_recv()

    outs = pl.pallas_call(
        body, name="grad_chip_wait_" + names[0], out_shape=tuple(pltpu.HBM(t.shape, t.dtype) for t in thru),
        in_specs=[HBM] * (2 * nb) + [SEM, SEM, ANY], out_specs=tuple([HBM] * (2 * nb)),
        input_output_aliases={m: m for m in range(2 * nb)},
        compiler_params=pltpu.CompilerParams(has_side_effects=EFFECT),
    )(*thru, send, recv, after)
    return list(outs[nb:])


def _allreduce_small(part):
    rows = part.shape[0]
    flips = [(a, b, e) for a in (0, 1) for b in (0, 1) for e in (0, 1) if (a, b, e) != (0, 0, 0)]

    def body(p_ref, o_ref, all_ref, send, recv):
        x, y, c = _me()
        me = 4 * x + 2 * y + c
        all_ref[me] = p_ref[...]
        cps = []
        for k, (a, b, e) in enumerate(flips):
            cp = _remote(p_ref, all_ref.at[me], send.at[k], recv.at[k], (x ^ a, y ^ b, c ^ e))
            cp.start()
            cps.append(cp)
        for k, (a, b, e) in enumerate(flips):
            peer = 4 * (x ^ a) + 2 * (y ^ b) + (c ^ e)
            _remote(p_ref, all_ref.at[peer], send.at[k], recv.at[k], (x ^ a, y ^ b, c ^ e)).wait_recv()
        for cp in cps:
            cp.wait_send()
        tot = all_ref[0]
        for dev in range(1, 8):
            tot = tot + all_ref[dev]
        o_ref[...] = tot

    vm = pl.BlockSpec(memory_space=pltpu.VMEM)
    return pl.pallas_call(
        body, name="allreduce_small", in_specs=[vm], out_specs=vm, out_shape=jax.ShapeDtypeStruct((rows, LANES), F32),
        scratch_shapes=[pltpu.VMEM((8, rows, LANES), F32), pltpu.SemaphoreType.DMA((7,)), pltpu.SemaphoreType.DMA((7,))],
    )(part)


def _rows_tile(rows, cols, mult, elems=1 << 19):
    return _pick(rows, max(mult, elems // cols // mult * mult), mult)


ADD_TILE = 1 << 20


def _add_pairs(g, got, kind, shard, where, name):
    p, r, c = got.shape
    tr = _rows_tile(r, c, 16, ADD_TILE)
    nt = r // tr

    def body(w_ref, a_ref, b_ref, o_ref):
        o_ref[...] = (a_ref[...].astype(F32) + b_ref[...].astype(F32)).astype(BF16)

    if kind == "row":
        g_map = lambda q, i, w_ref: ((2 * q + w_ref[1]) * nt + i, 0)
    else:
        g_map = lambda q, i, w_ref: (w_ref[1] * nt + i, q)
    spec = pl.BlockSpec((None, tr, c), lambda q, i, w_ref: (q, i, 0))
    return pl.pallas_call(
        body, name="grad_add_sibling_" + name,
        grid_spec=pltpu.PrefetchScalarGridSpec(
            num_scalar_prefetch=1, grid=(p, nt), in_specs=[pl.BlockSpec((tr, c), g_map), spec], out_specs=spec),
        out_shape=jax.ShapeDtypeStruct((p, r, c), BF16), compiler_params=_cp(("arbitrary", "arbitrary"), 32),
    )(where, g, got)


def _add_four(t, got, shard, where, name):
    _, r, c = t.shape
    tr = _rows_tile(r, c, 16, ADD_TILE)
    nt = r // tr

    def body(w_ref, own, t0, t1, t2, o_ref):
        o_ref[...] = ((own[...].astype(F32) + t0[...].astype(F32)) + t1[...].astype(F32)) + t2[...].astype(F32)

    spec = lambda q: pl.BlockSpec((None, tr, c), lambda i, w_ref: (q, i, 0))
    return pl.pallas_call(
        body, name="grad_add_chips_" + name,
        grid_spec=pltpu.PrefetchScalarGridSpec(
            num_scalar_prefetch=1, grid=(nt,),
            in_specs=[pl.BlockSpec((None, tr, c), lambda i, w_ref: (w_ref[0], i, 0)), spec(0), spec(1), spec(2)],
            out_specs=pl.BlockSpec((tr, c), lambda i, w_ref: (w_ref[1] * nt + i, 0))),
        out_shape=jax.ShapeDtypeStruct(shard, F32), compiler_params=_cp(("arbitrary",), 48),
    )(where, t, got, got, got)


def _adamw(w, g, m, v, name):
    r, c = w.shape
    tr = _rows_tile(r, c, 8)

    def body(w_ref, g_ref, m_ref, v_ref, go_ref, d_ref, mo_ref, vo_ref):
        gv = g_ref[...]
        mn = ADAM_B1 * m_ref[...] + (1.0 - ADAM_B1) * gv
        vn = ADAM_B2 * v_ref[...] + (1.0 - ADAM_B2) * (gv * gv)
        m_hat = mn / (1.0 - ADAM_B1 ** ADAM_STEP)
        v_hat = vn / (1.0 - ADAM_B2 ** ADAM_STEP)
        go_ref[...] = gv
        d_ref[...] = -ADAM_LR * (m_hat / (jnp.sqrt(v_hat) + ADAM_EPS) + ADAM_WD * w_ref[...])
        mo_ref[...] = mn
        vo_ref[...] = vn

    spec = pl.BlockSpec((tr, c), lambda i: (i, 0))
    return pl.pallas_call(
        body, name=name, grid=(r // tr,), in_specs=[spec] * 4, out_specs=[spec] * 4,
        out_shape=[jax.ShapeDtypeStruct((r, c), F32)] * 4, compiler_params=_cp(("arbitrary",), 32),
    )(w, g, m, v)


def _pack(vectors, rows):
    flat = jnp.concatenate([v.reshape(-1).astype(F32) for v in vectors])
    return jnp.pad(flat, (0, rows * LANES - flat.shape[0])).reshape(rows, LANES)


def _unpack(packed, shapes):
    flat = packed.reshape(-1)
    out, off = [], 0
    for shp in shapes:
        n = 1
        for t in shp:
            n *= t
        out.append(flat[off:off + n].reshape(shp))
        off += n
    return out


def _rows_for(shapes):
    n = sum(functools.reduce(lambda a, b: a * b, shp, 1) for shp in shapes)
    return -(-n // (8 * LANES)) * 8


def kernel(x, w_in, attn_sinks, short_conv_w, w_out, ln1_g, ln1_b, ffn_w_up, ffn_conv_w, ffn_w_down, ln2_g, ln2_b, loss_target, m_w_in, m_attn_sinks, m_short_conv_w, m_w_out, m_ln1_g, m_ln1_b, m_ffn_w_up, m_ffn_conv_w, m_ffn_w_down, m_ln2_g, m_ln2_b, v_w_in, v_attn_sinks, v_short_conv_w, v_w_out, v_ln1_g, v_ln1_b, v_ffn_w_up, v_ffn_conv_w, v_ffn_w_down, v_ln2_g, v_ln2_b):
    xs, tgt = x[0], loss_target[0]
    s, d = xs.shape
    chip = 2 * lax.axis_index("x") + lax.axis_index("y")

    w_big = dict(win_t=w_in[0], wout=w_out[0], wup=ffn_w_up[0], wdown=ffn_w_down[0])
    m_big = dict(win_t=m_w_in[0], wout=m_w_out[0], wup=m_ffn_w_up[0], wdown=m_ffn_w_down[0])
    v_big = dict(win_t=v_w_in[0], wout=v_w_out[0], wup=v_ffn_w_up[0], wdown=v_ffn_w_down[0])
    to_place = dict(w_big, win_t=w_in[0].T)
    geom = _geom({n: to_place[n].shape for n in BIG})
    pad8 = lambda a: jnp.pad(a[0], ((0, 5), (0, 0)))
    where = jnp.stack([chip, lax.axis_index("c")]).astype(jnp.int32)
    placed = {n: _place_shard(to_place[n], where[:1], geom[n][0], n) for n in BIG}
    full, (scw8, fcw8) = _allgather_weights(MIXER, placed, geom, [pad8(short_conv_w), pad8(ffn_conv_w)])
    send, recv, thru, token = _gather_start(FFN, placed, geom, scw8)
    a = _phase_mixer(xs, xs + token[0, 0], full["win_t"], full["wout"], scw8, attn_sinks, ln1_g, ln1_b)
    landed = _gather_forward(FFN, _gather_wait(FFN, send, recv, thru, geom, a["x1b"]), geom)
    full.update(zip(FFN, landed))
    f = _phase_ffn(a, tgt, full["wup"], full["wdown"], fcw8, ln2_g, ln2_b)

    def chip_sums_of(names, grads):
        from_sibling = _sibling_exchange(names, grads, geom)
        return [_add_pairs(grads[n], from_sibling[m], geom[n][0], geom[n][1], where, n) for m, n in enumerate(names)]

    ffn_sums = chip_sums_of(FFN, f)
    send, recv, thru, token = _chip_exchange_start(FFN, ffn_sums, geom, f["st2"])
    (dproj, dz1b), g_mixer, g_small = _phase_rest(a, f, full["wup"], full["wout"], full["win_t"], scw8, attn_sinks,
                                                  ln1_g + token[0:1, 0:1])

    def finish(names, sums, from_chips):
        halves = [_add_four(sums[m], from_chips[m], geom[n][1], where, n) for m, n in enumerate(names)]
        shards = _sibling_assemble(names, halves, geom)
        grads = {n: shards[m].T if n == "win_t" else shards[m] for m, n in enumerate(names)}
        return {n: _adamw(w_big[n], grads[n], m_big[n], v_big[n], "adamw_" + n) for n in names}

    mixer_sums = chip_sums_of(MIXER, g_mixer)
    send2, recv2, thru2, token2 = _chip_exchange_start(MIXER, mixer_sums, geom, f["st2"])
    grad_x = _grad_x(dproj, dz1b, full["win_t"], after=token2)
    upd = finish(FFN, ffn_sums, _chip_exchange_wait(FFN, send, recv, thru, grad_x))
    upd.update(finish(MIXER, mixer_sums, _chip_exchange_wait(MIXER, send2, recv2, thru2, upd[FFN[0]][1])))

    small_names = ("ln1_g", "ln1_b", "ln2_g", "ln2_b", "sinks", "fcw", "scw")
    small_shapes = [g_small[n].shape for n in small_names]
    red = _allreduce_small(_pack([g_small["loss_sq"].reshape(1)] + [g_small[n] for n in small_names],
                                 _rows_for([(1,)] + small_shapes)))
    loss_sq, *gs = _unpack(red, [(1,)] + small_shapes)
    gs = dict(zip(small_names, gs))
    loss = (0.5 / d) * loss_sq[0]
    fw, sw = ffn_conv_w.shape[2], short_conv_w.shape[2]
    gs["fcw"] = lax.dynamic_slice_in_dim(gs["fcw"], chip * fw, fw, axis=1)
    gs["scw"] = lax.dynamic_slice_in_dim(gs["scw"], chip * sw, sw, axis=1)

    sm_w = dict(ln1_g=ln1_g[0], ln1_b=ln1_b[0], ln2_g=ln2_g[0], ln2_b=ln2_b[0], sinks=attn_sinks[0],
                fcw=ffn_conv_w[0], scw=short_conv_w[0])
    sm_m = dict(ln1_g=m_ln1_g[0], ln1_b=m_ln1_b[0], ln2_g=m_ln2_g[0], ln2_b=m_ln2_b[0], sinks=m_attn_sinks[0],
                fcw=m_ffn_conv_w[0], scw=m_short_conv_w[0])
    sm_v = dict(ln1_g=v_ln1_g[0], ln1_b=v_ln1_b[0], ln2_g=v_ln2_g[0], ln2_b=v_ln2_b[0], sinks=v_attn_sinks[0],
                fcw=v_ffn_conv_w[0], scw=v_short_conv_w[0])
    shapes = [sm_w[n].shape for n in small_names]
    rows = _rows_for(shapes)
    packed = [_pack([t[n] for n in small_names], rows) for t in (sm_w, gs, sm_m, sm_v)]
    sm_out = [dict(zip(small_names, _unpack(a, shapes))) for a in _adamw(*packed, "adamw_small")]

    def leaf(kind, name):
        if name in ("w_in", "w_out", "ffn_w_up", "ffn_w_down"):
            key = dict(w_in="win_t", w_out="wout", ffn_w_up="wup", ffn_w_down="wdown")[name]
            return upd[key][kind][None]
        key = dict(attn_sinks="sinks", short_conv_w="scw", ffn_conv_w="fcw").get(name, name)
        return sm_out[kind][key][None]

    order = ("w_in", "attn_sinks", "short_conv_w", "w_out", "ln1_g", "ln1_b", "ffn_w_up", "ffn_conv_w", "ffn_w_down",
             "ln2_g", "ln2_b")
    outs = [loss, grad_x[None]]
    for kind in range(4):
        outs += [leaf(kind, n) for n in order]
    return tuple(outs)
```

```python
import functools

import jax
import jax.numpy as jnp
from jax import lax
from jax.experimental import pallas as pl
from jax.experimental.pallas import tpu as pltpu

F32 = jnp.float32
BF16 = jnp.bfloat16
MESH = pl.DeviceIdType.MESH
ANY = pl.BlockSpec(memory_space=pl.ANY)

HEAD_DIM = 64
N_Q_HEADS = 16
N_KV_HEADS = 2
ATTN_WIDTH = N_Q_HEADS * HEAD_DIM
KV_WIDTH = N_KV_HEADS * HEAD_DIM
BLOCK = 128
ROPE_THETA = 10000.0
LN_EPS = 1e-5
ALPHA = 2.0 ** 0.25
NEG_INF = -1e30
ADAM_LR, ADAM_B1, ADAM_B2, ADAM_EPS, ADAM_WD, ADAM_STEP = 0.001, 0.9, 0.999, 1e-08, 0.01, 10
N_CHIPS = 4
LANES = 128
MXU_DIM = 256
SLAB = 128


def _cp(sem, vmem_mb):
    return pltpu.CompilerParams(dimension_semantics=sem, vmem_limit_bytes=vmem_mb << 20)


def _matmul(a, b, *, mode, m, n, k, tm, tn, tk, out_dtype, name, vmem_mb, a_spec=None, b_spec=None,
            res=None, alpha=1.0, after=None):
    nj, ni, nk = n // tn, m // tm, k // tk
    assert nj * tn == n and ni * tm == m and nk * tk == k, (name, m, n, k, tm, tn, tk)
    if mode == "nn":
        dims = ((1,), (0,))
        a_spec = a_spec or pl.BlockSpec((tm, tk), lambda j, i, kk: (i, kk))
        b_spec = b_spec or pl.BlockSpec((tk, tn), lambda j, i, kk: (kk, j))
    elif mode == "nt":
        dims = ((1,), (1,))
        a_spec = a_spec or pl.BlockSpec((tm, tk), lambda j, i, kk: (i, kk))
        b_spec = b_spec or pl.BlockSpec((tn, tk), lambda j, i, kk: (j, kk))
    else:
        dims = ((0,), (0,))
        a_spec = a_spec or pl.BlockSpec((tk, tm), lambda j, i, kk: (kk, i))
        b_spec = b_spec or pl.BlockSpec((tk, tn), lambda j, i, kk: (kk, j))
    has_res = res is not None
    has_after = after is not None

    def body(*refs):
        refs = refs[1:] if has_after else refs
        a_ref, b_ref = refs[0], refs[1]
        res_ref = refs[2] if has_res else None
        o_ref = refs[2 + has_res]
        part = lax.dot_general(a_ref[...], b_ref[...], (dims, ((), ())), preferred_element_type=F32)

        def finish(acc):
            if has_res:
                acc = acc + alpha * res_ref[...].astype(F32)
            o_ref[...] = acc.astype(o_ref.dtype)

        if nk == 1:
            finish(part)
        else:
            acc_ref = refs[3 + has_res]
            kk = pl.program_id(2)

            @pl.when(kk == 0)
            def _():
                acc_ref[...] = part

            @pl.when(kk > 0)
            def _():
                acc_ref[...] += part

            @pl.when(kk == nk - 1)
            def _():
                finish(acc_ref[...])

    in_specs = [a_spec, b_spec]
    args = [a, b]
    if has_res:
        in_specs.append(pl.BlockSpec((tm, tn), lambda j, i, kk: (i, j)))
        args.append(res)
    if has_after:
        in_specs.insert(0, pl.BlockSpec(after.shape, lambda j, i, kk: (0, 0)))
        args.insert(0, after)
    return pl.pallas_call(
        body, name=name, grid=(nj, ni, nk), in_specs=in_specs,
        out_specs=pl.BlockSpec((tm, tn), lambda j, i, kk: (i, j)),
        out_shape=jax.ShapeDtypeStruct((m, n), out_dtype),
        scratch_shapes=[pltpu.VMEM((tm, tn), F32)] if nk > 1 else [],
        compiler_params=_cp(("arbitrary", "arbitrary", "arbitrary"), vmem_mb),
    )(*args)


def _pick(total, want, mult):
    if total <= want:
        return total
    for t in range(want, 0, -1):
        if total % t == 0 and t % mult == 0:
            return t
    return total


def _rope_tables(s):
    half = HEAD_DIM // 2
    inv_freq = ROPE_THETA ** (-jnp.arange(half, dtype=F32) / half)
    ang = jnp.arange(s, dtype=F32)[:, None] * inv_freq[None, :]
    cos = jnp.tile(jnp.cos(ang), (1, LANES // half))
    sin = jnp.tile(jnp.concatenate([-jnp.sin(ang), jnp.sin(ang)], axis=1), (1, LANES // HEAD_DIM))
    return cos, sin


def _rope(x, cos, sin, lo):
    partner = jnp.where(lo, pltpu.roll(x, LANES - HEAD_DIM // 2, 1), pltpu.roll(x, HEAD_DIM // 2, 1))
    return x * cos + partner * sin


def _dot(a, b, dims):
    return lax.dot_general(a, b, (dims, ((), ())), preferred_element_type=F32)


NN, NT, TN = ((1,), (0,)), ((1,), (1,)), ((0,), (0,))


def _kv_variants(t, head_lo):
    r = pltpu.roll(t, HEAD_DIM, 1)
    zero = jnp.zeros_like(t)
    a = (jnp.where(head_lo, t, zero).astype(BF16), jnp.where(head_lo, r, zero).astype(BF16))
    b = (jnp.where(head_lo, zero, r).astype(BF16), jnp.where(head_lo, zero, t).astype(BF16))
    return a, b


PAIRS_PER_KV = N_Q_HEADS // 2 // N_KV_HEADS
STACK = PAIRS_PER_KV * BLOCK


def _stack_pairs(ref, j, fn):
    return jnp.concatenate([fn(ref[:, p * LANES:(p + 1) * LANES])
                            for p in range(j * PAIRS_PER_KV, (j + 1) * PAIRS_PER_KV)], axis=0)


def _sink_row(sink_ref, j, hh):
    col = lax.broadcasted_iota(jnp.int32, (1, STACK), 1)
    heads = [2 * p + hh for p in range(j * PAIRS_PER_KV, (j + 1) * PAIRS_PER_KV)]
    row = jnp.full((1, STACK), sink_ref[0, heads[-1]], F32)
    for t in range(PAIRS_PER_KV - 2, -1, -1):
        row = jnp.where(col < (t + 1) * BLOCK, sink_ref[0, heads[t]], row)
    return row


def _attn_exps(qp, ka, kb, valid, sink_a, sink_b):
    out = []
    for kk, sink in ((ka, sink_a), (kb, sink_b)):
        s = jnp.where(valid, _dot(kk, qp, NT), NEG_INF)
        mx = jnp.maximum(jnp.max(s, axis=0, keepdims=True), sink)
        out.append((jnp.exp(s - mx), jnp.exp(sink - mx)))
    return out


def _attn_common(i, q_ref, k_ref, v_ref, kp_ref, vp_ref, cos_ref, sin_ref, cosp_ref, sinp_ref):
    lane = lax.broadcasted_iota(jnp.int32, (1, LANES), 1)
    lo = (lane % HEAD_DIM) < (HEAD_DIM // 2)
    head_lo = lane < HEAD_DIM
    cos, sin = cos_ref[...], sin_ref[...]
    kc = _rope(k_ref[...].astype(F32), cos, sin, lo)
    kp = _rope(kp_ref[...].astype(F32), cosp_ref[...], sinp_ref[...], lo)
    kext = jnp.concatenate([kp, kc], axis=0)
    vext = jnp.concatenate([vp_ref[...].astype(F32), v_ref[...].astype(F32)], axis=0)
    ka, kb = _kv_variants(kext, head_lo)
    va, vb = _kv_variants(vext, head_lo)
    qi = lax.broadcasted_iota(jnp.int32, (1, STACK), 1) % BLOCK
    kj = lax.broadcasted_iota(jnp.int32, (2 * BLOCK, 1), 0)
    valid = (kj > qi) & (kj <= qi + BLOCK) & ((kj >= BLOCK) | (i > 0))
    cos4 = jnp.concatenate([cos] * PAIRS_PER_KV, axis=0)
    sin4 = jnp.concatenate([sin] * PAIRS_PER_KV, axis=0)
    return lo, head_lo, cos, sin, cos4, sin4, ka, kb, va, vb, valid


def _attn_fwd(proj, sinks, cos, sin, s):
    nb = s // BLOCK
    kcol, vcol = ATTN_WIDTH // LANES, ATTN_WIDTH // LANES + 1

    def body(q_ref, k_ref, v_ref, kp_ref, vp_ref, cos_ref, sin_ref, cosp_ref, sinp_ref, sink_ref, o_ref):
        i = pl.program_id(0)
        lo, head_lo, cs, sn, cs4, sn4, ka, kb, va, vb, valid = _attn_common(
            i, q_ref, k_ref, v_ref, kp_ref, vp_ref, cos_ref, sin_ref, cosp_ref, sinp_ref)
        row = lax.broadcasted_iota(jnp.int32, (16, 1), 0)
        one = jnp.ones((), BF16)
        for j in range(N_KV_HEADS):
            q4 = _stack_pairs(q_ref, j, lambda t: t.astype(F32))
            qp = (_rope(q4, cs4, sn4, lo) * HEAD_DIM ** -0.5).astype(BF16)
            exps = _attn_exps(qp, ka[j], kb[j], valid, _sink_row(sink_ref, j, 0), _sink_row(sink_ref, j, 1))
            outs = []
            for (e, es), vv, mine in zip(exps, (va[j], vb[j]), (head_lo, ~head_lo)):
                ee = jnp.concatenate([e.astype(BF16), jnp.where(row == 0, es, 0.0).astype(BF16)], axis=0)
                tail = jnp.where((row == 0) & ~mine, one, jnp.zeros((), BF16))
                vx = jnp.concatenate([jnp.where(mine, vv, one), tail], axis=0)
                un = _dot(ee, vx, TN)
                outs.append(un / pltpu.roll(un, HEAD_DIM, 1))
            o = jnp.where(head_lo, outs[0], outs[1]).astype(BF16)
            for t in range(PAIRS_PER_KV):
                p = j * PAIRS_PER_KV + t
                o_ref[:, p * LANES:(p + 1) * LANES] = o[t * BLOCK:(t + 1) * BLOCK]

    prev = lambda i: (jnp.maximum(i - 1, 0), 0)
    return pl.pallas_call(
        body, name="attn_fwd", grid=(nb,),
        in_specs=[pl.BlockSpec((BLOCK, ATTN_WIDTH), lambda i: (i, 0)),
                  pl.BlockSpec((BLOCK, LANES), lambda i: (i, kcol)),
                  pl.BlockSpec((BLOCK, LANES), lambda i: (i, vcol)),
                  pl.BlockSpec((BLOCK, LANES), lambda i: (jnp.maximum(i - 1, 0), kcol)),
                  pl.BlockSpec((BLOCK, LANES), lambda i: (jnp.maximum(i - 1, 0), vcol)),
                  pl.BlockSpec((BLOCK, LANES), lambda i: (i, 0)),
                  pl.BlockSpec((BLOCK, LANES), lambda i: (i, 0)),
                  pl.BlockSpec((BLOCK, LANES), prev),
                  pl.BlockSpec((BLOCK, LANES), prev),
                  pl.BlockSpec(memory_space=pltpu.SMEM)],
        out_specs=pl.BlockSpec((BLOCK, ATTN_WIDTH), lambda i: (i, 0)),
        out_shape=jax.ShapeDtypeStruct((s, ATTN_WIDTH), BF16),
        compiler_params=_cp(("arbitrary",), 32),
    )(proj, proj, proj, proj, proj, cos, sin, cos, sin, sinks)


def _attn_bwd(proj, dmix, sinks, cos, sin, s):
    nb = s // BLOCK
    kcol, vcol = ATTN_WIDTH // LANES, ATTN_WIDTH // LANES + 1
    pairs_per_kv = N_Q_HEADS // 2 // N_KV_HEADS

    def body(q_ref, k_ref, v_ref, kp_ref, vp_ref, cos_ref, sin_ref, cosp_ref, sinp_ref, sink_ref, do_ref,
             dq_ref, dk_ref, dv_ref, dsink_ref, ck_ref, cv_ref):
        g = pl.program_id(0)
        i = nb - 1 - g

        @pl.when(g == 0)
        def _():
            ck_ref[...] = jnp.zeros_like(ck_ref)
            cv_ref[...] = jnp.zeros_like(cv_ref)
            dsink_ref[...] = jnp.zeros_like(dsink_ref)

        lo, head_lo, cs, sn, cs4, sn4, ka, kb, va, vb, valid = _attn_common(
            i, q_ref, k_ref, v_ref, kp_ref, vp_ref, cos_ref, sin_ref, cosp_ref, sinp_ref)
        lane = lax.broadcasted_iota(jnp.int32, (1, LANES), 1)
        dk_j, dv_j = [], []
        dsink = jnp.zeros((1, LANES), F32)
        for j in range(N_KV_HEADS):
            q4 = _stack_pairs(q_ref, j, lambda t: t.astype(F32))
            qp = (_rope(q4, cs4, sn4, lo) * HEAD_DIM ** -0.5).astype(BF16)
            exps = _attn_exps(qp, ka[j], kb[j], valid, _sink_row(sink_ref, j, 0), _sink_row(sink_ref, j, 1))
            do = _stack_pairs(do_ref, j, lambda t: t)
            dq_r = jnp.zeros((STACK, LANES), F32)
            dkc, dvc = [], []
            for hh, ((e, es), kk, vv) in enumerate(zip(exps, (ka[j], kb[j]), (va[j], vb[j]))):
                inv = 1.0 / (jnp.sum(e, axis=0, keepdims=True) + es)
                pr = e * inv
                dp = _dot(vv, do, NT)
                delta = jnp.sum(pr * dp, axis=0, keepdims=True)
                ds = (pr * (dp - delta)).astype(BF16)
                psd = es * inv * delta
                for t in range(PAIRS_PER_KV):
                    head = 2 * (j * PAIRS_PER_KV + t) + hh
                    dsink = dsink + jnp.where(
                        lane == head, -jnp.sum(psd[:, t * BLOCK:(t + 1) * BLOCK], axis=1, keepdims=True), 0.0)
                dq_r = dq_r + _dot(ds, kk, TN)
                dkc.append(_dot(ds, qp, NN))
                dvc.append(_dot(pr.astype(BF16), do, NN))
            dk_j.append(jnp.where(head_lo, dkc[0], dkc[1]))
            dv_j.append(jnp.where(head_lo, dvc[0], dvc[1]))
            dq = _rope(dq_r * HEAD_DIM ** -0.5, cs4, -sn4, lo).astype(BF16)
            for t in range(PAIRS_PER_KV):
                p = j * PAIRS_PER_KV + t
                dq_ref[:, p * LANES:(p + 1) * LANES] = dq[t * BLOCK:(t + 1) * BLOCK]
        tot_k = [t + pltpu.roll(t, HEAD_DIM, 1) for t in dk_j]
        tot_v = [t + pltpu.roll(t, HEAD_DIM, 1) for t in dv_j]
        dkext = jnp.where(head_lo, tot_k[0], tot_k[1])
        dvext = jnp.where(head_lo, tot_v[0], tot_v[1])
        dk_r = dkext[BLOCK:] + ck_ref[...]
        dk_ref[...] = _rope(dk_r, cs, -sn, lo).astype(BF16)
        dv_ref[...] = (dvext[BLOCK:] + cv_ref[...]).astype(BF16)
        ck_ref[...] = dkext[:BLOCK]
        cv_ref[...] = dvext[:BLOCK]
        dsink_ref[0:1, :] += dsink

    cur = lambda col: (lambda g: (nb - 1 - g, col))
    prv = lambda col: (lambda g: (jnp.maximum(nb - 2 - g, 0), col))
    blk = lambda w, f: pl.BlockSpec((BLOCK, w), f)
    return pl.pallas_call(
        body, name="attn_bwd", grid=(nb,),
        in_specs=[blk(ATTN_WIDTH, cur(0)), blk(LANES, cur(kcol)), blk(LANES, cur(vcol)),
                  blk(LANES, prv(kcol)), blk(LANES, prv(vcol)),
                  blk(LANES, cur(0)), blk(LANES, cur(0)), blk(LANES, prv(0)), blk(LANES, prv(0)),
                  pl.BlockSpec(memory_space=pltpu.SMEM),
                  blk(ATTN_WIDTH, cur(0))],
        out_specs=[blk(ATTN_WIDTH, cur(0)), blk(LANES, cur(0)), blk(LANES, cur(0)),
                   pl.BlockSpec((8, LANES), lambda g: (0, 0))],
        out_shape=[jax.ShapeDtypeStruct((s, ATTN_WIDTH), BF16), jax.ShapeDtypeStruct((s, LANES), BF16),
                   jax.ShapeDtypeStruct((s, LANES), BF16), jax.ShapeDtypeStruct((8, LANES), F32)],
        scratch_shapes=[pltpu.VMEM((BLOCK, LANES), F32), pltpu.VMEM((BLOCK, LANES), F32)],
        compiler_params=_cp(("arbitrary",), 32),
    )(proj, proj, proj, proj, proj, cos, sin, cos, sin, sinks, dmix)


def _causal_conv(x, prev8, w):
    row = lax.broadcasted_iota(jnp.int32, (8, 1), 0)
    r1, r2 = pltpu.roll(x, 1, 0), pltpu.roll(x, 2, 0)
    s1 = jnp.concatenate([jnp.where(row == 0, prev8[7:8], r1[:8]), r1[8:]], axis=0)
    s2 = jnp.concatenate([jnp.where(row == 0, prev8[6:7], jnp.where(row == 1, prev8[7:8], r2[:8])), r2[8:]], axis=0)
    return w[0:1] * s2 + w[1:2] * s1 + w[2:3] * x, s1, s2


def _conv_bwd(dy, x, w, next8):
    r = x.shape[0]
    row = lax.broadcasted_iota(jnp.int32, (8, 1), 0)
    r1, r2 = pltpu.roll(dy, r - 1, 0), pltpu.roll(dy, r - 2, 0)
    n1 = jnp.concatenate([r1[:r - 8], jnp.where(row == 7, next8[0:1], r1[r - 8:])], axis=0)
    n2 = jnp.concatenate([r2[:r - 8], jnp.where(row == 6, next8[0:1], jnp.where(row == 7, next8[1:2], r2[r - 8:]))],
                         axis=0)
    dx = w[2:3] * dy + w[1:2] * n1 + w[0:1] * n2
    dws = [jnp.sum(t * x, axis=0, keepdims=True) for t in (n2, n1, dy)]
    return dx, dws


CONV_COLS = 256


def _convmix_cols(d):
    conv_w = d - ATTN_WIDTH
    base = (ATTN_WIDTH + 2 * KV_WIDTH) // CONV_COLS
    step = conv_w // CONV_COLS
    return base, base + step, base + 2 * step, step


def _convmix_fwd(proj, scw8, s, d):
    gb0, gc0, h0, ncb = _convmix_cols(d)
    tr = _pick(s, 1024, 16)
    ni = s // tr

    def body(gb_ref, gc_ref, h_ref, w_ref, o_ref, carry_ref):
        @pl.when(pl.program_id(1) == 0)
        def _():
            carry_ref[...] = jnp.zeros_like(carry_ref)

        gch = gc_ref[...].astype(F32) * h_ref[...].astype(F32)
        cc, _, _ = _causal_conv(gch, carry_ref[...], w_ref[...])
        o_ref[...] = (gb_ref[...].astype(F32) * cc).astype(BF16)
        carry_ref[...] = gch[tr - 8:]

    spec = lambda c0: pl.BlockSpec((tr, CONV_COLS), lambda j, i: (i, c0 + j))
    return pl.pallas_call(
        body, name="convmix_fwd", grid=(ncb, ni),
        in_specs=[spec(gb0), spec(gc0), spec(h0), pl.BlockSpec((8, CONV_COLS), lambda j, i: (0, j))],
        out_specs=pl.BlockSpec((tr, CONV_COLS), lambda j, i: (i, j)),
        out_shape=jax.ShapeDtypeStruct((s, d - ATTN_WIDTH), BF16),
        scratch_shapes=[pltpu.VMEM((8, CONV_COLS), F32)],
        compiler_params=_cp(("arbitrary", "arbitrary"), 32),
    )(proj, proj, proj, scw8)


def _convmix_bwd(proj, dmix, scw8, s, d):
    gb0, gc0, h0, ncb = _convmix_cols(d)
    tr = _pick(s, 1024, 16)
    ni = s // tr
    dc0 = ATTN_WIDTH // CONV_COLS

    def body(dc_ref, gb_ref, gc_ref, h_ref, gcp_ref, hp_ref, w_ref, d3_ref, dw_ref, nxt_ref):
        g = pl.program_id(1)
        i = ni - 1 - g

        @pl.when(g == 0)
        def _():
            nxt_ref[...] = jnp.zeros_like(nxt_ref)
            dw_ref[...] = jnp.zeros_like(dw_ref)

        w = w_ref[...]
        gb, gc, h = gb_ref[...].astype(F32), gc_ref[...].astype(F32), h_ref[...].astype(F32)
        gch = gc * h
        prev8 = (gcp_ref[...].astype(F32) * hp_ref[...].astype(F32))[8:16] * (i > 0).astype(F32)
        cc, s1, s2 = _causal_conv(gch, prev8, w)
        dc = dc_ref[...].astype(F32)
        dcc = dc * gb
        dgch, dws = _conv_bwd(dcc, gch, w, nxt_ref[...])
        d3_ref[0] = (dc * cc).astype(BF16)
        d3_ref[1] = (dgch * h).astype(BF16)
        d3_ref[2] = (dgch * gc).astype(BF16)
        for t in range(3):
            dw_ref[t:t + 1, :] += dws[t]
        nxt_ref[...] = dcc[0:8]

    cur = lambda c0: pl.BlockSpec((tr, CONV_COLS), lambda j, g: (ni - 1 - g, c0 + j))
    prv = lambda c0: pl.BlockSpec((16, CONV_COLS), lambda j, g: (jnp.maximum((ni - 1 - g) * (tr // 16) - 1, 0), c0 + j))
    return pl.pallas_call(
        body, name="convmix_bwd", grid=(ncb, ni),
        in_specs=[cur(dc0), cur(gb0), cur(gc0), cur(h0), prv(gc0), prv(h0),
                  pl.BlockSpec((8, CONV_COLS), lambda j, g: (0, j))],
        out_specs=[pl.BlockSpec((3, tr, CONV_COLS), lambda j, g: (0, ni - 1 - g, j)),
                   pl.BlockSpec((8, CONV_COLS), lambda j, g: (0, j))],
        out_shape=[jax.ShapeDtypeStruct((3, s, d - ATTN_WIDTH), BF16), jax.ShapeDtypeStruct((8, d - ATTN_WIDTH), F32)],
        scratch_shapes=[pltpu.VMEM((8, CONV_COLS), F32)],
        compiler_params=_cp(("arbitrary", "arbitrary"), 32),
    )(dmix, proj, proj, proj, proj, proj, scw8)


def _ln_fwd(z):
    mu = jnp.mean(z, axis=-1, keepdims=True)
    zc = z - mu
    var = jnp.mean(zc * zc, axis=-1, keepdims=True)
    rstd = lax.rsqrt(var + LN_EPS)
    return zc * rstd, rstd


def _ln_bwd(dout, xh, rstd, g):
    dxh = dout * g
    c1 = jnp.mean(dxh, axis=-1, keepdims=True)
    c2 = jnp.mean(dxh * xh, axis=-1, keepdims=True)
    dz = rstd * (dxh - c1 - xh * c2)
    return dz, jnp.sum(dout * xh, axis=0, keepdims=True), jnp.sum(dout, axis=0, keepdims=True)


def _outproj_ln1(attn, conv, wout, x, g1, b1, s, d):
    tm = _pick(s, 256, 16)
    ka = attn.shape[1]

    def body(a_ref, c_ref, wt_ref, wb_ref, x_ref, g_ref, b_ref, x1_ref, x1b_ref, xh_ref, rs_ref):
        y = _dot(a_ref[...], wt_ref[...], NN) + _dot(c_ref[...], wb_ref[...], NN)
        xh, rstd = _ln_fwd(ALPHA * x_ref[...] + y)
        x1 = xh * g_ref[...] + b_ref[...]
        x1_ref[...] = x1
        x1b_ref[...] = x1.astype(BF16)
        xh_ref[...] = xh.astype(BF16)
        rs_ref[...] = rstd

    row = lambda w: pl.BlockSpec((tm, w), lambda i: (i, 0))
    vec = pl.BlockSpec((1, d), lambda i: (0, 0))
    return pl.pallas_call(
        body, name="outproj_ln1", grid=(s // tm,),
        in_specs=[row(ka), row(d - ka), pl.BlockSpec((ka, d), lambda i: (0, 0)),
                  pl.BlockSpec((d - ka, d), lambda i: (ka // (d - ka), 0)), row(d), vec, vec],
        out_specs=[row(d), row(d), row(d), row(1)],
        out_shape=[jax.ShapeDtypeStruct((s, d), F32), jax.ShapeDtypeStruct((s, d), BF16),
                   jax.ShapeDtypeStruct((s, d), BF16), jax.ShapeDtypeStruct((s, 1), F32)],
        compiler_params=_cp(("arbitrary",), 48),
    )(attn, conv, wout, wout, x, g1, b1)


def _ffn_up(x1b, wup, fcw8, s, d, dff):
    tm = _pick(s, 1024, 16)
    tn = _pick(dff, 512, LANES)
    nj, ni = dff // tn, s // tm

    def body(x_ref, wa_ref, wg_ref, ca_ref, cg_ref, u_ref, y_ref, h_ref, carry_ref):
        @pl.when(pl.program_id(1) == 0)
        def _():
            carry_ref[...] = jnp.zeros_like(carry_ref)

        xa = x_ref[...]
        ys = []
        for part, (w_ref, c_ref) in enumerate(((wa_ref, ca_ref), (wg_ref, cg_ref))):
            ub = _dot(xa, w_ref[...], NN).astype(BF16)
            u_ref[part] = ub
            u = ub.astype(F32)
            y, _, _ = _causal_conv(u, carry_ref[part], c_ref[...])
            carry_ref[part] = u[tm - 8:]
            yb = y.astype(BF16)
            y_ref[part] = yb
            ys.append(yb.astype(F32))
        a2, g2 = ys
        sig = 1.0 / (1.0 + jnp.exp(-a2))
        h_ref[...] = (a2 * sig * g2).astype(BF16)

    return pl.pallas_call(
        body, name="ffn_up", grid=(nj, ni),
        in_specs=[pl.BlockSpec((tm, d), lambda j, i: (i, 0)),
                  pl.BlockSpec((d, tn), lambda j, i: (0, j)),
                  pl.BlockSpec((d, tn), lambda j, i: (0, j + nj)),
                  pl.BlockSpec((8, tn), lambda j, i: (0, j)),
                  pl.BlockSpec((8, tn), lambda j, i: (0, j + nj))],
        out_specs=[pl.BlockSpec((2, tm, tn), lambda j, i: (0, i, j)),
                   pl.BlockSpec((2, tm, tn), lambda j, i: (0, i, j)),
                   pl.BlockSpec((tm, tn), lambda j, i: (i, j))],
        out_shape=[jax.ShapeDtypeStruct((2, s, dff), BF16), jax.ShapeDtypeStruct((2, s, dff), BF16),
                   jax.ShapeDtypeStruct((s, dff), BF16)],
        scratch_shapes=[pltpu.VMEM((2, 8, tn), F32)],
        compiler_params=_cp(("arbitrary", "arbitrary"), 56),
    )(x1b, wup, wup, fcw8, fcw8)


def _ffn_mid_bwd(dz2b, wdown, u3, y3, fcw8, s, d, dff):
    tm = _pick(s, 1024, 16)
    tn = _pick(dff, 512, LANES)
    nj, ni = dff // tn, s // tm

    def body(dz_ref, wd_ref, u_ref, y_ref, ca_ref, cg_ref, du_ref, dw_ref, nxt_ref):
        @pl.when(pl.program_id(1) == 0)
        def _():
            nxt_ref[...] = jnp.zeros_like(nxt_ref)
            dw_ref[...] = jnp.zeros_like(dw_ref)

        a2, g2 = y_ref[0].astype(F32), y_ref[1].astype(F32)
        sig = 1.0 / (1.0 + jnp.exp(-a2))
        silu = a2 * sig
        dhv = _dot(dz_ref[...], wd_ref[...], NT)
        dys = (dhv * g2 * (sig * (1.0 + a2 * (1.0 - sig))), dhv * silu)
        for part, (c_ref, dy) in enumerate(zip((ca_ref, cg_ref), dys)):
            dx, dws = _conv_bwd(dy, u_ref[part].astype(F32), c_ref[...], nxt_ref[part])
            du_ref[part] = dx.astype(BF16)
            for t in range(3):
                dw_ref[part, t:t + 1, :] += dws[t]
            nxt_ref[part] = dy[0:8]

    return pl.pallas_call(
        body, name="ffn_mid_bwd", grid=(nj, ni),
        in_specs=[pl.BlockSpec((tm, d), lambda j, g: (ni - 1 - g, 0)),
                  pl.BlockSpec((tn, d), lambda j, g: (j, 0)),
                  pl.BlockSpec((2, tm, tn), lambda j, g: (0, ni - 1 - g, j)),
                  pl.BlockSpec((2, tm, tn), lambda j, g: (0, ni - 1 - g, j)),
                  pl.BlockSpec((8, tn), lambda j, g: (0, j)),
                  pl.BlockSpec((8, tn), lambda j, g: (0, j + nj))],
        out_specs=[pl.BlockSpec((2, tm, tn), lambda j, g: (0, ni - 1 - g, j)),
                   pl.BlockSpec((2, 8, tn), lambda j, g: (0, 0, j))],
        out_shape=[jax.ShapeDtypeStruct((2, s, dff), BF16), jax.ShapeDtypeStruct((2, 8, dff), F32)],
        scratch_shapes=[pltpu.VMEM((2, 8, tn), F32)],
        compiler_params=_cp(("arbitrary", "arbitrary"), 56),
    )(dz2b, wdown, u3, y3, fcw8, fcw8)


def _ffn_down_loss(hmid, wdown, x1, target, g2, b2, s, d, dff):
    tm = _pick(s, 512, SLAB)
    tk = _pick(dff, 1408, LANES)
    ni, nk = s // tm, dff // tk
    slab = min(SLAB, tm)

    def body(h_ref, w_ref, x1_ref, t_ref, g_ref, b_ref, dzb_ref, st_ref, acc_ref):
        i, kk = pl.program_id(0), pl.program_id(1)

        @pl.when((i == 0) & (kk == 0))
        def _():
            st_ref[...] = jnp.zeros_like(st_ref)

        part = _dot(h_ref[...], w_ref[...], NN)

        @pl.when(kk == 0)
        def _():
            acc_ref[...] = part

        @pl.when(kk > 0)
        def _():
            acc_ref[...] += part

        @pl.when(kk == nk - 1)
        def _():
            g, b = g_ref[...], b_ref[...]

            def one(sl, carry):
                rows = pl.ds(pl.multiple_of(sl * slab, slab), slab)
                xh, rstd = _ln_fwd(ALPHA * x1_ref[rows, :] + acc_ref[rows, :])
                diff = xh * g + b - t_ref[rows, :]
                sq = jnp.sum(jnp.sum(diff * diff, axis=1, keepdims=True), axis=0, keepdims=True)
                dz, dg, db = _ln_bwd(diff * (1.0 / d), xh, rstd, g)
                dzb_ref[rows, :] = dz.astype(BF16)
                st_ref[0:1, :] += dg
                st_ref[1:2, :] += db
                st_ref[2:3, :] += sq
                return carry

            lax.fori_loop(0, tm // slab, one, 0)

    row = pl.BlockSpec((tm, d), lambda i, kk: (i, 0))
    vec = pl.BlockSpec((1, d), lambda i, kk: (0, 0))
    return pl.pallas_call(
        body, name="ffn_down_loss", grid=(ni, nk),
        in_specs=[pl.BlockSpec((tm, tk), lambda i, kk: (i, kk)), pl.BlockSpec((tk, d), lambda i, kk: (kk, 0)),
                  row, row, vec, vec],
        out_specs=[row, pl.BlockSpec((8, d), lambda i, kk: (0, 0))],
        out_shape=[jax.ShapeDtypeStruct((s, d), BF16), jax.ShapeDtypeStruct((8, d), F32)],
        scratch_shapes=[pltpu.VMEM((tm, d), F32)],
        compiler_params=_cp(("arbitrary", "arbitrary"), 48),
    )(hmid, wdown, x1, target, g2, b2)


def _ffn_dx_ln1_bwd(du3, wup, dz2b, xh1, rstd1, g1, s, d, dff):
    tm = _pick(s, 512, SLAB)
    tk = _pick(dff, 2816, MXU_DIM)
    nkh = dff // tk
    ni, nk = s // tm, 2 * nkh
    slab = min(SLAB, tm)

    def body(a_ref, w_ref, dz2_ref, xh_ref, rs_ref, g_ref, dzb_ref, st_ref, acc_ref):
        i, kk = pl.program_id(0), pl.program_id(1)

        @pl.when((i == 0) & (kk == 0))
        def _():
            st_ref[...] = jnp.zeros_like(st_ref)

        part = _dot(a_ref[...], w_ref[...], NT)

        @pl.when(kk == 0)
        def _():
            acc_ref[...] = part

        @pl.when(kk > 0)
        def _():
            acc_ref[...] += part

        @pl.when(kk == nk - 1)
        def _():
            g = g_ref[...]

            def one(sl, carry):
                rows = pl.ds(pl.multiple_of(sl * slab, slab), slab)
                dx1 = ALPHA * dz2_ref[rows, :].astype(F32) + acc_ref[rows, :]
                dz, dg, db = _ln_bwd(dx1, xh_ref[rows, :].astype(F32), rs_ref[rows, :], g)
                dzb_ref[rows, :] = dz.astype(BF16)
                st_ref[0:1, :] += dg
                st_ref[1:2, :] += db
                return carry

            lax.fori_loop(0, tm // slab, one, 0)

    row = pl.BlockSpec((tm, d), lambda i, kk: (i, 0))
    row1 = pl.BlockSpec((tm, d), lambda i, kk: (i, 0), pipeline_mode=pl.Buffered(1))
    return pl.pallas_call(
        body, name="ffn_dx_ln1_bwd", grid=(ni, nk),
        in_specs=[pl.BlockSpec((None, tm, tk), lambda i, kk: (kk // nkh, i, kk % nkh)),
                  pl.BlockSpec((d, tk), lambda i, kk: (0, kk)),
                  row1, row1, pl.BlockSpec((tm, 1), lambda i, kk: (i, 0)), pl.BlockSpec((1, d), lambda i, kk: (0, 0))],
        out_specs=[row, pl.BlockSpec((8, d), lambda i, kk: (0, 0))],
        out_shape=[jax.ShapeDtypeStruct((s, d), BF16), jax.ShapeDtypeStruct((8, d), F32)],
        scratch_shapes=[pltpu.VMEM((tm, d), F32)],
        compiler_params=_cp(("arbitrary", "arbitrary"), 56),
    )(du3, wup, dz2b, xh1, rstd1, g1)


def _phase_mixer(x, x_in, win_t, wout, scw8, sinks, ln1_g, ln1_b):
    s, d = x.shape
    n_in = win_t.shape[0]
    xb = x_in.astype(BF16)
    cos, sin = _rope_tables(s)
    proj = _matmul(xb, win_t, mode="nt", m=s, n=n_in, k=d, tm=_pick(s, 512, 16), tn=n_in, tk=d, out_dtype=BF16,
                   name="in_proj", vmem_mb=48,
                   b_spec=pl.BlockSpec((n_in, d), lambda j, i, kk: (0, 0), pipeline_mode=pl.Buffered(1)))
    attn = _attn_fwd(proj, sinks, cos, sin, s)
    conv = _convmix_fwd(proj, scw8, s, d)
    x1, x1b, xh1, rstd1 = _outproj_ln1(attn, conv, wout, x, ln1_g, ln1_b, s, d)
    return dict(xb=xb, cos=cos, sin=sin, proj=proj, attn=attn, conv=conv, x1=x1, x1b=x1b, xh1=xh1, rstd1=rstd1)


def _phase_ffn(a, target, wup, wdown, fcw8, ln2_g, ln2_b):
    x1, x1b = a["x1"], a["x1b"]
    s, d = x1.shape
    dff = wdown.shape[0]
    u3, y3, hmid = _ffn_up(x1b, wup, fcw8, s, d, dff)
    dz2b, st2 = _ffn_down_loss(hmid, wdown, x1, target, ln2_g, ln2_b, s, d, dff)

    ts = _pick(s, 2048, 16)
    g_wdown = _matmul(hmid, dz2b, mode="tn", m=dff, n=d, k=s, tm=_pick(dff, 2816, MXU_DIM), tn=_pick(d, 512, MXU_DIM),
                      tk=ts, out_dtype=BF16, name="grad_w_down", vmem_mb=56)
    du3, dfcw = _ffn_mid_bwd(dz2b, wdown, u3, y3, fcw8, s, d, dff)
    tnu = _pick(dff, 2816, MXU_DIM)
    njh = dff // tnu
    g_wup = _matmul(x1b, du3, mode="tn", m=d, n=2 * dff, k=s, tm=_pick(d, 512, LANES), tn=tnu, tk=ts, out_dtype=BF16,
                    name="grad_w_up", vmem_mb=56,
                    b_spec=pl.BlockSpec((None, ts, tnu), lambda j, i, kk: (j // njh, kk, j % njh)))
    return dict(du3=du3, dz2b=dz2b, st2=st2, dfcw=dfcw, wdown=g_wdown, wup=g_wup)


def _phase_rest(a, f, wup, wout, win_t, scw8, sinks, ln1_g):
    xb, cos, sin, proj, attn, conv = a["xb"], a["cos"], a["sin"], a["proj"], a["attn"], a["conv"]
    du3, dz2b, st2, dfcw = f["du3"], f["dz2b"], f["st2"], f["dfcw"]
    s, d = a["x1"].shape
    dff = wup.shape[1] // 2
    n_in = win_t.shape[0]
    ts = _pick(s, 2048, 16)
    dz1b, st1 = _ffn_dx_ln1_bwd(du3, wup, dz2b, a["xh1"], a["rstd1"], ln1_g, s, d, dff)

    mix = jnp.concatenate([attn, conv], axis=1)
    g_wout = _matmul(mix, dz1b, mode="tn", m=d, n=d, k=s, tm=_pick(d, 1024, LANES), tn=_pick(d, 1024, LANES), tk=ts,
                     out_dtype=BF16, name="grad_w_out", vmem_mb=48)
    dmix = _matmul(dz1b, wout, mode="nt", m=s, n=d, k=d, tm=_pick(s, 1024, 16), tn=_pick(d, 1024, LANES), tk=d,
                   out_dtype=BF16, name="out_dmix", vmem_mb=48)
    d3, dscw = _convmix_bwd(proj, dmix, scw8, s, d)
    dq, dk, dv, dsink = _attn_bwd(proj, dmix, sinks, cos, sin, s)
    dproj = jnp.concatenate([dq, dk, dv, d3[0], d3[1], d3[2]], axis=1)
    g_win_t = _matmul(dproj, xb, mode="tn", m=n_in, n=d, k=s, tm=_pick(n_in, 2176, LANES), tn=_pick(d, 512, LANES),
                      tk=ts, out_dtype=BF16, name="grad_w_in", vmem_mb=48)
    small = dict(loss_sq=st2[2, 0], ln2_g=st2[0], ln2_b=st2[1], ln1_g=st1[0], ln1_b=st1[1], sinks=dsink[0, :N_Q_HEADS],
                 fcw=jnp.concatenate([dfcw[0, :3], dfcw[1, :3]], axis=1), scw=dscw[:3])
    return (dproj, dz1b), dict(win_t=g_win_t, wout=g_wout), small


def _grad_x(dproj, dz1b, win_t, after=None):
    s, n_in = dproj.shape
    d = win_t.shape[1]
    return _matmul(dproj, win_t, mode="nn", m=s, n=d, k=n_in, tm=_pick(s, 512, 16), tn=_pick(d, 1024, LANES),
                   tk=n_in, out_dtype=F32, name="grad_x", vmem_mb=56, res=dz1b, alpha=ALPHA, after=after)


def _local_step(x, target, win_t, wout, wup, wdown, scw8, fcw8, sinks, ln1_g, ln1_b, ln2_g, ln2_b):
    a = _phase_mixer(x, x, win_t, wout, scw8, sinks, ln1_g, ln1_b)
    f = _phase_ffn(a, target, wup, wdown, fcw8, ln2_g, ln2_b)
    (dproj, dz1b), g, small = _phase_rest(a, f, wup, wout, win_t, scw8, sinks, ln1_g)
    return _grad_x(dproj, dz1b, win_t), dict(g, wup=f["wup"], wdown=f["wdown"]), small


MIXER = ("win_t", "wout")
FFN = ("wup", "wdown")
BIG = MIXER + FFN


def _geom(shard_shapes):
    out = {}
    for name in BIG:
        r, c = shard_shapes[name]
        out[name] = ("col" if name == "wup" else "row", (r, c), (r // 2, c))
    return out


def _full_shape(kind, shard):
    r, c = shard
    return (N_CHIPS * r, c) if kind == "row" else (r, N_CHIPS * c)


def _piece_of(ref, kind, shard, chip, half):
    r, c = shard
    if kind == "row":
        return ref.at[pl.ds(chip * r + half * (r // 2), r // 2), :]
    return ref.at[pl.ds(half * (r // 2), r // 2), pl.ds(chip * c, c)]


def _shard_piece(ref, shard, half):
    r, _ = shard
    return ref.at[pl.ds(half * (r // 2), r // 2), :]


def _me():
    return lax.axis_index("x"), lax.axis_index("y"), lax.axis_index("c")


def _other_chips(x, y):
    return [(1 - x, y), (x, 1 - y), (1 - x, 1 - y)]


def _remote(src, dst, send_sem, recv_sem, dev):
    return pltpu.make_async_remote_copy(src_ref=src, dst_ref=dst, send_sem=send_sem, recv_sem=recv_sem,
                                        device_id=dev, device_id_type=MESH)


def _place_shard(w, chip1, kind, name):
    r, c = w.shape
    tr = _rows_tile(r, c, 16)
    nt = r // tr

    def body(chip_ref, w_ref, o_ref):
        o_ref[...] = w_ref[...].astype(BF16)

    out_map = (lambda i, chip_ref: (chip_ref[0] * nt + i, 0)) if kind == "row" else (lambda i, chip_ref: (i, chip_ref[0]))
    return pl.pallas_call(
        body, name="place_" + name,
        grid_spec=pltpu.PrefetchScalarGridSpec(
            num_scalar_prefetch=1, grid=(nt,),
            in_specs=[pl.BlockSpec((tr, c), lambda i, chip_ref: (i, 0))],
            out_specs=pl.BlockSpec((tr, c), out_map)),
        out_shape=jax.ShapeDtypeStruct(_full_shape(kind, (r, c)), BF16),
        compiler_params=_cp(("arbitrary",), 32),
    )(chip1, w)


def _allgather_weights(names, placed, geom, small_shards):
    nb, ns = len(names), len(small_shards)
    small_w = [a.shape[1] for a in small_shards]

    def body(*refs):
        sm = refs[nb:nb + ns]
        full = refs[nb + ns:2 * nb + ns]
        smf = refs[2 * nb + ns:2 * nb + 2 * ns]
        send, recv, loc = refs[2 * nb + 2 * ns:]
        x, y, c = _me()
        chip = 2 * x + y
        sib = (x, y, 1 - c)
        others = _other_chips(x, y)
        locals_, sends = [], []
        for m, name in enumerate(names):
            kind, shard, _ = geom[name]
            mine = _piece_of(full[m], kind, shard, chip, c)
            for k, (qx, qy) in enumerate(others):
                cp = _remote(mine, mine, send.at[6 * m + k], recv.at[6 * m + k], (qx, qy, c))
                cp.start()
                sends.append(cp)
        for t in range(ns):
            cp = pltpu.make_async_copy(sm[t], smf[t].at[:, pl.ds(chip * small_w[t], small_w[t])], loc.at[t])
            cp.start()
            locals_.append(cp)
            for k, (qx, qy) in enumerate(others):
                cp = _remote(sm[t], smf[t].at[:, pl.ds(chip * small_w[t], small_w[t])],
                             send.at[6 * nb + 3 * t + k], recv.at[6 * nb + 3 * t + k], (qx, qy, c))
                cp.start()
                sends.append(cp)
        for m, name in enumerate(names):
            kind, shard, _ = geom[name]
            for k, (qx, qy) in enumerate(others):
                got = _piece_of(full[m], kind, shard, 2 * qx + qy, c)
                _remote(got, got, send.at[6 * m + k], recv.at[6 * m + k], (qx, qy, c)).wait_recv()
                cp = _remote(got, got, send.at[6 * m + 3 + k], recv.at[6 * m + 3 + k], sib)
                cp.start()
                sends.append(cp)
        for t in range(ns):
            for k, (qx, qy) in enumerate(others):
                got = smf[t].at[:, pl.ds((2 * qx + qy) * small_w[t], small_w[t])]
                _remote(got, got, send.at[6 * nb + 3 * t + k], recv.at[6 * nb + 3 * t + k], (qx, qy, c)).wait_recv()
        for m, name in enumerate(names):
            kind, shard, _ = geom[name]
            for k, (qx, qy) in enumerate(others):
                got = _piece_of(full[m], kind, shard, 2 * qx + qy, 1 - c)
                _remote(got, got, send.at[6 * m + 3 + k], recv.at[6 * m + 3 + k], sib).wait_recv()
        for cp in sends:
            cp.wait_send()
        for cp in locals_:
            cp.wait()

    nsem = 6 * nb + 3 * ns
    out_shape = [jax.ShapeDtypeStruct(placed[n].shape, BF16) for n in names]
    out_shape += [jax.ShapeDtypeStruct((8, N_CHIPS * w), F32) for w in small_w]
    outs = pl.pallas_call(
        body, name="allgather_weights", in_specs=[ANY] * (nb + ns), out_specs=[ANY] * (nb + ns), out_shape=out_shape,
        input_output_aliases={m: m for m in range(nb)},
        scratch_shapes=[pltpu.SemaphoreType.DMA((nsem,)), pltpu.SemaphoreType.DMA((nsem,)),
                        pltpu.SemaphoreType.DMA((ns,))],
    )(*[placed[n] for n in names], *small_shards)
    return dict(zip(names, outs[:nb])), list(outs[nb:])


def _sibling_exchange(names, grads, geom):
    nb = len(names)

    def body(*refs):
        g = refs[:nb]
        got = refs[nb:2 * nb]
        send, recv = refs[2 * nb:]
        x, y, c = _me()
        sib = (x, y, 1 - c)
        cps = []
        for m, name in enumerate(names):
            kind, shard, _ = geom[name]
            for r in range(N_CHIPS):
                cp = _remote(_piece_of(g[m], kind, shard, r, 1 - c), got[m].at[r],
                             send.at[N_CHIPS * m + r], recv.at[N_CHIPS * m + r], sib)
                cp.start()
                cps.append(cp)
        for cp in cps:
            cp.wait_recv()
        for cp in cps:
            cp.wait_send()

    return pl.pallas_call(
        body, name="grad_sibling_exchange_" + names[0], in_specs=[ANY] * nb, out_specs=[ANY] * nb,
        out_shape=[jax.ShapeDtypeStruct((N_CHIPS,) + geom[n][2], BF16) for n in names],
        scratch_shapes=[pltpu.SemaphoreType.DMA((N_CHIPS * nb,)), pltpu.SemaphoreType.DMA((N_CHIPS * nb,))],
    )(*[grads[n] for n in names])


def _sibling_assemble(names, shards, geom):
    nb = len(names)

    def body(*refs):
        full = refs[nb:2 * nb]
        send, recv = refs[2 * nb:]
        x, y, c = _me()
        sib = (x, y, 1 - c)
        cps = []
        for m, name in enumerate(names):
            mine = _shard_piece(full[m], geom[name][1], c)
            cp = _remote(mine, mine, send.at[m], recv.at[m], sib)
            cp.start()
            cps.append(cp)
        for m, name in enumerate(names):
            theirs = _shard_piece(full[m], geom[name][1], 1 - c)
            _remote(theirs, theirs, send.at[m], recv.at[m], sib).wait_recv()
        for cp in cps:
            cp.wait_send()

    return pl.pallas_call(
        body, name="grad_sibling_assemble_" + names[0], in_specs=[ANY] * nb, out_specs=[ANY] * nb,
        out_shape=[jax.ShapeDtypeStruct(geom[n][1], F32) for n in names],
        input_output_aliases={m: m for m in range(nb)},
        scratch_shapes=[pltpu.SemaphoreType.DMA((nb,)), pltpu.SemaphoreType.DMA((nb,))],
    )(*shards)


HBM = pl.BlockSpec(memory_space=pltpu.HBM)
SEM = pl.BlockSpec(memory_space=pltpu.SEMAPHORE)
EFFECT = pltpu.SideEffectType.DATAFLOW_SIDE_EFFECTING
TOKEN = jax.ShapeDtypeStruct((8, LANES), F32)


def _hbm(a):
    return pltpu.with_memory_space_constraint(a, pltpu.HBM)


def _gather_copies(names, full, geom, send, recv):
    x, y, c = _me()
    out = []
    for m, name in enumerate(names):
        kind, shard, _ = geom[name]
        mine = _piece_of(full[m], kind, shard, 2 * x + y, c)
        for k, (qx, qy) in enumerate(_other_chips(x, y)):
            theirs = _piece_of(full[m], kind, shard, 2 * qx + qy, c)
            out.append((_remote(mine, mine, send.at[3 * m + k], recv.at[3 * m + k], (qx, qy, c)),
                        _remote(theirs, theirs, send.at[3 * m + k], recv.at[3 * m + k], (qx, qy, c))))
    return out


def _gather_start(names, placed, geom, after):
    nb = len(names)

    def body(*refs):
        full = refs[:nb]
        send, recv = refs[nb + 1], refs[nb + 2]
        token = refs[2 * nb + 3]
        for cp, _ in _gather_copies(names, full, geom, send, recv):
            cp.start()
        token[...] = jnp.zeros_like(token)

    outs = pl.pallas_call(
        body, name="gather_start_" + names[0],
        out_shape=(pltpu.SemaphoreType.DMA((3 * nb,)), pltpu.SemaphoreType.DMA((3 * nb,)),
                   *[pltpu.HBM(placed[n].shape, BF16) for n in names], TOKEN),
        in_specs=[HBM] * nb + [ANY], out_specs=(SEM, SEM, *[HBM] * nb, pl.BlockSpec(memory_space=pltpu.VMEM)),
        input_output_aliases={m: 2 + m for m in range(nb)},
        compiler_params=pltpu.CompilerParams(has_side_effects=EFFECT),
    )(*[_hbm(placed[n]) for n in names], after)
    return outs[0], outs[1], list(outs[2:2 + nb]), outs[2 + nb]


def _gather_wait(names, send, recv, thru, geom, after):
    nb = len(names)

    def body(*refs):
        full = refs[:nb]
        for mine, theirs in _gather_copies(names, full, geom, refs[nb], refs[nb + 1]):
            mine.wait_send()
            theirs.wait_recv()

    return pl.pallas_call(
        body, name="gather_wait_" + names[0], out_shape=tuple(pltpu.HBM(t.shape, t.dtype) for t in thru),
        in_specs=[HBM] * nb + [SEM, SEM, ANY], out_specs=tuple([HBM] * nb),
        input_output_aliases={m: m for m in range(nb)},
        compiler_params=pltpu.CompilerParams(has_side_effects=EFFECT),
    )(*thru, send, recv, after)


def _gather_forward(names, full, geom):
    nb = len(names)

    def body(*refs):
        arr = refs[nb:2 * nb]
        send, recv = refs[2 * nb:]
        x, y, c = _me()
        sib = (x, y, 1 - c)
        cps = []
        for m, name in enumerate(names):
            kind, shard, _ = geom[name]
            for k, (qx, qy) in enumerate(_other_chips(x, y)):
                got = _piece_of(arr[m], kind, shard, 2 * qx + qy, c)
                cp = _remote(got, got, send.at[3 * m + k], recv.at[3 * m + k], sib)
                cp.start()
                cps.append(cp)
        for m, name in enumerate(names):
            kind, shard, _ = geom[name]
            for k, (qx, qy) in enumerate(_other_chips(x, y)):
                theirs = _piece_of(arr[m], kind, shard, 2 * qx + qy, 1 - c)
                _remote(theirs, theirs, send.at[3 * m + k], recv.at[3 * m + k], sib).wait_recv()
        for cp in cps:
            cp.wait_send()

    return pl.pallas_call(
        body, name="gather_forward_" + names[0], in_specs=[ANY] * nb, out_specs=[ANY] * nb,
        out_shape=[jax.ShapeDtypeStruct(a.shape, a.dtype) for a in full],
        input_output_aliases={m: m for m in range(nb)},
        scratch_shapes=[pltpu.SemaphoreType.DMA((3 * nb,)), pltpu.SemaphoreType.DMA((3 * nb,))],
    )(*full)


def _scatter_copies(nb, t, got, send, recv):
    x, y, c = _me()
    return [_remote(t[m].at[2 * qx + qy], got[m].at[k], send.at[3 * m + k], recv.at[3 * m + k], (qx, qy, c))
            for m in range(nb) for k, (qx, qy) in enumerate(_other_chips(x, y))]


def _chip_exchange_start(names, chip_sums, geom, after):
    nb = len(names)
    lands = [lax.empty((N_CHIPS - 1,) + geom[n][2], BF16) for n in names]

    def body(*refs):
        t, got = refs[:nb], refs[nb:2 * nb]
        send, recv = refs[2 * nb + 1], refs[2 * nb + 2]
        token = refs[4 * nb + 3]
        for cp in _scatter_copies(nb, t, got, send, recv):
            cp.start()
        token[...] = jnp.zeros_like(token)

    both = list(chip_sums) + lands
    outs = pl.pallas_call(
        body, name="grad_chip_start_" + names[0],
        out_shape=(pltpu.SemaphoreType.DMA((3 * nb,)), pltpu.SemaphoreType.DMA((3 * nb,)),
                   *[pltpu.HBM(a.shape, a.dtype) for a in both], TOKEN),
        in_specs=[HBM] * (2 * nb) + [ANY],
        out_specs=(SEM, SEM, *[HBM] * (2 * nb), pl.BlockSpec(memory_space=pltpu.VMEM)),
        input_output_aliases={m: 2 + m for m in range(2 * nb)},
        compiler_params=pltpu.CompilerParams(has_side_effects=EFFECT),
    )(*[_hbm(a) for a in both], after)
    return outs[0], outs[1], list(outs[2:2 + 2 * nb]), outs[2 + 2 * nb]


def _chip_exchange_wait(names, send, recv, thru, after):
    nb = len(names)

    def body(*refs):
        for cp in _scatter_copies(nb, refs[:nb], refs[nb:2 * nb], refs[2 * nb], refs[2 * nb + 1]):
            cp.wait_send()
            cp.wait_recv()

    outs = pl.pallas_call(
        body, name="grad_chip_wait_" + names[0], out_shape=tuple(pltpu.HBM(t.shape, t.dtype) for t in thru),
        in_specs=[HBM] * (2 * nb) + [SEM, SEM, ANY], out_specs=tuple([HBM] * (2 * nb)),
        input_output_aliases={m: m for m in range(2 * nb)},
        compiler_params=pltpu.CompilerParams(has_side_effects=EFFECT),
    )(*thru, send, recv, after)
    return list(outs[nb:])


def _allreduce_small(part):
    rows = part.shape[0]
    flips = [(a, b, e) for a in (0, 1) for b in (0, 1) for e in (0, 1) if (a, b, e) != (0, 0, 0)]

    def body(p_ref, o_ref, all_ref, send, recv):
        x, y, c = _me()
        me = 4 * x + 2 * y + c
        all_ref[me] = p_ref[...]
        cps = []
        for k, (a, b, e) in enumerate(flips):
            cp = _remote(p_ref, all_ref.at[me], send.at[k], recv.at[k], (x ^ a, y ^ b, c ^ e))
            cp.start()
            cps.append(cp)
        for k, (a, b, e) in enumerate(flips):
            peer = 4 * (x ^ a) + 2 * (y ^ b) + (c ^ e)
            _remote(p_ref, all_ref.at[peer], send.at[k], recv.at[k], (x ^ a, y ^ b, c ^ e)).wait_recv()
        for cp in cps:
            cp.wait_send()
        tot = all_ref[0]
        for dev in range(1, 8):
            tot = tot + all_ref[dev]
        o_ref[...] = tot

    vm = pl.BlockSpec(memory_space=pltpu.VMEM)
    return pl.pallas_call(
        body, name="allreduce_small", in_specs=[vm], out_specs=vm, out_shape=jax.ShapeDtypeStruct((rows, LANES), F32),
        scratch_shapes=[pltpu.VMEM((8, rows, LANES), F32), pltpu.SemaphoreType.DMA((7,)), pltpu.SemaphoreType.DMA((7,))],
    )(part)


def _rows_tile(rows, cols, mult, elems=1 << 19):
    return _pick(rows, max(mult, elems // cols // mult * mult), mult)


ADD_TILE = 1 << 20


def _add_pairs(g, got, kind, shard, where, name):
    p, r, c = got.shape
    tr = _rows_tile(r, c, 16, ADD_TILE)
    nt = r // tr

    def body(w_ref, a_ref, b_ref, o_ref):
        o_ref[...] = (a_ref[...].astype(F32) + b_ref[...].astype(F32)).astype(BF16)

    if kind == "row":
        g_map = lambda q, i, w_ref: ((2 * q + w_ref[1]) * nt + i, 0)
    else:
        g_map = lambda q, i, w_ref: (w_ref[1] * nt + i, q)
    spec = pl.BlockSpec((None, tr, c), lambda q, i, w_ref: (q, i, 0))
    return pl.pallas_call(
        body, name="grad_add_sibling_" + name,
        grid_spec=pltpu.PrefetchScalarGridSpec(
            num_scalar_prefetch=1, grid=(p, nt), in_specs=[pl.BlockSpec((tr, c), g_map), spec], out_specs=spec),
        out_shape=jax.ShapeDtypeStruct((p, r, c), BF16), compiler_params=_cp(("arbitrary", "arbitrary"), 32),
    )(where, g, got)


def _add_four(t, got, shard, where, name):
    _, r, c = t.shape
    tr = _rows_tile(r, c, 16, ADD_TILE)
    nt = r // tr

    def body(w_ref, own, t0, t1, t2, o_ref):
        o_ref[...] = ((own[...].astype(F32) + t0[...].astype(F32)) + t1[...].astype(F32)) + t2[...].astype(F32)

    spec = lambda q: pl.BlockSpec((None, tr, c), lambda i, w_ref: (q, i, 0))
    return pl.pallas_call(
        body, name="grad_add_chips_" + name,
        grid_spec=pltpu.PrefetchScalarGridSpec(
            num_scalar_prefetch=1, grid=(nt,),
            in_specs=[pl.BlockSpec((None, tr, c), lambda i, w_ref: (w_ref[0], i, 0)), spec(0), spec(1), spec(2)],
            out_specs=pl.BlockSpec((tr, c), lambda i, w_ref: (w_ref[1] * nt + i, 0))),
        out_shape=jax.ShapeDtypeStruct(shard, F32), compiler_params=_cp(("arbitrary",), 48),
    )(where, t, got, got, got)


def _adamw(w, g, m, v, name):
    r, c = w.shape
    tr = _rows_tile(r, c, 8)

    def body(w_ref, g_ref, m_ref, v_ref, go_ref, d_ref, mo_ref, vo_ref):
        gv = g_ref[...]
        mn = ADAM_B1 * m_ref[...] + (1.0 - ADAM_B1) * gv
        vn = ADAM_B2 * v_ref[...] + (1.0 - ADAM_B2) * (gv * gv)
        m_hat = mn / (1.0 - ADAM_B1 ** ADAM_STEP)
        v_hat = vn / (1.0 - ADAM_B2 ** ADAM_STEP)
        go_ref[...] = gv
        d_ref[...] = -ADAM_LR * (m_hat / (jnp.sqrt(v_hat) + ADAM_EPS) + ADAM_WD * w_ref[...])
        mo_ref[...] = mn
        vo_ref[...] = vn

    spec = pl.BlockSpec((tr, c), lambda i: (i, 0))
    return pl.pallas_call(
        body, name=name, grid=(r // tr,), in_specs=[spec] * 4, out_specs=[spec] * 4,
        out_shape=[jax.ShapeDtypeStruct((r, c), F32)] * 4, compiler_params=_cp(("arbitrary",), 32),
    )(w, g, m, v)


def _pack(vectors, rows):
    flat = jnp.concatenate([v.reshape(-1).astype(F32) for v in vectors])
    return jnp.pad(flat, (0, rows * LANES - flat.shape[0])).reshape(rows, LANES)


def _unpack(packed, shapes):
    flat = packed.reshape(-1)
    out, off = [], 0
    for shp in shapes:
        n = 1
        for t in shp:
            n *= t
        out.append(flat[off:off + n].reshape(shp))
        off += n
    return out


def _rows_for(shapes):
    n = sum(functools.reduce(lambda a, b: a * b, shp, 1) for shp in shapes)
    return -(-n // (8 * LANES)) * 8


def kernel(x, w_in, attn_sinks, short_conv_w, w_out, ln1_g, ln1_b, ffn_w_up, ffn_conv_w, ffn_w_down, ln2_g, ln2_b, loss_target, m_w_in, m_attn_sinks, m_short_conv_w, m_w_out, m_ln1_g, m_ln1_b, m_ffn_w_up, m_ffn_conv_w, m_ffn_w_down, m_ln2_g, m_ln2_b, v_w_in, v_attn_sinks, v_short_conv_w, v_w_out, v_ln1_g, v_ln1_b, v_ffn_w_up, v_ffn_conv_w, v_ffn_w_down, v_ln2_g, v_ln2_b):
    xs, tgt = x[0], loss_target[0]
    s, d = xs.shape
    chip = 2 * lax.axis_index("x") + lax.axis_index("y")

    w_big = dict(win_t=w_in[0], wout=w_out[0], wup=ffn_w_up[0], wdown=ffn_w_down[0])
    m_big = dict(win_t=m_w_in[0], wout=m_w_out[0], wup=m_ffn_w_up[0], wdown=m_ffn_w_down[0])
    v_big = dict(win_t=v_w_in[0], wout=v_w_out[0], wup=v_ffn_w_up[0], wdown=v_ffn_w_down[0])
    to_place = dict(w_big, win_t=w_in[0].T)
    geom = _geom({n: to_place[n].shape for n in BIG})
    pad8 = lambda a: jnp.pad(a[0], ((0, 5), (0, 0)))
    where = jnp.stack([chip, lax.axis_index("c")]).astype(jnp.int32)
    placed = {n: _place_shard(to_place[n], where[:1], geom[n][0], n) for n in BIG}
    full, (scw8, fcw8) = _allgather_weights(MIXER, placed, geom, [pad8(short_conv_w), pad8(ffn_conv_w)])
    send, recv, thru, token = _gather_start(FFN, placed, geom, scw8)
    a = _phase_mixer(xs, xs + token[0, 0], full["win_t"], full["wout"], scw8, attn_sinks, ln1_g, ln1_b)
    landed = _gather_forward(FFN, _gather_wait(FFN, send, recv, thru, geom, a["x1b"]), geom)
    full.update(zip(FFN, landed))
    f = _phase_ffn(a, tgt, full["wup"], full["wdown"], fcw8, ln2_g, ln2_b)

    def chip_sums_of(names, grads):
        from_sibling = _sibling_exchange(names, grads, geom)
        return [_add_pairs(grads[n], from_sibling[m], geom[n][0], geom[n][1], where, n) for m, n in enumerate(names)]

    ffn_sums = chip_sums_of(FFN, f)
    send, recv, thru, token = _chip_exchange_start(FFN, ffn_sums, geom, f["st2"])
    (dproj, dz1b), g_mixer, g_small = _phase_rest(a, f, full["wup"], full["wout"], full["win_t"], scw8, attn_sinks,
                                                  ln1_g + token[0:1, 0:1])

    def finish(names, sums, from_chips):
        halves = [_add_four(sums[m], from_chips[m], geom[n][1], where, n) for m, n in enumerate(names)]
        shards = _sibling_assemble(names, halves, geom)
        grads = {n: shards[m].T if n == "win_t" else shards[m] for m, n in enumerate(names)}
        return {n: _adamw(w_big[n], grads[n], m_big[n], v_big[n], "adamw_" + n) for n in names}

    mixer_sums = chip_sums_of(MIXER, g_mixer)
    send2, recv2, thru2, token2 = _chip_exchange_start(MIXER, mixer_sums, geom, f["st2"])
    grad_x = _grad_x(dproj, dz1b, full["win_t"], after=token2)
    upd = finish(FFN, ffn_sums, _chip_exchange_wait(FFN, send, recv, thru, grad_x))
    upd.update(finish(MIXER, mixer_sums, _chip_exchange_wait(MIXER, send2, recv2, thru2, upd[FFN[0]][1])))

    small_names = ("ln1_g", "ln1_b", "ln2_g", "ln2_b", "sinks", "fcw", "scw")
    small_shapes = [g_small[n].shape for n in small_names]
    red = _allreduce_small(_pack([g_small["loss_sq"].reshape(1)] + [g_small[n] for n in small_names],
                                 _rows_for([(1,)] + small_shapes)))
    loss_sq, *gs = _unpack(red, [(1,)] + small_shapes)
    gs = dict(zip(small_names, gs))
    loss = (0.5 / d) * loss_sq[0]
    fw, sw = ffn_conv_w.shape[2], short_conv_w.shape[2]
    gs["fcw"] = lax.dynamic_slice_in_dim(gs["fcw"], chip * fw, fw, axis=1)
    gs["scw"] = lax.dynamic_slice_in_dim(gs["scw"], chip * sw, sw, axis=1)

    sm_w = dict(ln1_g=ln1_g[0], ln1_b=ln1_b[0], ln2_g=ln2_g[0], ln2_b=ln2_b[0], sinks=attn_sinks[0],
                fcw=ffn_conv_w[0], scw=short_conv_w[0])
    sm_m = dict(ln1_g=m_ln1_g[0], ln1_b=m_ln1_b[0], ln2_g=m_ln2_g[0], ln2_b=m_ln2_b[0], sinks=m_attn_sinks[0],
                fcw=m_ffn_conv_w[0], scw=m_short_conv_w[0])
    sm_v = dict(ln1_g=v_ln1_g[0], ln1_b=v_ln1_b[0], ln2_g=v_ln2_g[0], ln2_b=v_ln2_b[0], sinks=v_attn_sinks[0],
                fcw=v_ffn_conv_w[0], scw=v_short_conv_w[0])
    shapes = [sm_w[n].shape for n in small_names]
    rows = _rows_for(shapes)
    packed = [_pack([t[n] for n in small_names], rows) for t in (sm_w, gs, sm_m, sm_v)]
    sm_out = [dict(zip(small_names, _unpack(a, shapes))) for a in _adamw(*packed, "adamw_small")]

    def leaf(kind, name):
        if name in ("w_in", "w_out", "ffn_w_up", "ffn_w_down"):
            key = dict(w_in="win_t", w_out="wout", ffn_w_up="wup", ffn_w_down="wdown")[name]
            return upd[key][kind][None]
        key = dict(attn_sinks="sinks", short_conv_w="scw", ffn_conv_w="fcw").get(name, name)
        return sm_out[kind][key][None]

    order = ("w_in", "attn_sinks", "short_conv_w", "w_out", "ln1_g", "ln1_b", "ffn_w_up", "ffn_conv_w", "ffn_w_down",
             "ln2_g", "ln2_b")
    outs = [loss, grad_x[None]]
    for kind in range(4):
        outs += [leaf(kind, n) for n in order]
    return tuple(outs)
```

```python
import functools

import jax
import jax.numpy as jnp
from jax import lax
from jax.experimental import pallas as pl
from jax.experimental.pallas import tpu as pltpu

F32 = jnp.float32
BF16 = jnp.bfloat16
MESH = pl.DeviceIdType.MESH
ANY = pl.BlockSpec(memory_space=pl.ANY)

HEAD_DIM = 64
N_Q_HEADS = 16
N_KV_HEADS = 2
ATTN_WIDTH = N_Q_HEADS * HEAD_DIM
KV_WIDTH = N_KV_HEADS * HEAD_DIM
BLOCK = 128
ROPE_THETA = 10000.0
LN_EPS = 1e-5
ALPHA = 2.0 ** 0.25
NEG_INF = -1e30
ADAM_LR, ADAM_B1, ADAM_B2, ADAM_EPS, ADAM_WD, ADAM_STEP = 0.001, 0.9, 0.999, 1e-08, 0.01, 10
N_CHIPS = 4
LANES = 128
MXU_DIM = 256
SLAB = 128


def _cp(sem, vmem_mb):
    return pltpu.CompilerParams(dimension_semantics=sem, vmem_limit_bytes=vmem_mb << 20)


def _matmul(a, b, *, mode, m, n, k, tm, tn, tk, out_dtype, name, vmem_mb, a_spec=None, b_spec=None,
            res=None, alpha=1.0, after=None):
    nj, ni, nk = n // tn, m // tm, k // tk
    assert nj * tn == n and ni * tm == m and nk * tk == k, (name, m, n, k, tm, tn, tk)
    if mode == "nn":
        dims = ((1,), (0,))
        a_spec = a_spec or pl.BlockSpec((tm, tk), lambda j, i, kk: (i, kk))
        b_spec = b_spec or pl.BlockSpec((tk, tn), lambda j, i, kk: (kk, j))
    elif mode == "nt":
        dims = ((1,), (1,))
        a_spec = a_spec or pl.BlockSpec((tm, tk), lambda j, i, kk: (i, kk))
        b_spec = b_spec or pl.BlockSpec((tn, tk), lambda j, i, kk: (j, kk))
    else:
        dims = ((0,), (0,))
        a_spec = a_spec or pl.BlockSpec((tk, tm), lambda j, i, kk: (kk, i))
        b_spec = b_spec or pl.BlockSpec((tk, tn), lambda j, i, kk: (kk, j))
    has_res = res is not None
    has_after = after is not None

    def body(*refs):
        refs = refs[1:] if has_after else refs
        a_ref, b_ref = refs[0], refs[1]
        res_ref = refs[2] if has_res else None
        o_ref = refs[2 + has_res]
        part = lax.dot_general(a_ref[...].astype(BF16), b_ref[...].astype(BF16), (dims, ((), ())),
                               preferred_element_type=F32)

        def finish(acc):
            if has_res:
                acc = acc + alpha * res_ref[...].astype(F32)
            o_ref[...] = acc.astype(o_ref.dtype)

        if nk == 1:
            finish(part)
        else:
            acc_ref = refs[3 + has_res]
            kk = pl.program_id(2)

            @pl.when(kk == 0)
            def _():
                acc_ref[...] = part

            @pl.when(kk > 0)
            def _():
                acc_ref[...] += part

            @pl.when(kk == nk - 1)
            def _():
                finish(acc_ref[...])

    in_specs = [a_spec, b_spec]
    args = [a, b]
    if has_res:
        in_specs.append(pl.BlockSpec((tm, tn), lambda j, i, kk: (i, j)))
        args.append(res)
    if has_after:
        in_specs.insert(0, pl.BlockSpec(after.shape, lambda j, i, kk: (0, 0)))
        args.insert(0, after)
    return pl.pallas_call(
        body, name=name, grid=(nj, ni, nk), in_specs=in_specs,
        out_specs=pl.BlockSpec((tm, tn), lambda j, i, kk: (i, j)),
        out_shape=jax.ShapeDtypeStruct((m, n), out_dtype),
        scratch_shapes=[pltpu.VMEM((tm, tn), F32)] if nk > 1 else [],
        compiler_params=_cp(("arbitrary", "arbitrary", "arbitrary"), vmem_mb),
    )(*args)


def _pick(total, want, mult):
    if total <= want:
        return total
    for t in range(want, 0, -1):
        if total % t == 0 and t % mult == 0:
            return t
    return total


def _rope_tables(s):
    half = HEAD_DIM // 2
    inv_freq = ROPE_THETA ** (-jnp.arange(half, dtype=F32) / half)
    ang = jnp.arange(s, dtype=F32)[:, None] * inv_freq[None, :]
    cos = jnp.tile(jnp.cos(ang), (1, LANES // half))
    sin = jnp.tile(jnp.concatenate([-jnp.sin(ang), jnp.sin(ang)], axis=1), (1, LANES // HEAD_DIM))
    return cos, sin


def _rope(x, cos, sin, lo):
    partner = jnp.where(lo, pltpu.roll(x, LANES - HEAD_DIM // 2, 1), pltpu.roll(x, HEAD_DIM // 2, 1))
    return x * cos + partner * sin


def _dot(a, b, dims):
    return lax.dot_general(a, b, (dims, ((), ())), preferred_element_type=F32)


NN, NT, TN = ((1,), (0,)), ((1,), (1,)), ((0,), (0,))


def _kv_variants(t, head_lo):
    r = pltpu.roll(t, HEAD_DIM, 1)
    zero = jnp.zeros_like(t)
    a = (jnp.where(head_lo, t, zero).astype(BF16), jnp.where(head_lo, r, zero).astype(BF16))
    b = (jnp.where(head_lo, zero, r).astype(BF16), jnp.where(head_lo, zero, t).astype(BF16))
    return a, b


PAIRS_PER_KV = N_Q_HEADS // 2 // N_KV_HEADS
STACK = PAIRS_PER_KV * BLOCK


def _stack_pairs(ref, j, fn):
    return jnp.concatenate([fn(ref[:, p * LANES:(p + 1) * LANES])
                            for p in range(j * PAIRS_PER_KV, (j + 1) * PAIRS_PER_KV)], axis=0)


def _sink_row(sink_ref, j, hh):
    col = lax.broadcasted_iota(jnp.int32, (1, STACK), 1)
    heads = [2 * p + hh for p in range(j * PAIRS_PER_KV, (j + 1) * PAIRS_PER_KV)]
    row = jnp.full((1, STACK), sink_ref[0, heads[-1]], F32)
    for t in range(PAIRS_PER_KV - 2, -1, -1):
        row = jnp.where(col < (t + 1) * BLOCK, sink_ref[0, heads[t]], row)
    return row


def _attn_exps(qp, ka, kb, valid, sink_a, sink_b):
    out = []
    for kk, sink in ((ka, sink_a), (kb, sink_b)):
        s = jnp.where(valid, _dot(kk, qp, NT), NEG_INF)
        mx = jnp.maximum(jnp.max(s, axis=0, keepdims=True), sink)
        out.append((jnp.exp(s - mx), jnp.exp(sink - mx)))
    return out


def _attn_common(i, q_ref, k_ref, v_ref, kp_ref, vp_ref, cos_ref, sin_ref, cosp_ref, sinp_ref):
    lane = lax.broadcasted_iota(jnp.int32, (1, LANES), 1)
    lo = (lane % HEAD_DIM) < (HEAD_DIM // 2)
    head_lo = lane < HEAD_DIM
    cos, sin = cos_ref[...], sin_ref[...]
    kc = _rope(k_ref[...].astype(F32), cos, sin, lo)
    kp = _rope(kp_ref[...].astype(F32), cosp_ref[...], sinp_ref[...], lo)
    kext = jnp.concatenate([kp, kc], axis=0)
    vext = jnp.concatenate([vp_ref[...].astype(F32), v_ref[...].astype(F32)], axis=0)
    ka, kb = _kv_variants(kext, head_lo)
    va, vb = _kv_variants(vext, head_lo)
    qi = lax.broadcasted_iota(jnp.int32, (1, STACK), 1) % BLOCK
    kj = lax.broadcasted_iota(jnp.int32, (2 * BLOCK, 1), 0)
    valid = (kj > qi) & (kj <= qi + BLOCK) & ((kj >= BLOCK) | (i > 0))
    cos4 = jnp.concatenate([cos] * PAIRS_PER_KV, axis=0)
    sin4 = jnp.concatenate([sin] * PAIRS_PER_KV, axis=0)
    return lo, head_lo, cos, sin, cos4, sin4, ka, kb, va, vb, valid


def _attn_fwd(proj, sinks, cos, sin, s):
    nb = s // BLOCK
    kcol, vcol = ATTN_WIDTH // LANES, ATTN_WIDTH // LANES + 1

    def body(q_ref, k_ref, v_ref, kp_ref, vp_ref, cos_ref, sin_ref, cosp_ref, sinp_ref, sink_ref, o_ref):
        i = pl.program_id(0)
        lo, head_lo, cs, sn, cs4, sn4, ka, kb, va, vb, valid = _attn_common(
            i, q_ref, k_ref, v_ref, kp_ref, vp_ref, cos_ref, sin_ref, cosp_ref, sinp_ref)
        row = lax.broadcasted_iota(jnp.int32, (16, 1), 0)
        one = jnp.ones((), BF16)
        for j in range(N_KV_HEADS):
            q4 = _stack_pairs(q_ref, j, lambda t: t.astype(F32))
            qp = (_rope(q4, cs4, sn4, lo) * HEAD_DIM ** -0.5).astype(BF16)
            exps = _attn_exps(qp, ka[j], kb[j], valid, _sink_row(sink_ref, j, 0), _sink_row(sink_ref, j, 1))
            outs = []
            for (e, es), vv, mine in zip(exps, (va[j], vb[j]), (head_lo, ~head_lo)):
                ee = jnp.concatenate([e.astype(BF16), jnp.where(row == 0, es, 0.0).astype(BF16)], axis=0)
                tail = jnp.where((row == 0) & ~mine, one, jnp.zeros((), BF16))
                vx = jnp.concatenate([jnp.where(mine, vv, one), tail], axis=0)
                un = _dot(ee, vx, TN)
                outs.append(un / pltpu.roll(un, HEAD_DIM, 1))
            o = jnp.where(head_lo, outs[0], outs[1]).astype(BF16)
            for t in range(PAIRS_PER_KV):
                p = j * PAIRS_PER_KV + t
                o_ref[:, p * LANES:(p + 1) * LANES] = o[t * BLOCK:(t + 1) * BLOCK]

    prev = lambda i: (jnp.maximum(i - 1, 0), 0)
    return pl.pallas_call(
        body, name="attn_fwd", grid=(nb,),
        in_specs=[pl.BlockSpec((BLOCK, ATTN_WIDTH), lambda i: (i, 0)),
                  pl.BlockSpec((BLOCK, LANES), lambda i: (i, kcol)),
                  pl.BlockSpec((BLOCK, LANES), lambda i: (i, vcol)),
                  pl.BlockSpec((BLOCK, LANES), lambda i: (jnp.maximum(i - 1, 0), kcol)),
                  pl.BlockSpec((BLOCK, LANES), lambda i: (jnp.maximum(i - 1, 0), vcol)),
                  pl.BlockSpec((BLOCK, LANES), lambda i: (i, 0)),
                  pl.BlockSpec((BLOCK, LANES), lambda i: (i, 0)),
                  pl.BlockSpec((BLOCK, LANES), prev),
                  pl.BlockSpec((BLOCK, LANES), prev),
                  pl.BlockSpec(memory_space=pltpu.SMEM)],
        out_specs=pl.BlockSpec((BLOCK, ATTN_WIDTH), lambda i: (i, 0)),
        out_shape=jax.ShapeDtypeStruct((s, ATTN_WIDTH), BF16),
        compiler_params=_cp(("arbitrary",), 32),
    )(proj, proj, proj, proj, proj, cos, sin, cos, sin, sinks)


def _attn_bwd(proj, dmix, sinks, cos, sin, s):
    nb = s // BLOCK
    kcol, vcol = ATTN_WIDTH // LANES, ATTN_WIDTH // LANES + 1
    pairs_per_kv = N_Q_HEADS // 2 // N_KV_HEADS

    def body(q_ref, k_ref, v_ref, kp_ref, vp_ref, cos_ref, sin_ref, cosp_ref, sinp_ref, sink_ref, do_ref,
             dq_ref, dk_ref, dv_ref, dsink_ref, ck_ref, cv_ref):
        g = pl.program_id(0)
        i = nb - 1 - g

        @pl.when(g == 0)
        def _():
            ck_ref[...] = jnp.zeros_like(ck_ref)
            cv_ref[...] = jnp.zeros_like(cv_ref)
            dsink_ref[...] = jnp.zeros_like(dsink_ref)

        lo, head_lo, cs, sn, cs4, sn4, ka, kb, va, vb, valid = _attn_common(
            i, q_ref, k_ref, v_ref, kp_ref, vp_ref, cos_ref, sin_ref, cosp_ref, sinp_ref)
        lane = lax.broadcasted_iota(jnp.int32, (1, LANES), 1)
        dk_j, dv_j = [], []
        dsink = jnp.zeros((1, LANES), F32)
        for j in range(N_KV_HEADS):
            q4 = _stack_pairs(q_ref, j, lambda t: t.astype(F32))
            qp = (_rope(q4, cs4, sn4, lo) * HEAD_DIM ** -0.5).astype(BF16)
            exps = _attn_exps(qp, ka[j], kb[j], valid, _sink_row(sink_ref, j, 0), _sink_row(sink_ref, j, 1))
            do = _stack_pairs(do_ref, j, lambda t: t)
            dq_r = jnp.zeros((STACK, LANES), F32)
            dkc, dvc = [], []
            for hh, ((e, es), kk, vv) in enumerate(zip(exps, (ka[j], kb[j]), (va[j], vb[j]))):
                inv = 1.0 / (jnp.sum(e, axis=0, keepdims=True) + es)
                pr = e * inv
                dp = _dot(vv, do, NT)
                delta = jnp.sum(pr * dp, axis=0, keepdims=True)
                ds = (pr * (dp - delta)).astype(BF16)
                psd = es * inv * delta
                for t in range(PAIRS_PER_KV):
                    head = 2 * (j * PAIRS_PER_KV + t) + hh
                    dsink = dsink + jnp.where(
                        lane == head, -jnp.sum(psd[:, t * BLOCK:(t + 1) * BLOCK], axis=1, keepdims=True), 0.0)
                dq_r = dq_r + _dot(ds, kk, TN)
                dkc.append(_dot(ds, qp, NN))
                dvc.append(_dot(pr.astype(BF16), do, NN))
            dk_j.append(jnp.where(head_lo, dkc[0], dkc[1]))
            dv_j.append(jnp.where(head_lo, dvc[0], dvc[1]))
            dq = _rope(dq_r * HEAD_DIM ** -0.5, cs4, -sn4, lo).astype(BF16)
            for t in range(PAIRS_PER_KV):
                p = j * PAIRS_PER_KV + t
                dq_ref[:, p * LANES:(p + 1) * LANES] = dq[t * BLOCK:(t + 1) * BLOCK]
        tot_k = [t + pltpu.roll(t, HEAD_DIM, 1) for t in dk_j]
        tot_v = [t + pltpu.roll(t, HEAD_DIM, 1) for t in dv_j]
        dkext = jnp.where(head_lo, tot_k[0], tot_k[1])
        dvext = jnp.where(head_lo, tot_v[0], tot_v[1])
        dk_r = dkext[BLOCK:] + ck_ref[...]
        dk_ref[...] = _rope(dk_r, cs, -sn, lo).astype(BF16)
        dv_ref[...] = (dvext[BLOCK:] + cv_ref[...]).astype(BF16)
        ck_ref[...] = dkext[:BLOCK]
        cv_ref[...] = dvext[:BLOCK]
        dsink_ref[0:1, :] += dsink

    cur = lambda col: (lambda g: (nb - 1 - g, col))
    prv = lambda col: (lambda g: (jnp.maximum(nb - 2 - g, 0), col))
    blk = lambda w, f: pl.BlockSpec((BLOCK, w), f)
    return pl.pallas_call(
        body, name="attn_bwd", grid=(nb,),
        in_specs=[blk(ATTN_WIDTH, cur(0)), blk(LANES, cur(kcol)), blk(LANES, cur(vcol)),
                  blk(LANES, prv(kcol)), blk(LANES, prv(vcol)),
                  blk(LANES, cur(0)), blk(LANES, cur(0)), blk(LANES, prv(0)), blk(LANES, prv(0)),
                  pl.BlockSpec(memory_space=pltpu.SMEM),
                  blk(ATTN_WIDTH, cur(0))],
        out_specs=[blk(ATTN_WIDTH, cur(0)), blk(LANES, cur(0)), blk(LANES, cur(0)),
                   pl.BlockSpec((8, LANES), lambda g: (0, 0))],
        out_shape=[jax.ShapeDtypeStruct((s, ATTN_WIDTH), BF16), jax.ShapeDtypeStruct((s, LANES), BF16),
                   jax.ShapeDtypeStruct((s, LANES), BF16), jax.ShapeDtypeStruct((8, LANES), F32)],
        scratch_shapes=[pltpu.VMEM((BLOCK, LANES), F32), pltpu.VMEM((BLOCK, LANES), F32)],
        compiler_params=_cp(("arbitrary",), 32),
    )(proj, proj, proj, proj, proj, cos, sin, cos, sin, sinks, dmix)


def _causal_conv(x, prev8, w):
    row = lax.broadcasted_iota(jnp.int32, (8, 1), 0)
    r1, r2 = pltpu.roll(x, 1, 0), pltpu.roll(x, 2, 0)
    s1 = jnp.concatenate([jnp.where(row == 0, prev8[7:8], r1[:8]), r1[8:]], axis=0)
    s2 = jnp.concatenate([jnp.where(row == 0, prev8[6:7], jnp.where(row == 1, prev8[7:8], r2[:8])), r2[8:]], axis=0)
    return w[0:1] * s2 + w[1:2] * s1 + w[2:3] * x


def _conv_bwd(dy, x, w, next8):
    r = x.shape[0]
    row = lax.broadcasted_iota(jnp.int32, (8, 1), 0)
    r1, r2 = pltpu.roll(dy, r - 1, 0), pltpu.roll(dy, r - 2, 0)
    n1 = jnp.concatenate([r1[:r - 8], jnp.where(row == 7, next8[0:1], r1[r - 8:])], axis=0)
    n2 = jnp.concatenate([r2[:r - 8], jnp.where(row == 6, next8[0:1], jnp.where(row == 7, next8[1:2], r2[r - 8:]))],
                         axis=0)
    dx = w[2:3] * dy + w[1:2] * n1 + w[0:1] * n2
    dws = [jnp.sum(t * x, axis=0, keepdims=True) for t in (n2, n1, dy)]
    return dx, dws


CONV_COLS = 256


def _convmix_cols(d):
    conv_w = d - ATTN_WIDTH
    base = (ATTN_WIDTH + 2 * KV_WIDTH) // CONV_COLS
    step = conv_w // CONV_COLS
    return base, base + step, base + 2 * step, step


def _convmix_fwd(proj, scw8, s, d):
    gb0, gc0, h0, ncb = _convmix_cols(d)
    tr = _pick(s, 1024, 16)
    ni = s // tr

    def body(gb_ref, gc_ref, h_ref, w_ref, o_ref, carry_ref):
        @pl.when(pl.program_id(1) == 0)
        def _():
            carry_ref[...] = jnp.zeros_like(carry_ref)

        gch = gc_ref[...].astype(F32) * h_ref[...].astype(F32)
        cc = _causal_conv(gch, carry_ref[...], w_ref[...])
        o_ref[...] = (gb_ref[...].astype(F32) * cc).astype(BF16)
        carry_ref[...] = gch[tr - 8:]

    spec = lambda c0: pl.BlockSpec((tr, CONV_COLS), lambda j, i: (i, c0 + j))
    return pl.pallas_call(
        body, name="convmix_fwd", grid=(ncb, ni),
        in_specs=[spec(gb0), spec(gc0), spec(h0), pl.BlockSpec((8, CONV_COLS), lambda j, i: (0, j))],
        out_specs=pl.BlockSpec((tr, CONV_COLS), lambda j, i: (i, j)),
        out_shape=jax.ShapeDtypeStruct((s, d - ATTN_WIDTH), BF16),
        scratch_shapes=[pltpu.VMEM((8, CONV_COLS), F32)],
        compiler_params=_cp(("arbitrary", "arbitrary"), 32),
    )(proj, proj, proj, scw8)


def _convmix_bwd(proj, dmix, scw8, s, d):
    gb0, gc0, h0, ncb = _convmix_cols(d)
    tr = _pick(s, 1024, 16)
    ni = s // tr
    dc0 = ATTN_WIDTH // CONV_COLS

    def body(dc_ref, gb_ref, gc_ref, h_ref, gcp_ref, hp_ref, w_ref, d3_ref, dw_ref, nxt_ref):
        g = pl.program_id(1)
        i = ni - 1 - g

        @pl.when(g == 0)
        def _():
            nxt_ref[...] = jnp.zeros_like(nxt_ref)
            dw_ref[...] = jnp.zeros_like(dw_ref)

        w = w_ref[...]
        gb, gc, h = gb_ref[...].astype(F32), gc_ref[...].astype(F32), h_ref[...].astype(F32)
        gch = gc * h
        prev8 = (gcp_ref[...].astype(F32) * hp_ref[...].astype(F32))[8:16] * (i > 0).astype(F32)
        cc = _causal_conv(gch, prev8, w)
        dc = dc_ref[...].astype(F32)
        dcc = dc * gb
        dgch, dws = _conv_bwd(dcc, gch, w, nxt_ref[...])
        d3_ref[0] = (dc * cc).astype(BF16)
        d3_ref[1] = (dgch * h).astype(BF16)
        d3_ref[2] = (dgch * gc).astype(BF16)
        for t in range(3):
            dw_ref[t:t + 1, :] += dws[t]
        nxt_ref[...] = dcc[0:8]

    cur = lambda c0: pl.BlockSpec((tr, CONV_COLS), lambda j, g: (ni - 1 - g, c0 + j))
    prv = lambda c0: pl.BlockSpec((16, CONV_COLS), lambda j, g: (jnp.maximum((ni - 1 - g) * (tr // 16) - 1, 0), c0 + j))
    return pl.pallas_call(
        body, name="convmix_bwd", grid=(ncb, ni),
        in_specs=[cur(dc0), cur(gb0), cur(gc0), cur(h0), prv(gc0), prv(h0),
                  pl.BlockSpec((8, CONV_COLS), lambda j, g: (0, j))],
        out_specs=[pl.BlockSpec((3, tr, CONV_COLS), lambda j, g: (0, ni - 1 - g, j)),
                   pl.BlockSpec((8, CONV_COLS), lambda j, g: (0, j))],
        out_shape=[jax.ShapeDtypeStruct((3, s, d - ATTN_WIDTH), BF16), jax.ShapeDtypeStruct((8, d - ATTN_WIDTH), F32)],
        scratch_shapes=[pltpu.VMEM((8, CONV_COLS), F32)],
        compiler_params=_cp(("arbitrary", "arbitrary"), 32),
    )(dmix, proj, proj, proj, proj, proj, scw8)


def _ln_fwd(z):
    mu = jnp.mean(z, axis=-1, keepdims=True)
    zc = z - mu
    var = jnp.mean(zc * zc, axis=-1, keepdims=True)
    rstd = lax.rsqrt(var + LN_EPS)
    return zc * rstd, rstd


def _ln_bwd(dout, xh, rstd, g):
    dxh = dout * g
    c1 = jnp.mean(dxh, axis=-1, keepdims=True)
    c2 = jnp.mean(dxh * xh, axis=-1, keepdims=True)
    dz = rstd * (dxh - c1 - xh * c2)
    return dz, jnp.sum(dout * xh, axis=0, keepdims=True), jnp.sum(dout, axis=0, keepdims=True)


def _outproj_ln1(attn, conv, wout, x, g1, b1, s, d):
    tm = _pick(s, 256, 16)
    ka = attn.shape[1]

    def body(a_ref, c_ref, wt_ref, wb_ref, x_ref, g_ref, b_ref, x1_ref, x1b_ref, xh_ref, rs_ref):
        y = _dot(a_ref[...], wt_ref[...], NN) + _dot(c_ref[...], wb_ref[...], NN)
        xh, rstd = _ln_fwd(ALPHA * x_ref[...] + y)
        x1 = xh * g_ref[...] + b_ref[...]
        x1_ref[...] = x1
        x1b_ref[...] = x1.astype(BF16)
        xh_ref[...] = xh.astype(BF16)
        rs_ref[...] = rstd

    row = lambda w: pl.BlockSpec((tm, w), lambda i: (i, 0))
    vec = pl.BlockSpec((1, d), lambda i: (0, 0))
    return pl.pallas_call(
        body, name="outproj_ln1", grid=(s // tm,),
        in_specs=[row(ka), row(d - ka), pl.BlockSpec((ka, d), lambda i: (0, 0)),
                  pl.BlockSpec((d - ka, d), lambda i: (ka // (d - ka), 0)), row(d), vec, vec],
        out_specs=[row(d), row(d), row(d), row(1)],
        out_shape=[jax.ShapeDtypeStruct((s, d), F32), jax.ShapeDtypeStruct((s, d), BF16),
                   jax.ShapeDtypeStruct((s, d), BF16), jax.ShapeDtypeStruct((s, 1), F32)],
        compiler_params=_cp(("arbitrary",), 48),
    )(attn, conv, wout, wout, x, g1, b1)


def _ffn_up(x1b, wup, fcw8, s, d, dff):
    tm = _pick(s, 1024, 16)
    tn = _pick(dff, 512, LANES)
    nj, ni = dff // tn, s // tm

    def body(x_ref, wa_ref, wg_ref, ca_ref, cg_ref, u_ref, y_ref, h_ref, carry_ref):
        @pl.when(pl.program_id(1) == 0)
        def _():
            carry_ref[...] = jnp.zeros_like(carry_ref)

        xa = x_ref[...]
        ys = []
        for part, (w_ref, c_ref) in enumerate(((wa_ref, ca_ref), (wg_ref, cg_ref))):
            ub = _dot(xa, w_ref[...], NN).astype(BF16)
            u_ref[part] = ub
            u = ub.astype(F32)
            y = _causal_conv(u, carry_ref[part], c_ref[...])
            carry_ref[part] = u[tm - 8:]
            yb = y.astype(BF16)
            y_ref[part] = yb
            ys.append(yb.astype(F32))
        a2, g2 = ys
        sig = 1.0 / (1.0 + jnp.exp(-a2))
        h_ref[...] = (a2 * sig * g2).astype(BF16)

    return pl.pallas_call(
        body, name="ffn_up", grid=(nj, ni),
        in_specs=[pl.BlockSpec((tm, d), lambda j, i: (i, 0)),
                  pl.BlockSpec((d, tn), lambda j, i: (0, j)),
                  pl.BlockSpec((d, tn), lambda j, i: (0, j + nj)),
                  pl.BlockSpec((8, tn), lambda j, i: (0, j)),
                  pl.BlockSpec((8, tn), lambda j, i: (0, j + nj))],
        out_specs=[pl.BlockSpec((2, tm, tn), lambda j, i: (0, i, j)),
                   pl.BlockSpec((2, tm, tn), lambda j, i: (0, i, j)),
                   pl.BlockSpec((tm, tn), lambda j, i: (i, j))],
        out_shape=[jax.ShapeDtypeStruct((2, s, dff), BF16), jax.ShapeDtypeStruct((2, s, dff), BF16),
                   jax.ShapeDtypeStruct((s, dff), BF16)],
        scratch_shapes=[pltpu.VMEM((2, 8, tn), F32)],
        compiler_params=_cp(("arbitrary", "arbitrary"), 56),
    )(x1b, wup, wup, fcw8, fcw8)


def _ffn_mid_bwd(dz2b, wdown, u3, y3, fcw8, s, d, dff):
    tm = _pick(s, 1024, 16)
    tn = _pick(dff, 512, LANES)
    nj, ni = dff // tn, s // tm

    def body(dz_ref, wd_ref, u_ref, y_ref, ca_ref, cg_ref, du_ref, dw_ref, nxt_ref):
        @pl.when(pl.program_id(1) == 0)
        def _():
            nxt_ref[...] = jnp.zeros_like(nxt_ref)
            dw_ref[...] = jnp.zeros_like(dw_ref)

        a2, g2 = y_ref[0].astype(F32), y_ref[1].astype(F32)
        sig = 1.0 / (1.0 + jnp.exp(-a2))
        silu = a2 * sig
        dhv = _dot(dz_ref[...], wd_ref[...], NT)
        dys = (dhv * g2 * (sig * (1.0 + a2 * (1.0 - sig))), dhv * silu)
        for part, (c_ref, dy) in enumerate(zip((ca_ref, cg_ref), dys)):
            dx, dws = _conv_bwd(dy, u_ref[part].astype(F32), c_ref[...], nxt_ref[part])
            du_ref[part] = dx.astype(BF16)
            for t in range(3):
                dw_ref[part, t:t + 1, :] += dws[t]
            nxt_ref[part] = dy[0:8]

    return pl.pallas_call(
        body, name="ffn_mid_bwd", grid=(nj, ni),
        in_specs=[pl.BlockSpec((tm, d), lambda j, g: (ni - 1 - g, 0)),
                  pl.BlockSpec((tn, d), lambda j, g: (j, 0)),
                  pl.BlockSpec((2, tm, tn), lambda j, g: (0, ni - 1 - g, j)),
                  pl.BlockSpec((2, tm, tn), lambda j, g: (0, ni - 1 - g, j)),
                  pl.BlockSpec((8, tn), lambda j, g: (0, j)),
                  pl.BlockSpec((8, tn), lambda j, g: (0, j + nj))],
        out_specs=[pl.BlockSpec((2, tm, tn), lambda j, g: (0, ni - 1 - g, j)),
                   pl.BlockSpec((2, 8, tn), lambda j, g: (0, 0, j))],
        out_shape=[jax.ShapeDtypeStruct((2, s, dff), BF16), jax.ShapeDtypeStruct((2, 8, dff), F32)],
        scratch_shapes=[pltpu.VMEM((2, 8, tn), F32)],
        compiler_params=_cp(("arbitrary", "arbitrary"), 56),
    )(dz2b, wdown, u3, y3, fcw8, fcw8)


def _ffn_down_loss(hmid, wdown, x1, target, g2, b2, s, d, dff):
    tm = _pick(s, 512, SLAB)
    tk = _pick(dff, 2816, MXU_DIM)
    ni, nk = s // tm, dff // tk
    slab = min(SLAB, tm)

    def body(h_ref, w_ref, x1_ref, t_ref, g_ref, b_ref, dzb_ref, st_ref, acc_ref):
        i, kk = pl.program_id(0), pl.program_id(1)

        @pl.when((i == 0) & (kk == 0))
        def _():
            st_ref[...] = jnp.zeros_like(st_ref)

        part = _dot(h_ref[...], w_ref[...], NN)

        @pl.when(kk == 0)
        def _():
            acc_ref[...] = part

        @pl.when(kk > 0)
        def _():
            acc_ref[...] += part

        @pl.when(kk == nk - 1)
        def _():
            g, b = g_ref[...], b_ref[...]

            def one(sl, carry):
                rows = pl.ds(pl.multiple_of(sl * slab, slab), slab)
                xh, rstd = _ln_fwd(ALPHA * x1_ref[rows, :] + acc_ref[rows, :])
                diff = xh * g + b - t_ref[rows, :]
                sq = jnp.sum(jnp.sum(diff * diff, axis=1, keepdims=True), axis=0, keepdims=True)
                dz, dg, db = _ln_bwd(diff * (1.0 / d), xh, rstd, g)
                dzb_ref[rows, :] = dz.astype(BF16)
                st_ref[0:1, :] += dg
                st_ref[1:2, :] += db
                st_ref[2:3, :] += sq
                return carry

            lax.fori_loop(0, tm // slab, one, 0)

    row = pl.BlockSpec((tm, d), lambda i, kk: (i, 0))
    row1 = pl.BlockSpec((tm, d), lambda i, kk: (i, 0), pipeline_mode=pl.Buffered(1))
    vec = pl.BlockSpec((1, d), lambda i, kk: (0, 0))
    return pl.pallas_call(
        body, name="ffn_down_loss", grid=(ni, nk),
        in_specs=[pl.BlockSpec((tm, tk), lambda i, kk: (i, kk)), pl.BlockSpec((tk, d), lambda i, kk: (kk, 0)),
                  row1, row1, vec, vec],
        out_specs=[row, pl.BlockSpec((8, d), lambda i, kk: (0, 0))],
        out_shape=[jax.ShapeDtypeStruct((s, d), BF16), jax.ShapeDtypeStruct((8, d), F32)],
        scratch_shapes=[pltpu.VMEM((tm, d), F32)],
        compiler_params=_cp(("arbitrary", "arbitrary"), 56),
    )(hmid, wdown, x1, target, g2, b2)


def _ffn_dx_ln1_bwd(du3, wup, dz2b, xh1, rstd1, g1, s, d, dff):
    tm = _pick(s, 512, SLAB)
    tk = _pick(dff, 2816, MXU_DIM)
    nkh = dff // tk
    ni, nk = s // tm, 2 * nkh
    slab = min(SLAB, tm)

    def body(a_ref, w_ref, dz2_ref, xh_ref, rs_ref, g_ref, dzb_ref, st_ref, acc_ref):
        i, kk = pl.program_id(0), pl.program_id(1)

        @pl.when((i == 0) & (kk == 0))
        def _():
            st_ref[...] = jnp.zeros_like(st_ref)

        part = _dot(a_ref[...], w_ref[...], NT)

        @pl.when(kk == 0)
        def _():
            acc_ref[...] = part

        @pl.when(kk > 0)
        def _():
            acc_ref[...] += part

        @pl.when(kk == nk - 1)
        def _():
            g = g_ref[...]

            def one(sl, carry):
                rows = pl.ds(pl.multiple_of(sl * slab, slab), slab)
                dx1 = ALPHA * dz2_ref[rows, :].astype(F32) + acc_ref[rows, :]
                dz, dg, db = _ln_bwd(dx1, xh_ref[rows, :].astype(F32), rs_ref[rows, :], g)
                dzb_ref[rows, :] = dz.astype(BF16)
                st_ref[0:1, :] += dg
                st_ref[1:2, :] += db
                return carry

            lax.fori_loop(0, tm // slab, one, 0)

    row = pl.BlockSpec((tm, d), lambda i, kk: (i, 0))
    row1 = pl.BlockSpec((tm, d), lambda i, kk: (i, 0), pipeline_mode=pl.Buffered(1))
    return pl.pallas_call(
        body, name="ffn_dx_ln1_bwd", grid=(ni, nk),
        in_specs=[pl.BlockSpec((None, tm, tk), lambda i, kk: (kk // nkh, i, kk % nkh)),
                  pl.BlockSpec((d, tk), lambda i, kk: (0, kk)),
                  row1, row1, pl.BlockSpec((tm, 1), lambda i, kk: (i, 0)), pl.BlockSpec((1, d), lambda i, kk: (0, 0))],
        out_specs=[row, pl.BlockSpec((8, d), lambda i, kk: (0, 0))],
        out_shape=[jax.ShapeDtypeStruct((s, d), BF16), jax.ShapeDtypeStruct((8, d), F32)],
        scratch_shapes=[pltpu.VMEM((tm, d), F32)],
        compiler_params=_cp(("arbitrary", "arbitrary"), 56),
    )(du3, wup, dz2b, xh1, rstd1, g1)


def _phase_mixer(x, win_t, wout, scw8, sinks, ln1_g, ln1_b, after=None):
    s, d = x.shape
    n_in = win_t.shape[0]
    cos, sin = _rope_tables(s)
    proj = _matmul(x, win_t, mode="nt", m=s, n=n_in, k=d, tm=_pick(s, 512, 16), tn=n_in, tk=d, out_dtype=BF16,
                   name="in_proj", vmem_mb=52, after=after,
                   b_spec=pl.BlockSpec((n_in, d), lambda j, i, kk: (0, 0), pipeline_mode=pl.Buffered(1)))
    attn = _attn_fwd(proj, sinks, cos, sin, s)
    conv = _convmix_fwd(proj, scw8, s, d)
    x1, x1b, xh1, rstd1 = _outproj_ln1(attn, conv, wout, x, ln1_g, ln1_b, s, d)
    return dict(x=x, cos=cos, sin=sin, proj=proj, attn=attn, conv=conv, x1=x1, x1b=x1b, xh1=xh1, rstd1=rstd1)


def _phase_ffn(a, target, wup, wdown, fcw8, ln2_g, ln2_b):
    x1, x1b = a["x1"], a["x1b"]
    s, d = x1.shape
    dff = wdown.shape[0]
    u3, y3, hmid = _ffn_up(x1b, wup, fcw8, s, d, dff)
    dz2b, st2 = _ffn_down_loss(hmid, wdown, x1, target, ln2_g, ln2_b, s, d, dff)

    ts = _pick(s, 2048, 16)
    g_wdown = _matmul(hmid, dz2b, mode="tn", m=dff, n=d, k=s, tm=_pick(dff, 2816, MXU_DIM), tn=_pick(d, 512, MXU_DIM),
                      tk=ts, out_dtype=BF16, name="grad_w_down", vmem_mb=56)
    du3, dfcw = _ffn_mid_bwd(dz2b, wdown, u3, y3, fcw8, s, d, dff)
    tnu = _pick(dff, 2816, MXU_DIM)
    njh = dff // tnu
    g_wup = _matmul(x1b, du3, mode="tn", m=d, n=2 * dff, k=s, tm=_pick(d, 512, LANES), tn=tnu, tk=ts, out_dtype=BF16,
                    name="grad_w_up", vmem_mb=56,
                    b_spec=pl.BlockSpec((None, ts, tnu), lambda j, i, kk: (j // njh, kk, j % njh)))
    return dict(du3=du3, dz2b=dz2b, st2=st2, dfcw=dfcw, wdown=g_wdown, wup=g_wup)


def _phase_rest(a, f, wup, wout, win_t, scw8, sinks, ln1_g, between=None):
    xb, cos, sin, proj, attn, conv = a["x"], a["cos"], a["sin"], a["proj"], a["attn"], a["conv"]
    du3, dz2b, st2, dfcw = f["du3"], f["dz2b"], f["st2"], f["dfcw"]
    s, d = a["x1"].shape
    dff = wup.shape[1] // 2
    n_in = win_t.shape[0]
    ts = _pick(s, 2048, 16)
    dz1b, st1 = _ffn_dx_ln1_bwd(du3, wup, dz2b, a["xh1"], a["rstd1"], ln1_g, s, d, dff)
    after = between(dz1b) if between is not None else None

    mix = jnp.concatenate([attn, conv], axis=1)
    g_wout = _matmul(mix, dz1b, mode="tn", m=d, n=d, k=s, tm=_pick(d, 1024, LANES), tn=_pick(d, 1024, LANES), tk=ts,
                     out_dtype=BF16, name="grad_w_out", vmem_mb=48, after=after)
    dmix = _matmul(dz1b, wout, mode="nt", m=s, n=d, k=d, tm=_pick(s, 1024, 16), tn=_pick(d, 1024, LANES), tk=d,
                   out_dtype=BF16, name="out_dmix", vmem_mb=48)
    d3, dscw = _convmix_bwd(proj, dmix, scw8, s, d)
    dq, dk, dv, dsink = _attn_bwd(proj, dmix, sinks, cos, sin, s)
    dproj = jnp.concatenate([dq, dk, dv, d3[0], d3[1], d3[2]], axis=1)
    g_win_t = _matmul(dproj, xb, mode="tn", m=n_in, n=d, k=s, tm=_pick(n_in, 2176, LANES), tn=_pick(d, 512, LANES),
                      tk=ts, out_dtype=BF16, name="grad_w_in", vmem_mb=48)
    small = dict(loss_sq=st2[2, 0], ln2_g=st2[0], ln2_b=st2[1], ln1_g=st1[0], ln1_b=st1[1], sinks=dsink[0, :N_Q_HEADS],
                 fcw=jnp.concatenate([dfcw[0, :3], dfcw[1, :3]], axis=1), scw=dscw[:3])
    return (dproj, dz1b), dict(win_t=g_win_t, wout=g_wout), small


def _grad_x(dproj, dz1b, win_t, after=None):
    s, n_in = dproj.shape
    d = win_t.shape[1]
    return _matmul(dproj, win_t, mode="nn", m=s, n=d, k=n_in, tm=_pick(s, 512, 16), tn=_pick(d, 1024, LANES),
                   tk=n_in, out_dtype=F32, name="grad_x", vmem_mb=56, res=dz1b, alpha=ALPHA, after=after)


def _local_step(x, target, win_t, wout, wup, wdown, scw8, fcw8, sinks, ln1_g, ln1_b, ln2_g, ln2_b):
    a = _phase_mixer(x, win_t, wout, scw8, sinks, ln1_g, ln1_b)
    f = _phase_ffn(a, target, wup, wdown, fcw8, ln2_g, ln2_b)
    (dproj, dz1b), g, small = _phase_rest(a, f, wup, wout, win_t, scw8, sinks, ln1_g)
    return _grad_x(dproj, dz1b, win_t), dict(g, wup=f["wup"], wdown=f["wdown"]), small


MIXER = ("win_t", "wout")
FFN = ("wup", "wdown")
BIG = MIXER + FFN


def _geom(shard_shapes):
    out = {}
    for name in BIG:
        r, c = shard_shapes[name]
        out[name] = ("col" if name == "wup" else "row", (r, c), (r // 2, c))
    return out


def _full_shape(kind, shard):
    r, c = shard
    return (N_CHIPS * r, c) if kind == "row" else (r, N_CHIPS * c)


def _piece_of(ref, kind, shard, chip, half):
    r, c = shard
    if kind == "row":
        return ref.at[pl.ds(chip * r + half * (r // 2), r // 2), :]
    return ref.at[pl.ds(half * (r // 2), r // 2), pl.ds(chip * c, c)]


def _shard_piece(ref, shard, half):
    r, _ = shard
    return ref.at[pl.ds(half * (r // 2), r // 2), :]


def _me():
    return lax.axis_index("x"), lax.axis_index("y"), lax.axis_index("c")


def _other_chips(x, y):
    return [(1 - x, y), (x, 1 - y), (1 - x, 1 - y)]


def _remote(src, dst, send_sem, recv_sem, dev):
    return pltpu.make_async_remote_copy(src_ref=src, dst_ref=dst, send_sem=send_sem, recv_sem=recv_sem,
                                        device_id=dev, device_id_type=MESH)


def _place_shard(w, chip1, kind, name):
    r, c = w.shape
    tr = _rows_tile(r, c, 16)
    nt = r // tr

    def body(chip_ref, w_ref, o_ref):
        o_ref[...] = w_ref[...].astype(BF16)

    out_map = (lambda i, chip_ref: (chip_ref[0] * nt + i, 0)) if kind == "row" else (lambda i, chip_ref: (i, chip_ref[0]))
    return pl.pallas_call(
        body, name="place_" + name,
        grid_spec=pltpu.PrefetchScalarGridSpec(
            num_scalar_prefetch=1, grid=(nt,),
            in_specs=[pl.BlockSpec((tr, c), lambda i, chip_ref: (i, 0))],
            out_specs=pl.BlockSpec((tr, c), out_map)),
        out_shape=jax.ShapeDtypeStruct(_full_shape(kind, (r, c)), BF16),
        compiler_params=_cp(("arbitrary",), 32),
    )(chip1, w)


def _allgather_weights(names, placed, geom, small_shards):
    nb, ns = len(names), len(small_shards)
    small_w = [a.shape[1] for a in small_shards]

    def body(*refs):
        sm = refs[nb:nb + ns]
        full = refs[nb + ns:2 * nb + ns]
        smf = refs[2 * nb + ns:2 * nb + 2 * ns]
        send, recv, loc = refs[2 * nb + 2 * ns:]
        x, y, c = _me()
        chip = 2 * x + y
        sib = (x, y, 1 - c)
        others = _other_chips(x, y)
        locals_, sends = [], []
        for m, name in enumerate(names):
            kind, shard, _ = geom[name]
            mine = _piece_of(full[m], kind, shard, chip, c)
            for k, (qx, qy) in enumerate(others):
                cp = _remote(mine, mine, send.at[6 * m + k], recv.at[6 * m + k], (qx, qy, c))
                cp.start()
                sends.append(cp)
        for t in range(ns):
            cp = pltpu.make_async_copy(sm[t], smf[t].at[:, pl.ds(chip * small_w[t], small_w[t])], loc.at[t])
            cp.start()
            locals_.append(cp)
            for k, (qx, qy) in enumerate(others):
                cp = _remote(sm[t], smf[t].at[:, pl.ds(chip * small_w[t], small_w[t])],
                             send.at[6 * nb + 3 * t + k], recv.at[6 * nb + 3 * t + k], (qx, qy, c))
                cp.start()
                sends.append(cp)
        for m, name in enumerate(names):
            kind, shard, _ = geom[name]
            for k, (qx, qy) in enumerate(others):
                got = _piece_of(full[m], kind, shard, 2 * qx + qy, c)
                _remote(got, got, send.at[6 * m + k], recv.at[6 * m + k], (qx, qy, c)).wait_recv()
                cp = _remote(got, got, send.at[6 * m + 3 + k], recv.at[6 * m + 3 + k], sib)
                cp.start()
                sends.append(cp)
        for t in range(ns):
            for k, (qx, qy) in enumerate(others):
                got = smf[t].at[:, pl.ds((2 * qx + qy) * small_w[t], small_w[t])]
                _remote(got, got, send.at[6 * nb + 3 * t + k], recv.at[6 * nb + 3 * t + k], (qx, qy, c)).wait_recv()
        for m, name in enumerate(names):
            kind, shard, _ = geom[name]
            for k, (qx, qy) in enumerate(others):
                got = _piece_of(full[m], kind, shard, 2 * qx + qy, 1 - c)
                _remote(got, got, send.at[6 * m + 3 + k], recv.at[6 * m + 3 + k], sib).wait_recv()
        for cp in sends:
            cp.wait_send()
        for cp in locals_:
            cp.wait()

    nsem = 6 * nb + 3 * ns
    out_shape = [jax.ShapeDtypeStruct(placed[n].shape, BF16) for n in names]
    out_shape += [jax.ShapeDtypeStruct((8, N_CHIPS * w), F32) for w in small_w]
    outs = pl.pallas_call(
        body, name="allgather_weights", in_specs=[ANY] * (nb + ns), out_specs=[ANY] * (nb + ns), out_shape=out_shape,
        input_output_aliases={m: m for m in range(nb)},
        scratch_shapes=[pltpu.SemaphoreType.DMA((nsem,)), pltpu.SemaphoreType.DMA((nsem,)),
                        pltpu.SemaphoreType.DMA((ns,))],
    )(*[placed[n] for n in names], *small_shards)
    return dict(zip(names, outs[:nb])), list(outs[nb:])


def _sibling_exchange(names, grads, geom):
    nb = len(names)

    def body(*refs):
        g = refs[:nb]
        got = refs[nb:2 * nb]
        send, recv = refs[2 * nb:]
        x, y, c = _me()
        sib = (x, y, 1 - c)
        cps = []
        for m, name in enumerate(names):
            kind, shard, _ = geom[name]
            for r in range(N_CHIPS):
                cp = _remote(_piece_of(g[m], kind, shard, r, 1 - c), got[m].at[r],
                             send.at[N_CHIPS * m + r], recv.at[N_CHIPS * m + r], sib)
                cp.start()
                cps.append(cp)
        for cp in cps:
            cp.wait_recv()
        for cp in cps:
            cp.wait_send()

    return pl.pallas_call(
        body, name="grad_sibling_exchange_" + names[0], in_specs=[ANY] * nb, out_specs=[ANY] * nb,
        out_shape=[jax.ShapeDtypeStruct((N_CHIPS,) + geom[n][2], BF16) for n in names],
        scratch_shapes=[pltpu.SemaphoreType.DMA((N_CHIPS * nb,)), pltpu.SemaphoreType.DMA((N_CHIPS * nb,))],
    )(*[grads[n] for n in names])


def _sibling_assemble(names, shards, geom):
    nb = len(names)

    def body(*refs):
        full = refs[nb:2 * nb]
        send, recv = refs[2 * nb:]
        x, y, c = _me()
        sib = (x, y, 1 - c)
        cps = []
        for m, name in enumerate(names):
            mine = _shard_piece(full[m], geom[name][1], c)
            cp = _remote(mine, mine, send.at[m], recv.at[m], sib)
            cp.start()
            cps.append(cp)
        for m, name in enumerate(names):
            theirs = _shard_piece(full[m], geom[name][1], 1 - c)
            _remote(theirs, theirs, send.at[m], recv.at[m], sib).wait_recv()
        for cp in cps:
            cp.wait_send()

    return pl.pallas_call(
        body, name="grad_sibling_assemble_" + names[0], in_specs=[ANY] * nb, out_specs=[ANY] * nb,
        out_shape=[jax.ShapeDtypeStruct(geom[n][1], F32) for n in names],
        input_output_aliases={m: m for m in range(nb)},
        scratch_shapes=[pltpu.SemaphoreType.DMA((nb,)), pltpu.SemaphoreType.DMA((nb,))],
    )(*shards)


HBM = pl.BlockSpec(memory_space=pltpu.HBM)
SEM = pl.BlockSpec(memory_space=pltpu.SEMAPHORE)
EFFECT = pltpu.SideEffectType.DATAFLOW_SIDE_EFFECTING
TOKEN = jax.ShapeDtypeStruct((8, LANES), F32)


def _hbm(a):
    return pltpu.with_memory_space_constraint(a, pltpu.HBM)


def _gather_copies(names, full, geom, send, recv):
    x, y, c = _me()
    out = []
    for m, name in enumerate(names):
        kind, shard, _ = geom[name]
        mine = _piece_of(full[m], kind, shard, 2 * x + y, c)
        for k, (qx, qy) in enumerate(_other_chips(x, y)):
            theirs = _piece_of(full[m], kind, shard, 2 * qx + qy, c)
            out.append((_remote(mine, mine, send.at[3 * m + k], recv.at[3 * m + k], (qx, qy, c)),
                        _remote(theirs, theirs, send.at[3 * m + k], recv.at[3 * m + k], (qx, qy, c))))
    return out


def _gather_start(names, placed, geom, after):
    nb = len(names)

    def body(*refs):
        full = refs[:nb]
        send, recv = refs[nb + 1], refs[nb + 2]
        token = refs[2 * nb + 3]
        for cp, _ in _gather_copies(names, full, geom, send, recv):
            cp.start()
        token[...] = jnp.zeros_like(token)

    outs = pl.pallas_call(
        body, name="gather_start_" + names[0],
        out_shape=(pltpu.SemaphoreType.DMA((3 * nb,)), pltpu.SemaphoreType.DMA((3 * nb,)),
                   *[pltpu.HBM(placed[n].shape, BF16) for n in names], TOKEN),
        in_specs=[HBM] * nb + [ANY], out_specs=(SEM, SEM, *[HBM] * nb, pl.BlockSpec(memory_space=pltpu.VMEM)),
        input_output_aliases={m: 2 + m for m in range(nb)},
        compiler_params=pltpu.CompilerParams(has_side_effects=EFFECT),
    )(*[_hbm(placed[n]) for n in names], after)
    return outs[0], outs[1], list(outs[2:2 + nb]), outs[2 + nb]


def _gather_wait(names, send, recv, thru, geom, after):
    nb = len(names)

    def body(*refs):
        full = refs[:nb]
        for mine, theirs in _gather_copies(names, full, geom, refs[nb], refs[nb + 1]):
            mine.wait_send()
            theirs.wait_recv()

    return pl.pallas_call(
        body, name="gather_wait_" + names[0], out_shape=tuple(pltpu.HBM(t.shape, t.dtype) for t in thru),
        in_specs=[HBM] * nb + [SEM, SEM, ANY], out_specs=tuple([HBM] * nb),
        input_output_aliases={m: m for m in range(nb)},
        compiler_params=pltpu.CompilerParams(has_side_effects=EFFECT),
    )(*thru, send, recv, after)


def _gather_forward(names, full, geom):
    nb = len(names)

    def body(*refs):
        arr = refs[nb:2 * nb]
        send, recv = refs[2 * nb:]
        x, y, c = _me()
        sib = (x, y, 1 - c)
        cps = []
        for m, name in enumerate(names):
            kind, shard, _ = geom[name]
            for k, (qx, qy) in enumerate(_other_chips(x, y)):
                got = _piece_of(arr[m], kind, shard, 2 * qx + qy, c)
                cp = _remote(got, got, send.at[3 * m + k], recv.at[3 * m + k], sib)
                cp.start()
                cps.append(cp)
        for m, name in enumerate(names):
            kind, shard, _ = geom[name]
            for k, (qx, qy) in enumerate(_other_chips(x, y)):
                theirs = _piece_of(arr[m], kind, shard, 2 * qx + qy, 1 - c)
                _remote(theirs, theirs, send.at[3 * m + k], recv.at[3 * m + k], sib).wait_recv()
        for cp in cps:
            cp.wait_send()

    return pl.pallas_call(
        body, name="gather_forward_" + names[0], in_specs=[ANY] * nb, out_specs=[ANY] * nb,
        out_shape=[jax.ShapeDtypeStruct(a.shape, a.dtype) for a in full],
        input_output_aliases={m: m for m in range(nb)},
        scratch_shapes=[pltpu.SemaphoreType.DMA((3 * nb,)), pltpu.SemaphoreType.DMA((3 * nb,))],
    )(*full)


def _scatter_copies(nb, t, got, send, recv):
    x, y, c = _me()
    return [_remote(t[m].at[2 * qx + qy], got[m].at[k], send.at[3 * m + k], recv.at[3 * m + k], (qx, qy, c))
            for m in range(nb) for k, (qx, qy) in enumerate(_other_chips(x, y))]


def _chip_exchange_start(names, chip_sums, geom, after):
    nb = len(names)
    lands = [lax.empty((N_CHIPS - 1,) + geom[n][2], BF16) for n in names]

    def body(*refs):
        t, got = refs[:nb], refs[nb:2 * nb]
        send, recv = refs[2 * nb + 1], refs[2 * nb + 2]
        token = refs[4 * nb + 3]
        for cp in _scatter_copies(nb, t, got, send, recv):
            cp.start()
        token[...] = jnp.zeros_like(token)

    both = list(chip_sums) + lands
    outs = pl.pallas_call(
        body, name="grad_chip_start_" + names[0],
        out_shape=(pltpu.SemaphoreType.DMA((3 * nb,)), pltpu.SemaphoreType.DMA((3 * nb,)),
                   *[pltpu.HBM(a.shape, a.dtype) for a in both], TOKEN),
        in_specs=[HBM] * (2 * nb) + [ANY],
        out_specs=(SEM, SEM, *[HBM] * (2 * nb), pl.BlockSpec(memory_space=pltpu.VMEM)),
        input_output_aliases={m: 2 + m for m in range(2 * nb)},
        compiler_params=pltpu.CompilerParams(has_side_effects=EFFECT),
    )(*[_hbm(a) for a in both], after)
    return outs[0], outs[1], list(outs[2:2 + 2 * nb]), outs[2 + 2 * nb]


def _chip_exchange_wait(names, send, recv, thru, after):
    nb = len(names)

    def body(*refs):
        for cp in _scatter_copies(nb, refs[:nb], refs[nb:2 * nb], refs[2 * nb], refs[2 * nb + 1]):
            cp.wait_send()
            cp.wait_recv()

    outs = pl.pallas_call(
        body, name="grad_chip_wait_" + names[0], out_shape=tuple(pltpu.HBM(t.shape, t.dtype) for t in thru),
        in_specs=[HBM] * (2 * nb) + [SEM, SEM, ANY], out_specs=tuple([HBM] * (2 * nb)),
        input_output_aliases={m: m for m in range(2 * nb)},
        compiler_params=pltpu.CompilerParams(has_side_effects=EFFECT),
    )(*thru, send, recv, after)
    return list(outs[nb:])


def _sibling_copies(names, g, got, geom, send, recv):
    x, y, c = _me()
    out = []
    for m, name in enumerate(names):
        kind, shard, _ = geom[name]
        for r in range(N_CHIPS):
            out.append(_remote(_piece_of(g[m], kind, shard, r, 1 - c), got[m].at[r],
                               send.at[N_CHIPS * m + r], recv.at[N_CHIPS * m + r], (x, y, 1 - c)))
    return out


def _sibling_exchange_start(names, grads, geom, after):
    nb = len(names)
    lands = [lax.empty((N_CHIPS,) + geom[n][2], BF16) for n in names]

    def body(*refs):
        for cp in _sibling_copies(names, refs[:nb], refs[nb:2 * nb], geom, refs[2 * nb + 1], refs[2 * nb + 2]):
            cp.start()
        token = refs[4 * nb + 3]
        token[...] = jnp.zeros_like(token)

    both = [grads[n] for n in names] + lands
    outs = pl.pallas_call(
        body, name="grad_sibling_start_" + names[0],
        out_shape=(pltpu.SemaphoreType.DMA((N_CHIPS * nb,)), pltpu.SemaphoreType.DMA((N_CHIPS * nb,)),
                   *[pltpu.HBM(a.shape, a.dtype) for a in both], TOKEN),
        in_specs=[HBM] * (2 * nb) + [ANY],
        out_specs=(SEM, SEM, *[HBM] * (2 * nb), pl.BlockSpec(memory_space=pltpu.VMEM)),
        input_output_aliases={m: 2 + m for m in range(2 * nb)},
        compiler_params=pltpu.CompilerParams(has_side_effects=EFFECT),
    )(*[_hbm(a) for a in both], after)
    return outs[0], outs[1], list(outs[2:2 + 2 * nb]), outs[2 + 2 * nb]


def _sibling_exchange_wait(names, send, recv, thru, geom, after):
    nb = len(names)

    def body(*refs):
        for cp in _sibling_copies(names, refs[:nb], refs[nb:2 * nb], geom, refs[2 * nb], refs[2 * nb + 1]):
            cp.wait_send()
            cp.wait_recv()

    outs = pl.pallas_call(
        body, name="grad_sibling_wait_" + names[0], out_shape=tuple(pltpu.HBM(t.shape, t.dtype) for t in thru),
        in_specs=[HBM] * (2 * nb) + [SEM, SEM, ANY], out_specs=tuple([HBM] * (2 * nb)),
        input_output_aliases={m: m for m in range(2 * nb)},
        compiler_params=pltpu.CompilerParams(has_side_effects=EFFECT),
    )(*thru, send, recv, after)
    return list(outs[:nb]), list(outs[nb:])


def _allreduce_small(part):
    rows = part.shape[0]
    flips = [(a, b, e) for a in (0, 1) for b in (0, 1) for e in (0, 1) if (a, b, e) != (0, 0, 0)]

    def body(p_ref, o_ref, all_ref, send, recv):
        x, y, c = _me()
        me = 4 * x + 2 * y + c
        all_ref[me] = p_ref[...]
        cps = []
        for k, (a, b, e) in enumerate(flips):
            cp = _remote(p_ref, all_ref.at[me], send.at[k], recv.at[k], (x ^ a, y ^ b, c ^ e))
            cp.start()
            cps.append(cp)
        for k, (a, b, e) in enumerate(flips):
            peer = 4 * (x ^ a) + 2 * (y ^ b) + (c ^ e)
            _remote(p_ref, all_ref.at[peer], send.at[k], recv.at[k], (x ^ a, y ^ b, c ^ e)).wait_recv()
        for cp in cps:
            cp.wait_send()
        tot = all_ref[0]
        for dev in range(1, 8):
            tot = tot + all_ref[dev]
        o_ref[...] = tot

    vm = pl.BlockSpec(memory_space=pltpu.VMEM)
    return pl.pallas_call(
        body, name="allreduce_small", in_specs=[vm], out_specs=vm, out_shape=jax.ShapeDtypeStruct((rows, LANES), F32),
        scratch_shapes=[pltpu.VMEM((8, rows, LANES), F32), pltpu.SemaphoreType.DMA((7,)), pltpu.SemaphoreType.DMA((7,))],
    )(part)


def _rows_tile(rows, cols, mult, elems=1 << 19):
    return _pick(rows, max(mult, elems // cols // mult * mult), mult)


ADD_TILE = 1 << 20


def _add_pairs(g, got, kind, shard, where, name):
    p, r, c = got.shape
    tr = _rows_tile(r, c, 16, ADD_TILE)
    nt = r // tr

    def body(w_ref, a_ref, b_ref, o_ref):
        o_ref[...] = (a_ref[...].astype(F32) + b_ref[...].astype(F32)).astype(BF16)

    if kind == "row":
        g_map = lambda q, i, w_ref: ((2 * q + w_ref[1]) * nt + i, 0)
    else:
        g_map = lambda q, i, w_ref: (w_ref[1] * nt + i, q)
    spec = pl.BlockSpec((None, tr, c), lambda q, i, w_ref: (q, i, 0))
    return pl.pallas_call(
        body, name="grad_add_sibling_" + name,
        grid_spec=pltpu.PrefetchScalarGridSpec(
            num_scalar_prefetch=1, grid=(p, nt), in_specs=[pl.BlockSpec((tr, c), g_map), spec], out_specs=spec),
        out_shape=jax.ShapeDtypeStruct((p, r, c), BF16), compiler_params=_cp(("arbitrary", "arbitrary"), 32),
    )(where, g, got)


def _add_four(t, got, shard, where, name):
    _, r, c = t.shape
    tr = _rows_tile(r, c, 16, ADD_TILE)
    nt = r // tr

    def body(w_ref, own, t0, t1, t2, o_ref):
        o_ref[...] = ((own[...].astype(F32) + t0[...].astype(F32)) + t1[...].astype(F32)) + t2[...].astype(F32)

    spec = lambda q: pl.BlockSpec((None, tr, c), lambda i, w_ref: (q, i, 0))
    return pl.pallas_call(
        body, name="grad_add_chips_" + name,
        grid_spec=pltpu.PrefetchScalarGridSpec(
            num_scalar_prefetch=1, grid=(nt,),
            in_specs=[pl.BlockSpec((None, tr, c), lambda i, w_ref: (w_ref[0], i, 0)), spec(0), spec(1), spec(2)],
            out_specs=pl.BlockSpec((tr, c), lambda i, w_ref: (w_ref[1] * nt + i, 0))),
        out_shape=jax.ShapeDtypeStruct(shard, F32), compiler_params=_cp(("arbitrary",), 48),
    )(where, t, got, got, got)


def _adamw(w, g, m, v, name):
    r, c = w.shape
    tr = _rows_tile(r, c, 8)

    def body(w_ref, g_ref, m_ref, v_ref, go_ref, d_ref, mo_ref, vo_ref):
        gv = g_ref[...]
        mn = ADAM_B1 * m_ref[...] + (1.0 - ADAM_B1) * gv
        vn = ADAM_B2 * v_ref[...] + (1.0 - ADAM_B2) * (gv * gv)
        m_hat = mn / (1.0 - ADAM_B1 ** ADAM_STEP)
        v_hat = vn / (1.0 - ADAM_B2 ** ADAM_STEP)
        go_ref[...] = gv
        d_ref[...] = -ADAM_LR * (m_hat / (jnp.sqrt(v_hat) + ADAM_EPS) + ADAM_WD * w_ref[...])
        mo_ref[...] = mn
        vo_ref[...] = vn

    spec = pl.BlockSpec((tr, c), lambda i: (i, 0))
    return pl.pallas_call(
        body, name=name, grid=(r // tr,), in_specs=[spec] * 4, out_specs=[spec] * 4,
        out_shape=[jax.ShapeDtypeStruct((r, c), F32)] * 4, compiler_params=_cp(("arbitrary",), 32),
    )(w, g, m, v)


def _pack(vectors, rows):
    flat = jnp.concatenate([v.reshape(-1).astype(F32) for v in vectors])
    return jnp.pad(flat, (0, rows * LANES - flat.shape[0])).reshape(rows, LANES)


def _unpack(packed, shapes):
    flat = packed.reshape(-1)
    out, off = [], 0
    for shp in shapes:
        n = 1
        for t in shp:
            n *= t
        out.append(flat[off:off + n].reshape(shp))
        off += n
    return out


def _rows_for(shapes):
    n = sum(functools.reduce(lambda a, b: a * b, shp, 1) for shp in shapes)
    return -(-n // (8 * LANES)) * 8


def kernel(x, w_in, attn_sinks, short_conv_w, w_out, ln1_g, ln1_b, ffn_w_up, ffn_conv_w, ffn_w_down, ln2_g, ln2_b, loss_target, m_w_in, m_attn_sinks, m_short_conv_w, m_w_out, m_ln1_g, m_ln1_b, m_ffn_w_up, m_ffn_conv_w, m_ffn_w_down, m_ln2_g, m_ln2_b, v_w_in, v_attn_sinks, v_short_conv_w, v_w_out, v_ln1_g, v_ln1_b, v_ffn_w_up, v_ffn_conv_w, v_ffn_w_down, v_ln2_g, v_ln2_b):
    xs, tgt = x[0], loss_target[0]
    s, d = xs.shape
    chip = 2 * lax.axis_index("x") + lax.axis_index("y")

    w_big = dict(win_t=w_in[0], wout=w_out[0], wup=ffn_w_up[0], wdown=ffn_w_down[0])
    m_big = dict(win_t=m_w_in[0], wout=m_w_out[0], wup=m_ffn_w_up[0], wdown=m_ffn_w_down[0])
    v_big = dict(win_t=v_w_in[0], wout=v_w_out[0], wup=v_ffn_w_up[0], wdown=v_ffn_w_down[0])
    to_place = dict(w_big, win_t=w_in[0].T)
    geom = _geom({n: to_place[n].shape for n in BIG})
    pad8 = lambda a: jnp.pad(a[0], ((0, 5), (0, 0)))
    where = jnp.stack([chip, lax.axis_index("c")]).astype(jnp.int32)
    placed = {n: _place_shard(to_place[n], where[:1], geom[n][0], n) for n in BIG}
    full, (scw8, fcw8) = _allgather_weights(MIXER, placed, geom, [pad8(short_conv_w), pad8(ffn_conv_w)])
    send, recv, thru, token = _gather_start(FFN, placed, geom, scw8)
    a = _phase_mixer(xs, full["win_t"], full["wout"], scw8, attn_sinks, ln1_g, ln1_b, after=token)
    landed = _gather_forward(FFN, _gather_wait(FFN, send, recv, thru, geom, a["x1b"]), geom)
    full.update(zip(FFN, landed))
    f = _phase_ffn(a, tgt, full["wup"], full["wdown"], fcw8, ln2_g, ln2_b)

    def add_pairs(names, grads, from_sibling):
        return [_add_pairs(grads[m], from_sibling[m], geom[n][0], geom[n][1], where, n) for m, n in enumerate(names)]

    sib_send, sib_recv, sib_thru, sib_token = _sibling_exchange_start(FFN, f, geom, f["st2"])
    started = {}

    def between(dz1b):
        grads, from_sibling = _sibling_exchange_wait(FFN, sib_send, sib_recv, sib_thru, geom, dz1b)
        started["sums"] = add_pairs(FFN, grads, from_sibling)
        started["chip"] = _chip_exchange_start(FFN, started["sums"], geom, f["st2"])
        return started["chip"][3]

    (dproj, dz1b), g_mixer, g_small = _phase_rest(a, f, full["wup"], full["wout"], full["win_t"], scw8, attn_sinks,
                                                  ln1_g + sib_token[0:1, 0:1], between=between)
    ffn_sums = started["sums"]
    send, recv, thru, _ = started["chip"]

    def finish(names, sums, from_chips):
        halves = [_add_four(sums[m], from_chips[m], geom[n][1], where, n) for m, n in enumerate(names)]
        shards = _sibling_assemble(names, halves, geom)
        grads = {n: shards[m].T if n == "win_t" else shards[m] for m, n in enumerate(names)}
        return {n: _adamw(w_big[n], grads[n], m_big[n], v_big[n], "adamw_" + n) for n in names}

    mixer_sums = add_pairs(MIXER, [g_mixer[n] for n in MIXER], _sibling_exchange(MIXER, g_mixer, geom))
    send2, recv2, thru2, token2 = _chip_exchange_start(MIXER, mixer_sums, geom, f["st2"])
    grad_x = _grad_x(dproj, dz1b, full["win_t"], after=token2)
    upd = finish(FFN, ffn_sums, _chip_exchange_wait(FFN, send, recv, thru, grad_x))
    upd.update(finish(MIXER, mixer_sums, _chip_exchange_wait(MIXER, send2, recv2, thru2, upd[FFN[0]][1])))

    small_names = ("ln1_g", "ln1_b", "ln2_g", "ln2_b", "sinks", "fcw", "scw")
    small_shapes = [g_small[n].shape for n in small_names]
    red = _allreduce_small(_pack([g_small["loss_sq"].reshape(1)] + [g_small[n] for n in small_names],
                                 _rows_for([(1,)] + small_shapes)))
    loss_sq, *gs = _unpack(red, [(1,)] + small_shapes)
    gs = dict(zip(small_names, gs))
    loss = (0.5 / d) * loss_sq[0]
    fw, sw = ffn_conv_w.shape[2], short_conv_w.shape[2]
    gs["fcw"] = lax.dynamic_slice_in_dim(gs["fcw"], chip * fw, fw, axis=1)
    gs["scw"] = lax.dynamic_slice_in_dim(gs["scw"], chip * sw, sw, axis=1)

    sm_w = dict(ln1_g=ln1_g[0], ln1_b=ln1_b[0], ln2_g=ln2_g[0], ln2_b=ln2_b[0], sinks=attn_sinks[0],
                fcw=ffn_conv_w[0], scw=short_conv_w[0])
    sm_m = dict(ln1_g=m_ln1_g[0], ln1_b=m_ln1_b[0], ln2_g=m_ln2_g[0], ln2_b=m_ln2_b[0], sinks=m_attn_sinks[0],
                fcw=m_ffn_conv_w[0], scw=m_short_conv_w[0])
    sm_v = dict(ln1_g=v_ln1_g[0], ln1_b=v_ln1_b[0], ln2_g=v_ln2_g[0], ln2_b=v_ln2_b[0], sinks=v_attn_sinks[0],
                fcw=v_ffn_conv_w[0], scw=v_short_conv_w[0])
    shapes = [sm_w[n].shape for n in small_names]
    rows = _rows_for(shapes)
    packed = [_pack([t[n] for n in small_names], rows) for t in (sm_w, gs, sm_m, sm_v)]
    sm_out = [dict(zip(small_names, _unpack(a, shapes))) for a in _adamw(*packed, "adamw_small")]

    def leaf(kind, name):
        if name in ("w_in", "w_out", "ffn_w_up", "ffn_w_down"):
            key = dict(w_in="win_t", w_out="wout", ffn_w_up="wup", ffn_w_down="wdown")[name]
            return upd[key][kind][None]
        key = dict(attn_sinks="sinks", short_conv_w="scw", ffn_conv_w="fcw").get(name, name)
        return sm_out[kind][key][None]

    order = ("w_in", "attn_sinks", "short_conv_w", "w_out", "ln1_g", "ln1_b", "ffn_w_up", "ffn_conv_w", "ffn_w_down",
             "ln2_g", "ln2_b")
    outs = [loss, grad_x[None]]
    for kind in range(4):
        outs += [leaf(kind, n) for n in order]
    return tuple(outs)
```

```python
import functools

import jax
import jax.numpy as jnp
from jax import lax
from jax.experimental import pallas as pl
from jax.experimental.pallas import tpu as pltpu

F32 = jnp.float32
BF16 = jnp.bfloat16
MESH = pl.DeviceIdType.MESH
ANY = pl.BlockSpec(memory_space=pl.ANY)

HEAD_DIM = 64
N_Q_HEADS = 16
N_KV_HEADS = 2
ATTN_WIDTH = N_Q_HEADS * HEAD_DIM
KV_WIDTH = N_KV_HEADS * HEAD_DIM
BLOCK = 128
ROPE_THETA = 10000.0
LN_EPS = 1e-5
ALPHA = 2.0 ** 0.25
NEG_INF = -1e30
ADAM_LR, ADAM_B1, ADAM_B2, ADAM_EPS, ADAM_WD, ADAM_STEP = 0.001, 0.9, 0.999, 1e-08, 0.01, 10
N_CHIPS = 4
LANES = 128
MXU_DIM = 256
SLAB = 128


def _cp(sem, vmem_mb):
    return pltpu.CompilerParams(dimension_semantics=sem, vmem_limit_bytes=vmem_mb << 20)


def _matmul(a, b, *, mode, m, n, k, tm, tn, tk, out_dtype, name, vmem_mb, a_spec=None, b_spec=None,
            res=None, alpha=1.0, after=None):
    nj, ni, nk = n // tn, m // tm, k // tk
    assert nj * tn == n and ni * tm == m and nk * tk == k, (name, m, n, k, tm, tn, tk)
    if mode == "nn":
        dims = ((1,), (0,))
        a_spec = a_spec or pl.BlockSpec((tm, tk), lambda j, i, kk: (i, kk))
        b_spec = b_spec or pl.BlockSpec((tk, tn), lambda j, i, kk: (kk, j))
    elif mode == "nt":
        dims = ((1,), (1,))
        a_spec = a_spec or pl.BlockSpec((tm, tk), lambda j, i, kk: (i, kk))
        b_spec = b_spec or pl.BlockSpec((tn, tk), lambda j, i, kk: (j, kk))
    else:
        dims = ((0,), (0,))
        a_spec = a_spec or pl.BlockSpec((tk, tm), lambda j, i, kk: (kk, i))
        b_spec = b_spec or pl.BlockSpec((tk, tn), lambda j, i, kk: (kk, j))
    has_res = res is not None
    has_after = after is not None

    def body(*refs):
        refs = refs[1:] if has_after else refs
        a_ref, b_ref = refs[0], refs[1]
        res_ref = refs[2] if has_res else None
        o_ref = refs[2 + has_res]
        part = lax.dot_general(a_ref[...].astype(BF16), b_ref[...].astype(BF16), (dims, ((), ())),
                               preferred_element_type=F32)

        def finish(acc):
            if has_res:
                acc = acc + alpha * res_ref[...].astype(F32)
            o_ref[...] = acc.astype(o_ref.dtype)

        if nk == 1:
            finish(part)
        else:
            acc_ref = refs[3 + has_res]
            kk = pl.program_id(2)

            @pl.when(kk == 0)
            def _():
                acc_ref[...] = part

            @pl.when(kk > 0)
            def _():
                acc_ref[...] += part

            @pl.when(kk == nk - 1)
            def _():
                finish(acc_ref[...])

    in_specs = [a_spec, b_spec]
    args = [a, b]
    if has_res:
        in_specs.append(pl.BlockSpec((tm, tn), lambda j, i, kk: (i, j)))
        args.append(res)
    if has_after:
        in_specs.insert(0, pl.BlockSpec(after.shape, lambda j, i, kk: (0, 0)))
        args.insert(0, after)
    return pl.pallas_call(
        body, name=name, grid=(nj, ni, nk), in_specs=in_specs,
        out_specs=pl.BlockSpec((tm, tn), lambda j, i, kk: (i, j)),
        out_shape=jax.ShapeDtypeStruct((m, n), out_dtype),
        scratch_shapes=[pltpu.VMEM((tm, tn), F32)] if nk > 1 else [],
        compiler_params=_cp(("arbitrary", "arbitrary", "arbitrary"), vmem_mb),
    )(*args)


def _pick(total, want, mult):
    if total <= want:
        return total
    for t in range(want, 0, -1):
        if total % t == 0 and t % mult == 0:
            return t
    return total


def _rope_tables(s):
    half = HEAD_DIM // 2
    inv_freq = ROPE_THETA ** (-jnp.arange(half, dtype=F32) / half)
    ang = jnp.arange(s, dtype=F32)[:, None] * inv_freq[None, :]
    cos = jnp.tile(jnp.cos(ang), (1, LANES // half))
    sin = jnp.tile(jnp.concatenate([-jnp.sin(ang), jnp.sin(ang)], axis=1), (1, LANES // HEAD_DIM))
    return cos, sin


def _rope(x, cos, sin, lo):
    partner = jnp.where(lo, pltpu.roll(x, LANES - HEAD_DIM // 2, 1), pltpu.roll(x, HEAD_DIM // 2, 1))
    return x * cos + partner * sin


def _dot(a, b, dims):
    return lax.dot_general(a, b, (dims, ((), ())), preferred_element_type=F32)


NN, NT, TN = ((1,), (0,)), ((1,), (1,)), ((0,), (0,))


def _kv_variants(t, head_lo):
    r = pltpu.roll(t, HEAD_DIM, 1)
    zero = jnp.zeros_like(t)
    a = (jnp.where(head_lo, t, zero).astype(BF16), jnp.where(head_lo, r, zero).astype(BF16))
    b = (jnp.where(head_lo, zero, r).astype(BF16), jnp.where(head_lo, zero, t).astype(BF16))
    return a, b


PAIRS_PER_KV = N_Q_HEADS // 2 // N_KV_HEADS
STACK = PAIRS_PER_KV * BLOCK


def _stack_pairs(ref, j, fn):
    return jnp.concatenate([fn(ref[:, p * LANES:(p + 1) * LANES])
                            for p in range(j * PAIRS_PER_KV, (j + 1) * PAIRS_PER_KV)], axis=0)


def _sink_row(sink_ref, j, hh):
    col = lax.broadcasted_iota(jnp.int32, (1, STACK), 1)
    heads = [2 * p + hh for p in range(j * PAIRS_PER_KV, (j + 1) * PAIRS_PER_KV)]
    row = jnp.full((1, STACK), sink_ref[0, heads[-1]], F32)
    for t in range(PAIRS_PER_KV - 2, -1, -1):
        row = jnp.where(col < (t + 1) * BLOCK, sink_ref[0, heads[t]], row)
    return row


def _attn_exps(qp, ka, kb, valid, sink_a, sink_b):
    out = []
    for kk, sink in ((ka, sink_a), (kb, sink_b)):
        s = jnp.where(valid, _dot(kk, qp, NT), NEG_INF)
        mx = jnp.maximum(jnp.max(s, axis=0, keepdims=True), sink)
        out.append((jnp.exp(s - mx), jnp.exp(sink - mx)))
    return out


def _attn_common(i, q_ref, k_ref, v_ref, kp_ref, vp_ref, cos_ref, sin_ref, cosp_ref, sinp_ref):
    lane = lax.broadcasted_iota(jnp.int32, (1, LANES), 1)
    lo = (lane % HEAD_DIM) < (HEAD_DIM // 2)
    head_lo = lane < HEAD_DIM
    cos, sin = cos_ref[...], sin_ref[...]
    kc = _rope(k_ref[...].astype(F32), cos, sin, lo)
    kp = _rope(kp_ref[...].astype(F32), cosp_ref[...], sinp_ref[...], lo)
    kext = jnp.concatenate([kp, kc], axis=0)
    vext = jnp.concatenate([vp_ref[...].astype(F32), v_ref[...].astype(F32)], axis=0)
    ka, kb = _kv_variants(kext, head_lo)
    va, vb = _kv_variants(vext, head_lo)
    qi = lax.broadcasted_iota(jnp.int32, (1, STACK), 1) % BLOCK
    kj = lax.broadcasted_iota(jnp.int32, (2 * BLOCK, 1), 0)
    valid = (kj > qi) & (kj <= qi + BLOCK) & ((kj >= BLOCK) | (i > 0))
    cos4 = jnp.concatenate([cos] * PAIRS_PER_KV, axis=0)
    sin4 = jnp.concatenate([sin] * PAIRS_PER_KV, axis=0)
    return lo, head_lo, cos, sin, cos4, sin4, ka, kb, va, vb, valid


def _attn_fwd(proj, sinks, cos, sin, s):
    nb = s // BLOCK
    kcol, vcol = ATTN_WIDTH // LANES, ATTN_WIDTH // LANES + 1

    def body(q_ref, k_ref, v_ref, kp_ref, vp_ref, cos_ref, sin_ref, cosp_ref, sinp_ref, sink_ref, o_ref):
        i = pl.program_id(0)
        lo, head_lo, cs, sn, cs4, sn4, ka, kb, va, vb, valid = _attn_common(
            i, q_ref, k_ref, v_ref, kp_ref, vp_ref, cos_ref, sin_ref, cosp_ref, sinp_ref)
        row = lax.broadcasted_iota(jnp.int32, (16, 1), 0)
        one = jnp.ones((), BF16)
        for j in range(N_KV_HEADS):
            q4 = _stack_pairs(q_ref, j, lambda t: t.astype(F32))
            qp = (_rope(q4, cs4, sn4, lo) * HEAD_DIM ** -0.5).astype(BF16)
            exps = _attn_exps(qp, ka[j], kb[j], valid, _sink_row(sink_ref, j, 0), _sink_row(sink_ref, j, 1))
            outs = []
            for (e, es), vv, mine in zip(exps, (va[j], vb[j]), (head_lo, ~head_lo)):
                ee = jnp.concatenate([e.astype(BF16), jnp.where(row == 0, es, 0.0).astype(BF16)], axis=0)
                tail = jnp.where((row == 0) & ~mine, one, jnp.zeros((), BF16))
                vx = jnp.concatenate([jnp.where(mine, vv, one), tail], axis=0)
                un = _dot(ee, vx, TN)
                outs.append(un / pltpu.roll(un, HEAD_DIM, 1))
            o = jnp.where(head_lo, outs[0], outs[1]).astype(BF16)
            for t in range(PAIRS_PER_KV):
                p = j * PAIRS_PER_KV + t
                o_ref[:, p * LANES:(p + 1) * LANES] = o[t * BLOCK:(t + 1) * BLOCK]

    prev = lambda i: (jnp.maximum(i - 1, 0), 0)
    return pl.pallas_call(
        body, name="attn_fwd", grid=(nb,),
        in_specs=[pl.BlockSpec((BLOCK, ATTN_WIDTH), lambda i: (i, 0)),
                  pl.BlockSpec((BLOCK, LANES), lambda i: (i, kcol)),
                  pl.BlockSpec((BLOCK, LANES), lambda i: (i, vcol)),
                  pl.BlockSpec((BLOCK, LANES), lambda i: (jnp.maximum(i - 1, 0), kcol)),
                  pl.BlockSpec((BLOCK, LANES), lambda i: (jnp.maximum(i - 1, 0), vcol)),
                  pl.BlockSpec((BLOCK, LANES), lambda i: (i, 0)),
                  pl.BlockSpec((BLOCK, LANES), lambda i: (i, 0)),
                  pl.BlockSpec((BLOCK, LANES), prev),
                  pl.BlockSpec((BLOCK, LANES), prev),
                  pl.BlockSpec(memory_space=pltpu.SMEM)],
        out_specs=pl.BlockSpec((BLOCK, ATTN_WIDTH), lambda i: (i, 0)),
        out_shape=jax.ShapeDtypeStruct((s, ATTN_WIDTH), BF16),
        compiler_params=_cp(("arbitrary",), 32),
    )(proj, proj, proj, proj, proj, cos, sin, cos, sin, sinks)


def _attn_bwd(proj, dmix, sinks, cos, sin, s):
    nb = s // BLOCK
    kcol, vcol = ATTN_WIDTH // LANES, ATTN_WIDTH // LANES + 1
    pairs_per_kv = N_Q_HEADS // 2 // N_KV_HEADS

    def body(q_ref, k_ref, v_ref, kp_ref, vp_ref, cos_ref, sin_ref, cosp_ref, sinp_ref, sink_ref, do_ref,
             dq_ref, dk_ref, dv_ref, dsink_ref, ck_ref, cv_ref):
        g = pl.program_id(0)
        i = nb - 1 - g

        @pl.when(g == 0)
        def _():
            ck_ref[...] = jnp.zeros_like(ck_ref)
            cv_ref[...] = jnp.zeros_like(cv_ref)
            dsink_ref[...] = jnp.zeros_like(dsink_ref)

        lo, head_lo, cs, sn, cs4, sn4, ka, kb, va, vb, valid = _attn_common(
            i, q_ref, k_ref, v_ref, kp_ref, vp_ref, cos_ref, sin_ref, cosp_ref, sinp_ref)
        lane = lax.broadcasted_iota(jnp.int32, (1, LANES), 1)
        dk_j, dv_j = [], []
        dsink = jnp.zeros((1, LANES), F32)
        for j in range(N_KV_HEADS):
            q4 = _stack_pairs(q_ref, j, lambda t: t.astype(F32))
            qp = (_rope(q4, cs4, sn4, lo) * HEAD_DIM ** -0.5).astype(BF16)
            exps = _attn_exps(qp, ka[j], kb[j], valid, _sink_row(sink_ref, j, 0), _sink_row(sink_ref, j, 1))
            do = _stack_pairs(do_ref, j, lambda t: t)
            dq_r = jnp.zeros((STACK, LANES), F32)
            dkc, dvc = [], []
            for hh, ((e, es), kk, vv) in enumerate(zip(exps, (ka[j], kb[j]), (va[j], vb[j]))):
                inv = 1.0 / (jnp.sum(e, axis=0, keepdims=True) + es)
                pr = e * inv
                dp = _dot(vv, do, NT)
                delta = jnp.sum(pr * dp, axis=0, keepdims=True)
                ds = (pr * (dp - delta)).astype(BF16)
                psd = es * inv * delta
                for t in range(PAIRS_PER_KV):
                    head = 2 * (j * PAIRS_PER_KV + t) + hh
                    dsink = dsink + jnp.where(
                        lane == head, -jnp.sum(psd[:, t * BLOCK:(t + 1) * BLOCK], axis=1, keepdims=True), 0.0)
                dq_r = dq_r + _dot(ds, kk, TN)
                dkc.append(_dot(ds, qp, NN))
                dvc.append(_dot(pr.astype(BF16), do, NN))
            dk_j.append(jnp.where(head_lo, dkc[0], dkc[1]))
            dv_j.append(jnp.where(head_lo, dvc[0], dvc[1]))
            dq = _rope(dq_r * HEAD_DIM ** -0.5, cs4, -sn4, lo).astype(BF16)
            for t in range(PAIRS_PER_KV):
                p = j * PAIRS_PER_KV + t
                dq_ref[:, p * LANES:(p + 1) * LANES] = dq[t * BLOCK:(t + 1) * BLOCK]
        tot_k = [t + pltpu.roll(t, HEAD_DIM, 1) for t in dk_j]
        tot_v = [t + pltpu.roll(t, HEAD_DIM, 1) for t in dv_j]
        dkext = jnp.where(head_lo, tot_k[0], tot_k[1])
        dvext = jnp.where(head_lo, tot_v[0], tot_v[1])
        dk_r = dkext[BLOCK:] + ck_ref[...]
        dk_ref[...] = _rope(dk_r, cs, -sn, lo).astype(BF16)
        dv_ref[...] = (dvext[BLOCK:] + cv_ref[...]).astype(BF16)
        ck_ref[...] = dkext[:BLOCK]
        cv_ref[...] = dvext[:BLOCK]
        dsink_ref[0:1, :] += dsink

    cur = lambda col: (lambda g: (nb - 1 - g, col))
    prv = lambda col: (lambda g: (jnp.maximum(nb - 2 - g, 0), col))
    blk = lambda w, f: pl.BlockSpec((BLOCK, w), f)
    return pl.pallas_call(
        body, name="attn_bwd", grid=(nb,),
        in_specs=[blk(ATTN_WIDTH, cur(0)), blk(LANES, cur(kcol)), blk(LANES, cur(vcol)),
                  blk(LANES, prv(kcol)), blk(LANES, prv(vcol)),
                  blk(LANES, cur(0)), blk(LANES, cur(0)), blk(LANES, prv(0)), blk(LANES, prv(0)),
                  pl.BlockSpec(memory_space=pltpu.SMEM),
                  blk(ATTN_WIDTH, cur(0))],
        out_specs=[blk(ATTN_WIDTH, cur(0)), blk(LANES, cur(0)), blk(LANES, cur(0)),
                   pl.BlockSpec((8, LANES), lambda g: (0, 0))],
        out_shape=[jax.ShapeDtypeStruct((s, ATTN_WIDTH), BF16), jax.ShapeDtypeStruct((s, LANES), BF16),
                   jax.ShapeDtypeStruct((s, LANES), BF16), jax.ShapeDtypeStruct((8, LANES), F32)],
        scratch_shapes=[pltpu.VMEM((BLOCK, LANES), F32), pltpu.VMEM((BLOCK, LANES), F32)],
        compiler_params=_cp(("arbitrary",), 32),
    )(proj, proj, proj, proj, proj, cos, sin, cos, sin, sinks, dmix)


def _causal_conv(x, prev8, w):
    row = lax.broadcasted_iota(jnp.int32, (8, 1), 0)
    r1, r2 = pltpu.roll(x, 1, 0), pltpu.roll(x, 2, 0)
    s1 = jnp.concatenate([jnp.where(row == 0, prev8[7:8], r1[:8]), r1[8:]], axis=0)
    s2 = jnp.concatenate([jnp.where(row == 0, prev8[6:7], jnp.where(row == 1, prev8[7:8], r2[:8])), r2[8:]], axis=0)
    return w[0:1] * s2 + w[1:2] * s1 + w[2:3] * x


def _conv_bwd(dy, x, w, next8):
    r = x.shape[0]
    row = lax.broadcasted_iota(jnp.int32, (8, 1), 0)
    r1, r2 = pltpu.roll(dy, r - 1, 0), pltpu.roll(dy, r - 2, 0)
    n1 = jnp.concatenate([r1[:r - 8], jnp.where(row == 7, next8[0:1], r1[r - 8:])], axis=0)
    n2 = jnp.concatenate([r2[:r - 8], jnp.where(row == 6, next8[0:1], jnp.where(row == 7, next8[1:2], r2[r - 8:]))],
                         axis=0)
    dx = w[2:3] * dy + w[1:2] * n1 + w[0:1] * n2
    dws = [jnp.sum(t * x, axis=0, keepdims=True) for t in (n2, n1, dy)]
    return dx, dws


CONV_COLS = 256


def _convmix_cols(d):
    conv_w = d - ATTN_WIDTH
    base = (ATTN_WIDTH + 2 * KV_WIDTH) // CONV_COLS
    step = conv_w // CONV_COLS
    return base, base + step, base + 2 * step, step


def _convmix_fwd(proj, scw8, s, d):
    gb0, gc0, h0, ncb = _convmix_cols(d)
    tr = _pick(s, 1024, 16)
    ni = s // tr

    def body(gb_ref, gc_ref, h_ref, w_ref, o_ref, carry_ref):
        @pl.when(pl.program_id(1) == 0)
        def _():
            carry_ref[...] = jnp.zeros_like(carry_ref)

        gch = gc_ref[...].astype(F32) * h_ref[...].astype(F32)
        cc = _causal_conv(gch, carry_ref[...], w_ref[...])
        o_ref[...] = (gb_ref[...].astype(F32) * cc).astype(BF16)
        carry_ref[...] = gch[tr - 8:]

    spec = lambda c0: pl.BlockSpec((tr, CONV_COLS), lambda j, i: (i, c0 + j))
    return pl.pallas_call(
        body, name="convmix_fwd", grid=(ncb, ni),
        in_specs=[spec(gb0), spec(gc0), spec(h0), pl.BlockSpec((8, CONV_COLS), lambda j, i: (0, j))],
        out_specs=pl.BlockSpec((tr, CONV_COLS), lambda j, i: (i, j)),
        out_shape=jax.ShapeDtypeStruct((s, d - ATTN_WIDTH), BF16),
        scratch_shapes=[pltpu.VMEM((8, CONV_COLS), F32)],
        compiler_params=_cp(("arbitrary", "arbitrary"), 32),
    )(proj, proj, proj, scw8)


def _convmix_bwd(proj, dmix, scw8, s, d):
    gb0, gc0, h0, ncb = _convmix_cols(d)
    tr = _pick(s, 1024, 16)
    ni = s // tr
    dc0 = ATTN_WIDTH // CONV_COLS

    def body(dc_ref, gb_ref, gc_ref, h_ref, gcp_ref, hp_ref, w_ref, d3_ref, dw_ref, nxt_ref):
        g = pl.program_id(1)
        i = ni - 1 - g

        @pl.when(g == 0)
        def _():
            nxt_ref[...] = jnp.zeros_like(nxt_ref)
            dw_ref[...] = jnp.zeros_like(dw_ref)

        w = w_ref[...]
        gb, gc, h = gb_ref[...].astype(F32), gc_ref[...].astype(F32), h_ref[...].astype(F32)
        gch = gc * h
        prev8 = (gcp_ref[...].astype(F32) * hp_ref[...].astype(F32))[8:16] * (i > 0).astype(F32)
        cc = _causal_conv(gch, prev8, w)
        dc = dc_ref[...].astype(F32)
        dcc = dc * gb
        dgch, dws = _conv_bwd(dcc, gch, w, nxt_ref[...])
        d3_ref[0] = (dc * cc).astype(BF16)
        d3_ref[1] = (dgch * h).astype(BF16)
        d3_ref[2] = (dgch * gc).astype(BF16)
        for t in range(3):
            dw_ref[t:t + 1, :] += dws[t]
        nxt_ref[...] = dcc[0:8]

    cur = lambda c0: pl.BlockSpec((tr, CONV_COLS), lambda j, g: (ni - 1 - g, c0 + j))
    prv = lambda c0: pl.BlockSpec((16, CONV_COLS), lambda j, g: (jnp.maximum((ni - 1 - g) * (tr // 16) - 1, 0), c0 + j))
    return pl.pallas_call(
        body, name="convmix_bwd", grid=(ncb, ni),
        in_specs=[cur(dc0), cur(gb0), cur(gc0), cur(h0), prv(gc0), prv(h0),
                  pl.BlockSpec((8, CONV_COLS), lambda j, g: (0, j))],
        out_specs=[pl.BlockSpec((3, tr, CONV_COLS), lambda j, g: (0, ni - 1 - g, j)),
                   pl.BlockSpec((8, CONV_COLS), lambda j, g: (0, j))],
        out_shape=[jax.ShapeDtypeStruct((3, s, d - ATTN_WIDTH), BF16), jax.ShapeDtypeStruct((8, d - ATTN_WIDTH), F32)],
        scratch_shapes=[pltpu.VMEM((8, CONV_COLS), F32)],
        compiler_params=_cp(("arbitrary", "arbitrary"), 32),
    )(dmix, proj, proj, proj, proj, proj, scw8)


def _ln_fwd(z):
    mu = jnp.mean(z, axis=-1, keepdims=True)
    zc = z - mu
    var = jnp.mean(zc * zc, axis=-1, keepdims=True)
    rstd = lax.rsqrt(var + LN_EPS)
    return zc * rstd, rstd


def _ln_bwd(dout, xh, rstd, g):
    dxh = dout * g
    c1 = jnp.mean(dxh, axis=-1, keepdims=True)
    c2 = jnp.mean(dxh * xh, axis=-1, keepdims=True)
    dz = rstd * (dxh - c1 - xh * c2)
    return dz, jnp.sum(dout * xh, axis=0, keepdims=True), jnp.sum(dout, axis=0, keepdims=True)


def _outproj_ln1(attn, conv, wout, x, g1, b1, s, d):
    tm = _pick(s, 256, 16)
    ka = attn.shape[1]

    def body(a_ref, c_ref, wt_ref, wb_ref, x_ref, g_ref, b_ref, x1_ref, x1b_ref, xh_ref, rs_ref):
        y = _dot(a_ref[...], wt_ref[...], NN) + _dot(c_ref[...], wb_ref[...], NN)
        xh, rstd = _ln_fwd(ALPHA * x_ref[...] + y)
        x1 = xh * g_ref[...] + b_ref[...]
        x1_ref[...] = x1
        x1b_ref[...] = x1.astype(BF16)
        xh_ref[...] = xh.astype(BF16)
        rs_ref[...] = rstd

    row = lambda w: pl.BlockSpec((tm, w), lambda i: (i, 0))
    vec = pl.BlockSpec((1, d), lambda i: (0, 0))
    return pl.pallas_call(
        body, name="outproj_ln1", grid=(s // tm,),
        in_specs=[row(ka), row(d - ka), pl.BlockSpec((ka, d), lambda i: (0, 0)),
                  pl.BlockSpec((d - ka, d), lambda i: (ka // (d - ka), 0)), row(d), vec, vec],
        out_specs=[row(d), row(d), row(d), row(1)],
        out_shape=[jax.ShapeDtypeStruct((s, d), F32), jax.ShapeDtypeStruct((s, d), BF16),
                   jax.ShapeDtypeStruct((s, d), BF16), jax.ShapeDtypeStruct((s, 1), F32)],
        compiler_params=_cp(("arbitrary",), 48),
    )(attn, conv, wout, wout, x, g1, b1)


def _ffn_up(x1b, wup, fcw8, s, d, dff):
    tm = _pick(s, 1024, 16)
    tn = _pick(dff, 512, LANES)
    nj, ni = dff // tn, s // tm

    def body(x_ref, wa_ref, wg_ref, ca_ref, cg_ref, u_ref, y_ref, h_ref, carry_ref):
        @pl.when(pl.program_id(1) == 0)
        def _():
            carry_ref[...] = jnp.zeros_like(carry_ref)

        xa = x_ref[...]
        ys = []
        for part, (w_ref, c_ref) in enumerate(((wa_ref, ca_ref), (wg_ref, cg_ref))):
            ub = _dot(xa, w_ref[...], NN).astype(BF16)
            u_ref[part] = ub
            u = ub.astype(F32)
            y = _causal_conv(u, carry_ref[part], c_ref[...])
            carry_ref[part] = u[tm - 8:]
            yb = y.astype(BF16)
            y_ref[part] = yb
            ys.append(yb.astype(F32))
        a2, g2 = ys
        sig = 1.0 / (1.0 + jnp.exp(-a2))
        h_ref[...] = (a2 * sig * g2).astype(BF16)

    return pl.pallas_call(
        body, name="ffn_up", grid=(nj, ni),
        in_specs=[pl.BlockSpec((tm, d), lambda j, i: (i, 0)),
                  pl.BlockSpec((d, tn), lambda j, i: (0, j)),
                  pl.BlockSpec((d, tn), lambda j, i: (0, j + nj)),
                  pl.BlockSpec((8, tn), lambda j, i: (0, j)),
                  pl.BlockSpec((8, tn), lambda j, i: (0, j + nj))],
        out_specs=[pl.BlockSpec((2, tm, tn), lambda j, i: (0, i, j)),
                   pl.BlockSpec((2, tm, tn), lambda j, i: (0, i, j)),
                   pl.BlockSpec((tm, tn), lambda j, i: (i, j))],
        out_shape=[jax.ShapeDtypeStruct((2, s, dff), BF16), jax.ShapeDtypeStruct((2, s, dff), BF16),
                   jax.ShapeDtypeStruct((s, dff), BF16)],
        scratch_shapes=[pltpu.VMEM((2, 8, tn), F32)],
        compiler_params=_cp(("arbitrary", "arbitrary"), 56),
    )(x1b, wup, wup, fcw8, fcw8)


def _ffn_mid_bwd(dz2b, wdown, u3, y3, fcw8, s, d, dff):
    tm = _pick(s, 1024, 16)
    tn = _pick(dff, 512, LANES)
    nj, ni = dff // tn, s // tm

    def body(dz_ref, wd_ref, u_ref, y_ref, ca_ref, cg_ref, du_ref, dw_ref, nxt_ref):
        @pl.when(pl.program_id(1) == 0)
        def _():
            nxt_ref[...] = jnp.zeros_like(nxt_ref)
            dw_ref[...] = jnp.zeros_like(dw_ref)

        a2, g2 = y_ref[0].astype(F32), y_ref[1].astype(F32)
        sig = 1.0 / (1.0 + jnp.exp(-a2))
        silu = a2 * sig
        dhv = _dot(dz_ref[...], wd_ref[...], NT)
        dys = (dhv * g2 * (sig * (1.0 + a2 * (1.0 - sig))), dhv * silu)
        for part, (c_ref, dy) in enumerate(zip((ca_ref, cg_ref), dys)):
            dx, dws = _conv_bwd(dy, u_ref[part].astype(F32), c_ref[...], nxt_ref[part])
            du_ref[part] = dx.astype(BF16)
            for t in range(3):
                dw_ref[part, t:t + 1, :] += dws[t]
            nxt_ref[part] = dy[0:8]

    return pl.pallas_call(
        body, name="ffn_mid_bwd", grid=(nj, ni),
        in_specs=[pl.BlockSpec((tm, d), lambda j, g: (ni - 1 - g, 0)),
                  pl.BlockSpec((tn, d), lambda j, g: (j, 0)),
                  pl.BlockSpec((2, tm, tn), lambda j, g: (0, ni - 1 - g, j)),
                  pl.BlockSpec((2, tm, tn), lambda j, g: (0, ni - 1 - g, j)),
                  pl.BlockSpec((8, tn), lambda j, g: (0, j)),
                  pl.BlockSpec((8, tn), lambda j, g: (0, j + nj))],
        out_specs=[pl.BlockSpec((2, tm, tn), lambda j, g: (0, ni - 1 - g, j)),
                   pl.BlockSpec((2, 8, tn), lambda j, g: (0, 0, j))],
        out_shape=[jax.ShapeDtypeStruct((2, s, dff), BF16), jax.ShapeDtypeStruct((2, 8, dff), F32)],
        scratch_shapes=[pltpu.VMEM((2, 8, tn), F32)],
        compiler_params=_cp(("arbitrary", "arbitrary"), 56),
    )(dz2b, wdown, u3, y3, fcw8, fcw8)


def _ffn_down_loss(hmid, wdown, x1, target, g2, b2, s, d, dff):
    tm = _pick(s, 512, SLAB)
    tk = _pick(dff, 1408, LANES)
    ni, nk = s // tm, dff // tk
    slab = min(SLAB, tm)

    def body(h_ref, w_ref, x1_ref, t_ref, g_ref, b_ref, dzb_ref, st_ref, acc_ref):
        i, kk = pl.program_id(0), pl.program_id(1)

        @pl.when((i == 0) & (kk == 0))
        def _():
            st_ref[...] = jnp.zeros_like(st_ref)

        part = _dot(h_ref[...], w_ref[...], NN)

        @pl.when(kk == 0)
        def _():
            acc_ref[...] = part

        @pl.when(kk > 0)
        def _():
            acc_ref[...] += part

        @pl.when(kk == nk - 1)
        def _():
            g, b = g_ref[...], b_ref[...]

            def one(sl, carry):
                rows = pl.ds(pl.multiple_of(sl * slab, slab), slab)
                xh, rstd = _ln_fwd(ALPHA * x1_ref[rows, :] + acc_ref[rows, :])
                diff = xh * g + b - t_ref[rows, :]
                sq = jnp.sum(jnp.sum(diff * diff, axis=1, keepdims=True), axis=0, keepdims=True)
                dz, dg, db = _ln_bwd(diff * (1.0 / d), xh, rstd, g)
                dzb_ref[rows, :] = dz.astype(BF16)
                st_ref[0:1, :] += dg
                st_ref[1:2, :] += db
                st_ref[2:3, :] += sq
                return carry

            lax.fori_loop(0, tm // slab, one, 0)

    row = pl.BlockSpec((tm, d), lambda i, kk: (i, 0))
    vec = pl.BlockSpec((1, d), lambda i, kk: (0, 0))
    return pl.pallas_call(
        body, name="ffn_down_loss", grid=(ni, nk),
        in_specs=[pl.BlockSpec((tm, tk), lambda i, kk: (i, kk)), pl.BlockSpec((tk, d), lambda i, kk: (kk, 0)),
                  row, row, vec, vec],
        out_specs=[row, pl.BlockSpec((8, d), lambda i, kk: (0, 0))],
        out_shape=[jax.ShapeDtypeStruct((s, d), BF16), jax.ShapeDtypeStruct((8, d), F32)],
        scratch_shapes=[pltpu.VMEM((tm, d), F32)],
        compiler_params=_cp(("arbitrary", "arbitrary"), 48),
    )(hmid, wdown, x1, target, g2, b2)


def _ffn_dx_ln1_bwd(du3, wup, dz2b, xh1, rstd1, g1, s, d, dff):
    tm = _pick(s, 512, SLAB)
    tk = _pick(dff, 2816, MXU_DIM)
    nkh = dff // tk
    ni, nk = s // tm, 2 * nkh
    slab = min(SLAB, tm)

    def body(a_ref, w_ref, dz2_ref, xh_ref, rs_ref, g_ref, dzb_ref, st_ref, acc_ref):
        i, kk = pl.program_id(0), pl.program_id(1)

        @pl.when((i == 0) & (kk == 0))
        def _():
            st_ref[...] = jnp.zeros_like(st_ref)

        part = _dot(a_ref[...], w_ref[...], NT)

        @pl.when(kk == 0)
        def _():
            acc_ref[...] = part

        @pl.when(kk > 0)
        def _():
            acc_ref[...] += part

        @pl.when(kk == nk - 1)
        def _():
            g = g_ref[...]

            def one(sl, carry):
                rows = pl.ds(pl.multiple_of(sl * slab, slab), slab)
                dx1 = ALPHA * dz2_ref[rows, :].astype(F32) + acc_ref[rows, :]
                dz, dg, db = _ln_bwd(dx1, xh_ref[rows, :].astype(F32), rs_ref[rows, :], g)
                dzb_ref[rows, :] = dz.astype(BF16)
                st_ref[0:1, :] += dg
                st_ref[1:2, :] += db
                return carry

            lax.fori_loop(0, tm // slab, one, 0)

    row = pl.BlockSpec((tm, d), lambda i, kk: (i, 0))
    row1 = pl.BlockSpec((tm, d), lambda i, kk: (i, 0), pipeline_mode=pl.Buffered(1))
    return pl.pallas_call(
        body, name="ffn_dx_ln1_bwd", grid=(ni, nk),
        in_specs=[pl.BlockSpec((None, tm, tk), lambda i, kk: (kk // nkh, i, kk % nkh)),
                  pl.BlockSpec((d, tk), lambda i, kk: (0, kk)),
                  row1, row1, pl.BlockSpec((tm, 1), lambda i, kk: (i, 0)), pl.BlockSpec((1, d), lambda i, kk: (0, 0))],
        out_specs=[row, pl.BlockSpec((8, d), lambda i, kk: (0, 0))],
        out_shape=[jax.ShapeDtypeStruct((s, d), BF16), jax.ShapeDtypeStruct((8, d), F32)],
        scratch_shapes=[pltpu.VMEM((tm, d), F32)],
        compiler_params=_cp(("arbitrary", "arbitrary"), 56),
    )(du3, wup, dz2b, xh1, rstd1, g1)


def _phase_mixer(x, win_t, wout, scw8, sinks, ln1_g, ln1_b, after=None):
    s, d = x.shape
    n_in = win_t.shape[0]
    cos, sin = _rope_tables(s)
    proj = _matmul(x, win_t, mode="nt", m=s, n=n_in, k=d, tm=_pick(s, 512, 16), tn=n_in, tk=d, out_dtype=BF16,
                   name="in_proj", vmem_mb=52, after=after,
                   b_spec=pl.BlockSpec((n_in, d), lambda j, i, kk: (0, 0), pipeline_mode=pl.Buffered(1)))
    attn = _attn_fwd(proj, sinks, cos, sin, s)
    conv = _convmix_fwd(proj, scw8, s, d)
    x1, x1b, xh1, rstd1 = _outproj_ln1(attn, conv, wout, x, ln1_g, ln1_b, s, d)
    return dict(x=x, cos=cos, sin=sin, proj=proj, attn=attn, conv=conv, x1=x1, x1b=x1b, xh1=xh1, rstd1=rstd1)


def _phase_ffn(a, target, wup, wdown, fcw8, ln2_g, ln2_b):
    x1, x1b = a["x1"], a["x1b"]
    s, d = x1.shape
    dff = wdown.shape[0]
    u3, y3, hmid = _ffn_up(x1b, wup, fcw8, s, d, dff)
    dz2b, st2 = _ffn_down_loss(hmid, wdown, x1, target, ln2_g, ln2_b, s, d, dff)

    ts = _pick(s, 2048, 16)
    g_wdown = _matmul(hmid, dz2b, mode="tn", m=dff, n=d, k=s, tm=_pick(dff, 2816, MXU_DIM), tn=_pick(d, 512, MXU_DIM),
                      tk=ts, out_dtype=BF16, name="grad_w_down", vmem_mb=56)
    du3, dfcw = _ffn_mid_bwd(dz2b, wdown, u3, y3, fcw8, s, d, dff)
    tnu = _pick(dff, 2816, MXU_DIM)
    njh = dff // tnu
    g_wup = _matmul(x1b, du3, mode="tn", m=d, n=2 * dff, k=s, tm=_pick(d, 512, LANES), tn=tnu, tk=ts, out_dtype=BF16,
                    name="grad_w_up", vmem_mb=56,
                    b_spec=pl.BlockSpec((None, ts, tnu), lambda j, i, kk: (j // njh, kk, j % njh)))
    return dict(du3=du3, dz2b=dz2b, st2=st2, dfcw=dfcw, wdown=g_wdown, wup=g_wup)


def _phase_rest(a, f, wup, wout, win_t, scw8, sinks, ln1_g, between=None):
    xb, cos, sin, proj, attn, conv = a["x"], a["cos"], a["sin"], a["proj"], a["attn"], a["conv"]
    du3, dz2b, st2, dfcw = f["du3"], f["dz2b"], f["st2"], f["dfcw"]
    s, d = a["x1"].shape
    dff = wup.shape[1] // 2
    n_in = win_t.shape[0]
    ts = _pick(s, 2048, 16)
    dz1b, st1 = _ffn_dx_ln1_bwd(du3, wup, dz2b, a["xh1"], a["rstd1"], ln1_g, s, d, dff)
    after = between(dz1b) if between is not None else None

    mix = jnp.concatenate([attn, conv], axis=1)
    g_wout = _matmul(mix, dz1b, mode="tn", m=d, n=d, k=s, tm=_pick(d, 1024, LANES), tn=_pick(d, 1024, LANES), tk=ts,
                     out_dtype=BF16, name="grad_w_out", vmem_mb=48, after=after)
    dmix = _matmul(dz1b, wout, mode="nt", m=s, n=d, k=d, tm=_pick(s, 1024, 16), tn=_pick(d, 1024, LANES), tk=d,
                   out_dtype=BF16, name="out_dmix", vmem_mb=48, after=after)
    d3, dscw = _convmix_bwd(proj, dmix, scw8, s, d)
    dq, dk, dv, dsink = _attn_bwd(proj, dmix, sinks, cos, sin, s)
    dproj = jnp.concatenate([dq, dk, dv, d3[0], d3[1], d3[2]], axis=1)
    g_win_t = _matmul(dproj, xb, mode="tn", m=n_in, n=d, k=s, tm=_pick(n_in, 2176, LANES), tn=_pick(d, 512, LANES),
                      tk=ts, out_dtype=BF16, name="grad_w_in", vmem_mb=48)
    small = dict(loss_sq=st2[2, 0], ln2_g=st2[0], ln2_b=st2[1], ln1_g=st1[0], ln1_b=st1[1], sinks=dsink[0, :N_Q_HEADS],
                 fcw=jnp.concatenate([dfcw[0, :3], dfcw[1, :3]], axis=1), scw=dscw[:3])
    return (dproj, dz1b), dict(win_t=g_win_t, wout=g_wout), small


def _grad_x(dproj, dz1b, win_t, after=None):
    s, n_in = dproj.shape
    d = win_t.shape[1]
    return _matmul(dproj, win_t, mode="nn", m=s, n=d, k=n_in, tm=_pick(s, 512, 16), tn=_pick(d, 1024, LANES),
                   tk=n_in, out_dtype=F32, name="grad_x", vmem_mb=56, res=dz1b, alpha=ALPHA, after=after)


def _local_step(x, target, win_t, wout, wup, wdown, scw8, fcw8, sinks, ln1_g, ln1_b, ln2_g, ln2_b):
    a = _phase_mixer(x, win_t, wout, scw8, sinks, ln1_g, ln1_b)
    f = _phase_ffn(a, target, wup, wdown, fcw8, ln2_g, ln2_b)
    (dproj, dz1b), g, small = _phase_rest(a, f, wup, wout, win_t, scw8, sinks, ln1_g)
    return _grad_x(dproj, dz1b, win_t), dict(g, wup=f["wup"], wdown=f["wdown"]), small


MIXER = ("win_t", "wout")
FFN = ("wup", "wdown")
BIG = MIXER + FFN


def _geom(shard_shapes):
    out = {}
    for name in BIG:
        r, c = shard_shapes[name]
        out[name] = ("col" if name == "wup" else "row", (r, c), (r // 2, c))
    return out


def _full_shape(kind, shard):
    r, c = shard
    return (N_CHIPS * r, c) if kind == "row" else (r, N_CHIPS * c)


def _piece_of(ref, kind, shard, chip, half):
    r, c = shard
    if kind == "row":
        return ref.at[pl.ds(chip * r + half * (r // 2), r // 2), :]
    return ref.at[pl.ds(half * (r // 2), r // 2), pl.ds(chip * c, c)]


def _shard_piece(ref, shard, half):
    r, _ = shard
    return ref.at[pl.ds(half * (r // 2), r // 2), :]


def _me():
    return lax.axis_index("x"), lax.axis_index("y"), lax.axis_index("c")


def _other_chips(x, y):
    return [(1 - x, y), (x, 1 - y), (1 - x, 1 - y)]


def _remote(src, dst, send_sem, recv_sem, dev):
    return pltpu.make_async_remote_copy(src_ref=src, dst_ref=dst, send_sem=send_sem, recv_sem=recv_sem,
                                        device_id=dev, device_id_type=MESH)


def _place_shard(w, chip1, kind, name):
    r, c = w.shape
    tr = _rows_tile(r, c, 16)
    nt = r // tr

    def body(chip_ref, w_ref, o_ref):
        o_ref[...] = w_ref[...].astype(BF16)

    out_map = (lambda i, chip_ref: (chip_ref[0] * nt + i, 0)) if kind == "row" else (lambda i, chip_ref: (i, chip_ref[0]))
    return pl.pallas_call(
        body, name="place_" + name,
        grid_spec=pltpu.PrefetchScalarGridSpec(
            num_scalar_prefetch=1, grid=(nt,),
            in_specs=[pl.BlockSpec((tr, c), lambda i, chip_ref: (i, 0))],
            out_specs=pl.BlockSpec((tr, c), out_map)),
        out_shape=jax.ShapeDtypeStruct(_full_shape(kind, (r, c)), BF16),
        compiler_params=_cp(("arbitrary",), 32),
    )(chip1, w)


def _allgather_weights(names, placed, geom, small_shards):
    nb, ns = len(names), len(small_shards)
    small_w = [a.shape[1] for a in small_shards]

    def body(*refs):
        sm = refs[nb:nb + ns]
        full = refs[nb + ns:2 * nb + ns]
        smf = refs[2 * nb + ns:2 * nb + 2 * ns]
        send, recv, loc = refs[2 * nb + 2 * ns:]
        x, y, c = _me()
        chip = 2 * x + y
        sib = (x, y, 1 - c)
        others = _other_chips(x, y)
        locals_, sends = [], []
        for m, name in enumerate(names):
            kind, shard, _ = geom[name]
            mine = _piece_of(full[m], kind, shard, chip, c)
            for k, (qx, qy) in enumerate(others):
                cp = _remote(mine, mine, send.at[6 * m + k], recv.at[6 * m + k], (qx, qy, c))
                cp.start()
                sends.append(cp)
        for t in range(ns):
            cp = pltpu.make_async_copy(sm[t], smf[t].at[:, pl.ds(chip * small_w[t], small_w[t])], loc.at[t])
            cp.start()
            locals_.append(cp)
            for k, (qx, qy) in enumerate(others):
                cp = _remote(sm[t], smf[t].at[:, pl.ds(chip * small_w[t], small_w[t])],
                             send.at[6 * nb + 3 * t + k], recv.at[6 * nb + 3 * t + k], (qx, qy, c))
                cp.start()
                sends.append(cp)
        for m, name in enumerate(names):
            kind, shard, _ = geom[name]
            for k, (qx, qy) in enumerate(others):
                got = _piece_of(full[m], kind, shard, 2 * qx + qy, c)
                _remote(got, got, send.at[6 * m + k], recv.at[6 * m + k], (qx, qy, c)).wait_recv()
                cp = _remote(got, got, send.at[6 * m + 3 + k], recv.at[6 * m + 3 + k], sib)
                cp.start()
                sends.append(cp)
        for t in range(ns):
            for k, (qx, qy) in enumerate(others):
                got = smf[t].at[:, pl.ds((2 * qx + qy) * small_w[t], small_w[t])]
                _remote(got, got, send.at[6 * nb + 3 * t + k], recv.at[6 * nb + 3 * t + k], (qx, qy, c)).wait_recv()
        for m, name in enumerate(names):
            kind, shard, _ = geom[name]
            for k, (qx, qy) in enumerate(others):
                got = _piece_of(full[m], kind, shard, 2 * qx + qy, 1 - c)
                _remote(got, got, send.at[6 * m + 3 + k], recv.at[6 * m + 3 + k], sib).wait_recv()
        for cp in sends:
            cp.wait_send()
        for cp in locals_:
            cp.wait()

    nsem = 6 * nb + 3 * ns
    out_shape = [jax.ShapeDtypeStruct(placed[n].shape, BF16) for n in names]
    out_shape += [jax.ShapeDtypeStruct((8, N_CHIPS * w), F32) for w in small_w]
    outs = pl.pallas_call(
        body, name="allgather_weights", in_specs=[ANY] * (nb + ns), out_specs=[ANY] * (nb + ns), out_shape=out_shape,
        input_output_aliases={m: m for m in range(nb)},
        scratch_shapes=[pltpu.SemaphoreType.DMA((nsem,)), pltpu.SemaphoreType.DMA((nsem,)),
                        pltpu.SemaphoreType.DMA((ns,))],
    )(*[placed[n] for n in names], *small_shards)
    return dict(zip(names, outs[:nb])), list(outs[nb:])


def _sibling_exchange(names, grads, geom):
    nb = len(names)

    def body(*refs):
        g = refs[:nb]
        got = refs[nb:2 * nb]
        send, recv = refs[2 * nb:]
        x, y, c = _me()
        sib = (x, y, 1 - c)
        cps = []
        for m, name in enumerate(names):
            kind, shard, _ = geom[name]
            for r in range(N_CHIPS):
                cp = _remote(_piece_of(g[m], kind, shard, r, 1 - c), got[m].at[r],
                             send.at[N_CHIPS * m + r], recv.at[N_CHIPS * m + r], sib)
                cp.start()
                cps.append(cp)
        for cp in cps:
            cp.wait_recv()
        for cp in cps:
            cp.wait_send()

    return pl.pallas_call(
        body, name="grad_sibling_exchange_" + names[0], in_specs=[ANY] * nb, out_specs=[ANY] * nb,
        out_shape=[jax.ShapeDtypeStruct((N_CHIPS,) + geom[n][2], BF16) for n in names],
        scratch_shapes=[pltpu.SemaphoreType.DMA((N_CHIPS * nb,)), pltpu.SemaphoreType.DMA((N_CHIPS * nb,))],
    )(*[grads[n] for n in names])


def _sibling_assemble(names, shards, geom):
    nb = len(names)

    def body(*refs):
        full = refs[nb:2 * nb]
        send, recv = refs[2 * nb:]
        x, y, c = _me()
        sib = (x, y, 1 - c)
        cps = []
        for m, name in enumerate(names):
            mine = _shard_piece(full[m], geom[name][1], c)
            cp = _remote(mine, mine, send.at[m], recv.at[m], sib)
            cp.start()
            cps.append(cp)
        for m, name in enumerate(names):
            theirs = _shard_piece(full[m], geom[name][1], 1 - c)
            _remote(theirs, theirs, send.at[m], recv.at[m], sib).wait_recv()
        for cp in cps:
            cp.wait_send()

    return pl.pallas_call(
        body, name="grad_sibling_assemble_" + names[0], in_specs=[ANY] * nb, out_specs=[ANY] * nb,
        out_shape=[jax.ShapeDtypeStruct(geom[n][1], F32) for n in names],
        input_output_aliases={m: m for m in range(nb)},
        scratch_shapes=[pltpu.SemaphoreType.DMA((nb,)), pltpu.SemaphoreType.DMA((nb,))],
    )(*shards)


HBM = pl.BlockSpec(memory_space=pltpu.HBM)
SEM = pl.BlockSpec(memory_space=pltpu.SEMAPHORE)
EFFECT = pltpu.SideEffectType.DATAFLOW_SIDE_EFFECTING
TOKEN = jax.ShapeDtypeStruct((8, LANES), F32)


def _hbm(a):
    return pltpu.with_memory_space_constraint(a, pltpu.HBM)


def _gather_copies(names, full, geom, send, recv):
    x, y, c = _me()
    out = []
    for m, name in enumerate(names):
        kind, shard, _ = geom[name]
        mine = _piece_of(full[m], kind, shard, 2 * x + y, c)
        for k, (qx, qy) in enumerate(_other_chips(x, y)):
            theirs = _piece_of(full[m], kind, shard, 2 * qx + qy, c)
            out.append((_remote(mine, mine, send.at[3 * m + k], recv.at[3 * m + k], (qx, qy, c)),
                        _remote(theirs, theirs, send.at[3 * m + k], recv.at[3 * m + k], (qx, qy, c))))
    return out


def _gather_start(names, placed, geom, after):
    nb = len(names)

    def body(*refs):
        full = refs[:nb]
        send, recv = refs[nb + 1], refs[nb + 2]
        token = refs[2 * nb + 3]
        for cp, _ in _gather_copies(names, full, geom, send, recv):
            cp.start()
        token[...] = jnp.zeros_like(token)

    outs = pl.pallas_call(
        body, name="gather_start_" + names[0],
        out_shape=(pltpu.SemaphoreType.DMA((3 * nb,)), pltpu.SemaphoreType.DMA((3 * nb,)),
                   *[pltpu.HBM(placed[n].shape, BF16) for n in names], TOKEN),
        in_specs=[HBM] * nb + [ANY], out_specs=(SEM, SEM, *[HBM] * nb, pl.BlockSpec(memory_space=pltpu.VMEM)),
        input_output_aliases={m: 2 + m for m in range(nb)},
        compiler_params=pltpu.CompilerParams(has_side_effects=EFFECT),
    )(*[_hbm(placed[n]) for n in names], after)
    return outs[0], outs[1], list(outs[2:2 + nb]), outs[2 + nb]


def _gather_wait(names, send, recv, thru, geom, after):
    nb = len(names)

    def body(*refs):
        full = refs[:nb]
        for mine, theirs in _gather_copies(names, full, geom, refs[nb], refs[nb + 1]):
            mine.wait_send()
            theirs.wait_recv()

    return pl.pallas_call(
        body, name="gather_wait_" + names[0], out_shape=tuple(pltpu.HBM(t.shape, t.dtype) for t in thru),
        in_specs=[HBM] * nb + [SEM, SEM, ANY], out_specs=tuple([HBM] * nb),
        input_output_aliases={m: m for m in range(nb)},
        compiler_params=pltpu.CompilerParams(has_side_effects=EFFECT),
    )(*thru, send, recv, after)


def _gather_forward(names, full, geom):
    nb = len(names)

    def body(*refs):
        arr = refs[nb:2 * nb]
        send, recv = refs[2 * nb:]
        x, y, c = _me()
        sib = (x, y, 1 - c)
        cps = []
        for m, name in enumerate(names):
            kind, shard, _ = geom[name]
            for k, (qx, qy) in enumerate(_other_chips(x, y)):
                got = _piece_of(arr[m], kind, shard, 2 * qx + qy, c)
                cp = _remote(got, got, send.at[3 * m + k], recv.at[3 * m + k], sib)
                cp.start()
                cps.append(cp)
        for m, name in enumerate(names):
            kind, shard, _ = geom[name]
            for k, (qx, qy) in enumerate(_other_chips(x, y)):
                theirs = _piece_of(arr[m], kind, shard, 2 * qx + qy, 1 - c)
                _remote(theirs, theirs, send.at[3 * m + k], recv.at[3 * m + k], sib).wait_recv()
        for cp in cps:
            cp.wait_send()

    return pl.pallas_call(
        body, name="gather_forward_" + names[0], in_specs=[ANY] * nb, out_specs=[ANY] * nb,
        out_shape=[jax.ShapeDtypeStruct(a.shape, a.dtype) for a in full],
        input_output_aliases={m: m for m in range(nb)},
        scratch_shapes=[pltpu.SemaphoreType.DMA((3 * nb,)), pltpu.SemaphoreType.DMA((3 * nb,))],
    )(*full)


def _scatter_copies(nb, t, got, send, recv):
    x, y, c = _me()
    return [_remote(t[m].at[2 * qx + qy], got[m].at[k], send.at[3 * m + k], recv.at[3 * m + k], (qx, qy, c))
            for m in range(nb) for k, (qx, qy) in enumerate(_other_chips(x, y))]


def _chip_exchange_start(names, chip_sums, geom, after):
    nb = len(names)
    lands = [lax.empty((N_CHIPS - 1,) + geom[n][2], BF16) for n in names]

    def body(*refs):
        t, got = refs[:nb], refs[nb:2 * nb]
        send, recv = refs[2 * nb + 1], refs[2 * nb + 2]
        token = refs[4 * nb + 3]
        for cp in _scatter_copies(nb, t, got, send, recv):
            cp.start()
        token[...] = jnp.zeros_like(token)

    both = list(chip_sums) + lands
    outs = pl.pallas_call(
        body, name="grad_chip_start_" + names[0],
        out_shape=(pltpu.SemaphoreType.DMA((3 * nb,)), pltpu.SemaphoreType.DMA((3 * nb,)),
                   *[pltpu.HBM(a.shape, a.dtype) for a in both], TOKEN),
        in_specs=[HBM] * (2 * nb) + [ANY],
        out_specs=(SEM, SEM, *[HBM] * (2 * nb), pl.BlockSpec(memory_space=pltpu.VMEM)),
        input_output_aliases={m: 2 + m for m in range(2 * nb)},
        compiler_params=pltpu.CompilerParams(has_side_effects=EFFECT),
    )(*[_hbm(a) for a in both], after)
    return outs[0], outs[1], list(outs[2:2 + 2 * nb]), outs[2 + 2 * nb]


def _chip_exchange_wait(names, send, recv, thru, after):
    nb = len(names)

    def body(*refs):
        for cp in _scatter_copies(nb, refs[:nb], refs[nb:2 * nb], refs[2 * nb], refs[2 * nb + 1]):
            cp.wait_send()
            cp.wait_recv()

    outs = pl.pallas_call(
        body, name="grad_chip_wait_" + names[0], out_shape=tuple(pltpu.HBM(t.shape, t.dtype) for t in thru),
        in_specs=[HBM] * (2 * nb) + [SEM, SEM, ANY], out_specs=tuple([HBM] * (2 * nb)),
        input_output_aliases={m: m for m in range(2 * nb)},
        compiler_params=pltpu.CompilerParams(has_side_effects=EFFECT),
    )(*thru, send, recv, after)
    return list(outs[nb:])


def _sibling_copies(names, g, got, geom, send, recv):
    x, y, c = _me()
    out = []
    for m, name in enumerate(names):
        kind, shard, _ = geom[name]
        for r in range(N_CHIPS):
            out.append(_remote(_piece_of(g[m], kind, shard, r, 1 - c), got[m].at[r],
                               send.at[N_CHIPS * m + r], recv.at[N_CHIPS * m + r], (x, y, 1 - c)))
    return out


def _sibling_exchange_start(names, grads, geom, after):
    nb = len(names)
    lands = [lax.empty((N_CHIPS,) + geom[n][2], BF16) for n in names]

    def body(*refs):
        for cp in _sibling_copies(names, refs[:nb], refs[nb:2 * nb], geom, refs[2 * nb + 1], refs[2 * nb + 2]):
            cp.start()
        token = refs[4 * nb + 3]
        token[...] = jnp.zeros_like(token)

    both = [grads[n] for n in names] + lands
    outs = pl.pallas_call(
        body, name="grad_sibling_start_" + names[0],
        out_shape=(pltpu.SemaphoreType.DMA((N_CHIPS * nb,)), pltpu.SemaphoreType.DMA((N_CHIPS * nb,)),
                   *[pltpu.HBM(a.shape, a.dtype) for a in both], TOKEN),
        in_specs=[HBM] * (2 * nb) + [ANY],
        out_specs=(SEM, SEM, *[HBM] * (2 * nb), pl.BlockSpec(memory_space=pltpu.VMEM)),
        input_output_aliases={m: 2 + m for m in range(2 * nb)},
        compiler_params=pltpu.CompilerParams(has_side_effects=EFFECT),
    )(*[_hbm(a) for a in both], after)
    return outs[0], outs[1], list(outs[2:2 + 2 * nb]), outs[2 + 2 * nb]


def _sibling_exchange_wait(names, send, recv, thru, geom, after):
    nb = len(names)

    def body(*refs):
        for cp in _sibling_copies(names, refs[:nb], refs[nb:2 * nb], geom, refs[2 * nb], refs[2 * nb + 1]):
            cp.wait_send()
            cp.wait_recv()

    outs = pl.pallas_call(
        body, name="grad_sibling_wait_" + names[0], out_shape=tuple(pltpu.HBM(t.shape, t.dtype) for t in thru),
        in_specs=[HBM] * (2 * nb) + [SEM, SEM, ANY], out_specs=tuple([HBM] * (2 * nb)),
        input_output_aliases={m: m for m in range(2 * nb)},
        compiler_params=pltpu.CompilerParams(has_side_effects=EFFECT),
    )(*thru, send, recv, after)
    return list(outs[:nb]), list(outs[nb:])


def _allreduce_small(part):
    rows = part.shape[0]
    flips = [(a, b, e) for a in (0, 1) for b in (0, 1) for e in (0, 1) if (a, b, e) != (0, 0, 0)]

    def body(p_ref, o_ref, all_ref, send, recv):
        x, y, c = _me()
        me = 4 * x + 2 * y + c
        all_ref[me] = p_ref[...]
        cps = []
        for k, (a, b, e) in enumerate(flips):
            cp = _remote(p_ref, all_ref.at[me], send.at[k], recv.at[k], (x ^ a, y ^ b, c ^ e))
            cp.start()
            cps.append(cp)
        for k, (a, b, e) in enumerate(flips):
            peer = 4 * (x ^ a) + 2 * (y ^ b) + (c ^ e)
            _remote(p_ref, all_ref.at[peer], send.at[k], recv.at[k], (x ^ a, y ^ b, c ^ e)).wait_recv()
        for cp in cps:
            cp.wait_send()
        tot = all_ref[0]
        for dev in range(1, 8):
            tot = tot + all_ref[dev]
        o_ref[...] = tot

    vm = pl.BlockSpec(memory_space=pltpu.VMEM)
    return pl.pallas_call(
        body, name="allreduce_small", in_specs=[vm], out_specs=vm, out_shape=jax.ShapeDtypeStruct((rows, LANES), F32),
        scratch_shapes=[pltpu.VMEM((8, rows, LANES), F32), pltpu.SemaphoreType.DMA((7,)), pltpu.SemaphoreType.DMA((7,))],
    )(part)


def _rows_tile(rows, cols, mult, elems=1 << 19):
    return _pick(rows, max(mult, elems // cols // mult * mult), mult)


ADD_TILE = 1 << 20


def _add_pairs(g, got, kind, shard, where, name):
    p, r, c = got.shape
    tr = _rows_tile(r, c, 16, ADD_TILE)
    nt = r // tr

    def body(w_ref, a_ref, b_ref, o_ref):
        o_ref[...] = (a_ref[...].astype(F32) + b_ref[...].astype(F32)).astype(BF16)

    if kind == "row":
        g_map = lambda q, i, w_ref: ((2 * q + w_ref[1]) * nt + i, 0)
    else:
        g_map = lambda q, i, w_ref: (w_ref[1] * nt + i, q)
    spec = pl.BlockSpec((None, tr, c), lambda q, i, w_ref: (q, i, 0))
    return pl.pallas_call(
        body, name="grad_add_sibling_" + name,
        grid_spec=pltpu.PrefetchScalarGridSpec(
            num_scalar_prefetch=1, grid=(p, nt), in_specs=[pl.BlockSpec((tr, c), g_map), spec], out_specs=spec),
        out_shape=jax.ShapeDtypeStruct((p, r, c), BF16), compiler_params=_cp(("arbitrary", "arbitrary"), 32),
    )(where, g, got)


def _add_four(t, got, shard, where, name):
    _, r, c = t.shape
    tr = _rows_tile(r, c, 16, ADD_TILE)
    nt = r // tr

    def body(w_ref, own, t0, t1, t2, o_ref):
        o_ref[...] = ((own[...].astype(F32) + t0[...].astype(F32)) + t1[...].astype(F32)) + t2[...].astype(F32)

    spec = lambda q: pl.BlockSpec((None, tr, c), lambda i, w_ref: (q, i, 0))
    return pl.pallas_call(
        body, name="grad_add_chips_" + name,
        grid_spec=pltpu.PrefetchScalarGridSpec(
            num_scalar_prefetch=1, grid=(nt,),
            in_specs=[pl.BlockSpec((None, tr, c), lambda i, w_ref: (w_ref[0], i, 0)), spec(0), spec(1), spec(2)],
            out_specs=pl.BlockSpec((tr, c), lambda i, w_ref: (w_ref[1] * nt + i, 0))),
        out_shape=jax.ShapeDtypeStruct(shard, F32), compiler_params=_cp(("arbitrary",), 48),
    )(where, t, got, got, got)


def _adamw(w, g, m, v, name):
    r, c = w.shape
    tr = _rows_tile(r, c, 8)

    def body(w_ref, g_ref, m_ref, v_ref, go_ref, d_ref, mo_ref, vo_ref):
        gv = g_ref[...]
        mn = ADAM_B1 * m_ref[...] + (1.0 - ADAM_B1) * gv
        vn = ADAM_B2 * v_ref[...] + (1.0 - ADAM_B2) * (gv * gv)
        m_hat = mn / (1.0 - ADAM_B1 ** ADAM_STEP)
        v_hat = vn / (1.0 - ADAM_B2 ** ADAM_STEP)
        go_ref[...] = gv
        d_ref[...] = -ADAM_LR * (m_hat / (jnp.sqrt(v_hat) + ADAM_EPS) + ADAM_WD * w_ref[...])
        mo_ref[...] = mn
        vo_ref[...] = vn

    spec = pl.BlockSpec((tr, c), lambda i: (i, 0))
    return pl.pallas_call(
        body, name=name, grid=(r // tr,), in_specs=[spec] * 4, out_specs=[spec] * 4,
        out_shape=[jax.ShapeDtypeStruct((r, c), F32)] * 4, compiler_params=_cp(("arbitrary",), 32),
    )(w, g, m, v)


def _pack(vectors, rows):
    flat = jnp.concatenate([v.reshape(-1).astype(F32) for v in vectors])
    return jnp.pad(flat, (0, rows * LANES - flat.shape[0])).reshape(rows, LANES)


def _unpack(packed, shapes):
    flat = packed.reshape(-1)
    out, off = [], 0
    for shp in shapes:
        n = 1
        for t in shp:
            n *= t
        out.append(flat[off:off + n].reshape(shp))
        off += n
    return out


def _rows_for(shapes):
    n = sum(functools.reduce(lambda a, b: a * b, shp, 1) for shp in shapes)
    return -(-n // (8 * LANES)) * 8


def kernel(x, w_in, attn_sinks, short_conv_w, w_out, ln1_g, ln1_b, ffn_w_up, ffn_conv_w, ffn_w_down, ln2_g, ln2_b, loss_target, m_w_in, m_attn_sinks, m_short_conv_w, m_w_out, m_ln1_g, m_ln1_b, m_ffn_w_up, m_ffn_conv_w, m_ffn_w_down, m_ln2_g, m_ln2_b, v_w_in, v_attn_sinks, v_short_conv_w, v_w_out, v_ln1_g, v_ln1_b, v_ffn_w_up, v_ffn_conv_w, v_ffn_w_down, v_ln2_g, v_ln2_b):
    xs, tgt = x[0], loss_target[0]
    s, d = xs.shape
    chip = 2 * lax.axis_index("x") + lax.axis_index("y")

    w_big = dict(win_t=w_in[0], wout=w_out[0], wup=ffn_w_up[0], wdown=ffn_w_down[0])
    m_big = dict(win_t=m_w_in[0], wout=m_w_out[0], wup=m_ffn_w_up[0], wdown=m_ffn_w_down[0])
    v_big = dict(win_t=v_w_in[0], wout=v_w_out[0], wup=v_ffn_w_up[0], wdown=v_ffn_w_down[0])
    to_place = dict(w_big, win_t=w_in[0].T)
    geom = _geom({n: to_place[n].shape for n in BIG})
    pad8 = lambda a: jnp.pad(a[0], ((0, 5), (0, 0)))
    where = jnp.stack([chip, lax.axis_index("c")]).astype(jnp.int32)
    placed = {n: _place_shard(to_place[n], where[:1], geom[n][0], n) for n in BIG}
    full, (scw8, fcw8) = _allgather_weights(MIXER, placed, geom, [pad8(short_conv_w), pad8(ffn_conv_w)])
    send, recv, thru, token = _gather_start(FFN, placed, geom, scw8)
    a = _phase_mixer(xs, full["win_t"], full["wout"], scw8, attn_sinks, ln1_g, ln1_b, after=token)
    landed = _gather_forward(FFN, _gather_wait(FFN, send, recv, thru, geom, a["x1b"]), geom)
    full.update(zip(FFN, landed))
    f = _phase_ffn(a, tgt, full["wup"], full["wdown"], fcw8, ln2_g, ln2_b)

    def add_pairs(names, grads, from_sibling):
        return [_add_pairs(grads[m], from_sibling[m], geom[n][0], geom[n][1], where, n) for m, n in enumerate(names)]

    sib_send, sib_recv, sib_thru, sib_token = _sibling_exchange_start(FFN, f, geom, f["st2"])
    started = {}

    def between(dz1b):
        grads, from_sibling = _sibling_exchange_wait(FFN, sib_send, sib_recv, sib_thru, geom, dz1b)
        started["sums"] = add_pairs(FFN, grads, from_sibling)
        started["chip"] = _chip_exchange_start(FFN, started["sums"], geom, f["st2"])
        return started["chip"][3]

    (dproj, dz1b), g_mixer, g_small = _phase_rest(a, f, full["wup"], full["wout"], full["win_t"], scw8, attn_sinks,
                                                  ln1_g + sib_token[0:1, 0:1], between=between)
    ffn_sums = started["sums"]
    send, recv, thru, _ = started["chip"]

    def finish(names, sums, from_chips):
        halves = [_add_four(sums[m], from_chips[m], geom[n][1], where, n) for m, n in enumerate(names)]
        shards = _sibling_assemble(names, halves, geom)
        grads = {n: shards[m].T if n == "win_t" else shards[m] for m, n in enumerate(names)}
        return {n: _adamw(w_big[n], grads[n], m_big[n], v_big[n], "adamw_" + n) for n in names}

    mixer_sums = add_pairs(MIXER, [g_mixer[n] for n in MIXER], _sibling_exchange(MIXER, g_mixer, geom))
    send2, recv2, thru2, token2 = _chip_exchange_start(MIXER, mixer_sums, geom, f["st2"])
    grad_x = _grad_x(dproj, dz1b, full["win_t"], after=token2)
    upd = finish(FFN, ffn_sums, _chip_exchange_wait(FFN, send, recv, thru, grad_x))
    upd.update(finish(MIXER, mixer_sums, _chip_exchange_wait(MIXER, send2, recv2, thru2, upd[FFN[0]][1])))

    small_names = ("ln1_g", "ln1_b", "ln2_g", "ln2_b", "sinks", "fcw", "scw")
    small_shapes = [g_small[n].shape for n in small_names]
    red = _allreduce_small(_pack([g_small["loss_sq"].reshape(1)] + [g_small[n] for n in small_names],
                                 _rows_for([(1,)] + small_shapes)))
    loss_sq, *gs = _unpack(red, [(1,)] + small_shapes)
    gs = dict(zip(small_names, gs))
    loss = (0.5 / d) * loss_sq[0]
    fw, sw = ffn_conv_w.shape[2], short_conv_w.shape[2]
    gs["fcw"] = lax.dynamic_slice_in_dim(gs["fcw"], chip * fw, fw, axis=1)
    gs["scw"] = lax.dynamic_slice_in_dim(gs["scw"], chip * sw, sw, axis=1)

    sm_w = dict(ln1_g=ln1_g[0], ln1_b=ln1_b[0], ln2_g=ln2_g[0], ln2_b=ln2_b[0], sinks=attn_sinks[0],
                fcw=ffn_conv_w[0], scw=short_conv_w[0])
    sm_m = dict(ln1_g=m_ln1_g[0], ln1_b=m_ln1_b[0], ln2_g=m_ln2_g[0], ln2_b=m_ln2_b[0], sinks=m_attn_sinks[0],
                fcw=m_ffn_conv_w[0], scw=m_short_conv_w[0])
    sm_v = dict(ln1_g=v_ln1_g[0], ln1_b=v_ln1_b[0], ln2_g=v_ln2_g[0], ln2_b=v_ln2_b[0], sinks=v_attn_sinks[0],
                fcw=v_ffn_conv_w[0], scw=v_short_conv_w[0])
    shapes = [sm_w[n].shape for n in small_names]
    rows = _rows_for(shapes)
    packed = [_pack([t[n] for n in small_names], rows) for t in (sm_w, gs, sm_m, sm_v)]
    sm_out = [dict(zip(small_names, _unpack(a, shapes))) for a in _adamw(*packed, "adamw_small")]

    def leaf(kind, name):
        if name in ("w_in", "w_out", "ffn_w_up", "ffn_w_down"):
            key = dict(w_in="win_t", w_out="wout", ffn_w_up="wup", ffn_w_down="wdown")[name]
            return upd[key][kind][None]
        key = dict(attn_sinks="sinks", short_conv_w="scw", ffn_conv_w="fcw").get(name, name)
        return sm_out[kind][key][None]

    order = ("w_in", "attn_sinks", "short_conv_w", "w_out", "ln1_g", "ln1_b", "ffn_w_up", "ffn_conv_w", "ffn_w_down",
             "ln2_g", "ln2_b")
    outs = [loss, grad_x[None]]
    for kind in range(4):
        outs += [leaf(kind, n) for n in order]
    return tuple(outs)
```

```python
import functools

import jax
import jax.numpy as jnp
from jax import lax
from jax.experimental import pallas as pl
from jax.experimental.pallas import tpu as pltpu

F32 = jnp.float32
BF16 = jnp.bfloat16
MESH = pl.DeviceIdType.MESH
ANY = pl.BlockSpec(memory_space=pl.ANY)

HEAD_DIM = 64
N_Q_HEADS = 16
N_KV_HEADS = 2
ATTN_WIDTH = N_Q_HEADS * HEAD_DIM
KV_WIDTH = N_KV_HEADS * HEAD_DIM
BLOCK = 128
ROPE_THETA = 10000.0
LN_EPS = 1e-5
ALPHA = 2.0 ** 0.25
NEG_INF = -1e30
ADAM_LR, ADAM_B1, ADAM_B2, ADAM_EPS, ADAM_WD, ADAM_STEP = 0.001, 0.9, 0.999, 1e-08, 0.01, 10
N_CHIPS = 4
LANES = 128
MXU_DIM = 256
SLAB = 128


def _cp(sem, vmem_mb):
    return pltpu.CompilerParams(dimension_semantics=sem, vmem_limit_bytes=vmem_mb << 20)


def _matmul(a, b, *, mode, m, n, k, tm, tn, tk, out_dtype, name, vmem_mb, a_spec=None, b_spec=None,
            res=None, alpha=1.0, after=None):
    nj, ni, nk = n // tn, m // tm, k // tk
    assert nj * tn == n and ni * tm == m and nk * tk == k, (name, m, n, k, tm, tn, tk)
    if mode == "nn":
        dims = ((1,), (0,))
        a_spec = a_spec or pl.BlockSpec((tm, tk), lambda j, i, kk: (i, kk))
        b_spec = b_spec or pl.BlockSpec((tk, tn), lambda j, i, kk: (kk, j))
    elif mode == "nt":
        dims = ((1,), (1,))
        a_spec = a_spec or pl.BlockSpec((tm, tk), lambda j, i, kk: (i, kk))
        b_spec = b_spec or pl.BlockSpec((tn, tk), lambda j, i, kk: (j, kk))
    else:
        dims = ((0,), (0,))
        a_spec = a_spec or pl.BlockSpec((tk, tm), lambda j, i, kk: (kk, i))
        b_spec = b_spec or pl.BlockSpec((tk, tn), lambda j, i, kk: (kk, j))
    has_res = res is not None
    has_after = after is not None

    def body(*refs):
        refs = refs[1:] if has_after else refs
        a_ref, b_ref = refs[0], refs[1]
        res_ref = refs[2] if has_res else None
        o_ref = refs[2 + has_res]
        part = lax.dot_general(a_ref[...].astype(BF16), b_ref[...].astype(BF16), (dims, ((), ())),
                               preferred_element_type=F32)

        def finish(acc):
            if has_res:
                acc = acc + alpha * res_ref[...].astype(F32)
            o_ref[...] = acc.astype(o_ref.dtype)

        if nk == 1:
            finish(part)
        else:
            acc_ref = refs[3 + has_res]
            kk = pl.program_id(2)

            @pl.when(kk == 0)
            def _():
                acc_ref[...] = part

            @pl.when(kk > 0)
            def _():
                acc_ref[...] += part

            @pl.when(kk == nk - 1)
            def _():
                finish(acc_ref[...])

    in_specs = [a_spec, b_spec]
    args = [a, b]
    if has_res:
        in_specs.append(pl.BlockSpec((tm, tn), lambda j, i, kk: (i, j)))
        args.append(res)
    if has_after:
        in_specs.insert(0, pl.BlockSpec(after.shape, lambda j, i, kk: (0, 0)))
        args.insert(0, after)
    return pl.pallas_call(
        body, name=name, grid=(nj, ni, nk), in_specs=in_specs,
        out_specs=pl.BlockSpec((tm, tn), lambda j, i, kk: (i, j)),
        out_shape=jax.ShapeDtypeStruct((m, n), out_dtype),
        scratch_shapes=[pltpu.VMEM((tm, tn), F32)] if nk > 1 else [],
        compiler_params=_cp(("arbitrary", "arbitrary", "arbitrary"), vmem_mb),
    )(*args)


def _pick(total, want, mult):
    if total <= want:
        return total
    for t in range(want, 0, -1):
        if total % t == 0 and t % mult == 0:
            return t
    return total


def _rope_tables(s):
    half = HEAD_DIM // 2
    inv_freq = ROPE_THETA ** (-jnp.arange(half, dtype=F32) / half)
    ang = jnp.arange(s, dtype=F32)[:, None] * inv_freq[None, :]
    cos = jnp.tile(jnp.cos(ang), (1, LANES // half))
    sin = jnp.tile(jnp.concatenate([-jnp.sin(ang), jnp.sin(ang)], axis=1), (1, LANES // HEAD_DIM))
    return cos, sin


def _rope(x, cos, sin, lo):
    partner = jnp.where(lo, pltpu.roll(x, LANES - HEAD_DIM // 2, 1), pltpu.roll(x, HEAD_DIM // 2, 1))
    return x * cos + partner * sin


def _dot(a, b, dims):
    return lax.dot_general(a, b, (dims, ((), ())), preferred_element_type=F32)


NN, NT, TN = ((1,), (0,)), ((1,), (1,)), ((0,), (0,))


def _kv_variants(t, head_lo):
    r = pltpu.roll(t, HEAD_DIM, 1)
    zero = jnp.zeros_like(t)
    a = (jnp.where(head_lo, t, zero).astype(BF16), jnp.where(head_lo, r, zero).astype(BF16))
    b = (jnp.where(head_lo, zero, r).astype(BF16), jnp.where(head_lo, zero, t).astype(BF16))
    return a, b


PAIRS_PER_KV = N_Q_HEADS // 2 // N_KV_HEADS
STACK = PAIRS_PER_KV * BLOCK


def _stack_pairs(ref, j, fn):
    return jnp.concatenate([fn(ref[:, p * LANES:(p + 1) * LANES])
                            for p in range(j * PAIRS_PER_KV, (j + 1) * PAIRS_PER_KV)], axis=0)


def _sink_row(sink_ref, j, hh):
    col = lax.broadcasted_iota(jnp.int32, (1, STACK), 1)
    heads = [2 * p + hh for p in range(j * PAIRS_PER_KV, (j + 1) * PAIRS_PER_KV)]
    row = jnp.full((1, STACK), sink_ref[0, heads[-1]], F32)
    for t in range(PAIRS_PER_KV - 2, -1, -1):
        row = jnp.where(col < (t + 1) * BLOCK, sink_ref[0, heads[t]], row)
    return row


def _attn_exps(qp, ka, kb, valid, sink_a, sink_b):
    out = []
    for kk, sink in ((ka, sink_a), (kb, sink_b)):
        s = jnp.where(valid, _dot(kk, qp, NT), NEG_INF)
        mx = jnp.maximum(jnp.max(s, axis=0, keepdims=True), sink)
        out.append((jnp.exp(s - mx), jnp.exp(sink - mx)))
    return out


def _attn_common(i, q_ref, k_ref, v_ref, kp_ref, vp_ref, cos_ref, sin_ref, cosp_ref, sinp_ref):
    lane = lax.broadcasted_iota(jnp.int32, (1, LANES), 1)
    lo = (lane % HEAD_DIM) < (HEAD_DIM // 2)
    head_lo = lane < HEAD_DIM
    cos, sin = cos_ref[...], sin_ref[...]
    kc = _rope(k_ref[...].astype(F32), cos, sin, lo)
    kp = _rope(kp_ref[...].astype(F32), cosp_ref[...], sinp_ref[...], lo)
    kext = jnp.concatenate([kp, kc], axis=0)
    vext = jnp.concatenate([vp_ref[...].astype(F32), v_ref[...].astype(F32)], axis=0)
    ka, kb = _kv_variants(kext, head_lo)
    va, vb = _kv_variants(vext, head_lo)
    qi = lax.broadcasted_iota(jnp.int32, (1, STACK), 1) % BLOCK
    kj = lax.broadcasted_iota(jnp.int32, (2 * BLOCK, 1), 0)
    valid = (kj > qi) & (kj <= qi + BLOCK) & ((kj >= BLOCK) | (i > 0))
    cos4 = jnp.concatenate([cos] * PAIRS_PER_KV, axis=0)
    sin4 = jnp.concatenate([sin] * PAIRS_PER_KV, axis=0)
    return lo, head_lo, cos, sin, cos4, sin4, ka, kb, va, vb, valid


def _attn_fwd(proj, sinks, cos, sin, s):
    nb = s // BLOCK
    kcol, vcol = ATTN_WIDTH // LANES, ATTN_WIDTH // LANES + 1

    def body(q_ref, k_ref, v_ref, kp_ref, vp_ref, cos_ref, sin_ref, cosp_ref, sinp_ref, sink_ref, o_ref):
        i = pl.program_id(0)
        lo, head_lo, cs, sn, cs4, sn4, ka, kb, va, vb, valid = _attn_common(
            i, q_ref, k_ref, v_ref, kp_ref, vp_ref, cos_ref, sin_ref, cosp_ref, sinp_ref)
        row = lax.broadcasted_iota(jnp.int32, (16, 1), 0)
        one = jnp.ones((), BF16)
        for j in range(N_KV_HEADS):
            q4 = _stack_pairs(q_ref, j, lambda t: t.astype(F32))
            qp = (_rope(q4, cs4, sn4, lo) * HEAD_DIM ** -0.5).astype(BF16)
            exps = _attn_exps(qp, ka[j], kb[j], valid, _sink_row(sink_ref, j, 0), _sink_row(sink_ref, j, 1))
            outs = []
            for (e, es), vv, mine in zip(exps, (va[j], vb[j]), (head_lo, ~head_lo)):
                ee = jnp.concatenate([e.astype(BF16), jnp.where(row == 0, es, 0.0).astype(BF16)], axis=0)
                tail = jnp.where((row == 0) & ~mine, one, jnp.zeros((), BF16))
                vx = jnp.concatenate([jnp.where(mine, vv, one), tail], axis=0)
                un = _dot(ee, vx, TN)
                outs.append(un / pltpu.roll(un, HEAD_DIM, 1))
            o = jnp.where(head_lo, outs[0], outs[1]).astype(BF16)
            for t in range(PAIRS_PER_KV):
                p = j * PAIRS_PER_KV + t
                o_ref[:, p * LANES:(p + 1) * LANES] = o[t * BLOCK:(t + 1) * BLOCK]

    prev = lambda i: (jnp.maximum(i - 1, 0), 0)
    return pl.pallas_call(
        body, name="attn_fwd", grid=(nb,),
        in_specs=[pl.BlockSpec((BLOCK, ATTN_WIDTH), lambda i: (i, 0)),
                  pl.BlockSpec((BLOCK, LANES), lambda i: (i, kcol)),
                  pl.BlockSpec((BLOCK, LANES), lambda i: (i, vcol)),
                  pl.BlockSpec((BLOCK, LANES), lambda i: (jnp.maximum(i - 1, 0), kcol)),
                  pl.BlockSpec((BLOCK, LANES), lambda i: (jnp.maximum(i - 1, 0), vcol)),
                  pl.BlockSpec((BLOCK, LANES), lambda i: (i, 0)),
                  pl.BlockSpec((BLOCK, LANES), lambda i: (i, 0)),
                  pl.BlockSpec((BLOCK, LANES), prev),
                  pl.BlockSpec((BLOCK, LANES), prev),
                  pl.BlockSpec(memory_space=pltpu.SMEM)],
        out_specs=pl.BlockSpec((BLOCK, ATTN_WIDTH), lambda i: (i, 0)),
        out_shape=jax.ShapeDtypeStruct((s, ATTN_WIDTH), BF16),
        compiler_params=_cp(("arbitrary",), 32),
    )(proj, proj, proj, proj, proj, cos, sin, cos, sin, sinks)


def _attn_bwd(proj, dmix, sinks, cos, sin, s):
    nb = s // BLOCK
    kcol, vcol = ATTN_WIDTH // LANES, ATTN_WIDTH // LANES + 1
    pairs_per_kv = N_Q_HEADS // 2 // N_KV_HEADS

    def body(q_ref, k_ref, v_ref, kp_ref, vp_ref, cos_ref, sin_ref, cosp_ref, sinp_ref, sink_ref, do_ref,
             dq_ref, dk_ref, dv_ref, dsink_ref, ck_ref, cv_ref):
        g = pl.program_id(0)
        i = nb - 1 - g

        @pl.when(g == 0)
        def _():
            ck_ref[...] = jnp.zeros_like(ck_ref)
            cv_ref[...] = jnp.zeros_like(cv_ref)
            dsink_ref[...] = jnp.zeros_like(dsink_ref)

        lo, head_lo, cs, sn, cs4, sn4, ka, kb, va, vb, valid = _attn_common(
            i, q_ref, k_ref, v_ref, kp_ref, vp_ref, cos_ref, sin_ref, cosp_ref, sinp_ref)
        lane = lax.broadcasted_iota(jnp.int32, (1, LANES), 1)
        dk_j, dv_j = [], []
        dsink = jnp.zeros((1, LANES), F32)
        for j in range(N_KV_HEADS):
            q4 = _stack_pairs(q_ref, j, lambda t: t.astype(F32))
            qp = (_rope(q4, cs4, sn4, lo) * HEAD_DIM ** -0.5).astype(BF16)
            exps = _attn_exps(qp, ka[j], kb[j], valid, _sink_row(sink_ref, j, 0), _sink_row(sink_ref, j, 1))
            do = _stack_pairs(do_ref, j, lambda t: t)
            dq_r = jnp.zeros((STACK, LANES), F32)
            dkc, dvc = [], []
            for hh, ((e, es), kk, vv) in enumerate(zip(exps, (ka[j], kb[j]), (va[j], vb[j]))):
                inv = 1.0 / (jnp.sum(e, axis=0, keepdims=True) + es)
                pr = e * inv
                dp = _dot(vv, do, NT)
                delta = jnp.sum(pr * dp, axis=0, keepdims=True)
                ds = (pr * (dp - delta)).astype(BF16)
                psd = es * inv * delta
                for t in range(PAIRS_PER_KV):
                    head = 2 * (j * PAIRS_PER_KV + t) + hh
                    dsink = dsink + jnp.where(
                        lane == head, -jnp.sum(psd[:, t * BLOCK:(t + 1) * BLOCK], axis=1, keepdims=True), 0.0)
                dq_r = dq_r + _dot(ds, kk, TN)
                dkc.append(_dot(ds, qp, NN))
                dvc.append(_dot(pr.astype(BF16), do, NN))
            dk_j.append(jnp.where(head_lo, dkc[0], dkc[1]))
            dv_j.append(jnp.where(head_lo, dvc[0], dvc[1]))
            dq = _rope(dq_r * HEAD_DIM ** -0.5, cs4, -sn4, lo).astype(BF16)
            for t in range(PAIRS_PER_KV):
                p = j * PAIRS_PER_KV + t
                dq_ref[:, p * LANES:(p + 1) * LANES] = dq[t * BLOCK:(t + 1) * BLOCK]
        tot_k = [t + pltpu.roll(t, HEAD_DIM, 1) for t in dk_j]
        tot_v = [t + pltpu.roll(t, HEAD_DIM, 1) for t in dv_j]
        dkext = jnp.where(head_lo, tot_k[0], tot_k[1])
        dvext = jnp.where(head_lo, tot_v[0], tot_v[1])
        dk_r = dkext[BLOCK:] + ck_ref[...]
        dk_ref[...] = _rope(dk_r, cs, -sn, lo).astype(BF16)
        dv_ref[...] = (dvext[BLOCK:] + cv_ref[...]).astype(BF16)
        ck_ref[...] = dkext[:BLOCK]
        cv_ref[...] = dvext[:BLOCK]
        dsink_ref[0:1, :] += dsink

    cur = lambda col: (lambda g: (nb - 1 - g, col))
    prv = lambda col: (lambda g: (jnp.maximum(nb - 2 - g, 0), col))
    blk = lambda w, f: pl.BlockSpec((BLOCK, w), f)
    return pl.pallas_call(
        body, name="attn_bwd", grid=(nb,),
        in_specs=[blk(ATTN_WIDTH, cur(0)), blk(LANES, cur(kcol)), blk(LANES, cur(vcol)),
                  blk(LANES, prv(kcol)), blk(LANES, prv(vcol)),
                  blk(LANES, cur(0)), blk(LANES, cur(0)), blk(LANES, prv(0)), blk(LANES, prv(0)),
                  pl.BlockSpec(memory_space=pltpu.SMEM),
                  blk(ATTN_WIDTH, cur(0))],
        out_specs=[blk(ATTN_WIDTH, cur(0)), blk(LANES, cur(0)), blk(LANES, cur(0)),
                   pl.BlockSpec((8, LANES), lambda g: (0, 0))],
        out_shape=[jax.ShapeDtypeStruct((s, ATTN_WIDTH), BF16), jax.ShapeDtypeStruct((s, LANES), BF16),
                   jax.ShapeDtypeStruct((s, LANES), BF16), jax.ShapeDtypeStruct((8, LANES), F32)],
        scratch_shapes=[pltpu.VMEM((BLOCK, LANES), F32), pltpu.VMEM((BLOCK, LANES), F32)],
        compiler_params=_cp(("arbitrary",), 32),
    )(proj, proj, proj, proj, proj, cos, sin, cos, sin, sinks, dmix)


def _causal_conv(x, prev8, w):
    row = lax.broadcasted_iota(jnp.int32, (8, 1), 0)
    r1, r2 = pltpu.roll(x, 1, 0), pltpu.roll(x, 2, 0)
    s1 = jnp.concatenate([jnp.where(row == 0, prev8[7:8], r1[:8]), r1[8:]], axis=0)
    s2 = jnp.concatenate([jnp.where(row == 0, prev8[6:7], jnp.where(row == 1, prev8[7:8], r2[:8])), r2[8:]], axis=0)
    return w[0:1] * s2 + w[1:2] * s1 + w[2:3] * x


def _conv_bwd(dy, x, w, next8):
    r = x.shape[0]
    row = lax.broadcasted_iota(jnp.int32, (8, 1), 0)
    r1, r2 = pltpu.roll(dy, r - 1, 0), pltpu.roll(dy, r - 2, 0)
    n1 = jnp.concatenate([r1[:r - 8], jnp.where(row == 7, next8[0:1], r1[r - 8:])], axis=0)
    n2 = jnp.concatenate([r2[:r - 8], jnp.where(row == 6, next8[0:1], jnp.where(row == 7, next8[1:2], r2[r - 8:]))],
                         axis=0)
    dx = w[2:3] * dy + w[1:2] * n1 + w[0:1] * n2
    dws = [jnp.sum(t * x, axis=0, keepdims=True) for t in (n2, n1, dy)]
    return dx, dws


CONV_COLS = 256


def _convmix_cols(d):
    conv_w = d - ATTN_WIDTH
    base = (ATTN_WIDTH + 2 * KV_WIDTH) // CONV_COLS
    step = conv_w // CONV_COLS
    return base, base + step, base + 2 * step, step


def _convmix_fwd(proj, scw8, s, d):
    gb0, gc0, h0, ncb = _convmix_cols(d)
    tr = _pick(s, 1024, 16)
    ni = s // tr

    def body(gb_ref, gc_ref, h_ref, w_ref, o_ref, carry_ref):
        @pl.when(pl.program_id(1) == 0)
        def _():
            carry_ref[...] = jnp.zeros_like(carry_ref)

        gch = gc_ref[...].astype(F32) * h_ref[...].astype(F32)
        cc = _causal_conv(gch, carry_ref[...], w_ref[...])
        o_ref[...] = (gb_ref[...].astype(F32) * cc).astype(BF16)
        carry_ref[...] = gch[tr - 8:]

    spec = lambda c0: pl.BlockSpec((tr, CONV_COLS), lambda j, i: (i, c0 + j))
    return pl.pallas_call(
        body, name="convmix_fwd", grid=(ncb, ni),
        in_specs=[spec(gb0), spec(gc0), spec(h0), pl.BlockSpec((8, CONV_COLS), lambda j, i: (0, j))],
        out_specs=pl.BlockSpec((tr, CONV_COLS), lambda j, i: (i, j)),
        out_shape=jax.ShapeDtypeStruct((s, d - ATTN_WIDTH), BF16),
        scratch_shapes=[pltpu.VMEM((8, CONV_COLS), F32)],
        compiler_params=_cp(("arbitrary", "arbitrary"), 32),
    )(proj, proj, proj, scw8)


def _convmix_bwd(proj, dmix, scw8, s, d):
    gb0, gc0, h0, ncb = _convmix_cols(d)
    tr = _pick(s, 1024, 16)
    ni = s // tr
    dc0 = ATTN_WIDTH // CONV_COLS

    def body(dc_ref, gb_ref, gc_ref, h_ref, gcp_ref, hp_ref, w_ref, d3_ref, dw_ref, nxt_ref):
        g = pl.program_id(1)
        i = ni - 1 - g

        @pl.when(g == 0)
        def _():
            nxt_ref[...] = jnp.zeros_like(nxt_ref)
            dw_ref[...] = jnp.zeros_like(dw_ref)

        w = w_ref[...]
        gb, gc, h = gb_ref[...].astype(F32), gc_ref[...].astype(F32), h_ref[...].astype(F32)
        gch = gc * h
        prev8 = (gcp_ref[...].astype(F32) * hp_ref[...].astype(F32))[8:16] * (i > 0).astype(F32)
        cc = _causal_conv(gch, prev8, w)
        dc = dc_ref[...].astype(F32)
        dcc = dc * gb
        dgch, dws = _conv_bwd(dcc, gch, w, nxt_ref[...])
        d3_ref[0] = (dc * cc).astype(BF16)
        d3_ref[1] = (dgch * h).astype(BF16)
        d3_ref[2] = (dgch * gc).astype(BF16)
        for t in range(3):
            dw_ref[t:t + 1, :] += dws[t]
        nxt_ref[...] = dcc[0:8]

    cur = lambda c0: pl.BlockSpec((tr, CONV_COLS), lambda j, g: (ni - 1 - g, c0 + j))
    prv = lambda c0: pl.BlockSpec((16, CONV_COLS), lambda j, g: (jnp.maximum((ni - 1 - g) * (tr // 16) - 1, 0), c0 + j))
    return pl.pallas_call(
        body, name="convmix_bwd", grid=(ncb, ni),
        in_specs=[cur(dc0), cur(gb0), cur(gc0), cur(h0), prv(gc0), prv(h0),
                  pl.BlockSpec((8, CONV_COLS), lambda j, g: (0, j))],
        out_specs=[pl.BlockSpec((3, tr, CONV_COLS), lambda j, g: (0, ni - 1 - g, j)),
                   pl.BlockSpec((8, CONV_COLS), lambda j, g: (0, j))],
        out_shape=[jax.ShapeDtypeStruct((3, s, d - ATTN_WIDTH), BF16), jax.ShapeDtypeStruct((8, d - ATTN_WIDTH), F32)],
        scratch_shapes=[pltpu.VMEM((8, CONV_COLS), F32)],
        compiler_params=_cp(("arbitrary", "arbitrary"), 32),
    )(dmix, proj, proj, proj, proj, proj, scw8)


def _ln_fwd(z):
    mu = jnp.mean(z, axis=-1, keepdims=True)
    zc = z - mu
    var = jnp.mean(zc * zc, axis=-1, keepdims=True)
    rstd = lax.rsqrt(var + LN_EPS)
    return zc * rstd, rstd


def _ln_bwd(dout, xh, rstd, g):
    dxh = dout * g
    c1 = jnp.mean(dxh, axis=-1, keepdims=True)
    c2 = jnp.mean(dxh * xh, axis=-1, keepdims=True)
    dz = rstd * (dxh - c1 - xh * c2)
    return dz, jnp.sum(dout * xh, axis=0, keepdims=True), jnp.sum(dout, axis=0, keepdims=True)


def _outproj_ln1(attn, conv, wout, x, g1, b1, s, d):
    tm = _pick(s, 512, 16)
    ka = attn.shape[1]
    one_buffer = pl.Buffered(1)

    def body(a_ref, c_ref, wt_ref, wb_ref, x_ref, g_ref, b_ref, x1_ref, x1b_ref, xh_ref, rs_ref):
        y = _dot(a_ref[...], wt_ref[...], NN) + _dot(c_ref[...], wb_ref[...], NN)
        xh, rstd = _ln_fwd(ALPHA * x_ref[...] + y)
        x1 = xh * g_ref[...] + b_ref[...]
        x1_ref[...] = x1
        x1b_ref[...] = x1.astype(BF16)
        xh_ref[...] = xh.astype(BF16)
        rs_ref[...] = rstd

    row = lambda w: pl.BlockSpec((tm, w), lambda i: (i, 0))
    vec = pl.BlockSpec((1, d), lambda i: (0, 0))
    return pl.pallas_call(
        body, name="outproj_ln1", grid=(s // tm,),
        in_specs=[row(ka), row(d - ka), pl.BlockSpec((ka, d), lambda i: (0, 0), pipeline_mode=one_buffer),
                  pl.BlockSpec((d - ka, d), lambda i: (ka // (d - ka), 0), pipeline_mode=one_buffer), row(d), vec, vec],
        out_specs=[row(d), row(d), row(d), row(1)],
        out_shape=[jax.ShapeDtypeStruct((s, d), F32), jax.ShapeDtypeStruct((s, d), BF16),
                   jax.ShapeDtypeStruct((s, d), BF16), jax.ShapeDtypeStruct((s, 1), F32)],
        compiler_params=_cp(("arbitrary",), 56),
    )(attn, conv, wout, wout, x, g1, b1)


def _ffn_up(x1b, wup, fcw8, s, d, dff):
    tm = _pick(s, 1024, 16)
    tn = _pick(dff, 512, LANES)
    nj, ni = dff // tn, s // tm

    def body(x_ref, wa_ref, wg_ref, ca_ref, cg_ref, u_ref, y_ref, h_ref, carry_ref):
        @pl.when(pl.program_id(1) == 0)
        def _():
            carry_ref[...] = jnp.zeros_like(carry_ref)

        xa = x_ref[...]
        ys = []
        for part, (w_ref, c_ref) in enumerate(((wa_ref, ca_ref), (wg_ref, cg_ref))):
            ub = _dot(xa, w_ref[...], NN).astype(BF16)
            u_ref[part] = ub
            u = ub.astype(F32)
            y = _causal_conv(u, carry_ref[part], c_ref[...])
            carry_ref[part] = u[tm - 8:]
            yb = y.astype(BF16)
            y_ref[part] = yb
            ys.append(yb.astype(F32))
        a2, g2 = ys
        sig = 1.0 / (1.0 + jnp.exp(-a2))
        h_ref[...] = (a2 * sig * g2).astype(BF16)

    return pl.pallas_call(
        body, name="ffn_up", grid=(nj, ni),
        in_specs=[pl.BlockSpec((tm, d), lambda j, i: (i, 0)),
                  pl.BlockSpec((d, tn), lambda j, i: (0, j)),
                  pl.BlockSpec((d, tn), lambda j, i: (0, j + nj)),
                  pl.BlockSpec((8, tn), lambda j, i: (0, j)),
                  pl.BlockSpec((8, tn), lambda j, i: (0, j + nj))],
        out_specs=[pl.BlockSpec((2, tm, tn), lambda j, i: (0, i, j)),
                   pl.BlockSpec((2, tm, tn), lambda j, i: (0, i, j)),
                   pl.BlockSpec((tm, tn), lambda j, i: (i, j))],
        out_shape=[jax.ShapeDtypeStruct((2, s, dff), BF16), jax.ShapeDtypeStruct((2, s, dff), BF16),
                   jax.ShapeDtypeStruct((s, dff), BF16)],
        scratch_shapes=[pltpu.VMEM((2, 8, tn), F32)],
        compiler_params=_cp(("arbitrary", "arbitrary"), 56),
    )(x1b, wup, wup, fcw8, fcw8)


def _ffn_mid_bwd(dz2b, wdown, u3, y3, fcw8, s, d, dff):
    tm = _pick(s, 1024, 16)
    tn = _pick(dff, 512, LANES)
    nj, ni = dff // tn, s // tm

    def body(dz_ref, wd_ref, u_ref, y_ref, ca_ref, cg_ref, du_ref, dw_ref, nxt_ref):
        @pl.when(pl.program_id(1) == 0)
        def _():
            nxt_ref[...] = jnp.zeros_like(nxt_ref)
            dw_ref[...] = jnp.zeros_like(dw_ref)

        a2, g2 = y_ref[0].astype(F32), y_ref[1].astype(F32)
        sig = 1.0 / (1.0 + jnp.exp(-a2))
        silu = a2 * sig
        dhv = _dot(dz_ref[...], wd_ref[...], NT)
        dys = (dhv * g2 * (sig * (1.0 + a2 * (1.0 - sig))), dhv * silu)
        for part, (c_ref, dy) in enumerate(zip((ca_ref, cg_ref), dys)):
            dx, dws = _conv_bwd(dy, u_ref[part].astype(F32), c_ref[...], nxt_ref[part])
            du_ref[part] = dx.astype(BF16)
            for t in range(3):
                dw_ref[part, t:t + 1, :] += dws[t]
            nxt_ref[part] = dy[0:8]

    return pl.pallas_call(
        body, name="ffn_mid_bwd", grid=(nj, ni),
        in_specs=[pl.BlockSpec((tm, d), lambda j, g: (ni - 1 - g, 0)),
                  pl.BlockSpec((tn, d), lambda j, g: (j, 0)),
                  pl.BlockSpec((2, tm, tn), lambda j, g: (0, ni - 1 - g, j)),
                  pl.BlockSpec((2, tm, tn), lambda j, g: (0, ni - 1 - g, j)),
                  pl.BlockSpec((8, tn), lambda j, g: (0, j)),
                  pl.BlockSpec((8, tn), lambda j, g: (0, j + nj))],
        out_specs=[pl.BlockSpec((2, tm, tn), lambda j, g: (0, ni - 1 - g, j)),
                   pl.BlockSpec((2, 8, tn), lambda j, g: (0, 0, j))],
        out_shape=[jax.ShapeDtypeStruct((2, s, dff), BF16), jax.ShapeDtypeStruct((2, 8, dff), F32)],
        scratch_shapes=[pltpu.VMEM((2, 8, tn), F32)],
        compiler_params=_cp(("arbitrary", "arbitrary"), 56),
    )(dz2b, wdown, u3, y3, fcw8, fcw8)


def _ffn_down_loss(hmid, wdown, x1, target, g2, b2, s, d, dff):
    tm = _pick(s, 512, SLAB)
    tk = _pick(dff, 1408, LANES)
    ni, nk = s // tm, dff // tk
    slab = min(SLAB, tm)

    def body(h_ref, w_ref, x1_ref, t_ref, g_ref, b_ref, dzb_ref, st_ref, acc_ref):
        i, kk = pl.program_id(0), pl.program_id(1)

        @pl.when((i == 0) & (kk == 0))
        def _():
            st_ref[...] = jnp.zeros_like(st_ref)

        part = _dot(h_ref[...], w_ref[...], NN)

        @pl.when(kk == 0)
        def _():
            acc_ref[...] = part

        @pl.when(kk > 0)
        def _():
            acc_ref[...] += part

        @pl.when(kk == nk - 1)
        def _():
            g, b = g_ref[...], b_ref[...]

            def one(sl, carry):
                rows = pl.ds(pl.multiple_of(sl * slab, slab), slab)
                xh, rstd = _ln_fwd(ALPHA * x1_ref[rows, :] + acc_ref[rows, :])
                diff = xh * g + b - t_ref[rows, :]
                sq = jnp.sum(jnp.sum(diff * diff, axis=1, keepdims=True), axis=0, keepdims=True)
                dz, dg, db = _ln_bwd(diff * (1.0 / d), xh, rstd, g)
                dzb_ref[rows, :] = dz.astype(BF16)
                st_ref[0:1, :] += dg
                st_ref[1:2, :] += db
                st_ref[2:3, :] += sq
                return carry

            lax.fori_loop(0, tm // slab, one, 0)

    row = pl.BlockSpec((tm, d), lambda i, kk: (i, 0))
    vec = pl.BlockSpec((1, d), lambda i, kk: (0, 0))
    return pl.pallas_call(
        body, name="ffn_down_loss", grid=(ni, nk),
        in_specs=[pl.BlockSpec((tm, tk), lambda i, kk: (i, kk)), pl.BlockSpec((tk, d), lambda i, kk: (kk, 0)),
                  row, row, vec, vec],
        out_specs=[row, pl.BlockSpec((8, d), lambda i, kk: (0, 0))],
        out_shape=[jax.ShapeDtypeStruct((s, d), BF16), jax.ShapeDtypeStruct((8, d), F32)],
        scratch_shapes=[pltpu.VMEM((tm, d), F32)],
        compiler_params=_cp(("arbitrary", "arbitrary"), 48),
    )(hmid, wdown, x1, target, g2, b2)


def _ffn_dx_ln1_bwd(du3, wup, dz2b, xh1, rstd1, g1, s, d, dff):
    tm = _pick(s, 512, SLAB)
    tk = _pick(dff, 2816, MXU_DIM)
    nkh = dff // tk
    ni, nk = s // tm, 2 * nkh
    slab = min(SLAB, tm)

    def body(a_ref, w_ref, dz2_ref, xh_ref, rs_ref, g_ref, dzb_ref, st_ref, acc_ref):
        i, kk = pl.program_id(0), pl.program_id(1)

        @pl.when((i == 0) & (kk == 0))
        def _():
            st_ref[...] = jnp.zeros_like(st_ref)

        part = _dot(a_ref[...], w_ref[...], NT)

        @pl.when(kk == 0)
        def _():
            acc_ref[...] = part

        @pl.when(kk > 0)
        def _():
            acc_ref[...] += part

        @pl.when(kk == nk - 1)
        def _():
            g = g_ref[...]

            def one(sl, carry):
                rows = pl.ds(pl.multiple_of(sl * slab, slab), slab)
                dx1 = ALPHA * dz2_ref[rows, :].astype(F32) + acc_ref[rows, :]
                dz, dg, db = _ln_bwd(dx1, xh_ref[rows, :].astype(F32), rs_ref[rows, :], g)
                dzb_ref[rows, :] = dz.astype(BF16)
                st_ref[0:1, :] += dg
                st_ref[1:2, :] += db
                return carry

            lax.fori_loop(0, tm // slab, one, 0)

    row = pl.BlockSpec((tm, d), lambda i, kk: (i, 0))
    row1 = pl.BlockSpec((tm, d), lambda i, kk: (i, 0), pipeline_mode=pl.Buffered(1))
    return pl.pallas_call(
        body, name="ffn_dx_ln1_bwd", grid=(ni, nk),
        in_specs=[pl.BlockSpec((None, tm, tk), lambda i, kk: (kk // nkh, i, kk % nkh)),
                  pl.BlockSpec((d, tk), lambda i, kk: (0, kk)),
                  row1, row1, pl.BlockSpec((tm, 1), lambda i, kk: (i, 0)), pl.BlockSpec((1, d), lambda i, kk: (0, 0))],
        out_specs=[row, pl.BlockSpec((8, d), lambda i, kk: (0, 0))],
        out_shape=[jax.ShapeDtypeStruct((s, d), BF16), jax.ShapeDtypeStruct((8, d), F32)],
        scratch_shapes=[pltpu.VMEM((tm, d), F32)],
        compiler_params=_cp(("arbitrary", "arbitrary"), 56),
    )(du3, wup, dz2b, xh1, rstd1, g1)


def _phase_mixer(x, win_t, wout, scw8, sinks, ln1_g, ln1_b, after=None):
    s, d = x.shape
    n_in = win_t.shape[0]
    cos, sin = _rope_tables(s)
    proj = _matmul(x, win_t, mode="nt", m=s, n=n_in, k=d, tm=_pick(s, 512, 16), tn=n_in, tk=d, out_dtype=BF16,
                   name="in_proj", vmem_mb=52, after=after,
                   b_spec=pl.BlockSpec((n_in, d), lambda j, i, kk: (0, 0), pipeline_mode=pl.Buffered(1)))
    attn = _attn_fwd(proj, sinks, cos, sin, s)
    conv = _convmix_fwd(proj, scw8, s, d)
    x1, x1b, xh1, rstd1 = _outproj_ln1(attn, conv, wout, x, ln1_g, ln1_b, s, d)
    return dict(x=x, cos=cos, sin=sin, proj=proj, attn=attn, conv=conv, x1=x1, x1b=x1b, xh1=xh1, rstd1=rstd1)


def _phase_ffn(a, target, wup, wdown, fcw8, ln2_g, ln2_b):
    x1, x1b = a["x1"], a["x1b"]
    s, d = x1.shape
    dff = wdown.shape[0]
    u3, y3, hmid = _ffn_up(x1b, wup, fcw8, s, d, dff)
    dz2b, st2 = _ffn_down_loss(hmid, wdown, x1, target, ln2_g, ln2_b, s, d, dff)

    g_wdown = _matmul(hmid, dz2b, mode="tn", m=dff, n=d, k=s, tm=_pick(dff, 512, MXU_DIM), tn=_pick(d, 512, MXU_DIM),
                      tk=s, out_dtype=BF16, name="grad_w_down", vmem_mb=56)
    du3, dfcw = _ffn_mid_bwd(dz2b, wdown, u3, y3, fcw8, s, d, dff)
    tnu = _pick(dff, 512, MXU_DIM)
    njh = dff // tnu
    g_wup = _matmul(x1b, du3, mode="tn", m=d, n=2 * dff, k=s, tm=_pick(d, 512, LANES), tn=tnu, tk=s, out_dtype=BF16,
                    name="grad_w_up", vmem_mb=56,
                    b_spec=pl.BlockSpec((None, s, tnu), lambda j, i, kk: (j // njh, kk, j % njh)))
    return dict(du3=du3, dz2b=dz2b, st2=st2, dfcw=dfcw, wdown=g_wdown, wup=g_wup)


def _phase_rest(a, f, wup, wout, win_t, scw8, sinks, ln1_g, between=None):
    xb, cos, sin, proj, attn, conv = a["x"], a["cos"], a["sin"], a["proj"], a["attn"], a["conv"]
    du3, dz2b, st2, dfcw = f["du3"], f["dz2b"], f["st2"], f["dfcw"]
    s, d = a["x1"].shape
    dff = wup.shape[1] // 2
    n_in = win_t.shape[0]
    ts = _pick(s, 2048, 16)
    dz1b, st1 = _ffn_dx_ln1_bwd(du3, wup, dz2b, a["xh1"], a["rstd1"], ln1_g, s, d, dff)
    after = between(dz1b) if between is not None else None

    mix = jnp.concatenate([attn, conv], axis=1)
    g_wout = _matmul(mix, dz1b, mode="tn", m=d, n=d, k=s, tm=_pick(d, 512, LANES), tn=_pick(d, 512, LANES), tk=s,
                     out_dtype=BF16, name="grad_w_out", vmem_mb=56, after=after)
    dmix = _matmul(dz1b, wout, mode="nt", m=s, n=d, k=d, tm=_pick(s, 1024, 16), tn=_pick(d, 1024, LANES), tk=d,
                   out_dtype=BF16, name="out_dmix", vmem_mb=48, after=after)
    d3, dscw = _convmix_bwd(proj, dmix, scw8, s, d)
    dq, dk, dv, dsink = _attn_bwd(proj, dmix, sinks, cos, sin, s)
    dproj = jnp.concatenate([dq, dk, dv, d3[0], d3[1], d3[2]], axis=1)
    g_win_t = _matmul(dproj, xb, mode="tn", m=n_in, n=d, k=s, tm=_pick(n_in, 2176, LANES), tn=_pick(d, 512, LANES),
                      tk=ts, out_dtype=BF16, name="grad_w_in", vmem_mb=48)
    small = dict(loss_sq=st2[2, 0], ln2_g=st2[0], ln2_b=st2[1], ln1_g=st1[0], ln1_b=st1[1], sinks=dsink[0, :N_Q_HEADS],
                 fcw=jnp.concatenate([dfcw[0, :3], dfcw[1, :3]], axis=1), scw=dscw[:3])
    return (dproj, dz1b), dict(win_t=g_win_t, wout=g_wout), small


def _grad_x(dproj, dz1b, win_t, after=None):
    s, n_in = dproj.shape
    d = win_t.shape[1]
    return _matmul(dproj, win_t, mode="nn", m=s, n=d, k=n_in, tm=_pick(s, 512, 16), tn=_pick(d, 1024, LANES),
                   tk=n_in, out_dtype=F32, name="grad_x", vmem_mb=56, res=dz1b, alpha=ALPHA, after=after)


def _local_step(x, target, win_t, wout, wup, wdown, scw8, fcw8, sinks, ln1_g, ln1_b, ln2_g, ln2_b):
    a = _phase_mixer(x, win_t, wout, scw8, sinks, ln1_g, ln1_b)
    f = _phase_ffn(a, target, wup, wdown, fcw8, ln2_g, ln2_b)
    (dproj, dz1b), g, small = _phase_rest(a, f, wup, wout, win_t, scw8, sinks, ln1_g)
    return _grad_x(dproj, dz1b, win_t), dict(g, wup=f["wup"], wdown=f["wdown"]), small


MIXER = ("win_t", "wout")
FFN = ("wup", "wdown")
BIG = MIXER + FFN


def _geom(shard_shapes):
    out = {}
    for name in BIG:
        r, c = shard_shapes[name]
        out[name] = ("col" if name == "wup" else "row", (r, c), (r // 2, c))
    return out


def _full_shape(kind, shard):
    r, c = shard
    return (N_CHIPS * r, c) if kind == "row" else (r, N_CHIPS * c)


def _piece_of(ref, kind, shard, chip, half):
    r, c = shard
    if kind == "row":
        return ref.at[pl.ds(chip * r + half * (r // 2), r // 2), :]
    return ref.at[pl.ds(half * (r // 2), r // 2), pl.ds(chip * c, c)]


def _shard_piece(ref, shard, half):
    r, _ = shard
    return ref.at[pl.ds(half * (r // 2), r // 2), :]


def _me():
    return lax.axis_index("x"), lax.axis_index("y"), lax.axis_index("c")


def _other_chips(x, y):
    return [(1 - x, y), (x, 1 - y), (1 - x, 1 - y)]


def _remote(src, dst, send_sem, recv_sem, dev):
    return pltpu.make_async_remote_copy(src_ref=src, dst_ref=dst, send_sem=send_sem, recv_sem=recv_sem,
                                        device_id=dev, device_id_type=MESH)


def _place_shard(w, chip1, kind, name):
    r, c = w.shape
    tr = _rows_tile(r, c, 16)
    nt = r // tr

    def body(chip_ref, w_ref, o_ref):
        o_ref[...] = w_ref[...].astype(BF16)

    out_map = (lambda i, chip_ref: (chip_ref[0] * nt + i, 0)) if kind == "row" else (lambda i, chip_ref: (i, chip_ref[0]))
    return pl.pallas_call(
        body, name="place_" + name,
        grid_spec=pltpu.PrefetchScalarGridSpec(
            num_scalar_prefetch=1, grid=(nt,),
            in_specs=[pl.BlockSpec((tr, c), lambda i, chip_ref: (i, 0))],
            out_specs=pl.BlockSpec((tr, c), out_map)),
        out_shape=jax.ShapeDtypeStruct(_full_shape(kind, (r, c)), BF16),
        compiler_params=_cp(("arbitrary",), 32),
    )(chip1, w)


def _allgather_weights(names, placed, geom, small_shards):
    nb, ns = len(names), len(small_shards)
    small_w = [a.shape[1] for a in small_shards]

    def body(*refs):
        sm = refs[nb:nb + ns]
        full = refs[nb + ns:2 * nb + ns]
        smf = refs[2 * nb + ns:2 * nb + 2 * ns]
        send, recv, loc = refs[2 * nb + 2 * ns:]
        x, y, c = _me()
        chip = 2 * x + y
        sib = (x, y, 1 - c)
        others = _other_chips(x, y)
        locals_, sends = [], []
        for m, name in enumerate(names):
            kind, shard, _ = geom[name]
            mine = _piece_of(full[m], kind, shard, chip, c)
            for k, (qx, qy) in enumerate(others):
                cp = _remote(mine, mine, send.at[6 * m + k], recv.at[6 * m + k], (qx, qy, c))
                cp.start()
                sends.append(cp)
        for t in range(ns):
            cp = pltpu.make_async_copy(sm[t], smf[t].at[:, pl.ds(chip * small_w[t], small_w[t])], loc.at[t])
            cp.start()
            locals_.append(cp)
            for k, (qx, qy) in enumerate(others):
                cp = _remote(sm[t], smf[t].at[:, pl.ds(chip * small_w[t], small_w[t])],
                             send.at[6 * nb + 3 * t + k], recv.at[6 * nb + 3 * t + k], (qx, qy, c))
                cp.start()
                sends.append(cp)
        for m, name in enumerate(names):
            kind, shard, _ = geom[name]
            for k, (qx, qy) in enumerate(others):
                got = _piece_of(full[m], kind, shard, 2 * qx + qy, c)
                _remote(got, got, send.at[6 * m + k], recv.at[6 * m + k], (qx, qy, c)).wait_recv()
                cp = _remote(got, got, send.at[6 * m + 3 + k], recv.at[6 * m + 3 + k], sib)
                cp.start()
                sends.append(cp)
        for t in range(ns):
            for k, (qx, qy) in enumerate(others):
                got = smf[t].at[:, pl.ds((2 * qx + qy) * small_w[t], small_w[t])]
                _remote(got, got, send.at[6 * nb + 3 * t + k], recv.at[6 * nb + 3 * t + k], (qx, qy, c)).wait_recv()
        for m, name in enumerate(names):
            kind, shard, _ = geom[name]
            for k, (qx, qy) in enumerate(others):
                got = _piece_of(full[m], kind, shard, 2 * qx + qy, 1 - c)
                _remote(got, got, send.at[6 * m + 3 + k], recv.at[6 * m + 3 + k], sib).wait_recv()
        for cp in sends:
            cp.wait_send()
        for cp in locals_:
            cp.wait()

    nsem = 6 * nb + 3 * ns
    out_shape = [jax.ShapeDtypeStruct(placed[n].shape, BF16) for n in names]
    out_shape += [jax.ShapeDtypeStruct((8, N_CHIPS * w), F32) for w in small_w]
    outs = pl.pallas_call(
        body, name="allgather_weights", in_specs=[ANY] * (nb + ns), out_specs=[ANY] * (nb + ns), out_shape=out_shape,
        input_output_aliases={m: m for m in range(nb)},
        scratch_shapes=[pltpu.SemaphoreType.DMA((nsem,)), pltpu.SemaphoreType.DMA((nsem,)),
                        pltpu.SemaphoreType.DMA((ns,))],
    )(*[placed[n] for n in names], *small_shards)
    return dict(zip(names, outs[:nb])), list(outs[nb:])


def _sibling_exchange(names, grads, geom):
    nb = len(names)

    def body(*refs):
        g = refs[:nb]
        got = refs[nb:2 * nb]
        send, recv = refs[2 * nb:]
        x, y, c = _me()
        sib = (x, y, 1 - c)
        cps = []
        for m, name in enumerate(names):
            kind, shard, _ = geom[name]
            for r in range(N_CHIPS):
                cp = _remote(_piece_of(g[m], kind, shard, r, 1 - c), got[m].at[r],
                             send.at[N_CHIPS * m + r], recv.at[N_CHIPS * m + r], sib)
                cp.start()
                cps.append(cp)
        for cp in cps:
            cp.wait_recv()
        for cp in cps:
            cp.wait_send()

    return pl.pallas_call(
        body, name="grad_sibling_exchange_" + names[0], in_specs=[ANY] * nb, out_specs=[ANY] * nb,
        out_shape=[jax.ShapeDtypeStruct((N_CHIPS,) + geom[n][2], BF16) for n in names],
        scratch_shapes=[pltpu.SemaphoreType.DMA((N_CHIPS * nb,)), pltpu.SemaphoreType.DMA((N_CHIPS * nb,))],
    )(*[grads[n] for n in names])


def _sibling_assemble(names, shards, geom):
    nb = len(names)

    def body(*refs):
        full = refs[nb:2 * nb]
        send, recv = refs[2 * nb:]
        x, y, c = _me()
        sib = (x, y, 1 - c)
        cps = []
        for m, name in enumerate(names):
            mine = _shard_piece(full[m], geom[name][1], c)
            cp = _remote(mine, mine, send.at[m], recv.at[m], sib)
            cp.start()
            cps.append(cp)
        for m, name in enumerate(names):
            theirs = _shard_piece(full[m], geom[name][1], 1 - c)
            _remote(theirs, theirs, send.at[m], recv.at[m], sib).wait_recv()
        for cp in cps:
            cp.wait_send()

    return pl.pallas_call(
        body, name="grad_sibling_assemble_" + names[0], in_specs=[ANY] * nb, out_specs=[ANY] * nb,
        out_shape=[jax.ShapeDtypeStruct(geom[n][1], F32) for n in names],
        input_output_aliases={m: m for m in range(nb)},
        scratch_shapes=[pltpu.SemaphoreType.DMA((nb,)), pltpu.SemaphoreType.DMA((nb,))],
    )(*shards)


HBM = pl.BlockSpec(memory_space=pltpu.HBM)
SEM = pl.BlockSpec(memory_space=pltpu.SEMAPHORE)
EFFECT = pltpu.SideEffectType.DATAFLOW_SIDE_EFFECTING
TOKEN = jax.ShapeDtypeStruct((8, LANES), F32)


def _hbm(a):
    return pltpu.with_memory_space_constraint(a, pltpu.HBM)


def _gather_copies(names, full, geom, send, recv):
    x, y, c = _me()
    out = []
    for m, name in enumerate(names):
        kind, shard, _ = geom[name]
        mine = _piece_of(full[m], kind, shard, 2 * x + y, c)
        for k, (qx, qy) in enumerate(_other_chips(x, y)):
            theirs = _piece_of(full[m], kind, shard, 2 * qx + qy, c)
            out.append((_remote(mine, mine, send.at[3 * m + k], recv.at[3 * m + k], (qx, qy, c)),
                        _remote(theirs, theirs, send.at[3 * m + k], recv.at[3 * m + k], (qx, qy, c))))
    return out


def _gather_start(names, placed, geom, after):
    nb = len(names)

    def body(*refs):
        full = refs[:nb]
        send, recv = refs[nb + 1], refs[nb + 2]
        token = refs[2 * nb + 3]
        for cp, _ in _gather_copies(names, full, geom, send, recv):
            cp.start()
        token[...] = jnp.zeros_like(token)

    outs = pl.pallas_call(
        body, name="gather_start_" + names[0],
        out_shape=(pltpu.SemaphoreType.DMA((3 * nb,)), pltpu.SemaphoreType.DMA((3 * nb,)),
                   *[pltpu.HBM(placed[n].shape, BF16) for n in names], TOKEN),
        in_specs=[HBM] * nb + [ANY], out_specs=(SEM, SEM, *[HBM] * nb, pl.BlockSpec(memory_space=pltpu.VMEM)),
        input_output_aliases={m: 2 + m for m in range(nb)},
        compiler_params=pltpu.CompilerParams(has_side_effects=EFFECT),
    )(*[_hbm(placed[n]) for n in names], after)
    return outs[0], outs[1], list(outs[2:2 + nb]), outs[2 + nb]


def _gather_wait(names, send, recv, thru, geom, after):
    nb = len(names)

    def body(*refs):
        full = refs[:nb]
        for mine, theirs in _gather_copies(names, full, geom, refs[nb], refs[nb + 1]):
            mine.wait_send()
            theirs.wait_recv()

    return pl.pallas_call(
        body, name="gather_wait_" + names[0], out_shape=tuple(pltpu.HBM(t.shape, t.dtype) for t in thru),
        in_specs=[HBM] * nb + [SEM, SEM, ANY], out_specs=tuple([HBM] * nb),
        input_output_aliases={m: m for m in range(nb)},
        compiler_params=pltpu.CompilerParams(has_side_effects=EFFECT),
    )(*thru, send, recv, after)


def _gather_forward(names, full, geom):
    nb = len(names)

    def body(*refs):
        arr = refs[nb:2 * nb]
        send, recv = refs[2 * nb:]
        x, y, c = _me()
        sib = (x, y, 1 - c)
        cps = []
        for m, name in enumerate(names):
            kind, shard, _ = geom[name]
            for k, (qx, qy) in enumerate(_other_chips(x, y)):
                got = _piece_of(arr[m], kind, shard, 2 * qx + qy, c)
                cp = _remote(got, got, send.at[3 * m + k], recv.at[3 * m + k], sib)
                cp.start()
                cps.append(cp)
        for m, name in enumerate(names):
            kind, shard, _ = geom[name]
            for k, (qx, qy) in enumerate(_other_chips(x, y)):
                theirs = _piece_of(arr[m], kind, shard, 2 * qx + qy, 1 - c)
                _remote(theirs, theirs, send.at[3 * m + k], recv.at[3 * m + k], sib).wait_recv()
        for cp in cps:
            cp.wait_send()

    return pl.pallas_call(
        body, name="gather_forward_" + names[0], in_specs=[ANY] * nb, out_specs=[ANY] * nb,
        out_shape=[jax.ShapeDtypeStruct(a.shape, a.dtype) for a in full],
        input_output_aliases={m: m for m in range(nb)},
        scratch_shapes=[pltpu.SemaphoreType.DMA((3 * nb,)), pltpu.SemaphoreType.DMA((3 * nb,))],
    )(*full)


def _scatter_copies(nb, t, got, send, recv):
    x, y, c = _me()
    return [_remote(t[m].at[2 * qx + qy], got[m].at[k], send.at[3 * m + k], recv.at[3 * m + k], (qx, qy, c))
            for m in range(nb) for k, (qx, qy) in enumerate(_other_chips(x, y))]


def _chip_exchange_start(names, chip_sums, geom, after):
    nb = len(names)
    lands = [lax.empty((N_CHIPS - 1,) + geom[n][2], BF16) for n in names]

    def body(*refs):
        t, got = refs[:nb], refs[nb:2 * nb]
        send, recv = refs[2 * nb + 1], refs[2 * nb + 2]
        token = refs[4 * nb + 3]
        for cp in _scatter_copies(nb, t, got, send, recv):
            cp.start()
        token[...] = jnp.zeros_like(token)

    both = list(chip_sums) + lands
    outs = pl.pallas_call(
        body, name="grad_chip_start_" + names[0],
        out_shape=(pltpu.SemaphoreType.DMA((3 * nb,)), pltpu.SemaphoreType.DMA((3 * nb,)),
                   *[pltpu.HBM(a.shape, a.dtype) for a in both], TOKEN),
        in_specs=[HBM] * (2 * nb) + [ANY],
        out_specs=(SEM, SEM, *[HBM] * (2 * nb), pl.BlockSpec(memory_space=pltpu.VMEM)),
        input_output_aliases={m: 2 + m for m in range(2 * nb)},
        compiler_params=pltpu.CompilerParams(has_side_effects=EFFECT),
    )(*[_hbm(a) for a in both], after)
    return outs[0], outs[1], list(outs[2:2 + 2 * nb]), outs[2 + 2 * nb]


def _chip_exchange_wait(names, send, recv, thru, after):
    nb = len(names)

    def body(*refs):
        for cp in _scatter_copies(nb, refs[:nb], refs[nb:2 * nb], refs[2 * nb], refs[2 * nb + 1]):
            cp.wait_send()
            cp.wait_recv()

    outs = pl.pallas_call(
        body, name="grad_chip_wait_" + names[0], out_shape=tuple(pltpu.HBM(t.shape, t.dtype) for t in thru),
        in_specs=[HBM] * (2 * nb) + [SEM, SEM, ANY], out_specs=tuple([HBM] * (2 * nb)),
        input_output_aliases={m: m for m in range(2 * nb)},
        compiler_params=pltpu.CompilerParams(has_side_effects=EFFECT),
    )(*thru, send, recv, after)
    return list(outs[nb:])


def _sibling_copies(names, g, got, geom, send, recv):
    x, y, c = _me()
    out = []
    for m, name in enumerate(names):
        kind, shard, _ = geom[name]
        for r in range(N_CHIPS):
            out.append(_remote(_piece_of(g[m], kind, shard, r, 1 - c), got[m].at[r],
                               send.at[N_CHIPS * m + r], recv.at[N_CHIPS * m + r], (x, y, 1 - c)))
    return out


def _sibling_exchange_start(names, grads, geom, after):
    nb = len(names)
    lands = [lax.empty((N_CHIPS,) + geom[n][2], BF16) for n in names]

    def body(*refs):
        for cp in _sibling_copies(names, refs[:nb], refs[nb:2 * nb], geom, refs[2 * nb + 1], refs[2 * nb + 2]):
            cp.start()
        token = refs[4 * nb + 3]
        token[...] = jnp.zeros_like(token)

    both = [grads[n] for n in names] + lands
    outs = pl.pallas_call(
        body, name="grad_sibling_start_" + names[0],
        out_shape=(pltpu.SemaphoreType.DMA((N_CHIPS * nb,)), pltpu.SemaphoreType.DMA((N_CHIPS * nb,)),
                   *[pltpu.HBM(a.shape, a.dtype) for a in both], TOKEN),
        in_specs=[HBM] * (2 * nb) + [ANY],
        out_specs=(SEM, SEM, *[HBM] * (2 * nb), pl.BlockSpec(memory_space=pltpu.VMEM)),
        input_output_aliases={m: 2 + m for m in range(2 * nb)},
        compiler_params=pltpu.CompilerParams(has_side_effects=EFFECT),
    )(*[_hbm(a) for a in both], after)
    return outs[0], outs[1], list(outs[2:2 + 2 * nb]), outs[2 + 2 * nb]


def _sibling_exchange_wait(names, send, recv, thru, geom, after):
    nb = len(names)

    def body(*refs):
        for cp in _sibling_copies(names, refs[:nb], refs[nb:2 * nb], geom, refs[2 * nb], refs[2 * nb + 1]):
            cp.wait_send()
            cp.wait_recv()

    outs = pl.pallas_call(
        body, name="grad_sibling_wait_" + names[0], out_shape=tuple(pltpu.HBM(t.shape, t.dtype) for t in thru),
        in_specs=[HBM] * (2 * nb) + [SEM, SEM, ANY], out_specs=tuple([HBM] * (2 * nb)),
        input_output_aliases={m: m for m in range(2 * nb)},
        compiler_params=pltpu.CompilerParams(has_side_effects=EFFECT),
    )(*thru, send, recv, after)
    return list(outs[:nb]), list(outs[nb:])


def _allreduce_small(part):
    rows = part.shape[0]
    flips = [(a, b, e) for a in (0, 1) for b in (0, 1) for e in (0, 1) if (a, b, e) != (0, 0, 0)]

    def body(p_ref, o_ref, all_ref, send, recv):
        x, y, c = _me()
        me = 4 * x + 2 * y + c
        all_ref[me] = p_ref[...]
        cps = []
        for k, (a, b, e) in enumerate(flips):
            cp = _remote(p_ref, all_ref.at[me], send.at[k], recv.at[k], (x ^ a, y ^ b, c ^ e))
            cp.start()
            cps.append(cp)
        for k, (a, b, e) in enumerate(flips):
            peer = 4 * (x ^ a) + 2 * (y ^ b) + (c ^ e)
            _remote(p_ref, all_ref.at[peer], send.at[k], recv.at[k], (x ^ a, y ^ b, c ^ e)).wait_recv()
        for cp in cps:
            cp.wait_send()
        tot = all_ref[0]
        for dev in range(1, 8):
            tot = tot + all_ref[dev]
        o_ref[...] = tot

    vm = pl.BlockSpec(memory_space=pltpu.VMEM)
    return pl.pallas_call(
        body, name="allreduce_small", in_specs=[vm], out_specs=vm, out_shape=jax.ShapeDtypeStruct((rows, LANES), F32),
        scratch_shapes=[pltpu.VMEM((8, rows, LANES), F32), pltpu.SemaphoreType.DMA((7,)), pltpu.SemaphoreType.DMA((7,))],
    )(part)


def _rows_tile(rows, cols, mult, elems=1 << 19):
    return _pick(rows, max(mult, elems // cols // mult * mult), mult)


ADD_TILE = 1 << 20


def _add_pairs(g, got, kind, shard, where, name):
    p, r, c = got.shape
    tr = _rows_tile(r, c, 16, ADD_TILE)
    nt = r // tr

    def body(w_ref, a_ref, b_ref, o_ref):
        o_ref[...] = (a_ref[...].astype(F32) + b_ref[...].astype(F32)).astype(BF16)

    if kind == "row":
        g_map = lambda q, i, w_ref: ((2 * q + w_ref[1]) * nt + i, 0)
    else:
        g_map = lambda q, i, w_ref: (w_ref[1] * nt + i, q)
    spec = pl.BlockSpec((None, tr, c), lambda q, i, w_ref: (q, i, 0))
    return pl.pallas_call(
        body, name="grad_add_sibling_" + name,
        grid_spec=pltpu.PrefetchScalarGridSpec(
            num_scalar_prefetch=1, grid=(p, nt), in_specs=[pl.BlockSpec((tr, c), g_map), spec], out_specs=spec),
        out_shape=jax.ShapeDtypeStruct((p, r, c), BF16), compiler_params=_cp(("arbitrary", "arbitrary"), 32),
    )(where, g, got)


def _add_four(t, got, shard, where, name):
    _, r, c = t.shape
    tr = _rows_tile(r, c, 16, ADD_TILE)
    nt = r // tr

    def body(w_ref, own, t0, t1, t2, o_ref):
        o_ref[...] = ((own[...].astype(F32) + t0[...].astype(F32)) + t1[...].astype(F32)) + t2[...].astype(F32)

    spec = lambda q: pl.BlockSpec((None, tr, c), lambda i, w_ref: (q, i, 0))
    return pl.pallas_call(
        body, name="grad_add_chips_" + name,
        grid_spec=pltpu.PrefetchScalarGridSpec(
            num_scalar_prefetch=1, grid=(nt,),
            in_specs=[pl.BlockSpec((None, tr, c), lambda i, w_ref: (w_ref[0], i, 0)), spec(0), spec(1), spec(2)],
            out_specs=pl.BlockSpec((tr, c), lambda i, w_ref: (w_ref[1] * nt + i, 0))),
        out_shape=jax.ShapeDtypeStruct(shard, F32), compiler_params=_cp(("arbitrary",), 48),
    )(where, t, got, got, got)


def _adamw(w, g, m, v, name):
    r, c = w.shape
    tr = _rows_tile(r, c, 8)

    def body(w_ref, g_ref, m_ref, v_ref, go_ref, d_ref, mo_ref, vo_ref):
        gv = g_ref[...]
        mn = ADAM_B1 * m_ref[...] + (1.0 - ADAM_B1) * gv
        vn = ADAM_B2 * v_ref[...] + (1.0 - ADAM_B2) * (gv * gv)
        m_hat = mn / (1.0 - ADAM_B1 ** ADAM_STEP)
        v_hat = vn / (1.0 - ADAM_B2 ** ADAM_STEP)
        go_ref[...] = gv
        d_ref[...] = -ADAM_LR * (m_hat / (jnp.sqrt(v_hat) + ADAM_EPS) + ADAM_WD * w_ref[...])
        mo_ref[...] = mn
        vo_ref[...] = vn

    spec = pl.BlockSpec((tr, c), lambda i: (i, 0))
    return pl.pallas_call(
        body, name=name, grid=(r // tr,), in_specs=[spec] * 4, out_specs=[spec] * 4,
        out_shape=[jax.ShapeDtypeStruct((r, c), F32)] * 4, compiler_params=_cp(("arbitrary",), 32),
    )(w, g, m, v)


def _pack(vectors, rows):
    flat = jnp.concatenate([v.reshape(-1).astype(F32) for v in vectors])
    return jnp.pad(flat, (0, rows * LANES - flat.shape[0])).reshape(rows, LANES)


def _unpack(packed, shapes):
    flat = packed.reshape(-1)
    out, off = [], 0
    for shp in shapes:
        n = 1
        for t in shp:
            n *= t
        out.append(flat[off:off + n].reshape(shp))
        off += n
    return out


def _rows_for(shapes):
    n = sum(functools.reduce(lambda a, b: a * b, shp, 1) for shp in shapes)
    return -(-n // (8 * LANES)) * 8


def kernel(x, w_in, attn_sinks, short_conv_w, w_out, ln1_g, ln1_b, ffn_w_up, ffn_conv_w, ffn_w_down, ln2_g, ln2_b, loss_target, m_w_in, m_attn_sinks, m_short_conv_w, m_w_out, m_ln1_g, m_ln1_b, m_ffn_w_up, m_ffn_conv_w, m_ffn_w_down, m_ln2_g, m_ln2_b, v_w_in, v_attn_sinks, v_short_conv_w, v_w_out, v_ln1_g, v_ln1_b, v_ffn_w_up, v_ffn_conv_w, v_ffn_w_down, v_ln2_g, v_ln2_b):
    xs, tgt = x[0], loss_target[0]
    s, d = xs.shape
    chip = 2 * lax.axis_index("x") + lax.axis_index("y")

    w_big = dict(win_t=w_in[0], wout=w_out[0], wup=ffn_w_up[0], wdown=ffn_w_down[0])
    m_big = dict(win_t=m_w_in[0], wout=m_w_out[0], wup=m_ffn_w_up[0], wdown=m_ffn_w_down[0])
    v_big = dict(win_t=v_w_in[0], wout=v_w_out[0], wup=v_ffn_w_up[0], wdown=v_ffn_w_down[0])
    to_place = dict(w_big, win_t=w_in[0].T)
    geom = _geom({n: to_place[n].shape for n in BIG})
    pad8 = lambda a: jnp.pad(a[0], ((0, 5), (0, 0)))
    where = jnp.stack([chip, lax.axis_index("c")]).astype(jnp.int32)
    placed = {n: _place_shard(to_place[n], where[:1], geom[n][0], n) for n in BIG}
    full, (scw8, fcw8) = _allgather_weights(MIXER, placed, geom, [pad8(short_conv_w), pad8(ffn_conv_w)])
    send, recv, thru, token = _gather_start(FFN, placed, geom, scw8)
    a = _phase_mixer(xs, full["win_t"], full["wout"], scw8, attn_sinks, ln1_g, ln1_b, after=token)
    landed = _gather_forward(FFN, _gather_wait(FFN, send, recv, thru, geom, a["x1b"]), geom)
    full.update(zip(FFN, landed))
    f = _phase_ffn(a, tgt, full["wup"], full["wdown"], fcw8, ln2_g, ln2_b)

    def add_pairs(names, grads, from_sibling):
        return [_add_pairs(grads[m], from_sibling[m], geom[n][0], geom[n][1], where, n) for m, n in enumerate(names)]

    sib_send, sib_recv, sib_thru, sib_token = _sibling_exchange_start(FFN, f, geom, f["st2"])
    started = {}

    def between(dz1b):
        grads, from_sibling = _sibling_exchange_wait(FFN, sib_send, sib_recv, sib_thru, geom, dz1b)
        started["sums"] = add_pairs(FFN, grads, from_sibling)
        started["chip"] = _chip_exchange_start(FFN, started["sums"], geom, f["st2"])
        return started["chip"][3]

    (dproj, dz1b), g_mixer, g_small = _phase_rest(a, f, full["wup"], full["wout"], full["win_t"], scw8, attn_sinks,
                                                  ln1_g + sib_token[0:1, 0:1], between=between)
    ffn_sums = started["sums"]
    send, recv, thru, _ = started["chip"]

    def finish(names, sums, from_chips):
        halves = [_add_four(sums[m], from_chips[m], geom[n][1], where, n) for m, n in enumerate(names)]
        shards = _sibling_assemble(names, halves, geom)
        grads = {n: shards[m].T if n == "win_t" else shards[m] for m, n in enumerate(names)}
        return {n: _adamw(w_big[n], grads[n], m_big[n], v_big[n], "adamw_" + n) for n in names}

    mixer_sums = add_pairs(MIXER, [g_mixer[n] for n in MIXER], _sibling_exchange(MIXER, g_mixer, geom))
    send2, recv2, thru2, token2 = _chip_exchange_start(MIXER, mixer_sums, geom, f["st2"])
    grad_x = _grad_x(dproj, dz1b, full["win_t"], after=token2)
    upd = finish(FFN, ffn_sums, _chip_exchange_wait(FFN, send, recv, thru, grad_x))
    upd.update(finish(MIXER, mixer_sums, _chip_exchange_wait(MIXER, send2, recv2, thru2, upd[FFN[0]][1])))

    small_names = ("ln1_g", "ln1_b", "ln2_g", "ln2_b", "sinks", "fcw", "scw")
    small_shapes = [g_small[n].shape for n in small_names]
    red = _allreduce_small(_pack([g_small["loss_sq"].reshape(1)] + [g_small[n] for n in small_names],
                                 _rows_for([(1,)] + small_shapes)))
    loss_sq, *gs = _unpack(red, [(1,)] + small_shapes)
    gs = dict(zip(small_names, gs))
    loss = (0.5 / d) * loss_sq[0]
    fw, sw = ffn_conv_w.shape[2], short_conv_w.shape[2]
    gs["fcw"] = lax.dynamic_slice_in_dim(gs["fcw"], chip * fw, fw, axis=1)
    gs["scw"] = lax.dynamic_slice_in_dim(gs["scw"], chip * sw, sw, axis=1)

    sm_w = dict(ln1_g=ln1_g[0], ln1_b=ln1_b[0], ln2_g=ln2_g[0], ln2_b=ln2_b[0], sinks=attn_sinks[0],
                fcw=ffn_conv_w[0], scw=short_conv_w[0])
    sm_m = dict(ln1_g=m_ln1_g[0], ln1_b=m_ln1_b[0], ln2_g=m_ln2_g[0], ln2_b=m_ln2_b[0], sinks=m_attn_sinks[0],
                fcw=m_ffn_conv_w[0], scw=m_short_conv_w[0])
    sm_v = dict(ln1_g=v_ln1_g[0], ln1_b=v_ln1_b[0], ln2_g=v_ln2_g[0], ln2_b=v_ln2_b[0], sinks=v_attn_sinks[0],
                fcw=v_ffn_conv_w[0], scw=v_short_conv_w[0])
    shapes = [sm_w[n].shape for n in small_names]
    rows = _rows_for(shapes)
    packed = [_pack([t[n] for n in small_names], rows) for t in (sm_w, gs, sm_m, sm_v)]
    sm_out = [dict(zip(small_names, _unpack(a, shapes))) for a in _adamw(*packed, "adamw_small")]

    def leaf(kind, name):
        if name in ("w_in", "w_out", "ffn_w_up", "ffn_w_down"):
            key = dict(w_in="win_t", w_out="wout", ffn_w_up="wup", ffn_w_down="wdown")[name]
            return upd[key][kind][None]
        key = dict(attn_sinks="sinks", short_conv_w="scw", ffn_conv_w="fcw").get(name, name)
        return sm_out[kind][key][None]

    order = ("w_in", "attn_sinks", "short_conv_w", "w_out", "ln1_g", "ln1_b", "ffn_w_up", "ffn_conv_w", "ffn_w_down",
             "ln2_g", "ln2_b")
    outs = [loss, grad_x[None]]
    for kind in range(4):
        outs += [leaf(kind, n) for n in order]
    return tuple(outs)
```

```python
import functools

import jax
import jax.numpy as jnp
from jax import lax
from jax.experimental import pallas as pl
from jax.experimental.pallas import tpu as pltpu

F32 = jnp.float32
BF16 = jnp.bfloat16
MESH = pl.DeviceIdType.MESH
ANY = pl.BlockSpec(memory_space=pl.ANY)

HEAD_DIM = 64
N_Q_HEADS = 16
N_KV_HEADS = 2
ATTN_WIDTH = N_Q_HEADS * HEAD_DIM
KV_WIDTH = N_KV_HEADS * HEAD_DIM
BLOCK = 128
ROPE_THETA = 10000.0
LN_EPS = 1e-5
ALPHA = 2.0 ** 0.25
NEG_INF = -1e30
ADAM_LR, ADAM_B1, ADAM_B2, ADAM_EPS, ADAM_WD, ADAM_STEP = 0.001, 0.9, 0.999, 1e-08, 0.01, 10
N_CHIPS = 4
LANES = 128
MXU_DIM = 256
SLAB = 128


def _cp(sem, vmem_mb):
    return pltpu.CompilerParams(dimension_semantics=sem, vmem_limit_bytes=vmem_mb << 20)


def _matmul(a, b, *, mode, m, n, k, tm, tn, tk, out_dtype, name, vmem_mb, a_spec=None, b_spec=None,
            res=None, alpha=1.0, after=None, m_outer=False):
    nj, ni, nk = n // tn, m // tm, k // tk
    assert nj * tn == n and ni * tm == m and nk * tk == k, (name, m, n, k, tm, tn, tk)
    if mode == "nn":
        dims = ((1,), (0,))
        a_spec = a_spec or pl.BlockSpec((tm, tk), lambda j, i, kk: (i, kk))
        b_spec = b_spec or pl.BlockSpec((tk, tn), lambda j, i, kk: (kk, j))
    elif mode == "nt":
        dims = ((1,), (1,))
        a_spec = a_spec or pl.BlockSpec((tm, tk), lambda j, i, kk: (i, kk))
        b_spec = b_spec or pl.BlockSpec((tn, tk), lambda j, i, kk: (j, kk))
    else:
        dims = ((0,), (0,))
        a_spec = a_spec or pl.BlockSpec((tk, tm), lambda j, i, kk: (kk, i))
        b_spec = b_spec or pl.BlockSpec((tk, tn), lambda j, i, kk: (kk, j))
    has_res = res is not None
    has_after = after is not None

    def body(*refs):
        refs = refs[1:] if has_after else refs
        a_ref, b_ref = refs[0], refs[1]
        res_ref = refs[2] if has_res else None
        o_ref = refs[2 + has_res]
        part = lax.dot_general(a_ref[...].astype(BF16), b_ref[...].astype(BF16), (dims, ((), ())),
                               preferred_element_type=F32)

        def finish(acc):
            if has_res:
                acc = acc + alpha * res_ref[...].astype(F32)
            o_ref[...] = acc.astype(o_ref.dtype)

        if nk == 1:
            finish(part)
        else:
            acc_ref = refs[3 + has_res]
            kk = pl.program_id(2)

            @pl.when(kk == 0)
            def _():
                acc_ref[...] = part

            @pl.when(kk > 0)
            def _():
                acc_ref[...] += part

            @pl.when(kk == nk - 1)
            def _():
                finish(acc_ref[...])

    in_specs = [a_spec, b_spec]
    args = [a, b]
    if has_res:
        in_specs.append(pl.BlockSpec((tm, tn), lambda j, i, kk: (i, j)))
        args.append(res)
    if has_after:
        in_specs.insert(0, pl.BlockSpec(after.shape, lambda j, i, kk: (0, 0)))
        args.insert(0, after)
    out_spec = pl.BlockSpec((tm, tn), lambda j, i, kk: (i, j))
    grid = (nj, ni, nk)
    if m_outer:
        swap = lambda sp: pl.BlockSpec(sp.block_shape, (lambda f: lambda i, j, kk: f(j, i, kk))(sp.index_map),
                                       pipeline_mode=sp.pipeline_mode)
        in_specs, out_spec, grid = [swap(sp) for sp in in_specs], swap(out_spec), (ni, nj, nk)
    return pl.pallas_call(
        body, name=name, grid=grid, in_specs=in_specs,
        out_specs=out_spec,
        out_shape=jax.ShapeDtypeStruct((m, n), out_dtype),
        scratch_shapes=[pltpu.VMEM((tm, tn), F32)] if nk > 1 else [],
        compiler_params=_cp(("arbitrary", "arbitrary", "arbitrary"), vmem_mb),
    )(*args)


def _pick(total, want, mult):
    if total <= want:
        return total
    for t in range(want, 0, -1):
        if total % t == 0 and t % mult == 0:
            return t
    return total


def _rope_tables(s):
    half = HEAD_DIM // 2
    inv_freq = ROPE_THETA ** (-jnp.arange(half, dtype=F32) / half)
    ang = jnp.arange(s, dtype=F32)[:, None] * inv_freq[None, :]
    cos = jnp.tile(jnp.cos(ang), (1, LANES // half))
    sin = jnp.tile(jnp.concatenate([-jnp.sin(ang), jnp.sin(ang)], axis=1), (1, LANES // HEAD_DIM))
    return cos, sin


def _rope(x, cos, sin, lo):
    partner = jnp.where(lo, pltpu.roll(x, LANES - HEAD_DIM // 2, 1), pltpu.roll(x, HEAD_DIM // 2, 1))
    return x * cos + partner * sin


def _dot(a, b, dims):
    return lax.dot_general(a, b, (dims, ((), ())), preferred_element_type=F32)


NN, NT, TN = ((1,), (0,)), ((1,), (1,)), ((0,), (0,))


def _kv_variants(t, head_lo):
    r = pltpu.roll(t, HEAD_DIM, 1)
    zero = jnp.zeros_like(t)
    a = (jnp.where(head_lo, t, zero).astype(BF16), jnp.where(head_lo, r, zero).astype(BF16))
    b = (jnp.where(head_lo, zero, r).astype(BF16), jnp.where(head_lo, zero, t).astype(BF16))
    return a, b


PAIRS_PER_KV = N_Q_HEADS // 2 // N_KV_HEADS
STACK = PAIRS_PER_KV * BLOCK


def _stack_pairs(ref, j, fn):
    return jnp.concatenate([fn(ref[:, p * LANES:(p + 1) * LANES])
                            for p in range(j * PAIRS_PER_KV, (j + 1) * PAIRS_PER_KV)], axis=0)


def _sink_row(sink_ref, j, hh):
    col = lax.broadcasted_iota(jnp.int32, (1, STACK), 1)
    heads = [2 * p + hh for p in range(j * PAIRS_PER_KV, (j + 1) * PAIRS_PER_KV)]
    row = jnp.full((1, STACK), sink_ref[0, heads[-1]], F32)
    for t in range(PAIRS_PER_KV - 2, -1, -1):
        row = jnp.where(col < (t + 1) * BLOCK, sink_ref[0, heads[t]], row)
    return row


def _attn_exps(qp, ka, kb, valid, sink_a, sink_b):
    out = []
    for kk, sink in ((ka, sink_a), (kb, sink_b)):
        s = jnp.where(valid, _dot(kk, qp, NT), NEG_INF)
        mx = jnp.maximum(jnp.max(s, axis=0, keepdims=True), sink)
        out.append((jnp.exp(s - mx), jnp.exp(sink - mx)))
    return out


def _attn_common(i, q_ref, k_ref, v_ref, kp_ref, vp_ref, cos_ref, sin_ref, cosp_ref, sinp_ref):
    lane = lax.broadcasted_iota(jnp.int32, (1, LANES), 1)
    lo = (lane % HEAD_DIM) < (HEAD_DIM // 2)
    head_lo = lane < HEAD_DIM
    cos, sin = cos_ref[...], sin_ref[...]
    kc = _rope(k_ref[...].astype(F32), cos, sin, lo)
    kp = _rope(kp_ref[...].astype(F32), cosp_ref[...], sinp_ref[...], lo)
    kext = jnp.concatenate([kp, kc], axis=0)
    vext = jnp.concatenate([vp_ref[...].astype(F32), v_ref[...].astype(F32)], axis=0)
    ka, kb = _kv_variants(kext, head_lo)
    va, vb = _kv_variants(vext, head_lo)
    qi = lax.broadcasted_iota(jnp.int32, (1, STACK), 1) % BLOCK
    kj = lax.broadcasted_iota(jnp.int32, (2 * BLOCK, 1), 0)
    valid = (kj > qi) & (kj <= qi + BLOCK) & ((kj >= BLOCK) | (i > 0))
    cos4 = jnp.concatenate([cos] * PAIRS_PER_KV, axis=0)
    sin4 = jnp.concatenate([sin] * PAIRS_PER_KV, axis=0)
    return lo, head_lo, cos, sin, cos4, sin4, ka, kb, va, vb, valid


def _attn_fwd(proj, sinks, cos, sin, s):
    nb = s // BLOCK
    kcol, vcol = ATTN_WIDTH // LANES, ATTN_WIDTH // LANES + 1

    def body(q_ref, k_ref, v_ref, kp_ref, vp_ref, cos_ref, sin_ref, cosp_ref, sinp_ref, sink_ref, o_ref):
        i = pl.program_id(0)
        lo, head_lo, cs, sn, cs4, sn4, ka, kb, va, vb, valid = _attn_common(
            i, q_ref, k_ref, v_ref, kp_ref, vp_ref, cos_ref, sin_ref, cosp_ref, sinp_ref)
        row = lax.broadcasted_iota(jnp.int32, (16, 1), 0)
        one = jnp.ones((), BF16)
        for j in range(N_KV_HEADS):
            q4 = _stack_pairs(q_ref, j, lambda t: t.astype(F32))
            qp = (_rope(q4, cs4, sn4, lo) * HEAD_DIM ** -0.5).astype(BF16)
            exps = _attn_exps(qp, ka[j], kb[j], valid, _sink_row(sink_ref, j, 0), _sink_row(sink_ref, j, 1))
            outs = []
            for (e, es), vv, mine in zip(exps, (va[j], vb[j]), (head_lo, ~head_lo)):
                ee = jnp.concatenate([e.astype(BF16), jnp.where(row == 0, es, 0.0).astype(BF16)], axis=0)
                tail = jnp.where((row == 0) & ~mine, one, jnp.zeros((), BF16))
                vx = jnp.concatenate([jnp.where(mine, vv, one), tail], axis=0)
                un = _dot(ee, vx, TN)
                outs.append(un / pltpu.roll(un, HEAD_DIM, 1))
            o = jnp.where(head_lo, outs[0], outs[1]).astype(BF16)
            for t in range(PAIRS_PER_KV):
                p = j * PAIRS_PER_KV + t
                o_ref[:, p * LANES:(p + 1) * LANES] = o[t * BLOCK:(t + 1) * BLOCK]

    prev = lambda i: (jnp.maximum(i - 1, 0), 0)
    return pl.pallas_call(
        body, name="attn_fwd", grid=(nb,),
        in_specs=[pl.BlockSpec((BLOCK, ATTN_WIDTH), lambda i: (i, 0)),
                  pl.BlockSpec((BLOCK, LANES), lambda i: (i, kcol)),
                  pl.BlockSpec((BLOCK, LANES), lambda i: (i, vcol)),
                  pl.BlockSpec((BLOCK, LANES), lambda i: (jnp.maximum(i - 1, 0), kcol)),
                  pl.BlockSpec((BLOCK, LANES), lambda i: (jnp.maximum(i - 1, 0), vcol)),
                  pl.BlockSpec((BLOCK, LANES), lambda i: (i, 0)),
                  pl.BlockSpec((BLOCK, LANES), lambda i: (i, 0)),
                  pl.BlockSpec((BLOCK, LANES), prev),
                  pl.BlockSpec((BLOCK, LANES), prev),
                  pl.BlockSpec(memory_space=pltpu.SMEM)],
        out_specs=pl.BlockSpec((BLOCK, ATTN_WIDTH), lambda i: (i, 0)),
        out_shape=jax.ShapeDtypeStruct((s, ATTN_WIDTH), BF16),
        compiler_params=_cp(("arbitrary",), 32),
    )(proj, proj, proj, proj, proj, cos, sin, cos, sin, sinks)


def _attn_bwd(proj, dmix, sinks, cos, sin, s):
    nb = s // BLOCK
    kcol, vcol = ATTN_WIDTH // LANES, ATTN_WIDTH // LANES + 1
    pairs_per_kv = N_Q_HEADS // 2 // N_KV_HEADS

    def body(q_ref, k_ref, v_ref, kp_ref, vp_ref, cos_ref, sin_ref, cosp_ref, sinp_ref, sink_ref, do_ref,
             dq_ref, dk_ref, dv_ref, dsink_ref, ck_ref, cv_ref):
        g = pl.program_id(0)
        i = nb - 1 - g

        @pl.when(g == 0)
        def _():
            ck_ref[...] = jnp.zeros_like(ck_ref)
            cv_ref[...] = jnp.zeros_like(cv_ref)
            dsink_ref[...] = jnp.zeros_like(dsink_ref)

        lo, head_lo, cs, sn, cs4, sn4, ka, kb, va, vb, valid = _attn_common(
            i, q_ref, k_ref, v_ref, kp_ref, vp_ref, cos_ref, sin_ref, cosp_ref, sinp_ref)
        lane = lax.broadcasted_iota(jnp.int32, (1, LANES), 1)
        dk_j, dv_j = [], []
        dsink = jnp.zeros((1, LANES), F32)
        for j in range(N_KV_HEADS):
            q4 = _stack_pairs(q_ref, j, lambda t: t.astype(F32))
            qp = (_rope(q4, cs4, sn4, lo) * HEAD_DIM ** -0.5).astype(BF16)
            exps = _attn_exps(qp, ka[j], kb[j], valid, _sink_row(sink_ref, j, 0), _sink_row(sink_ref, j, 1))
            do = _stack_pairs(do_ref, j, lambda t: t)
            dq_r = jnp.zeros((STACK, LANES), F32)
            dkc, dvc = [], []
            for hh, ((e, es), kk, vv) in enumerate(zip(exps, (ka[j], kb[j]), (va[j], vb[j]))):
                inv = 1.0 / (jnp.sum(e, axis=0, keepdims=True) + es)
                pr = e * inv
                dp = _dot(vv, do, NT)
                delta = jnp.sum(pr * dp, axis=0, keepdims=True)
                ds = (pr * (dp - delta)).astype(BF16)
                psd = es * inv * delta
                for t in range(PAIRS_PER_KV):
                    head = 2 * (j * PAIRS_PER_KV + t) + hh
                    dsink = dsink + jnp.where(
                        lane == head, -jnp.sum(psd[:, t * BLOCK:(t + 1) * BLOCK], axis=1, keepdims=True), 0.0)
                dq_r = dq_r + _dot(ds, kk, TN)
                dkc.append(_dot(ds, qp, NN))
                dvc.append(_dot(pr.astype(BF16), do, NN))
            dk_j.append(jnp.where(head_lo, dkc[0], dkc[1]))
            dv_j.append(jnp.where(head_lo, dvc[0], dvc[1]))
            dq = _rope(dq_r * HEAD_DIM ** -0.5, cs4, -sn4, lo).astype(BF16)
            for t in range(PAIRS_PER_KV):
                p = j * PAIRS_PER_KV + t
                dq_ref[:, p * LANES:(p + 1) * LANES] = dq[t * BLOCK:(t + 1) * BLOCK]
        tot_k = [t + pltpu.roll(t, HEAD_DIM, 1) for t in dk_j]
        tot_v = [t + pltpu.roll(t, HEAD_DIM, 1) for t in dv_j]
        dkext = jnp.where(head_lo, tot_k[0], tot_k[1])
        dvext = jnp.where(head_lo, tot_v[0], tot_v[1])
        dk_r = dkext[BLOCK:] + ck_ref[...]
        dk_ref[...] = _rope(dk_r, cs, -sn, lo).astype(BF16)
        dv_ref[...] = (dvext[BLOCK:] + cv_ref[...]).astype(BF16)
        ck_ref[...] = dkext[:BLOCK]
        cv_ref[...] = dvext[:BLOCK]
        dsink_ref[0:1, :] += dsink

    cur = lambda col: (lambda g: (nb - 1 - g, col))
    prv = lambda col: (lambda g: (jnp.maximum(nb - 2 - g, 0), col))
    blk = lambda w, f: pl.BlockSpec((BLOCK, w), f)
    return pl.pallas_call(
        body, name="attn_bwd", grid=(nb,),
        in_specs=[blk(ATTN_WIDTH, cur(0)), blk(LANES, cur(kcol)), blk(LANES, cur(vcol)),
                  blk(LANES, prv(kcol)), blk(LANES, prv(vcol)),
                  blk(LANES, cur(0)), blk(LANES, cur(0)), blk(LANES, prv(0)), blk(LANES, prv(0)),
                  pl.BlockSpec(memory_space=pltpu.SMEM),
                  blk(ATTN_WIDTH, cur(0))],
        out_specs=[blk(ATTN_WIDTH, cur(0)), blk(LANES, cur(0)), blk(LANES, cur(0)),
                   pl.BlockSpec((8, LANES), lambda g: (0, 0))],
        out_shape=[jax.ShapeDtypeStruct((s, ATTN_WIDTH), BF16), jax.ShapeDtypeStruct((s, LANES), BF16),
                   jax.ShapeDtypeStruct((s, LANES), BF16), jax.ShapeDtypeStruct((8, LANES), F32)],
        scratch_shapes=[pltpu.VMEM((BLOCK, LANES), F32), pltpu.VMEM((BLOCK, LANES), F32)],
        compiler_params=_cp(("arbitrary",), 32),
    )(proj, proj, proj, proj, proj, cos, sin, cos, sin, sinks, dmix)


def _causal_conv(x, prev8, w):
    row = lax.broadcasted_iota(jnp.int32, (8, 1), 0)
    r1, r2 = pltpu.roll(x, 1, 0), pltpu.roll(x, 2, 0)
    s1 = jnp.concatenate([jnp.where(row == 0, prev8[7:8], r1[:8]), r1[8:]], axis=0)
    s2 = jnp.concatenate([jnp.where(row == 0, prev8[6:7], jnp.where(row == 1, prev8[7:8], r2[:8])), r2[8:]], axis=0)
    return w[0:1] * s2 + w[1:2] * s1 + w[2:3] * x


def _conv_bwd(dy, x, w, next8):
    r = x.shape[0]
    row = lax.broadcasted_iota(jnp.int32, (8, 1), 0)
    r1, r2 = pltpu.roll(dy, r - 1, 0), pltpu.roll(dy, r - 2, 0)
    n1 = jnp.concatenate([r1[:r - 8], jnp.where(row == 7, next8[0:1], r1[r - 8:])], axis=0)
    n2 = jnp.concatenate([r2[:r - 8], jnp.where(row == 6, next8[0:1], jnp.where(row == 7, next8[1:2], r2[r - 8:]))],
                         axis=0)
    dx = w[2:3] * dy + w[1:2] * n1 + w[0:1] * n2
    dws = [jnp.sum(t * x, axis=0, keepdims=True) for t in (n2, n1, dy)]
    return dx, dws


CONV_COLS = 256


def _convmix_cols(d):
    conv_w = d - ATTN_WIDTH
    base = (ATTN_WIDTH + 2 * KV_WIDTH) // CONV_COLS
    step = conv_w // CONV_COLS
    return base, base + step, base + 2 * step, step


def _convmix_fwd(proj, scw8, s, d):
    gb0, gc0, h0, ncb = _convmix_cols(d)
    tr = _pick(s, 1024, 16)
    ni = s // tr

    def body(gb_ref, gc_ref, h_ref, w_ref, o_ref, carry_ref):
        @pl.when(pl.program_id(1) == 0)
        def _():
            carry_ref[...] = jnp.zeros_like(carry_ref)

        gch = gc_ref[...].astype(F32) * h_ref[...].astype(F32)
        cc = _causal_conv(gch, carry_ref[...], w_ref[...])
        o_ref[...] = (gb_ref[...].astype(F32) * cc).astype(BF16)
        carry_ref[...] = gch[tr - 8:]

    spec = lambda c0: pl.BlockSpec((tr, CONV_COLS), lambda j, i: (i, c0 + j))
    return pl.pallas_call(
        body, name="convmix_fwd", grid=(ncb, ni),
        in_specs=[spec(gb0), spec(gc0), spec(h0), pl.BlockSpec((8, CONV_COLS), lambda j, i: (0, j))],
        out_specs=pl.BlockSpec((tr, CONV_COLS), lambda j, i: (i, j)),
        out_shape=jax.ShapeDtypeStruct((s, d - ATTN_WIDTH), BF16),
        scratch_shapes=[pltpu.VMEM((8, CONV_COLS), F32)],
        compiler_params=_cp(("arbitrary", "arbitrary"), 32),
    )(proj, proj, proj, scw8)


def _convmix_bwd(proj, dmix, scw8, s, d):
    gb0, gc0, h0, ncb = _convmix_cols(d)
    tr = _pick(s, 1024, 16)
    ni = s // tr
    dc0 = ATTN_WIDTH // CONV_COLS

    def body(dc_ref, gb_ref, gc_ref, h_ref, gcp_ref, hp_ref, w_ref, d3_ref, dw_ref, nxt_ref):
        g = pl.program_id(1)
        i = ni - 1 - g

        @pl.when(g == 0)
        def _():
            nxt_ref[...] = jnp.zeros_like(nxt_ref)
            dw_ref[...] = jnp.zeros_like(dw_ref)

        w = w_ref[...]
        gb, gc, h = gb_ref[...].astype(F32), gc_ref[...].astype(F32), h_ref[...].astype(F32)
        gch = gc * h
        prev8 = (gcp_ref[...].astype(F32) * hp_ref[...].astype(F32))[8:16] * (i > 0).astype(F32)
        cc = _causal_conv(gch, prev8, w)
        dc = dc_ref[...].astype(F32)
        dcc = dc * gb
        dgch, dws = _conv_bwd(dcc, gch, w, nxt_ref[...])
        d3_ref[0] = (dc * cc).astype(BF16)
        d3_ref[1] = (dgch * h).astype(BF16)
        d3_ref[2] = (dgch * gc).astype(BF16)
        for t in range(3):
            dw_ref[t:t + 1, :] += dws[t]
        nxt_ref[...] = dcc[0:8]

    cur = lambda c0: pl.BlockSpec((tr, CONV_COLS), lambda j, g: (ni - 1 - g, c0 + j))
    prv = lambda c0: pl.BlockSpec((16, CONV_COLS), lambda j, g: (jnp.maximum((ni - 1 - g) * (tr // 16) - 1, 0), c0 + j))
    return pl.pallas_call(
        body, name="convmix_bwd", grid=(ncb, ni),
        in_specs=[cur(dc0), cur(gb0), cur(gc0), cur(h0), prv(gc0), prv(h0),
                  pl.BlockSpec((8, CONV_COLS), lambda j, g: (0, j))],
        out_specs=[pl.BlockSpec((3, tr, CONV_COLS), lambda j, g: (0, ni - 1 - g, j)),
                   pl.BlockSpec((8, CONV_COLS), lambda j, g: (0, j))],
        out_shape=[jax.ShapeDtypeStruct((3, s, d - ATTN_WIDTH), BF16), jax.ShapeDtypeStruct((8, d - ATTN_WIDTH), F32)],
        scratch_shapes=[pltpu.VMEM((8, CONV_COLS), F32)],
        compiler_params=_cp(("arbitrary", "arbitrary"), 32),
    )(dmix, proj, proj, proj, proj, proj, scw8)


def _ln_fwd(z):
    mu = jnp.mean(z, axis=-1, keepdims=True)
    zc = z - mu
    var = jnp.mean(zc * zc, axis=-1, keepdims=True)
    rstd = lax.rsqrt(var + LN_EPS)
    return zc * rstd, rstd


def _ln_bwd(dout, xh, rstd, g):
    dxh = dout * g
    c1 = jnp.mean(dxh, axis=-1, keepdims=True)
    c2 = jnp.mean(dxh * xh, axis=-1, keepdims=True)
    dz = rstd * (dxh - c1 - xh * c2)
    return dz, jnp.sum(dout * xh, axis=0, keepdims=True), jnp.sum(dout, axis=0, keepdims=True)


def _outproj_ln1(attn, conv, wout, x, g1, b1, s, d):
    tm = _pick(s, 512, 16)
    ka = attn.shape[1]
    one_buffer = pl.Buffered(1)

    def body(a_ref, c_ref, wt_ref, wb_ref, x_ref, g_ref, b_ref, x1_ref, x1b_ref, xh_ref, rs_ref):
        y = _dot(a_ref[...], wt_ref[...], NN) + _dot(c_ref[...], wb_ref[...], NN)
        xh, rstd = _ln_fwd(ALPHA * x_ref[...] + y)
        x1 = xh * g_ref[...] + b_ref[...]
        x1_ref[...] = x1
        x1b_ref[...] = x1.astype(BF16)
        xh_ref[...] = xh.astype(BF16)
        rs_ref[...] = rstd

    row = lambda w: pl.BlockSpec((tm, w), lambda i: (i, 0))
    vec = pl.BlockSpec((1, d), lambda i: (0, 0))
    return pl.pallas_call(
        body, name="outproj_ln1", grid=(s // tm,),
        in_specs=[row(ka), row(d - ka), pl.BlockSpec((ka, d), lambda i: (0, 0), pipeline_mode=one_buffer),
                  pl.BlockSpec((d - ka, d), lambda i: (ka // (d - ka), 0), pipeline_mode=one_buffer), row(d), vec, vec],
        out_specs=[row(d), row(d), row(d), row(1)],
        out_shape=[jax.ShapeDtypeStruct((s, d), F32), jax.ShapeDtypeStruct((s, d), BF16),
                   jax.ShapeDtypeStruct((s, d), BF16), jax.ShapeDtypeStruct((s, 1), F32)],
        compiler_params=_cp(("arbitrary",), 56),
    )(attn, conv, wout, wout, x, g1, b1)


def _ffn_up(x1b, wup, fcw8, s, d, dff):
    tm = _pick(s, 1024, 16)
    tn = _pick(dff, 512, LANES)
    nj, ni = dff // tn, s // tm

    def body(x_ref, wa_ref, wg_ref, ca_ref, cg_ref, u_ref, y_ref, h_ref, carry_ref):
        @pl.when(pl.program_id(1) == 0)
        def _():
            carry_ref[...] = jnp.zeros_like(carry_ref)

        xa = x_ref[...]
        ys = []
        for part, (w_ref, c_ref) in enumerate(((wa_ref, ca_ref), (wg_ref, cg_ref))):
            ub = _dot(xa, w_ref[...], NN).astype(BF16)
            u_ref[part] = ub
            u = ub.astype(F32)
            y = _causal_conv(u, carry_ref[part], c_ref[...])
            carry_ref[part] = u[tm - 8:]
            yb = y.astype(BF16)
            y_ref[part] = yb
            ys.append(yb.astype(F32))
        a2, g2 = ys
        sig = 1.0 / (1.0 + jnp.exp(-a2))
        h_ref[...] = (a2 * sig * g2).astype(BF16)

    return pl.pallas_call(
        body, name="ffn_up", grid=(nj, ni),
        in_specs=[pl.BlockSpec((tm, d), lambda j, i: (i, 0)),
                  pl.BlockSpec((d, tn), lambda j, i: (0, j)),
                  pl.BlockSpec((d, tn), lambda j, i: (0, j + nj)),
                  pl.BlockSpec((8, tn), lambda j, i: (0, j)),
                  pl.BlockSpec((8, tn), lambda j, i: (0, j + nj))],
        out_specs=[pl.BlockSpec((2, tm, tn), lambda j, i: (0, i, j)),
                   pl.BlockSpec((2, tm, tn), lambda j, i: (0, i, j)),
                   pl.BlockSpec((tm, tn), lambda j, i: (i, j))],
        out_shape=[jax.ShapeDtypeStruct((2, s, dff), BF16), jax.ShapeDtypeStruct((2, s, dff), BF16),
                   jax.ShapeDtypeStruct((s, dff), BF16)],
        scratch_shapes=[pltpu.VMEM((2, 8, tn), F32)],
        compiler_params=_cp(("arbitrary", "arbitrary"), 56),
    )(x1b, wup, wup, fcw8, fcw8)


def _ffn_mid_bwd(dz2b, wdown, u3, y3, fcw8, s, d, dff):
    tm = _pick(s, 1024, 16)
    tn = _pick(dff, 512, LANES)
    nj, ni = dff // tn, s // tm

    def body(dz_ref, wd_ref, u_ref, y_ref, ca_ref, cg_ref, du_ref, dw_ref, nxt_ref):
        @pl.when(pl.program_id(1) == 0)
        def _():
            nxt_ref[...] = jnp.zeros_like(nxt_ref)
            dw_ref[...] = jnp.zeros_like(dw_ref)

        a2, g2 = y_ref[0].astype(F32), y_ref[1].astype(F32)
        sig = 1.0 / (1.0 + jnp.exp(-a2))
        silu = a2 * sig
        dhv = _dot(dz_ref[...], wd_ref[...], NT)
        dys = (dhv * g2 * (sig * (1.0 + a2 * (1.0 - sig))), dhv * silu)
        for part, (c_ref, dy) in enumerate(zip((ca_ref, cg_ref), dys)):
            dx, dws = _conv_bwd(dy, u_ref[part].astype(F32), c_ref[...], nxt_ref[part])
            du_ref[part] = dx.astype(BF16)
            for t in range(3):
                dw_ref[part, t:t + 1, :] += dws[t]
            nxt_ref[part] = dy[0:8]

    return pl.pallas_call(
        body, name="ffn_mid_bwd", grid=(nj, ni),
        in_specs=[pl.BlockSpec((tm, d), lambda j, g: (ni - 1 - g, 0)),
                  pl.BlockSpec((tn, d), lambda j, g: (j, 0)),
                  pl.BlockSpec((2, tm, tn), lambda j, g: (0, ni - 1 - g, j)),
                  pl.BlockSpec((2, tm, tn), lambda j, g: (0, ni - 1 - g, j)),
                  pl.BlockSpec((8, tn), lambda j, g: (0, j)),
                  pl.BlockSpec((8, tn), lambda j, g: (0, j + nj))],
        out_specs=[pl.BlockSpec((2, tm, tn), lambda j, g: (0, ni - 1 - g, j)),
                   pl.BlockSpec((2, 8, tn), lambda j, g: (0, 0, j))],
        out_shape=[jax.ShapeDtypeStruct((2, s, dff), BF16), jax.ShapeDtypeStruct((2, 8, dff), F32)],
        scratch_shapes=[pltpu.VMEM((2, 8, tn), F32)],
        compiler_params=_cp(("arbitrary", "arbitrary"), 56),
    )(dz2b, wdown, u3, y3, fcw8, fcw8)


def _ffn_down_loss(hmid, wdown, x1, target, g2, b2, s, d, dff):
    tm = _pick(s, 512, SLAB)
    tk = _pick(dff, 1408, LANES)
    ni, nk = s // tm, dff // tk
    slab = min(SLAB, tm)

    def body(h_ref, w_ref, x1_ref, t_ref, g_ref, b_ref, dzb_ref, st_ref, acc_ref):
        i, kk = pl.program_id(0), pl.program_id(1)

        @pl.when((i == 0) & (kk == 0))
        def _():
            st_ref[...] = jnp.zeros_like(st_ref)

        part = _dot(h_ref[...], w_ref[...], NN)

        @pl.when(kk == 0)
        def _():
            acc_ref[...] = part

        @pl.when(kk > 0)
        def _():
            acc_ref[...] += part

        @pl.when(kk == nk - 1)
        def _():
            g, b = g_ref[...], b_ref[...]

            def one(sl, carry):
                rows = pl.ds(pl.multiple_of(sl * slab, slab), slab)
                xh, rstd = _ln_fwd(ALPHA * x1_ref[rows, :] + acc_ref[rows, :])
                diff = xh * g + b - t_ref[rows, :]
                sq = jnp.sum(jnp.sum(diff * diff, axis=1, keepdims=True), axis=0, keepdims=True)
                dz, dg, db = _ln_bwd(diff * (1.0 / d), xh, rstd, g)
                dzb_ref[rows, :] = dz.astype(BF16)
                st_ref[0:1, :] += dg
                st_ref[1:2, :] += db
                st_ref[2:3, :] += sq
                return carry

            lax.fori_loop(0, tm // slab, one, 0)

    row = pl.BlockSpec((tm, d), lambda i, kk: (i, 0))
    vec = pl.BlockSpec((1, d), lambda i, kk: (0, 0))
    return pl.pallas_call(
        body, name="ffn_down_loss", grid=(ni, nk),
        in_specs=[pl.BlockSpec((tm, tk), lambda i, kk: (i, kk)), pl.BlockSpec((tk, d), lambda i, kk: (kk, 0)),
                  row, row, vec, vec],
        out_specs=[row, pl.BlockSpec((8, d), lambda i, kk: (0, 0))],
        out_shape=[jax.ShapeDtypeStruct((s, d), BF16), jax.ShapeDtypeStruct((8, d), F32)],
        scratch_shapes=[pltpu.VMEM((tm, d), F32)],
        compiler_params=_cp(("arbitrary", "arbitrary"), 48),
    )(hmid, wdown, x1, target, g2, b2)


def _ffn_dx_ln1_bwd(du3, wup, dz2b, xh1, rstd1, g1, s, d, dff):
    tm = _pick(s, 512, SLAB)
    tk = _pick(dff, 2816, MXU_DIM)
    nkh = dff // tk
    ni, nk = s // tm, 2 * nkh
    slab = min(SLAB, tm)

    def body(a_ref, w_ref, dz2_ref, xh_ref, rs_ref, g_ref, dzb_ref, st_ref, acc_ref):
        i, kk = pl.program_id(0), pl.program_id(1)

        @pl.when((i == 0) & (kk == 0))
        def _():
            st_ref[...] = jnp.zeros_like(st_ref)

        part = _dot(a_ref[...], w_ref[...], NT)

        @pl.when(kk == 0)
        def _():
            acc_ref[...] = part

        @pl.when(kk > 0)
        def _():
            acc_ref[...] += part

        @pl.when(kk == nk - 1)
        def _():
            g = g_ref[...]

            def one(sl, carry):
                rows = pl.ds(pl.multiple_of(sl * slab, slab), slab)
                dx1 = ALPHA * dz2_ref[rows, :].astype(F32) + acc_ref[rows, :]
                dz, dg, db = _ln_bwd(dx1, xh_ref[rows, :].astype(F32), rs_ref[rows, :], g)
                dzb_ref[rows, :] = dz.astype(BF16)
                st_ref[0:1, :] += dg
                st_ref[1:2, :] += db
                return carry

            lax.fori_loop(0, tm // slab, one, 0)

    row = pl.BlockSpec((tm, d), lambda i, kk: (i, 0))
    row1 = pl.BlockSpec((tm, d), lambda i, kk: (i, 0), pipeline_mode=pl.Buffered(1))
    return pl.pallas_call(
        body, name="ffn_dx_ln1_bwd", grid=(ni, nk),
        in_specs=[pl.BlockSpec((None, tm, tk), lambda i, kk: (kk // nkh, i, kk % nkh)),
                  pl.BlockSpec((d, tk), lambda i, kk: (0, kk)),
                  row1, row1, pl.BlockSpec((tm, 1), lambda i, kk: (i, 0)), pl.BlockSpec((1, d), lambda i, kk: (0, 0))],
        out_specs=[row, pl.BlockSpec((8, d), lambda i, kk: (0, 0))],
        out_shape=[jax.ShapeDtypeStruct((s, d), BF16), jax.ShapeDtypeStruct((8, d), F32)],
        scratch_shapes=[pltpu.VMEM((tm, d), F32)],
        compiler_params=_cp(("arbitrary", "arbitrary"), 56),
    )(du3, wup, dz2b, xh1, rstd1, g1)


def _phase_mixer(x, win_t, wout, scw8, sinks, ln1_g, ln1_b, after=None):
    s, d = x.shape
    n_in = win_t.shape[0]
    cos, sin = _rope_tables(s)
    proj = _matmul(x, win_t, mode="nt", m=s, n=n_in, k=d, tm=_pick(s, 512, 16), tn=n_in, tk=d, out_dtype=BF16,
                   name="in_proj", vmem_mb=52, after=after,
                   b_spec=pl.BlockSpec((n_in, d), lambda j, i, kk: (0, 0), pipeline_mode=pl.Buffered(1)))
    attn = _attn_fwd(proj, sinks, cos, sin, s)
    conv = _convmix_fwd(proj, scw8, s, d)
    x1, x1b, xh1, rstd1 = _outproj_ln1(attn, conv, wout, x, ln1_g, ln1_b, s, d)
    return dict(x=x, cos=cos, sin=sin, proj=proj, attn=attn, conv=conv, x1=x1, x1b=x1b, xh1=xh1, rstd1=rstd1)


def _phase_ffn(a, target, wup, wdown, fcw8, ln2_g, ln2_b):
    x1, x1b = a["x1"], a["x1b"]
    s, d = x1.shape
    dff = wdown.shape[0]
    u3, y3, hmid = _ffn_up(x1b, wup, fcw8, s, d, dff)
    dz2b, st2 = _ffn_down_loss(hmid, wdown, x1, target, ln2_g, ln2_b, s, d, dff)

    tnw = _pick(d, 1024, MXU_DIM)
    keep_b = pl.BlockSpec((s, tnw), lambda j, i, kk: (kk, j), pipeline_mode=pl.Buffered(1))
    g_wdown = _matmul(hmid, dz2b, mode="tn", m=dff, n=d, k=s, tm=_pick(dff, 512, MXU_DIM), tn=tnw, tk=s,
                      out_dtype=BF16, name="grad_w_down", vmem_mb=56, b_spec=keep_b)
    du3, dfcw = _ffn_mid_bwd(dz2b, wdown, u3, y3, fcw8, s, d, dff)
    tnu = _pick(dff, 512, MXU_DIM)
    njh = dff // tnu
    tmu = _pick(d, 1024, LANES)
    g_wup = _matmul(x1b, du3, mode="tn", m=d, n=2 * dff, k=s, tm=tmu, tn=tnu, tk=s, out_dtype=BF16,
                    name="grad_w_up", vmem_mb=58, m_outer=True,
                    a_spec=pl.BlockSpec((s, tmu), lambda j, i, kk: (kk, i), pipeline_mode=pl.Buffered(1)),
                    b_spec=pl.BlockSpec((None, s, tnu), lambda j, i, kk: (j // njh, kk, j % njh)))
    return dict(du3=du3, dz2b=dz2b, st2=st2, dfcw=dfcw, wdown=g_wdown, wup=g_wup)


def _phase_rest(a, f, wup, wout, win_t, scw8, sinks, ln1_g, between=None):
    xb, cos, sin, proj, attn, conv = a["x"], a["cos"], a["sin"], a["proj"], a["attn"], a["conv"]
    du3, dz2b, st2, dfcw = f["du3"], f["dz2b"], f["st2"], f["dfcw"]
    s, d = a["x1"].shape
    dff = wup.shape[1] // 2
    n_in = win_t.shape[0]
    ts = _pick(s, 2048, 16)
    dz1b, st1 = _ffn_dx_ln1_bwd(du3, wup, dz2b, a["xh1"], a["rstd1"], ln1_g, s, d, dff)
    after = between(dz1b) if between is not None else None

    mix = jnp.concatenate([attn, conv], axis=1)
    tnw = _pick(d, 1024, MXU_DIM)
    g_wout = _matmul(mix, dz1b, mode="tn", m=d, n=d, k=s, tm=_pick(d, 512, LANES), tn=tnw, tk=s,
                     out_dtype=BF16, name="grad_w_out", vmem_mb=56, after=after,
                     b_spec=pl.BlockSpec((s, tnw), lambda j, i, kk: (kk, j), pipeline_mode=pl.Buffered(1)))
    dmix = _matmul(dz1b, wout, mode="nt", m=s, n=d, k=d, tm=_pick(s, 1024, 16), tn=_pick(d, 1024, LANES), tk=d,
                   out_dtype=BF16, name="out_dmix", vmem_mb=48, after=after)
    d3, dscw = _convmix_bwd(proj, dmix, scw8, s, d)
    dq, dk, dv, dsink = _attn_bwd(proj, dmix, sinks, cos, sin, s)
    dproj = jnp.concatenate([dq, dk, dv, d3[0], d3[1], d3[2]], axis=1)
    g_win_t = _matmul(dproj, xb, mode="tn", m=n_in, n=d, k=s, tm=_pick(n_in, 2176, LANES), tn=_pick(d, 512, LANES),
                      tk=ts, out_dtype=BF16, name="grad_w_in", vmem_mb=48)
    small = dict(loss_sq=st2[2, 0], ln2_g=st2[0], ln2_b=st2[1], ln1_g=st1[0], ln1_b=st1[1], sinks=dsink[0, :N_Q_HEADS],
                 fcw=jnp.concatenate([dfcw[0, :3], dfcw[1, :3]], axis=1), scw=dscw[:3])
    return (dproj, dz1b), dict(win_t=g_win_t, wout=g_wout), small


def _grad_x(dproj, dz1b, win_t, after=None):
    s, n_in = dproj.shape
    d = win_t.shape[1]
    return _matmul(dproj, win_t, mode="nn", m=s, n=d, k=n_in, tm=_pick(s, 512, 16), tn=_pick(d, 1024, LANES),
                   tk=n_in, out_dtype=F32, name="grad_x", vmem_mb=56, res=dz1b, alpha=ALPHA, after=after)


def _local_step(x, target, win_t, wout, wup, wdown, scw8, fcw8, sinks, ln1_g, ln1_b, ln2_g, ln2_b):
    a = _phase_mixer(x, win_t, wout, scw8, sinks, ln1_g, ln1_b)
    f = _phase_ffn(a, target, wup, wdown, fcw8, ln2_g, ln2_b)
    (dproj, dz1b), g, small = _phase_rest(a, f, wup, wout, win_t, scw8, sinks, ln1_g)
    return _grad_x(dproj, dz1b, win_t), dict(g, wup=f["wup"], wdown=f["wdown"]), small


MIXER = ("win_t", "wout")
FFN = ("wup", "wdown")
BIG = MIXER + FFN


def _geom(shard_shapes):
    out = {}
    for name in BIG:
        r, c = shard_shapes[name]
        out[name] = ("col" if name == "wup" else "row", (r, c), (r // 2, c))
    return out


def _full_shape(kind, shard):
    r, c = shard
    return (N_CHIPS * r, c) if kind == "row" else (r, N_CHIPS * c)


def _piece_of(ref, kind, shard, chip, half):
    r, c = shard
    if kind == "row":
        return ref.at[pl.ds(chip * r + half * (r // 2), r // 2), :]
    return ref.at[pl.ds(half * (r // 2), r // 2), pl.ds(chip * c, c)]


def _shard_piece(ref, shard, half):
    r, _ = shard
    return ref.at[pl.ds(half * (r // 2), r // 2), :]


def _me():
    return lax.axis_index("x"), lax.axis_index("y"), lax.axis_index("c")


def _other_chips(x, y):
    return [(1 - x, y), (x, 1 - y), (1 - x, 1 - y)]


def _remote(src, dst, send_sem, recv_sem, dev):
    return pltpu.make_async_remote_copy(src_ref=src, dst_ref=dst, send_sem=send_sem, recv_sem=recv_sem,
                                        device_id=dev, device_id_type=MESH)


def _place_shard(w, chip1, kind, name):
    r, c = w.shape
    tr = _rows_tile(r, c, 16)
    nt = r // tr

    def body(chip_ref, w_ref, o_ref):
        o_ref[...] = w_ref[...].astype(BF16)

    out_map = (lambda i, chip_ref: (chip_ref[0] * nt + i, 0)) if kind == "row" else (lambda i, chip_ref: (i, chip_ref[0]))
    return pl.pallas_call(
        body, name="place_" + name,
        grid_spec=pltpu.PrefetchScalarGridSpec(
            num_scalar_prefetch=1, grid=(nt,),
            in_specs=[pl.BlockSpec((tr, c), lambda i, chip_ref: (i, 0))],
            out_specs=pl.BlockSpec((tr, c), out_map)),
        out_shape=jax.ShapeDtypeStruct(_full_shape(kind, (r, c)), BF16),
        compiler_params=_cp(("arbitrary",), 32),
    )(chip1, w)


def _allgather_weights(names, placed, geom, small_shards):
    nb, ns = len(names), len(small_shards)
    small_w = [a.shape[1] for a in small_shards]

    def body(*refs):
        sm = refs[nb:nb + ns]
        full = refs[nb + ns:2 * nb + ns]
        smf = refs[2 * nb + ns:2 * nb + 2 * ns]
        send, recv, loc = refs[2 * nb + 2 * ns:]
        x, y, c = _me()
        chip = 2 * x + y
        sib = (x, y, 1 - c)
        others = _other_chips(x, y)
        locals_, sends = [], []
        for m, name in enumerate(names):
            kind, shard, _ = geom[name]
            mine = _piece_of(full[m], kind, shard, chip, c)
            for k, (qx, qy) in enumerate(others):
                cp = _remote(mine, mine, send.at[6 * m + k], recv.at[6 * m + k], (qx, qy, c))
                cp.start()
                sends.append(cp)
        for t in range(ns):
            cp = pltpu.make_async_copy(sm[t], smf[t].at[:, pl.ds(chip * small_w[t], small_w[t])], loc.at[t])
            cp.start()
            locals_.append(cp)
            for k, (qx, qy) in enumerate(others):
                cp = _remote(sm[t], smf[t].at[:, pl.ds(chip * small_w[t], small_w[t])],
                             send.at[6 * nb + 3 * t + k], recv.at[6 * nb + 3 * t + k], (qx, qy, c))
                cp.start()
                sends.append(cp)
        for m, name in enumerate(names):
            kind, shard, _ = geom[name]
            for k, (qx, qy) in enumerate(others):
                got = _piece_of(full[m], kind, shard, 2 * qx + qy, c)
                _remote(got, got, send.at[6 * m + k], recv.at[6 * m + k], (qx, qy, c)).wait_recv()
                cp = _remote(got, got, send.at[6 * m + 3 + k], recv.at[6 * m + 3 + k], sib)
                cp.start()
                sends.append(cp)
        for t in range(ns):
            for k, (qx, qy) in enumerate(others):
                got = smf[t].at[:, pl.ds((2 * qx + qy) * small_w[t], small_w[t])]
                _remote(got, got, send.at[6 * nb + 3 * t + k], recv.at[6 * nb + 3 * t + k], (qx, qy, c)).wait_recv()
        for m, name in enumerate(names):
            kind, shard, _ = geom[name]
            for k, (qx, qy) in enumerate(others):
                got = _piece_of(full[m], kind, shard, 2 * qx + qy, 1 - c)
                _remote(got, got, send.at[6 * m + 3 + k], recv.at[6 * m + 3 + k], sib).wait_recv()
        for cp in sends:
            cp.wait_send()
        for cp in locals_:
            cp.wait()

    nsem = 6 * nb + 3 * ns
    out_shape = [jax.ShapeDtypeStruct(placed[n].shape, BF16) for n in names]
    out_shape += [jax.ShapeDtypeStruct((8, N_CHIPS * w), F32) for w in small_w]
    outs = pl.pallas_call(
        body, name="allgather_weights", in_specs=[ANY] * (nb + ns), out_specs=[ANY] * (nb + ns), out_shape=out_shape,
        input_output_aliases={m: m for m in range(nb)},
        scratch_shapes=[pltpu.SemaphoreType.DMA((nsem,)), pltpu.SemaphoreType.DMA((nsem,)),
                        pltpu.SemaphoreType.DMA((ns,))],
    )(*[placed[n] for n in names], *small_shards)
    return dict(zip(names, outs[:nb])), list(outs[nb:])


def _sibling_exchange(names, grads, geom):
    nb = len(names)

    def body(*refs):
        g = refs[:nb]
        got = refs[nb:2 * nb]
        send, recv = refs[2 * nb:]
        x, y, c = _me()
        sib = (x, y, 1 - c)
        cps = []
        for m, name in enumerate(names):
            kind, shard, _ = geom[name]
            for r in range(N_CHIPS):
                cp = _remote(_piece_of(g[m], kind, shard, r, 1 - c), got[m].at[r],
                             send.at[N_CHIPS * m + r], recv.at[N_CHIPS * m + r], sib)
                cp.start()
                cps.append(cp)
        for cp in cps:
            cp.wait_recv()
        for cp in cps:
            cp.wait_send()

    return pl.pallas_call(
        body, name="grad_sibling_exchange_" + names[0], in_specs=[ANY] * nb, out_specs=[ANY] * nb,
        out_shape=[jax.ShapeDtypeStruct((N_CHIPS,) + geom[n][2], BF16) for n in names],
        scratch_shapes=[pltpu.SemaphoreType.DMA((N_CHIPS * nb,)), pltpu.SemaphoreType.DMA((N_CHIPS * nb,))],
    )(*[grads[n] for n in names])


def _sibling_assemble(names, shards, geom):
    nb = len(names)

    def body(*refs):
        full = refs[nb:2 * nb]
        send, recv = refs[2 * nb:]
        x, y, c = _me()
        sib = (x, y, 1 - c)
        cps = []
        for m, name in enumerate(names):
            mine = _shard_piece(full[m], geom[name][1], c)
            cp = _remote(mine, mine, send.at[m], recv.at[m], sib)
            cp.start()
            cps.append(cp)
        for m, name in enumerate(names):
            theirs = _shard_piece(full[m], geom[name][1], 1 - c)
            _remote(theirs, theirs, send.at[m], recv.at[m], sib).wait_recv()
        for cp in cps:
            cp.wait_send()

    return pl.pallas_call(
        body, name="grad_sibling_assemble_" + names[0], in_specs=[ANY] * nb, out_specs=[ANY] * nb,
        out_shape=[jax.ShapeDtypeStruct(geom[n][1], F32) for n in names],
        input_output_aliases={m: m for m in range(nb)},
        scratch_shapes=[pltpu.SemaphoreType.DMA((nb,)), pltpu.SemaphoreType.DMA((nb,))],
    )(*shards)


HBM = pl.BlockSpec(memory_space=pltpu.HBM)
SEM = pl.BlockSpec(memory_space=pltpu.SEMAPHORE)
EFFECT = pltpu.SideEffectType.DATAFLOW_SIDE_EFFECTING
TOKEN = jax.ShapeDtypeStruct((8, LANES), F32)


def _hbm(a):
    return pltpu.with_memory_space_constraint(a, pltpu.HBM)


def _gather_copies(names, full, geom, send, recv):
    x, y, c = _me()
    out = []
    for m, name in enumerate(names):
        kind, shard, _ = geom[name]
        mine = _piece_of(full[m], kind, shard, 2 * x + y, c)
        for k, (qx, qy) in enumerate(_other_chips(x, y)):
            theirs = _piece_of(full[m], kind, shard, 2 * qx + qy, c)
            out.append((_remote(mine, mine, send.at[3 * m + k], recv.at[3 * m + k], (qx, qy, c)),
                        _remote(theirs, theirs, send.at[3 * m + k], recv.at[3 * m + k], (qx, qy, c))))
    return out


def _gather_start(names, placed, geom, after):
    nb = len(names)

    def body(*refs):
        full = refs[:nb]
        send, recv = refs[nb + 1], refs[nb + 2]
        token = refs[2 * nb + 3]
        for cp, _ in _gather_copies(names, full, geom, send, recv):
            cp.start()
        token[...] = jnp.zeros_like(token)

    outs = pl.pallas_call(
        body, name="gather_start_" + names[0],
        out_shape=(pltpu.SemaphoreType.DMA((3 * nb,)), pltpu.SemaphoreType.DMA((3 * nb,)),
                   *[pltpu.HBM(placed[n].shape, BF16) for n in names], TOKEN),
        in_specs=[HBM] * nb + [ANY], out_specs=(SEM, SEM, *[HBM] * nb, pl.BlockSpec(memory_space=pltpu.VMEM)),
        input_output_aliases={m: 2 + m for m in range(nb)},
        compiler_params=pltpu.CompilerParams(has_side_effects=EFFECT),
    )(*[_hbm(placed[n]) for n in names], after)
    return outs[0], outs[1], list(outs[2:2 + nb]), outs[2 + nb]


def _gather_wait(names, send, recv, thru, geom, after):
    nb = len(names)

    def body(*refs):
        full = refs[:nb]
        for mine, theirs in _gather_copies(names, full, geom, refs[nb], refs[nb + 1]):
            mine.wait_send()
            theirs.wait_recv()

    return pl.pallas_call(
        body, name="gather_wait_" + names[0], out_shape=tuple(pltpu.HBM(t.shape, t.dtype) for t in thru),
        in_specs=[HBM] * nb + [SEM, SEM, ANY], out_specs=tuple([HBM] * nb),
        input_output_aliases={m: m for m in range(nb)},
        compiler_params=pltpu.CompilerParams(has_side_effects=EFFECT),
    )(*thru, send, recv, after)


def _gather_forward(names, full, geom):
    nb = len(names)

    def body(*refs):
        arr = refs[nb:2 * nb]
        send, recv = refs[2 * nb:]
        x, y, c = _me()
        sib = (x, y, 1 - c)
        cps = []
        for m, name in enumerate(names):
            kind, shard, _ = geom[name]
            for k, (qx, qy) in enumerate(_other_chips(x, y)):
                got = _piece_of(arr[m], kind, shard, 2 * qx + qy, c)
                cp = _remote(got, got, send.at[3 * m + k], recv.at[3 * m + k], sib)
                cp.start()
                cps.append(cp)
        for m, name in enumerate(names):
            kind, shard, _ = geom[name]
            for k, (qx, qy) in enumerate(_other_chips(x, y)):
                theirs = _piece_of(arr[m], kind, shard, 2 * qx + qy, 1 - c)
                _remote(theirs, theirs, send.at[3 * m + k], recv.at[3 * m + k], sib).wait_recv()
        for cp in cps:
            cp.wait_send()

    return pl.pallas_call(
        body, name="gather_forward_" + names[0], in_specs=[ANY] * nb, out_specs=[ANY] * nb,
        out_shape=[jax.ShapeDtypeStruct(a.shape, a.dtype) for a in full],
        input_output_aliases={m: m for m in range(nb)},
        scratch_shapes=[pltpu.SemaphoreType.DMA((3 * nb,)), pltpu.SemaphoreType.DMA((3 * nb,))],
    )(*full)


def _scatter_copies(nb, t, got, send, recv):
    x, y, c = _me()
    return [_remote(t[m].at[2 * qx + qy], got[m].at[k], send.at[3 * m + k], recv.at[3 * m + k], (qx, qy, c))
            for m in range(nb) for k, (qx, qy) in enumerate(_other_chips(x, y))]


def _chip_exchange_start(names, chip_sums, geom, after):
    nb = len(names)
    lands = [lax.empty((N_CHIPS - 1,) + geom[n][2], BF16) for n in names]

    def body(*refs):
        t, got = refs[:nb], refs[nb:2 * nb]
        send, recv = refs[2 * nb + 1], refs[2 * nb + 2]
        token = refs[4 * nb + 3]
        for cp in _scatter_copies(nb, t, got, send, recv):
            cp.start()
        token[...] = jnp.zeros_like(token)

    both = list(chip_sums) + lands
    outs = pl.pallas_call(
        body, name="grad_chip_start_" + names[0],
        out_shape=(pltpu.SemaphoreType.DMA((3 * nb,)), pltpu.SemaphoreType.DMA((3 * nb,)),
                   *[pltpu.HBM(a.shape, a.dtype) for a in both], TOKEN),
        in_specs=[HBM] * (2 * nb) + [ANY],
        out_specs=(SEM, SEM, *[HBM] * (2 * nb), pl.BlockSpec(memory_space=pltpu.VMEM)),
        input_output_aliases={m: 2 + m for m in range(2 * nb)},
        compiler_params=pltpu.CompilerParams(has_side_effects=EFFECT),
    )(*[_hbm(a) for a in both], after)
    return outs[0], outs[1], list(outs[2:2 + 2 * nb]), outs[2 + 2 * nb]


def _chip_exchange_wait(names, send, recv, thru, after):
    nb = len(names)

    def body(*refs):
        for cp in _scatter_copies(nb, refs[:nb], refs[nb:2 * nb], refs[2 * nb], refs[2 * nb + 1]):
            cp.wait_send()
            cp.wait_recv()

    outs = pl.pallas_call(
        body, name="grad_chip_wait_" + names[0], out_shape=tuple(pltpu.HBM(t.shape, t.dtype) for t in thru),
        in_specs=[HBM] * (2 * nb) + [SEM, SEM, ANY], out_specs=tuple([HBM] * (2 * nb)),
        input_output_aliases={m: m for m in range(2 * nb)},
        compiler_params=pltpu.CompilerParams(has_side_effects=EFFECT),
    )(*thru, send, recv, after)
    return list(outs[nb:])


def _sibling_copies(names, g, got, geom, send, recv):
    x, y, c = _me()
    out = []
    for m, name in enumerate(names):
        kind, shard, _ = geom[name]
        for r in range(N_CHIPS):
            out.append(_remote(_piece_of(g[m], kind, shard, r, 1 - c), got[m].at[r],
                               send.at[N_CHIPS * m + r], recv.at[N_CHIPS * m + r], (x, y, 1 - c)))
    return out


def _sibling_exchange_start(names, grads, geom, after):
    nb = len(names)
    lands = [lax.empty((N_CHIPS,) + geom[n][2], BF16) for n in names]

    def body(*refs):
        for cp in _sibling_copies(names, refs[:nb], refs[nb:2 * nb], geom, refs[2 * nb + 1], refs[2 * nb + 2]):
            cp.start()
        token = refs[4 * nb + 3]
        token[...] = jnp.zeros_like(token)

    both = [grads[n] for n in names] + lands
    outs = pl.pallas_call(
        body, name="grad_sibling_start_" + names[0],
        out_shape=(pltpu.SemaphoreType.DMA((N_CHIPS * nb,)), pltpu.SemaphoreType.DMA((N_CHIPS * nb,)),
                   *[pltpu.HBM(a.shape, a.dtype) for a in both], TOKEN),
        in_specs=[HBM] * (2 * nb) + [ANY],
        out_specs=(SEM, SEM, *[HBM] * (2 * nb), pl.BlockSpec(memory_space=pltpu.VMEM)),
        input_output_aliases={m: 2 + m for m in range(2 * nb)},
        compiler_params=pltpu.CompilerParams(has_side_effects=EFFECT),
    )(*[_hbm(a) for a in both], after)
    return outs[0], outs[1], list(outs[2:2 + 2 * nb]), outs[2 + 2 * nb]


def _sibling_exchange_wait(names, send, recv, thru, geom, after):
    nb = len(names)

    def body(*refs):
        for cp in _sibling_copies(names, refs[:nb], refs[nb:2 * nb], geom, refs[2 * nb], refs[2 * nb + 1]):
            cp.wait_send()
            cp.wait_recv()

    outs = pl.pallas_call(
        body, name="grad_sibling_wait_" + names[0], out_shape=tuple(pltpu.HBM(t.shape, t.dtype) for t in thru),
        in_specs=[HBM] * (2 * nb) + [SEM, SEM, ANY], out_specs=tuple([HBM] * (2 * nb)),
        input_output_aliases={m: m for m in range(2 * nb)},
        compiler_params=pltpu.CompilerParams(has_side_effects=EFFECT),
    )(*thru, send, recv, after)
    return list(outs[:nb]), list(outs[nb:])


def _allreduce_small(part):
    rows = part.shape[0]
    flips = [(a, b, e) for a in (0, 1) for b in (0, 1) for e in (0, 1) if (a, b, e) != (0, 0, 0)]

    def body(p_ref, o_ref, all_ref, send, recv):
        x, y, c = _me()
        me = 4 * x + 2 * y + c
        all_ref[me] = p_ref[...]
        cps = []
        for k, (a, b, e) in enumerate(flips):
            cp = _remote(p_ref, all_ref.at[me], send.at[k], recv.at[k], (x ^ a, y ^ b, c ^ e))
            cp.start()
            cps.append(cp)
        for k, (a, b, e) in enumerate(flips):
            peer = 4 * (x ^ a) + 2 * (y ^ b) + (c ^ e)
            _remote(p_ref, all_ref.at[peer], send.at[k], recv.at[k], (x ^ a, y ^ b, c ^ e)).wait_recv()
        for cp in cps:
            cp.wait_send()
        tot = all_ref[0]
        for dev in range(1, 8):
            tot = tot + all_ref[dev]
        o_ref[...] = tot

    vm = pl.BlockSpec(memory_space=pltpu.VMEM)
    return pl.pallas_call(
        body, name="allreduce_small", in_specs=[vm], out_specs=vm, out_shape=jax.ShapeDtypeStruct((rows, LANES), F32),
        scratch_shapes=[pltpu.VMEM((8, rows, LANES), F32), pltpu.SemaphoreType.DMA((7,)), pltpu.SemaphoreType.DMA((7,))],
    )(part)


def _rows_tile(rows, cols, mult, elems=1 << 19):
    return _pick(rows, max(mult, elems // cols // mult * mult), mult)


ADD_TILE = 1 << 20


def _add_pairs(g, got, kind, shard, where, name):
    p, r, c = got.shape
    tr = _rows_tile(r, c, 16, ADD_TILE)
    nt = r // tr

    def body(w_ref, a_ref, b_ref, o_ref):
        o_ref[...] = (a_ref[...].astype(F32) + b_ref[...].astype(F32)).astype(BF16)

    if kind == "row":
        g_map = lambda q, i, w_ref: ((2 * q + w_ref[1]) * nt + i, 0)
    else:
        g_map = lambda q, i, w_ref: (w_ref[1] * nt + i, q)
    spec = pl.BlockSpec((None, tr, c), lambda q, i, w_ref: (q, i, 0))
    return pl.pallas_call(
        body, name="grad_add_sibling_" + name,
        grid_spec=pltpu.PrefetchScalarGridSpec(
            num_scalar_prefetch=1, grid=(p, nt), in_specs=[pl.BlockSpec((tr, c), g_map), spec], out_specs=spec),
        out_shape=jax.ShapeDtypeStruct((p, r, c), BF16), compiler_params=_cp(("arbitrary", "arbitrary"), 32),
    )(where, g, got)


def _add_four(t, got, shard, where, name):
    _, r, c = t.shape
    tr = _rows_tile(r, c, 16, ADD_TILE)
    nt = r // tr

    def body(w_ref, own, t0, t1, t2, o_ref):
        o_ref[...] = ((own[...].astype(F32) + t0[...].astype(F32)) + t1[...].astype(F32)) + t2[...].astype(F32)

    spec = lambda q: pl.BlockSpec((None, tr, c), lambda i, w_ref: (q, i, 0))
    return pl.pallas_call(
        body, name="grad_add_chips_" + name,
        grid_spec=pltpu.PrefetchScalarGridSpec(
            num_scalar_prefetch=1, grid=(nt,),
            in_specs=[pl.BlockSpec((None, tr, c), lambda i, w_ref: (w_ref[0], i, 0)), spec(0), spec(1), spec(2)],
            out_specs=pl.BlockSpec((tr, c), lambda i, w_ref: (w_ref[1] * nt + i, 0))),
        out_shape=jax.ShapeDtypeStruct(shard, F32), compiler_params=_cp(("arbitrary",), 48),
    )(where, t, got, got, got)


def _adamw(w, g, m, v, name):
    r, c = w.shape
    tr = _rows_tile(r, c, 8)

    def body(w_ref, g_ref, m_ref, v_ref, go_ref, d_ref, mo_ref, vo_ref):
        gv = g_ref[...]
        mn = ADAM_B1 * m_ref[...] + (1.0 - ADAM_B1) * gv
        vn = ADAM_B2 * v_ref[...] + (1.0 - ADAM_B2) * (gv * gv)
        m_hat = mn / (1.0 - ADAM_B1 ** ADAM_STEP)
        v_hat = vn / (1.0 - ADAM_B2 ** ADAM_STEP)
        go_ref[...] = gv
        d_ref[...] = -ADAM_LR * (m_hat / (jnp.sqrt(v_hat) + ADAM_EPS) + ADAM_WD * w_ref[...])
        mo_ref[...] = mn
        vo_ref[...] = vn

    spec = pl.BlockSpec((tr, c), lambda i: (i, 0))
    return pl.pallas_call(
        body, name=name, grid=(r // tr,), in_specs=[spec] * 4, out_specs=[spec] * 4,
        out_shape=[jax.ShapeDtypeStruct((r, c), F32)] * 4, compiler_params=_cp(("arbitrary",), 32),
    )(w, g, m, v)


def _pack(vectors, rows):
    flat = jnp.concatenate([v.reshape(-1).astype(F32) for v in vectors])
    return jnp.pad(flat, (0, rows * LANES - flat.shape[0])).reshape(rows, LANES)


def _unpack(packed, shapes):
    flat = packed.reshape(-1)
    out, off = [], 0
    for shp in shapes:
        n = 1
        for t in shp:
            n *= t
        out.append(flat[off:off + n].reshape(shp))
        off += n
    return out


def _rows_for(shapes):
    n = sum(functools.reduce(lambda a, b: a * b, shp, 1) for shp in shapes)
    return -(-n // (8 * LANES)) * 8


def kernel(x, w_in, attn_sinks, short_conv_w, w_out, ln1_g, ln1_b, ffn_w_up, ffn_conv_w, ffn_w_down, ln2_g, ln2_b, loss_target, m_w_in, m_attn_sinks, m_short_conv_w, m_w_out, m_ln1_g, m_ln1_b, m_ffn_w_up, m_ffn_conv_w, m_ffn_w_down, m_ln2_g, m_ln2_b, v_w_in, v_attn_sinks, v_short_conv_w, v_w_out, v_ln1_g, v_ln1_b, v_ffn_w_up, v_ffn_conv_w, v_ffn_w_down, v_ln2_g, v_ln2_b):
    xs, tgt = x[0], loss_target[0]
    s, d = xs.shape
    chip = 2 * lax.axis_index("x") + lax.axis_index("y")

    w_big = dict(win_t=w_in[0], wout=w_out[0], wup=ffn_w_up[0], wdown=ffn_w_down[0])
    m_big = dict(win_t=m_w_in[0], wout=m_w_out[0], wup=m_ffn_w_up[0], wdown=m_ffn_w_down[0])
    v_big = dict(win_t=v_w_in[0], wout=v_w_out[0], wup=v_ffn_w_up[0], wdown=v_ffn_w_down[0])
    to_place = dict(w_big, win_t=w_in[0].T)
    geom = _geom({n: to_place[n].shape for n in BIG})
    pad8 = lambda a: jnp.pad(a[0], ((0, 5), (0, 0)))
    where = jnp.stack([chip, lax.axis_index("c")]).astype(jnp.int32)
    placed = {n: _place_shard(to_place[n], where[:1], geom[n][0], n) for n in BIG}
    full, (scw8, fcw8) = _allgather_weights(MIXER, placed, geom, [pad8(short_conv_w), pad8(ffn_conv_w)])
    send, recv, thru, token = _gather_start(FFN, placed, geom, scw8)
    a = _phase_mixer(xs, full["win_t"], full["wout"], scw8, attn_sinks, ln1_g, ln1_b, after=token)
    landed = _gather_forward(FFN, _gather_wait(FFN, send, recv, thru, geom, a["x1b"]), geom)
    full.update(zip(FFN, landed))
    f = _phase_ffn(a, tgt, full["wup"], full["wdown"], fcw8, ln2_g, ln2_b)

    def add_pairs(names, grads, from_sibling):
        return [_add_pairs(grads[m], from_sibling[m], geom[n][0], geom[n][1], where, n) for m, n in enumerate(names)]

    sib_send, sib_recv, sib_thru, sib_token = _sibling_exchange_start(FFN, f, geom, f["st2"])
    started = {}

    def between(dz1b):
        grads, from_sibling = _sibling_exchange_wait(FFN, sib_send, sib_recv, sib_thru, geom, dz1b)
        started["sums"] = add_pairs(FFN, grads, from_sibling)
        started["chip"] = _chip_exchange_start(FFN, started["sums"], geom, f["st2"])
        return started["chip"][3]

    (dproj, dz1b), g_mixer, g_small = _phase_rest(a, f, full["wup"], full["wout"], full["win_t"], scw8, attn_sinks,
                                                  ln1_g + sib_token[0:1, 0:1], between=between)
    ffn_sums = started["sums"]
    send, recv, thru, _ = started["chip"]

    def finish(names, sums, from_chips):
        halves = [_add_four(sums[m], from_chips[m], geom[n][1], where, n) for m, n in enumerate(names)]
        shards = _sibling_assemble(names, halves, geom)
        grads = {n: shards[m].T if n == "win_t" else shards[m] for m, n in enumerate(names)}
        return {n: _adamw(w_big[n], grads[n], m_big[n], v_big[n], "adamw_" + n) for n in names}

    mixer_sums = add_pairs(MIXER, [g_mixer[n] for n in MIXER], _sibling_exchange(MIXER, g_mixer, geom))
    send2, recv2, thru2, token2 = _chip_exchange_start(MIXER, mixer_sums, geom, f["st2"])
    grad_x = _grad_x(dproj, dz1b, full["win_t"], after=token2)
    upd = finish(FFN, ffn_sums, _chip_exchange_wait(FFN, send, recv, thru, grad_x))
    upd.update(finish(MIXER, mixer_sums, _chip_exchange_wait(MIXER, send2, recv2, thru2, upd[FFN[0]][1])))

    small_names = ("ln1_g", "ln1_b", "ln2_g", "ln2_b", "sinks", "fcw", "scw")
    small_shapes = [g_small[n].shape for n in small_names]
    red = _allreduce_small(_pack([g_small["loss_sq"].reshape(1)] + [g_small[n] for n in small_names],
                                 _rows_for([(1,)] + small_shapes)))
    loss_sq, *gs = _unpack(red, [(1,)] + small_shapes)
    gs = dict(zip(small_names, gs))
    loss = (0.5 / d) * loss_sq[0]
    fw, sw = ffn_conv_w.shape[2], short_conv_w.shape[2]
    gs["fcw"] = lax.dynamic_slice_in_dim(gs["fcw"], chip * fw, fw, axis=1)
    gs["scw"] = lax.dynamic_slice_in_dim(gs["scw"], chip * sw, sw, axis=1)

    sm_w = dict(ln1_g=ln1_g[0], ln1_b=ln1_b[0], ln2_g=ln2_g[0], ln2_b=ln2_b[0], sinks=attn_sinks[0],
                fcw=ffn_conv_w[0], scw=short_conv_w[0])
    sm_m = dict(ln1_g=m_ln1_g[0], ln1_b=m_ln1_b[0], ln2_g=m_ln2_g[0], ln2_b=m_ln2_b[0], sinks=m_attn_sinks[0],
                fcw=m_ffn_conv_w[0], scw=m_short_conv_w[0])
    sm_v = dict(ln1_g=v_ln1_g[0], ln1_b=v_ln1_b[0], ln2_g=v_ln2_g[0], ln2_b=v_ln2_b[0], sinks=v_attn_sinks[0],
                fcw=v_ffn_conv_w[0], scw=v_short_conv_w[0])
    shapes = [sm_w[n].shape for n in small_names]
    rows = _rows_for(shapes)
    packed = [_pack([t[n] for n in small_names], rows) for t in (sm_w, gs, sm_m, sm_v)]
    sm_out = [dict(zip(small_names, _unpack(a, shapes))) for a in _adamw(*packed, "adamw_small")]

    def leaf(kind, name):
        if name in ("w_in", "w_out", "ffn_w_up", "ffn_w_down"):
            key = dict(w_in="win_t", w_out="wout", ffn_w_up="wup", ffn_w_down="wdown")[name]
            return upd[key][kind][None]
        key = dict(attn_sinks="sinks", short_conv_w="scw", ffn_conv_w="fcw").get(name, name)
        return sm_out[kind][key][None]

    order = ("w_in", "attn_sinks", "short_conv_w", "w_out", "ln1_g", "ln1_b", "ffn_w_up", "ffn_conv_w", "ffn_w_down",
             "ln2_g", "ln2_b")
    outs = [loss, grad_x[None]]
    for kind in range(4):
        outs += [leaf(kind, n) for n in order]
    return tuple(outs)
```

```python
import functools

import jax
import jax.numpy as jnp
from jax import lax
from jax.experimental import pallas as pl
from jax.experimental.pallas import tpu as pltpu

F32 = jnp.float32
BF16 = jnp.bfloat16
MESH = pl.DeviceIdType.MESH
ANY = pl.BlockSpec(memory_space=pl.ANY)

HEAD_DIM = 64
N_Q_HEADS = 16
N_KV_HEADS = 2
ATTN_WIDTH = N_Q_HEADS * HEAD_DIM
KV_WIDTH = N_KV_HEADS * HEAD_DIM
BLOCK = 128
ROPE_THETA = 10000.0
LN_EPS = 1e-5
ALPHA = 2.0 ** 0.25
NEG_INF = -1e30
ADAM_LR, ADAM_B1, ADAM_B2, ADAM_EPS, ADAM_WD, ADAM_STEP = 0.001, 0.9, 0.999, 1e-08, 0.01, 10
N_CHIPS = 4
LANES = 128
MXU_DIM = 256
SLAB = 128


def _cp(sem, vmem_mb):
    return pltpu.CompilerParams(dimension_semantics=sem, vmem_limit_bytes=vmem_mb << 20)


def _matmul(a, b, *, mode, m, n, k, tm, tn, tk, out_dtype, name, vmem_mb, a_spec=None, b_spec=None,
            res=None, alpha=1.0, after=None, m_outer=False):
    nj, ni, nk = n // tn, m // tm, k // tk
    assert nj * tn == n and ni * tm == m and nk * tk == k, (name, m, n, k, tm, tn, tk)
    if mode == "nn":
        dims = ((1,), (0,))
        a_spec = a_spec or pl.BlockSpec((tm, tk), lambda j, i, kk: (i, kk))
        b_spec = b_spec or pl.BlockSpec((tk, tn), lambda j, i, kk: (kk, j))
    elif mode == "nt":
        dims = ((1,), (1,))
        a_spec = a_spec or pl.BlockSpec((tm, tk), lambda j, i, kk: (i, kk))
        b_spec = b_spec or pl.BlockSpec((tn, tk), lambda j, i, kk: (j, kk))
    else:
        dims = ((0,), (0,))
        a_spec = a_spec or pl.BlockSpec((tk, tm), lambda j, i, kk: (kk, i))
        b_spec = b_spec or pl.BlockSpec((tk, tn), lambda j, i, kk: (kk, j))
    has_res = res is not None
    has_after = after is not None

    def body(*refs):
        refs = refs[1:] if has_after else refs
        a_ref, b_ref = refs[0], refs[1]
        res_ref = refs[2] if has_res else None
        o_ref = refs[2 + has_res]
        part = lax.dot_general(a_ref[...].astype(BF16), b_ref[...].astype(BF16), (dims, ((), ())),
                               preferred_element_type=F32)

        def finish(acc):
            if has_res:
                acc = acc + alpha * res_ref[...].astype(F32)
            o_ref[...] = acc.astype(o_ref.dtype)

        if nk == 1:
            finish(part)
        else:
            acc_ref = refs[3 + has_res]
            kk = pl.program_id(2)

            @pl.when(kk == 0)
            def _():
                acc_ref[...] = part

            @pl.when(kk > 0)
            def _():
                acc_ref[...] += part

            @pl.when(kk == nk - 1)
            def _():
                finish(acc_ref[...])

    in_specs = [a_spec, b_spec]
    args = [a, b]
    if has_res:
        in_specs.append(pl.BlockSpec((tm, tn), lambda j, i, kk: (i, j)))
        args.append(res)
    if has_after:
        in_specs.insert(0, pl.BlockSpec(after.shape, lambda j, i, kk: (0, 0)))
        args.insert(0, after)
    out_spec = pl.BlockSpec((tm, tn), lambda j, i, kk: (i, j))
    grid = (nj, ni, nk)
    if m_outer:
        swap = lambda sp: pl.BlockSpec(sp.block_shape, (lambda f: lambda i, j, kk: f(j, i, kk))(sp.index_map),
                                       pipeline_mode=sp.pipeline_mode)
        in_specs, out_spec, grid = [swap(sp) for sp in in_specs], swap(out_spec), (ni, nj, nk)
    return pl.pallas_call(
        body, name=name, grid=grid, in_specs=in_specs,
        out_specs=out_spec,
        out_shape=jax.ShapeDtypeStruct((m, n), out_dtype),
        scratch_shapes=[pltpu.VMEM((tm, tn), F32)] if nk > 1 else [],
        compiler_params=_cp(("arbitrary", "arbitrary", "arbitrary"), vmem_mb),
    )(*args)


def _pick(total, want, mult):
    if total <= want:
        return total
    for t in range(want, 0, -1):
        if total % t == 0 and t % mult == 0:
            return t
    return total


def _rope_tables(s):
    half = HEAD_DIM // 2
    inv_freq = ROPE_THETA ** (-jnp.arange(half, dtype=F32) / half)
    ang = jnp.arange(s, dtype=F32)[:, None] * inv_freq[None, :]
    cos = jnp.tile(jnp.cos(ang), (1, LANES // half))
    sin = jnp.tile(jnp.concatenate([-jnp.sin(ang), jnp.sin(ang)], axis=1), (1, LANES // HEAD_DIM))
    return cos, sin


def _rope(x, cos, sin, lo):
    partner = jnp.where(lo, pltpu.roll(x, LANES - HEAD_DIM // 2, 1), pltpu.roll(x, HEAD_DIM // 2, 1))
    return x * cos + partner * sin


def _dot(a, b, dims):
    return lax.dot_general(a, b, (dims, ((), ())), preferred_element_type=F32)


NN, NT, TN = ((1,), (0,)), ((1,), (1,)), ((0,), (0,))


def _kv_variants(t, head_lo):
    r = pltpu.roll(t, HEAD_DIM, 1)
    zero = jnp.zeros_like(t)
    a = (jnp.where(head_lo, t, zero).astype(BF16), jnp.where(head_lo, r, zero).astype(BF16))
    b = (jnp.where(head_lo, zero, r).astype(BF16), jnp.where(head_lo, zero, t).astype(BF16))
    return a, b


PAIRS_PER_KV = N_Q_HEADS // 2 // N_KV_HEADS
STACK = PAIRS_PER_KV * BLOCK


def _stack_pairs(ref, j, fn):
    return jnp.concatenate([fn(ref[:, p * LANES:(p + 1) * LANES])
                            for p in range(j * PAIRS_PER_KV, (j + 1) * PAIRS_PER_KV)], axis=0)


def _sink_row(sink_ref, j, hh):
    col = lax.broadcasted_iota(jnp.int32, (1, STACK), 1)
    heads = [2 * p + hh for p in range(j * PAIRS_PER_KV, (j + 1) * PAIRS_PER_KV)]
    row = jnp.full((1, STACK), sink_ref[0, heads[-1]], F32)
    for t in range(PAIRS_PER_KV - 2, -1, -1):
        row = jnp.where(col < (t + 1) * BLOCK, sink_ref[0, heads[t]], row)
    return row


def _attn_exps(qp, ka, kb, valid, sink_a, sink_b):
    out = []
    for kk, sink in ((ka, sink_a), (kb, sink_b)):
        s = jnp.where(valid, _dot(kk, qp, NT), NEG_INF)
        mx = jnp.maximum(jnp.max(s, axis=0, keepdims=True), sink)
        out.append((jnp.exp(s - mx), jnp.exp(sink - mx)))
    return out


def _attn_common(i, q_ref, k_ref, v_ref, kp_ref, vp_ref, cos_ref, sin_ref, cosp_ref, sinp_ref):
    lane = lax.broadcasted_iota(jnp.int32, (1, LANES), 1)
    lo = (lane % HEAD_DIM) < (HEAD_DIM // 2)
    head_lo = lane < HEAD_DIM
    cos, sin = cos_ref[...], sin_ref[...]
    kc = _rope(k_ref[...].astype(F32), cos, sin, lo)
    kp = _rope(kp_ref[...].astype(F32), cosp_ref[...], sinp_ref[...], lo)
    kext = jnp.concatenate([kp, kc], axis=0)
    vext = jnp.concatenate([vp_ref[...].astype(F32), v_ref[...].astype(F32)], axis=0)
    ka, kb = _kv_variants(kext, head_lo)
    va, vb = _kv_variants(vext, head_lo)
    qi = lax.broadcasted_iota(jnp.int32, (1, STACK), 1) % BLOCK
    kj = lax.broadcasted_iota(jnp.int32, (2 * BLOCK, 1), 0)
    valid = (kj > qi) & (kj <= qi + BLOCK) & ((kj >= BLOCK) | (i > 0))
    cos4 = jnp.concatenate([cos] * PAIRS_PER_KV, axis=0)
    sin4 = jnp.concatenate([sin] * PAIRS_PER_KV, axis=0)
    return lo, head_lo, cos, sin, cos4, sin4, ka, kb, va, vb, valid


def _attn_fwd(proj, sinks, cos, sin, s):
    nb = s // BLOCK
    kcol, vcol = ATTN_WIDTH // LANES, ATTN_WIDTH // LANES + 1

    def body(q_ref, k_ref, v_ref, kp_ref, vp_ref, cos_ref, sin_ref, cosp_ref, sinp_ref, sink_ref, o_ref):
        i = pl.program_id(0)
        lo, head_lo, cs, sn, cs4, sn4, ka, kb, va, vb, valid = _attn_common(
            i, q_ref, k_ref, v_ref, kp_ref, vp_ref, cos_ref, sin_ref, cosp_ref, sinp_ref)
        row = lax.broadcasted_iota(jnp.int32, (16, 1), 0)
        one = jnp.ones((), BF16)
        for j in range(N_KV_HEADS):
            q4 = _stack_pairs(q_ref, j, lambda t: t.astype(F32))
            qp = (_rope(q4, cs4, sn4, lo) * HEAD_DIM ** -0.5).astype(BF16)
            exps = _attn_exps(qp, ka[j], kb[j], valid, _sink_row(sink_ref, j, 0), _sink_row(sink_ref, j, 1))
            outs = []
            for (e, es), vv, mine in zip(exps, (va[j], vb[j]), (head_lo, ~head_lo)):
                ee = jnp.concatenate([e.astype(BF16), jnp.where(row == 0, es, 0.0).astype(BF16)], axis=0)
                tail = jnp.where((row == 0) & ~mine, one, jnp.zeros((), BF16))
                vx = jnp.concatenate([jnp.where(mine, vv, one), tail], axis=0)
                un = _dot(ee, vx, TN)
                outs.append(un / pltpu.roll(un, HEAD_DIM, 1))
            o = jnp.where(head_lo, outs[0], outs[1]).astype(BF16)
            for t in range(PAIRS_PER_KV):
                p = j * PAIRS_PER_KV + t
                o_ref[:, p * LANES:(p + 1) * LANES] = o[t * BLOCK:(t + 1) * BLOCK]

    prev = lambda i: (jnp.maximum(i - 1, 0), 0)
    return pl.pallas_call(
        body, name="attn_fwd", grid=(nb,),
        in_specs=[pl.BlockSpec((BLOCK, ATTN_WIDTH), lambda i: (i, 0)),
                  pl.BlockSpec((BLOCK, LANES), lambda i: (i, kcol)),
                  pl.BlockSpec((BLOCK, LANES), lambda i: (i, vcol)),
                  pl.BlockSpec((BLOCK, LANES), lambda i: (jnp.maximum(i - 1, 0), kcol)),
                  pl.BlockSpec((BLOCK, LANES), lambda i: (jnp.maximum(i - 1, 0), vcol)),
                  pl.BlockSpec((BLOCK, LANES), lambda i: (i, 0)),
                  pl.BlockSpec((BLOCK, LANES), lambda i: (i, 0)),
                  pl.BlockSpec((BLOCK, LANES), prev),
                  pl.BlockSpec((BLOCK, LANES), prev),
                  pl.BlockSpec(memory_space=pltpu.SMEM)],
        out_specs=pl.BlockSpec((BLOCK, ATTN_WIDTH), lambda i: (i, 0)),
        out_shape=jax.ShapeDtypeStruct((s, ATTN_WIDTH), BF16),
        compiler_params=_cp(("arbitrary",), 32),
    )(proj, proj, proj, proj, proj, cos, sin, cos, sin, sinks)


def _attn_bwd(proj, dmix, sinks, cos, sin, s):
    nb = s // BLOCK
    kcol, vcol = ATTN_WIDTH // LANES, ATTN_WIDTH // LANES + 1
    pairs_per_kv = N_Q_HEADS // 2 // N_KV_HEADS

    def body(q_ref, k_ref, v_ref, kp_ref, vp_ref, cos_ref, sin_ref, cosp_ref, sinp_ref, sink_ref, do_ref,
             dq_ref, dk_ref, dv_ref, dsink_ref, ck_ref, cv_ref):
        g = pl.program_id(0)
        i = nb - 1 - g

        @pl.when(g == 0)
        def _():
            ck_ref[...] = jnp.zeros_like(ck_ref)
            cv_ref[...] = jnp.zeros_like(cv_ref)
            dsink_ref[...] = jnp.zeros_like(dsink_ref)

        lo, head_lo, cs, sn, cs4, sn4, ka, kb, va, vb, valid = _attn_common(
            i, q_ref, k_ref, v_ref, kp_ref, vp_ref, cos_ref, sin_ref, cosp_ref, sinp_ref)
        lane = lax.broadcasted_iota(jnp.int32, (1, LANES), 1)
        dk_j, dv_j = [], []
        dsink = jnp.zeros((1, LANES), F32)
        for j in range(N_KV_HEADS):
            q4 = _stack_pairs(q_ref, j, lambda t: t.astype(F32))
            qp = (_rope(q4, cs4, sn4, lo) * HEAD_DIM ** -0.5).astype(BF16)
            exps = _attn_exps(qp, ka[j], kb[j], valid, _sink_row(sink_ref, j, 0), _sink_row(sink_ref, j, 1))
            do = _stack_pairs(do_ref, j, lambda t: t)
            dq_r = jnp.zeros((STACK, LANES), F32)
            dkc, dvc = [], []
            for hh, ((e, es), kk, vv) in enumerate(zip(exps, (ka[j], kb[j]), (va[j], vb[j]))):
                inv = 1.0 / (jnp.sum(e, axis=0, keepdims=True) + es)
                pr = e * inv
                dp = _dot(vv, do, NT)
                delta = jnp.sum(pr * dp, axis=0, keepdims=True)
                ds = (pr * (dp - delta)).astype(BF16)
                psd = es * inv * delta
                for t in range(PAIRS_PER_KV):
                    head = 2 * (j * PAIRS_PER_KV + t) + hh
                    dsink = dsink + jnp.where(
                        lane == head, -jnp.sum(psd[:, t * BLOCK:(t + 1) * BLOCK], axis=1, keepdims=True), 0.0)
                dq_r = dq_r + _dot(ds, kk, TN)
                dkc.append(_dot(ds, qp, NN))
                dvc.append(_dot(pr.astype(BF16), do, NN))
            dk_j.append(jnp.where(head_lo, dkc[0], dkc[1]))
            dv_j.append(jnp.where(head_lo, dvc[0], dvc[1]))
            dq = _rope(dq_r * HEAD_DIM ** -0.5, cs4, -sn4, lo).astype(BF16)
            for t in range(PAIRS_PER_KV):
                p = j * PAIRS_PER_KV + t
                dq_ref[:, p * LANES:(p + 1) * LANES] = dq[t * BLOCK:(t + 1) * BLOCK]
        tot_k = [t + pltpu.roll(t, HEAD_DIM, 1) for t in dk_j]
        tot_v = [t + pltpu.roll(t, HEAD_DIM, 1) for t in dv_j]
        dkext = jnp.where(head_lo, tot_k[0], tot_k[1])
        dvext = jnp.where(head_lo, tot_v[0], tot_v[1])
        dk_r = dkext[BLOCK:] + ck_ref[...]
        dk_ref[...] = _rope(dk_r, cs, -sn, lo).astype(BF16)
        dv_ref[...] = (dvext[BLOCK:] + cv_ref[...]).astype(BF16)
        ck_ref[...] = dkext[:BLOCK]
        cv_ref[...] = dvext[:BLOCK]
        dsink_ref[0:1, :] += dsink

    cur = lambda col: (lambda g: (nb - 1 - g, col))
    prv = lambda col: (lambda g: (jnp.maximum(nb - 2 - g, 0), col))
    blk = lambda w, f: pl.BlockSpec((BLOCK, w), f)
    return pl.pallas_call(
        body, name="attn_bwd", grid=(nb,),
        in_specs=[blk(ATTN_WIDTH, cur(0)), blk(LANES, cur(kcol)), blk(LANES, cur(vcol)),
                  blk(LANES, prv(kcol)), blk(LANES, prv(vcol)),
                  blk(LANES, cur(0)), blk(LANES, cur(0)), blk(LANES, prv(0)), blk(LANES, prv(0)),
                  pl.BlockSpec(memory_space=pltpu.SMEM),
                  blk(ATTN_WIDTH, cur(0))],
        out_specs=[blk(ATTN_WIDTH, cur(0)), blk(LANES, cur(0)), blk(LANES, cur(0)),
                   pl.BlockSpec((8, LANES), lambda g: (0, 0))],
        out_shape=[jax.ShapeDtypeStruct((s, ATTN_WIDTH), BF16), jax.ShapeDtypeStruct((s, LANES), BF16),
                   jax.ShapeDtypeStruct((s, LANES), BF16), jax.ShapeDtypeStruct((8, LANES), F32)],
        scratch_shapes=[pltpu.VMEM((BLOCK, LANES), F32), pltpu.VMEM((BLOCK, LANES), F32)],
        compiler_params=_cp(("arbitrary",), 32),
    )(proj, proj, proj, proj, proj, cos, sin, cos, sin, sinks, dmix)


def _causal_conv(x, prev8, w):
    row = lax.broadcasted_iota(jnp.int32, (8, 1), 0)
    r1, r2 = pltpu.roll(x, 1, 0), pltpu.roll(x, 2, 0)
    s1 = jnp.concatenate([jnp.where(row == 0, prev8[7:8], r1[:8]), r1[8:]], axis=0)
    s2 = jnp.concatenate([jnp.where(row == 0, prev8[6:7], jnp.where(row == 1, prev8[7:8], r2[:8])), r2[8:]], axis=0)
    return w[0:1] * s2 + w[1:2] * s1 + w[2:3] * x


def _conv_bwd(dy, x, w, next8):
    r = x.shape[0]
    row = lax.broadcasted_iota(jnp.int32, (8, 1), 0)
    r1, r2 = pltpu.roll(dy, r - 1, 0), pltpu.roll(dy, r - 2, 0)
    n1 = jnp.concatenate([r1[:r - 8], jnp.where(row == 7, next8[0:1], r1[r - 8:])], axis=0)
    n2 = jnp.concatenate([r2[:r - 8], jnp.where(row == 6, next8[0:1], jnp.where(row == 7, next8[1:2], r2[r - 8:]))],
                         axis=0)
    dx = w[2:3] * dy + w[1:2] * n1 + w[0:1] * n2
    dws = [jnp.sum(t * x, axis=0, keepdims=True) for t in (n2, n1, dy)]
    return dx, dws


CONV_COLS = 256


def _convmix_cols(d):
    conv_w = d - ATTN_WIDTH
    base = (ATTN_WIDTH + 2 * KV_WIDTH) // CONV_COLS
    step = conv_w // CONV_COLS
    return base, base + step, base + 2 * step, step


def _convmix_fwd(proj, scw8, s, d):
    gb0, gc0, h0, ncb = _convmix_cols(d)
    tr = _pick(s, 1024, 16)
    ni = s // tr

    def body(gb_ref, gc_ref, h_ref, w_ref, o_ref, carry_ref):
        @pl.when(pl.program_id(1) == 0)
        def _():
            carry_ref[...] = jnp.zeros_like(carry_ref)

        gch = gc_ref[...].astype(F32) * h_ref[...].astype(F32)
        cc = _causal_conv(gch, carry_ref[...], w_ref[...])
        o_ref[...] = (gb_ref[...].astype(F32) * cc).astype(BF16)
        carry_ref[...] = gch[tr - 8:]

    spec = lambda c0: pl.BlockSpec((tr, CONV_COLS), lambda j, i: (i, c0 + j))
    return pl.pallas_call(
        body, name="convmix_fwd", grid=(ncb, ni),
        in_specs=[spec(gb0), spec(gc0), spec(h0), pl.BlockSpec((8, CONV_COLS), lambda j, i: (0, j))],
        out_specs=pl.BlockSpec((tr, CONV_COLS), lambda j, i: (i, j)),
        out_shape=jax.ShapeDtypeStruct((s, d - ATTN_WIDTH), BF16),
        scratch_shapes=[pltpu.VMEM((8, CONV_COLS), F32)],
        compiler_params=_cp(("arbitrary", "arbitrary"), 32),
    )(proj, proj, proj, scw8)


def _convmix_bwd(proj, dmix, scw8, s, d):
    gb0, gc0, h0, ncb = _convmix_cols(d)
    tr = _pick(s, 1024, 16)
    ni = s // tr
    dc0 = ATTN_WIDTH // CONV_COLS

    def body(dc_ref, gb_ref, gc_ref, h_ref, gcp_ref, hp_ref, w_ref, d3_ref, dw_ref, nxt_ref):
        g = pl.program_id(1)
        i = ni - 1 - g

        @pl.when(g == 0)
        def _():
            nxt_ref[...] = jnp.zeros_like(nxt_ref)
            dw_ref[...] = jnp.zeros_like(dw_ref)

        w = w_ref[...]
        gb, gc, h = gb_ref[...].astype(F32), gc_ref[...].astype(F32), h_ref[...].astype(F32)
        gch = gc * h
        prev8 = (gcp_ref[...].astype(F32) * hp_ref[...].astype(F32))[8:16] * (i > 0).astype(F32)
        cc = _causal_conv(gch, prev8, w)
        dc = dc_ref[...].astype(F32)
        dcc = dc * gb
        dgch, dws = _conv_bwd(dcc, gch, w, nxt_ref[...])
        d3_ref[0] = (dc * cc).astype(BF16)
        d3_ref[1] = (dgch * h).astype(BF16)
        d3_ref[2] = (dgch * gc).astype(BF16)
        for t in range(3):
            dw_ref[t:t + 1, :] += dws[t]
        nxt_ref[...] = dcc[0:8]

    cur = lambda c0: pl.BlockSpec((tr, CONV_COLS), lambda j, g: (ni - 1 - g, c0 + j))
    prv = lambda c0: pl.BlockSpec((16, CONV_COLS), lambda j, g: (jnp.maximum((ni - 1 - g) * (tr // 16) - 1, 0), c0 + j))
    return pl.pallas_call(
        body, name="convmix_bwd", grid=(ncb, ni),
        in_specs=[cur(dc0), cur(gb0), cur(gc0), cur(h0), prv(gc0), prv(h0),
                  pl.BlockSpec((8, CONV_COLS), lambda j, g: (0, j))],
        out_specs=[pl.BlockSpec((3, tr, CONV_COLS), lambda j, g: (0, ni - 1 - g, j)),
                   pl.BlockSpec((8, CONV_COLS), lambda j, g: (0, j))],
        out_shape=[jax.ShapeDtypeStruct((3, s, d - ATTN_WIDTH), BF16), jax.ShapeDtypeStruct((8, d - ATTN_WIDTH), F32)],
        scratch_shapes=[pltpu.VMEM((8, CONV_COLS), F32)],
        compiler_params=_cp(("arbitrary", "arbitrary"), 32),
    )(dmix, proj, proj, proj, proj, proj, scw8)


def _ln_fwd(z):
    mu = jnp.mean(z, axis=-1, keepdims=True)
    zc = z - mu
    var = jnp.mean(zc * zc, axis=-1, keepdims=True)
    rstd = lax.rsqrt(var + LN_EPS)
    return zc * rstd, rstd


def _ln_bwd(dout, xh, rstd, g):
    dxh = dout * g
    c1 = jnp.mean(dxh, axis=-1, keepdims=True)
    c2 = jnp.mean(dxh * xh, axis=-1, keepdims=True)
    dz = rstd * (dxh - c1 - xh * c2)
    return dz, jnp.sum(dout * xh, axis=0, keepdims=True), jnp.sum(dout, axis=0, keepdims=True)


def _outproj_ln1(attn, conv, wout, x, g1, b1, s, d):
    tm = _pick(s, 512, 16)
    ka = attn.shape[1]
    one_buffer = pl.Buffered(1)

    def body(a_ref, c_ref, wt_ref, wb_ref, x_ref, g_ref, b_ref, x1_ref, x1b_ref, xh_ref, rs_ref):
        y = _dot(a_ref[...], wt_ref[...], NN) + _dot(c_ref[...], wb_ref[...], NN)
        xh, rstd = _ln_fwd(ALPHA * x_ref[...] + y)
        x1 = xh * g_ref[...] + b_ref[...]
        x1_ref[...] = x1
        x1b_ref[...] = x1.astype(BF16)
        xh_ref[...] = xh.astype(BF16)
        rs_ref[...] = rstd

    row = lambda w: pl.BlockSpec((tm, w), lambda i: (i, 0))
    vec = pl.BlockSpec((1, d), lambda i: (0, 0))
    return pl.pallas_call(
        body, name="outproj_ln1", grid=(s // tm,),
        in_specs=[row(ka), row(d - ka), pl.BlockSpec((ka, d), lambda i: (0, 0), pipeline_mode=one_buffer),
                  pl.BlockSpec((d - ka, d), lambda i: (ka // (d - ka), 0), pipeline_mode=one_buffer), row(d), vec, vec],
        out_specs=[row(d), row(d), row(d), row(1)],
        out_shape=[jax.ShapeDtypeStruct((s, d), F32), jax.ShapeDtypeStruct((s, d), BF16),
                   jax.ShapeDtypeStruct((s, d), BF16), jax.ShapeDtypeStruct((s, 1), F32)],
        compiler_params=_cp(("arbitrary",), 56),
    )(attn, conv, wout, wout, x, g1, b1)


def _ffn_up(x1b, wup, fcw8, s, d, dff):
    tm = _pick(s, 1024, 16)
    tn = _pick(dff, 512, LANES)
    nj, ni = dff // tn, s // tm

    def body(x_ref, wa_ref, wg_ref, ca_ref, cg_ref, u_ref, y_ref, h_ref, carry_ref):
        @pl.when(pl.program_id(1) == 0)
        def _():
            carry_ref[...] = jnp.zeros_like(carry_ref)

        xa = x_ref[...]
        ys = []
        for part, (w_ref, c_ref) in enumerate(((wa_ref, ca_ref), (wg_ref, cg_ref))):
            ub = _dot(xa, w_ref[...], NN).astype(BF16)
            u_ref[part] = ub
            u = ub.astype(F32)
            y = _causal_conv(u, carry_ref[part], c_ref[...])
            carry_ref[part] = u[tm - 8:]
            yb = y.astype(BF16)
            y_ref[part] = yb
            ys.append(yb.astype(F32))
        a2, g2 = ys
        sig = 1.0 / (1.0 + jnp.exp(-a2))
        h_ref[...] = (a2 * sig * g2).astype(BF16)

    return pl.pallas_call(
        body, name="ffn_up", grid=(nj, ni),
        in_specs=[pl.BlockSpec((tm, d), lambda j, i: (i, 0)),
                  pl.BlockSpec((d, tn), lambda j, i: (0, j)),
                  pl.BlockSpec((d, tn), lambda j, i: (0, j + nj)),
                  pl.BlockSpec((8, tn), lambda j, i: (0, j)),
                  pl.BlockSpec((8, tn), lambda j, i: (0, j + nj))],
        out_specs=[pl.BlockSpec((2, tm, tn), lambda j, i: (0, i, j)),
                   pl.BlockSpec((2, tm, tn), lambda j, i: (0, i, j)),
                   pl.BlockSpec((tm, tn), lambda j, i: (i, j))],
        out_shape=[jax.ShapeDtypeStruct((2, s, dff), BF16), jax.ShapeDtypeStruct((2, s, dff), BF16),
                   jax.ShapeDtypeStruct((s, dff), BF16)],
        scratch_shapes=[pltpu.VMEM((2, 8, tn), F32)],
        compiler_params=_cp(("arbitrary", "arbitrary"), 56),
    )(x1b, wup, wup, fcw8, fcw8)


def _ffn_mid_bwd(dz2b, wdown, u3, y3, fcw8, s, d, dff):
    tm = _pick(s, 1024, 16)
    tn = _pick(dff, 512, LANES)
    nj, ni = dff // tn, s // tm

    def body(dz_ref, wd_ref, u_ref, y_ref, ca_ref, cg_ref, du_ref, dw_ref, nxt_ref):
        @pl.when(pl.program_id(1) == 0)
        def _():
            nxt_ref[...] = jnp.zeros_like(nxt_ref)
            dw_ref[...] = jnp.zeros_like(dw_ref)

        a2, g2 = y_ref[0].astype(F32), y_ref[1].astype(F32)
        sig = 1.0 / (1.0 + jnp.exp(-a2))
        silu = a2 * sig
        dhv = _dot(dz_ref[...], wd_ref[...], NT)
        dys = (dhv * g2 * (sig * (1.0 + a2 * (1.0 - sig))), dhv * silu)
        for part, (c_ref, dy) in enumerate(zip((ca_ref, cg_ref), dys)):
            dx, dws = _conv_bwd(dy, u_ref[part].astype(F32), c_ref[...], nxt_ref[part])
            du_ref[part] = dx.astype(BF16)
            for t in range(3):
                dw_ref[part, t:t + 1, :] += dws[t]
            nxt_ref[part] = dy[0:8]

    return pl.pallas_call(
        body, name="ffn_mid_bwd", grid=(nj, ni),
        in_specs=[pl.BlockSpec((tm, d), lambda j, g: (ni - 1 - g, 0)),
                  pl.BlockSpec((tn, d), lambda j, g: (j, 0)),
                  pl.BlockSpec((2, tm, tn), lambda j, g: (0, ni - 1 - g, j)),
                  pl.BlockSpec((2, tm, tn), lambda j, g: (0, ni - 1 - g, j)),
                  pl.BlockSpec((8, tn), lambda j, g: (0, j)),
                  pl.BlockSpec((8, tn), lambda j, g: (0, j + nj))],
        out_specs=[pl.BlockSpec((2, tm, tn), lambda j, g: (0, ni - 1 - g, j)),
                   pl.BlockSpec((2, 8, tn), lambda j, g: (0, 0, j))],
        out_shape=[jax.ShapeDtypeStruct((2, s, dff), BF16), jax.ShapeDtypeStruct((2, 8, dff), F32)],
        scratch_shapes=[pltpu.VMEM((2, 8, tn), F32)],
        compiler_params=_cp(("arbitrary", "arbitrary"), 56),
    )(dz2b, wdown, u3, y3, fcw8, fcw8)


def _ffn_down_loss(hmid, wdown, x1, target, g2, b2, s, d, dff):
    tm = _pick(s, 512, SLAB)
    tk = _pick(dff, 1408, LANES)
    ni, nk = s // tm, dff // tk
    slab = min(SLAB, tm)

    def body(h_ref, w_ref, x1_ref, t_ref, g_ref, b_ref, dzb_ref, st_ref, acc_ref):
        i, kk = pl.program_id(0), pl.program_id(1)

        @pl.when((i == 0) & (kk == 0))
        def _():
            st_ref[...] = jnp.zeros_like(st_ref)

        part = _dot(h_ref[...], w_ref[...], NN)

        @pl.when(kk == 0)
        def _():
            acc_ref[...] = part

        @pl.when(kk > 0)
        def _():
            acc_ref[...] += part

        @pl.when(kk == nk - 1)
        def _():
            g, b = g_ref[...], b_ref[...]

            def one(sl, carry):
                rows = pl.ds(pl.multiple_of(sl * slab, slab), slab)
                xh, rstd = _ln_fwd(ALPHA * x1_ref[rows, :] + acc_ref[rows, :])
                diff = xh * g + b - t_ref[rows, :]
                sq = jnp.sum(jnp.sum(diff * diff, axis=1, keepdims=True), axis=0, keepdims=True)
                dz, dg, db = _ln_bwd(diff * (1.0 / d), xh, rstd, g)
                dzb_ref[rows, :] = dz.astype(BF16)
                st_ref[0:1, :] += dg
                st_ref[1:2, :] += db
                st_ref[2:3, :] += sq
                return carry

            lax.fori_loop(0, tm // slab, one, 0)

    row = pl.BlockSpec((tm, d), lambda i, kk: (i, 0))
    vec = pl.BlockSpec((1, d), lambda i, kk: (0, 0))
    return pl.pallas_call(
        body, name="ffn_down_loss", grid=(ni, nk),
        in_specs=[pl.BlockSpec((tm, tk), lambda i, kk: (i, kk)), pl.BlockSpec((tk, d), lambda i, kk: (kk, 0)),
                  row, row, vec, vec],
        out_specs=[row, pl.BlockSpec((8, d), lambda i, kk: (0, 0))],
        out_shape=[jax.ShapeDtypeStruct((s, d), BF16), jax.ShapeDtypeStruct((8, d), F32)],
        scratch_shapes=[pltpu.VMEM((tm, d), F32)],
        compiler_params=_cp(("arbitrary", "arbitrary"), 48),
    )(hmid, wdown, x1, target, g2, b2)


def _ffn_dx_ln1_bwd(du3, wup, dz2b, xh1, rstd1, g1, s, d, dff):
    tm = _pick(s, 512, SLAB)
    tk = _pick(dff, 2816, MXU_DIM)
    nkh = dff // tk
    ni, nk = s // tm, 2 * nkh
    slab = min(SLAB, tm)

    def body(a_ref, w_ref, dz2_ref, xh_ref, rs_ref, g_ref, dzb_ref, st_ref, acc_ref):
        i, kk = pl.program_id(0), pl.program_id(1)

        @pl.when((i == 0) & (kk == 0))
        def _():
            st_ref[...] = jnp.zeros_like(st_ref)

        part = _dot(a_ref[...], w_ref[...], NT)

        @pl.when(kk == 0)
        def _():
            acc_ref[...] = part

        @pl.when(kk > 0)
        def _():
            acc_ref[...] += part

        @pl.when(kk == nk - 1)
        def _():
            g = g_ref[...]

            def one(sl, carry):
                rows = pl.ds(pl.multiple_of(sl * slab, slab), slab)
                dx1 = ALPHA * dz2_ref[rows, :].astype(F32) + acc_ref[rows, :]
                dz, dg, db = _ln_bwd(dx1, xh_ref[rows, :].astype(F32), rs_ref[rows, :], g)
                dzb_ref[rows, :] = dz.astype(BF16)
                st_ref[0:1, :] += dg
                st_ref[1:2, :] += db
                return carry

            lax.fori_loop(0, tm // slab, one, 0)

    row = pl.BlockSpec((tm, d), lambda i, kk: (i, 0))
    row1 = pl.BlockSpec((tm, d), lambda i, kk: (i, 0), pipeline_mode=pl.Buffered(1))
    return pl.pallas_call(
        body, name="ffn_dx_ln1_bwd", grid=(ni, nk),
        in_specs=[pl.BlockSpec((None, tm, tk), lambda i, kk: (kk // nkh, i, kk % nkh)),
                  pl.BlockSpec((d, tk), lambda i, kk: (0, kk)),
                  row1, row1, pl.BlockSpec((tm, 1), lambda i, kk: (i, 0)), pl.BlockSpec((1, d), lambda i, kk: (0, 0))],
        out_specs=[row, pl.BlockSpec((8, d), lambda i, kk: (0, 0))],
        out_shape=[jax.ShapeDtypeStruct((s, d), BF16), jax.ShapeDtypeStruct((8, d), F32)],
        scratch_shapes=[pltpu.VMEM((tm, d), F32)],
        compiler_params=_cp(("arbitrary", "arbitrary"), 56),
    )(du3, wup, dz2b, xh1, rstd1, g1)


def _phase_mixer(x, win_t, wout, scw8, sinks, ln1_g, ln1_b, after=None):
    s, d = x.shape
    n_in = win_t.shape[0]
    cos, sin = _rope_tables(s)
    proj = _matmul(x, win_t, mode="nt", m=s, n=n_in, k=d, tm=_pick(s, 512, 16), tn=n_in, tk=d, out_dtype=BF16,
                   name="in_proj", vmem_mb=52, after=after,
                   b_spec=pl.BlockSpec((n_in, d), lambda j, i, kk: (0, 0), pipeline_mode=pl.Buffered(1)))
    attn = _attn_fwd(proj, sinks, cos, sin, s)
    conv = _convmix_fwd(proj, scw8, s, d)
    x1, x1b, xh1, rstd1 = _outproj_ln1(attn, conv, wout, x, ln1_g, ln1_b, s, d)
    return dict(x=x, cos=cos, sin=sin, proj=proj, attn=attn, conv=conv, x1=x1, x1b=x1b, xh1=xh1, rstd1=rstd1)


def _phase_ffn(a, target, wup, wdown, fcw8, ln2_g, ln2_b):
    x1, x1b = a["x1"], a["x1b"]
    s, d = x1.shape
    dff = wdown.shape[0]
    u3, y3, hmid = _ffn_up(x1b, wup, fcw8, s, d, dff)
    dz2b, st2 = _ffn_down_loss(hmid, wdown, x1, target, ln2_g, ln2_b, s, d, dff)

    tnw = _pick(d, 1024, MXU_DIM)
    keep_b = pl.BlockSpec((s, tnw), lambda j, i, kk: (kk, j), pipeline_mode=pl.Buffered(1))
    g_wdown = _matmul(hmid, dz2b, mode="tn", m=dff, n=d, k=s, tm=_pick(dff, 512, MXU_DIM), tn=tnw, tk=s,
                      out_dtype=BF16, name="grad_w_down", vmem_mb=56, b_spec=keep_b)
    du3, dfcw = _ffn_mid_bwd(dz2b, wdown, u3, y3, fcw8, s, d, dff)
    tnu = _pick(dff, 512, MXU_DIM)
    njh = dff // tnu
    tmu = _pick(d, 1024, LANES)
    g_wup = _matmul(x1b, du3, mode="tn", m=d, n=2 * dff, k=s, tm=tmu, tn=tnu, tk=s, out_dtype=BF16,
                    name="grad_w_up", vmem_mb=58, m_outer=True,
                    a_spec=pl.BlockSpec((s, tmu), lambda j, i, kk: (kk, i), pipeline_mode=pl.Buffered(1)),
                    b_spec=pl.BlockSpec((None, s, tnu), lambda j, i, kk: (j // njh, kk, j % njh)))
    return dict(du3=du3, dz2b=dz2b, st2=st2, dfcw=dfcw, wdown=g_wdown, wup=g_wup)


def _phase_rest(a, f, wup, wout, win_t, scw8, sinks, ln1_g, between=None):
    xb, cos, sin, proj, attn, conv = a["x"], a["cos"], a["sin"], a["proj"], a["attn"], a["conv"]
    du3, dz2b, st2, dfcw = f["du3"], f["dz2b"], f["st2"], f["dfcw"]
    s, d = a["x1"].shape
    dff = wup.shape[1] // 2
    n_in = win_t.shape[0]
    ts = _pick(s, 2048, 16)
    dz1b, st1 = _ffn_dx_ln1_bwd(du3, wup, dz2b, a["xh1"], a["rstd1"], ln1_g, s, d, dff)
    after = between(dz1b) if between is not None else None

    tnw = _pick(d, 1024, MXU_DIM)
    halves = [_matmul(part, dz1b, mode="tn", m=part.shape[1], n=d, k=s, tm=_pick(part.shape[1], 512, LANES), tn=tnw,
                      tk=s, out_dtype=BF16, name="grad_w_out_" + tag, vmem_mb=56, after=after,
                      b_spec=pl.BlockSpec((s, tnw), lambda j, i, kk: (kk, j), pipeline_mode=pl.Buffered(1)))
              for tag, part in (("attn", attn), ("conv", conv))]
    g_wout = jnp.concatenate(halves, axis=0)
    dmix = _matmul(dz1b, wout, mode="nt", m=s, n=d, k=d, tm=_pick(s, 1024, 16), tn=_pick(d, 1024, LANES), tk=d,
                   out_dtype=BF16, name="out_dmix", vmem_mb=48, after=after)
    d3, dscw = _convmix_bwd(proj, dmix, scw8, s, d)
    dq, dk, dv, dsink = _attn_bwd(proj, dmix, sinks, cos, sin, s)
    dproj = jnp.concatenate([dq, dk, dv, d3[0], d3[1], d3[2]], axis=1)
    g_win_t = _matmul(dproj, xb, mode="tn", m=n_in, n=d, k=s, tm=_pick(n_in, 2176, LANES), tn=_pick(d, 512, LANES),
                      tk=ts, out_dtype=BF16, name="grad_w_in", vmem_mb=48)
    small = dict(loss_sq=st2[2, 0], ln2_g=st2[0], ln2_b=st2[1], ln1_g=st1[0], ln1_b=st1[1], sinks=dsink[0, :N_Q_HEADS],
                 fcw=jnp.concatenate([dfcw[0, :3], dfcw[1, :3]], axis=1), scw=dscw[:3])
    return (dproj, dz1b), dict(win_t=g_win_t, wout=g_wout), small


def _grad_x(dproj, dz1b, win_t, after=None):
    s, n_in = dproj.shape
    d = win_t.shape[1]
    return _matmul(dproj, win_t, mode="nn", m=s, n=d, k=n_in, tm=_pick(s, 512, 16), tn=_pick(d, 1024, LANES),
                   tk=n_in, out_dtype=F32, name="grad_x", vmem_mb=56, res=dz1b, alpha=ALPHA, after=after)


def _local_step(x, target, win_t, wout, wup, wdown, scw8, fcw8, sinks, ln1_g, ln1_b, ln2_g, ln2_b):
    a = _phase_mixer(x, win_t, wout, scw8, sinks, ln1_g, ln1_b)
    f = _phase_ffn(a, target, wup, wdown, fcw8, ln2_g, ln2_b)
    (dproj, dz1b), g, small = _phase_rest(a, f, wup, wout, win_t, scw8, sinks, ln1_g)
    return _grad_x(dproj, dz1b, win_t), dict(g, wup=f["wup"], wdown=f["wdown"]), small


MIXER = ("win_t", "wout")
FFN = ("wup", "wdown")
BIG = MIXER + FFN


def _geom(shard_shapes):
    out = {}
    for name in BIG:
        r, c = shard_shapes[name]
        out[name] = ("col" if name == "wup" else "row", (r, c), (r // 2, c))
    return out


def _full_shape(kind, shard):
    r, c = shard
    return (N_CHIPS * r, c) if kind == "row" else (r, N_CHIPS * c)


def _piece_of(ref, kind, shard, chip, half):
    r, c = shard
    if kind == "row":
        return ref.at[pl.ds(chip * r + half * (r // 2), r // 2), :]
    return ref.at[pl.ds(half * (r // 2), r // 2), pl.ds(chip * c, c)]


def _shard_piece(ref, shard, half):
    r, _ = shard
    return ref.at[pl.ds(half * (r // 2), r // 2), :]


def _me():
    return lax.axis_index("x"), lax.axis_index("y"), lax.axis_index("c")


def _other_chips(x, y):
    return [(1 - x, y), (x, 1 - y), (1 - x, 1 - y)]


def _remote(src, dst, send_sem, recv_sem, dev):
    return pltpu.make_async_remote_copy(src_ref=src, dst_ref=dst, send_sem=send_sem, recv_sem=recv_sem,
                                        device_id=dev, device_id_type=MESH)


def _place_shard(w, chip1, kind, name):
    r, c = w.shape
    tr = _rows_tile(r, c, 16)
    nt = r // tr

    def body(chip_ref, w_ref, o_ref):
        o_ref[...] = w_ref[...].astype(BF16)

    out_map = (lambda i, chip_ref: (chip_ref[0] * nt + i, 0)) if kind == "row" else (lambda i, chip_ref: (i, chip_ref[0]))
    return pl.pallas_call(
        body, name="place_" + name,
        grid_spec=pltpu.PrefetchScalarGridSpec(
            num_scalar_prefetch=1, grid=(nt,),
            in_specs=[pl.BlockSpec((tr, c), lambda i, chip_ref: (i, 0))],
            out_specs=pl.BlockSpec((tr, c), out_map)),
        out_shape=jax.ShapeDtypeStruct(_full_shape(kind, (r, c)), BF16),
        compiler_params=_cp(("arbitrary",), 32),
    )(chip1, w)


def _allgather_weights(names, placed, geom, small_shards):
    nb, ns = len(names), len(small_shards)
    small_w = [a.shape[1] for a in small_shards]

    def body(*refs):
        sm = refs[nb:nb + ns]
        full = refs[nb + ns:2 * nb + ns]
        smf = refs[2 * nb + ns:2 * nb + 2 * ns]
        send, recv, loc = refs[2 * nb + 2 * ns:]
        x, y, c = _me()
        chip = 2 * x + y
        sib = (x, y, 1 - c)
        others = _other_chips(x, y)
        locals_, sends = [], []
        for m, name in enumerate(names):
            kind, shard, _ = geom[name]
            mine = _piece_of(full[m], kind, shard, chip, c)
            for k, (qx, qy) in enumerate(others):
                cp = _remote(mine, mine, send.at[6 * m + k], recv.at[6 * m + k], (qx, qy, c))
                cp.start()
                sends.append(cp)
        for t in range(ns):
            cp = pltpu.make_async_copy(sm[t], smf[t].at[:, pl.ds(chip * small_w[t], small_w[t])], loc.at[t])
            cp.start()
            locals_.append(cp)
            for k, (qx, qy) in enumerate(others):
                cp = _remote(sm[t], smf[t].at[:, pl.ds(chip * small_w[t], small_w[t])],
                             send.at[6 * nb + 3 * t + k], recv.at[6 * nb + 3 * t + k], (qx, qy, c))
                cp.start()
                sends.append(cp)
        for m, name in enumerate(names):
            kind, shard, _ = geom[name]
            for k, (qx, qy) in enumerate(others):
                got = _piece_of(full[m], kind, shard, 2 * qx + qy, c)
                _remote(got, got, send.at[6 * m + k], recv.at[6 * m + k], (qx, qy, c)).wait_recv()
                cp = _remote(got, got, send.at[6 * m + 3 + k], recv.at[6 * m + 3 + k], sib)
                cp.start()
                sends.append(cp)
        for t in range(ns):
            for k, (qx, qy) in enumerate(others):
                got = smf[t].at[:, pl.ds((2 * qx + qy) * small_w[t], small_w[t])]
                _remote(got, got, send.at[6 * nb + 3 * t + k], recv.at[6 * nb + 3 * t + k], (qx, qy, c)).wait_recv()
        for m, name in enumerate(names):
            kind, shard, _ = geom[name]
            for k, (qx, qy) in enumerate(others):
                got = _piece_of(full[m], kind, shard, 2 * qx + qy, 1 - c)
                _remote(got, got, send.at[6 * m + 3 + k], recv.at[6 * m + 3 + k], sib).wait_recv()
        for cp in sends:
            cp.wait_send()
        for cp in locals_:
            cp.wait()

    nsem = 6 * nb + 3 * ns
    out_shape = [jax.ShapeDtypeStruct(placed[n].shape, BF16) for n in names]
    out_shape += [jax.ShapeDtypeStruct((8, N_CHIPS * w), F32) for w in small_w]
    outs = pl.pallas_call(
        body, name="allgather_weights", in_specs=[ANY] * (nb + ns), out_specs=[ANY] * (nb + ns), out_shape=out_shape,
        input_output_aliases={m: m for m in range(nb)},
        scratch_shapes=[pltpu.SemaphoreType.DMA((nsem,)), pltpu.SemaphoreType.DMA((nsem,)),
                        pltpu.SemaphoreType.DMA((ns,))],
    )(*[placed[n] for n in names], *small_shards)
    return dict(zip(names, outs[:nb])), list(outs[nb:])


def _sibling_exchange(names, grads, geom):
    nb = len(names)

    def body(*refs):
        g = refs[:nb]
        got = refs[nb:2 * nb]
        send, recv = refs[2 * nb:]
        x, y, c = _me()
        sib = (x, y, 1 - c)
        cps = []
        for m, name in enumerate(names):
            kind, shard, _ = geom[name]
            for r in range(N_CHIPS):
                cp = _remote(_piece_of(g[m], kind, shard, r, 1 - c), got[m].at[r],
                             send.at[N_CHIPS * m + r], recv.at[N_CHIPS * m + r], sib)
                cp.start()
                cps.append(cp)
        for cp in cps:
            cp.wait_recv()
        for cp in cps:
            cp.wait_send()

    return pl.pallas_call(
        body, name="grad_sibling_exchange_" + names[0], in_specs=[ANY] * nb, out_specs=[ANY] * nb,
        out_shape=[jax.ShapeDtypeStruct((N_CHIPS,) + geom[n][2], BF16) for n in names],
        scratch_shapes=[pltpu.SemaphoreType.DMA((N_CHIPS * nb,)), pltpu.SemaphoreType.DMA((N_CHIPS * nb,))],
    )(*[grads[n] for n in names])


def _sibling_assemble(names, shards, geom):
    nb = len(names)

    def body(*refs):
        full = refs[nb:2 * nb]
        send, recv = refs[2 * nb:]
        x, y, c = _me()
        sib = (x, y, 1 - c)
        cps = []
        for m, name in enumerate(names):
            mine = _shard_piece(full[m], geom[name][1], c)
            cp = _remote(mine, mine, send.at[m], recv.at[m], sib)
            cp.start()
            cps.append(cp)
        for m, name in enumerate(names):
            theirs = _shard_piece(full[m], geom[name][1], 1 - c)
            _remote(theirs, theirs, send.at[m], recv.at[m], sib).wait_recv()
        for cp in cps:
            cp.wait_send()

    return pl.pallas_call(
        body, name="grad_sibling_assemble_" + names[0], in_specs=[ANY] * nb, out_specs=[ANY] * nb,
        out_shape=[jax.ShapeDtypeStruct(geom[n][1], F32) for n in names],
        input_output_aliases={m: m for m in range(nb)},
        scratch_shapes=[pltpu.SemaphoreType.DMA((nb,)), pltpu.SemaphoreType.DMA((nb,))],
    )(*shards)


HBM = pl.BlockSpec(memory_space=pltpu.HBM)
SEM = pl.BlockSpec(memory_space=pltpu.SEMAPHORE)
EFFECT = pltpu.SideEffectType.DATAFLOW_SIDE_EFFECTING
TOKEN = jax.ShapeDtypeStruct((8, LANES), F32)


def _hbm(a):
    return pltpu.with_memory_space_constraint(a, pltpu.HBM)


def _gather_copies(names, full, geom, send, recv):
    x, y, c = _me()
    out = []
    for m, name in enumerate(names):
        kind, shard, _ = geom[name]
        mine = _piece_of(full[m], kind, shard, 2 * x + y, c)
        for k, (qx, qy) in enumerate(_other_chips(x, y)):
            theirs = _piece_of(full[m], kind, shard, 2 * qx + qy, c)
            out.append((_remote(mine, mine, send.at[3 * m + k], recv.at[3 * m + k], (qx, qy, c)),
                        _remote(theirs, theirs, send.at[3 * m + k], recv.at[3 * m + k], (qx, qy, c))))
    return out


def _gather_start(names, placed, geom, after):
    nb = len(names)

    def body(*refs):
        full = refs[:nb]
        send, recv = refs[nb + 1], refs[nb + 2]
        token = refs[2 * nb + 3]
        for cp, _ in _gather_copies(names, full, geom, send, recv):
            cp.start()
        token[...] = jnp.zeros_like(token)

    outs = pl.pallas_call(
        body, name="gather_start_" + names[0],
        out_shape=(pltpu.SemaphoreType.DMA((3 * nb,)), pltpu.SemaphoreType.DMA((3 * nb,)),
                   *[pltpu.HBM(placed[n].shape, BF16) for n in names], TOKEN),
        in_specs=[HBM] * nb + [ANY], out_specs=(SEM, SEM, *[HBM] * nb, pl.BlockSpec(memory_space=pltpu.VMEM)),
        input_output_aliases={m: 2 + m for m in range(nb)},
        compiler_params=pltpu.CompilerParams(has_side_effects=EFFECT),
    )(*[_hbm(placed[n]) for n in names], after)
    return outs[0], outs[1], list(outs[2:2 + nb]), outs[2 + nb]


def _gather_wait(names, send, recv, thru, geom, after):
    nb = len(names)

    def body(*refs):
        full = refs[:nb]
        for mine, theirs in _gather_copies(names, full, geom, refs[nb], refs[nb + 1]):
            mine.wait_send()
            theirs.wait_recv()

    return pl.pallas_call(
        body, name="gather_wait_" + names[0], out_shape=tuple(pltpu.HBM(t.shape, t.dtype) for t in thru),
        in_specs=[HBM] * nb + [SEM, SEM, ANY], out_specs=tuple([HBM] * nb),
        input_output_aliases={m: m for m in range(nb)},
        compiler_params=pltpu.CompilerParams(has_side_effects=EFFECT),
    )(*thru, send, recv, after)


def _gather_forward(names, full, geom):
    nb = len(names)

    def body(*refs):
        arr = refs[nb:2 * nb]
        send, recv = refs[2 * nb:]
        x, y, c = _me()
        sib = (x, y, 1 - c)
        cps = []
        for m, name in enumerate(names):
            kind, shard, _ = geom[name]
            for k, (qx, qy) in enumerate(_other_chips(x, y)):
                got = _piece_of(arr[m], kind, shard, 2 * qx + qy, c)
                cp = _remote(got, got, send.at[3 * m + k], recv.at[3 * m + k], sib)
                cp.start()
                cps.append(cp)
        for m, name in enumerate(names):
            kind, shard, _ = geom[name]
            for k, (qx, qy) in enumerate(_other_chips(x, y)):
                theirs = _piece_of(arr[m], kind, shard, 2 * qx + qy, 1 - c)
                _remote(theirs, theirs, send.at[3 * m + k], recv.at[3 * m + k], sib).wait_recv()
        for cp in cps:
            cp.wait_send()

    return pl.pallas_call(
        body, name="gather_forward_" + names[0], in_specs=[ANY] * nb, out_specs=[ANY] * nb,
        out_shape=[jax.ShapeDtypeStruct(a.shape, a.dtype) for a in full],
        input_output_aliases={m: m for m in range(nb)},
        scratch_shapes=[pltpu.SemaphoreType.DMA((3 * nb,)), pltpu.SemaphoreType.DMA((3 * nb,))],
    )(*full)


def _scatter_copies(nb, t, got, send, recv):
    x, y, c = _me()
    return [_remote(t[m].at[2 * qx + qy], got[m].at[k], send.at[3 * m + k], recv.at[3 * m + k], (qx, qy, c))
            for m in range(nb) for k, (qx, qy) in enumerate(_other_chips(x, y))]


def _chip_exchange_start(names, chip_sums, geom, after):
    nb = len(names)
    lands = [lax.empty((N_CHIPS - 1,) + geom[n][2], BF16) for n in names]

    def body(*refs):
        t, got = refs[:nb], refs[nb:2 * nb]
        send, recv = refs[2 * nb + 1], refs[2 * nb + 2]
        token = refs[4 * nb + 3]
        for cp in _scatter_copies(nb, t, got, send, recv):
            cp.start()
        token[...] = jnp.zeros_like(token)

    both = list(chip_sums) + lands
    outs = pl.pallas_call(
        body, name="grad_chip_start_" + names[0],
        out_shape=(pltpu.SemaphoreType.DMA((3 * nb,)), pltpu.SemaphoreType.DMA((3 * nb,)),
                   *[pltpu.HBM(a.shape, a.dtype) for a in both], TOKEN),
        in_specs=[HBM] * (2 * nb) + [ANY],
        out_specs=(SEM, SEM, *[HBM] * (2 * nb), pl.BlockSpec(memory_space=pltpu.VMEM)),
        input_output_aliases={m: 2 + m for m in range(2 * nb)},
        compiler_params=pltpu.CompilerParams(has_side_effects=EFFECT),
    )(*[_hbm(a) for a in both], after)
    return outs[0], outs[1], list(outs[2:2 + 2 * nb]), outs[2 + 2 * nb]


def _chip_exchange_wait(names, send, recv, thru, after):
    nb = len(names)

    def body(*refs):
        for cp in _scatter_copies(nb, refs[:nb], refs[nb:2 * nb], refs[2 * nb], refs[2 * nb + 1]):
            cp.wait_send()
            cp.wait_recv()

    outs = pl.pallas_call(
        body, name="grad_chip_wait_" + names[0], out_shape=tuple(pltpu.HBM(t.shape, t.dtype) for t in thru),
        in_specs=[HBM] * (2 * nb) + [SEM, SEM, ANY], out_specs=tuple([HBM] * (2 * nb)),
        input_output_aliases={m: m for m in range(2 * nb)},
        compiler_params=pltpu.CompilerParams(has_side_effects=EFFECT),
    )(*thru, send, recv, after)
    return list(outs[:nb]), list(outs[nb:])


def _sibling_copies(names, g, got, geom, send, recv):
    x, y, c = _me()
    out = []
    for m, name in enumerate(names):
        kind, shard, _ = geom[name]
        for r in range(N_CHIPS):
            out.append(_remote(_piece_of(g[m], kind, shard, r, 1 - c), got[m].at[r],
                               send.at[N_CHIPS * m + r], recv.at[N_CHIPS * m + r], (x, y, 1 - c)))
    return out


def _sibling_exchange_start(names, grads, geom, after):
    nb = len(names)
    lands = [lax.empty((N_CHIPS,) + geom[n][2], BF16) for n in names]

    def body(*refs):
        for cp in _sibling_copies(names, refs[:nb], refs[nb:2 * nb], geom, refs[2 * nb + 1], refs[2 * nb + 2]):
            cp.start()
        token = refs[4 * nb + 3]
        token[...] = jnp.zeros_like(token)

    both = [grads[n] for n in names] + lands
    outs = pl.pallas_call(
        body, name="grad_sibling_start_" + names[0],
        out_shape=(pltpu.SemaphoreType.DMA((N_CHIPS * nb,)), pltpu.SemaphoreType.DMA((N_CHIPS * nb,)),
                   *[pltpu.HBM(a.shape, a.dtype) for a in both], TOKEN),
        in_specs=[HBM] * (2 * nb) + [ANY],
        out_specs=(SEM, SEM, *[HBM] * (2 * nb), pl.BlockSpec(memory_space=pltpu.VMEM)),
        input_output_aliases={m: 2 + m for m in range(2 * nb)},
        compiler_params=pltpu.CompilerParams(has_side_effects=EFFECT),
    )(*[_hbm(a) for a in both], after)
    return outs[0], outs[1], list(outs[2:2 + 2 * nb]), outs[2 + 2 * nb]


def _sibling_exchange_wait(names, send, recv, thru, geom, after):
    nb = len(names)

    def body(*refs):
        for cp in _sibling_copies(names, refs[:nb], refs[nb:2 * nb], geom, refs[2 * nb], refs[2 * nb + 1]):
            cp.wait_send()
            cp.wait_recv()

    outs = pl.pallas_call(
        body, name="grad_sibling_wait_" + names[0], out_shape=tuple(pltpu.HBM(t.shape, t.dtype) for t in thru),
        in_specs=[HBM] * (2 * nb) + [SEM, SEM, ANY], out_specs=tuple([HBM] * (2 * nb)),
        input_output_aliases={m: m for m in range(2 * nb)},
        compiler_params=pltpu.CompilerParams(has_side_effects=EFFECT),
    )(*thru, send, recv, after)
    return list(outs[:nb]), list(outs[nb:])


def _allreduce_small(part):
    rows = part.shape[0]
    flips = [(a, b, e) for a in (0, 1) for b in (0, 1) for e in (0, 1) if (a, b, e) != (0, 0, 0)]

    def body(p_ref, o_ref, all_ref, send, recv):
        x, y, c = _me()
        me = 4 * x + 2 * y + c
        all_ref[me] = p_ref[...]
        cps = []
        for k, (a, b, e) in enumerate(flips):
            cp = _remote(p_ref, all_ref.at[me], send.at[k], recv.at[k], (x ^ a, y ^ b, c ^ e))
            cp.start()
            cps.append(cp)
        for k, (a, b, e) in enumerate(flips):
            peer = 4 * (x ^ a) + 2 * (y ^ b) + (c ^ e)
            _remote(p_ref, all_ref.at[peer], send.at[k], recv.at[k], (x ^ a, y ^ b, c ^ e)).wait_recv()
        for cp in cps:
            cp.wait_send()
        tot = all_ref[0]
        for dev in range(1, 8):
            tot = tot + all_ref[dev]
        o_ref[...] = tot

    vm = pl.BlockSpec(memory_space=pltpu.VMEM)
    return pl.pallas_call(
        body, name="allreduce_small", in_specs=[vm], out_specs=vm, out_shape=jax.ShapeDtypeStruct((rows, LANES), F32),
        scratch_shapes=[pltpu.VMEM((8, rows, LANES), F32), pltpu.SemaphoreType.DMA((7,)), pltpu.SemaphoreType.DMA((7,))],
    )(part)


def _rows_tile(rows, cols, mult, elems=1 << 19):
    return _pick(rows, max(mult, elems // cols // mult * mult), mult)


ADD_TILE = 1 << 20


def _add_pairs(g, got, kind, shard, where, name):
    p, r, c = got.shape
    tr = _rows_tile(r, c, 16, ADD_TILE)
    nt = r // tr

    def body(w_ref, a_ref, b_ref, o_ref):
        o_ref[...] = (a_ref[...].astype(F32) + b_ref[...].astype(F32)).astype(BF16)

    if kind == "row":
        g_map = lambda q, i, w_ref: ((2 * q + w_ref[1]) * nt + i, 0)
    else:
        g_map = lambda q, i, w_ref: (w_ref[1] * nt + i, q)
    spec = pl.BlockSpec((None, tr, c), lambda q, i, w_ref: (q, i, 0))
    return pl.pallas_call(
        body, name="grad_add_sibling_" + name,
        grid_spec=pltpu.PrefetchScalarGridSpec(
            num_scalar_prefetch=1, grid=(p, nt), in_specs=[pl.BlockSpec((tr, c), g_map), spec], out_specs=spec),
        out_shape=jax.ShapeDtypeStruct((p, r, c), BF16), compiler_params=_cp(("arbitrary", "arbitrary"), 32),
    )(where, g, got)


def _add_four(t, got, shard, where, name):
    _, r, c = t.shape
    tr = _rows_tile(r, c, 16, ADD_TILE)
    nt = r // tr

    def body(w_ref, own, t0, t1, t2, o_ref):
        o_ref[...] = ((own[...].astype(F32) + t0[...].astype(F32)) + t1[...].astype(F32)) + t2[...].astype(F32)

    spec = lambda q: pl.BlockSpec((None, tr, c), lambda i, w_ref: (q, i, 0))
    return pl.pallas_call(
        body, name="grad_add_chips_" + name,
        grid_spec=pltpu.PrefetchScalarGridSpec(
            num_scalar_prefetch=1, grid=(nt,),
            in_specs=[pl.BlockSpec((None, tr, c), lambda i, w_ref: (w_ref[0], i, 0)), spec(0), spec(1), spec(2)],
            out_specs=pl.BlockSpec((tr, c), lambda i, w_ref: (w_ref[1] * nt + i, 0))),
        out_shape=jax.ShapeDtypeStruct(shard, F32), compiler_params=_cp(("arbitrary",), 48),
    )(where, t, got, got, got)


def _adamw(w, g, m, v, name):
    r, c = w.shape
    tr = _rows_tile(r, c, 8)

    def body(w_ref, g_ref, m_ref, v_ref, go_ref, d_ref, mo_ref, vo_ref):
        gv = g_ref[...]
        mn = ADAM_B1 * m_ref[...] + (1.0 - ADAM_B1) * gv
        vn = ADAM_B2 * v_ref[...] + (1.0 - ADAM_B2) * (gv * gv)
        m_hat = mn / (1.0 - ADAM_B1 ** ADAM_STEP)
        v_hat = vn / (1.0 - ADAM_B2 ** ADAM_STEP)
        go_ref[...] = gv
        d_ref[...] = -ADAM_LR * (m_hat / (jnp.sqrt(v_hat) + ADAM_EPS) + ADAM_WD * w_ref[...])
        mo_ref[...] = mn
        vo_ref[...] = vn

    spec = pl.BlockSpec((tr, c), lambda i: (i, 0))
    return pl.pallas_call(
        body, name=name, grid=(r // tr,), in_specs=[spec] * 4, out_specs=[spec] * 4,
        out_shape=[jax.ShapeDtypeStruct((r, c), F32)] * 4, compiler_params=_cp(("arbitrary",), 32),
    )(w, g, m, v)


def _pack(vectors, rows):
    flat = jnp.concatenate([v.reshape(-1).astype(F32) for v in vectors])
    return jnp.pad(flat, (0, rows * LANES - flat.shape[0])).reshape(rows, LANES)


def _unpack(packed, shapes):
    flat = packed.reshape(-1)
    out, off = [], 0
    for shp in shapes:
        n = 1
        for t in shp:
            n *= t
        out.append(flat[off:off + n].reshape(shp))
        off += n
    return out


def _rows_for(shapes):
    n = sum(functools.reduce(lambda a, b: a * b, shp, 1) for shp in shapes)
    return -(-n // (8 * LANES)) * 8


def kernel(x, w_in, attn_sinks, short_conv_w, w_out, ln1_g, ln1_b, ffn_w_up, ffn_conv_w, ffn_w_down, ln2_g, ln2_b, loss_target, m_w_in, m_attn_sinks, m_short_conv_w, m_w_out, m_ln1_g, m_ln1_b, m_ffn_w_up, m_ffn_conv_w, m_ffn_w_down, m_ln2_g, m_ln2_b, v_w_in, v_attn_sinks, v_short_conv_w, v_w_out, v_ln1_g, v_ln1_b, v_ffn_w_up, v_ffn_conv_w, v_ffn_w_down, v_ln2_g, v_ln2_b):
    xs, tgt = x[0], loss_target[0]
    s, d = xs.shape
    chip = 2 * lax.axis_index("x") + lax.axis_index("y")

    w_big = dict(win_t=w_in[0], wout=w_out[0], wup=ffn_w_up[0], wdown=ffn_w_down[0])
    m_big = dict(win_t=m_w_in[0], wout=m_w_out[0], wup=m_ffn_w_up[0], wdown=m_ffn_w_down[0])
    v_big = dict(win_t=v_w_in[0], wout=v_w_out[0], wup=v_ffn_w_up[0], wdown=v_ffn_w_down[0])
    to_place = dict(w_big, win_t=w_in[0].T)
    geom = _geom({n: to_place[n].shape for n in BIG})
    pad8 = lambda a: jnp.pad(a[0], ((0, 5), (0, 0)))
    where = jnp.stack([chip, lax.axis_index("c")]).astype(jnp.int32)
    placed = {n: _place_shard(to_place[n], where[:1], geom[n][0], n) for n in BIG}
    full, (scw8, fcw8) = _allgather_weights(MIXER, placed, geom, [pad8(short_conv_w), pad8(ffn_conv_w)])
    send, recv, thru, token = _gather_start(FFN, placed, geom, scw8)
    a = _phase_mixer(xs, full["win_t"], full["wout"], scw8, attn_sinks, ln1_g, ln1_b, after=token)
    landed = _gather_forward(FFN, _gather_wait(FFN, send, recv, thru, geom, a["x1b"]), geom)
    full.update(zip(FFN, landed))
    f = _phase_ffn(a, tgt, full["wup"], full["wdown"], fcw8, ln2_g, ln2_b)

    def add_pairs(names, grads, from_sibling):
        return [_add_pairs(grads[m], from_sibling[m], geom[n][0], geom[n][1], where, n) for m, n in enumerate(names)]

    sib_send, sib_recv, sib_thru, sib_token = _sibling_exchange_start(FFN, f, geom, f["st2"])
    started = {}

    def between(dz1b):
        grads, from_sibling = _sibling_exchange_wait(FFN, sib_send, sib_recv, sib_thru, geom, dz1b)
        started["sums"] = add_pairs(FFN, grads, from_sibling)
        started["chip"] = _chip_exchange_start(FFN, started["sums"], geom, f["st2"])
        return started["chip"][3]

    (dproj, dz1b), g_mixer, g_small = _phase_rest(a, f, full["wup"], full["wout"], full["win_t"], scw8, attn_sinks,
                                                  ln1_g + sib_token[0:1, 0:1], between=between)
    send, recv, thru, _ = started["chip"]

    def finish(names, exchanged):
        sums, from_chips = exchanged
        halves = [_add_four(sums[m], from_chips[m], geom[n][1], where, n) for m, n in enumerate(names)]
        shards = _sibling_assemble(names, halves, geom)
        grads = {n: shards[m].T if n == "win_t" else shards[m] for m, n in enumerate(names)}
        return {n: _adamw(w_big[n], grads[n], m_big[n], v_big[n], "adamw_" + n) for n in names}

    mixer_sums = add_pairs(MIXER, [g_mixer[n] for n in MIXER], _sibling_exchange(MIXER, g_mixer, geom))
    send2, recv2, thru2, token2 = _chip_exchange_start(MIXER, mixer_sums, geom, f["st2"])
    grad_x = _grad_x(dproj, dz1b, full["win_t"], after=token2)
    upd = finish(FFN, _chip_exchange_wait(FFN, send, recv, thru, grad_x))
    upd.update(finish(MIXER, _chip_exchange_wait(MIXER, send2, recv2, thru2, upd[FFN[0]][1])))

    small_names = ("ln1_g", "ln1_b", "ln2_g", "ln2_b", "sinks", "fcw", "scw")
    small_shapes = [g_small[n].shape for n in small_names]
    red = _allreduce_small(_pack([g_small["loss_sq"].reshape(1)] + [g_small[n] for n in small_names],
                                 _rows_for([(1,)] + small_shapes)))
    loss_sq, *gs = _unpack(red, [(1,)] + small_shapes)
    gs = dict(zip(small_names, gs))
    loss = (0.5 / d) * loss_sq[0]
    fw, sw = ffn_conv_w.shape[2], short_conv_w.shape[2]
    gs["fcw"] = lax.dynamic_slice_in_dim(gs["fcw"], chip * fw, fw, axis=1)
    gs["scw"] = lax.dynamic_slice_in_dim(gs["scw"], chip * sw, sw, axis=1)

    sm_w = dict(ln1_g=ln1_g[0], ln1_b=ln1_b[0], ln2_g=ln2_g[0], ln2_b=ln2_b[0], sinks=attn_sinks[0],
                fcw=ffn_conv_w[0], scw=short_conv_w[0])
    sm_m = dict(ln1_g=m_ln1_g[0], ln1_b=m_ln1_b[0], ln2_g=m_ln2_g[0], ln2_b=m_ln2_b[0], sinks=m_attn_sinks[0],
                fcw=m_ffn_conv_w[0], scw=m_short_conv_w[0])
    sm_v = dict(ln1_g=v_ln1_g[0], ln1_b=v_ln1_b[0], ln2_g=v_ln2_g[0], ln2_b=v_ln2_b[0], sinks=v_attn_sinks[0],
                fcw=v_ffn_conv_w[0], scw=v_short_conv_w[0])
    shapes = [sm_w[n].shape for n in small_names]
    rows = _rows_for(shapes)
    packed = [_pack([t[n] for n in small_names], rows) for t in (sm_w, gs, sm_m, sm_v)]
    sm_out = [dict(zip(small_names, _unpack(a, shapes))) for a in _adamw(*packed, "adamw_small")]

    def leaf(kind, name):
        if name in ("w_in", "w_out", "ffn_w_up", "ffn_w_down"):
            key = dict(w_in="win_t", w_out="wout", ffn_w_up="wup", ffn_w_down="wdown")[name]
            return upd[key][kind][None]
        key = dict(attn_sinks="sinks", short_conv_w="scw", ffn_conv_w="fcw").get(name, name)
        return sm_out[kind][key][None]

    order = ("w_in", "attn_sinks", "short_conv_w", "w_out", "ln1_g", "ln1_b", "ffn_w_up", "ffn_conv_w", "ffn_w_down",
             "ln2_g", "ln2_b")
    outs = [loss, grad_x[None]]
    for kind in range(4):
        outs += [leaf(kind, n) for n in order]
    return tuple(outs)
```

```python
import functools

import jax
import jax.numpy as jnp
from jax import lax
from jax.experimental import pallas as pl
from jax.experimental.pallas import tpu as pltpu

F32 = jnp.float32
BF16 = jnp.bfloat16
MESH = pl.DeviceIdType.MESH
ANY = pl.BlockSpec(memory_space=pl.ANY)

HEAD_DIM = 64
N_Q_HEADS = 16
N_KV_HEADS = 2
ATTN_WIDTH = N_Q_HEADS * HEAD_DIM
KV_WIDTH = N_KV_HEADS * HEAD_DIM
BLOCK = 128
ROPE_THETA = 10000.0
LN_EPS = 1e-5
ALPHA = 2.0 ** 0.25
NEG_INF = -1e30
ADAM_LR, ADAM_B1, ADAM_B2, ADAM_EPS, ADAM_WD, ADAM_STEP = 0.001, 0.9, 0.999, 1e-08, 0.01, 10
N_CHIPS = 4
LANES = 128
MXU_DIM = 256
SLAB = 128


def _cp(sem, vmem_mb):
    return pltpu.CompilerParams(dimension_semantics=sem, vmem_limit_bytes=vmem_mb << 20)


def _matmul(a, b, *, mode, m, n, k, tm, tn, tk, out_dtype, name, vmem_mb, a_spec=None, b_spec=None,
            res=None, alpha=1.0, after=None, m_outer=False):
    nj, ni, nk = n // tn, m // tm, k // tk
    assert nj * tn == n and ni * tm == m and nk * tk == k, (name, m, n, k, tm, tn, tk)
    if mode == "nn":
        dims = ((1,), (0,))
        a_spec = a_spec or pl.BlockSpec((tm, tk), lambda j, i, kk: (i, kk))
        b_spec = b_spec or pl.BlockSpec((tk, tn), lambda j, i, kk: (kk, j))
    elif mode == "nt":
        dims = ((1,), (1,))
        a_spec = a_spec or pl.BlockSpec((tm, tk), lambda j, i, kk: (i, kk))
        b_spec = b_spec or pl.BlockSpec((tn, tk), lambda j, i, kk: (j, kk))
    else:
        dims = ((0,), (0,))
        a_spec = a_spec or pl.BlockSpec((tk, tm), lambda j, i, kk: (kk, i))
        b_spec = b_spec or pl.BlockSpec((tk, tn), lambda j, i, kk: (kk, j))
    has_res = res is not None
    has_after = after is not None

    def body(*refs):
        refs = refs[1:] if has_after else refs
        a_ref, b_ref = refs[0], refs[1]
        res_ref = refs[2] if has_res else None
        o_ref = refs[2 + has_res]
        part = lax.dot_general(a_ref[...].astype(BF16), b_ref[...].astype(BF16), (dims, ((), ())),
                               preferred_element_type=F32)

        def finish(acc):
            if has_res:
                acc = acc + alpha * res_ref[...].astype(F32)
            o_ref[...] = acc.astype(o_ref.dtype)

        if nk == 1:
            finish(part)
        else:
            acc_ref = refs[3 + has_res]
            kk = pl.program_id(2)

            @pl.when(kk == 0)
            def _():
                acc_ref[...] = part

            @pl.when(kk > 0)
            def _():
                acc_ref[...] += part

            @pl.when(kk == nk - 1)
            def _():
                finish(acc_ref[...])

    in_specs = [a_spec, b_spec]
    args = [a, b]
    if has_res:
        in_specs.append(pl.BlockSpec((tm, tn), lambda j, i, kk: (i, j)))
        args.append(res)
    if has_after:
        in_specs.insert(0, pl.BlockSpec(after.shape, lambda j, i, kk: (0, 0)))
        args.insert(0, after)
    out_spec = pl.BlockSpec((tm, tn), lambda j, i, kk: (i, j))
    grid = (nj, ni, nk)
    if m_outer:
        swap = lambda sp: pl.BlockSpec(sp.block_shape, (lambda f: lambda i, j, kk: f(j, i, kk))(sp.index_map),
                                       pipeline_mode=sp.pipeline_mode)
        in_specs, out_spec, grid = [swap(sp) for sp in in_specs], swap(out_spec), (ni, nj, nk)
    return pl.pallas_call(
        body, name=name, grid=grid, in_specs=in_specs,
        out_specs=out_spec,
        out_shape=jax.ShapeDtypeStruct((m, n), out_dtype),
        scratch_shapes=[pltpu.VMEM((tm, tn), F32)] if nk > 1 else [],
        compiler_params=_cp(("arbitrary", "arbitrary", "arbitrary"), vmem_mb),
    )(*args)


def _pick(total, want, mult):
    if total <= want:
        return total
    for t in range(want, 0, -1):
        if total % t == 0 and t % mult == 0:
            return t
    return total


def _rope_tables(s):
    half = HEAD_DIM // 2
    inv_freq = ROPE_THETA ** (-jnp.arange(half, dtype=F32) / half)
    ang = jnp.arange(s, dtype=F32)[:, None] * inv_freq[None, :]
    cos = jnp.tile(jnp.cos(ang), (1, LANES // half))
    sin = jnp.tile(jnp.concatenate([-jnp.sin(ang), jnp.sin(ang)], axis=1), (1, LANES // HEAD_DIM))
    return cos, sin


def _rope(x, cos, sin, lo):
    partner = jnp.where(lo, pltpu.roll(x, LANES - HEAD_DIM // 2, 1), pltpu.roll(x, HEAD_DIM // 2, 1))
    return x * cos + partner * sin


def _dot(a, b, dims):
    return lax.dot_general(a, b, (dims, ((), ())), preferred_element_type=F32)


NN, NT, TN = ((1,), (0,)), ((1,), (1,)), ((0,), (0,))


def _kv_variants(t, head_lo):
    r = pltpu.roll(t, HEAD_DIM, 1)
    zero = jnp.zeros_like(t)
    a = (jnp.where(head_lo, t, zero).astype(BF16), jnp.where(head_lo, r, zero).astype(BF16))
    b = (jnp.where(head_lo, zero, r).astype(BF16), jnp.where(head_lo, zero, t).astype(BF16))
    return a, b


PAIRS_PER_KV = N_Q_HEADS // 2 // N_KV_HEADS
STACK = PAIRS_PER_KV * BLOCK


def _stack_pairs(ref, j, fn):
    return jnp.concatenate([fn(ref[:, p * LANES:(p + 1) * LANES])
                            for p in range(j * PAIRS_PER_KV, (j + 1) * PAIRS_PER_KV)], axis=0)


def _sink_row(sink_ref, j, hh):
    col = lax.broadcasted_iota(jnp.int32, (1, STACK), 1)
    heads = [2 * p + hh for p in range(j * PAIRS_PER_KV, (j + 1) * PAIRS_PER_KV)]
    row = jnp.full((1, STACK), sink_ref[0, heads[-1]], F32)
    for t in range(PAIRS_PER_KV - 2, -1, -1):
        row = jnp.where(col < (t + 1) * BLOCK, sink_ref[0, heads[t]], row)
    return row


def _attn_exps(qp, ka, kb, valid, sink_a, sink_b):
    out = []
    for kk, sink in ((ka, sink_a), (kb, sink_b)):
        s = jnp.where(valid, _dot(kk, qp, NT), NEG_INF)
        mx = jnp.maximum(jnp.max(s, axis=0, keepdims=True), sink)
        out.append((jnp.exp(s - mx), jnp.exp(sink - mx)))
    return out


def _attn_common(i, q_ref, k_ref, v_ref, kp_ref, vp_ref, cos_ref, sin_ref, cosp_ref, sinp_ref):
    lane = lax.broadcasted_iota(jnp.int32, (1, LANES), 1)
    lo = (lane % HEAD_DIM) < (HEAD_DIM // 2)
    head_lo = lane < HEAD_DIM
    cos, sin = cos_ref[...], sin_ref[...]
    kc = _rope(k_ref[...].astype(F32), cos, sin, lo)
    kp = _rope(kp_ref[...].astype(F32), cosp_ref[...], sinp_ref[...], lo)
    kext = jnp.concatenate([kp, kc], axis=0)
    vext = jnp.concatenate([vp_ref[...].astype(F32), v_ref[...].astype(F32)], axis=0)
    ka, kb = _kv_variants(kext, head_lo)
    va, vb = _kv_variants(vext, head_lo)
    qi = lax.broadcasted_iota(jnp.int32, (1, STACK), 1) % BLOCK
    kj = lax.broadcasted_iota(jnp.int32, (2 * BLOCK, 1), 0)
    valid = (kj > qi) & (kj <= qi + BLOCK) & ((kj >= BLOCK) | (i > 0))
    cos4 = jnp.concatenate([cos] * PAIRS_PER_KV, axis=0)
    sin4 = jnp.concatenate([sin] * PAIRS_PER_KV, axis=0)
    return lo, head_lo, cos, sin, cos4, sin4, ka, kb, va, vb, valid


def _attn_fwd(proj, sinks, cos, sin, s):
    nb = s // BLOCK
    kcol, vcol = ATTN_WIDTH // LANES, ATTN_WIDTH // LANES + 1

    def body(q_ref, k_ref, v_ref, kp_ref, vp_ref, cos_ref, sin_ref, cosp_ref, sinp_ref, sink_ref, o_ref):
        i = pl.program_id(0)
        lo, head_lo, cs, sn, cs4, sn4, ka, kb, va, vb, valid = _attn_common(
            i, q_ref, k_ref, v_ref, kp_ref, vp_ref, cos_ref, sin_ref, cosp_ref, sinp_ref)
        row = lax.broadcasted_iota(jnp.int32, (16, 1), 0)
        one = jnp.ones((), BF16)
        for j in range(N_KV_HEADS):
            q4 = _stack_pairs(q_ref, j, lambda t: t.astype(F32))
            qp = (_rope(q4, cs4, sn4, lo) * HEAD_DIM ** -0.5).astype(BF16)
            exps = _attn_exps(qp, ka[j], kb[j], valid, _sink_row(sink_ref, j, 0), _sink_row(sink_ref, j, 1))
            outs = []
            for (e, es), vv, mine in zip(exps, (va[j], vb[j]), (head_lo, ~head_lo)):
                ee = jnp.concatenate([e.astype(BF16), jnp.where(row == 0, es, 0.0).astype(BF16)], axis=0)
                tail = jnp.where((row == 0) & ~mine, one, jnp.zeros((), BF16))
                vx = jnp.concatenate([jnp.where(mine, vv, one), tail], axis=0)
                un = _dot(ee, vx, TN)
                outs.append(un / pltpu.roll(un, HEAD_DIM, 1))
            o = jnp.where(head_lo, outs[0], outs[1]).astype(BF16)
            for t in range(PAIRS_PER_KV):
                p = j * PAIRS_PER_KV + t
                o_ref[:, p * LANES:(p + 1) * LANES] = o[t * BLOCK:(t + 1) * BLOCK]

    prev = lambda i: (jnp.maximum(i - 1, 0), 0)
    return pl.pallas_call(
        body, name="attn_fwd", grid=(nb,),
        in_specs=[pl.BlockSpec((BLOCK, ATTN_WIDTH), lambda i: (i, 0)),
                  pl.BlockSpec((BLOCK, LANES), lambda i: (i, kcol)),
                  pl.BlockSpec((BLOCK, LANES), lambda i: (i, vcol)),
                  pl.BlockSpec((BLOCK, LANES), lambda i: (jnp.maximum(i - 1, 0), kcol)),
                  pl.BlockSpec((BLOCK, LANES), lambda i: (jnp.maximum(i - 1, 0), vcol)),
                  pl.BlockSpec((BLOCK, LANES), lambda i: (i, 0)),
                  pl.BlockSpec((BLOCK, LANES), lambda i: (i, 0)),
                  pl.BlockSpec((BLOCK, LANES), prev),
                  pl.BlockSpec((BLOCK, LANES), prev),
                  pl.BlockSpec(memory_space=pltpu.SMEM)],
        out_specs=pl.BlockSpec((BLOCK, ATTN_WIDTH), lambda i: (i, 0)),
        out_shape=jax.ShapeDtypeStruct((s, ATTN_WIDTH), BF16),
        compiler_params=_cp(("arbitrary",), 32),
    )(proj, proj, proj, proj, proj, cos, sin, cos, sin, sinks)


def _attn_bwd(proj, dmix, sinks, cos, sin, s):
    nb = s // BLOCK
    kcol, vcol = ATTN_WIDTH // LANES, ATTN_WIDTH // LANES + 1
    pairs_per_kv = N_Q_HEADS // 2 // N_KV_HEADS

    def body(q_ref, k_ref, v_ref, kp_ref, vp_ref, cos_ref, sin_ref, cosp_ref, sinp_ref, sink_ref, do_ref,
             dq_ref, dk_ref, dv_ref, dsink_ref, ck_ref, cv_ref):
        g = pl.program_id(0)
        i = nb - 1 - g

        @pl.when(g == 0)
        def _():
            ck_ref[...] = jnp.zeros_like(ck_ref)
            cv_ref[...] = jnp.zeros_like(cv_ref)
            dsink_ref[...] = jnp.zeros_like(dsink_ref)

        lo, head_lo, cs, sn, cs4, sn4, ka, kb, va, vb, valid = _attn_common(
            i, q_ref, k_ref, v_ref, kp_ref, vp_ref, cos_ref, sin_ref, cosp_ref, sinp_ref)
        lane = lax.broadcasted_iota(jnp.int32, (1, LANES), 1)
        dk_j, dv_j = [], []
        dsink = jnp.zeros((1, LANES), F32)
        for j in range(N_KV_HEADS):
            q4 = _stack_pairs(q_ref, j, lambda t: t.astype(F32))
            qp = (_rope(q4, cs4, sn4, lo) * HEAD_DIM ** -0.5).astype(BF16)
            exps = _attn_exps(qp, ka[j], kb[j], valid, _sink_row(sink_ref, j, 0), _sink_row(sink_ref, j, 1))
            do = _stack_pairs(do_ref, j, lambda t: t)
            dq_r = jnp.zeros((STACK, LANES), F32)
            dkc, dvc = [], []
            for hh, ((e, es), kk, vv) in enumerate(zip(exps, (ka[j], kb[j]), (va[j], vb[j]))):
                inv = 1.0 / (jnp.sum(e, axis=0, keepdims=True) + es)
                pr = e * inv
                dp = _dot(vv, do, NT)
                delta = jnp.sum(pr * dp, axis=0, keepdims=True)
                ds = (pr * (dp - delta)).astype(BF16)
                psd = es * inv * delta
                for t in range(PAIRS_PER_KV):
                    head = 2 * (j * PAIRS_PER_KV + t) + hh
                    dsink = dsink + jnp.where(
                        lane == head, -jnp.sum(psd[:, t * BLOCK:(t + 1) * BLOCK], axis=1, keepdims=True), 0.0)
                dq_r = dq_r + _dot(ds, kk, TN)
                dkc.append(_dot(ds, qp, NN))
                dvc.append(_dot(pr.astype(BF16), do, NN))
            dk_j.append(jnp.where(head_lo, dkc[0], dkc[1]))
            dv_j.append(jnp.where(head_lo, dvc[0], dvc[1]))
            dq = _rope(dq_r * HEAD_DIM ** -0.5, cs4, -sn4, lo).astype(BF16)
            for t in range(PAIRS_PER_KV):
                p = j * PAIRS_PER_KV + t
                dq_ref[:, p * LANES:(p + 1) * LANES] = dq[t * BLOCK:(t + 1) * BLOCK]
        tot_k = [t + pltpu.roll(t, HEAD_DIM, 1) for t in dk_j]
        tot_v = [t + pltpu.roll(t, HEAD_DIM, 1) for t in dv_j]
        dkext = jnp.where(head_lo, tot_k[0], tot_k[1])
        dvext = jnp.where(head_lo, tot_v[0], tot_v[1])
        dk_r = dkext[BLOCK:] + ck_ref[...]
        dk_ref[...] = _rope(dk_r, cs, -sn, lo).astype(BF16)
        dv_ref[...] = (dvext[BLOCK:] + cv_ref[...]).astype(BF16)
        ck_ref[...] = dkext[:BLOCK]
        cv_ref[...] = dvext[:BLOCK]
        dsink_ref[0:1, :] += dsink

    cur = lambda col: (lambda g: (nb - 1 - g, col))
    prv = lambda col: (lambda g: (jnp.maximum(nb - 2 - g, 0), col))
    blk = lambda w, f: pl.BlockSpec((BLOCK, w), f)
    return pl.pallas_call(
        body, name="attn_bwd", grid=(nb,),
        in_specs=[blk(ATTN_WIDTH, cur(0)), blk(LANES, cur(kcol)), blk(LANES, cur(vcol)),
                  blk(LANES, prv(kcol)), blk(LANES, prv(vcol)),
                  blk(LANES, cur(0)), blk(LANES, cur(0)), blk(LANES, prv(0)), blk(LANES, prv(0)),
                  pl.BlockSpec(memory_space=pltpu.SMEM),
                  blk(ATTN_WIDTH, cur(0))],
        out_specs=[blk(ATTN_WIDTH, cur(0)), blk(LANES, cur(0)), blk(LANES, cur(0)),
                   pl.BlockSpec((8, LANES), lambda g: (0, 0))],
        out_shape=[jax.ShapeDtypeStruct((s, ATTN_WIDTH), BF16), jax.ShapeDtypeStruct((s, LANES), BF16),
                   jax.ShapeDtypeStruct((s, LANES), BF16), jax.ShapeDtypeStruct((8, LANES), F32)],
        scratch_shapes=[pltpu.VMEM((BLOCK, LANES), F32), pltpu.VMEM((BLOCK, LANES), F32)],
        compiler_params=_cp(("arbitrary",), 32),
    )(proj, proj, proj, proj, proj, cos, sin, cos, sin, sinks, dmix)


def _causal_conv(x, prev8, w):
    row = lax.broadcasted_iota(jnp.int32, (8, 1), 0)
    r1, r2 = pltpu.roll(x, 1, 0), pltpu.roll(x, 2, 0)
    s1 = jnp.concatenate([jnp.where(row == 0, prev8[7:8], r1[:8]), r1[8:]], axis=0)
    s2 = jnp.concatenate([jnp.where(row == 0, prev8[6:7], jnp.where(row == 1, prev8[7:8], r2[:8])), r2[8:]], axis=0)
    return w[0:1] * s2 + w[1:2] * s1 + w[2:3] * x


def _conv_bwd(dy, x, w, next8):
    r = x.shape[0]
    row = lax.broadcasted_iota(jnp.int32, (8, 1), 0)
    r1, r2 = pltpu.roll(dy, r - 1, 0), pltpu.roll(dy, r - 2, 0)
    n1 = jnp.concatenate([r1[:r - 8], jnp.where(row == 7, next8[0:1], r1[r - 8:])], axis=0)
    n2 = jnp.concatenate([r2[:r - 8], jnp.where(row == 6, next8[0:1], jnp.where(row == 7, next8[1:2], r2[r - 8:]))],
                         axis=0)
    dx = w[2:3] * dy + w[1:2] * n1 + w[0:1] * n2
    dws = [jnp.sum(t * x, axis=0, keepdims=True) for t in (n2, n1, dy)]
    return dx, dws


CONV_COLS = 256


def _convmix_cols(d):
    conv_w = d - ATTN_WIDTH
    base = (ATTN_WIDTH + 2 * KV_WIDTH) // CONV_COLS
    step = conv_w // CONV_COLS
    return base, base + step, base + 2 * step, step


def _convmix_fwd(proj, scw8, s, d):
    gb0, gc0, h0, ncb = _convmix_cols(d)
    tr = _pick(s, 1024, 16)
    ni = s // tr

    def body(gb_ref, gc_ref, h_ref, w_ref, o_ref, carry_ref):
        @pl.when(pl.program_id(1) == 0)
        def _():
            carry_ref[...] = jnp.zeros_like(carry_ref)

        gch = gc_ref[...].astype(F32) * h_ref[...].astype(F32)
        cc = _causal_conv(gch, carry_ref[...], w_ref[...])
        o_ref[...] = (gb_ref[...].astype(F32) * cc).astype(BF16)
        carry_ref[...] = gch[tr - 8:]

    spec = lambda c0: pl.BlockSpec((tr, CONV_COLS), lambda j, i: (i, c0 + j))
    return pl.pallas_call(
        body, name="convmix_fwd", grid=(ncb, ni),
        in_specs=[spec(gb0), spec(gc0), spec(h0), pl.BlockSpec((8, CONV_COLS), lambda j, i: (0, j))],
        out_specs=pl.BlockSpec((tr, CONV_COLS), lambda j, i: (i, j)),
        out_shape=jax.ShapeDtypeStruct((s, d - ATTN_WIDTH), BF16),
        scratch_shapes=[pltpu.VMEM((8, CONV_COLS), F32)],
        compiler_params=_cp(("arbitrary", "arbitrary"), 32),
    )(proj, proj, proj, scw8)


def _convmix_bwd(proj, dmix, scw8, s, d):
    gb0, gc0, h0, ncb = _convmix_cols(d)
    tr = _pick(s, 1024, 16)
    ni = s // tr
    dc0 = ATTN_WIDTH // CONV_COLS

    def body(dc_ref, gb_ref, gc_ref, h_ref, gcp_ref, hp_ref, w_ref, d3_ref, dw_ref, nxt_ref):
        g = pl.program_id(1)
        i = ni - 1 - g

        @pl.when(g == 0)
        def _():
            nxt_ref[...] = jnp.zeros_like(nxt_ref)
            dw_ref[...] = jnp.zeros_like(dw_ref)

        w = w_ref[...]
        gb, gc, h = gb_ref[...].astype(F32), gc_ref[...].astype(F32), h_ref[...].astype(F32)
        gch = gc * h
        prev8 = (gcp_ref[...].astype(F32) * hp_ref[...].astype(F32))[8:16] * (i > 0).astype(F32)
        cc = _causal_conv(gch, prev8, w)
        dc = dc_ref[...].astype(F32)
        dcc = dc * gb
        dgch, dws = _conv_bwd(dcc, gch, w, nxt_ref[...])
        d3_ref[0] = (dc * cc).astype(BF16)
        d3_ref[1] = (dgch * h).astype(BF16)
        d3_ref[2] = (dgch * gc).astype(BF16)
        for t in range(3):
            dw_ref[t:t + 1, :] += dws[t]
        nxt_ref[...] = dcc[0:8]

    cur = lambda c0: pl.BlockSpec((tr, CONV_COLS), lambda j, g: (ni - 1 - g, c0 + j))
    prv = lambda c0: pl.BlockSpec((16, CONV_COLS), lambda j, g: (jnp.maximum((ni - 1 - g) * (tr // 16) - 1, 0), c0 + j))
    return pl.pallas_call(
        body, name="convmix_bwd", grid=(ncb, ni),
        in_specs=[cur(dc0), cur(gb0), cur(gc0), cur(h0), prv(gc0), prv(h0),
                  pl.BlockSpec((8, CONV_COLS), lambda j, g: (0, j))],
        out_specs=[pl.BlockSpec((3, tr, CONV_COLS), lambda j, g: (0, ni - 1 - g, j)),
                   pl.BlockSpec((8, CONV_COLS), lambda j, g: (0, j))],
        out_shape=[jax.ShapeDtypeStruct((3, s, d - ATTN_WIDTH), BF16), jax.ShapeDtypeStruct((8, d - ATTN_WIDTH), F32)],
        scratch_shapes=[pltpu.VMEM((8, CONV_COLS), F32)],
        compiler_params=_cp(("arbitrary", "arbitrary"), 32),
    )(dmix, proj, proj, proj, proj, proj, scw8)


def _ln_fwd(z):
    mu = jnp.mean(z, axis=-1, keepdims=True)
    zc = z - mu
    var = jnp.mean(zc * zc, axis=-1, keepdims=True)
    rstd = lax.rsqrt(var + LN_EPS)
    return zc * rstd, rstd


def _ln_bwd(dout, xh, rstd, g):
    dxh = dout * g
    c1 = jnp.mean(dxh, axis=-1, keepdims=True)
    c2 = jnp.mean(dxh * xh, axis=-1, keepdims=True)
    dz = rstd * (dxh - c1 - xh * c2)
    return dz, jnp.sum(dout * xh, axis=0, keepdims=True), jnp.sum(dout, axis=0, keepdims=True)


def _outproj_ln1(attn, conv, wout, x, g1, b1, s, d):
    tm = _pick(s, 512, 16)
    ka = attn.shape[1]
    one_buffer = pl.Buffered(1)

    def body(a_ref, c_ref, wt_ref, wb_ref, x_ref, g_ref, b_ref, x1_ref, x1b_ref, xh_ref, rs_ref):
        y = _dot(a_ref[...], wt_ref[...], NN) + _dot(c_ref[...], wb_ref[...], NN)
        xh, rstd = _ln_fwd(ALPHA * x_ref[...] + y)
        x1 = xh * g_ref[...] + b_ref[...]
        x1_ref[...] = x1
        x1b_ref[...] = x1.astype(BF16)
        xh_ref[...] = xh.astype(BF16)
        rs_ref[...] = rstd

    row = lambda w: pl.BlockSpec((tm, w), lambda i: (i, 0))
    vec = pl.BlockSpec((1, d), lambda i: (0, 0))
    return pl.pallas_call(
        body, name="outproj_ln1", grid=(s // tm,),
        in_specs=[row(ka), row(d - ka), pl.BlockSpec((ka, d), lambda i: (0, 0), pipeline_mode=one_buffer),
                  pl.BlockSpec((d - ka, d), lambda i: (ka // (d - ka), 0), pipeline_mode=one_buffer), row(d), vec, vec],
        out_specs=[row(d), row(d), row(d), row(1)],
        out_shape=[jax.ShapeDtypeStruct((s, d), F32), jax.ShapeDtypeStruct((s, d), BF16),
                   jax.ShapeDtypeStruct((s, d), BF16), jax.ShapeDtypeStruct((s, 1), F32)],
        compiler_params=_cp(("arbitrary",), 56),
    )(attn, conv, wout, wout, x, g1, b1)


def _ffn_up(x1b, wup, fcw8, s, d, dff):
    tm = _pick(s, 1024, 16)
    tn = _pick(dff, 512, LANES)
    nj, ni = dff // tn, s // tm

    def body(x_ref, wa_ref, wg_ref, ca_ref, cg_ref, u_ref, y_ref, h_ref, carry_ref):
        @pl.when(pl.program_id(1) == 0)
        def _():
            carry_ref[...] = jnp.zeros_like(carry_ref)

        xa = x_ref[...]
        ys = []
        for part, (w_ref, c_ref) in enumerate(((wa_ref, ca_ref), (wg_ref, cg_ref))):
            ub = _dot(xa, w_ref[...], NN).astype(BF16)
            u_ref[part] = ub
            u = ub.astype(F32)
            y = _causal_conv(u, carry_ref[part], c_ref[...])
            carry_ref[part] = u[tm - 8:]
            yb = y.astype(BF16)
            y_ref[part] = yb
            ys.append(yb.astype(F32))
        a2, g2 = ys
        sig = 1.0 / (1.0 + jnp.exp(-a2))
        h_ref[...] = (a2 * sig * g2).astype(BF16)

    return pl.pallas_call(
        body, name="ffn_up", grid=(nj, ni),
        in_specs=[pl.BlockSpec((tm, d), lambda j, i: (i, 0)),
                  pl.BlockSpec((d, tn), lambda j, i: (0, j)),
                  pl.BlockSpec((d, tn), lambda j, i: (0, j + nj)),
                  pl.BlockSpec((8, tn), lambda j, i: (0, j)),
                  pl.BlockSpec((8, tn), lambda j, i: (0, j + nj))],
        out_specs=[pl.BlockSpec((2, tm, tn), lambda j, i: (0, i, j)),
                   pl.BlockSpec((2, tm, tn), lambda j, i: (0, i, j)),
                   pl.BlockSpec((tm, tn), lambda j, i: (i, j))],
        out_shape=[jax.ShapeDtypeStruct((2, s, dff), BF16), jax.ShapeDtypeStruct((2, s, dff), BF16),
                   jax.ShapeDtypeStruct((s, dff), BF16)],
        scratch_shapes=[pltpu.VMEM((2, 8, tn), F32)],
        compiler_params=_cp(("arbitrary", "arbitrary"), 56),
    )(x1b, wup, wup, fcw8, fcw8)


def _ffn_mid_bwd(dz2b, wdown, u3, y3, fcw8, s, d, dff):
    tm = _pick(s, 1024, 16)
    tn = _pick(dff, 512, LANES)
    nj, ni = dff // tn, s // tm

    def body(dz_ref, wd_ref, u_ref, y_ref, ca_ref, cg_ref, du_ref, dw_ref, nxt_ref):
        @pl.when(pl.program_id(1) == 0)
        def _():
            nxt_ref[...] = jnp.zeros_like(nxt_ref)
            dw_ref[...] = jnp.zeros_like(dw_ref)

        a2, g2 = y_ref[0].astype(F32), y_ref[1].astype(F32)
        sig = 1.0 / (1.0 + jnp.exp(-a2))
        silu = a2 * sig
        dhv = _dot(dz_ref[...], wd_ref[...], NT)
        dys = (dhv * g2 * (sig * (1.0 + a2 * (1.0 - sig))), dhv * silu)
        for part, (c_ref, dy) in enumerate(zip((ca_ref, cg_ref), dys)):
            dx, dws = _conv_bwd(dy, u_ref[part].astype(F32), c_ref[...], nxt_ref[part])
            du_ref[part] = dx.astype(BF16)
            for t in range(3):
                dw_ref[part, t:t + 1, :] += dws[t]
            nxt_ref[part] = dy[0:8]

    return pl.pallas_call(
        body, name="ffn_mid_bwd", grid=(nj, ni),
        in_specs=[pl.BlockSpec((tm, d), lambda j, g: (ni - 1 - g, 0)),
                  pl.BlockSpec((tn, d), lambda j, g: (j, 0)),
                  pl.BlockSpec((2, tm, tn), lambda j, g: (0, ni - 1 - g, j)),
                  pl.BlockSpec((2, tm, tn), lambda j, g: (0, ni - 1 - g, j)),
                  pl.BlockSpec((8, tn), lambda j, g: (0, j)),
                  pl.BlockSpec((8, tn), lambda j, g: (0, j + nj))],
        out_specs=[pl.BlockSpec((2, tm, tn), lambda j, g: (0, ni - 1 - g, j)),
                   pl.BlockSpec((2, 8, tn), lambda j, g: (0, 0, j))],
        out_shape=[jax.ShapeDtypeStruct((2, s, dff), BF16), jax.ShapeDtypeStruct((2, 8, dff), F32)],
        scratch_shapes=[pltpu.VMEM((2, 8, tn), F32)],
        compiler_params=_cp(("arbitrary", "arbitrary"), 56),
    )(dz2b, wdown, u3, y3, fcw8, fcw8)


def _ffn_down_loss(hmid, wdown, x1, target, g2, b2, s, d, dff):
    tm = _pick(s, 512, SLAB)
    tk = _pick(dff, 1408, LANES)
    ni, nk = s // tm, dff // tk
    slab = min(SLAB, tm)

    def body(h_ref, w_ref, x1_ref, t_ref, g_ref, b_ref, dzb_ref, st_ref, acc_ref):
        i, kk = pl.program_id(0), pl.program_id(1)

        @pl.when((i == 0) & (kk == 0))
        def _():
            st_ref[...] = jnp.zeros_like(st_ref)

        part = _dot(h_ref[...], w_ref[...], NN)

        @pl.when(kk == 0)
        def _():
            acc_ref[...] = part

        @pl.when(kk > 0)
        def _():
            acc_ref[...] += part

        @pl.when(kk == nk - 1)
        def _():
            g, b = g_ref[...], b_ref[...]

            def one(sl, carry):
                rows = pl.ds(pl.multiple_of(sl * slab, slab), slab)
                xh, rstd = _ln_fwd(ALPHA * x1_ref[rows, :] + acc_ref[rows, :])
                diff = xh * g + b - t_ref[rows, :]
                sq = jnp.sum(jnp.sum(diff * diff, axis=1, keepdims=True), axis=0, keepdims=True)
                dz, dg, db = _ln_bwd(diff * (1.0 / d), xh, rstd, g)
                dzb_ref[rows, :] = dz.astype(BF16)
                st_ref[0:1, :] += dg
                st_ref[1:2, :] += db
                st_ref[2:3, :] += sq
                return carry

            lax.fori_loop(0, tm // slab, one, 0)

    row = pl.BlockSpec((tm, d), lambda i, kk: (i, 0))
    vec = pl.BlockSpec((1, d), lambda i, kk: (0, 0))
    return pl.pallas_call(
        body, name="ffn_down_loss", grid=(ni, nk),
        in_specs=[pl.BlockSpec((tm, tk), lambda i, kk: (i, kk)), pl.BlockSpec((tk, d), lambda i, kk: (kk, 0)),
                  row, row, vec, vec],
        out_specs=[row, pl.BlockSpec((8, d), lambda i, kk: (0, 0))],
        out_shape=[jax.ShapeDtypeStruct((s, d), BF16), jax.ShapeDtypeStruct((8, d), F32)],
        scratch_shapes=[pltpu.VMEM((tm, d), F32)],
        compiler_params=_cp(("arbitrary", "arbitrary"), 48),
    )(hmid, wdown, x1, target, g2, b2)


def _ffn_dx_ln1_bwd(du3, wup, dz2b, xh1, rstd1, g1, s, d, dff):
    tm = _pick(s, 512, SLAB)
    tk = _pick(dff, 2816, MXU_DIM)
    nkh = dff // tk
    ni, nk = s // tm, 2 * nkh
    slab = min(SLAB, tm)

    def body(a_ref, w_ref, dz2_ref, xh_ref, rs_ref, g_ref, dzb_ref, st_ref, acc_ref):
        i, kk = pl.program_id(0), pl.program_id(1)

        @pl.when((i == 0) & (kk == 0))
        def _():
            st_ref[...] = jnp.zeros_like(st_ref)

        part = _dot(a_ref[...], w_ref[...], NT)

        @pl.when(kk == 0)
        def _():
            acc_ref[...] = part

        @pl.when(kk > 0)
        def _():
            acc_ref[...] += part

        @pl.when(kk == nk - 1)
        def _():
            g = g_ref[...]

            def one(sl, carry):
                rows = pl.ds(pl.multiple_of(sl * slab, slab), slab)
                dx1 = ALPHA * dz2_ref[rows, :].astype(F32) + acc_ref[rows, :]
                dz, dg, db = _ln_bwd(dx1, xh_ref[rows, :].astype(F32), rs_ref[rows, :], g)
                dzb_ref[rows, :] = dz.astype(BF16)
                st_ref[0:1, :] += dg
                st_ref[1:2, :] += db
                return carry

            lax.fori_loop(0, tm // slab, one, 0)

    row = pl.BlockSpec((tm, d), lambda i, kk: (i, 0))
    row1 = pl.BlockSpec((tm, d), lambda i, kk: (i, 0), pipeline_mode=pl.Buffered(1))
    return pl.pallas_call(
        body, name="ffn_dx_ln1_bwd", grid=(ni, nk),
        in_specs=[pl.BlockSpec((None, tm, tk), lambda i, kk: (kk // nkh, i, kk % nkh)),
                  pl.BlockSpec((d, tk), lambda i, kk: (0, kk)),
                  row1, row1, pl.BlockSpec((tm, 1), lambda i, kk: (i, 0)), pl.BlockSpec((1, d), lambda i, kk: (0, 0))],
        out_specs=[row, pl.BlockSpec((8, d), lambda i, kk: (0, 0))],
        out_shape=[jax.ShapeDtypeStruct((s, d), BF16), jax.ShapeDtypeStruct((8, d), F32)],
        scratch_shapes=[pltpu.VMEM((tm, d), F32)],
        compiler_params=_cp(("arbitrary", "arbitrary"), 56),
    )(du3, wup, dz2b, xh1, rstd1, g1)


def _phase_mixer(x, win_t, wout, scw8, sinks, ln1_g, ln1_b, after=None):
    s, d = x.shape
    n_in = win_t.shape[0]
    cos, sin = _rope_tables(s)
    proj = _matmul(x, win_t, mode="nt", m=s, n=n_in, k=d, tm=_pick(s, 512, 16), tn=n_in, tk=d, out_dtype=BF16,
                   name="in_proj", vmem_mb=52, after=after,
                   b_spec=pl.BlockSpec((n_in, d), lambda j, i, kk: (0, 0), pipeline_mode=pl.Buffered(1)))
    attn = _attn_fwd(proj, sinks, cos, sin, s)
    conv = _convmix_fwd(proj, scw8, s, d)
    wout = wout(attn) if callable(wout) else wout
    x1, x1b, xh1, rstd1 = _outproj_ln1(attn, conv, wout, x, ln1_g, ln1_b, s, d)
    return dict(x=x, cos=cos, sin=sin, proj=proj, attn=attn, conv=conv, x1=x1, x1b=x1b, xh1=xh1, rstd1=rstd1,
                wout=wout)


def _phase_ffn(a, target, wup, wdown, fcw8, ln2_g, ln2_b):
    x1, x1b = a["x1"], a["x1b"]
    s, d = x1.shape
    dff = wdown.shape[0]
    u3, y3, hmid = _ffn_up(x1b, wup, fcw8, s, d, dff)
    dz2b, st2 = _ffn_down_loss(hmid, wdown, x1, target, ln2_g, ln2_b, s, d, dff)

    tnw = _pick(d, 1024, MXU_DIM)
    keep_b = pl.BlockSpec((s, tnw), lambda j, i, kk: (kk, j), pipeline_mode=pl.Buffered(1))
    g_wdown = _matmul(hmid, dz2b, mode="tn", m=dff, n=d, k=s, tm=_pick(dff, 512, MXU_DIM), tn=tnw, tk=s,
                      out_dtype=BF16, name="grad_w_down", vmem_mb=56, b_spec=keep_b)
    du3, dfcw = _ffn_mid_bwd(dz2b, wdown, u3, y3, fcw8, s, d, dff)
    tnu = _pick(dff, 512, MXU_DIM)
    njh = dff // tnu
    tmu = _pick(d, 1024, LANES)
    g_wup = _matmul(x1b, du3, mode="tn", m=d, n=2 * dff, k=s, tm=tmu, tn=tnu, tk=s, out_dtype=BF16,
                    name="grad_w_up", vmem_mb=58, m_outer=True,
                    a_spec=pl.BlockSpec((s, tmu), lambda j, i, kk: (kk, i), pipeline_mode=pl.Buffered(1)),
                    b_spec=pl.BlockSpec((None, s, tnu), lambda j, i, kk: (j // njh, kk, j % njh)))
    return dict(du3=du3, dz2b=dz2b, st2=st2, dfcw=dfcw, wdown=g_wdown, wup=g_wup)


def _phase_rest(a, f, wup, wout, win_t, scw8, sinks, ln1_g, between=None):
    xb, cos, sin, proj, attn, conv = a["x"], a["cos"], a["sin"], a["proj"], a["attn"], a["conv"]
    du3, dz2b, st2, dfcw = f["du3"], f["dz2b"], f["st2"], f["dfcw"]
    s, d = a["x1"].shape
    dff = wup.shape[1] // 2
    n_in = win_t.shape[0]
    ts = _pick(s, 2048, 16)
    dz1b, st1 = _ffn_dx_ln1_bwd(du3, wup, dz2b, a["xh1"], a["rstd1"], ln1_g, s, d, dff)
    after = between(dz1b) if between is not None else None

    tnw = _pick(d, 1024, MXU_DIM)
    halves = [_matmul(part, dz1b, mode="tn", m=part.shape[1], n=d, k=s, tm=_pick(part.shape[1], 512, LANES), tn=tnw,
                      tk=s, out_dtype=BF16, name="grad_w_out_" + tag, vmem_mb=56, after=after,
                      b_spec=pl.BlockSpec((s, tnw), lambda j, i, kk: (kk, j), pipeline_mode=pl.Buffered(1)))
              for tag, part in (("attn", attn), ("conv", conv))]
    g_wout = jnp.concatenate(halves, axis=0)
    dmix = _matmul(dz1b, wout, mode="nt", m=s, n=d, k=d, tm=_pick(s, 1024, 16), tn=_pick(d, 1024, LANES), tk=d,
                   out_dtype=BF16, name="out_dmix", vmem_mb=48, after=after)
    d3, dscw = _convmix_bwd(proj, dmix, scw8, s, d)
    dq, dk, dv, dsink = _attn_bwd(proj, dmix, sinks, cos, sin, s)
    dproj = jnp.concatenate([dq, dk, dv, d3[0], d3[1], d3[2]], axis=1)
    g_win_t = _matmul(dproj, xb, mode="tn", m=n_in, n=d, k=s, tm=_pick(n_in, 2176, LANES), tn=_pick(d, 512, LANES),
                      tk=ts, out_dtype=BF16, name="grad_w_in", vmem_mb=48)
    small = dict(loss_sq=st2[2, 0], ln2_g=st2[0], ln2_b=st2[1], ln1_g=st1[0], ln1_b=st1[1], sinks=dsink[0, :N_Q_HEADS],
                 fcw=jnp.concatenate([dfcw[0, :3], dfcw[1, :3]], axis=1), scw=dscw[:3])
    return (dproj, dz1b), dict(win_t=g_win_t, wout=g_wout), small


def _grad_x(dproj, dz1b, win_t, after=None):
    s, n_in = dproj.shape
    d = win_t.shape[1]
    return _matmul(dproj, win_t, mode="nn", m=s, n=d, k=n_in, tm=_pick(s, 512, 16), tn=_pick(d, 1024, LANES),
                   tk=n_in, out_dtype=F32, name="grad_x", vmem_mb=56, res=dz1b, alpha=ALPHA, after=after)


def _local_step(x, target, win_t, wout, wup, wdown, scw8, fcw8, sinks, ln1_g, ln1_b, ln2_g, ln2_b):
    a = _phase_mixer(x, win_t, wout, scw8, sinks, ln1_g, ln1_b)
    f = _phase_ffn(a, target, wup, wdown, fcw8, ln2_g, ln2_b)
    (dproj, dz1b), g, small = _phase_rest(a, f, wup, wout, win_t, scw8, sinks, ln1_g)
    return _grad_x(dproj, dz1b, win_t), dict(g, wup=f["wup"], wdown=f["wdown"]), small


W_IN, W_OUT = ("win_t",), ("wout",)
MIXER = W_IN + W_OUT
FFN = ("wup", "wdown")
BIG = MIXER + FFN


def _geom(shard_shapes):
    out = {}
    for name in BIG:
        r, c = shard_shapes[name]
        out[name] = ("col" if name == "wup" else "row", (r, c), (r // 2, c))
    return out


def _full_shape(kind, shard):
    r, c = shard
    return (N_CHIPS * r, c) if kind == "row" else (r, N_CHIPS * c)


def _piece_of(ref, kind, shard, chip, half):
    r, c = shard
    if kind == "row":
        return ref.at[pl.ds(chip * r + half * (r // 2), r // 2), :]
    return ref.at[pl.ds(half * (r // 2), r // 2), pl.ds(chip * c, c)]


def _shard_piece(ref, shard, half):
    r, _ = shard
    return ref.at[pl.ds(half * (r // 2), r // 2), :]


def _me():
    return lax.axis_index("x"), lax.axis_index("y"), lax.axis_index("c")


def _other_chips(x, y):
    return [(1 - x, y), (x, 1 - y), (1 - x, 1 - y)]


def _remote(src, dst, send_sem, recv_sem, dev):
    return pltpu.make_async_remote_copy(src_ref=src, dst_ref=dst, send_sem=send_sem, recv_sem=recv_sem,
                                        device_id=dev, device_id_type=MESH)


def _place_shard(w, chip1, kind, name):
    r, c = w.shape
    tr = _rows_tile(r, c, 16)
    nt = r // tr

    def body(chip_ref, w_ref, o_ref):
        o_ref[...] = w_ref[...].astype(BF16)

    out_map = (lambda i, chip_ref: (chip_ref[0] * nt + i, 0)) if kind == "row" else (lambda i, chip_ref: (i, chip_ref[0]))
    return pl.pallas_call(
        body, name="place_" + name,
        grid_spec=pltpu.PrefetchScalarGridSpec(
            num_scalar_prefetch=1, grid=(nt,),
            in_specs=[pl.BlockSpec((tr, c), lambda i, chip_ref: (i, 0))],
            out_specs=pl.BlockSpec((tr, c), out_map)),
        out_shape=jax.ShapeDtypeStruct(_full_shape(kind, (r, c)), BF16),
        compiler_params=_cp(("arbitrary",), 32),
    )(chip1, w)


def _allgather_weights(names, placed, geom, small_shards):
    nb, ns = len(names), len(small_shards)
    small_w = [a.shape[1] for a in small_shards]

    def body(*refs):
        sm = refs[nb:nb + ns]
        full = refs[nb + ns:2 * nb + ns]
        smf = refs[2 * nb + ns:2 * nb + 2 * ns]
        send, recv, loc = refs[2 * nb + 2 * ns:]
        x, y, c = _me()
        chip = 2 * x + y
        sib = (x, y, 1 - c)
        others = _other_chips(x, y)
        locals_, sends = [], []
        for m, name in enumerate(names):
            kind, shard, _ = geom[name]
            mine = _piece_of(full[m], kind, shard, chip, c)
            for k, (qx, qy) in enumerate(others):
                cp = _remote(mine, mine, send.at[6 * m + k], recv.at[6 * m + k], (qx, qy, c))
                cp.start()
                sends.append(cp)
        for t in range(ns):
            cp = pltpu.make_async_copy(sm[t], smf[t].at[:, pl.ds(chip * small_w[t], small_w[t])], loc.at[t])
            cp.start()
            locals_.append(cp)
            for k, (qx, qy) in enumerate(others):
                cp = _remote(sm[t], smf[t].at[:, pl.ds(chip * small_w[t], small_w[t])],
                             send.at[6 * nb + 3 * t + k], recv.at[6 * nb + 3 * t + k], (qx, qy, c))
                cp.start()
                sends.append(cp)
        for m, name in enumerate(names):
            kind, shard, _ = geom[name]
            for k, (qx, qy) in enumerate(others):
                got = _piece_of(full[m], kind, shard, 2 * qx + qy, c)
                _remote(got, got, send.at[6 * m + k], recv.at[6 * m + k], (qx, qy, c)).wait_recv()
                cp = _remote(got, got, send.at[6 * m + 3 + k], recv.at[6 * m + 3 + k], sib)
                cp.start()
                sends.append(cp)
        for t in range(ns):
            for k, (qx, qy) in enumerate(others):
                got = smf[t].at[:, pl.ds((2 * qx + qy) * small_w[t], small_w[t])]
                _remote(got, got, send.at[6 * nb + 3 * t + k], recv.at[6 * nb + 3 * t + k], (qx, qy, c)).wait_recv()
        for m, name in enumerate(names):
            kind, shard, _ = geom[name]
            for k, (qx, qy) in enumerate(others):
                got = _piece_of(full[m], kind, shard, 2 * qx + qy, 1 - c)
                _remote(got, got, send.at[6 * m + 3 + k], recv.at[6 * m + 3 + k], sib).wait_recv()
        for cp in sends:
            cp.wait_send()
        for cp in locals_:
            cp.wait()

    nsem = 6 * nb + 3 * ns
    out_shape = [jax.ShapeDtypeStruct(placed[n].shape, BF16) for n in names]
    out_shape += [jax.ShapeDtypeStruct((8, N_CHIPS * w), F32) for w in small_w]
    outs = pl.pallas_call(
        body, name="allgather_weights", in_specs=[ANY] * (nb + ns), out_specs=[ANY] * (nb + ns), out_shape=out_shape,
        input_output_aliases={m: m for m in range(nb)},
        scratch_shapes=[pltpu.SemaphoreType.DMA((nsem,)), pltpu.SemaphoreType.DMA((nsem,)),
                        pltpu.SemaphoreType.DMA((ns,))],
    )(*[placed[n] for n in names], *small_shards)
    return dict(zip(names, outs[:nb])), list(outs[nb:])


def _sibling_exchange(names, grads, geom):
    nb = len(names)

    def body(*refs):
        g = refs[:nb]
        got = refs[nb:2 * nb]
        send, recv = refs[2 * nb:]
        x, y, c = _me()
        sib = (x, y, 1 - c)
        cps = []
        for m, name in enumerate(names):
            kind, shard, _ = geom[name]
            for r in range(N_CHIPS):
                cp = _remote(_piece_of(g[m], kind, shard, r, 1 - c), got[m].at[r],
                             send.at[N_CHIPS * m + r], recv.at[N_CHIPS * m + r], sib)
                cp.start()
                cps.append(cp)
        for cp in cps:
            cp.wait_recv()
        for cp in cps:
            cp.wait_send()

    return pl.pallas_call(
        body, name="grad_sibling_exchange_" + names[0], in_specs=[ANY] * nb, out_specs=[ANY] * nb,
        out_shape=[jax.ShapeDtypeStruct((N_CHIPS,) + geom[n][2], BF16) for n in names],
        scratch_shapes=[pltpu.SemaphoreType.DMA((N_CHIPS * nb,)), pltpu.SemaphoreType.DMA((N_CHIPS * nb,))],
    )(*[grads[n] for n in names])


def _sibling_assemble(names, shards, geom):
    nb = len(names)

    def body(*refs):
        full = refs[nb:2 * nb]
        send, recv = refs[2 * nb:]
        x, y, c = _me()
        sib = (x, y, 1 - c)
        cps = []
        for m, name in enumerate(names):
            mine = _shard_piece(full[m], geom[name][1], c)
            cp = _remote(mine, mine, send.at[m], recv.at[m], sib)
            cp.start()
            cps.append(cp)
        for m, name in enumerate(names):
            theirs = _shard_piece(full[m], geom[name][1], 1 - c)
            _remote(theirs, theirs, send.at[m], recv.at[m], sib).wait_recv()
        for cp in cps:
            cp.wait_send()

    return pl.pallas_call(
        body, name="grad_sibling_assemble_" + names[0], in_specs=[ANY] * nb, out_specs=[ANY] * nb,
        out_shape=[jax.ShapeDtypeStruct(geom[n][1], F32) for n in names],
        input_output_aliases={m: m for m in range(nb)},
        scratch_shapes=[pltpu.SemaphoreType.DMA((nb,)), pltpu.SemaphoreType.DMA((nb,))],
    )(*shards)


HBM = pl.BlockSpec(memory_space=pltpu.HBM)
SEM = pl.BlockSpec(memory_space=pltpu.SEMAPHORE)
EFFECT = pltpu.SideEffectType.DATAFLOW_SIDE_EFFECTING
TOKEN = jax.ShapeDtypeStruct((8, LANES), F32)


def _hbm(a):
    return pltpu.with_memory_space_constraint(a, pltpu.HBM)


def _gather_copies(names, full, geom, send, recv):
    x, y, c = _me()
    out = []
    for m, name in enumerate(names):
        kind, shard, _ = geom[name]
        mine = _piece_of(full[m], kind, shard, 2 * x + y, c)
        for k, (qx, qy) in enumerate(_other_chips(x, y)):
            theirs = _piece_of(full[m], kind, shard, 2 * qx + qy, c)
            out.append((_remote(mine, mine, send.at[3 * m + k], recv.at[3 * m + k], (qx, qy, c)),
                        _remote(theirs, theirs, send.at[3 * m + k], recv.at[3 * m + k], (qx, qy, c))))
    return out


def _gather_start(names, placed, geom, after):
    nb = len(names)

    def body(*refs):
        full = refs[:nb]
        send, recv = refs[nb + 1], refs[nb + 2]
        token = refs[2 * nb + 3]
        for cp, _ in _gather_copies(names, full, geom, send, recv):
            cp.start()
        token[...] = jnp.zeros_like(token)

    outs = pl.pallas_call(
        body, name="gather_start_" + names[0],
        out_shape=(pltpu.SemaphoreType.DMA((3 * nb,)), pltpu.SemaphoreType.DMA((3 * nb,)),
                   *[pltpu.HBM(placed[n].shape, BF16) for n in names], TOKEN),
        in_specs=[HBM] * nb + [ANY], out_specs=(SEM, SEM, *[HBM] * nb, pl.BlockSpec(memory_space=pltpu.VMEM)),
        input_output_aliases={m: 2 + m for m in range(nb)},
        compiler_params=pltpu.CompilerParams(has_side_effects=EFFECT),
    )(*[_hbm(placed[n]) for n in names], after)
    return outs[0], outs[1], list(outs[2:2 + nb]), outs[2 + nb]


def _gather_wait(names, send, recv, thru, geom, after):
    nb = len(names)

    def body(*refs):
        full = refs[:nb]
        for mine, theirs in _gather_copies(names, full, geom, refs[nb], refs[nb + 1]):
            mine.wait_send()
            theirs.wait_recv()

    return pl.pallas_call(
        body, name="gather_wait_" + names[0], out_shape=tuple(pltpu.HBM(t.shape, t.dtype) for t in thru),
        in_specs=[HBM] * nb + [SEM, SEM, ANY], out_specs=tuple([HBM] * nb),
        input_output_aliases={m: m for m in range(nb)},
        compiler_params=pltpu.CompilerParams(has_side_effects=EFFECT),
    )(*thru, send, recv, after)


def _gather_forward(names, full, geom):
    nb = len(names)

    def body(*refs):
        arr = refs[nb:2 * nb]
        send, recv = refs[2 * nb:]
        x, y, c = _me()
        sib = (x, y, 1 - c)
        cps = []
        for m, name in enumerate(names):
            kind, shard, _ = geom[name]
            for k, (qx, qy) in enumerate(_other_chips(x, y)):
                got = _piece_of(arr[m], kind, shard, 2 * qx + qy, c)
                cp = _remote(got, got, send.at[3 * m + k], recv.at[3 * m + k], sib)
                cp.start()
                cps.append(cp)
        for m, name in enumerate(names):
            kind, shard, _ = geom[name]
            for k, (qx, qy) in enumerate(_other_chips(x, y)):
                theirs = _piece_of(arr[m], kind, shard, 2 * qx + qy, 1 - c)
                _remote(theirs, theirs, send.at[3 * m + k], recv.at[3 * m + k], sib).wait_recv()
        for cp in cps:
            cp.wait_send()

    return pl.pallas_call(
        body, name="gather_forward_" + names[0], in_specs=[ANY] * nb, out_specs=[ANY] * nb,
        out_shape=[jax.ShapeDtypeStruct(a.shape, a.dtype) for a in full],
        input_output_aliases={m: m for m in range(nb)},
        scratch_shapes=[pltpu.SemaphoreType.DMA((3 * nb,)), pltpu.SemaphoreType.DMA((3 * nb,))],
    )(*full)


def _scatter_copies(nb, t, got, send, recv):
    x, y, c = _me()
    return [_remote(t[m].at[2 * qx + qy], got[m].at[k], send.at[3 * m + k], recv.at[3 * m + k], (qx, qy, c))
            for m in range(nb) for k, (qx, qy) in enumerate(_other_chips(x, y))]


def _chip_exchange_start(names, chip_sums, geom, after):
    nb = len(names)
    lands = [lax.empty((N_CHIPS - 1,) + geom[n][2], BF16) for n in names]

    def body(*refs):
        t, got = refs[:nb], refs[nb:2 * nb]
        send, recv = refs[2 * nb + 1], refs[2 * nb + 2]
        token = refs[4 * nb + 3]
        for cp in _scatter_copies(nb, t, got, send, recv):
            cp.start()
        token[...] = jnp.zeros_like(token)

    both = list(chip_sums) + lands
    outs = pl.pallas_call(
        body, name="grad_chip_start_" + names[0],
        out_shape=(pltpu.SemaphoreType.DMA((3 * nb,)), pltpu.SemaphoreType.DMA((3 * nb,)),
                   *[pltpu.HBM(a.shape, a.dtype) for a in both], TOKEN),
        in_specs=[HBM] * (2 * nb) + [ANY],
        out_specs=(SEM, SEM, *[HBM] * (2 * nb), pl.BlockSpec(memory_space=pltpu.VMEM)),
        input_output_aliases={m: 2 + m for m in range(2 * nb)},
        compiler_params=pltpu.CompilerParams(has_side_effects=EFFECT),
    )(*[_hbm(a) for a in both], after)
    return outs[0], outs[1], list(outs[2:2 + 2 * nb]), outs[2 + 2 * nb]


def _chip_exchange_wait(names, send, recv, thru, after):
    nb = len(names)

    def body(*refs):
        for cp in _scatter_copies(nb, refs[:nb], refs[nb:2 * nb], refs[2 * nb], refs[2 * nb + 1]):
            cp.wait_send()
            cp.wait_recv()

    outs = pl.pallas_call(
        body, name="grad_chip_wait_" + names[0], out_shape=tuple(pltpu.HBM(t.shape, t.dtype) for t in thru),
        in_specs=[HBM] * (2 * nb) + [SEM, SEM, ANY], out_specs=tuple([HBM] * (2 * nb)),
        input_output_aliases={m: m for m in range(2 * nb)},
        compiler_params=pltpu.CompilerParams(has_side_effects=EFFECT),
    )(*thru, send, recv, after)
    return list(outs[:nb]), list(outs[nb:])


def _sibling_copies(names, g, got, geom, send, recv):
    x, y, c = _me()
    out = []
    for m, name in enumerate(names):
        kind, shard, _ = geom[name]
        for r in range(N_CHIPS):
            out.append(_remote(_piece_of(g[m], kind, shard, r, 1 - c), got[m].at[r],
                               send.at[N_CHIPS * m + r], recv.at[N_CHIPS * m + r], (x, y, 1 - c)))
    return out


def _sibling_exchange_start(names, grads, geom, after):
    nb = len(names)
    lands = [lax.empty((N_CHIPS,) + geom[n][2], BF16) for n in names]

    def body(*refs):
        for cp in _sibling_copies(names, refs[:nb], refs[nb:2 * nb], geom, refs[2 * nb + 1], refs[2 * nb + 2]):
            cp.start()
        token = refs[4 * nb + 3]
        token[...] = jnp.zeros_like(token)

    both = [grads[n] for n in names] + lands
    outs = pl.pallas_call(
        body, name="grad_sibling_start_" + names[0],
        out_shape=(pltpu.SemaphoreType.DMA((N_CHIPS * nb,)), pltpu.SemaphoreType.DMA((N_CHIPS * nb,)),
                   *[pltpu.HBM(a.shape, a.dtype) for a in both], TOKEN),
        in_specs=[HBM] * (2 * nb) + [ANY],
        out_specs=(SEM, SEM, *[HBM] * (2 * nb), pl.BlockSpec(memory_space=pltpu.VMEM)),
        input_output_aliases={m: 2 + m for m in range(2 * nb)},
        compiler_params=pltpu.CompilerParams(has_side_effects=EFFECT),
    )(*[_hbm(a) for a in both], after)
    return outs[0], outs[1], list(outs[2:2 + 2 * nb]), outs[2 + 2 * nb]


def _sibling_exchange_wait(names, send, recv, thru, geom, after):
    nb = len(names)

    def body(*refs):
        for cp in _sibling_copies(names, refs[:nb], refs[nb:2 * nb], geom, refs[2 * nb], refs[2 * nb + 1]):
            cp.wait_send()
            cp.wait_recv()

    outs = pl.pallas_call(
        body, name="grad_sibling_wait_" + names[0], out_shape=tuple(pltpu.HBM(t.shape, t.dtype) for t in thru),
        in_specs=[HBM] * (2 * nb) + [SEM, SEM, ANY], out_specs=tuple([HBM] * (2 * nb)),
        input_output_aliases={m: m for m in range(2 * nb)},
        compiler_params=pltpu.CompilerParams(has_side_effects=EFFECT),
    )(*thru, send, recv, after)
    return list(outs[:nb]), list(outs[nb:])


def _allreduce_small(part):
    rows = part.shape[0]
    flips = [(a, b, e) for a in (0, 1) for b in (0, 1) for e in (0, 1) if (a, b, e) != (0, 0, 0)]

    def body(p_ref, o_ref, all_ref, send, recv):
        x, y, c = _me()
        me = 4 * x + 2 * y + c
        all_ref[me] = p_ref[...]
        cps = []
        for k, (a, b, e) in enumerate(flips):
            cp = _remote(p_ref, all_ref.at[me], send.at[k], recv.at[k], (x ^ a, y ^ b, c ^ e))
            cp.start()
            cps.append(cp)
        for k, (a, b, e) in enumerate(flips):
            peer = 4 * (x ^ a) + 2 * (y ^ b) + (c ^ e)
            _remote(p_ref, all_ref.at[peer], send.at[k], recv.at[k], (x ^ a, y ^ b, c ^ e)).wait_recv()
        for cp in cps:
            cp.wait_send()
        tot = all_ref[0]
        for dev in range(1, 8):
            tot = tot + all_ref[dev]
        o_ref[...] = tot

    vm = pl.BlockSpec(memory_space=pltpu.VMEM)
    return pl.pallas_call(
        body, name="allreduce_small", in_specs=[vm], out_specs=vm, out_shape=jax.ShapeDtypeStruct((rows, LANES), F32),
        scratch_shapes=[pltpu.VMEM((8, rows, LANES), F32), pltpu.SemaphoreType.DMA((7,)), pltpu.SemaphoreType.DMA((7,))],
    )(part)


def _rows_tile(rows, cols, mult, elems=1 << 19):
    return _pick(rows, max(mult, elems // cols // mult * mult), mult)


ADD_TILE = 1 << 20


def _add_pairs(g, got, kind, shard, where, name):
    p, r, c = got.shape
    tr = _rows_tile(r, c, 16, ADD_TILE)
    nt = r // tr

    def body(w_ref, a_ref, b_ref, o_ref):
        o_ref[...] = (a_ref[...].astype(F32) + b_ref[...].astype(F32)).astype(BF16)

    if kind == "row":
        g_map = lambda q, i, w_ref: ((2 * q + w_ref[1]) * nt + i, 0)
    else:
        g_map = lambda q, i, w_ref: (w_ref[1] * nt + i, q)
    spec = pl.BlockSpec((None, tr, c), lambda q, i, w_ref: (q, i, 0))
    return pl.pallas_call(
        body, name="grad_add_sibling_" + name,
        grid_spec=pltpu.PrefetchScalarGridSpec(
            num_scalar_prefetch=1, grid=(p, nt), in_specs=[pl.BlockSpec((tr, c), g_map), spec], out_specs=spec),
        out_shape=jax.ShapeDtypeStruct((p, r, c), BF16), compiler_params=_cp(("arbitrary", "arbitrary"), 32),
    )(where, g, got)


def _add_four(t, got, shard, where, name):
    _, r, c = t.shape
    tr = _rows_tile(r, c, 16, ADD_TILE)
    nt = r // tr

    def body(w_ref, own, t0, t1, t2, o_ref):
        o_ref[...] = ((own[...].astype(F32) + t0[...].astype(F32)) + t1[...].astype(F32)) + t2[...].astype(F32)

    spec = lambda q: pl.BlockSpec((None, tr, c), lambda i, w_ref: (q, i, 0))
    return pl.pallas_call(
        body, name="grad_add_chips_" + name,
        grid_spec=pltpu.PrefetchScalarGridSpec(
            num_scalar_prefetch=1, grid=(nt,),
            in_specs=[pl.BlockSpec((None, tr, c), lambda i, w_ref: (w_ref[0], i, 0)), spec(0), spec(1), spec(2)],
            out_specs=pl.BlockSpec((tr, c), lambda i, w_ref: (w_ref[1] * nt + i, 0))),
        out_shape=jax.ShapeDtypeStruct(shard, F32), compiler_params=_cp(("arbitrary",), 48),
    )(where, t, got, got, got)


def _adamw(w, g, m, v, name):
    r, c = w.shape
    tr = _rows_tile(r, c, 8)

    def body(w_ref, g_ref, m_ref, v_ref, go_ref, d_ref, mo_ref, vo_ref):
        gv = g_ref[...]
        mn = ADAM_B1 * m_ref[...] + (1.0 - ADAM_B1) * gv
        vn = ADAM_B2 * v_ref[...] + (1.0 - ADAM_B2) * (gv * gv)
        m_hat = mn / (1.0 - ADAM_B1 ** ADAM_STEP)
        v_hat = vn / (1.0 - ADAM_B2 ** ADAM_STEP)
        go_ref[...] = gv
        d_ref[...] = -ADAM_LR * (m_hat / (jnp.sqrt(v_hat) + ADAM_EPS) + ADAM_WD * w_ref[...])
        mo_ref[...] = mn
        vo_ref[...] = vn

    spec = pl.BlockSpec((tr, c), lambda i: (i, 0))
    return pl.pallas_call(
        body, name=name, grid=(r // tr,), in_specs=[spec] * 4, out_specs=[spec] * 4,
        out_shape=[jax.ShapeDtypeStruct((r, c), F32)] * 4, compiler_params=_cp(("arbitrary",), 32),
    )(w, g, m, v)


def _pack(vectors, rows):
    flat = jnp.concatenate([v.reshape(-1).astype(F32) for v in vectors])
    return jnp.pad(flat, (0, rows * LANES - flat.shape[0])).reshape(rows, LANES)


def _unpack(packed, shapes):
    flat = packed.reshape(-1)
    out, off = [], 0
    for shp in shapes:
        n = 1
        for t in shp:
            n *= t
        out.append(flat[off:off + n].reshape(shp))
        off += n
    return out


def _rows_for(shapes):
    n = sum(functools.reduce(lambda a, b: a * b, shp, 1) for shp in shapes)
    return -(-n // (8 * LANES)) * 8


def kernel(x, w_in, attn_sinks, short_conv_w, w_out, ln1_g, ln1_b, ffn_w_up, ffn_conv_w, ffn_w_down, ln2_g, ln2_b, loss_target, m_w_in, m_attn_sinks, m_short_conv_w, m_w_out, m_ln1_g, m_ln1_b, m_ffn_w_up, m_ffn_conv_w, m_ffn_w_down, m_ln2_g, m_ln2_b, v_w_in, v_attn_sinks, v_short_conv_w, v_w_out, v_ln1_g, v_ln1_b, v_ffn_w_up, v_ffn_conv_w, v_ffn_w_down, v_ln2_g, v_ln2_b):
    xs, tgt = x[0], loss_target[0]
    s, d = xs.shape
    chip = 2 * lax.axis_index("x") + lax.axis_index("y")

    w_big = dict(win_t=w_in[0], wout=w_out[0], wup=ffn_w_up[0], wdown=ffn_w_down[0])
    m_big = dict(win_t=m_w_in[0], wout=m_w_out[0], wup=m_ffn_w_up[0], wdown=m_ffn_w_down[0])
    v_big = dict(win_t=v_w_in[0], wout=v_w_out[0], wup=v_ffn_w_up[0], wdown=v_ffn_w_down[0])
    to_place = dict(w_big, win_t=w_in[0].T)
    geom = _geom({n: to_place[n].shape for n in BIG})
    pad8 = lambda a: jnp.pad(a[0], ((0, 5), (0, 0)))
    where = jnp.stack([chip, lax.axis_index("c")]).astype(jnp.int32)
    placed = {n: _place_shard(to_place[n], where[:1], geom[n][0], n) for n in BIG}
    full, (scw8, fcw8) = _allgather_weights(W_IN, placed, geom, [pad8(short_conv_w), pad8(ffn_conv_w)])
    o_send, o_recv, o_thru, o_token = _gather_start(W_OUT, placed, geom, scw8)
    send, recv, thru, token = _gather_start(FFN, placed, geom, o_token)

    def wout_behind(attn):
        return _gather_forward(W_OUT, _gather_wait(W_OUT, o_send, o_recv, o_thru, geom, attn), geom)[0]

    a = _phase_mixer(xs, full["win_t"], wout_behind, scw8, attn_sinks, ln1_g, ln1_b, after=token)
    full["wout"] = a["wout"]
    landed = _gather_forward(FFN, _gather_wait(FFN, send, recv, thru, geom, a["x1b"]), geom)
    full.update(zip(FFN, landed))
    f = _phase_ffn(a, tgt, full["wup"], full["wdown"], fcw8, ln2_g, ln2_b)

    def add_pairs(names, grads, from_sibling):
        return [_add_pairs(grads[m], from_sibling[m], geom[n][0], geom[n][1], where, n) for m, n in enumerate(names)]

    sib_send, sib_recv, sib_thru, sib_token = _sibling_exchange_start(FFN, f, geom, f["st2"])
    started = {}

    def between(dz1b):
        grads, from_sibling = _sibling_exchange_wait(FFN, sib_send, sib_recv, sib_thru, geom, dz1b)
        started["sums"] = add_pairs(FFN, grads, from_sibling)
        started["chip"] = _chip_exchange_start(FFN, started["sums"], geom, f["st2"])
        return started["chip"][3]

    (dproj, dz1b), g_mixer, g_small = _phase_rest(a, f, full["wup"], full["wout"], full["win_t"], scw8, attn_sinks,
                                                  ln1_g + sib_token[0:1, 0:1], between=between)
    send, recv, thru, _ = started["chip"]

    def finish(names, exchanged):
        sums, from_chips = exchanged
        halves = [_add_four(sums[m], from_chips[m], geom[n][1], where, n) for m, n in enumerate(names)]
        shards = _sibling_assemble(names, halves, geom)
        grads = {n: shards[m].T if n == "win_t" else shards[m] for m, n in enumerate(names)}
        return {n: _adamw(w_big[n], grads[n], m_big[n], v_big[n], "adamw_" + n) for n in names}

    mixer_sums = add_pairs(MIXER, [g_mixer[n] for n in MIXER], _sibling_exchange(MIXER, g_mixer, geom))
    send2, recv2, thru2, token2 = _chip_exchange_start(MIXER, mixer_sums, geom, f["st2"])
    grad_x = _grad_x(dproj, dz1b, full["win_t"], after=token2)
    upd = finish(FFN, _chip_exchange_wait(FFN, send, recv, thru, grad_x))
    upd.update(finish(MIXER, _chip_exchange_wait(MIXER, send2, recv2, thru2, upd[FFN[0]][1])))

    small_names = ("ln1_g", "ln1_b", "ln2_g", "ln2_b", "sinks", "fcw", "scw")
    small_shapes = [g_small[n].shape for n in small_names]
    red = _allreduce_small(_pack([g_small["loss_sq"].reshape(1)] + [g_small[n] for n in small_names],
                                 _rows_for([(1,)] + small_shapes)))
    loss_sq, *gs = _unpack(red, [(1,)] + small_shapes)
    gs = dict(zip(small_names, gs))
    loss = (0.5 / d) * loss_sq[0]
    fw, sw = ffn_conv_w.shape[2], short_conv_w.shape[2]
    gs["fcw"] = lax.dynamic_slice_in_dim(gs["fcw"], chip * fw, fw, axis=1)
    gs["scw"] = lax.dynamic_slice_in_dim(gs["scw"], chip * sw, sw, axis=1)

    sm_w = dict(ln1_g=ln1_g[0], ln1_b=ln1_b[0], ln2_g=ln2_g[0], ln2_b=ln2_b[0], sinks=attn_sinks[0],
                fcw=ffn_conv_w[0], scw=short_conv_w[0])
    sm_m = dict(ln1_g=m_ln1_g[0], ln1_b=m_ln1_b[0], ln2_g=m_ln2_g[0], ln2_b=m_ln2_b[0], sinks=m_attn_sinks[0],
                fcw=m_ffn_conv_w[0], scw=m_short_conv_w[0])
    sm_v = dict(ln1_g=v_ln1_g[0], ln1_b=v_ln1_b[0], ln2_g=v_ln2_g[0], ln2_b=v_ln2_b[0], sinks=v_attn_sinks[0],
                fcw=v_ffn_conv_w[0], scw=v_short_conv_w[0])
    shapes = [sm_w[n].shape for n in small_names]
    rows = _rows_for(shapes)
    packed = [_pack([t[n] for n in small_names], rows) for t in (sm_w, gs, sm_m, sm_v)]
    sm_out = [dict(zip(small_names, _unpack(a, shapes))) for a in _adamw(*packed, "adamw_small")]

    def leaf(kind, name):
        if name in ("w_in", "w_out", "ffn_w_up", "ffn_w_down"):
            key = dict(w_in="win_t", w_out="wout", ffn_w_up="wup", ffn_w_down="wdown")[name]
            return upd[key][kind][None]
        key = dict(attn_sinks="sinks", short_conv_w="scw", ffn_conv_w="fcw").get(name, name)
        return sm_out[kind][key][None]

    order = ("w_in", "attn_sinks", "short_conv_w", "w_out", "ln1_g", "ln1_b", "ffn_w_up", "ffn_conv_w", "ffn_w_down",
             "ln2_g", "ln2_b")
    outs = [loss, grad_x[None]]
    for kind in range(4):
        outs += [leaf(kind, n) for n in order]
    return tuple(outs)
```

```python
import functools

import jax
import jax.numpy as jnp
from jax import lax
from jax.experimental import pallas as pl
from jax.experimental.pallas import tpu as pltpu

F32 = jnp.float32
BF16 = jnp.bfloat16
MESH = pl.DeviceIdType.MESH
ANY = pl.BlockSpec(memory_space=pl.ANY)

HEAD_DIM = 64
N_Q_HEADS = 16
N_KV_HEADS = 2
ATTN_WIDTH = N_Q_HEADS * HEAD_DIM
KV_WIDTH = N_KV_HEADS * HEAD_DIM
BLOCK = 128
ROPE_THETA = 10000.0
LN_EPS = 1e-5
ALPHA = 2.0 ** 0.25
NEG_INF = -1e30
ADAM_LR, ADAM_B1, ADAM_B2, ADAM_EPS, ADAM_WD, ADAM_STEP = 0.001, 0.9, 0.999, 1e-08, 0.01, 10
N_CHIPS = 4
LANES = 128
MXU_DIM = 256
SLAB = 128


def _cp(sem, vmem_mb):
    return pltpu.CompilerParams(dimension_semantics=sem, vmem_limit_bytes=vmem_mb << 20)


def _matmul(a, b, *, mode, m, n, k, tm, tn, tk, out_dtype, name, vmem_mb, a_spec=None, b_spec=None,
            res=None, alpha=1.0, after=None, m_outer=False):
    nj, ni, nk = n // tn, m // tm, k // tk
    assert nj * tn == n and ni * tm == m and nk * tk == k, (name, m, n, k, tm, tn, tk)
    if mode == "nn":
        dims = ((1,), (0,))
        a_spec = a_spec or pl.BlockSpec((tm, tk), lambda j, i, kk: (i, kk))
        b_spec = b_spec or pl.BlockSpec((tk, tn), lambda j, i, kk: (kk, j))
    elif mode == "nt":
        dims = ((1,), (1,))
        a_spec = a_spec or pl.BlockSpec((tm, tk), lambda j, i, kk: (i, kk))
        b_spec = b_spec or pl.BlockSpec((tn, tk), lambda j, i, kk: (j, kk))
    else:
        dims = ((0,), (0,))
        a_spec = a_spec or pl.BlockSpec((tk, tm), lambda j, i, kk: (kk, i))
        b_spec = b_spec or pl.BlockSpec((tk, tn), lambda j, i, kk: (kk, j))
    has_res = res is not None
    has_after = after is not None

    def body(*refs):
        refs = refs[1:] if has_after else refs
        a_ref, b_ref = refs[0], refs[1]
        res_ref = refs[2] if has_res else None
        o_ref = refs[2 + has_res]
        part = lax.dot_general(a_ref[...].astype(BF16), b_ref[...].astype(BF16), (dims, ((), ())),
                               preferred_element_type=F32)

        def finish(acc):
            if has_res:
                acc = acc + alpha * res_ref[...].astype(F32)
            o_ref[...] = acc.astype(o_ref.dtype)

        if nk == 1:
            finish(part)
        else:
            acc_ref = refs[3 + has_res]
            kk = pl.program_id(2)

            @pl.when(kk == 0)
            def _():
                acc_ref[...] = part

            @pl.when(kk > 0)
            def _():
                acc_ref[...] += part

            @pl.when(kk == nk - 1)
            def _():
                finish(acc_ref[...])

    in_specs = [a_spec, b_spec]
    args = [a, b]
    if has_res:
        in_specs.append(pl.BlockSpec((tm, tn), lambda j, i, kk: (i, j)))
        args.append(res)
    if has_after:
        in_specs.insert(0, pl.BlockSpec(after.shape, lambda j, i, kk: (0, 0)))
        args.insert(0, after)
    out_spec = pl.BlockSpec((tm, tn), lambda j, i, kk: (i, j))
    grid = (nj, ni, nk)
    if m_outer:
        swap = lambda sp: pl.BlockSpec(sp.block_shape, (lambda f: lambda i, j, kk: f(j, i, kk))(sp.index_map),
                                       pipeline_mode=sp.pipeline_mode)
        in_specs, out_spec, grid = [swap(sp) for sp in in_specs], swap(out_spec), (ni, nj, nk)
    return pl.pallas_call(
        body, name=name, grid=grid, in_specs=in_specs,
        out_specs=out_spec,
        out_shape=jax.ShapeDtypeStruct((m, n), out_dtype),
        scratch_shapes=[pltpu.VMEM((tm, tn), F32)] if nk > 1 else [],
        compiler_params=_cp(("arbitrary", "arbitrary", "arbitrary"), vmem_mb),
    )(*args)


def _pick(total, want, mult):
    if total <= want:
        return total
    for t in range(want, 0, -1):
        if total % t == 0 and t % mult == 0:
            return t
    return total


def _rope_tables(s):
    half = HEAD_DIM // 2
    inv_freq = ROPE_THETA ** (-jnp.arange(half, dtype=F32) / half)
    ang = jnp.arange(s, dtype=F32)[:, None] * inv_freq[None, :]
    cos = jnp.tile(jnp.cos(ang), (1, LANES // half))
    sin = jnp.tile(jnp.concatenate([-jnp.sin(ang), jnp.sin(ang)], axis=1), (1, LANES // HEAD_DIM))
    return cos, sin


def _rope(x, cos, sin, lo):
    partner = jnp.where(lo, pltpu.roll(x, LANES - HEAD_DIM // 2, 1), pltpu.roll(x, HEAD_DIM // 2, 1))
    return x * cos + partner * sin


def _dot(a, b, dims):
    return lax.dot_general(a, b, (dims, ((), ())), preferred_element_type=F32)


NN, NT, TN = ((1,), (0,)), ((1,), (1,)), ((0,), (0,))


def _kv_variants(t, head_lo):
    r = pltpu.roll(t, HEAD_DIM, 1)
    zero = jnp.zeros_like(t)
    a = (jnp.where(head_lo, t, zero).astype(BF16), jnp.where(head_lo, r, zero).astype(BF16))
    b = (jnp.where(head_lo, zero, r).astype(BF16), jnp.where(head_lo, zero, t).astype(BF16))
    return a, b


PAIRS_PER_KV = N_Q_HEADS // 2 // N_KV_HEADS
STACK = PAIRS_PER_KV * BLOCK


def _stack_pairs(ref, j, fn):
    return jnp.concatenate([fn(ref[:, p * LANES:(p + 1) * LANES])
                            for p in range(j * PAIRS_PER_KV, (j + 1) * PAIRS_PER_KV)], axis=0)


def _sink_row(sink_ref, j, hh):
    col = lax.broadcasted_iota(jnp.int32, (1, STACK), 1)
    heads = [2 * p + hh for p in range(j * PAIRS_PER_KV, (j + 1) * PAIRS_PER_KV)]
    row = jnp.full((1, STACK), sink_ref[0, heads[-1]], F32)
    for t in range(PAIRS_PER_KV - 2, -1, -1):
        row = jnp.where(col < (t + 1) * BLOCK, sink_ref[0, heads[t]], row)
    return row


def _attn_exps(qp, ka, kb, valid, sink_a, sink_b):
    out = []
    for kk, sink in ((ka, sink_a), (kb, sink_b)):
        s = jnp.where(valid, _dot(kk, qp, NT), NEG_INF)
        mx = jnp.maximum(jnp.max(s, axis=0, keepdims=True), sink)
        out.append((jnp.exp(s - mx), jnp.exp(sink - mx)))
    return out


def _attn_common(i, q_ref, k_ref, v_ref, kp_ref, vp_ref, cos_ref, sin_ref, cosp_ref, sinp_ref):
    lane = lax.broadcasted_iota(jnp.int32, (1, LANES), 1)
    lo = (lane % HEAD_DIM) < (HEAD_DIM // 2)
    head_lo = lane < HEAD_DIM
    cos, sin = cos_ref[...], sin_ref[...]
    kc = _rope(k_ref[...].astype(F32), cos, sin, lo)
    kp = _rope(kp_ref[...].astype(F32), cosp_ref[...], sinp_ref[...], lo)
    kext = jnp.concatenate([kp, kc], axis=0)
    vext = jnp.concatenate([vp_ref[...].astype(F32), v_ref[...].astype(F32)], axis=0)
    ka, kb = _kv_variants(kext, head_lo)
    va, vb = _kv_variants(vext, head_lo)
    qi = lax.broadcasted_iota(jnp.int32, (1, STACK), 1) % BLOCK
    kj = lax.broadcasted_iota(jnp.int32, (2 * BLOCK, 1), 0)
    valid = (kj > qi) & (kj <= qi + BLOCK) & ((kj >= BLOCK) | (i > 0))
    cos4 = jnp.concatenate([cos] * PAIRS_PER_KV, axis=0)
    sin4 = jnp.concatenate([sin] * PAIRS_PER_KV, axis=0)
    return lo, head_lo, cos, sin, cos4, sin4, ka, kb, va, vb, valid


def _attn_fwd(proj, sinks, cos, sin, s):
    nb = s // BLOCK
    kcol, vcol = ATTN_WIDTH // LANES, ATTN_WIDTH // LANES + 1

    def body(q_ref, k_ref, v_ref, kp_ref, vp_ref, cos_ref, sin_ref, cosp_ref, sinp_ref, sink_ref, o_ref):
        i = pl.program_id(0)
        lo, head_lo, _, _, cs4, sn4, ka, kb, va, vb, valid = _attn_common(
            i, q_ref, k_ref, v_ref, kp_ref, vp_ref, cos_ref, sin_ref, cosp_ref, sinp_ref)
        row = lax.broadcasted_iota(jnp.int32, (16, 1), 0)
        one = jnp.ones((), BF16)
        for j in range(N_KV_HEADS):
            q4 = _stack_pairs(q_ref, j, lambda t: t.astype(F32))
            qp = (_rope(q4, cs4, sn4, lo) * HEAD_DIM ** -0.5).astype(BF16)
            exps = _attn_exps(qp, ka[j], kb[j], valid, _sink_row(sink_ref, j, 0), _sink_row(sink_ref, j, 1))
            outs = []
            for (e, es), vv, mine in zip(exps, (va[j], vb[j]), (head_lo, ~head_lo)):
                ee = jnp.concatenate([e.astype(BF16), jnp.where(row == 0, es, 0.0).astype(BF16)], axis=0)
                tail = jnp.where((row == 0) & ~mine, one, jnp.zeros((), BF16))
                vx = jnp.concatenate([jnp.where(mine, vv, one), tail], axis=0)
                un = _dot(ee, vx, TN)
                outs.append(un / pltpu.roll(un, HEAD_DIM, 1))
            o = jnp.where(head_lo, outs[0], outs[1]).astype(BF16)
            for t in range(PAIRS_PER_KV):
                p = j * PAIRS_PER_KV + t
                o_ref[:, p * LANES:(p + 1) * LANES] = o[t * BLOCK:(t + 1) * BLOCK]

    prev = lambda i: (jnp.maximum(i - 1, 0), 0)
    return pl.pallas_call(
        body, name="attn_fwd", grid=(nb,),
        in_specs=[pl.BlockSpec((BLOCK, ATTN_WIDTH), lambda i: (i, 0)),
                  pl.BlockSpec((BLOCK, LANES), lambda i: (i, kcol)),
                  pl.BlockSpec((BLOCK, LANES), lambda i: (i, vcol)),
                  pl.BlockSpec((BLOCK, LANES), lambda i: (jnp.maximum(i - 1, 0), kcol)),
                  pl.BlockSpec((BLOCK, LANES), lambda i: (jnp.maximum(i - 1, 0), vcol)),
                  pl.BlockSpec((BLOCK, LANES), lambda i: (i, 0)),
                  pl.BlockSpec((BLOCK, LANES), lambda i: (i, 0)),
                  pl.BlockSpec((BLOCK, LANES), prev),
                  pl.BlockSpec((BLOCK, LANES), prev),
                  pl.BlockSpec(memory_space=pltpu.SMEM)],
        out_specs=pl.BlockSpec((BLOCK, ATTN_WIDTH), lambda i: (i, 0)),
        out_shape=jax.ShapeDtypeStruct((s, ATTN_WIDTH), BF16),
        compiler_params=_cp(("arbitrary",), 32),
    )(proj, proj, proj, proj, proj, cos, sin, cos, sin, sinks)


def _attn_bwd(proj, dmix, sinks, cos, sin, s):
    nb = s // BLOCK
    kcol, vcol = ATTN_WIDTH // LANES, ATTN_WIDTH // LANES + 1

    def body(q_ref, k_ref, v_ref, kp_ref, vp_ref, cos_ref, sin_ref, cosp_ref, sinp_ref, sink_ref, do_ref,
             dq_ref, dk_ref, dv_ref, dsink_ref, ck_ref, cv_ref):
        g = pl.program_id(0)
        i = nb - 1 - g

        @pl.when(g == 0)
        def _():
            ck_ref[...] = jnp.zeros_like(ck_ref)
            cv_ref[...] = jnp.zeros_like(cv_ref)
            dsink_ref[...] = jnp.zeros_like(dsink_ref)

        lo, head_lo, cs, sn, cs4, sn4, ka, kb, va, vb, valid = _attn_common(
            i, q_ref, k_ref, v_ref, kp_ref, vp_ref, cos_ref, sin_ref, cosp_ref, sinp_ref)
        lane = lax.broadcasted_iota(jnp.int32, (1, LANES), 1)
        dk_j, dv_j = [], []
        dsink = jnp.zeros((1, LANES), F32)
        for j in range(N_KV_HEADS):
            q4 = _stack_pairs(q_ref, j, lambda t: t.astype(F32))
            qp = (_rope(q4, cs4, sn4, lo) * HEAD_DIM ** -0.5).astype(BF16)
            exps = _attn_exps(qp, ka[j], kb[j], valid, _sink_row(sink_ref, j, 0), _sink_row(sink_ref, j, 1))
            do = _stack_pairs(do_ref, j, lambda t: t)
            dq_r = jnp.zeros((STACK, LANES), F32)
            dkc, dvc = [], []
            for hh, ((e, es), kk, vv) in enumerate(zip(exps, (ka[j], kb[j]), (va[j], vb[j]))):
                inv = 1.0 / (jnp.sum(e, axis=0, keepdims=True) + es)
                pr = e * inv
                dp = _dot(vv, do, NT)
                delta = jnp.sum(pr * dp, axis=0, keepdims=True)
                ds = (pr * (dp - delta)).astype(BF16)
                psd = es * inv * delta
                for t in range(PAIRS_PER_KV):
                    head = 2 * (j * PAIRS_PER_KV + t) + hh
                    dsink = dsink + jnp.where(
                        lane == head, -jnp.sum(psd[:, t * BLOCK:(t + 1) * BLOCK], axis=1, keepdims=True), 0.0)
                dq_r = dq_r + _dot(ds, kk, TN)
                dkc.append(_dot(ds, qp, NN))
                dvc.append(_dot(pr.astype(BF16), do, NN))
            dk_j.append(jnp.where(head_lo, dkc[0], dkc[1]))
            dv_j.append(jnp.where(head_lo, dvc[0], dvc[1]))
            dq = _rope(dq_r * HEAD_DIM ** -0.5, cs4, -sn4, lo).astype(BF16)
            for t in range(PAIRS_PER_KV):
                p = j * PAIRS_PER_KV + t
                dq_ref[:, p * LANES:(p + 1) * LANES] = dq[t * BLOCK:(t + 1) * BLOCK]
        tot_k = [t + pltpu.roll(t, HEAD_DIM, 1) for t in dk_j]
        tot_v = [t + pltpu.roll(t, HEAD_DIM, 1) for t in dv_j]
        dkext = jnp.where(head_lo, tot_k[0], tot_k[1])
        dvext = jnp.where(head_lo, tot_v[0], tot_v[1])
        dk_r = dkext[BLOCK:] + ck_ref[...]
        dk_ref[...] = _rope(dk_r, cs, -sn, lo).astype(BF16)
        dv_ref[...] = (dvext[BLOCK:] + cv_ref[...]).astype(BF16)
        ck_ref[...] = dkext[:BLOCK]
        cv_ref[...] = dvext[:BLOCK]
        dsink_ref[0:1, :] += dsink

    cur = lambda col: (lambda g: (nb - 1 - g, col))
    prv = lambda col: (lambda g: (jnp.maximum(nb - 2 - g, 0), col))
    blk = lambda w, f: pl.BlockSpec((BLOCK, w), f)
    return pl.pallas_call(
        body, name="attn_bwd", grid=(nb,),
        in_specs=[blk(ATTN_WIDTH, cur(0)), blk(LANES, cur(kcol)), blk(LANES, cur(vcol)),
                  blk(LANES, prv(kcol)), blk(LANES, prv(vcol)),
                  blk(LANES, cur(0)), blk(LANES, cur(0)), blk(LANES, prv(0)), blk(LANES, prv(0)),
                  pl.BlockSpec(memory_space=pltpu.SMEM),
                  blk(ATTN_WIDTH, cur(0))],
        out_specs=[blk(ATTN_WIDTH, cur(0)), blk(LANES, cur(0)), blk(LANES, cur(0)),
                   pl.BlockSpec((8, LANES), lambda g: (0, 0))],
        out_shape=[jax.ShapeDtypeStruct((s, ATTN_WIDTH), BF16), jax.ShapeDtypeStruct((s, LANES), BF16),
                   jax.ShapeDtypeStruct((s, LANES), BF16), jax.ShapeDtypeStruct((8, LANES), F32)],
        scratch_shapes=[pltpu.VMEM((BLOCK, LANES), F32), pltpu.VMEM((BLOCK, LANES), F32)],
        compiler_params=_cp(("arbitrary",), 32),
    )(proj, proj, proj, proj, proj, cos, sin, cos, sin, sinks, dmix)


def _causal_conv(x, prev8, w):
    row = lax.broadcasted_iota(jnp.int32, (8, 1), 0)
    r1, r2 = pltpu.roll(x, 1, 0), pltpu.roll(x, 2, 0)
    s1 = jnp.concatenate([jnp.where(row == 0, prev8[7:8], r1[:8]), r1[8:]], axis=0)
    s2 = jnp.concatenate([jnp.where(row == 0, prev8[6:7], jnp.where(row == 1, prev8[7:8], r2[:8])), r2[8:]], axis=0)
    return w[0:1] * s2 + w[1:2] * s1 + w[2:3] * x


def _conv_bwd(dy, x, w, next8):
    r = x.shape[0]
    row = lax.broadcasted_iota(jnp.int32, (8, 1), 0)
    r1, r2 = pltpu.roll(dy, r - 1, 0), pltpu.roll(dy, r - 2, 0)
    n1 = jnp.concatenate([r1[:r - 8], jnp.where(row == 7, next8[0:1], r1[r - 8:])], axis=0)
    n2 = jnp.concatenate([r2[:r - 8], jnp.where(row == 6, next8[0:1], jnp.where(row == 7, next8[1:2], r2[r - 8:]))],
                         axis=0)
    dx = w[2:3] * dy + w[1:2] * n1 + w[0:1] * n2
    dws = [jnp.sum(t * x, axis=0, keepdims=True) for t in (n2, n1, dy)]
    return dx, dws


CONV_COLS = 256


def _convmix_cols(d):
    conv_w = d - ATTN_WIDTH
    base = (ATTN_WIDTH + 2 * KV_WIDTH) // CONV_COLS
    step = conv_w // CONV_COLS
    return base, base + step, base + 2 * step, step


def _convmix_fwd(proj, scw8, s, d):
    gb0, gc0, h0, ncb = _convmix_cols(d)
    tr = _pick(s, 1024, 16)
    ni = s // tr

    def body(gb_ref, gc_ref, h_ref, w_ref, o_ref, carry_ref):
        @pl.when(pl.program_id(1) == 0)
        def _():
            carry_ref[...] = jnp.zeros_like(carry_ref)

        gch = gc_ref[...].astype(F32) * h_ref[...].astype(F32)
        cc = _causal_conv(gch, carry_ref[...], w_ref[...])
        o_ref[...] = (gb_ref[...].astype(F32) * cc).astype(BF16)
        carry_ref[...] = gch[tr - 8:]

    spec = lambda c0: pl.BlockSpec((tr, CONV_COLS), lambda j, i: (i, c0 + j))
    return pl.pallas_call(
        body, name="convmix_fwd", grid=(ncb, ni),
        in_specs=[spec(gb0), spec(gc0), spec(h0), pl.BlockSpec((8, CONV_COLS), lambda j, i: (0, j))],
        out_specs=pl.BlockSpec((tr, CONV_COLS), lambda j, i: (i, j)),
        out_shape=jax.ShapeDtypeStruct((s, d - ATTN_WIDTH), BF16),
        scratch_shapes=[pltpu.VMEM((8, CONV_COLS), F32)],
        compiler_params=_cp(("arbitrary", "arbitrary"), 32),
    )(proj, proj, proj, scw8)


def _convmix_bwd(proj, dmix, scw8, s, d):
    gb0, gc0, h0, ncb = _convmix_cols(d)
    tr = _pick(s, 1024, 16)
    ni = s // tr
    dc0 = ATTN_WIDTH // CONV_COLS

    def body(dc_ref, gb_ref, gc_ref, h_ref, gcp_ref, hp_ref, w_ref, d3_ref, dw_ref, nxt_ref):
        g = pl.program_id(1)
        i = ni - 1 - g

        @pl.when(g == 0)
        def _():
            nxt_ref[...] = jnp.zeros_like(nxt_ref)
            dw_ref[...] = jnp.zeros_like(dw_ref)

        w = w_ref[...]
        gb, gc, h = gb_ref[...].astype(F32), gc_ref[...].astype(F32), h_ref[...].astype(F32)
        gch = gc * h
        prev8 = (gcp_ref[...].astype(F32) * hp_ref[...].astype(F32))[8:16] * (i > 0).astype(F32)
        cc = _causal_conv(gch, prev8, w)
        dc = dc_ref[...].astype(F32)
        dcc = dc * gb
        dgch, dws = _conv_bwd(dcc, gch, w, nxt_ref[...])
        d3_ref[0] = (dc * cc).astype(BF16)
        d3_ref[1] = (dgch * h).astype(BF16)
        d3_ref[2] = (dgch * gc).astype(BF16)
        for t in range(3):
            dw_ref[t:t + 1, :] += dws[t]
        nxt_ref[...] = dcc[0:8]

    cur = lambda c0: pl.BlockSpec((tr, CONV_COLS), lambda j, g: (ni - 1 - g, c0 + j))
    prv = lambda c0: pl.BlockSpec((16, CONV_COLS), lambda j, g: (jnp.maximum((ni - 1 - g) * (tr // 16) - 1, 0), c0 + j))
    return pl.pallas_call(
        body, name="convmix_bwd", grid=(ncb, ni),
        in_specs=[cur(dc0), cur(gb0), cur(gc0), cur(h0), prv(gc0), prv(h0),
                  pl.BlockSpec((8, CONV_COLS), lambda j, g: (0, j))],
        out_specs=[pl.BlockSpec((3, tr, CONV_COLS), lambda j, g: (0, ni - 1 - g, j)),
                   pl.BlockSpec((8, CONV_COLS), lambda j, g: (0, j))],
        out_shape=[jax.ShapeDtypeStruct((3, s, d - ATTN_WIDTH), BF16), jax.ShapeDtypeStruct((8, d - ATTN_WIDTH), F32)],
        scratch_shapes=[pltpu.VMEM((8, CONV_COLS), F32)],
        compiler_params=_cp(("arbitrary", "arbitrary"), 32),
    )(dmix, proj, proj, proj, proj, proj, scw8)


def _ln_fwd(z):
    mu = jnp.mean(z, axis=-1, keepdims=True)
    zc = z - mu
    var = jnp.mean(zc * zc, axis=-1, keepdims=True)
    rstd = lax.rsqrt(var + LN_EPS)
    return zc * rstd, rstd


def _ln_bwd(dout, xh, rstd, g):
    dxh = dout * g
    c1 = jnp.mean(dxh, axis=-1, keepdims=True)
    c2 = jnp.mean(dxh * xh, axis=-1, keepdims=True)
    dz = rstd * (dxh - c1 - xh * c2)
    return dz, jnp.sum(dout * xh, axis=0, keepdims=True), jnp.sum(dout, axis=0, keepdims=True)


def _outproj_ln1(attn, conv, wout, x, g1, b1, s, d):
    tm = _pick(s, 512, 16)
    ka = attn.shape[1]
    one_buffer = pl.Buffered(1)

    def body(a_ref, c_ref, wt_ref, wb_ref, x_ref, g_ref, b_ref, x1_ref, x1b_ref, xh_ref, rs_ref):
        y = _dot(a_ref[...], wt_ref[...], NN) + _dot(c_ref[...], wb_ref[...], NN)
        xh, rstd = _ln_fwd(ALPHA * x_ref[...] + y)
        x1 = xh * g_ref[...] + b_ref[...]
        x1_ref[...] = x1
        x1b_ref[...] = x1.astype(BF16)
        xh_ref[...] = xh.astype(BF16)
        rs_ref[...] = rstd

    row = lambda w: pl.BlockSpec((tm, w), lambda i: (i, 0))
    vec = pl.BlockSpec((1, d), lambda i: (0, 0))
    return pl.pallas_call(
        body, name="outproj_ln1", grid=(s // tm,),
        in_specs=[row(ka), row(d - ka), pl.BlockSpec((ka, d), lambda i: (0, 0), pipeline_mode=one_buffer),
                  pl.BlockSpec((d - ka, d), lambda i: (ka // (d - ka), 0), pipeline_mode=one_buffer), row(d), vec, vec],
        out_specs=[row(d), row(d), row(d), row(1)],
        out_shape=[jax.ShapeDtypeStruct((s, d), F32), jax.ShapeDtypeStruct((s, d), BF16),
                   jax.ShapeDtypeStruct((s, d), BF16), jax.ShapeDtypeStruct((s, 1), F32)],
        compiler_params=_cp(("arbitrary",), 56),
    )(attn, conv, wout, wout, x, g1, b1)


def _ffn_up(x1b, wup, fcw8, s, d, dff):
    tm = _pick(s, 1024, 16)
    tn = _pick(dff, 512, LANES)
    nj, ni = dff // tn, s // tm

    def body(x_ref, wa_ref, wg_ref, ca_ref, cg_ref, u_ref, y_ref, h_ref, carry_ref):
        @pl.when(pl.program_id(1) == 0)
        def _():
            carry_ref[...] = jnp.zeros_like(carry_ref)

        xa = x_ref[...]
        ys = []
        for part, (w_ref, c_ref) in enumerate(((wa_ref, ca_ref), (wg_ref, cg_ref))):
            ub = _dot(xa, w_ref[...], NN).astype(BF16)
            u_ref[part] = ub
            u = ub.astype(F32)
            y = _causal_conv(u, carry_ref[part], c_ref[...])
            carry_ref[part] = u[tm - 8:]
            yb = y.astype(BF16)
            y_ref[part] = yb
            ys.append(yb.astype(F32))
        a2, g2 = ys
        sig = 1.0 / (1.0 + jnp.exp(-a2))
        h_ref[...] = (a2 * sig * g2).astype(BF16)

    return pl.pallas_call(
        body, name="ffn_up", grid=(nj, ni),
        in_specs=[pl.BlockSpec((tm, d), lambda j, i: (i, 0)),
                  pl.BlockSpec((d, tn), lambda j, i: (0, j)),
                  pl.BlockSpec((d, tn), lambda j, i: (0, j + nj)),
                  pl.BlockSpec((8, tn), lambda j, i: (0, j)),
                  pl.BlockSpec((8, tn), lambda j, i: (0, j + nj))],
        out_specs=[pl.BlockSpec((2, tm, tn), lambda j, i: (0, i, j)),
                   pl.BlockSpec((2, tm, tn), lambda j, i: (0, i, j)),
                   pl.BlockSpec((tm, tn), lambda j, i: (i, j))],
        out_shape=[jax.ShapeDtypeStruct((2, s, dff), BF16), jax.ShapeDtypeStruct((2, s, dff), BF16),
                   jax.ShapeDtypeStruct((s, dff), BF16)],
        scratch_shapes=[pltpu.VMEM((2, 8, tn), F32)],
        compiler_params=_cp(("arbitrary", "arbitrary"), 56),
    )(x1b, wup, wup, fcw8, fcw8)


def _ffn_mid_bwd(dz2b, wdown, u3, y3, fcw8, s, d, dff):
    tm = _pick(s, 1024, 16)
    tn = _pick(dff, 512, LANES)
    nj, ni = dff // tn, s // tm

    def body(dz_ref, wd_ref, u_ref, y_ref, ca_ref, cg_ref, du_ref, dw_ref, nxt_ref):
        @pl.when(pl.program_id(1) == 0)
        def _():
            nxt_ref[...] = jnp.zeros_like(nxt_ref)
            dw_ref[...] = jnp.zeros_like(dw_ref)

        a2, g2 = y_ref[0].astype(F32), y_ref[1].astype(F32)
        sig = 1.0 / (1.0 + jnp.exp(-a2))
        silu = a2 * sig
        dhv = _dot(dz_ref[...], wd_ref[...], NT)
        dys = (dhv * g2 * (sig * (1.0 + a2 * (1.0 - sig))), dhv * silu)
        for part, (c_ref, dy) in enumerate(zip((ca_ref, cg_ref), dys)):
            dx, dws = _conv_bwd(dy, u_ref[part].astype(F32), c_ref[...], nxt_ref[part])
            du_ref[part] = dx.astype(BF16)
            for t in range(3):
                dw_ref[part, t:t + 1, :] += dws[t]
            nxt_ref[part] = dy[0:8]

    return pl.pallas_call(
        body, name="ffn_mid_bwd", grid=(nj, ni),
        in_specs=[pl.BlockSpec((tm, d), lambda j, g: (ni - 1 - g, 0)),
                  pl.BlockSpec((tn, d), lambda j, g: (j, 0)),
                  pl.BlockSpec((2, tm, tn), lambda j, g: (0, ni - 1 - g, j)),
                  pl.BlockSpec((2, tm, tn), lambda j, g: (0, ni - 1 - g, j)),
                  pl.BlockSpec((8, tn), lambda j, g: (0, j)),
                  pl.BlockSpec((8, tn), lambda j, g: (0, j + nj))],
        out_specs=[pl.BlockSpec((2, tm, tn), lambda j, g: (0, ni - 1 - g, j)),
                   pl.BlockSpec((2, 8, tn), lambda j, g: (0, 0, j))],
        out_shape=[jax.ShapeDtypeStruct((2, s, dff), BF16), jax.ShapeDtypeStruct((2, 8, dff), F32)],
        scratch_shapes=[pltpu.VMEM((2, 8, tn), F32)],
        compiler_params=_cp(("arbitrary", "arbitrary"), 56),
    )(dz2b, wdown, u3, y3, fcw8, fcw8)


def _ffn_down_loss(hmid, wdown, x1, target, g2, b2, s, d, dff):
    tm = _pick(s, 512, SLAB)
    tk = _pick(dff, 1408, LANES)
    ni, nk = s // tm, dff // tk
    slab = min(SLAB, tm)

    def body(h_ref, w_ref, x1_ref, t_ref, g_ref, b_ref, dzb_ref, st_ref, acc_ref):
        i, kk = pl.program_id(0), pl.program_id(1)

        @pl.when((i == 0) & (kk == 0))
        def _():
            st_ref[...] = jnp.zeros_like(st_ref)

        part = _dot(h_ref[...], w_ref[...], NN)

        @pl.when(kk == 0)
        def _():
            acc_ref[...] = part

        @pl.when(kk > 0)
        def _():
            acc_ref[...] += part

        @pl.when(kk == nk - 1)
        def _():
            g, b = g_ref[...], b_ref[...]

            def one(sl, carry):
                rows = pl.ds(pl.multiple_of(sl * slab, slab), slab)
                xh, rstd = _ln_fwd(ALPHA * x1_ref[rows, :] + acc_ref[rows, :])
                diff = xh * g + b - t_ref[rows, :]
                sq = jnp.sum(jnp.sum(diff * diff, axis=1, keepdims=True), axis=0, keepdims=True)
                dz, dg, db = _ln_bwd(diff * (1.0 / d), xh, rstd, g)
                dzb_ref[rows, :] = dz.astype(BF16)
                st_ref[0:1, :] += dg
                st_ref[1:2, :] += db
                st_ref[2:3, :] += sq
                return carry

            lax.fori_loop(0, tm // slab, one, 0)

    row = pl.BlockSpec((tm, d), lambda i, kk: (i, 0))
    vec = pl.BlockSpec((1, d), lambda i, kk: (0, 0))
    return pl.pallas_call(
        body, name="ffn_down_loss", grid=(ni, nk),
        in_specs=[pl.BlockSpec((tm, tk), lambda i, kk: (i, kk)), pl.BlockSpec((tk, d), lambda i, kk: (kk, 0)),
                  row, row, vec, vec],
        out_specs=[row, pl.BlockSpec((8, d), lambda i, kk: (0, 0))],
        out_shape=[jax.ShapeDtypeStruct((s, d), BF16), jax.ShapeDtypeStruct((8, d), F32)],
        scratch_shapes=[pltpu.VMEM((tm, d), F32)],
        compiler_params=_cp(("arbitrary", "arbitrary"), 48),
    )(hmid, wdown, x1, target, g2, b2)


def _ffn_dx_ln1_bwd(du3, wup, dz2b, xh1, rstd1, g1, s, d, dff):
    tm = _pick(s, 512, SLAB)
    tk = _pick(dff, 2816, MXU_DIM)
    nkh = dff // tk
    ni, nk = s // tm, 2 * nkh
    slab = min(SLAB, tm)

    def body(a_ref, w_ref, dz2_ref, xh_ref, rs_ref, g_ref, dzb_ref, st_ref, acc_ref):
        i, kk = pl.program_id(0), pl.program_id(1)

        @pl.when((i == 0) & (kk == 0))
        def _():
            st_ref[...] = jnp.zeros_like(st_ref)

        part = _dot(a_ref[...], w_ref[...], NT)

        @pl.when(kk == 0)
        def _():
            acc_ref[...] = part

        @pl.when(kk > 0)
        def _():
            acc_ref[...] += part

        @pl.when(kk == nk - 1)
        def _():
            g = g_ref[...]

            def one(sl, carry):
                rows = pl.ds(pl.multiple_of(sl * slab, slab), slab)
                dx1 = ALPHA * dz2_ref[rows, :].astype(F32) + acc_ref[rows, :]
                dz, dg, db = _ln_bwd(dx1, xh_ref[rows, :].astype(F32), rs_ref[rows, :], g)
                dzb_ref[rows, :] = dz.astype(BF16)
                st_ref[0:1, :] += dg
                st_ref[1:2, :] += db
                return carry

            lax.fori_loop(0, tm // slab, one, 0)

    row = pl.BlockSpec((tm, d), lambda i, kk: (i, 0))
    row1 = pl.BlockSpec((tm, d), lambda i, kk: (i, 0), pipeline_mode=pl.Buffered(1))
    return pl.pallas_call(
        body, name="ffn_dx_ln1_bwd", grid=(ni, nk),
        in_specs=[pl.BlockSpec((None, tm, tk), lambda i, kk: (kk // nkh, i, kk % nkh)),
                  pl.BlockSpec((d, tk), lambda i, kk: (0, kk)),
                  row1, row1, pl.BlockSpec((tm, 1), lambda i, kk: (i, 0)), pl.BlockSpec((1, d), lambda i, kk: (0, 0))],
        out_specs=[row, pl.BlockSpec((8, d), lambda i, kk: (0, 0))],
        out_shape=[jax.ShapeDtypeStruct((s, d), BF16), jax.ShapeDtypeStruct((8, d), F32)],
        scratch_shapes=[pltpu.VMEM((tm, d), F32)],
        compiler_params=_cp(("arbitrary", "arbitrary"), 56),
    )(du3, wup, dz2b, xh1, rstd1, g1)


def _phase_mixer(x, win_t, wout, scw8, sinks, ln1_g, ln1_b, after=None):
    s, d = x.shape
    n_in = win_t.shape[0]
    cos, sin = _rope_tables(s)
    proj = _matmul(x, win_t, mode="nt", m=s, n=n_in, k=d, tm=_pick(s, 512, 16), tn=n_in, tk=d, out_dtype=BF16,
                   name="in_proj", vmem_mb=52, after=after,
                   b_spec=pl.BlockSpec((n_in, d), lambda j, i, kk: (0, 0), pipeline_mode=pl.Buffered(1)))
    attn = _attn_fwd(proj, sinks, cos, sin, s)
    conv = _convmix_fwd(proj, scw8, s, d)
    wout = wout(attn) if callable(wout) else wout
    x1, x1b, xh1, rstd1 = _outproj_ln1(attn, conv, wout, x, ln1_g, ln1_b, s, d)
    return dict(x=x, cos=cos, sin=sin, proj=proj, attn=attn, conv=conv, x1=x1, x1b=x1b, xh1=xh1, rstd1=rstd1,
                wout=wout)


def _phase_ffn(a, target, wup, wdown, fcw8, ln2_g, ln2_b):
    x1, x1b = a["x1"], a["x1b"]
    s, d = x1.shape
    dff = wdown.shape[0]
    u3, y3, hmid = _ffn_up(x1b, wup, fcw8, s, d, dff)
    dz2b, st2 = _ffn_down_loss(hmid, wdown, x1, target, ln2_g, ln2_b, s, d, dff)

    tnw = _pick(d, 1024, MXU_DIM)
    keep_b = pl.BlockSpec((s, tnw), lambda j, i, kk: (kk, j), pipeline_mode=pl.Buffered(1))
    g_wdown = _matmul(hmid, dz2b, mode="tn", m=dff, n=d, k=s, tm=_pick(dff, 512, MXU_DIM), tn=tnw, tk=s,
                      out_dtype=BF16, name="grad_w_down", vmem_mb=56, b_spec=keep_b)
    du3, dfcw = _ffn_mid_bwd(dz2b, wdown, u3, y3, fcw8, s, d, dff)
    tnu = _pick(dff, 512, MXU_DIM)
    njh = dff // tnu
    tmu = _pick(d, 1024, LANES)
    g_wup = _matmul(x1b, du3, mode="tn", m=d, n=2 * dff, k=s, tm=tmu, tn=tnu, tk=s, out_dtype=BF16,
                    name="grad_w_up", vmem_mb=58, m_outer=True,
                    a_spec=pl.BlockSpec((s, tmu), lambda j, i, kk: (kk, i), pipeline_mode=pl.Buffered(1)),
                    b_spec=pl.BlockSpec((None, s, tnu), lambda j, i, kk: (j // njh, kk, j % njh)))
    return dict(du3=du3, dz2b=dz2b, st2=st2, dfcw=dfcw, wdown=g_wdown, wup=g_wup)


def _phase_rest(a, f, wup, wout, win_t, scw8, sinks, ln1_g, between=None):
    xb, cos, sin, proj, attn, conv = a["x"], a["cos"], a["sin"], a["proj"], a["attn"], a["conv"]
    du3, dz2b, st2, dfcw = f["du3"], f["dz2b"], f["st2"], f["dfcw"]
    s, d = a["x1"].shape
    dff = wup.shape[1] // 2
    n_in = win_t.shape[0]
    ts = _pick(s, 2048, 16)
    dz1b, st1 = _ffn_dx_ln1_bwd(du3, wup, dz2b, a["xh1"], a["rstd1"], ln1_g, s, d, dff)
    after = between(dz1b) if between is not None else None

    tnw = _pick(d, 1024, MXU_DIM)
    halves = [_matmul(part, dz1b, mode="tn", m=part.shape[1], n=d, k=s, tm=_pick(part.shape[1], 512, LANES), tn=tnw,
                      tk=s, out_dtype=BF16, name="grad_w_out_" + tag, vmem_mb=56, after=after,
                      b_spec=pl.BlockSpec((s, tnw), lambda j, i, kk: (kk, j), pipeline_mode=pl.Buffered(1)))
              for tag, part in (("attn", attn), ("conv", conv))]
    g_wout = jnp.concatenate(halves, axis=0)
    dmix = _matmul(dz1b, wout, mode="nt", m=s, n=d, k=d, tm=_pick(s, 1024, 16), tn=_pick(d, 1024, LANES), tk=d,
                   out_dtype=BF16, name="out_dmix", vmem_mb=48, after=after)
    d3, dscw = _convmix_bwd(proj, dmix, scw8, s, d)
    dq, dk, dv, dsink = _attn_bwd(proj, dmix, sinks, cos, sin, s)
    dproj = jnp.concatenate([dq, dk, dv, d3[0], d3[1], d3[2]], axis=1)
    g_win_t = _matmul(dproj, xb, mode="tn", m=n_in, n=d, k=s, tm=_pick(n_in, 2176, LANES), tn=_pick(d, 512, LANES),
                      tk=ts, out_dtype=BF16, name="grad_w_in", vmem_mb=48)
    small = dict(loss_sq=st2[2, 0], ln2_g=st2[0], ln2_b=st2[1], ln1_g=st1[0], ln1_b=st1[1], sinks=dsink[0, :N_Q_HEADS],
                 fcw=jnp.concatenate([dfcw[0, :3], dfcw[1, :3]], axis=1), scw=dscw[:3])
    return (dproj, dz1b), dict(win_t=g_win_t, wout=g_wout), small


def _grad_x(dproj, dz1b, win_t, after=None):
    s, n_in = dproj.shape
    d = win_t.shape[1]
    return _matmul(dproj, win_t, mode="nn", m=s, n=d, k=n_in, tm=_pick(s, 512, 16), tn=_pick(d, 1024, LANES),
                   tk=n_in, out_dtype=F32, name="grad_x", vmem_mb=56, res=dz1b, alpha=ALPHA, after=after)


def _local_step(x, target, win_t, wout, wup, wdown, scw8, fcw8, sinks, ln1_g, ln1_b, ln2_g, ln2_b):
    a = _phase_mixer(x, win_t, wout, scw8, sinks, ln1_g, ln1_b)
    f = _phase_ffn(a, target, wup, wdown, fcw8, ln2_g, ln2_b)
    (dproj, dz1b), g, small = _phase_rest(a, f, wup, wout, win_t, scw8, sinks, ln1_g)
    return _grad_x(dproj, dz1b, win_t), dict(g, wup=f["wup"], wdown=f["wdown"]), small


W_IN, W_OUT = ("win_t",), ("wout",)
MIXER = W_IN + W_OUT
FFN = ("wup", "wdown")
BIG = MIXER + FFN


def _geom(shard_shapes):
    out = {}
    for name in BIG:
        r, c = shard_shapes[name]
        out[name] = ("col" if name == "wup" else "row", (r, c), (r // 2, c))
    return out


def _full_shape(kind, shard):
    r, c = shard
    return (N_CHIPS * r, c) if kind == "row" else (r, N_CHIPS * c)


def _piece_of(ref, kind, shard, chip, half):
    r, c = shard
    if kind == "row":
        return ref.at[pl.ds(chip * r + half * (r // 2), r // 2), :]
    return ref.at[pl.ds(half * (r // 2), r // 2), pl.ds(chip * c, c)]


def _shard_piece(ref, shard, half):
    r, _ = shard
    return ref.at[pl.ds(half * (r // 2), r // 2), :]


def _me():
    return lax.axis_index("x"), lax.axis_index("y"), lax.axis_index("c")


def _other_chips(x, y):
    return [(1 - x, y), (x, 1 - y), (1 - x, 1 - y)]


def _remote(src, dst, send_sem, recv_sem, dev):
    return pltpu.make_async_remote_copy(src_ref=src, dst_ref=dst, send_sem=send_sem, recv_sem=recv_sem,
                                        device_id=dev, device_id_type=MESH)


def _place_shard(w, chip1, kind, name):
    r, c = w.shape
    tr = _rows_tile(r, c, 16, ADD_TILE)
    nt = r // tr

    def body(chip_ref, w_ref, o_ref):
        o_ref[...] = w_ref[...].astype(BF16)

    out_map = (lambda i, chip_ref: (chip_ref[0] * nt + i, 0)) if kind == "row" else (lambda i, chip_ref: (i, chip_ref[0]))
    return pl.pallas_call(
        body, name="place_" + name,
        grid_spec=pltpu.PrefetchScalarGridSpec(
            num_scalar_prefetch=1, grid=(nt,),
            in_specs=[pl.BlockSpec((tr, c), lambda i, chip_ref: (i, 0))],
            out_specs=pl.BlockSpec((tr, c), out_map)),
        out_shape=jax.ShapeDtypeStruct(_full_shape(kind, (r, c)), BF16),
        compiler_params=_cp(("arbitrary",), 32),
    )(chip1, w)


def _allgather_weights(names, placed, geom, small_shards):
    nb, ns = len(names), len(small_shards)
    small_w = [a.shape[1] for a in small_shards]

    def body(*refs):
        sm = refs[nb:nb + ns]
        full = refs[nb + ns:2 * nb + ns]
        smf = refs[2 * nb + ns:2 * nb + 2 * ns]
        send, recv, loc = refs[2 * nb + 2 * ns:]
        x, y, c = _me()
        chip = 2 * x + y
        sib = (x, y, 1 - c)
        others = _other_chips(x, y)
        locals_, sends = [], []
        for m, name in enumerate(names):
            kind, shard, _ = geom[name]
            mine = _piece_of(full[m], kind, shard, chip, c)
            for k, (qx, qy) in enumerate(others):
                cp = _remote(mine, mine, send.at[6 * m + k], recv.at[6 * m + k], (qx, qy, c))
                cp.start()
                sends.append(cp)
        for t in range(ns):
            cp = pltpu.make_async_copy(sm[t], smf[t].at[:, pl.ds(chip * small_w[t], small_w[t])], loc.at[t])
            cp.start()
            locals_.append(cp)
            for k, (qx, qy) in enumerate(others):
                cp = _remote(sm[t], smf[t].at[:, pl.ds(chip * small_w[t], small_w[t])],
                             send.at[6 * nb + 3 * t + k], recv.at[6 * nb + 3 * t + k], (qx, qy, c))
                cp.start()
                sends.append(cp)
        for m, name in enumerate(names):
            kind, shard, _ = geom[name]
            for k, (qx, qy) in enumerate(others):
                got = _piece_of(full[m], kind, shard, 2 * qx + qy, c)
                _remote(got, got, send.at[6 * m + k], recv.at[6 * m + k], (qx, qy, c)).wait_recv()
                cp = _remote(got, got, send.at[6 * m + 3 + k], recv.at[6 * m + 3 + k], sib)
                cp.start()
                sends.append(cp)
        for t in range(ns):
            for k, (qx, qy) in enumerate(others):
                got = smf[t].at[:, pl.ds((2 * qx + qy) * small_w[t], small_w[t])]
                _remote(got, got, send.at[6 * nb + 3 * t + k], recv.at[6 * nb + 3 * t + k], (qx, qy, c)).wait_recv()
        for m, name in enumerate(names):
            kind, shard, _ = geom[name]
            for k, (qx, qy) in enumerate(others):
                got = _piece_of(full[m], kind, shard, 2 * qx + qy, 1 - c)
                _remote(got, got, send.at[6 * m + 3 + k], recv.at[6 * m + 3 + k], sib).wait_recv()
        for cp in sends:
            cp.wait_send()
        for cp in locals_:
            cp.wait()

    nsem = 6 * nb + 3 * ns
    out_shape = [jax.ShapeDtypeStruct(placed[n].shape, BF16) for n in names]
    out_shape += [jax.ShapeDtypeStruct((8, N_CHIPS * w), F32) for w in small_w]
    outs = pl.pallas_call(
        body, name="allgather_weights", in_specs=[ANY] * (nb + ns), out_specs=[ANY] * (nb + ns), out_shape=out_shape,
        input_output_aliases={m: m for m in range(nb)},
        scratch_shapes=[pltpu.SemaphoreType.DMA((nsem,)), pltpu.SemaphoreType.DMA((nsem,)),
                        pltpu.SemaphoreType.DMA((ns,))],
    )(*[placed[n] for n in names], *small_shards)
    return dict(zip(names, outs[:nb])), list(outs[nb:])


def _sibling_exchange(names, grads, geom):
    nb = len(names)

    def body(*refs):
        g = refs[:nb]
        got = refs[nb:2 * nb]
        send, recv = refs[2 * nb:]
        x, y, c = _me()
        sib = (x, y, 1 - c)
        cps = []
        for m, name in enumerate(names):
            kind, shard, _ = geom[name]
            for r in range(N_CHIPS):
                cp = _remote(_piece_of(g[m], kind, shard, r, 1 - c), got[m].at[r],
                             send.at[N_CHIPS * m + r], recv.at[N_CHIPS * m + r], sib)
                cp.start()
                cps.append(cp)
        for cp in cps:
            cp.wait_recv()
        for cp in cps:
            cp.wait_send()

    return pl.pallas_call(
        body, name="grad_sibling_exchange_" + names[0], in_specs=[ANY] * nb, out_specs=[ANY] * nb,
        out_shape=[jax.ShapeDtypeStruct((N_CHIPS,) + geom[n][2], BF16) for n in names],
        scratch_shapes=[pltpu.SemaphoreType.DMA((N_CHIPS * nb,)), pltpu.SemaphoreType.DMA((N_CHIPS * nb,))],
    )(*[grads[n] for n in names])


def _sibling_assemble(names, shards, geom):
    nb = len(names)

    def body(*refs):
        full = refs[nb:2 * nb]
        send, recv = refs[2 * nb:]
        x, y, c = _me()
        sib = (x, y, 1 - c)
        cps = []
        for m, name in enumerate(names):
            mine = _shard_piece(full[m], geom[name][1], c)
            cp = _remote(mine, mine, send.at[m], recv.at[m], sib)
            cp.start()
            cps.append(cp)
        for m, name in enumerate(names):
            theirs = _shard_piece(full[m], geom[name][1], 1 - c)
            _remote(theirs, theirs, send.at[m], recv.at[m], sib).wait_recv()
        for cp in cps:
            cp.wait_send()

    return pl.pallas_call(
        body, name="grad_sibling_assemble_" + names[0], in_specs=[ANY] * nb, out_specs=[ANY] * nb,
        out_shape=[jax.ShapeDtypeStruct(geom[n][1], F32) for n in names],
        input_output_aliases={m: m for m in range(nb)},
        scratch_shapes=[pltpu.SemaphoreType.DMA((nb,)), pltpu.SemaphoreType.DMA((nb,))],
    )(*shards)


HBM = pl.BlockSpec(memory_space=pltpu.HBM)
SEM = pl.BlockSpec(memory_space=pltpu.SEMAPHORE)
EFFECT = pltpu.SideEffectType.DATAFLOW_SIDE_EFFECTING
TOKEN = jax.ShapeDtypeStruct((8, LANES), F32)


def _hbm(a):
    return pltpu.with_memory_space_constraint(a, pltpu.HBM)


def _gather_copies(names, full, geom, send, recv):
    x, y, c = _me()
    out = []
    for m, name in enumerate(names):
        kind, shard, _ = geom[name]
        mine = _piece_of(full[m], kind, shard, 2 * x + y, c)
        for k, (qx, qy) in enumerate(_other_chips(x, y)):
            theirs = _piece_of(full[m], kind, shard, 2 * qx + qy, c)
            out.append((_remote(mine, mine, send.at[3 * m + k], recv.at[3 * m + k], (qx, qy, c)),
                        _remote(theirs, theirs, send.at[3 * m + k], recv.at[3 * m + k], (qx, qy, c))))
    return out


def _gather_start(names, placed, geom, after):
    nb = len(names)

    def body(*refs):
        full = refs[:nb]
        send, recv = refs[nb + 1], refs[nb + 2]
        token = refs[2 * nb + 3]
        for cp, _ in _gather_copies(names, full, geom, send, recv):
            cp.start()
        token[...] = jnp.zeros_like(token)

    outs = pl.pallas_call(
        body, name="gather_start_" + names[0],
        out_shape=(pltpu.SemaphoreType.DMA((3 * nb,)), pltpu.SemaphoreType.DMA((3 * nb,)),
                   *[pltpu.HBM(placed[n].shape, BF16) for n in names], TOKEN),
        in_specs=[HBM] * nb + [ANY], out_specs=(SEM, SEM, *[HBM] * nb, pl.BlockSpec(memory_space=pltpu.VMEM)),
        input_output_aliases={m: 2 + m for m in range(nb)},
        compiler_params=pltpu.CompilerParams(has_side_effects=EFFECT),
    )(*[_hbm(placed[n]) for n in names], after)
    return outs[0], outs[1], list(outs[2:2 + nb]), outs[2 + nb]


def _gather_wait(names, send, recv, thru, geom, after):
    nb = len(names)

    def body(*refs):
        full = refs[:nb]
        for mine, theirs in _gather_copies(names, full, geom, refs[nb], refs[nb + 1]):
            mine.wait_send()
            theirs.wait_recv()

    return pl.pallas_call(
        body, name="gather_wait_" + names[0], out_shape=tuple(pltpu.HBM(t.shape, t.dtype) for t in thru),
        in_specs=[HBM] * nb + [SEM, SEM, ANY], out_specs=tuple([HBM] * nb),
        input_output_aliases={m: m for m in range(nb)},
        compiler_params=pltpu.CompilerParams(has_side_effects=EFFECT),
    )(*thru, send, recv, after)


def _gather_forward(names, full, geom):
    nb = len(names)

    def body(*refs):
        arr = refs[nb:2 * nb]
        send, recv = refs[2 * nb:]
        x, y, c = _me()
        sib = (x, y, 1 - c)
        cps = []
        for m, name in enumerate(names):
            kind, shard, _ = geom[name]
            for k, (qx, qy) in enumerate(_other_chips(x, y)):
                got = _piece_of(arr[m], kind, shard, 2 * qx + qy, c)
                cp = _remote(got, got, send.at[3 * m + k], recv.at[3 * m + k], sib)
                cp.start()
                cps.append(cp)
        for m, name in enumerate(names):
            kind, shard, _ = geom[name]
            for k, (qx, qy) in enumerate(_other_chips(x, y)):
                theirs = _piece_of(arr[m], kind, shard, 2 * qx + qy, 1 - c)
                _remote(theirs, theirs, send.at[3 * m + k], recv.at[3 * m + k], sib).wait_recv()
        for cp in cps:
            cp.wait_send()

    return pl.pallas_call(
        body, name="gather_forward_" + names[0], in_specs=[ANY] * nb, out_specs=[ANY] * nb,
        out_shape=[jax.ShapeDtypeStruct(a.shape, a.dtype) for a in full],
        input_output_aliases={m: m for m in range(nb)},
        scratch_shapes=[pltpu.SemaphoreType.DMA((3 * nb,)), pltpu.SemaphoreType.DMA((3 * nb,))],
    )(*full)


def _scatter_copies(nb, t, got, send, recv):
    x, y, c = _me()
    return [_remote(t[m].at[2 * qx + qy], got[m].at[k], send.at[3 * m + k], recv.at[3 * m + k], (qx, qy, c))
            for m in range(nb) for k, (qx, qy) in enumerate(_other_chips(x, y))]


def _chip_exchange_start(names, chip_sums, geom, after):
    nb = len(names)
    lands = [lax.empty((N_CHIPS - 1,) + geom[n][2], BF16) for n in names]

    def body(*refs):
        t, got = refs[:nb], refs[nb:2 * nb]
        send, recv = refs[2 * nb + 1], refs[2 * nb + 2]
        token = refs[4 * nb + 3]
        for cp in _scatter_copies(nb, t, got, send, recv):
            cp.start()
        token[...] = jnp.zeros_like(token)

    both = list(chip_sums) + lands
    outs = pl.pallas_call(
        body, name="grad_chip_start_" + names[0],
        out_shape=(pltpu.SemaphoreType.DMA((3 * nb,)), pltpu.SemaphoreType.DMA((3 * nb,)),
                   *[pltpu.HBM(a.shape, a.dtype) for a in both], TOKEN),
        in_specs=[HBM] * (2 * nb) + [ANY],
        out_specs=(SEM, SEM, *[HBM] * (2 * nb), pl.BlockSpec(memory_space=pltpu.VMEM)),
        input_output_aliases={m: 2 + m for m in range(2 * nb)},
        compiler_params=pltpu.CompilerParams(has_side_effects=EFFECT),
    )(*[_hbm(a) for a in both], after)
    return outs[0], outs[1], list(outs[2:2 + 2 * nb]), outs[2 + 2 * nb]


def _chip_exchange_wait(names, send, recv, thru, after):
    nb = len(names)

    def body(*refs):
        for cp in _scatter_copies(nb, refs[:nb], refs[nb:2 * nb], refs[2 * nb], refs[2 * nb + 1]):
            cp.wait_send()
            cp.wait_recv()

    outs = pl.pallas_call(
        body, name="grad_chip_wait_" + names[0], out_shape=tuple(pltpu.HBM(t.shape, t.dtype) for t in thru),
        in_specs=[HBM] * (2 * nb) + [SEM, SEM, ANY], out_specs=tuple([HBM] * (2 * nb)),
        input_output_aliases={m: m for m in range(2 * nb)},
        compiler_params=pltpu.CompilerParams(has_side_effects=EFFECT),
    )(*thru, send, recv, after)
    return list(outs[:nb]), list(outs[nb:])


def _sibling_copies(names, g, got, geom, send, recv):
    x, y, c = _me()
    out = []
    for m, name in enumerate(names):
        kind, shard, _ = geom[name]
        for r in range(N_CHIPS):
            out.append(_remote(_piece_of(g[m], kind, shard, r, 1 - c), got[m].at[r],
                               send.at[N_CHIPS * m + r], recv.at[N_CHIPS * m + r], (x, y, 1 - c)))
    return out


def _sibling_exchange_start(names, grads, geom, after):
    nb = len(names)
    lands = [lax.empty((N_CHIPS,) + geom[n][2], BF16) for n in names]

    def body(*refs):
        for cp in _sibling_copies(names, refs[:nb], refs[nb:2 * nb], geom, refs[2 * nb + 1], refs[2 * nb + 2]):
            cp.start()
        token = refs[4 * nb + 3]
        token[...] = jnp.zeros_like(token)

    both = [grads[n] for n in names] + lands
    outs = pl.pallas_call(
        body, name="grad_sibling_start_" + names[0],
        out_shape=(pltpu.SemaphoreType.DMA((N_CHIPS * nb,)), pltpu.SemaphoreType.DMA((N_CHIPS * nb,)),
                   *[pltpu.HBM(a.shape, a.dtype) for a in both], TOKEN),
        in_specs=[HBM] * (2 * nb) + [ANY],
        out_specs=(SEM, SEM, *[HBM] * (2 * nb), pl.BlockSpec(memory_space=pltpu.VMEM)),
        input_output_aliases={m: 2 + m for m in range(2 * nb)},
        compiler_params=pltpu.CompilerParams(has_side_effects=EFFECT),
    )(*[_hbm(a) for a in both], after)
    return outs[0], outs[1], list(outs[2:2 + 2 * nb]), outs[2 + 2 * nb]


def _sibling_exchange_wait(names, send, recv, thru, geom, after):
    nb = len(names)

    def body(*refs):
        for cp in _sibling_copies(names, refs[:nb], refs[nb:2 * nb], geom, refs[2 * nb], refs[2 * nb + 1]):
            cp.wait_send()
            cp.wait_recv()

    outs = pl.pallas_call(
        body, name="grad_sibling_wait_" + names[0], out_shape=tuple(pltpu.HBM(t.shape, t.dtype) for t in thru),
        in_specs=[HBM] * (2 * nb) + [SEM, SEM, ANY], out_specs=tuple([HBM] * (2 * nb)),
        input_output_aliases={m: m for m in range(2 * nb)},
        compiler_params=pltpu.CompilerParams(has_side_effects=EFFECT),
    )(*thru, send, recv, after)
    return list(outs[:nb]), list(outs[nb:])


def _allreduce_small(part):
    rows = part.shape[0]
    flips = [(a, b, e) for a in (0, 1) for b in (0, 1) for e in (0, 1) if (a, b, e) != (0, 0, 0)]

    def body(p_ref, o_ref, all_ref, send, recv):
        x, y, c = _me()
        me = 4 * x + 2 * y + c
        all_ref[me] = p_ref[...]
        cps = []
        for k, (a, b, e) in enumerate(flips):
            cp = _remote(p_ref, all_ref.at[me], send.at[k], recv.at[k], (x ^ a, y ^ b, c ^ e))
            cp.start()
            cps.append(cp)
        for k, (a, b, e) in enumerate(flips):
            peer = 4 * (x ^ a) + 2 * (y ^ b) + (c ^ e)
            _remote(p_ref, all_ref.at[peer], send.at[k], recv.at[k], (x ^ a, y ^ b, c ^ e)).wait_recv()
        for cp in cps:
            cp.wait_send()
        tot = all_ref[0]
        for dev in range(1, 8):
            tot = tot + all_ref[dev]
        o_ref[...] = tot

    vm = pl.BlockSpec(memory_space=pltpu.VMEM)
    return pl.pallas_call(
        body, name="allreduce_small", in_specs=[vm], out_specs=vm, out_shape=jax.ShapeDtypeStruct((rows, LANES), F32),
        scratch_shapes=[pltpu.VMEM((8, rows, LANES), F32), pltpu.SemaphoreType.DMA((7,)), pltpu.SemaphoreType.DMA((7,))],
    )(part)


def _rows_tile(rows, cols, mult, elems=1 << 19):
    return _pick(rows, max(mult, elems // cols // mult * mult), mult)


ADD_TILE = 1 << 20


def _add_pairs(g, got, kind, shard, where, name):
    p, r, c = got.shape
    tr = _rows_tile(r, c, 16, ADD_TILE)
    nt = r // tr

    def body(w_ref, a_ref, b_ref, o_ref):
        o_ref[...] = (a_ref[...].astype(F32) + b_ref[...].astype(F32)).astype(BF16)

    if kind == "row":
        g_map = lambda q, i, w_ref: ((2 * q + w_ref[1]) * nt + i, 0)
    else:
        g_map = lambda q, i, w_ref: (w_ref[1] * nt + i, q)
    spec = pl.BlockSpec((None, tr, c), lambda q, i, w_ref: (q, i, 0))
    return pl.pallas_call(
        body, name="grad_add_sibling_" + name,
        grid_spec=pltpu.PrefetchScalarGridSpec(
            num_scalar_prefetch=1, grid=(p, nt), in_specs=[pl.BlockSpec((tr, c), g_map), spec], out_specs=spec),
        out_shape=jax.ShapeDtypeStruct((p, r, c), BF16), compiler_params=_cp(("arbitrary", "arbitrary"), 32),
    )(where, g, got)


def _add_four(t, got, shard, where, name):
    _, r, c = t.shape
    tr = _rows_tile(r, c, 16, ADD_TILE)
    nt = r // tr

    def body(w_ref, own, t0, t1, t2, o_ref):
        o_ref[...] = ((own[...].astype(F32) + t0[...].astype(F32)) + t1[...].astype(F32)) + t2[...].astype(F32)

    spec = lambda q: pl.BlockSpec((None, tr, c), lambda i, w_ref: (q, i, 0))
    return pl.pallas_call(
        body, name="grad_add_chips_" + name,
        grid_spec=pltpu.PrefetchScalarGridSpec(
            num_scalar_prefetch=1, grid=(nt,),
            in_specs=[pl.BlockSpec((None, tr, c), lambda i, w_ref: (w_ref[0], i, 0)), spec(0), spec(1), spec(2)],
            out_specs=pl.BlockSpec((tr, c), lambda i, w_ref: (w_ref[1] * nt + i, 0))),
        out_shape=jax.ShapeDtypeStruct(shard, F32), compiler_params=_cp(("arbitrary",), 48),
    )(where, t, got, got, got)


def _adamw(w, g, m, v, name):
    r, c = w.shape
    tr = _rows_tile(r, c, 8)

    def body(w_ref, g_ref, m_ref, v_ref, go_ref, d_ref, mo_ref, vo_ref):
        gv = g_ref[...]
        mn = ADAM_B1 * m_ref[...] + (1.0 - ADAM_B1) * gv
        vn = ADAM_B2 * v_ref[...] + (1.0 - ADAM_B2) * (gv * gv)
        m_hat = mn / (1.0 - ADAM_B1 ** ADAM_STEP)
        v_hat = vn / (1.0 - ADAM_B2 ** ADAM_STEP)
        go_ref[...] = gv
        d_ref[...] = -ADAM_LR * (m_hat / (jnp.sqrt(v_hat) + ADAM_EPS) + ADAM_WD * w_ref[...])
        mo_ref[...] = mn
        vo_ref[...] = vn

    spec = pl.BlockSpec((tr, c), lambda i: (i, 0))
    return pl.pallas_call(
        body, name=name, grid=(r // tr,), in_specs=[spec] * 4, out_specs=[spec] * 4,
        out_shape=[jax.ShapeDtypeStruct((r, c), F32)] * 4, compiler_params=_cp(("arbitrary",), 32),
    )(w, g, m, v)


def _pack(vectors, rows):
    flat = jnp.concatenate([v.reshape(-1).astype(F32) for v in vectors])
    return jnp.pad(flat, (0, rows * LANES - flat.shape[0])).reshape(rows, LANES)


def _unpack(packed, shapes):
    flat = packed.reshape(-1)
    out, off = [], 0
    for shp in shapes:
        n = 1
        for t in shp:
            n *= t
        out.append(flat[off:off + n].reshape(shp))
        off += n
    return out


def _rows_for(shapes):
    n = sum(functools.reduce(lambda a, b: a * b, shp, 1) for shp in shapes)
    return -(-n // (8 * LANES)) * 8


def kernel(x, w_in, attn_sinks, short_conv_w, w_out, ln1_g, ln1_b, ffn_w_up, ffn_conv_w, ffn_w_down, ln2_g, ln2_b, loss_target, m_w_in, m_attn_sinks, m_short_conv_w, m_w_out, m_ln1_g, m_ln1_b, m_ffn_w_up, m_ffn_conv_w, m_ffn_w_down, m_ln2_g, m_ln2_b, v_w_in, v_attn_sinks, v_short_conv_w, v_w_out, v_ln1_g, v_ln1_b, v_ffn_w_up, v_ffn_conv_w, v_ffn_w_down, v_ln2_g, v_ln2_b):
    xs, tgt = x[0], loss_target[0]
    d = xs.shape[1]
    chip = 2 * lax.axis_index("x") + lax.axis_index("y")

    w_big = dict(win_t=w_in[0], wout=w_out[0], wup=ffn_w_up[0], wdown=ffn_w_down[0])
    m_big = dict(win_t=m_w_in[0], wout=m_w_out[0], wup=m_ffn_w_up[0], wdown=m_ffn_w_down[0])
    v_big = dict(win_t=v_w_in[0], wout=v_w_out[0], wup=v_ffn_w_up[0], wdown=v_ffn_w_down[0])
    to_place = dict(w_big, win_t=w_in[0].T)
    geom = _geom({n: to_place[n].shape for n in BIG})
    pad8 = lambda a: jnp.pad(a[0], ((0, 5), (0, 0)))
    where = jnp.stack([chip, lax.axis_index("c")]).astype(jnp.int32)
    placed = {n: _place_shard(to_place[n], where[:1], geom[n][0], n) for n in BIG}
    full, (scw8, fcw8) = _allgather_weights(W_IN, placed, geom, [pad8(short_conv_w), pad8(ffn_conv_w)])
    o_send, o_recv, o_thru, o_token = _gather_start(W_OUT, placed, geom, scw8)
    send, recv, thru, token = _gather_start(FFN, placed, geom, o_token)

    def wout_behind(attn):
        return _gather_forward(W_OUT, _gather_wait(W_OUT, o_send, o_recv, o_thru, geom, attn), geom)[0]

    a = _phase_mixer(xs, full["win_t"], wout_behind, scw8, attn_sinks, ln1_g, ln1_b, after=token)
    full["wout"] = a["wout"]
    landed = _gather_forward(FFN, _gather_wait(FFN, send, recv, thru, geom, a["x1b"]), geom)
    full.update(zip(FFN, landed))
    f = _phase_ffn(a, tgt, full["wup"], full["wdown"], fcw8, ln2_g, ln2_b)

    def add_pairs(names, grads, from_sibling):
        return [_add_pairs(grads[m], from_sibling[m], geom[n][0], geom[n][1], where, n) for m, n in enumerate(names)]

    sib_send, sib_recv, sib_thru, sib_token = _sibling_exchange_start(FFN, f, geom, f["st2"])
    started = {}

    def between(dz1b):
        grads, from_sibling = _sibling_exchange_wait(FFN, sib_send, sib_recv, sib_thru, geom, dz1b)
        started["sums"] = add_pairs(FFN, grads, from_sibling)
        started["chip"] = _chip_exchange_start(FFN, started["sums"], geom, f["st2"])
        return started["chip"][3]

    (dproj, dz1b), g_mixer, g_small = _phase_rest(a, f, full["wup"], full["wout"], full["win_t"], scw8, attn_sinks,
                                                  ln1_g + sib_token[0:1, 0:1], between=between)
    send, recv, thru, _ = started["chip"]

    def finish(names, exchanged):
        sums, from_chips = exchanged
        halves = [_add_four(sums[m], from_chips[m], geom[n][1], where, n) for m, n in enumerate(names)]
        shards = _sibling_assemble(names, halves, geom)
        grads = {n: shards[m].T if n == "win_t" else shards[m] for m, n in enumerate(names)}
        return {n: _adamw(w_big[n], grads[n], m_big[n], v_big[n], "adamw_" + n) for n in names}

    mixer_sums = add_pairs(MIXER, [g_mixer[n] for n in MIXER], _sibling_exchange(MIXER, g_mixer, geom))
    send2, recv2, thru2, token2 = _chip_exchange_start(MIXER, mixer_sums, geom, f["st2"])
    grad_x = _grad_x(dproj, dz1b, full["win_t"], after=token2)
    upd = finish(FFN, _chip_exchange_wait(FFN, send, recv, thru, grad_x))
    upd.update(finish(MIXER, _chip_exchange_wait(MIXER, send2, recv2, thru2, upd[FFN[0]][1])))

    small_names = ("ln1_g", "ln1_b", "ln2_g", "ln2_b", "sinks", "fcw", "scw")
    small_shapes = [g_small[n].shape for n in small_names]
    red = _allreduce_small(_pack([g_small["loss_sq"].reshape(1)] + [g_small[n] for n in small_names],
                                 _rows_for([(1,)] + small_shapes)))
    loss_sq, *gs = _unpack(red, [(1,)] + small_shapes)
    gs = dict(zip(small_names, gs))
    loss = (0.5 / d) * loss_sq[0]
    fw, sw = ffn_conv_w.shape[2], short_conv_w.shape[2]
    gs["fcw"] = lax.dynamic_slice_in_dim(gs["fcw"], chip * fw, fw, axis=1)
    gs["scw"] = lax.dynamic_slice_in_dim(gs["scw"], chip * sw, sw, axis=1)

    sm_w = dict(ln1_g=ln1_g[0], ln1_b=ln1_b[0], ln2_g=ln2_g[0], ln2_b=ln2_b[0], sinks=attn_sinks[0],
                fcw=ffn_conv_w[0], scw=short_conv_w[0])
    sm_m = dict(ln1_g=m_ln1_g[0], ln1_b=m_ln1_b[0], ln2_g=m_ln2_g[0], ln2_b=m_ln2_b[0], sinks=m_attn_sinks[0],
                fcw=m_ffn_conv_w[0], scw=m_short_conv_w[0])
    sm_v = dict(ln1_g=v_ln1_g[0], ln1_b=v_ln1_b[0], ln2_g=v_ln2_g[0], ln2_b=v_ln2_b[0], sinks=v_attn_sinks[0],
                fcw=v_ffn_conv_w[0], scw=v_short_conv_w[0])
    shapes = [sm_w[n].shape for n in small_names]
    rows = _rows_for(shapes)
    packed = [_pack([t[n] for n in small_names], rows) for t in (sm_w, gs, sm_m, sm_v)]
    sm_out = [dict(zip(small_names, _unpack(a, shapes))) for a in _adamw(*packed, "adamw_small")]

    def leaf(kind, name):
        if name in ("w_in", "w_out", "ffn_w_up", "ffn_w_down"):
            key = dict(w_in="win_t", w_out="wout", ffn_w_up="wup", ffn_w_down="wdown")[name]
            return upd[key][kind][None]
        key = dict(attn_sinks="sinks", short_conv_w="scw", ffn_conv_w="fcw").get(name, name)
        return sm_out[kind][key][None]

    order = ("w_in", "attn_sinks", "short_conv_w", "w_out", "ln1_g", "ln1_b", "ffn_w_up", "ffn_conv_w", "ffn_w_down",
             "ln2_g", "ln2_b")
    outs = [loss, grad_x[None]]
    for kind in range(4):
        outs += [leaf(kind, n) for n in order]
    return tuple(outs)
```

```python
import functools

import jax
import jax.numpy as jnp
from jax import lax
from jax.experimental import pallas as pl
from jax.experimental.pallas import tpu as pltpu

F32 = jnp.float32
BF16 = jnp.bfloat16
MESH = pl.DeviceIdType.MESH
ANY = pl.BlockSpec(memory_space=pl.ANY)

HEAD_DIM = 64
N_Q_HEADS = 16
N_KV_HEADS = 2
ATTN_WIDTH = N_Q_HEADS * HEAD_DIM
KV_WIDTH = N_KV_HEADS * HEAD_DIM
BLOCK = 128
ROPE_THETA = 10000.0
LN_EPS = 1e-5
ALPHA = 2.0 ** 0.25
NEG_INF = -1e30
ADAM_LR, ADAM_B1, ADAM_B2, ADAM_EPS, ADAM_WD, ADAM_STEP = 0.001, 0.9, 0.999, 1e-08, 0.01, 10
N_CHIPS = 4
LANES = 128
MXU_DIM = 256
SLAB = 128


def _cp(sem, vmem_mb):
    return pltpu.CompilerParams(dimension_semantics=sem, vmem_limit_bytes=vmem_mb << 20)


def _matmul(a, b, *, mode, m, n, k, tm, tn, tk, out_dtype, name, vmem_mb, a_spec=None, b_spec=None,
            res=None, alpha=1.0, after=None, m_outer=False):
    nj, ni, nk = n // tn, m // tm, k // tk
    assert nj * tn == n and ni * tm == m and nk * tk == k, (name, m, n, k, tm, tn, tk)
    if mode == "nn":
        dims = ((1,), (0,))
        a_spec = a_spec or pl.BlockSpec((tm, tk), lambda j, i, kk: (i, kk))
        b_spec = b_spec or pl.BlockSpec((tk, tn), lambda j, i, kk: (kk, j))
    elif mode == "nt":
        dims = ((1,), (1,))
        a_spec = a_spec or pl.BlockSpec((tm, tk), lambda j, i, kk: (i, kk))
        b_spec = b_spec or pl.BlockSpec((tn, tk), lambda j, i, kk: (j, kk))
    else:
        dims = ((0,), (0,))
        a_spec = a_spec or pl.BlockSpec((tk, tm), lambda j, i, kk: (kk, i))
        b_spec = b_spec or pl.BlockSpec((tk, tn), lambda j, i, kk: (kk, j))
    has_res = res is not None
    has_after = after is not None

    def body(*refs):
        refs = refs[1:] if has_after else refs
        a_ref, b_ref = refs[0], refs[1]
        res_ref = refs[2] if has_res else None
        o_ref = refs[2 + has_res]
        part = lax.dot_general(a_ref[...].astype(BF16), b_ref[...].astype(BF16), (dims, ((), ())),
                               preferred_element_type=F32)

        def finish(acc):
            if has_res:
                acc = acc + alpha * res_ref[...].astype(F32)
            o_ref[...] = acc.astype(o_ref.dtype)

        if nk == 1:
            finish(part)
        else:
            acc_ref = refs[3 + has_res]
            kk = pl.program_id(2)

            @pl.when(kk == 0)
            def _():
                acc_ref[...] = part

            @pl.when(kk > 0)
            def _():
                acc_ref[...] += part

            @pl.when(kk == nk - 1)
            def _():
                finish(acc_ref[...])

    in_specs = [a_spec, b_spec]
    args = [a, b]
    if has_res:
        in_specs.append(pl.BlockSpec((tm, tn), lambda j, i, kk: (i, j)))
        args.append(res)
    if has_after:
        in_specs.insert(0, pl.BlockSpec(after.shape, lambda j, i, kk: (0, 0)))
        args.insert(0, after)
    out_spec = pl.BlockSpec((tm, tn), lambda j, i, kk: (i, j))
    grid = (nj, ni, nk)
    if m_outer:
        swap = lambda sp: pl.BlockSpec(sp.block_shape, (lambda f: lambda i, j, kk: f(j, i, kk))(sp.index_map),
                                       pipeline_mode=sp.pipeline_mode)
        in_specs, out_spec, grid = [swap(sp) for sp in in_specs], swap(out_spec), (ni, nj, nk)
    return pl.pallas_call(
        body, name=name, grid=grid, in_specs=in_specs,
        out_specs=out_spec,
        out_shape=jax.ShapeDtypeStruct((m, n), out_dtype),
        scratch_shapes=[pltpu.VMEM((tm, tn), F32)] if nk > 1 else [],
        compiler_params=_cp(("arbitrary", "arbitrary", "arbitrary"), vmem_mb),
    )(*args)


def _pick(total, want, mult):
    if total <= want:
        return total
    for t in range(want, 0, -1):
        if total % t == 0 and t % mult == 0:
            return t
    return total


def _rope_tables(s):
    half = HEAD_DIM // 2
    inv_freq = ROPE_THETA ** (-jnp.arange(half, dtype=F32) / half)
    ang = jnp.arange(s, dtype=F32)[:, None] * inv_freq[None, :]
    cos = jnp.tile(jnp.cos(ang), (1, LANES // half))
    sin = jnp.tile(jnp.concatenate([-jnp.sin(ang), jnp.sin(ang)], axis=1), (1, LANES // HEAD_DIM))
    return cos, sin


def _rope(x, cos, sin, lo):
    partner = jnp.where(lo, pltpu.roll(x, LANES - HEAD_DIM // 2, 1), pltpu.roll(x, HEAD_DIM // 2, 1))
    return x * cos + partner * sin


def _dot(a, b, dims):
    return lax.dot_general(a, b, (dims, ((), ())), preferred_element_type=F32)


NN, NT, TN = ((1,), (0,)), ((1,), (1,)), ((0,), (0,))


def _kv_variants(t, head_lo):
    r = pltpu.roll(t, HEAD_DIM, 1)
    zero = jnp.zeros_like(t)
    a = (jnp.where(head_lo, t, zero).astype(BF16), jnp.where(head_lo, r, zero).astype(BF16))
    b = (jnp.where(head_lo, zero, r).astype(BF16), jnp.where(head_lo, zero, t).astype(BF16))
    return a, b


PAIRS_PER_KV = N_Q_HEADS // 2 // N_KV_HEADS
STACK = PAIRS_PER_KV * BLOCK


def _stack_pairs(ref, j, fn):
    return jnp.concatenate([fn(ref[:, p * LANES:(p + 1) * LANES])
                            for p in range(j * PAIRS_PER_KV, (j + 1) * PAIRS_PER_KV)], axis=0)


def _sink_row(sink_ref, j, hh):
    col = lax.broadcasted_iota(jnp.int32, (1, STACK), 1)
    heads = [2 * p + hh for p in range(j * PAIRS_PER_KV, (j + 1) * PAIRS_PER_KV)]
    row = jnp.full((1, STACK), sink_ref[0, heads[-1]], F32)
    for t in range(PAIRS_PER_KV - 2, -1, -1):
        row = jnp.where(col < (t + 1) * BLOCK, sink_ref[0, heads[t]], row)
    return row


def _attn_exps(qp, ka, kb, valid, sink_a, sink_b):
    out = []
    for kk, sink in ((ka, sink_a), (kb, sink_b)):
        s = jnp.where(valid, _dot(kk, qp, NT), NEG_INF)
        mx = jnp.maximum(jnp.max(s, axis=0, keepdims=True), sink)
        out.append((jnp.exp(s - mx), jnp.exp(sink - mx)))
    return out


def _attn_common(i, q_ref, k_ref, v_ref, kp_ref, vp_ref, cos_ref, sin_ref, cosp_ref, sinp_ref):
    lane = lax.broadcasted_iota(jnp.int32, (1, LANES), 1)
    lo = (lane % HEAD_DIM) < (HEAD_DIM // 2)
    head_lo = lane < HEAD_DIM
    cos, sin = cos_ref[...], sin_ref[...]
    kc = _rope(k_ref[...].astype(F32), cos, sin, lo)
    kp = _rope(kp_ref[...].astype(F32), cosp_ref[...], sinp_ref[...], lo)
    kext = jnp.concatenate([kp, kc], axis=0)
    vext = jnp.concatenate([vp_ref[...].astype(F32), v_ref[...].astype(F32)], axis=0)
    ka, kb = _kv_variants(kext, head_lo)
    va, vb = _kv_variants(vext, head_lo)
    qi = lax.broadcasted_iota(jnp.int32, (1, STACK), 1) % BLOCK
    kj = lax.broadcasted_iota(jnp.int32, (2 * BLOCK, 1), 0)
    valid = (kj > qi) & (kj <= qi + BLOCK) & ((kj >= BLOCK) | (i > 0))
    cos4 = jnp.concatenate([cos] * PAIRS_PER_KV, axis=0)
    sin4 = jnp.concatenate([sin] * PAIRS_PER_KV, axis=0)
    return lo, head_lo, cos, sin, cos4, sin4, ka, kb, va, vb, valid


def _attn_fwd(proj, sinks, cos, sin, s):
    nb = s // BLOCK
    kcol, vcol = ATTN_WIDTH // LANES, ATTN_WIDTH // LANES + 1

    def body(q_ref, k_ref, v_ref, kp_ref, vp_ref, cos_ref, sin_ref, cosp_ref, sinp_ref, sink_ref, o_ref):
        i = pl.program_id(0)
        lo, head_lo, _, _, cs4, sn4, ka, kb, va, vb, valid = _attn_common(
            i, q_ref, k_ref, v_ref, kp_ref, vp_ref, cos_ref, sin_ref, cosp_ref, sinp_ref)
        row = lax.broadcasted_iota(jnp.int32, (16, 1), 0)
        one = jnp.ones((), BF16)
        for j in range(N_KV_HEADS):
            q4 = _stack_pairs(q_ref, j, lambda t: t.astype(F32))
            qp = (_rope(q4, cs4, sn4, lo) * HEAD_DIM ** -0.5).astype(BF16)
            exps = _attn_exps(qp, ka[j], kb[j], valid, _sink_row(sink_ref, j, 0), _sink_row(sink_ref, j, 1))
            outs = []
            for (e, es), vv, mine in zip(exps, (va[j], vb[j]), (head_lo, ~head_lo)):
                ee = jnp.concatenate([e.astype(BF16), jnp.where(row == 0, es, 0.0).astype(BF16)], axis=0)
                tail = jnp.where((row == 0) & ~mine, one, jnp.zeros((), BF16))
                vx = jnp.concatenate([jnp.where(mine, vv, one), tail], axis=0)
                un = _dot(ee, vx, TN)
                outs.append(un / pltpu.roll(un, HEAD_DIM, 1))
            o = jnp.where(head_lo, outs[0], outs[1]).astype(BF16)
            for t in range(PAIRS_PER_KV):
                p = j * PAIRS_PER_KV + t
                o_ref[:, p * LANES:(p + 1) * LANES] = o[t * BLOCK:(t + 1) * BLOCK]

    prev = lambda i: (jnp.maximum(i - 1, 0), 0)
    return pl.pallas_call(
        body, name="attn_fwd", grid=(nb,),
        in_specs=[pl.BlockSpec((BLOCK, ATTN_WIDTH), lambda i: (i, 0)),
                  pl.BlockSpec((BLOCK, LANES), lambda i: (i, kcol)),
                  pl.BlockSpec((BLOCK, LANES), lambda i: (i, vcol)),
                  pl.BlockSpec((BLOCK, LANES), lambda i: (jnp.maximum(i - 1, 0), kcol)),
                  pl.BlockSpec((BLOCK, LANES), lambda i: (jnp.maximum(i - 1, 0), vcol)),
                  pl.BlockSpec((BLOCK, LANES), lambda i: (i, 0)),
                  pl.BlockSpec((BLOCK, LANES), lambda i: (i, 0)),
                  pl.BlockSpec((BLOCK, LANES), prev),
                  pl.BlockSpec((BLOCK, LANES), prev),
                  pl.BlockSpec(memory_space=pltpu.SMEM)],
        out_specs=pl.BlockSpec((BLOCK, ATTN_WIDTH), lambda i: (i, 0)),
        out_shape=jax.ShapeDtypeStruct((s, ATTN_WIDTH), BF16),
        compiler_params=_cp(("arbitrary",), 32),
    )(proj, proj, proj, proj, proj, cos, sin, cos, sin, sinks)


def _attn_bwd(proj, dmix, sinks, cos, sin, s):
    nb = s // BLOCK
    kcol, vcol = ATTN_WIDTH // LANES, ATTN_WIDTH // LANES + 1

    def body(q_ref, k_ref, v_ref, kp_ref, vp_ref, cos_ref, sin_ref, cosp_ref, sinp_ref, sink_ref, do_ref,
             dq_ref, dk_ref, dv_ref, dsink_ref, ck_ref, cv_ref):
        g = pl.program_id(0)
        i = nb - 1 - g

        @pl.when(g == 0)
        def _():
            ck_ref[...] = jnp.zeros_like(ck_ref)
            cv_ref[...] = jnp.zeros_like(cv_ref)
            dsink_ref[...] = jnp.zeros_like(dsink_ref)

        lo, head_lo, cs, sn, cs4, sn4, ka, kb, va, vb, valid = _attn_common(
            i, q_ref, k_ref, v_ref, kp_ref, vp_ref, cos_ref, sin_ref, cosp_ref, sinp_ref)
        lane = lax.broadcasted_iota(jnp.int32, (1, LANES), 1)
        dk_j, dv_j = [], []
        dsink = jnp.zeros((1, LANES), F32)
        for j in range(N_KV_HEADS):
            q4 = _stack_pairs(q_ref, j, lambda t: t.astype(F32))
            qp = (_rope(q4, cs4, sn4, lo) * HEAD_DIM ** -0.5).astype(BF16)
            exps = _attn_exps(qp, ka[j], kb[j], valid, _sink_row(sink_ref, j, 0), _sink_row(sink_ref, j, 1))
            do = _stack_pairs(do_ref, j, lambda t: t)
            dq_r = jnp.zeros((STACK, LANES), F32)
            dkc, dvc = [], []
            for hh, ((e, es), kk, vv) in enumerate(zip(exps, (ka[j], kb[j]), (va[j], vb[j]))):
                inv = 1.0 / (jnp.sum(e, axis=0, keepdims=True) + es)
                pr = e * inv
                dp = _dot(vv, do, NT)
                delta = jnp.sum(pr * dp, axis=0, keepdims=True)
                ds = (pr * (dp - delta)).astype(BF16)
                psd = es * inv * delta
                for t in range(PAIRS_PER_KV):
                    head = 2 * (j * PAIRS_PER_KV + t) + hh
                    dsink = dsink + jnp.where(
                        lane == head, -jnp.sum(psd[:, t * BLOCK:(t + 1) * BLOCK], axis=1, keepdims=True), 0.0)
                dq_r = dq_r + _dot(ds, kk, TN)
                dkc.append(_dot(ds, qp, NN))
                dvc.append(_dot(pr.astype(BF16), do, NN))
            dk_j.append(jnp.where(head_lo, dkc[0], dkc[1]))
            dv_j.append(jnp.where(head_lo, dvc[0], dvc[1]))
            dq = _rope(dq_r * HEAD_DIM ** -0.5, cs4, -sn4, lo).astype(BF16)
            for t in range(PAIRS_PER_KV):
                p = j * PAIRS_PER_KV + t
                dq_ref[:, p * LANES:(p + 1) * LANES] = dq[t * BLOCK:(t + 1) * BLOCK]
        tot_k = [t + pltpu.roll(t, HEAD_DIM, 1) for t in dk_j]
        tot_v = [t + pltpu.roll(t, HEAD_DIM, 1) for t in dv_j]
        dkext = jnp.where(head_lo, tot_k[0], tot_k[1])
        dvext = jnp.where(head_lo, tot_v[0], tot_v[1])
        dk_r = dkext[BLOCK:] + ck_ref[...]
        dk_ref[...] = _rope(dk_r, cs, -sn, lo).astype(BF16)
        dv_ref[...] = (dvext[BLOCK:] + cv_ref[...]).astype(BF16)
        ck_ref[...] = dkext[:BLOCK]
        cv_ref[...] = dvext[:BLOCK]
        dsink_ref[0:1, :] += dsink

    cur = lambda col: (lambda g: (nb - 1 - g, col))
    prv = lambda col: (lambda g: (jnp.maximum(nb - 2 - g, 0), col))
    blk = lambda w, f: pl.BlockSpec((BLOCK, w), f)
    return pl.pallas_call(
        body, name="attn_bwd", grid=(nb,),
        in_specs=[blk(ATTN_WIDTH, cur(0)), blk(LANES, cur(kcol)), blk(LANES, cur(vcol)),
                  blk(LANES, prv(kcol)), blk(LANES, prv(vcol)),
                  blk(LANES, cur(0)), blk(LANES, cur(0)), blk(LANES, prv(0)), blk(LANES, prv(0)),
                  pl.BlockSpec(memory_space=pltpu.SMEM),
                  blk(ATTN_WIDTH, cur(0))],
        out_specs=[blk(ATTN_WIDTH, cur(0)), blk(LANES, cur(0)), blk(LANES, cur(0)),
                   pl.BlockSpec((8, LANES), lambda g: (0, 0))],
        out_shape=[jax.ShapeDtypeStruct((s, ATTN_WIDTH), BF16), jax.ShapeDtypeStruct((s, LANES), BF16),
                   jax.ShapeDtypeStruct((s, LANES), BF16), jax.ShapeDtypeStruct((8, LANES), F32)],
        scratch_shapes=[pltpu.VMEM((BLOCK, LANES), F32), pltpu.VMEM((BLOCK, LANES), F32)],
        compiler_params=_cp(("arbitrary",), 32),
    )(proj, proj, proj, proj, proj, cos, sin, cos, sin, sinks, dmix)


def _causal_conv(x, prev8, w):
    row = lax.broadcasted_iota(jnp.int32, (8, 1), 0)
    r1, r2 = pltpu.roll(x, 1, 0), pltpu.roll(x, 2, 0)
    s1 = jnp.concatenate([jnp.where(row == 0, prev8[7:8], r1[:8]), r1[8:]], axis=0)
    s2 = jnp.concatenate([jnp.where(row == 0, prev8[6:7], jnp.where(row == 1, prev8[7:8], r2[:8])), r2[8:]], axis=0)
    return w[0:1] * s2 + w[1:2] * s1 + w[2:3] * x


def _conv_bwd(dy, x, w, next8):
    r = x.shape[0]
    row = lax.broadcasted_iota(jnp.int32, (8, 1), 0)
    r1, r2 = pltpu.roll(dy, r - 1, 0), pltpu.roll(dy, r - 2, 0)
    n1 = jnp.concatenate([r1[:r - 8], jnp.where(row == 7, next8[0:1], r1[r - 8:])], axis=0)
    n2 = jnp.concatenate([r2[:r - 8], jnp.where(row == 6, next8[0:1], jnp.where(row == 7, next8[1:2], r2[r - 8:]))],
                         axis=0)
    dx = w[2:3] * dy + w[1:2] * n1 + w[0:1] * n2
    dws = [jnp.sum(t * x, axis=0, keepdims=True) for t in (n2, n1, dy)]
    return dx, dws


CONV_COLS = 256


def _convmix_cols(d):
    conv_w = d - ATTN_WIDTH
    base = (ATTN_WIDTH + 2 * KV_WIDTH) // CONV_COLS
    step = conv_w // CONV_COLS
    return base, base + step, base + 2 * step, step


def _convmix_fwd(proj, scw8, s, d):
    gb0, gc0, h0, ncb = _convmix_cols(d)
    tr = _pick(s, 1024, 16)
    ni = s // tr

    def body(gb_ref, gc_ref, h_ref, w_ref, o_ref, carry_ref):
        @pl.when(pl.program_id(1) == 0)
        def _():
            carry_ref[...] = jnp.zeros_like(carry_ref)

        gch = gc_ref[...].astype(F32) * h_ref[...].astype(F32)
        cc = _causal_conv(gch, carry_ref[...], w_ref[...])
        o_ref[...] = (gb_ref[...].astype(F32) * cc).astype(BF16)
        carry_ref[...] = gch[tr - 8:]

    spec = lambda c0: pl.BlockSpec((tr, CONV_COLS), lambda j, i: (i, c0 + j))
    return pl.pallas_call(
        body, name="convmix_fwd", grid=(ncb, ni),
        in_specs=[spec(gb0), spec(gc0), spec(h0), pl.BlockSpec((8, CONV_COLS), lambda j, i: (0, j))],
        out_specs=pl.BlockSpec((tr, CONV_COLS), lambda j, i: (i, j)),
        out_shape=jax.ShapeDtypeStruct((s, d - ATTN_WIDTH), BF16),
        scratch_shapes=[pltpu.VMEM((8, CONV_COLS), F32)],
        compiler_params=_cp(("arbitrary", "arbitrary"), 32),
    )(proj, proj, proj, scw8)


def _convmix_bwd(proj, dmix, scw8, s, d):
    gb0, gc0, h0, ncb = _convmix_cols(d)
    tr = _pick(s, 1024, 16)
    ni = s // tr
    dc0 = ATTN_WIDTH // CONV_COLS

    def body(dc_ref, gb_ref, gc_ref, h_ref, gcp_ref, hp_ref, w_ref, d3_ref, dw_ref, nxt_ref):
        g = pl.program_id(1)
        i = ni - 1 - g

        @pl.when(g == 0)
        def _():
            nxt_ref[...] = jnp.zeros_like(nxt_ref)
            dw_ref[...] = jnp.zeros_like(dw_ref)

        w = w_ref[...]
        gb, gc, h = gb_ref[...].astype(F32), gc_ref[...].astype(F32), h_ref[...].astype(F32)
        gch = gc * h
        prev8 = (gcp_ref[...].astype(F32) * hp_ref[...].astype(F32))[8:16] * (i > 0).astype(F32)
        cc = _causal_conv(gch, prev8, w)
        dc = dc_ref[...].astype(F32)
        dcc = dc * gb
        dgch, dws = _conv_bwd(dcc, gch, w, nxt_ref[...])
        d3_ref[0] = (dc * cc).astype(BF16)
        d3_ref[1] = (dgch * h).astype(BF16)
        d3_ref[2] = (dgch * gc).astype(BF16)
        for t in range(3):
            dw_ref[t:t + 1, :] += dws[t]
        nxt_ref[...] = dcc[0:8]

    cur = lambda c0: pl.BlockSpec((tr, CONV_COLS), lambda j, g: (ni - 1 - g, c0 + j))
    prv = lambda c0: pl.BlockSpec((16, CONV_COLS), lambda j, g: (jnp.maximum((ni - 1 - g) * (tr // 16) - 1, 0), c0 + j))
    return pl.pallas_call(
        body, name="convmix_bwd", grid=(ncb, ni),
        in_specs=[cur(dc0), cur(gb0), cur(gc0), cur(h0), prv(gc0), prv(h0),
                  pl.BlockSpec((8, CONV_COLS), lambda j, g: (0, j))],
        out_specs=[pl.BlockSpec((3, tr, CONV_COLS), lambda j, g: (0, ni - 1 - g, j)),
                   pl.BlockSpec((8, CONV_COLS), lambda j, g: (0, j))],
        out_shape=[jax.ShapeDtypeStruct((3, s, d - ATTN_WIDTH), BF16), jax.ShapeDtypeStruct((8, d - ATTN_WIDTH), F32)],
        scratch_shapes=[pltpu.VMEM((8, CONV_COLS), F32)],
        compiler_params=_cp(("arbitrary", "arbitrary"), 32),
    )(dmix, proj, proj, proj, proj, proj, scw8)


def _ln_fwd(z):
    mu = jnp.mean(z, axis=-1, keepdims=True)
    zc = z - mu
    var = jnp.mean(zc * zc, axis=-1, keepdims=True)
    rstd = lax.rsqrt(var + LN_EPS)
    return zc * rstd, rstd


def _ln_bwd(dout, xh, rstd, g):
    dxh = dout * g
    c1 = jnp.mean(dxh, axis=-1, keepdims=True)
    c2 = jnp.mean(dxh * xh, axis=-1, keepdims=True)
    dz = rstd * (dxh - c1 - xh * c2)
    return dz, jnp.sum(dout * xh, axis=0, keepdims=True), jnp.sum(dout, axis=0, keepdims=True)


def _outproj_ln1(attn, conv, wout, x, g1, b1, s, d):
    tm = _pick(s, 512, 16)
    ka = attn.shape[1]
    one_buffer = pl.Buffered(1)

    def body(a_ref, c_ref, wt_ref, wb_ref, x_ref, g_ref, b_ref, x1_ref, x1b_ref, xh_ref, rs_ref):
        y = _dot(a_ref[...], wt_ref[...], NN) + _dot(c_ref[...], wb_ref[...], NN)
        xh, rstd = _ln_fwd(ALPHA * x_ref[...] + y)
        x1 = xh * g_ref[...] + b_ref[...]
        x1_ref[...] = x1
        x1b_ref[...] = x1.astype(BF16)
        xh_ref[...] = xh.astype(BF16)
        rs_ref[...] = rstd

    row = lambda w: pl.BlockSpec((tm, w), lambda i: (i, 0))
    vec = pl.BlockSpec((1, d), lambda i: (0, 0))
    return pl.pallas_call(
        body, name="outproj_ln1", grid=(s // tm,),
        in_specs=[row(ka), row(d - ka), pl.BlockSpec((ka, d), lambda i: (0, 0), pipeline_mode=one_buffer),
                  pl.BlockSpec((d - ka, d), lambda i: (ka // (d - ka), 0), pipeline_mode=one_buffer), row(d), vec, vec],
        out_specs=[row(d), row(d), row(d), row(1)],
        out_shape=[jax.ShapeDtypeStruct((s, d), F32), jax.ShapeDtypeStruct((s, d), BF16),
                   jax.ShapeDtypeStruct((s, d), BF16), jax.ShapeDtypeStruct((s, 1), F32)],
        compiler_params=_cp(("arbitrary",), 56),
    )(attn, conv, wout, wout, x, g1, b1)


def _ffn_up(x1b, wup, fcw8, s, d, dff):
    tm = _pick(s, 1024, 16)
    tn = _pick(dff, 512, LANES)
    nj, ni = dff // tn, s // tm

    def body(x_ref, wa_ref, wg_ref, ca_ref, cg_ref, u_ref, y_ref, h_ref, carry_ref):
        @pl.when(pl.program_id(1) == 0)
        def _():
            carry_ref[...] = jnp.zeros_like(carry_ref)

        xa = x_ref[...]
        ys = []
        for part, (w_ref, c_ref) in enumerate(((wa_ref, ca_ref), (wg_ref, cg_ref))):
            u = _dot(xa, w_ref[...], NN)
            u_ref[part] = u.astype(BF16)
            y = _causal_conv(u, carry_ref[part], c_ref[...])
            carry_ref[part] = u[tm - 8:]
            y_ref[part] = y.astype(BF16)
            ys.append(y)
        a2, g2 = ys
        sig = 1.0 / (1.0 + jnp.exp(-a2))
        h_ref[...] = (a2 * sig * g2).astype(BF16)

    return pl.pallas_call(
        body, name="ffn_up", grid=(nj, ni),
        in_specs=[pl.BlockSpec((tm, d), lambda j, i: (i, 0)),
                  pl.BlockSpec((d, tn), lambda j, i: (0, j)),
                  pl.BlockSpec((d, tn), lambda j, i: (0, j + nj)),
                  pl.BlockSpec((8, tn), lambda j, i: (0, j)),
                  pl.BlockSpec((8, tn), lambda j, i: (0, j + nj))],
        out_specs=[pl.BlockSpec((2, tm, tn), lambda j, i: (0, i, j)),
                   pl.BlockSpec((2, tm, tn), lambda j, i: (0, i, j)),
                   pl.BlockSpec((tm, tn), lambda j, i: (i, j))],
        out_shape=[jax.ShapeDtypeStruct((2, s, dff), BF16), jax.ShapeDtypeStruct((2, s, dff), BF16),
                   jax.ShapeDtypeStruct((s, dff), BF16)],
        scratch_shapes=[pltpu.VMEM((2, 8, tn), F32)],
        compiler_params=_cp(("arbitrary", "arbitrary"), 56),
    )(x1b, wup, wup, fcw8, fcw8)


def _ffn_mid_bwd(dz2b, wdown, u3, y3, fcw8, s, d, dff):
    tm = _pick(s, 1024, 16)
    tn = _pick(dff, 512, LANES)
    nj, ni = dff // tn, s // tm

    def body(dz_ref, wd_ref, u_ref, y_ref, ca_ref, cg_ref, du_ref, dw_ref, nxt_ref):
        @pl.when(pl.program_id(1) == 0)
        def _():
            nxt_ref[...] = jnp.zeros_like(nxt_ref)
            dw_ref[...] = jnp.zeros_like(dw_ref)

        a2, g2 = y_ref[0].astype(F32), y_ref[1].astype(F32)
        sig = 1.0 / (1.0 + jnp.exp(-a2))
        silu = a2 * sig
        dhv = _dot(dz_ref[...], wd_ref[...], NT)
        dys = (dhv * g2 * (sig * (1.0 + a2 * (1.0 - sig))), dhv * silu)
        for part, (c_ref, dy) in enumerate(zip((ca_ref, cg_ref), dys)):
            dx, dws = _conv_bwd(dy, u_ref[part].astype(F32), c_ref[...], nxt_ref[part])
            du_ref[part] = dx.astype(BF16)
            for t in range(3):
                dw_ref[part, t:t + 1, :] += dws[t]
            nxt_ref[part] = dy[0:8]

    return pl.pallas_call(
        body, name="ffn_mid_bwd", grid=(nj, ni),
        in_specs=[pl.BlockSpec((tm, d), lambda j, g: (ni - 1 - g, 0)),
                  pl.BlockSpec((tn, d), lambda j, g: (j, 0)),
                  pl.BlockSpec((2, tm, tn), lambda j, g: (0, ni - 1 - g, j)),
                  pl.BlockSpec((2, tm, tn), lambda j, g: (0, ni - 1 - g, j)),
                  pl.BlockSpec((8, tn), lambda j, g: (0, j)),
                  pl.BlockSpec((8, tn), lambda j, g: (0, j + nj))],
        out_specs=[pl.BlockSpec((2, tm, tn), lambda j, g: (0, ni - 1 - g, j)),
                   pl.BlockSpec((2, 8, tn), lambda j, g: (0, 0, j))],
        out_shape=[jax.ShapeDtypeStruct((2, s, dff), BF16), jax.ShapeDtypeStruct((2, 8, dff), F32)],
        scratch_shapes=[pltpu.VMEM((2, 8, tn), F32)],
        compiler_params=_cp(("arbitrary", "arbitrary"), 56),
    )(dz2b, wdown, u3, y3, fcw8, fcw8)


def _ffn_down_loss(hmid, wdown, x1, target, g2, b2, s, d, dff):
    tm = _pick(s, 512, SLAB)
    tk = _pick(dff, 1408, LANES)
    ni, nk = s // tm, dff // tk
    slab = min(SLAB, tm)

    def body(h_ref, w_ref, x1_ref, t_ref, g_ref, b_ref, dzb_ref, st_ref, acc_ref):
        i, kk = pl.program_id(0), pl.program_id(1)

        @pl.when((i == 0) & (kk == 0))
        def _():
            st_ref[...] = jnp.zeros_like(st_ref)

        part = _dot(h_ref[...], w_ref[...], NN)

        @pl.when(kk == 0)
        def _():
            acc_ref[...] = part

        @pl.when(kk > 0)
        def _():
            acc_ref[...] += part

        @pl.when(kk == nk - 1)
        def _():
            g, b = g_ref[...], b_ref[...]

            def one(sl, carry):
                rows = pl.ds(pl.multiple_of(sl * slab, slab), slab)
                xh, rstd = _ln_fwd(ALPHA * x1_ref[rows, :] + acc_ref[rows, :])
                diff = xh * g + b - t_ref[rows, :]
                sq = jnp.sum(jnp.sum(diff * diff, axis=1, keepdims=True), axis=0, keepdims=True)
                dz, dg, db = _ln_bwd(diff * (1.0 / d), xh, rstd, g)
                dzb_ref[rows, :] = dz.astype(BF16)
                st_ref[0:1, :] += dg
                st_ref[1:2, :] += db
                st_ref[2:3, :] += sq
                return carry

            lax.fori_loop(0, tm // slab, one, 0)

    row = pl.BlockSpec((tm, d), lambda i, kk: (i, 0))
    vec = pl.BlockSpec((1, d), lambda i, kk: (0, 0))
    return pl.pallas_call(
        body, name="ffn_down_loss", grid=(ni, nk),
        in_specs=[pl.BlockSpec((tm, tk), lambda i, kk: (i, kk)), pl.BlockSpec((tk, d), lambda i, kk: (kk, 0)),
                  row, row, vec, vec],
        out_specs=[row, pl.BlockSpec((8, d), lambda i, kk: (0, 0))],
        out_shape=[jax.ShapeDtypeStruct((s, d), BF16), jax.ShapeDtypeStruct((8, d), F32)],
        scratch_shapes=[pltpu.VMEM((tm, d), F32)],
        compiler_params=_cp(("arbitrary", "arbitrary"), 48),
    )(hmid, wdown, x1, target, g2, b2)


def _ffn_dx_ln1_bwd(du3, wup, dz2b, xh1, rstd1, g1, s, d, dff):
    tm = _pick(s, 512, SLAB)
    tk = _pick(dff, 2816, MXU_DIM)
    nkh = dff // tk
    ni, nk = s // tm, 2 * nkh
    slab = min(SLAB, tm)

    def body(a_ref, w_ref, dz2_ref, xh_ref, rs_ref, g_ref, dzb_ref, st_ref, acc_ref):
        i, kk = pl.program_id(0), pl.program_id(1)

        @pl.when((i == 0) & (kk == 0))
        def _():
            st_ref[...] = jnp.zeros_like(st_ref)

        part = _dot(a_ref[...], w_ref[...], NT)

        @pl.when(kk == 0)
        def _():
            acc_ref[...] = part

        @pl.when(kk > 0)
        def _():
            acc_ref[...] += part

        @pl.when(kk == nk - 1)
        def _():
            g = g_ref[...]

            def one(sl, carry):
                rows = pl.ds(pl.multiple_of(sl * slab, slab), slab)
                dx1 = ALPHA * dz2_ref[rows, :].astype(F32) + acc_ref[rows, :]
                dz, dg, db = _ln_bwd(dx1, xh_ref[rows, :].astype(F32), rs_ref[rows, :], g)
                dzb_ref[rows, :] = dz.astype(BF16)
                st_ref[0:1, :] += dg
                st_ref[1:2, :] += db
                return carry

            lax.fori_loop(0, tm // slab, one, 0)

    row = pl.BlockSpec((tm, d), lambda i, kk: (i, 0))
    row1 = pl.BlockSpec((tm, d), lambda i, kk: (i, 0), pipeline_mode=pl.Buffered(1))
    return pl.pallas_call(
        body, name="ffn_dx_ln1_bwd", grid=(ni, nk),
        in_specs=[pl.BlockSpec((None, tm, tk), lambda i, kk: (kk // nkh, i, kk % nkh)),
                  pl.BlockSpec((d, tk), lambda i, kk: (0, kk)),
                  row1, row1, pl.BlockSpec((tm, 1), lambda i, kk: (i, 0)), pl.BlockSpec((1, d), lambda i, kk: (0, 0))],
        out_specs=[row, pl.BlockSpec((8, d), lambda i, kk: (0, 0))],
        out_shape=[jax.ShapeDtypeStruct((s, d), BF16), jax.ShapeDtypeStruct((8, d), F32)],
        scratch_shapes=[pltpu.VMEM((tm, d), F32)],
        compiler_params=_cp(("arbitrary", "arbitrary"), 56),
    )(du3, wup, dz2b, xh1, rstd1, g1)


def _phase_mixer(x, win_t, wout, scw8, sinks, ln1_g, ln1_b, after=None):
    s, d = x.shape
    n_in = win_t.shape[0]
    cos, sin = _rope_tables(s)
    proj = _matmul(x, win_t, mode="nt", m=s, n=n_in, k=d, tm=_pick(s, 512, 16), tn=n_in, tk=d, out_dtype=BF16,
                   name="in_proj", vmem_mb=52, after=after,
                   b_spec=pl.BlockSpec((n_in, d), lambda j, i, kk: (0, 0), pipeline_mode=pl.Buffered(1)))
    attn = _attn_fwd(proj, sinks, cos, sin, s)
    conv = _convmix_fwd(proj, scw8, s, d)
    wout = wout(attn) if callable(wout) else wout
    x1, x1b, xh1, rstd1 = _outproj_ln1(attn, conv, wout, x, ln1_g, ln1_b, s, d)
    return dict(x=x, cos=cos, sin=sin, proj=proj, attn=attn, conv=conv, x1=x1, x1b=x1b, xh1=xh1, rstd1=rstd1,
                wout=wout)


def _phase_ffn(a, target, wup, wdown, fcw8, ln2_g, ln2_b):
    x1, x1b = a["x1"], a["x1b"]
    s, d = x1.shape
    dff = wdown.shape[0]
    u3, y3, hmid = _ffn_up(x1b, wup, fcw8, s, d, dff)
    dz2b, st2 = _ffn_down_loss(hmid, wdown, x1, target, ln2_g, ln2_b, s, d, dff)

    tnw = _pick(d, 1024, MXU_DIM)
    keep_b = pl.BlockSpec((s, tnw), lambda j, i, kk: (kk, j), pipeline_mode=pl.Buffered(1))
    g_wdown = _matmul(hmid, dz2b, mode="tn", m=dff, n=d, k=s, tm=_pick(dff, 512, MXU_DIM), tn=tnw, tk=s,
                      out_dtype=BF16, name="grad_w_down", vmem_mb=56, b_spec=keep_b)
    du3, dfcw = _ffn_mid_bwd(dz2b, wdown, u3, y3, fcw8, s, d, dff)
    tnu = _pick(dff, 512, MXU_DIM)
    njh = dff // tnu
    tmu = _pick(d, 1024, LANES)
    g_wup = _matmul(x1b, du3, mode="tn", m=d, n=2 * dff, k=s, tm=tmu, tn=tnu, tk=s, out_dtype=BF16,
                    name="grad_w_up", vmem_mb=58, m_outer=True,
                    a_spec=pl.BlockSpec((s, tmu), lambda j, i, kk: (kk, i), pipeline_mode=pl.Buffered(1)),
                    b_spec=pl.BlockSpec((None, s, tnu), lambda j, i, kk: (j // njh, kk, j % njh)))
    return dict(du3=du3, dz2b=dz2b, st2=st2, dfcw=dfcw, wdown=g_wdown, wup=g_wup)


def _phase_rest(a, f, wup, wout, win_t, scw8, sinks, ln1_g, between=None):
    xb, cos, sin, proj, attn, conv = a["x"], a["cos"], a["sin"], a["proj"], a["attn"], a["conv"]
    du3, dz2b, st2, dfcw = f["du3"], f["dz2b"], f["st2"], f["dfcw"]
    s, d = a["x1"].shape
    dff = wup.shape[1] // 2
    n_in = win_t.shape[0]
    ts = _pick(s, 2048, 16)
    dz1b, st1 = _ffn_dx_ln1_bwd(du3, wup, dz2b, a["xh1"], a["rstd1"], ln1_g, s, d, dff)
    after = between(dz1b) if between is not None else None

    tnw = _pick(d, 1024, MXU_DIM)
    halves = [_matmul(part, dz1b, mode="tn", m=part.shape[1], n=d, k=s, tm=_pick(part.shape[1], 512, LANES), tn=tnw,
                      tk=s, out_dtype=BF16, name="grad_w_out_" + tag, vmem_mb=56, after=after,
                      b_spec=pl.BlockSpec((s, tnw), lambda j, i, kk: (kk, j), pipeline_mode=pl.Buffered(1)))
              for tag, part in (("attn", attn), ("conv", conv))]
    g_wout = jnp.concatenate(halves, axis=0)
    dmix = _matmul(dz1b, wout, mode="nt", m=s, n=d, k=d, tm=_pick(s, 1024, 16), tn=_pick(d, 1024, LANES), tk=d,
                   out_dtype=BF16, name="out_dmix", vmem_mb=48, after=after)
    d3, dscw = _convmix_bwd(proj, dmix, scw8, s, d)
    dq, dk, dv, dsink = _attn_bwd(proj, dmix, sinks, cos, sin, s)
    dproj = jnp.concatenate([dq, dk, dv, d3[0], d3[1], d3[2]], axis=1)
    g_win_t = _matmul(dproj, xb, mode="tn", m=n_in, n=d, k=s, tm=_pick(n_in, 2176, LANES), tn=_pick(d, 512, LANES),
                      tk=ts, out_dtype=BF16, name="grad_w_in", vmem_mb=48)
    small = dict(loss_sq=st2[2, 0], ln2_g=st2[0], ln2_b=st2[1], ln1_g=st1[0], ln1_b=st1[1], sinks=dsink[0, :N_Q_HEADS],
                 fcw=jnp.concatenate([dfcw[0, :3], dfcw[1, :3]], axis=1), scw=dscw[:3])
    return (dproj, dz1b), dict(win_t=g_win_t, wout=g_wout), small


def _grad_x(dproj, dz1b, win_t, after=None):
    s, n_in = dproj.shape
    d = win_t.shape[1]
    return _matmul(dproj, win_t, mode="nn", m=s, n=d, k=n_in, tm=_pick(s, 512, 16), tn=_pick(d, 1024, LANES),
                   tk=n_in, out_dtype=F32, name="grad_x", vmem_mb=56, res=dz1b, alpha=ALPHA, after=after)


def _local_step(x, target, win_t, wout, wup, wdown, scw8, fcw8, sinks, ln1_g, ln1_b, ln2_g, ln2_b):
    a = _phase_mixer(x, win_t, wout, scw8, sinks, ln1_g, ln1_b)
    f = _phase_ffn(a, target, wup, wdown, fcw8, ln2_g, ln2_b)
    (dproj, dz1b), g, small = _phase_rest(a, f, wup, wout, win_t, scw8, sinks, ln1_g)
    return _grad_x(dproj, dz1b, win_t), dict(g, wup=f["wup"], wdown=f["wdown"]), small


W_IN, W_OUT = ("win_t",), ("wout",)
MIXER = W_IN + W_OUT
FFN = ("wup", "wdown")
BIG = MIXER + FFN


def _geom(shard_shapes):
    out = {}
    for name in BIG:
        r, c = shard_shapes[name]
        out[name] = ("col" if name == "wup" else "row", (r, c), (r // 2, c))
    return out


def _full_shape(kind, shard):
    r, c = shard
    return (N_CHIPS * r, c) if kind == "row" else (r, N_CHIPS * c)


def _piece_of(ref, kind, shard, chip, half):
    r, c = shard
    if kind == "row":
        return ref.at[pl.ds(chip * r + half * (r // 2), r // 2), :]
    return ref.at[pl.ds(half * (r // 2), r // 2), pl.ds(chip * c, c)]


def _shard_piece(ref, shard, half):
    r, _ = shard
    return ref.at[pl.ds(half * (r // 2), r // 2), :]


def _me():
    return lax.axis_index("x"), lax.axis_index("y"), lax.axis_index("c")


def _other_chips(x, y):
    return [(1 - x, y), (x, 1 - y), (1 - x, 1 - y)]


def _remote(src, dst, send_sem, recv_sem, dev):
    return pltpu.make_async_remote_copy(src_ref=src, dst_ref=dst, send_sem=send_sem, recv_sem=recv_sem,
                                        device_id=dev, device_id_type=MESH)


def _place_shard(w, chip1, kind, name):
    r, c = w.shape
    tr = _rows_tile(r, c, 16, ADD_TILE)
    nt = r // tr

    def body(chip_ref, w_ref, o_ref):
        o_ref[...] = w_ref[...].astype(BF16)

    out_map = (lambda i, chip_ref: (chip_ref[0] * nt + i, 0)) if kind == "row" else (lambda i, chip_ref: (i, chip_ref[0]))
    return pl.pallas_call(
        body, name="place_" + name,
        grid_spec=pltpu.PrefetchScalarGridSpec(
            num_scalar_prefetch=1, grid=(nt,),
            in_specs=[pl.BlockSpec((tr, c), lambda i, chip_ref: (i, 0))],
            out_specs=pl.BlockSpec((tr, c), out_map)),
        out_shape=jax.ShapeDtypeStruct(_full_shape(kind, (r, c)), BF16),
        compiler_params=_cp(("arbitrary",), 32),
    )(chip1, w)


def _allgather_weights(names, placed, geom, small_shards):
    nb, ns = len(names), len(small_shards)
    small_w = [a.shape[1] for a in small_shards]

    def body(*refs):
        sm = refs[nb:nb + ns]
        full = refs[nb + ns:2 * nb + ns]
        smf = refs[2 * nb + ns:2 * nb + 2 * ns]
        send, recv, loc = refs[2 * nb + 2 * ns:]
        x, y, c = _me()
        chip = 2 * x + y
        sib = (x, y, 1 - c)
        others = _other_chips(x, y)
        locals_, sends = [], []
        for m, name in enumerate(names):
            kind, shard, _ = geom[name]
            mine = _piece_of(full[m], kind, shard, chip, c)
            for k, (qx, qy) in enumerate(others):
                cp = _remote(mine, mine, send.at[6 * m + k], recv.at[6 * m + k], (qx, qy, c))
                cp.start()
                sends.append(cp)
        for t in range(ns):
            cp = pltpu.make_async_copy(sm[t], smf[t].at[:, pl.ds(chip * small_w[t], small_w[t])], loc.at[t])
            cp.start()
            locals_.append(cp)
            for k, (qx, qy) in enumerate(others):
                cp = _remote(sm[t], smf[t].at[:, pl.ds(chip * small_w[t], small_w[t])],
                             send.at[6 * nb + 3 * t + k], recv.at[6 * nb + 3 * t + k], (qx, qy, c))
                cp.start()
                sends.append(cp)
        for m, name in enumerate(names):
            kind, shard, _ = geom[name]
            for k, (qx, qy) in enumerate(others):
                got = _piece_of(full[m], kind, shard, 2 * qx + qy, c)
                _remote(got, got, send.at[6 * m + k], recv.at[6 * m + k], (qx, qy, c)).wait_recv()
                cp = _remote(got, got, send.at[6 * m + 3 + k], recv.at[6 * m + 3 + k], sib)
                cp.start()
                sends.append(cp)
        for t in range(ns):
            for k, (qx, qy) in enumerate(others):
                got = smf[t].at[:, pl.ds((2 * qx + qy) * small_w[t], small_w[t])]
                _remote(got, got, send.at[6 * nb + 3 * t + k], recv.at[6 * nb + 3 * t + k], (qx, qy, c)).wait_recv()
        for m, name in enumerate(names):
            kind, shard, _ = geom[name]
            for k, (qx, qy) in enumerate(others):
                got = _piece_of(full[m], kind, shard, 2 * qx + qy, 1 - c)
                _remote(got, got, send.at[6 * m + 3 + k], recv.at[6 * m + 3 + k], sib).wait_recv()
        for cp in sends:
            cp.wait_send()
        for cp in locals_:
            cp.wait()

    nsem = 6 * nb + 3 * ns
    out_shape = [jax.ShapeDtypeStruct(placed[n].shape, BF16) for n in names]
    out_shape += [jax.ShapeDtypeStruct((8, N_CHIPS * w), F32) for w in small_w]
    outs = pl.pallas_call(
        body, name="allgather_weights", in_specs=[ANY] * (nb + ns), out_specs=[ANY] * (nb + ns), out_shape=out_shape,
        input_output_aliases={m: m for m in range(nb)},
        scratch_shapes=[pltpu.SemaphoreType.DMA((nsem,)), pltpu.SemaphoreType.DMA((nsem,)),
                        pltpu.SemaphoreType.DMA((ns,))],
    )(*[placed[n] for n in names], *small_shards)
    return dict(zip(names, outs[:nb])), list(outs[nb:])


def _sibling_exchange(names, grads, geom):
    nb = len(names)

    def body(*refs):
        g = refs[:nb]
        got = refs[nb:2 * nb]
        send, recv = refs[2 * nb:]
        x, y, c = _me()
        sib = (x, y, 1 - c)
        cps = []
        for m, name in enumerate(names):
            kind, shard, _ = geom[name]
            for r in range(N_CHIPS):
                cp = _remote(_piece_of(g[m], kind, shard, r, 1 - c), got[m].at[r],
                             send.at[N_CHIPS * m + r], recv.at[N_CHIPS * m + r], sib)
                cp.start()
                cps.append(cp)
        for cp in cps:
            cp.wait_recv()
        for cp in cps:
            cp.wait_send()

    return pl.pallas_call(
        body, name="grad_sibling_exchange_" + names[0], in_specs=[ANY] * nb, out_specs=[ANY] * nb,
        out_shape=[jax.ShapeDtypeStruct((N_CHIPS,) + geom[n][2], BF16) for n in names],
        scratch_shapes=[pltpu.SemaphoreType.DMA((N_CHIPS * nb,)), pltpu.SemaphoreType.DMA((N_CHIPS * nb,))],
    )(*[grads[n] for n in names])


def _sibling_assemble(names, shards, geom):
    nb = len(names)

    def body(*refs):
        full = refs[nb:2 * nb]
        send, recv = refs[2 * nb:]
        x, y, c = _me()
        sib = (x, y, 1 - c)
        cps = []
        for m, name in enumerate(names):
            mine = _shard_piece(full[m], geom[name][1], c)
            cp = _remote(mine, mine, send.at[m], recv.at[m], sib)
            cp.start()
            cps.append(cp)
        for m, name in enumerate(names):
            theirs = _shard_piece(full[m], geom[name][1], 1 - c)
            _remote(theirs, theirs, send.at[m], recv.at[m], sib).wait_recv()
        for cp in cps:
            cp.wait_send()

    return pl.pallas_call(
        body, name="grad_sibling_assemble_" + names[0], in_specs=[ANY] * nb, out_specs=[ANY] * nb,
        out_shape=[jax.ShapeDtypeStruct(geom[n][1], F32) for n in names],
        input_output_aliases={m: m for m in range(nb)},
        scratch_shapes=[pltpu.SemaphoreType.DMA((nb,)), pltpu.SemaphoreType.DMA((nb,))],
    )(*shards)


HBM = pl.BlockSpec(memory_space=pltpu.HBM)
SEM = pl.BlockSpec(memory_space=pltpu.SEMAPHORE)
EFFECT = pltpu.SideEffectType.DATAFLOW_SIDE_EFFECTING
TOKEN = jax.ShapeDtypeStruct((8, LANES), F32)


def _hbm(a):
    return pltpu.with_memory_space_constraint(a, pltpu.HBM)


def _gather_copies(names, full, geom, send, recv):
    x, y, c = _me()
    out = []
    for m, name in enumerate(names):
        kind, shard, _ = geom[name]
        mine = _piece_of(full[m], kind, shard, 2 * x + y, c)
        for k, (qx, qy) in enumerate(_other_chips(x, y)):
            theirs = _piece_of(full[m], kind, shard, 2 * qx + qy, c)
            out.append((_remote(mine, mine, send.at[3 * m + k], recv.at[3 * m + k], (qx, qy, c)),
                        _remote(theirs, theirs, send.at[3 * m + k], recv.at[3 * m + k], (qx, qy, c))))
    return out


def _gather_start(names, placed, geom, after):
    nb = len(names)

    def body(*refs):
        full = refs[:nb]
        send, recv = refs[nb + 1], refs[nb + 2]
        token = refs[2 * nb + 3]
        for cp, _ in _gather_copies(names, full, geom, send, recv):
            cp.start()
        token[...] = jnp.zeros_like(token)

    outs = pl.pallas_call(
        body, name="gather_start_" + names[0],
        out_shape=(pltpu.SemaphoreType.DMA((3 * nb,)), pltpu.SemaphoreType.DMA((3 * nb,)),
                   *[pltpu.HBM(placed[n].shape, BF16) for n in names], TOKEN),
        in_specs=[HBM] * nb + [ANY], out_specs=(SEM, SEM, *[HBM] * nb, pl.BlockSpec(memory_space=pltpu.VMEM)),
        input_output_aliases={m: 2 + m for m in range(nb)},
        compiler_params=pltpu.CompilerParams(has_side_effects=EFFECT),
    )(*[_hbm(placed[n]) for n in names], after)
    return outs[0], outs[1], list(outs[2:2 + nb]), outs[2 + nb]


def _gather_wait(names, send, recv, thru, geom, after):
    nb = len(names)

    def body(*refs):
        full = refs[:nb]
        for mine, theirs in _gather_copies(names, full, geom, refs[nb], refs[nb + 1]):
            mine.wait_send()
            theirs.wait_recv()

    return pl.pallas_call(
        body, name="gather_wait_" + names[0], out_shape=tuple(pltpu.HBM(t.shape, t.dtype) for t in thru),
        in_specs=[HBM] * nb + [SEM, SEM, ANY], out_specs=tuple([HBM] * nb),
        input_output_aliases={m: m for m in range(nb)},
        compiler_params=pltpu.CompilerParams(has_side_effects=EFFECT),
    )(*thru, send, recv, after)


def _gather_forward(names, full, geom):
    nb = len(names)

    def body(*refs):
        arr = refs[nb:2 * nb]
        send, recv = refs[2 * nb:]
        x, y, c = _me()
        sib = (x, y, 1 - c)
        cps = []
        for m, name in enumerate(names):
            kind, shard, _ = geom[name]
            for k, (qx, qy) in enumerate(_other_chips(x, y)):
                got = _piece_of(arr[m], kind, shard, 2 * qx + qy, c)
                cp = _remote(got, got, send.at[3 * m + k], recv.at[3 * m + k], sib)
                cp.start()
                cps.append(cp)
        for m, name in enumerate(names):
            kind, shard, _ = geom[name]
            for k, (qx, qy) in enumerate(_other_chips(x, y)):
                theirs = _piece_of(arr[m], kind, shard, 2 * qx + qy, 1 - c)
                _remote(theirs, theirs, send.at[3 * m + k], recv.at[3 * m + k], sib).wait_recv()
        for cp in cps:
            cp.wait_send()

    return pl.pallas_call(
        body, name="gather_forward_" + names[0], in_specs=[ANY] * nb, out_specs=[ANY] * nb,
        out_shape=[jax.ShapeDtypeStruct(a.shape, a.dtype) for a in full],
        input_output_aliases={m: m for m in range(nb)},
        scratch_shapes=[pltpu.SemaphoreType.DMA((3 * nb,)), pltpu.SemaphoreType.DMA((3 * nb,))],
    )(*full)


def _scatter_copies(nb, t, got, send, recv):
    x, y, c = _me()
    return [_remote(t[m].at[2 * qx + qy], got[m].at[k], send.at[3 * m + k], recv.at[3 * m + k], (qx, qy, c))
            for m in range(nb) for k, (qx, qy) in enumerate(_other_chips(x, y))]


def _chip_exchange_start(names, chip_sums, geom, after):
    nb = len(names)
    lands = [lax.empty((N_CHIPS - 1,) + geom[n][2], BF16) for n in names]

    def body(*refs):
        t, got = refs[:nb], refs[nb:2 * nb]
        send, recv = refs[2 * nb + 1], refs[2 * nb + 2]
        token = refs[4 * nb + 3]
        for cp in _scatter_copies(nb, t, got, send, recv):
            cp.start()
        token[...] = jnp.zeros_like(token)

    both = list(chip_sums) + lands
    outs = pl.pallas_call(
        body, name="grad_chip_start_" + names[0],
        out_shape=(pltpu.SemaphoreType.DMA((3 * nb,)), pltpu.SemaphoreType.DMA((3 * nb,)),
                   *[pltpu.HBM(a.shape, a.dtype) for a in both], TOKEN),
        in_specs=[HBM] * (2 * nb) + [ANY],
        out_specs=(SEM, SEM, *[HBM] * (2 * nb), pl.BlockSpec(memory_space=pltpu.VMEM)),
        input_output_aliases={m: 2 + m for m in range(2 * nb)},
        compiler_params=pltpu.CompilerParams(has_side_effects=EFFECT),
    )(*[_hbm(a) for a in both], after)
    return outs[0], outs[1], list(outs[2:2 + 2 * nb]), outs[2 + 2 * nb]


def _chip_exchange_wait(names, send, recv, thru, after):
    nb = len(names)

    def body(*refs):
        for cp in _scatter_copies(nb, refs[:nb], refs[nb:2 * nb], refs[2 * nb], refs[2 * nb + 1]):
            cp.wait_send()
            cp.wait_recv()

    outs = pl.pallas_call(
        body, name="grad_chip_wait_" + names[0], out_shape=tuple(pltpu.HBM(t.shape, t.dtype) for t in thru),
        in_specs=[HBM] * (2 * nb) + [SEM, SEM, ANY], out_specs=tuple([HBM] * (2 * nb)),
        input_output_aliases={m: m for m in range(2 * nb)},
        compiler_params=pltpu.CompilerParams(has_side_effects=EFFECT),
    )(*thru, send, recv, after)
    return list(outs[:nb]), list(outs[nb:])


def _sibling_copies(names, g, got, geom, send, recv):
    x, y, c = _me()
    out = []
    for m, name in enumerate(names):
        kind, shard, _ = geom[name]
        for r in range(N_CHIPS):
            out.append(_remote(_piece_of(g[m], kind, shard, r, 1 - c), got[m].at[r],
                               send.at[N_CHIPS * m + r], recv.at[N_CHIPS * m + r], (x, y, 1 - c)))
    return out


def _sibling_exchange_start(names, grads, geom, after):
    nb = len(names)
    lands = [lax.empty((N_CHIPS,) + geom[n][2], BF16) for n in names]

    def body(*refs):
        for cp in _sibling_copies(names, refs[:nb], refs[nb:2 * nb], geom, refs[2 * nb + 1], refs[2 * nb + 2]):
            cp.start()
        token = refs[4 * nb + 3]
        token[...] = jnp.zeros_like(token)

    both = [grads[n] for n in names] + lands
    outs = pl.pallas_call(
        body, name="grad_sibling_start_" + names[0],
        out_shape=(pltpu.SemaphoreType.DMA((N_CHIPS * nb,)), pltpu.SemaphoreType.DMA((N_CHIPS * nb,)),
                   *[pltpu.HBM(a.shape, a.dtype) for a in both], TOKEN),
        in_specs=[HBM] * (2 * nb) + [ANY],
        out_specs=(SEM, SEM, *[HBM] * (2 * nb), pl.BlockSpec(memory_space=pltpu.VMEM)),
        input_output_aliases={m: 2 + m for m in range(2 * nb)},
        compiler_params=pltpu.CompilerParams(has_side_effects=EFFECT),
    )(*[_hbm(a) for a in both], after)
    return outs[0], outs[1], list(outs[2:2 + 2 * nb]), outs[2 + 2 * nb]


def _sibling_exchange_wait(names, send, recv, thru, geom, after):
    nb = len(names)

    def body(*refs):
        for cp in _sibling_copies(names, refs[:nb], refs[nb:2 * nb], geom, refs[2 * nb], refs[2 * nb + 1]):
            cp.wait_send()
            cp.wait_recv()

    outs = pl.pallas_call(
        body, name="grad_sibling_wait_" + names[0], out_shape=tuple(pltpu.HBM(t.shape, t.dtype) for t in thru),
        in_specs=[HBM] * (2 * nb) + [SEM, SEM, ANY], out_specs=tuple([HBM] * (2 * nb)),
        input_output_aliases={m: m for m in range(2 * nb)},
        compiler_params=pltpu.CompilerParams(has_side_effects=EFFECT),
    )(*thru, send, recv, after)
    return list(outs[:nb]), list(outs[nb:])


def _allreduce_small(part):
    rows = part.shape[0]
    flips = [(a, b, e) for a in (0, 1) for b in (0, 1) for e in (0, 1) if (a, b, e) != (0, 0, 0)]

    def body(p_ref, o_ref, all_ref, send, recv):
        x, y, c = _me()
        me = 4 * x + 2 * y + c
        all_ref[me] = p_ref[...]
        cps = []
        for k, (a, b, e) in enumerate(flips):
            cp = _remote(p_ref, all_ref.at[me], send.at[k], recv.at[k], (x ^ a, y ^ b, c ^ e))
            cp.start()
            cps.append(cp)
        for k, (a, b, e) in enumerate(flips):
            peer = 4 * (x ^ a) + 2 * (y ^ b) + (c ^ e)
            _remote(p_ref, all_ref.at[peer], send.at[k], recv.at[k], (x ^ a, y ^ b, c ^ e)).wait_recv()
        for cp in cps:
            cp.wait_send()
        tot = all_ref[0]
        for dev in range(1, 8):
            tot = tot + all_ref[dev]
        o_ref[...] = tot

    vm = pl.BlockSpec(memory_space=pltpu.VMEM)
    return pl.pallas_call(
        body, name="allreduce_small", in_specs=[vm], out_specs=vm, out_shape=jax.ShapeDtypeStruct((rows, LANES), F32),
        scratch_shapes=[pltpu.VMEM((8, rows, LANES), F32), pltpu.SemaphoreType.DMA((7,)), pltpu.SemaphoreType.DMA((7,))],
    )(part)


def _rows_tile(rows, cols, mult, elems=1 << 19):
    return _pick(rows, max(mult, elems // cols // mult * mult), mult)


ADD_TILE = 1 << 20


def _add_pairs(g, got, kind, shard, where, name):
    p, r, c = got.shape
    tr = _rows_tile(r, c, 16, ADD_TILE)
    nt = r // tr

    def body(w_ref, a_ref, b_ref, o_ref):
        o_ref[...] = (a_ref[...].astype(F32) + b_ref[...].astype(F32)).astype(BF16)

    if kind == "row":
        g_map = lambda q, i, w_ref: ((2 * q + w_ref[1]) * nt + i, 0)
    else:
        g_map = lambda q, i, w_ref: (w_ref[1] * nt + i, q)
    spec = pl.BlockSpec((None, tr, c), lambda q, i, w_ref: (q, i, 0))
    return pl.pallas_call(
        body, name="grad_add_sibling_" + name,
        grid_spec=pltpu.PrefetchScalarGridSpec(
            num_scalar_prefetch=1, grid=(p, nt), in_specs=[pl.BlockSpec((tr, c), g_map), spec], out_specs=spec),
        out_shape=jax.ShapeDtypeStruct((p, r, c), BF16), compiler_params=_cp(("arbitrary", "arbitrary"), 32),
    )(where, g, got)


def _add_four(t, got, shard, where, name):
    _, r, c = t.shape
    tr = _rows_tile(r, c, 16, ADD_TILE)
    nt = r // tr

    def body(w_ref, own, t0, t1, t2, o_ref):
        o_ref[...] = ((own[...].astype(F32) + t0[...].astype(F32)) + t1[...].astype(F32)) + t2[...].astype(F32)

    spec = lambda q: pl.BlockSpec((None, tr, c), lambda i, w_ref: (q, i, 0))
    return pl.pallas_call(
        body, name="grad_add_chips_" + name,
        grid_spec=pltpu.PrefetchScalarGridSpec(
            num_scalar_prefetch=1, grid=(nt,),
            in_specs=[pl.BlockSpec((None, tr, c), lambda i, w_ref: (w_ref[0], i, 0)), spec(0), spec(1), spec(2)],
            out_specs=pl.BlockSpec((tr, c), lambda i, w_ref: (w_ref[1] * nt + i, 0))),
        out_shape=jax.ShapeDtypeStruct(shard, F32), compiler_params=_cp(("arbitrary",), 48),
    )(where, t, got, got, got)


def _adamw(w, g, m, v, name):
    r, c = w.shape
    tr = _rows_tile(r, c, 8)

    def body(w_ref, g_ref, m_ref, v_ref, go_ref, d_ref, mo_ref, vo_ref):
        gv = g_ref[...]
        mn = ADAM_B1 * m_ref[...] + (1.0 - ADAM_B1) * gv
        vn = ADAM_B2 * v_ref[...] + (1.0 - ADAM_B2) * (gv * gv)
        m_hat = mn / (1.0 - ADAM_B1 ** ADAM_STEP)
        v_hat = vn / (1.0 - ADAM_B2 ** ADAM_STEP)
        go_ref[...] = gv
        d_ref[...] = -ADAM_LR * (m_hat / (jnp.sqrt(v_hat) + ADAM_EPS) + ADAM_WD * w_ref[...])
        mo_ref[...] = mn
        vo_ref[...] = vn

    spec = pl.BlockSpec((tr, c), lambda i: (i, 0))
    return pl.pallas_call(
        body, name=name, grid=(r // tr,), in_specs=[spec] * 4, out_specs=[spec] * 4,
        out_shape=[jax.ShapeDtypeStruct((r, c), F32)] * 4, compiler_params=_cp(("arbitrary",), 32),
    )(w, g, m, v)


def _pack(vectors, rows):
    flat = jnp.concatenate([v.reshape(-1).astype(F32) for v in vectors])
    return jnp.pad(flat, (0, rows * LANES - flat.shape[0])).reshape(rows, LANES)


def _unpack(packed, shapes):
    flat = packed.reshape(-1)
    out, off = [], 0
    for shp in shapes:
        n = 1
        for t in shp:
            n *= t
        out.append(flat[off:off + n].reshape(shp))
        off += n
    return out


def _rows_for(shapes):
    n = sum(functools.reduce(lambda a, b: a * b, shp, 1) for shp in shapes)
    return -(-n // (8 * LANES)) * 8


def kernel(x, w_in, attn_sinks, short_conv_w, w_out, ln1_g, ln1_b, ffn_w_up, ffn_conv_w, ffn_w_down, ln2_g, ln2_b, loss_target, m_w_in, m_attn_sinks, m_short_conv_w, m_w_out, m_ln1_g, m_ln1_b, m_ffn_w_up, m_ffn_conv_w, m_ffn_w_down, m_ln2_g, m_ln2_b, v_w_in, v_attn_sinks, v_short_conv_w, v_w_out, v_ln1_g, v_ln1_b, v_ffn_w_up, v_ffn_conv_w, v_ffn_w_down, v_ln2_g, v_ln2_b):
    xs, tgt = x[0], loss_target[0]
    d = xs.shape[1]
    chip = 2 * lax.axis_index("x") + lax.axis_index("y")

    w_big = dict(win_t=w_in[0], wout=w_out[0], wup=ffn_w_up[0], wdown=ffn_w_down[0])
    m_big = dict(win_t=m_w_in[0], wout=m_w_out[0], wup=m_ffn_w_up[0], wdown=m_ffn_w_down[0])
    v_big = dict(win_t=v_w_in[0], wout=v_w_out[0], wup=v_ffn_w_up[0], wdown=v_ffn_w_down[0])
    to_place = dict(w_big, win_t=w_in[0].T)
    geom = _geom({n: to_place[n].shape for n in BIG})
    pad8 = lambda a: jnp.pad(a[0], ((0, 5), (0, 0)))
    where = jnp.stack([chip, lax.axis_index("c")]).astype(jnp.int32)
    placed = {n: _place_shard(to_place[n], where[:1], geom[n][0], n) for n in BIG}
    full, (scw8, fcw8) = _allgather_weights(W_IN, placed, geom, [pad8(short_conv_w), pad8(ffn_conv_w)])
    o_send, o_recv, o_thru, o_token = _gather_start(W_OUT, placed, geom, scw8)
    send, recv, thru, token = _gather_start(FFN, placed, geom, o_token)

    def wout_behind(attn):
        return _gather_forward(W_OUT, _gather_wait(W_OUT, o_send, o_recv, o_thru, geom, attn), geom)[0]

    a = _phase_mixer(xs, full["win_t"], wout_behind, scw8, attn_sinks, ln1_g, ln1_b, after=token)
    full["wout"] = a["wout"]
    landed = _gather_forward(FFN, _gather_wait(FFN, send, recv, thru, geom, a["x1b"]), geom)
    full.update(zip(FFN, landed))
    f = _phase_ffn(a, tgt, full["wup"], full["wdown"], fcw8, ln2_g, ln2_b)

    def add_pairs(names, grads, from_sibling):
        return [_add_pairs(grads[m], from_sibling[m], geom[n][0], geom[n][1], where, n) for m, n in enumerate(names)]

    sib_send, sib_recv, sib_thru, sib_token = _sibling_exchange_start(FFN, f, geom, f["st2"])
    started = {}

    def between(dz1b):
        grads, from_sibling = _sibling_exchange_wait(FFN, sib_send, sib_recv, sib_thru, geom, dz1b)
        started["sums"] = add_pairs(FFN, grads, from_sibling)
        started["chip"] = _chip_exchange_start(FFN, started["sums"], geom, f["st2"])
        return started["chip"][3]

    (dproj, dz1b), g_mixer, g_small = _phase_rest(a, f, full["wup"], full["wout"], full["win_t"], scw8, attn_sinks,
                                                  ln1_g + sib_token[0:1, 0:1], between=between)
    send, recv, thru, _ = started["chip"]

    def finish(names, exchanged):
        sums, from_chips = exchanged
        halves = [_add_four(sums[m], from_chips[m], geom[n][1], where, n) for m, n in enumerate(names)]
        shards = _sibling_assemble(names, halves, geom)
        grads = {n: shards[m].T if n == "win_t" else shards[m] for m, n in enumerate(names)}
        return {n: _adamw(w_big[n], grads[n], m_big[n], v_big[n], "adamw_" + n) for n in names}

    mixer_sums = add_pairs(MIXER, [g_mixer[n] for n in MIXER], _sibling_exchange(MIXER, g_mixer, geom))
    send2, recv2, thru2, token2 = _chip_exchange_start(MIXER, mixer_sums, geom, f["st2"])
    grad_x = _grad_x(dproj, dz1b, full["win_t"], after=token2)
    upd = finish(FFN, _chip_exchange_wait(FFN, send, recv, thru, grad_x))
    upd.update(finish(MIXER, _chip_exchange_wait(MIXER, send2, recv2, thru2, upd[FFN[0]][1])))

    small_names = ("ln1_g", "ln1_b", "ln2_g", "ln2_b", "sinks", "fcw", "scw")
    small_shapes = [g_small[n].shape for n in small_names]
    red = _allreduce_small(_pack([g_small["loss_sq"].reshape(1)] + [g_small[n] for n in small_names],
                                 _rows_for([(1,)] + small_shapes)))
    loss_sq, *gs = _unpack(red, [(1,)] + small_shapes)
    gs = dict(zip(small_names, gs))
    loss = (0.5 / d) * loss_sq[0]
    fw, sw = ffn_conv_w.shape[2], short_conv_w.shape[2]
    gs["fcw"] = lax.dynamic_slice_in_dim(gs["fcw"], chip * fw, fw, axis=1)
    gs["scw"] = lax.dynamic_slice_in_dim(gs["scw"], chip * sw, sw, axis=1)

    sm_w = dict(ln1_g=ln1_g[0], ln1_b=ln1_b[0], ln2_g=ln2_g[0], ln2_b=ln2_b[0], sinks=attn_sinks[0],
                fcw=ffn_conv_w[0], scw=short_conv_w[0])
    sm_m = dict(ln1_g=m_ln1_g[0], ln1_b=m_ln1_b[0], ln2_g=m_ln2_g[0], ln2_b=m_ln2_b[0], sinks=m_attn_sinks[0],
                fcw=m_ffn_conv_w[0], scw=m_short_conv_w[0])
    sm_v = dict(ln1_g=v_ln1_g[0], ln1_b=v_ln1_b[0], ln2_g=v_ln2_g[0], ln2_b=v_ln2_b[0], sinks=v_attn_sinks[0],
                fcw=v_ffn_conv_w[0], scw=v_short_conv_w[0])
    shapes = [sm_w[n].shape for n in small_names]
    rows = _rows_for(shapes)
    packed = [_pack([t[n] for n in small_names], rows) for t in (sm_w, gs, sm_m, sm_v)]
    sm_out = [dict(zip(small_names, _unpack(a, shapes))) for a in _adamw(*packed, "adamw_small")]

    def leaf(kind, name):
        if name in ("w_in", "w_out", "ffn_w_up", "ffn_w_down"):
            key = dict(w_in="win_t", w_out="wout", ffn_w_up="wup", ffn_w_down="wdown")[name]
            return upd[key][kind][None]
        key = dict(attn_sinks="sinks", short_conv_w="scw", ffn_conv_w="fcw").get(name, name)
        return sm_out[kind][key][None]

    order = ("w_in", "attn_sinks", "short_conv_w", "w_out", "ln1_g", "ln1_b", "ffn_w_up", "ffn_conv_w", "ffn_w_down",
             "ln2_g", "ln2_b")
    outs = [loss, grad_x[None]]
    for kind in range(4):
        outs += [leaf(kind, n) for n in order]
    return tuple(outs)
```

```python
import functools

import jax
import jax.numpy as jnp
from jax import lax
from jax.experimental import pallas as pl
from jax.experimental.pallas import tpu as pltpu

F32 = jnp.float32
BF16 = jnp.bfloat16
MESH = pl.DeviceIdType.MESH
ANY = pl.BlockSpec(memory_space=pl.ANY)

HEAD_DIM = 64
N_Q_HEADS = 16
N_KV_HEADS = 2
ATTN_WIDTH = N_Q_HEADS * HEAD_DIM
KV_WIDTH = N_KV_HEADS * HEAD_DIM
BLOCK = 128
ROPE_THETA = 10000.0
LN_EPS = 1e-5
ALPHA = 2.0 ** 0.25
NEG_INF = -1e30
ADAM_LR, ADAM_B1, ADAM_B2, ADAM_EPS, ADAM_WD, ADAM_STEP = 0.001, 0.9, 0.999, 1e-08, 0.01, 10
N_CHIPS = 4
LANES = 128
MXU_DIM = 256
SLAB = 128


def _cp(sem, vmem_mb):
    return pltpu.CompilerParams(dimension_semantics=sem, vmem_limit_bytes=vmem_mb << 20)


def _matmul(a, b, *, mode, m, n, k, tm, tn, tk, out_dtype, name, vmem_mb, a_spec=None, b_spec=None,
            res=None, alpha=1.0, after=None, m_outer=False):
    nj, ni, nk = n // tn, m // tm, k // tk
    assert nj * tn == n and ni * tm == m and nk * tk == k, (name, m, n, k, tm, tn, tk)
    if mode == "nn":
        dims = ((1,), (0,))
        a_spec = a_spec or pl.BlockSpec((tm, tk), lambda j, i, kk: (i, kk))
        b_spec = b_spec or pl.BlockSpec((tk, tn), lambda j, i, kk: (kk, j))
    elif mode == "nt":
        dims = ((1,), (1,))
        a_spec = a_spec or pl.BlockSpec((tm, tk), lambda j, i, kk: (i, kk))
        b_spec = b_spec or pl.BlockSpec((tn, tk), lambda j, i, kk: (j, kk))
    else:
        dims = ((0,), (0,))
        a_spec = a_spec or pl.BlockSpec((tk, tm), lambda j, i, kk: (kk, i))
        b_spec = b_spec or pl.BlockSpec((tk, tn), lambda j, i, kk: (kk, j))
    has_res = res is not None
    has_after = after is not None

    def body(*refs):
        refs = refs[1:] if has_after else refs
        a_ref, b_ref = refs[0], refs[1]
        res_ref = refs[2] if has_res else None
        o_ref = refs[2 + has_res]
        part = lax.dot_general(a_ref[...].astype(BF16), b_ref[...].astype(BF16), (dims, ((), ())),
                               preferred_element_type=F32)

        def finish(acc):
            if has_res:
                acc = acc + alpha * res_ref[...].astype(F32)
            o_ref[...] = acc.astype(o_ref.dtype)

        if nk == 1:
            finish(part)
        else:
            acc_ref = refs[3 + has_res]
            kk = pl.program_id(2)

            @pl.when(kk == 0)
            def _():
                acc_ref[...] = part

            @pl.when(kk > 0)
            def _():
                acc_ref[...] += part

            @pl.when(kk == nk - 1)
            def _():
                finish(acc_ref[...])

    in_specs = [a_spec, b_spec]
    args = [a, b]
    if has_res:
        in_specs.append(pl.BlockSpec((tm, tn), lambda j, i, kk: (i, j)))
        args.append(res)
    if has_after:
        in_specs.insert(0, pl.BlockSpec(after.shape, lambda j, i, kk: (0, 0)))
        args.insert(0, after)
    out_spec = pl.BlockSpec((tm, tn), lambda j, i, kk: (i, j))
    grid = (nj, ni, nk)
    if m_outer:
        swap = lambda sp: pl.BlockSpec(sp.block_shape, (lambda f: lambda i, j, kk: f(j, i, kk))(sp.index_map),
                                       pipeline_mode=sp.pipeline_mode)
        in_specs, out_spec, grid = [swap(sp) for sp in in_specs], swap(out_spec), (ni, nj, nk)
    return pl.pallas_call(
        body, name=name, grid=grid, in_specs=in_specs,
        out_specs=out_spec,
        out_shape=jax.ShapeDtypeStruct((m, n), out_dtype),
        scratch_shapes=[pltpu.VMEM((tm, tn), F32)] if nk > 1 else [],
        compiler_params=_cp(("arbitrary", "arbitrary", "arbitrary"), vmem_mb),
    )(*args)


def _pick(total, want, mult):
    if total <= want:
        return total
    for t in range(want, 0, -1):
        if total % t == 0 and t % mult == 0:
            return t
    return total


def _rope_tables(s):
    half = HEAD_DIM // 2
    inv_freq = ROPE_THETA ** (-jnp.arange(half, dtype=F32) / half)
    ang = jnp.arange(s, dtype=F32)[:, None] * inv_freq[None, :]
    cos = jnp.tile(jnp.cos(ang), (1, LANES // half))
    sin = jnp.tile(jnp.concatenate([-jnp.sin(ang), jnp.sin(ang)], axis=1), (1, LANES // HEAD_DIM))
    return cos, sin


def _rope(x, cos, sin, lo):
    partner = jnp.where(lo, pltpu.roll(x, LANES - HEAD_DIM // 2, 1), pltpu.roll(x, HEAD_DIM // 2, 1))
    return x * cos + partner * sin


def _dot(a, b, dims):
    return lax.dot_general(a, b, (dims, ((), ())), preferred_element_type=F32)


NN, NT, TN = ((1,), (0,)), ((1,), (1,)), ((0,), (0,))


def _kv_variants(t, head_lo):
    r = pltpu.roll(t, HEAD_DIM, 1)
    zero = jnp.zeros_like(t)
    a = (jnp.where(head_lo, t, zero).astype(BF16), jnp.where(head_lo, r, zero).astype(BF16))
    b = (jnp.where(head_lo, zero, r).astype(BF16), jnp.where(head_lo, zero, t).astype(BF16))
    return a, b


PAIRS_PER_KV = N_Q_HEADS // 2 // N_KV_HEADS
STACK = PAIRS_PER_KV * BLOCK


def _stack_pairs(ref, j, fn):
    return jnp.concatenate([fn(ref[:, p * LANES:(p + 1) * LANES])
                            for p in range(j * PAIRS_PER_KV, (j + 1) * PAIRS_PER_KV)], axis=0)


def _sink_row(sink_ref, j, hh):
    col = lax.broadcasted_iota(jnp.int32, (1, STACK), 1)
    heads = [2 * p + hh for p in range(j * PAIRS_PER_KV, (j + 1) * PAIRS_PER_KV)]
    row = jnp.full((1, STACK), sink_ref[0, heads[-1]], F32)
    for t in range(PAIRS_PER_KV - 2, -1, -1):
        row = jnp.where(col < (t + 1) * BLOCK, sink_ref[0, heads[t]], row)
    return row


def _attn_exps(qp, ka, kb, valid, sink_a, sink_b):
    out = []
    for kk, sink in ((ka, sink_a), (kb, sink_b)):
        s = jnp.where(valid, _dot(kk, qp, NT), NEG_INF)
        mx = jnp.maximum(jnp.max(s, axis=0, keepdims=True), sink)
        out.append((jnp.exp(s - mx), jnp.exp(sink - mx)))
    return out


def _attn_common(i, q_ref, k_ref, v_ref, kp_ref, vp_ref, cos_ref, sin_ref, cosp_ref, sinp_ref):
    lane = lax.broadcasted_iota(jnp.int32, (1, LANES), 1)
    lo = (lane % HEAD_DIM) < (HEAD_DIM // 2)
    head_lo = lane < HEAD_DIM
    cos, sin = cos_ref[...], sin_ref[...]
    kc = _rope(k_ref[...].astype(F32), cos, sin, lo)
    kp = _rope(kp_ref[...].astype(F32), cosp_ref[...], sinp_ref[...], lo)
    kext = jnp.concatenate([kp, kc], axis=0)
    vext = jnp.concatenate([vp_ref[...].astype(F32), v_ref[...].astype(F32)], axis=0)
    ka, kb = _kv_variants(kext, head_lo)
    va, vb = _kv_variants(vext, head_lo)
    qi = lax.broadcasted_iota(jnp.int32, (1, STACK), 1) % BLOCK
    kj = lax.broadcasted_iota(jnp.int32, (2 * BLOCK, 1), 0)
    valid = (kj > qi) & (kj <= qi + BLOCK) & ((kj >= BLOCK) | (i > 0))
    cos4 = jnp.concatenate([cos] * PAIRS_PER_KV, axis=0)
    sin4 = jnp.concatenate([sin] * PAIRS_PER_KV, axis=0)
    return lo, head_lo, cos, sin, cos4, sin4, ka, kb, va, vb, valid


def _attn_fwd(proj, sinks, cos, sin, s):
    nb = s // BLOCK
    kcol, vcol = ATTN_WIDTH // LANES, ATTN_WIDTH // LANES + 1

    def body(q_ref, k_ref, v_ref, kp_ref, vp_ref, cos_ref, sin_ref, cosp_ref, sinp_ref, sink_ref, o_ref):
        i = pl.program_id(0)
        lo, head_lo, _, _, cs4, sn4, ka, kb, va, vb, valid = _attn_common(
            i, q_ref, k_ref, v_ref, kp_ref, vp_ref, cos_ref, sin_ref, cosp_ref, sinp_ref)
        row = lax.broadcasted_iota(jnp.int32, (16, 1), 0)
        one = jnp.ones((), BF16)
        for j in range(N_KV_HEADS):
            q4 = _stack_pairs(q_ref, j, lambda t: t.astype(F32))
            qp = (_rope(q4, cs4, sn4, lo) * HEAD_DIM ** -0.5).astype(BF16)
            exps = _attn_exps(qp, ka[j], kb[j], valid, _sink_row(sink_ref, j, 0), _sink_row(sink_ref, j, 1))
            outs = []
            for (e, es), vv, mine in zip(exps, (va[j], vb[j]), (head_lo, ~head_lo)):
                ee = jnp.concatenate([e.astype(BF16), jnp.where(row == 0, es, 0.0).astype(BF16)], axis=0)
                tail = jnp.where((row == 0) & ~mine, one, jnp.zeros((), BF16))
                vx = jnp.concatenate([jnp.where(mine, vv, one), tail], axis=0)
                un = _dot(ee, vx, TN)
                outs.append(un / pltpu.roll(un, HEAD_DIM, 1))
            o = jnp.where(head_lo, outs[0], outs[1]).astype(BF16)
            for t in range(PAIRS_PER_KV):
                p = j * PAIRS_PER_KV + t
                o_ref[:, p * LANES:(p + 1) * LANES] = o[t * BLOCK:(t + 1) * BLOCK]

    prev = lambda i: (jnp.maximum(i - 1, 0), 0)
    return pl.pallas_call(
        body, name="attn_fwd", grid=(nb,),
        in_specs=[pl.BlockSpec((BLOCK, ATTN_WIDTH), lambda i: (i, 0)),
                  pl.BlockSpec((BLOCK, LANES), lambda i: (i, kcol)),
                  pl.BlockSpec((BLOCK, LANES), lambda i: (i, vcol)),
                  pl.BlockSpec((BLOCK, LANES), lambda i: (jnp.maximum(i - 1, 0), kcol)),
                  pl.BlockSpec((BLOCK, LANES), lambda i: (jnp.maximum(i - 1, 0), vcol)),
                  pl.BlockSpec((BLOCK, LANES), lambda i: (i, 0)),
                  pl.BlockSpec((BLOCK, LANES), lambda i: (i, 0)),
                  pl.BlockSpec((BLOCK, LANES), prev),
                  pl.BlockSpec((BLOCK, LANES), prev),
                  pl.BlockSpec(memory_space=pltpu.SMEM)],
        out_specs=pl.BlockSpec((BLOCK, ATTN_WIDTH), lambda i: (i, 0)),
        out_shape=jax.ShapeDtypeStruct((s, ATTN_WIDTH), BF16),
        compiler_params=_cp(("arbitrary",), 32),
    )(proj, proj, proj, proj, proj, cos, sin, cos, sin, sinks)


def _attn_bwd(proj, dmix, sinks, cos, sin, s):
    nb = s // BLOCK
    kcol, vcol = ATTN_WIDTH // LANES, ATTN_WIDTH // LANES + 1

    def body(q_ref, k_ref, v_ref, kp_ref, vp_ref, cos_ref, sin_ref, cosp_ref, sinp_ref, sink_ref, do_ref,
             dq_ref, dk_ref, dv_ref, dsink_ref, ck_ref, cv_ref):
        g = pl.program_id(0)
        i = nb - 1 - g

        @pl.when(g == 0)
        def _():
            ck_ref[...] = jnp.zeros_like(ck_ref)
            cv_ref[...] = jnp.zeros_like(cv_ref)
            dsink_ref[...] = jnp.zeros_like(dsink_ref)

        lo, head_lo, cs, sn, cs4, sn4, ka, kb, va, vb, valid = _attn_common(
            i, q_ref, k_ref, v_ref, kp_ref, vp_ref, cos_ref, sin_ref, cosp_ref, sinp_ref)
        lane = lax.broadcasted_iota(jnp.int32, (1, LANES), 1)
        dk_j, dv_j = [], []
        dsink = jnp.zeros((1, LANES), F32)
        for j in range(N_KV_HEADS):
            q4 = _stack_pairs(q_ref, j, lambda t: t.astype(F32))
            qp = (_rope(q4, cs4, sn4, lo) * HEAD_DIM ** -0.5).astype(BF16)
            exps = _attn_exps(qp, ka[j], kb[j], valid, _sink_row(sink_ref, j, 0), _sink_row(sink_ref, j, 1))
            do = _stack_pairs(do_ref, j, lambda t: t)
            dq_r = jnp.zeros((STACK, LANES), F32)
            dkc, dvc = [], []
            for hh, ((e, es), kk, vv) in enumerate(zip(exps, (ka[j], kb[j]), (va[j], vb[j]))):
                inv = 1.0 / (jnp.sum(e, axis=0, keepdims=True) + es)
                pr = e * inv
                dp = _dot(vv, do, NT)
                delta = jnp.sum(pr * dp, axis=0, keepdims=True)
                ds = (pr * (dp - delta)).astype(BF16)
                psd = es * inv * delta
                for t in range(PAIRS_PER_KV):
                    head = 2 * (j * PAIRS_PER_KV + t) + hh
                    dsink = dsink + jnp.where(
                        lane == head, -jnp.sum(psd[:, t * BLOCK:(t + 1) * BLOCK], axis=1, keepdims=True), 0.0)
                dq_r = dq_r + _dot(ds, kk, TN)
                dkc.append(_dot(ds, qp, NN))
                dvc.append(_dot(pr.astype(BF16), do, NN))
            dk_j.append(jnp.where(head_lo, dkc[0], dkc[1]))
            dv_j.append(jnp.where(head_lo, dvc[0], dvc[1]))
            dq = _rope(dq_r * HEAD_DIM ** -0.5, cs4, -sn4, lo).astype(BF16)
            for t in range(PAIRS_PER_KV):
                p = j * PAIRS_PER_KV + t
                dq_ref[:, p * LANES:(p + 1) * LANES] = dq[t * BLOCK:(t + 1) * BLOCK]
        tot_k = [t + pltpu.roll(t, HEAD_DIM, 1) for t in dk_j]
        tot_v = [t + pltpu.roll(t, HEAD_DIM, 1) for t in dv_j]
        dkext = jnp.where(head_lo, tot_k[0], tot_k[1])
        dvext = jnp.where(head_lo, tot_v[0], tot_v[1])
        dk_r = dkext[BLOCK:] + ck_ref[...]
        dk_ref[...] = _rope(dk_r, cs, -sn, lo).astype(BF16)
        dv_ref[...] = (dvext[BLOCK:] + cv_ref[...]).astype(BF16)
        ck_ref[...] = dkext[:BLOCK]
        cv_ref[...] = dvext[:BLOCK]
        dsink_ref[0:1, :] += dsink

    cur = lambda col: (lambda g: (nb - 1 - g, col))
    prv = lambda col: (lambda g: (jnp.maximum(nb - 2 - g, 0), col))
    blk = lambda w, f: pl.BlockSpec((BLOCK, w), f)
    return pl.pallas_call(
        body, name="attn_bwd", grid=(nb,),
        in_specs=[blk(ATTN_WIDTH, cur(0)), blk(LANES, cur(kcol)), blk(LANES, cur(vcol)),
                  blk(LANES, prv(kcol)), blk(LANES, prv(vcol)),
                  blk(LANES, cur(0)), blk(LANES, cur(0)), blk(LANES, prv(0)), blk(LANES, prv(0)),
                  pl.BlockSpec(memory_space=pltpu.SMEM),
                  blk(ATTN_WIDTH, cur(0))],
        out_specs=[blk(ATTN_WIDTH, cur(0)), blk(LANES, cur(0)), blk(LANES, cur(0)),
                   pl.BlockSpec((8, LANES), lambda g: (0, 0))],
        out_shape=[jax.ShapeDtypeStruct((s, ATTN_WIDTH), BF16), jax.ShapeDtypeStruct((s, LANES), BF16),
                   jax.ShapeDtypeStruct((s, LANES), BF16), jax.ShapeDtypeStruct((8, LANES), F32)],
        scratch_shapes=[pltpu.VMEM((BLOCK, LANES), F32), pltpu.VMEM((BLOCK, LANES), F32)],
        compiler_params=_cp(("arbitrary",), 32),
    )(proj, proj, proj, proj, proj, cos, sin, cos, sin, sinks, dmix)


def _causal_conv(x, prev8, w):
    row = lax.broadcasted_iota(jnp.int32, (8, 1), 0)
    r1, r2 = pltpu.roll(x, 1, 0), pltpu.roll(x, 2, 0)
    s1 = jnp.concatenate([jnp.where(row == 0, prev8[7:8], r1[:8]), r1[8:]], axis=0)
    s2 = jnp.concatenate([jnp.where(row == 0, prev8[6:7], jnp.where(row == 1, prev8[7:8], r2[:8])), r2[8:]], axis=0)
    return w[0:1] * s2 + w[1:2] * s1 + w[2:3] * x


def _conv_bwd(dy, x, w, next8):
    r = x.shape[0]
    row = lax.broadcasted_iota(jnp.int32, (8, 1), 0)
    r1, r2 = pltpu.roll(dy, r - 1, 0), pltpu.roll(dy, r - 2, 0)
    n1 = jnp.concatenate([r1[:r - 8], jnp.where(row == 7, next8[0:1], r1[r - 8:])], axis=0)
    n2 = jnp.concatenate([r2[:r - 8], jnp.where(row == 6, next8[0:1], jnp.where(row == 7, next8[1:2], r2[r - 8:]))],
                         axis=0)
    dx = w[2:3] * dy + w[1:2] * n1 + w[0:1] * n2
    dws = [jnp.sum(t * x, axis=0, keepdims=True) for t in (n2, n1, dy)]
    return dx, dws


CONV_COLS = 256


def _convmix_cols(d):
    conv_w = d - ATTN_WIDTH
    base = (ATTN_WIDTH + 2 * KV_WIDTH) // CONV_COLS
    step = conv_w // CONV_COLS
    return base, base + step, base + 2 * step, step


def _convmix_fwd(proj, scw8, s, d):
    gb0, gc0, h0, ncb = _convmix_cols(d)
    tr = _pick(s, 1024, 16)
    ni = s // tr

    def body(gb_ref, gc_ref, h_ref, w_ref, o_ref, carry_ref):
        @pl.when(pl.program_id(1) == 0)
        def _():
            carry_ref[...] = jnp.zeros_like(carry_ref)

        gch = gc_ref[...].astype(F32) * h_ref[...].astype(F32)
        cc = _causal_conv(gch, carry_ref[...], w_ref[...])
        o_ref[...] = (gb_ref[...].astype(F32) * cc).astype(BF16)
        carry_ref[...] = gch[tr - 8:]

    spec = lambda c0: pl.BlockSpec((tr, CONV_COLS), lambda j, i: (i, c0 + j))
    return pl.pallas_call(
        body, name="convmix_fwd", grid=(ncb, ni),
        in_specs=[spec(gb0), spec(gc0), spec(h0), pl.BlockSpec((8, CONV_COLS), lambda j, i: (0, j))],
        out_specs=pl.BlockSpec((tr, CONV_COLS), lambda j, i: (i, j)),
        out_shape=jax.ShapeDtypeStruct((s, d - ATTN_WIDTH), BF16),
        scratch_shapes=[pltpu.VMEM((8, CONV_COLS), F32)],
        compiler_params=_cp(("arbitrary", "arbitrary"), 32),
    )(proj, proj, proj, scw8)


def _convmix_bwd(proj, dmix, scw8, s, d):
    gb0, gc0, h0, ncb = _convmix_cols(d)
    tr = _pick(s, 1024, 16)
    ni = s // tr
    dc0 = ATTN_WIDTH // CONV_COLS

    def body(dc_ref, gb_ref, gc_ref, h_ref, gcp_ref, hp_ref, w_ref, d3_ref, dw_ref, nxt_ref):
        g = pl.program_id(1)
        i = ni - 1 - g

        @pl.when(g == 0)
        def _():
            nxt_ref[...] = jnp.zeros_like(nxt_ref)
            dw_ref[...] = jnp.zeros_like(dw_ref)

        w = w_ref[...]
        gb, gc, h = gb_ref[...].astype(F32), gc_ref[...].astype(F32), h_ref[...].astype(F32)
        gch = gc * h
        prev8 = (gcp_ref[...].astype(F32) * hp_ref[...].astype(F32))[8:16] * (i > 0).astype(F32)
        cc = _causal_conv(gch, prev8, w)
        dc = dc_ref[...].astype(F32)
        dcc = dc * gb
        dgch, dws = _conv_bwd(dcc, gch, w, nxt_ref[...])
        d3_ref[0] = (dc * cc).astype(BF16)
        d3_ref[1] = (dgch * h).astype(BF16)
        d3_ref[2] = (dgch * gc).astype(BF16)
        for t in range(3):
            dw_ref[t:t + 1, :] += dws[t]
        nxt_ref[...] = dcc[0:8]

    cur = lambda c0: pl.BlockSpec((tr, CONV_COLS), lambda j, g: (ni - 1 - g, c0 + j))
    prv = lambda c0: pl.BlockSpec((16, CONV_COLS), lambda j, g: (jnp.maximum((ni - 1 - g) * (tr // 16) - 1, 0), c0 + j))
    return pl.pallas_call(
        body, name="convmix_bwd", grid=(ncb, ni),
        in_specs=[cur(dc0), cur(gb0), cur(gc0), cur(h0), prv(gc0), prv(h0),
                  pl.BlockSpec((8, CONV_COLS), lambda j, g: (0, j))],
        out_specs=[pl.BlockSpec((3, tr, CONV_COLS), lambda j, g: (0, ni - 1 - g, j)),
                   pl.BlockSpec((8, CONV_COLS), lambda j, g: (0, j))],
        out_shape=[jax.ShapeDtypeStruct((3, s, d - ATTN_WIDTH), BF16), jax.ShapeDtypeStruct((8, d - ATTN_WIDTH), F32)],
        scratch_shapes=[pltpu.VMEM((8, CONV_COLS), F32)],
        compiler_params=_cp(("arbitrary", "arbitrary"), 32),
    )(dmix, proj, proj, proj, proj, proj, scw8)


def _ln_fwd(z):
    mu = jnp.mean(z, axis=-1, keepdims=True)
    zc = z - mu
    var = jnp.mean(zc * zc, axis=-1, keepdims=True)
    rstd = lax.rsqrt(var + LN_EPS)
    return zc * rstd, rstd


def _ln_bwd(dout, xh, rstd, g):
    dxh = dout * g
    c1 = jnp.mean(dxh, axis=-1, keepdims=True)
    c2 = jnp.mean(dxh * xh, axis=-1, keepdims=True)
    dz = rstd * (dxh - c1 - xh * c2)
    return dz, jnp.sum(dout * xh, axis=0, keepdims=True), jnp.sum(dout, axis=0, keepdims=True)


def _outproj_ln1(attn, conv, wout, x, g1, b1, s, d):
    tm = _pick(s, 512, 16)
    ka = attn.shape[1]
    one_buffer = pl.Buffered(1)

    def body(a_ref, c_ref, wt_ref, wb_ref, x_ref, g_ref, b_ref, x1_ref, x1b_ref, xh_ref, rs_ref):
        y = _dot(a_ref[...], wt_ref[...], NN) + _dot(c_ref[...], wb_ref[...], NN)
        xh, rstd = _ln_fwd(ALPHA * x_ref[...] + y)
        x1 = xh * g_ref[...] + b_ref[...]
        x1_ref[...] = x1
        x1b_ref[...] = x1.astype(BF16)
        xh_ref[...] = xh.astype(BF16)
        rs_ref[...] = rstd

    row = lambda w: pl.BlockSpec((tm, w), lambda i: (i, 0))
    vec = pl.BlockSpec((1, d), lambda i: (0, 0))
    return pl.pallas_call(
        body, name="outproj_ln1", grid=(s // tm,),
        in_specs=[row(ka), row(d - ka), pl.BlockSpec((ka, d), lambda i: (0, 0), pipeline_mode=one_buffer),
                  pl.BlockSpec((d - ka, d), lambda i: (ka // (d - ka), 0), pipeline_mode=one_buffer), row(d), vec, vec],
        out_specs=[row(d), row(d), row(d), row(1)],
        out_shape=[jax.ShapeDtypeStruct((s, d), F32), jax.ShapeDtypeStruct((s, d), BF16),
                   jax.ShapeDtypeStruct((s, d), BF16), jax.ShapeDtypeStruct((s, 1), F32)],
        compiler_params=_cp(("arbitrary",), 56),
    )(attn, conv, wout, wout, x, g1, b1)


def _ffn_up(x1b, wup, fcw8, s, d, dff):
    tm = _pick(s, 1024, 16)
    tn = _pick(dff, 512, LANES)
    nj, ni = dff // tn, s // tm

    def body(x_ref, wa_ref, wg_ref, ca_ref, cg_ref, u_ref, y_ref, h_ref, carry_ref):
        @pl.when(pl.program_id(1) == 0)
        def _():
            carry_ref[...] = jnp.zeros_like(carry_ref)

        xa = x_ref[...]
        ys = []
        for part, (w_ref, c_ref) in enumerate(((wa_ref, ca_ref), (wg_ref, cg_ref))):
            u = _dot(xa, w_ref[...], NN)
            u_ref[part] = u.astype(BF16)
            y = _causal_conv(u, carry_ref[part], c_ref[...])
            carry_ref[part] = u[tm - 8:]
            y_ref[part] = y.astype(BF16)
            ys.append(y)
        a2, g2 = ys
        sig = 1.0 / (1.0 + jnp.exp(-a2))
        h_ref[...] = (a2 * sig * g2).astype(BF16)

    return pl.pallas_call(
        body, name="ffn_up", grid=(nj, ni),
        in_specs=[pl.BlockSpec((tm, d), lambda j, i: (i, 0)),
                  pl.BlockSpec((d, tn), lambda j, i: (0, j)),
                  pl.BlockSpec((d, tn), lambda j, i: (0, j + nj)),
                  pl.BlockSpec((8, tn), lambda j, i: (0, j)),
                  pl.BlockSpec((8, tn), lambda j, i: (0, j + nj))],
        out_specs=[pl.BlockSpec((2, tm, tn), lambda j, i: (0, i, j)),
                   pl.BlockSpec((2, tm, tn), lambda j, i: (0, i, j)),
                   pl.BlockSpec((tm, tn), lambda j, i: (i, j))],
        out_shape=[jax.ShapeDtypeStruct((2, s, dff), BF16), jax.ShapeDtypeStruct((2, s, dff), BF16),
                   jax.ShapeDtypeStruct((s, dff), BF16)],
        scratch_shapes=[pltpu.VMEM((2, 8, tn), F32)],
        compiler_params=_cp(("arbitrary", "arbitrary"), 56),
    )(x1b, wup, wup, fcw8, fcw8)


def _ffn_mid_bwd(dz2b, wdown, u3, y3, fcw8, s, d, dff):
    tm = _pick(s, 1024, 16)
    tn = _pick(dff, 512, LANES)
    nj, ni = dff // tn, s // tm

    def body(dz_ref, wd_ref, u_ref, y_ref, ca_ref, cg_ref, du_ref, dw_ref, nxt_ref):
        @pl.when(pl.program_id(1) == 0)
        def _():
            nxt_ref[...] = jnp.zeros_like(nxt_ref)
            dw_ref[...] = jnp.zeros_like(dw_ref)

        a2, g2 = y_ref[0].astype(F32), y_ref[1].astype(F32)
        sig = 1.0 / (1.0 + jnp.exp(-a2))
        silu = a2 * sig
        dhv = _dot(dz_ref[...], wd_ref[...], NT)
        dys = (dhv * g2 * (sig * (1.0 + a2 * (1.0 - sig))), dhv * silu)
        for part, (c_ref, dy) in enumerate(zip((ca_ref, cg_ref), dys)):
            dx, dws = _conv_bwd(dy, u_ref[part].astype(F32), c_ref[...], nxt_ref[part])
            du_ref[part] = dx.astype(BF16)
            for t in range(3):
                dw_ref[part, t:t + 1, :] += dws[t]
            nxt_ref[part] = dy[0:8]

    return pl.pallas_call(
        body, name="ffn_mid_bwd", grid=(nj, ni),
        in_specs=[pl.BlockSpec((tm, d), lambda j, g: (ni - 1 - g, 0)),
                  pl.BlockSpec((tn, d), lambda j, g: (j, 0)),
                  pl.BlockSpec((2, tm, tn), lambda j, g: (0, ni - 1 - g, j)),
                  pl.BlockSpec((2, tm, tn), lambda j, g: (0, ni - 1 - g, j)),
                  pl.BlockSpec((8, tn), lambda j, g: (0, j)),
                  pl.BlockSpec((8, tn), lambda j, g: (0, j + nj))],
        out_specs=[pl.BlockSpec((2, tm, tn), lambda j, g: (0, ni - 1 - g, j)),
                   pl.BlockSpec((2, 8, tn), lambda j, g: (0, 0, j))],
        out_shape=[jax.ShapeDtypeStruct((2, s, dff), BF16), jax.ShapeDtypeStruct((2, 8, dff), F32)],
        scratch_shapes=[pltpu.VMEM((2, 8, tn), F32)],
        compiler_params=_cp(("arbitrary", "arbitrary"), 56),
    )(dz2b, wdown, u3, y3, fcw8, fcw8)


def _ffn_down_loss(hmid, wdown, x1, target, g2, b2, s, d, dff):
    tm = _pick(s, 512, SLAB)
    tk = _pick(dff, 1408, LANES)
    ni, nk = s // tm, dff // tk
    slab = min(SLAB, tm)

    def body(h_ref, w_ref, x1_ref, t_ref, g_ref, b_ref, dzb_ref, st_ref, acc_ref):
        i, kk = pl.program_id(0), pl.program_id(1)

        @pl.when((i == 0) & (kk == 0))
        def _():
            st_ref[...] = jnp.zeros_like(st_ref)

        part = _dot(h_ref[...], w_ref[...], NN)

        @pl.when(kk == 0)
        def _():
            acc_ref[...] = part

        @pl.when(kk > 0)
        def _():
            acc_ref[...] += part

        @pl.when(kk == nk - 1)
        def _():
            g, b = g_ref[...], b_ref[...]

            def one(sl, carry):
                rows = pl.ds(pl.multiple_of(sl * slab, slab), slab)
                xh, rstd = _ln_fwd(ALPHA * x1_ref[rows, :] + acc_ref[rows, :])
                diff = xh * g + b - t_ref[rows, :]
                sq = jnp.sum(jnp.sum(diff * diff, axis=1, keepdims=True), axis=0, keepdims=True)
                dz, dg, db = _ln_bwd(diff * (1.0 / d), xh, rstd, g)
                dzb_ref[rows, :] = dz.astype(BF16)
                st_ref[0:1, :] += dg
                st_ref[1:2, :] += db
                st_ref[2:3, :] += sq
                return carry

            lax.fori_loop(0, tm // slab, one, 0)

    row = pl.BlockSpec((tm, d), lambda i, kk: (i, 0))
    vec = pl.BlockSpec((1, d), lambda i, kk: (0, 0))
    return pl.pallas_call(
        body, name="ffn_down_loss", grid=(ni, nk),
        in_specs=[pl.BlockSpec((tm, tk), lambda i, kk: (i, kk)), pl.BlockSpec((tk, d), lambda i, kk: (kk, 0)),
                  row, row, vec, vec],
        out_specs=[row, pl.BlockSpec((8, d), lambda i, kk: (0, 0))],
        out_shape=[jax.ShapeDtypeStruct((s, d), BF16), jax.ShapeDtypeStruct((8, d), F32)],
        scratch_shapes=[pltpu.VMEM((tm, d), F32)],
        compiler_params=_cp(("arbitrary", "arbitrary"), 48),
    )(hmid, wdown, x1, target, g2, b2)


def _ffn_dx_ln1_bwd(du3, wup, dz2b, xh1, rstd1, g1, s, d, dff):
    tm = _pick(s, 512, SLAB)
    tk = _pick(dff, 2816, MXU_DIM)
    nkh = dff // tk
    ni, nk = s // tm, 2 * nkh
    slab = min(SLAB, tm)

    def body(a_ref, w_ref, dz2_ref, xh_ref, rs_ref, g_ref, dzb_ref, st_ref, acc_ref):
        i, kk = pl.program_id(0), pl.program_id(1)

        @pl.when((i == 0) & (kk == 0))
        def _():
            st_ref[...] = jnp.zeros_like(st_ref)

        part = _dot(a_ref[...], w_ref[...], NT)

        @pl.when(kk == 0)
        def _():
            acc_ref[...] = part

        @pl.when(kk > 0)
        def _():
            acc_ref[...] += part

        @pl.when(kk == nk - 1)
        def _():
            g = g_ref[...]

            def one(sl, carry):
                rows = pl.ds(pl.multiple_of(sl * slab, slab), slab)
                dx1 = ALPHA * dz2_ref[rows, :].astype(F32) + acc_ref[rows, :]
                dz, dg, db = _ln_bwd(dx1, xh_ref[rows, :].astype(F32), rs_ref[rows, :], g)
                dzb_ref[rows, :] = dz.astype(BF16)
                st_ref[0:1, :] += dg
                st_ref[1:2, :] += db
                return carry

            lax.fori_loop(0, tm // slab, one, 0)

    row = pl.BlockSpec((tm, d), lambda i, kk: (i, 0))
    return pl.pallas_call(
        body, name="ffn_dx_ln1_bwd", grid=(ni, nk),
        in_specs=[pl.BlockSpec((None, tm, tk), lambda i, kk: (kk // nkh, i, kk % nkh)),
                  pl.BlockSpec((d, tk), lambda i, kk: (0, kk)),
                  row, row, pl.BlockSpec((tm, 1), lambda i, kk: (i, 0)), pl.BlockSpec((1, d), lambda i, kk: (0, 0))],
        out_specs=[row, pl.BlockSpec((8, d), lambda i, kk: (0, 0))],
        out_shape=[jax.ShapeDtypeStruct((s, d), BF16), jax.ShapeDtypeStruct((8, d), F32)],
        scratch_shapes=[pltpu.VMEM((tm, d), F32)],
        compiler_params=_cp(("arbitrary", "arbitrary"), 56),
    )(du3, wup, dz2b, xh1, rstd1, g1)


def _phase_mixer(x, win_t, wout, scw8, sinks, ln1_g, ln1_b, after=None):
    s, d = x.shape
    n_in = win_t.shape[0]
    cos, sin = _rope_tables(s)
    proj = _matmul(x, win_t, mode="nt", m=s, n=n_in, k=d, tm=_pick(s, 512, 16), tn=n_in, tk=d, out_dtype=BF16,
                   name="in_proj", vmem_mb=52, after=after,
                   b_spec=pl.BlockSpec((n_in, d), lambda j, i, kk: (0, 0), pipeline_mode=pl.Buffered(1)))
    attn = _attn_fwd(proj, sinks, cos, sin, s)
    conv = _convmix_fwd(proj, scw8, s, d)
    wout = wout(attn) if callable(wout) else wout
    x1, x1b, xh1, rstd1 = _outproj_ln1(attn, conv, wout, x, ln1_g, ln1_b, s, d)
    return dict(x=x, cos=cos, sin=sin, proj=proj, attn=attn, conv=conv, x1=x1, x1b=x1b, xh1=xh1, rstd1=rstd1,
                wout=wout)


def _phase_ffn(a, target, wup, wdown, fcw8, ln2_g, ln2_b):
    x1, x1b = a["x1"], a["x1b"]
    s, d = x1.shape
    dff = wdown.shape[0]
    u3, y3, hmid = _ffn_up(x1b, wup, fcw8, s, d, dff)
    dz2b, st2 = _ffn_down_loss(hmid, wdown, x1, target, ln2_g, ln2_b, s, d, dff)

    tnw = _pick(d, 1024, MXU_DIM)
    keep_b = pl.BlockSpec((s, tnw), lambda j, i, kk: (kk, j), pipeline_mode=pl.Buffered(1))
    g_wdown = _matmul(hmid, dz2b, mode="tn", m=dff, n=d, k=s, tm=_pick(dff, 512, MXU_DIM), tn=tnw, tk=s,
                      out_dtype=BF16, name="grad_w_down", vmem_mb=56, b_spec=keep_b)
    du3, dfcw = _ffn_mid_bwd(dz2b, wdown, u3, y3, fcw8, s, d, dff)
    tnu = _pick(dff, 512, MXU_DIM)
    njh = dff // tnu
    tmu = _pick(d, 1024, LANES)
    g_wup = _matmul(x1b, du3, mode="tn", m=d, n=2 * dff, k=s, tm=tmu, tn=tnu, tk=s, out_dtype=BF16,
                    name="grad_w_up", vmem_mb=58, m_outer=True,
                    a_spec=pl.BlockSpec((s, tmu), lambda j, i, kk: (kk, i), pipeline_mode=pl.Buffered(1)),
                    b_spec=pl.BlockSpec((None, s, tnu), lambda j, i, kk: (j // njh, kk, j % njh)))
    return dict(du3=du3, dz2b=dz2b, st2=st2, dfcw=dfcw, wdown=g_wdown, wup=g_wup)


def _phase_rest(a, f, wup, wout, win_t, scw8, sinks, ln1_g, between=None):
    xb, cos, sin, proj, attn, conv = a["x"], a["cos"], a["sin"], a["proj"], a["attn"], a["conv"]
    du3, dz2b, st2, dfcw = f["du3"], f["dz2b"], f["st2"], f["dfcw"]
    s, d = a["x1"].shape
    dff = wup.shape[1] // 2
    n_in = win_t.shape[0]
    ts = _pick(s, 2048, 16)
    dz1b, st1 = _ffn_dx_ln1_bwd(du3, wup, dz2b, a["xh1"], a["rstd1"], ln1_g, s, d, dff)
    after = between(dz1b) if between is not None else None

    tnw = _pick(d, 1024, MXU_DIM)
    halves = [_matmul(part, dz1b, mode="tn", m=part.shape[1], n=d, k=s, tm=_pick(part.shape[1], 512, LANES), tn=tnw,
                      tk=s, out_dtype=BF16, name="grad_w_out_" + tag, vmem_mb=56, after=after,
                      b_spec=pl.BlockSpec((s, tnw), lambda j, i, kk: (kk, j), pipeline_mode=pl.Buffered(1)))
              for tag, part in (("attn", attn), ("conv", conv))]
    g_wout = jnp.concatenate(halves, axis=0)
    dmix = _matmul(dz1b, wout, mode="nt", m=s, n=d, k=d, tm=_pick(s, 1024, 16), tn=_pick(d, 1024, LANES), tk=d,
                   out_dtype=BF16, name="out_dmix", vmem_mb=48, after=after)
    d3, dscw = _convmix_bwd(proj, dmix, scw8, s, d)
    dq, dk, dv, dsink = _attn_bwd(proj, dmix, sinks, cos, sin, s)
    dproj = jnp.concatenate([dq, dk, dv, d3[0], d3[1], d3[2]], axis=1)
    g_win_t = _matmul(dproj, xb, mode="tn", m=n_in, n=d, k=s, tm=_pick(n_in, 2176, LANES), tn=_pick(d, 512, LANES),
                      tk=ts, out_dtype=BF16, name="grad_w_in", vmem_mb=48)
    small = dict(loss_sq=st2[2, 0], ln2_g=st2[0], ln2_b=st2[1], ln1_g=st1[0], ln1_b=st1[1], sinks=dsink[0, :N_Q_HEADS],
                 fcw=jnp.concatenate([dfcw[0, :3], dfcw[1, :3]], axis=1), scw=dscw[:3])
    return (dproj, dz1b), dict(win_t=g_win_t, wout=g_wout), small


def _grad_x(dproj, dz1b, win_t, after=None):
    s, n_in = dproj.shape
    d = win_t.shape[1]
    return _matmul(dproj, win_t, mode="nn", m=s, n=d, k=n_in, tm=_pick(s, 512, 16), tn=_pick(d, 1024, LANES),
                   tk=n_in, out_dtype=F32, name="grad_x", vmem_mb=56, res=dz1b, alpha=ALPHA, after=after)


def _local_step(x, target, win_t, wout, wup, wdown, scw8, fcw8, sinks, ln1_g, ln1_b, ln2_g, ln2_b):
    a = _phase_mixer(x, win_t, wout, scw8, sinks, ln1_g, ln1_b)
    f = _phase_ffn(a, target, wup, wdown, fcw8, ln2_g, ln2_b)
    (dproj, dz1b), g, small = _phase_rest(a, f, wup, wout, win_t, scw8, sinks, ln1_g)
    return _grad_x(dproj, dz1b, win_t), dict(g, wup=f["wup"], wdown=f["wdown"]), small


W_IN, W_OUT = ("win_t",), ("wout",)
MIXER = W_IN + W_OUT
FFN = ("wup", "wdown")
BIG = MIXER + FFN


def _geom(shard_shapes):
    out = {}
    for name in BIG:
        r, c = shard_shapes[name]
        out[name] = ("col" if name == "wup" else "row", (r, c), (r // 2, c))
    return out


def _full_shape(kind, shard):
    r, c = shard
    return (N_CHIPS * r, c) if kind == "row" else (r, N_CHIPS * c)


def _piece_of(ref, kind, shard, chip, half):
    r, c = shard
    if kind == "row":
        return ref.at[pl.ds(chip * r + half * (r // 2), r // 2), :]
    return ref.at[pl.ds(half * (r // 2), r // 2), pl.ds(chip * c, c)]


def _shard_piece(ref, shard, half):
    r, _ = shard
    return ref.at[pl.ds(half * (r // 2), r // 2), :]


def _me():
    return lax.axis_index("x"), lax.axis_index("y"), lax.axis_index("c")


def _other_chips(x, y):
    return [(1 - x, y), (x, 1 - y), (1 - x, 1 - y)]


def _remote(src, dst, send_sem, recv_sem, dev):
    return pltpu.make_async_remote_copy(src_ref=src, dst_ref=dst, send_sem=send_sem, recv_sem=recv_sem,
                                        device_id=dev, device_id_type=MESH)


def _place_shard(w, chip1, kind, name):
    r, c = w.shape
    tr = _rows_tile(r, c, 16, ADD_TILE)
    nt = r // tr

    def body(chip_ref, w_ref, o_ref):
        o_ref[...] = w_ref[...].astype(BF16)

    out_map = (lambda i, chip_ref: (chip_ref[0] * nt + i, 0)) if kind == "row" else (lambda i, chip_ref: (i, chip_ref[0]))
    return pl.pallas_call(
        body, name="place_" + name,
        grid_spec=pltpu.PrefetchScalarGridSpec(
            num_scalar_prefetch=1, grid=(nt,),
            in_specs=[pl.BlockSpec((tr, c), lambda i, chip_ref: (i, 0))],
            out_specs=pl.BlockSpec((tr, c), out_map)),
        out_shape=jax.ShapeDtypeStruct(_full_shape(kind, (r, c)), BF16),
        compiler_params=_cp(("arbitrary",), 32),
    )(chip1, w)


def _allgather_weights(names, placed, geom, small_shards):
    nb, ns = len(names), len(small_shards)
    small_w = [a.shape[1] for a in small_shards]

    def body(*refs):
        sm = refs[nb:nb + ns]
        full = refs[nb + ns:2 * nb + ns]
        smf = refs[2 * nb + ns:2 * nb + 2 * ns]
        send, recv, loc = refs[2 * nb + 2 * ns:]
        x, y, c = _me()
        chip = 2 * x + y
        sib = (x, y, 1 - c)
        others = _other_chips(x, y)
        locals_, sends = [], []
        for m, name in enumerate(names):
            kind, shard, _ = geom[name]
            mine = _piece_of(full[m], kind, shard, chip, c)
            for k, (qx, qy) in enumerate(others):
                cp = _remote(mine, mine, send.at[6 * m + k], recv.at[6 * m + k], (qx, qy, c))
                cp.start()
                sends.append(cp)
        for t in range(ns):
            cp = pltpu.make_async_copy(sm[t], smf[t].at[:, pl.ds(chip * small_w[t], small_w[t])], loc.at[t])
            cp.start()
            locals_.append(cp)
            for k, (qx, qy) in enumerate(others):
                cp = _remote(sm[t], smf[t].at[:, pl.ds(chip * small_w[t], small_w[t])],
                             send.at[6 * nb + 3 * t + k], recv.at[6 * nb + 3 * t + k], (qx, qy, c))
                cp.start()
                sends.append(cp)
        for m, name in enumerate(names):
            kind, shard, _ = geom[name]
            for k, (qx, qy) in enumerate(others):
                got = _piece_of(full[m], kind, shard, 2 * qx + qy, c)
                _remote(got, got, send.at[6 * m + k], recv.at[6 * m + k], (qx, qy, c)).wait_recv()
                cp = _remote(got, got, send.at[6 * m + 3 + k], recv.at[6 * m + 3 + k], sib)
                cp.start()
                sends.append(cp)
        for t in range(ns):
            for k, (qx, qy) in enumerate(others):
                got = smf[t].at[:, pl.ds((2 * qx + qy) * small_w[t], small_w[t])]
                _remote(got, got, send.at[6 * nb + 3 * t + k], recv.at[6 * nb + 3 * t + k], (qx, qy, c)).wait_recv()
        for m, name in enumerate(names):
            kind, shard, _ = geom[name]
            for k, (qx, qy) in enumerate(others):
                got = _piece_of(full[m], kind, shard, 2 * qx + qy, 1 - c)
                _remote(got, got, send.at[6 * m + 3 + k], recv.at[6 * m + 3 + k], sib).wait_recv()
        for cp in sends:
            cp.wait_send()
        for cp in locals_:
            cp.wait()

    nsem = 6 * nb + 3 * ns
    out_shape = [jax.ShapeDtypeStruct(placed[n].shape, BF16) for n in names]
    out_shape += [jax.ShapeDtypeStruct((8, N_CHIPS * w), F32) for w in small_w]
    outs = pl.pallas_call(
        body, name="allgather_weights", in_specs=[ANY] * (nb + ns), out_specs=[ANY] * (nb + ns), out_shape=out_shape,
        input_output_aliases={m: m for m in range(nb)},
        scratch_shapes=[pltpu.SemaphoreType.DMA((nsem,)), pltpu.SemaphoreType.DMA((nsem,)),
                        pltpu.SemaphoreType.DMA((ns,))],
    )(*[placed[n] for n in names], *small_shards)
    return dict(zip(names, outs[:nb])), list(outs[nb:])


def _sibling_exchange(names, grads, geom):
    nb = len(names)

    def body(*refs):
        g = refs[:nb]
        got = refs[nb:2 * nb]
        send, recv = refs[2 * nb:]
        x, y, c = _me()
        sib = (x, y, 1 - c)
        cps = []
        for m, name in enumerate(names):
            kind, shard, _ = geom[name]
            for r in range(N_CHIPS):
                cp = _remote(_piece_of(g[m], kind, shard, r, 1 - c), got[m].at[r],
                             send.at[N_CHIPS * m + r], recv.at[N_CHIPS * m + r], sib)
                cp.start()
                cps.append(cp)
        for cp in cps:
            cp.wait_recv()
        for cp in cps:
            cp.wait_send()

    return pl.pallas_call(
        body, name="grad_sibling_exchange_" + names[0], in_specs=[ANY] * nb, out_specs=[ANY] * nb,
        out_shape=[jax.ShapeDtypeStruct((N_CHIPS,) + geom[n][2], BF16) for n in names],
        scratch_shapes=[pltpu.SemaphoreType.DMA((N_CHIPS * nb,)), pltpu.SemaphoreType.DMA((N_CHIPS * nb,))],
    )(*[grads[n] for n in names])


def _sibling_assemble(names, shards, geom):
    nb = len(names)

    def body(*refs):
        full = refs[nb:2 * nb]
        send, recv = refs[2 * nb:]
        x, y, c = _me()
        sib = (x, y, 1 - c)
        cps = []
        for m, name in enumerate(names):
            mine = _shard_piece(full[m], geom[name][1], c)
            cp = _remote(mine, mine, send.at[m], recv.at[m], sib)
            cp.start()
            cps.append(cp)
        for m, name in enumerate(names):
            theirs = _shard_piece(full[m], geom[name][1], 1 - c)
            _remote(theirs, theirs, send.at[m], recv.at[m], sib).wait_recv()
        for cp in cps:
            cp.wait_send()

    return pl.pallas_call(
        body, name="grad_sibling_assemble_" + names[0], in_specs=[ANY] * nb, out_specs=[ANY] * nb,
        out_shape=[jax.ShapeDtypeStruct(geom[n][1], F32) for n in names],
        input_output_aliases={m: m for m in range(nb)},
        scratch_shapes=[pltpu.SemaphoreType.DMA((nb,)), pltpu.SemaphoreType.DMA((nb,))],
    )(*shards)


HBM = pl.BlockSpec(memory_space=pltpu.HBM)
SEM = pl.BlockSpec(memory_space=pltpu.SEMAPHORE)
EFFECT = pltpu.SideEffectType.DATAFLOW_SIDE_EFFECTING
TOKEN = jax.ShapeDtypeStruct((8, LANES), F32)


def _hbm(a):
    return pltpu.with_memory_space_constraint(a, pltpu.HBM)


def _gather_copies(names, full, geom, send, recv):
    x, y, c = _me()
    out = []
    for m, name in enumerate(names):
        kind, shard, _ = geom[name]
        mine = _piece_of(full[m], kind, shard, 2 * x + y, c)
        for k, (qx, qy) in enumerate(_other_chips(x, y)):
            theirs = _piece_of(full[m], kind, shard, 2 * qx + qy, c)
            out.append((_remote(mine, mine, send.at[3 * m + k], recv.at[3 * m + k], (qx, qy, c)),
                        _remote(theirs, theirs, send.at[3 * m + k], recv.at[3 * m + k], (qx, qy, c))))
    return out


def _gather_start(names, placed, geom, after):
    nb = len(names)

    def body(*refs):
        full = refs[:nb]
        send, recv = refs[nb + 1], refs[nb + 2]
        token = refs[2 * nb + 3]
        for cp, _ in _gather_copies(names, full, geom, send, recv):
            cp.start()
        token[...] = jnp.zeros_like(token)

    outs = pl.pallas_call(
        body, name="gather_start_" + names[0],
        out_shape=(pltpu.SemaphoreType.DMA((3 * nb,)), pltpu.SemaphoreType.DMA((3 * nb,)),
                   *[pltpu.HBM(placed[n].shape, BF16) for n in names], TOKEN),
        in_specs=[HBM] * nb + [ANY], out_specs=(SEM, SEM, *[HBM] * nb, pl.BlockSpec(memory_space=pltpu.VMEM)),
        input_output_aliases={m: 2 + m for m in range(nb)},
        compiler_params=pltpu.CompilerParams(has_side_effects=EFFECT),
    )(*[_hbm(placed[n]) for n in names], after)
    return outs[0], outs[1], list(outs[2:2 + nb]), outs[2 + nb]


def _gather_wait(names, send, recv, thru, geom, after):
    nb = len(names)

    def body(*refs):
        full = refs[:nb]
        for mine, theirs in _gather_copies(names, full, geom, refs[nb], refs[nb + 1]):
            mine.wait_send()
            theirs.wait_recv()

    return pl.pallas_call(
        body, name="gather_wait_" + names[0], out_shape=tuple(pltpu.HBM(t.shape, t.dtype) for t in thru),
        in_specs=[HBM] * nb + [SEM, SEM, ANY], out_specs=tuple([HBM] * nb),
        input_output_aliases={m: m for m in range(nb)},
        compiler_params=pltpu.CompilerParams(has_side_effects=EFFECT),
    )(*thru, send, recv, after)


def _gather_forward(names, full, geom):
    nb = len(names)

    def body(*refs):
        arr = refs[nb:2 * nb]
        send, recv = refs[2 * nb:]
        x, y, c = _me()
        sib = (x, y, 1 - c)
        cps = []
        for m, name in enumerate(names):
            kind, shard, _ = geom[name]
            for k, (qx, qy) in enumerate(_other_chips(x, y)):
                got = _piece_of(arr[m], kind, shard, 2 * qx + qy, c)
                cp = _remote(got, got, send.at[3 * m + k], recv.at[3 * m + k], sib)
                cp.start()
                cps.append(cp)
        for m, name in enumerate(names):
            kind, shard, _ = geom[name]
            for k, (qx, qy) in enumerate(_other_chips(x, y)):
                theirs = _piece_of(arr[m], kind, shard, 2 * qx + qy, 1 - c)
                _remote(theirs, theirs, send.at[3 * m + k], recv.at[3 * m + k], sib).wait_recv()
        for cp in cps:
            cp.wait_send()

    return pl.pallas_call(
        body, name="gather_forward_" + names[0], in_specs=[ANY] * nb, out_specs=[ANY] * nb,
        out_shape=[jax.ShapeDtypeStruct(a.shape, a.dtype) for a in full],
        input_output_aliases={m: m for m in range(nb)},
        scratch_shapes=[pltpu.SemaphoreType.DMA((3 * nb,)), pltpu.SemaphoreType.DMA((3 * nb,))],
    )(*full)


def _scatter_copies(nb, t, got, send, recv):
    x, y, c = _me()
    return [_remote(t[m].at[2 * qx + qy], got[m].at[k], send.at[3 * m + k], recv.at[3 * m + k], (qx, qy, c))
            for m in range(nb) for k, (qx, qy) in enumerate(_other_chips(x, y))]


def _chip_exchange_start(names, chip_sums, geom, after):
    nb = len(names)
    lands = [lax.empty((N_CHIPS - 1,) + geom[n][2], BF16) for n in names]

    def body(*refs):
        t, got = refs[:nb], refs[nb:2 * nb]
        send, recv = refs[2 * nb + 1], refs[2 * nb + 2]
        token = refs[4 * nb + 3]
        for cp in _scatter_copies(nb, t, got, send, recv):
            cp.start()
        token[...] = jnp.zeros_like(token)

    both = list(chip_sums) + lands
    outs = pl.pallas_call(
        body, name="grad_chip_start_" + names[0],
        out_shape=(pltpu.SemaphoreType.DMA((3 * nb,)), pltpu.SemaphoreType.DMA((3 * nb,)),
                   *[pltpu.HBM(a.shape, a.dtype) for a in both], TOKEN),
        in_specs=[HBM] * (2 * nb) + [ANY],
        out_specs=(SEM, SEM, *[HBM] * (2 * nb), pl.BlockSpec(memory_space=pltpu.VMEM)),
        input_output_aliases={m: 2 + m for m in range(2 * nb)},
        compiler_params=pltpu.CompilerParams(has_side_effects=EFFECT),
    )(*[_hbm(a) for a in both], after)
    return outs[0], outs[1], list(outs[2:2 + 2 * nb]), outs[2 + 2 * nb]


def _chip_exchange_wait(names, send, recv, thru, after):
    nb = len(names)

    def body(*refs):
        for cp in _scatter_copies(nb, refs[:nb], refs[nb:2 * nb], refs[2 * nb], refs[2 * nb + 1]):
            cp.wait_send()
            cp.wait_recv()

    outs = pl.pallas_call(
        body, name="grad_chip_wait_" + names[0], out_shape=tuple(pltpu.HBM(t.shape, t.dtype) for t in thru),
        in_specs=[HBM] * (2 * nb) + [SEM, SEM, ANY], out_specs=tuple([HBM] * (2 * nb)),
        input_output_aliases={m: m for m in range(2 * nb)},
        compiler_params=pltpu.CompilerParams(has_side_effects=EFFECT),
    )(*thru, send, recv, after)
    return list(outs[:nb]), list(outs[nb:])


def _sibling_copies(names, g, got, geom, send, recv):
    x, y, c = _me()
    out = []
    for m, name in enumerate(names):
        kind, shard, _ = geom[name]
        for r in range(N_CHIPS):
            out.append(_remote(_piece_of(g[m], kind, shard, r, 1 - c), got[m].at[r],
                               send.at[N_CHIPS * m + r], recv.at[N_CHIPS * m + r], (x, y, 1 - c)))
    return out


def _sibling_exchange_start(names, grads, geom, after):
    nb = len(names)
    lands = [lax.empty((N_CHIPS,) + geom[n][2], BF16) for n in names]

    def body(*refs):
        for cp in _sibling_copies(names, refs[:nb], refs[nb:2 * nb], geom, refs[2 * nb + 1], refs[2 * nb + 2]):
            cp.start()
        token = refs[4 * nb + 3]
        token[...] = jnp.zeros_like(token)

    both = [grads[n] for n in names] + lands
    outs = pl.pallas_call(
        body, name="grad_sibling_start_" + names[0],
        out_shape=(pltpu.SemaphoreType.DMA((N_CHIPS * nb,)), pltpu.SemaphoreType.DMA((N_CHIPS * nb,)),
                   *[pltpu.HBM(a.shape, a.dtype) for a in both], TOKEN),
        in_specs=[HBM] * (2 * nb) + [ANY],
        out_specs=(SEM, SEM, *[HBM] * (2 * nb), pl.BlockSpec(memory_space=pltpu.VMEM)),
        input_output_aliases={m: 2 + m for m in range(2 * nb)},
        compiler_params=pltpu.CompilerParams(has_side_effects=EFFECT),
    )(*[_hbm(a) for a in both], after)
    return outs[0], outs[1], list(outs[2:2 + 2 * nb]), outs[2 + 2 * nb]


def _sibling_exchange_wait(names, send, recv, thru, geom, after):
    nb = len(names)

    def body(*refs):
        for cp in _sibling_copies(names, refs[:nb], refs[nb:2 * nb], geom, refs[2 * nb], refs[2 * nb + 1]):
            cp.wait_send()
            cp.wait_recv()

    outs = pl.pallas_call(
        body, name="grad_sibling_wait_" + names[0], out_shape=tuple(pltpu.HBM(t.shape, t.dtype) for t in thru),
        in_specs=[HBM] * (2 * nb) + [SEM, SEM, ANY], out_specs=tuple([HBM] * (2 * nb)),
        input_output_aliases={m: m for m in range(2 * nb)},
        compiler_params=pltpu.CompilerParams(has_side_effects=EFFECT),
    )(*thru, send, recv, after)
    return list(outs[:nb]), list(outs[nb:])


def _allreduce_small(part):
    rows = part.shape[0]
    flips = [(a, b, e) for a in (0, 1) for b in (0, 1) for e in (0, 1) if (a, b, e) != (0, 0, 0)]

    def body(p_ref, o_ref, all_ref, send, recv):
        x, y, c = _me()
        me = 4 * x + 2 * y + c
        all_ref[me] = p_ref[...]
        cps = []
        for k, (a, b, e) in enumerate(flips):
            cp = _remote(p_ref, all_ref.at[me], send.at[k], recv.at[k], (x ^ a, y ^ b, c ^ e))
            cp.start()
            cps.append(cp)
        for k, (a, b, e) in enumerate(flips):
            peer = 4 * (x ^ a) + 2 * (y ^ b) + (c ^ e)
            _remote(p_ref, all_ref.at[peer], send.at[k], recv.at[k], (x ^ a, y ^ b, c ^ e)).wait_recv()
        for cp in cps:
            cp.wait_send()
        tot = all_ref[0]
        for dev in range(1, 8):
            tot = tot + all_ref[dev]
        o_ref[...] = tot

    vm = pl.BlockSpec(memory_space=pltpu.VMEM)
    return pl.pallas_call(
        body, name="allreduce_small", in_specs=[vm], out_specs=vm, out_shape=jax.ShapeDtypeStruct((rows, LANES), F32),
        scratch_shapes=[pltpu.VMEM((8, rows, LANES), F32), pltpu.SemaphoreType.DMA((7,)), pltpu.SemaphoreType.DMA((7,))],
    )(part)


def _rows_tile(rows, cols, mult, elems=1 << 19):
    return _pick(rows, max(mult, elems // cols // mult * mult), mult)


ADD_TILE = 1 << 20


def _add_pairs(g, got, kind, shard, where, name):
    p, r, c = got.shape
    tr = _rows_tile(r, c, 16, ADD_TILE)
    nt = r // tr

    def body(w_ref, a_ref, b_ref, o_ref):
        o_ref[...] = (a_ref[...].astype(F32) + b_ref[...].astype(F32)).astype(BF16)

    if kind == "row":
        g_map = lambda q, i, w_ref: ((2 * q + w_ref[1]) * nt + i, 0)
    else:
        g_map = lambda q, i, w_ref: (w_ref[1] * nt + i, q)
    spec = pl.BlockSpec((None, tr, c), lambda q, i, w_ref: (q, i, 0))
    return pl.pallas_call(
        body, name="grad_add_sibling_" + name,
        grid_spec=pltpu.PrefetchScalarGridSpec(
            num_scalar_prefetch=1, grid=(p, nt), in_specs=[pl.BlockSpec((tr, c), g_map), spec], out_specs=spec),
        out_shape=jax.ShapeDtypeStruct((p, r, c), BF16), compiler_params=_cp(("arbitrary", "arbitrary"), 32),
    )(where, g, got)


def _add_four(t, got, shard, where, name):
    _, r, c = t.shape
    tr = _rows_tile(r, c, 16, ADD_TILE)
    nt = r // tr

    def body(w_ref, own, t0, t1, t2, o_ref):
        o_ref[...] = ((own[...].astype(F32) + t0[...].astype(F32)) + t1[...].astype(F32)) + t2[...].astype(F32)

    spec = lambda q: pl.BlockSpec((None, tr, c), lambda i, w_ref: (q, i, 0))
    return pl.pallas_call(
        body, name="grad_add_chips_" + name,
        grid_spec=pltpu.PrefetchScalarGridSpec(
            num_scalar_prefetch=1, grid=(nt,),
            in_specs=[pl.BlockSpec((None, tr, c), lambda i, w_ref: (w_ref[0], i, 0)), spec(0), spec(1), spec(2)],
            out_specs=pl.BlockSpec((tr, c), lambda i, w_ref: (w_ref[1] * nt + i, 0))),
        out_shape=jax.ShapeDtypeStruct(shard, F32), compiler_params=_cp(("arbitrary",), 48),
    )(where, t, got, got, got)


def _adamw(w, g, m, v, name):
    r, c = w.shape
    tr = _rows_tile(r, c, 8)

    def body(w_ref, g_ref, m_ref, v_ref, go_ref, d_ref, mo_ref, vo_ref):
        gv = g_ref[...]
        mn = ADAM_B1 * m_ref[...] + (1.0 - ADAM_B1) * gv
        vn = ADAM_B2 * v_ref[...] + (1.0 - ADAM_B2) * (gv * gv)
        m_hat = mn / (1.0 - ADAM_B1 ** ADAM_STEP)
        v_hat = vn / (1.0 - ADAM_B2 ** ADAM_STEP)
        go_ref[...] = gv
        d_ref[...] = -ADAM_LR * (m_hat / (jnp.sqrt(v_hat) + ADAM_EPS) + ADAM_WD * w_ref[...])
        mo_ref[...] = mn
        vo_ref[...] = vn

    spec = pl.BlockSpec((tr, c), lambda i: (i, 0))
    return pl.pallas_call(
        body, name=name, grid=(r // tr,), in_specs=[spec] * 4, out_specs=[spec] * 4,
        out_shape=[jax.ShapeDtypeStruct((r, c), F32)] * 4, compiler_params=_cp(("arbitrary",), 32),
    )(w, g, m, v)


def _pack(vectors, rows):
    flat = jnp.concatenate([v.reshape(-1).astype(F32) for v in vectors])
    return jnp.pad(flat, (0, rows * LANES - flat.shape[0])).reshape(rows, LANES)


def _unpack(packed, shapes):
    flat = packed.reshape(-1)
    out, off = [], 0
    for shp in shapes:
        n = 1
        for t in shp:
            n *= t
        out.append(flat[off:off + n].reshape(shp))
        off += n
    return out


def _rows_for(shapes):
    n = sum(functools.reduce(lambda a, b: a * b, shp, 1) for shp in shapes)
    return -(-n // (8 * LANES)) * 8


def kernel(x, w_in, attn_sinks, short_conv_w, w_out, ln1_g, ln1_b, ffn_w_up, ffn_conv_w, ffn_w_down, ln2_g, ln2_b, loss_target, m_w_in, m_attn_sinks, m_short_conv_w, m_w_out, m_ln1_g, m_ln1_b, m_ffn_w_up, m_ffn_conv_w, m_ffn_w_down, m_ln2_g, m_ln2_b, v_w_in, v_attn_sinks, v_short_conv_w, v_w_out, v_ln1_g, v_ln1_b, v_ffn_w_up, v_ffn_conv_w, v_ffn_w_down, v_ln2_g, v_ln2_b):
    xs, tgt = x[0], loss_target[0]
    d = xs.shape[1]
    chip = 2 * lax.axis_index("x") + lax.axis_index("y")

    w_big = dict(win_t=w_in[0], wout=w_out[0], wup=ffn_w_up[0], wdown=ffn_w_down[0])
    m_big = dict(win_t=m_w_in[0], wout=m_w_out[0], wup=m_ffn_w_up[0], wdown=m_ffn_w_down[0])
    v_big = dict(win_t=v_w_in[0], wout=v_w_out[0], wup=v_ffn_w_up[0], wdown=v_ffn_w_down[0])
    to_place = dict(w_big, win_t=w_in[0].T)
    geom = _geom({n: to_place[n].shape for n in BIG})
    pad8 = lambda a: jnp.pad(a[0], ((0, 5), (0, 0)))
    where = jnp.stack([chip, lax.axis_index("c")]).astype(jnp.int32)
    placed = {n: _place_shard(to_place[n], where[:1], geom[n][0], n) for n in BIG}
    full, (scw8, fcw8) = _allgather_weights(W_IN, placed, geom, [pad8(short_conv_w), pad8(ffn_conv_w)])
    o_send, o_recv, o_thru, o_token = _gather_start(W_OUT, placed, geom, scw8)
    send, recv, thru, token = _gather_start(FFN, placed, geom, o_token)

    def wout_behind(attn):
        return _gather_forward(W_OUT, _gather_wait(W_OUT, o_send, o_recv, o_thru, geom, attn), geom)[0]

    a = _phase_mixer(xs, full["win_t"], wout_behind, scw8, attn_sinks, ln1_g, ln1_b, after=token)
    full["wout"] = a["wout"]
    landed = _gather_forward(FFN, _gather_wait(FFN, send, recv, thru, geom, a["x1b"]), geom)
    full.update(zip(FFN, landed))
    f = _phase_ffn(a, tgt, full["wup"], full["wdown"], fcw8, ln2_g, ln2_b)

    def add_pairs(names, grads, from_sibling):
        return [_add_pairs(grads[m], from_sibling[m], geom[n][0], geom[n][1], where, n) for m, n in enumerate(names)]

    sib_send, sib_recv, sib_thru, sib_token = _sibling_exchange_start(FFN, f, geom, f["st2"])
    started = {}

    def between(dz1b):
        grads, from_sibling = _sibling_exchange_wait(FFN, sib_send, sib_recv, sib_thru, geom, dz1b)
        started["sums"] = add_pairs(FFN, grads, from_sibling)
        started["chip"] = _chip_exchange_start(FFN, started["sums"], geom, f["st2"])
        return started["chip"][3]

    (dproj, dz1b), g_mixer, g_small = _phase_rest(a, f, full["wup"], full["wout"], full["win_t"], scw8, attn_sinks,
                                                  ln1_g + sib_token[0:1, 0:1], between=between)
    send, recv, thru, _ = started["chip"]

    def finish(names, exchanged):
        sums, from_chips = exchanged
        halves = [_add_four(sums[m], from_chips[m], geom[n][1], where, n) for m, n in enumerate(names)]
        shards = _sibling_assemble(names, halves, geom)
        grads = {n: shards[m].T if n == "win_t" else shards[m] for m, n in enumerate(names)}
        return {n: _adamw(w_big[n], grads[n], m_big[n], v_big[n], "adamw_" + n) for n in names}

    mixer_sums = add_pairs(MIXER, [g_mixer[n] for n in MIXER], _sibling_exchange(MIXER, g_mixer, geom))
    send2, recv2, thru2, token2 = _chip_exchange_start(MIXER, mixer_sums, geom, f["st2"])
    grad_x = _grad_x(dproj, dz1b, full["win_t"], after=token2)
    upd = finish(FFN, _chip_exchange_wait(FFN, send, recv, thru, grad_x))
    upd.update(finish(MIXER, _chip_exchange_wait(MIXER, send2, recv2, thru2, upd[FFN[0]][1])))

    small_names = ("ln1_g", "ln1_b", "ln2_g", "ln2_b", "sinks", "fcw", "scw")
    small_shapes = [g_small[n].shape for n in small_names]
    red = _allreduce_small(_pack([g_small["loss_sq"].reshape(1)] + [g_small[n] for n in small_names],
                                 _rows_for([(1,)] + small_shapes)))
    loss_sq, *gs = _unpack(red, [(1,)] + small_shapes)
    gs = dict(zip(small_names, gs))
    loss = (0.5 / d) * loss_sq[0]
    fw, sw = ffn_conv_w.shape[2], short_conv_w.shape[2]
    gs["fcw"] = lax.dynamic_slice_in_dim(gs["fcw"], chip * fw, fw, axis=1)
    gs["scw"] = lax.dynamic_slice_in_dim(gs["scw"], chip * sw, sw, axis=1)

    sm_w = dict(ln1_g=ln1_g[0], ln1_b=ln1_b[0], ln2_g=ln2_g[0], ln2_b=ln2_b[0], sinks=attn_sinks[0],
                fcw=ffn_conv_w[0], scw=short_conv_w[0])
    sm_m = dict(ln1_g=m_ln1_g[0], ln1_b=m_ln1_b[0], ln2_g=m_ln2_g[0], ln2_b=m_ln2_b[0], sinks=m_attn_sinks[0],
                fcw=m_ffn_conv_w[0], scw=m_short_conv_w[0])
    sm_v = dict(ln1_g=v_ln1_g[0], ln1_b=v_ln1_b[0], ln2_g=v_ln2_g[0], ln2_b=v_ln2_b[0], sinks=v_attn_sinks[0],
                fcw=v_ffn_conv_w[0], scw=v_short_conv_w[0])
    shapes = [sm_w[n].shape for n in small_names]
    rows = _rows_for(shapes)
    packed = [_pack([t[n] for n in small_names], rows) for t in (sm_w, gs, sm_m, sm_v)]
    sm_out = [dict(zip(small_names, _unpack(a, shapes))) for a in _adamw(*packed, "adamw_small")]

    def leaf(kind, name):
        if name in ("w_in", "w_out", "ffn_w_up", "ffn_w_down"):
            key = dict(w_in="win_t", w_out="wout", ffn_w_up="wup", ffn_w_down="wdown")[name]
            return upd[key][kind][None]
        key = dict(attn_sinks="sinks", short_conv_w="scw", ffn_conv_w="fcw").get(name, name)
        return sm_out[kind][key][None]

    order = ("w_in", "attn_sinks", "short_conv_w", "w_out", "ln1_g", "ln1_b", "ffn_w_up", "ffn_conv_w", "ffn_w_down",
             "ln2_g", "ln2_b")
    outs = [loss, grad_x[None]]
    for kind in range(4):
        outs += [leaf(kind, n) for n in order]
    return tuple(outs)
```
